```python
import jax, jax.numpy as jnp
from jax import lax
import numpy as np

D_MODEL = 2048
BATCH = 8
SEQ = 8192
DEPTH = 2

HEAD_DIM = 128
N_SB_HEADS = 12
N_MEM_HEADS = 4
N_MLA_HEADS = 12
MEM_LEN = 256
Q_LORA_RANK = 512
KV_LORA_RANK = 512
QK_NOPE_DIM = 128
QK_ROPE_DIM = 64
V_HEAD_DIM = 128
ROPE_THETA = 10000.0
BLOCK_Q = 128
EPS = 1e-6
N_A_LAYERS = DEPTH // 2
N_B_LAYERS = DEPTH - N_A_LAYERS
SB_W = N_SB_HEADS * HEAD_DIM
MEM_W = N_MEM_HEADS * HEAD_DIM
MLA_W = N_MLA_HEADS * V_HEAD_DIM
A_IN_SIZES = (SB_W, SB_W, SB_W, SB_W, MEM_W, MEM_W)
B_IN_SIZES = (Q_LORA_RANK, MLA_W, MEM_W, MEM_W)
A_IN_W = sum(A_IN_SIZES)
B_IN_W = sum(B_IN_SIZES)
MIX_A_W = SB_W + MEM_W
MIX_B_W = MLA_W + MEM_W

kernel_name = "yoco_stickbreak_mla_memory_hybrid"


def _split(x, sizes):
    idx = [int(i) for i in np.cumsum(sizes)[:-1]]
    return jnp.split(x, idx, axis=-1)


def rmsnorm(x, g):
    xf = x.astype(jnp.float32)
    y = xf * lax.rsqrt(jnp.mean(xf * xf, axis=-1, keepdims=True) + EPS)
    return (y * g.astype(jnp.float32)).astype(x.dtype)


def rope_tables(positions):
    inv_freq = jnp.power(ROPE_THETA, -jnp.arange(0, QK_ROPE_DIM, 2, dtype=jnp.float32) / QK_ROPE_DIM)
    ang = positions.astype(jnp.float32)[..., None] * inv_freq
    return jnp.cos(ang), jnp.sin(ang)


def apply_rope(x, cos, sin):
    half = x.shape[-1] // 2
    x1, x2 = x[..., :half], x[..., half:]
    return jnp.concatenate([x1 * cos - x2 * sin, x2 * cos + x1 * sin], axis=-1).astype(x.dtype)


def _to_blocks(t):
    b, s = t.shape[0], t.shape[1]
    return t.reshape(b, s // BLOCK_Q, BLOCK_Q, *t.shape[2:]).swapaxes(0, 1)


def _from_blocks(t):
    t = t.swapaxes(0, 1)
    return t.reshape(t.shape[0], t.shape[1] * t.shape[2], *t.shape[3:])


def stick_breaking_attention(q, k, v):
    s_len, d = q.shape[1], q.shape[-1]
    scale = d ** -0.5
    key_pos = jnp.arange(s_len)

    def block(args):
        qi, bi = args
        z = jnp.einsum('bqhd,bkhd->bhqk', qi, k).astype(jnp.float32) * scale
        q_pos = bi * BLOCK_Q + jnp.arange(BLOCK_Q)
        causal = key_pos[None, :] < q_pos[:, None]
        log_1m_beta = jnp.where(causal, jax.nn.log_sigmoid(-z), 0.0)
        between = lax.cumsum(log_1m_beta, axis=3, reverse=True) - log_1m_beta
        a = jnp.where(causal, jnp.exp(jax.nn.log_sigmoid(z) + between), 0.0)
        return jnp.einsum('bhqk,bkhd->bqhd', a.astype(v.dtype), v)

    nblk = s_len // BLOCK_Q
    out = lax.map(block, (_to_blocks(q), jnp.arange(nblk)))
    return _from_blocks(out)


def mla_causal_attention(q_nope, q_rope, k_nope, k_rope, v):
    s_len = q_nope.shape[1]
    scale = (QK_NOPE_DIM + QK_ROPE_DIM) ** -0.5
    key_pos = jnp.arange(s_len)

    def block(args):
        qn, qr, bi = args
        s = (jnp.einsum('bqhd,bkhd->bhqk', qn, k_nope)
             + jnp.einsum('bqhr,bkr->bhqk', qr, k_rope)).astype(jnp.float32) * scale
        q_pos = bi * BLOCK_Q + jnp.arange(BLOCK_Q)
        mask = key_pos[None, :] <= q_pos[:, None]
        p = jax.nn.softmax(jnp.where(mask, s, -jnp.inf), axis=-1)
        return jnp.einsum('bhqk,bkhd->bqhd', p.astype(v.dtype), v)

    nblk = s_len // BLOCK_Q
    out = lax.map(block, (_to_blocks(q_nope), _to_blocks(q_rope), jnp.arange(nblk)))
    return _from_blocks(out)


def memory_attention(q_m, mem, g_norm, w_kv, g_q, g_k):
    b, s_len, _ = q_m.shape
    mk, mv = _split(rmsnorm(mem, g_norm) @ w_kv, (MEM_W, MEM_W))
    mk = rmsnorm(mk.reshape(b, -1, N_MEM_HEADS, HEAD_DIM), g_k)
    mv = mv.reshape(b, -1, N_MEM_HEADS, HEAD_DIM)
    q = rmsnorm(q_m.reshape(b, s_len, N_MEM_HEADS, HEAD_DIM), g_q)
    s = jnp.einsum('bqhd,bmhd->bhqm', q, mk).astype(jnp.float32) * HEAD_DIM ** -0.5
    p = jax.nn.softmax(s, axis=-1)
    return jnp.einsum('bhqm,bmhd->bqhd', p.astype(mv.dtype), mv).reshape(b, s_len, MEM_W)


def _fwd_setup_inputs(seed: int = 0) -> dict:
    key = jax.random.key(seed)
    ks = iter(jax.random.split(key, 32))
    f32 = jnp.float32

    def w(shape, fan_in):
        return jax.random.normal(next(ks), shape, f32) * fan_in ** -0.5

    def gain(shape):
        return 1.0 + 0.02 * jax.random.normal(next(ks), shape, f32)

    x = jax.random.normal(next(ks), (BATCH, SEQ, D_MODEL), f32)
    mem = jax.random.normal(next(ks), (BATCH, MEM_LEN, D_MODEL), f32)
    positions = jnp.broadcast_to(jnp.arange(SEQ, dtype=jnp.int32)[None, :], (BATCH, SEQ))
    return {
        "x": x,
        "mem": mem,
        "positions": positions,
        "a_norm": gain((N_A_LAYERS, D_MODEL)),
        "a_w_in": w((N_A_LAYERS, D_MODEL, A_IN_W), D_MODEL),
        "a_w_out": w((N_A_LAYERS, MIX_A_W, D_MODEL), MIX_A_W),
        "kv_norm": gain((D_MODEL,)),
        "w_dkv": w((D_MODEL, KV_LORA_RANK + QK_ROPE_DIM), D_MODEL),
        "g_ckv": gain((KV_LORA_RANK,)),
        "w_ukv": w((KV_LORA_RANK, N_MLA_HEADS * (QK_NOPE_DIM + V_HEAD_DIM)), KV_LORA_RANK),
        "g_k_nope": gain((QK_NOPE_DIM,)),
        "g_k_rope": gain((QK_ROPE_DIM,)),
        "b_norm": gain((N_B_LAYERS, D_MODEL)),
        "b_w_in": w((N_B_LAYERS, D_MODEL, B_IN_W), D_MODEL),
        "b_g_q_lat": gain((N_B_LAYERS, Q_LORA_RANK)),
        "b_w_uq": w((N_B_LAYERS, Q_LORA_RANK, N_MLA_HEADS * (QK_NOPE_DIM + QK_ROPE_DIM)), Q_LORA_RANK),
        "b_g_q_nope": gain((N_B_LAYERS, QK_NOPE_DIM)),
        "b_g_q_rope": gain((N_B_LAYERS, QK_ROPE_DIM)),
        "b_w_out": w((N_B_LAYERS, MIX_B_W, D_MODEL), MIX_B_W),
        "mem_norm": gain((DEPTH, D_MODEL)),
        "w_mem_kv": w((DEPTH, D_MODEL, 2 * MEM_W), D_MODEL),
        "g_mem_q": gain((DEPTH, HEAD_DIM)),
        "g_mem_k": gain((DEPTH, HEAD_DIM)),
    }


def _fwd_reference(x, mem, positions, a_norm, a_w_in, a_w_out, kv_norm, w_dkv, g_ckv, w_ukv,
              g_k_nope, g_k_rope, b_norm, b_w_in, b_g_q_lat, b_w_uq, b_g_q_nope, b_g_q_rope,
              b_w_out, mem_norm, w_mem_kv, g_mem_q, g_mem_k):
    b, s_len, _ = x.shape
    cos, sin = rope_tables(positions)
    cos_h, sin_h = cos[:, :, None, :], sin[:, :, None, :]
    shared = None
    for layer in range(DEPTH):
        if layer < N_A_LAYERS:
            i = layer
            h = rmsnorm(x, a_norm[i])
            q, k, v, g_sb, q_m, g_m = _split(h @ a_w_in[i], A_IN_SIZES)
            heads = lambda t: t.reshape(b, s_len, N_SB_HEADS, HEAD_DIM)
            sb = stick_breaking_attention(heads(q), heads(k), heads(v)).reshape(b, s_len, SB_W)
            mo = memory_attention(q_m, mem, mem_norm[layer], w_mem_kv[layer], g_mem_q[layer], g_mem_k[layer])
            mixed = jnp.concatenate([sb * jax.nn.silu(g_sb), mo * jax.nn.silu(g_m)], axis=-1)
            x = x + mixed @ a_w_out[i]
        else:
            j = layer - N_A_LAYERS
            if shared is None:
                c_kv, k_r = _split(rmsnorm(x, kv_norm) @ w_dkv, (KV_LORA_RANK, QK_ROPE_DIM))
                kv = (rmsnorm(c_kv, g_ckv) @ w_ukv).reshape(b, s_len, N_MLA_HEADS, QK_NOPE_DIM + V_HEAD_DIM)
                k_nope = rmsnorm(kv[..., :QK_NOPE_DIM], g_k_nope)
                v_mla = kv[..., QK_NOPE_DIM:]
                k_rope = apply_rope(rmsnorm(k_r, g_k_rope), cos, sin)
                shared = (k_nope, k_rope, v_mla)
            k_nope, k_rope, v_mla = shared
            h = rmsnorm(x, b_norm[j])
            q_lat, g_mla, q_m, g_m = _split(h @ b_w_in[j], B_IN_SIZES)
            q = (rmsnorm(q_lat, b_g_q_lat[j]) @ b_w_uq[j]).reshape(
                b, s_len, N_MLA_HEADS, QK_NOPE_DIM + QK_ROPE_DIM)
            q_nope = rmsnorm(q[..., :QK_NOPE_DIM], b_g_q_nope[j])
            q_rope = apply_rope(rmsnorm(q[..., QK_NOPE_DIM:], b_g_q_rope[j]), cos_h, sin_h)
            att = mla_causal_attention(q_nope, q_rope, k_nope, k_rope, v_mla).reshape(b, s_len, MLA_W)
            mo = memory_attention(q_m, mem, mem_norm[layer], w_mem_kv[layer], g_mem_q[layer], g_mem_k[layer])
            mixed = jnp.concatenate([att * jax.nn.silu(g_mla), mo * jax.nn.silu(g_m)], axis=-1)
            x = x + mixed @ b_w_out[j]
    return x


import jax as _jax
import jax.numpy as _jnp

TWIN_FORMAT = 'train_step'
FWD_PARAMS = ['x', 'mem', 'positions', 'a_norm', 'a_w_in', 'a_w_out', 'kv_norm', 'w_dkv', 'g_ckv', 'w_ukv', 'g_k_nope', 'g_k_rope', 'b_norm', 'b_w_in', 'b_g_q_lat', 'b_w_uq', 'b_g_q_nope', 'b_g_q_rope', 'b_w_out', 'mem_norm', 'w_mem_kv', 'g_mem_q', 'g_mem_k']
TWIN_WEIGHTS = ['a_norm', 'a_w_in', 'a_w_out', 'kv_norm', 'w_dkv', 'g_ckv', 'w_ukv', 'g_k_nope', 'g_k_rope', 'b_norm', 'b_w_in', 'b_g_q_lat', 'b_w_uq', 'b_g_q_nope', 'b_g_q_rope', 'b_w_out', 'mem_norm', 'w_mem_kv', 'g_mem_q', 'g_mem_k']
TWIN_DIFF_INPUT = 'x'
TWIN_INPUTS = ['x', 'mem', 'positions', 'a_norm', 'a_w_in', 'a_w_out', 'kv_norm', 'w_dkv', 'g_ckv', 'w_ukv', 'g_k_nope', 'g_k_rope', 'b_norm', 'b_w_in', 'b_g_q_lat', 'b_w_uq', 'b_g_q_nope', 'b_g_q_rope', 'b_w_out', 'mem_norm', 'w_mem_kv', 'g_mem_q', 'g_mem_k', 'loss_target', 'm_a_norm', 'm_a_w_in', 'm_a_w_out', 'm_kv_norm', 'm_w_dkv', 'm_g_ckv', 'm_w_ukv', 'm_g_k_nope', 'm_g_k_rope', 'm_b_norm', 'm_b_w_in', 'm_b_g_q_lat', 'm_b_w_uq', 'm_b_g_q_nope', 'm_b_g_q_rope', 'm_b_w_out', 'm_mem_norm', 'm_w_mem_kv', 'm_g_mem_q', 'm_g_mem_k', 'v_a_norm', 'v_a_w_in', 'v_a_w_out', 'v_kv_norm', 'v_w_dkv', 'v_g_ckv', 'v_w_ukv', 'v_g_k_nope', 'v_g_k_rope', 'v_b_norm', 'v_b_w_in', 'v_b_g_q_lat', 'v_b_w_uq', 'v_b_g_q_nope', 'v_b_g_q_rope', 'v_b_w_out', 'v_mem_norm', 'v_w_mem_kv', 'v_g_mem_q', 'v_g_mem_k']
TWIN_OUTPUTS = ['loss', 'grad_x', 'grad_a_norm', 'grad_a_w_in', 'grad_a_w_out', 'grad_kv_norm', 'grad_w_dkv', 'grad_g_ckv', 'grad_w_ukv', 'grad_g_k_nope', 'grad_g_k_rope', 'grad_b_norm', 'grad_b_w_in', 'grad_b_g_q_lat', 'grad_b_w_uq', 'grad_b_g_q_nope', 'grad_b_g_q_rope', 'grad_b_w_out', 'grad_mem_norm', 'grad_w_mem_kv', 'grad_g_mem_q', 'grad_g_mem_k', 'delta_a_norm', 'delta_a_w_in', 'delta_a_w_out', 'delta_kv_norm', 'delta_w_dkv', 'delta_g_ckv', 'delta_w_ukv', 'delta_g_k_nope', 'delta_g_k_rope', 'delta_b_norm', 'delta_b_w_in', 'delta_b_g_q_lat', 'delta_b_w_uq', 'delta_b_g_q_nope', 'delta_b_g_q_rope', 'delta_b_w_out', 'delta_mem_norm', 'delta_w_mem_kv', 'delta_g_mem_q', 'delta_g_mem_k', 'new_m_a_norm', 'new_m_a_w_in', 'new_m_a_w_out', 'new_m_kv_norm', 'new_m_w_dkv', 'new_m_g_ckv', 'new_m_w_ukv', 'new_m_g_k_nope', 'new_m_g_k_rope', 'new_m_b_norm', 'new_m_b_w_in', 'new_m_b_g_q_lat', 'new_m_b_w_uq', 'new_m_b_g_q_nope', 'new_m_b_g_q_rope', 'new_m_b_w_out', 'new_m_mem_norm', 'new_m_w_mem_kv', 'new_m_g_mem_q', 'new_m_g_mem_k', 'new_v_a_norm', 'new_v_a_w_in', 'new_v_a_w_out', 'new_v_kv_norm', 'new_v_w_dkv', 'new_v_g_ckv', 'new_v_w_ukv', 'new_v_g_k_nope', 'new_v_g_k_rope', 'new_v_b_norm', 'new_v_b_w_in', 'new_v_b_g_q_lat', 'new_v_b_w_uq', 'new_v_b_g_q_nope', 'new_v_b_g_q_rope', 'new_v_b_w_out', 'new_v_mem_norm', 'new_v_w_mem_kv', 'new_v_g_mem_q', 'new_v_g_mem_k']
TWIN_LEAF_KINDS = {'loss': 'loss', 'grad_x': 'grad_x', 'grad_a_norm': 'grad_w', 'grad_a_w_in': 'grad_w', 'grad_a_w_out': 'grad_w', 'grad_kv_norm': 'grad_w', 'grad_w_dkv': 'grad_w', 'grad_g_ckv': 'grad_w', 'grad_w_ukv': 'grad_w', 'grad_g_k_nope': 'grad_w', 'grad_g_k_rope': 'grad_w', 'grad_b_norm': 'grad_w', 'grad_b_w_in': 'grad_w', 'grad_b_g_q_lat': 'grad_w', 'grad_b_w_uq': 'grad_w', 'grad_b_g_q_nope': 'grad_w', 'grad_b_g_q_rope': 'grad_w', 'grad_b_w_out': 'grad_w', 'grad_mem_norm': 'grad_w', 'grad_w_mem_kv': 'grad_w', 'grad_g_mem_q': 'grad_w', 'grad_g_mem_k': 'grad_w', 'delta_a_norm': 'delta_w', 'delta_a_w_in': 'delta_w', 'delta_a_w_out': 'delta_w', 'delta_kv_norm': 'delta_w', 'delta_w_dkv': 'delta_w', 'delta_g_ckv': 'delta_w', 'delta_w_ukv': 'delta_w', 'delta_g_k_nope': 'delta_w', 'delta_g_k_rope': 'delta_w', 'delta_b_norm': 'delta_w', 'delta_b_w_in': 'delta_w', 'delta_b_g_q_lat': 'delta_w', 'delta_b_w_uq': 'delta_w', 'delta_b_g_q_nope': 'delta_w', 'delta_b_g_q_rope': 'delta_w', 'delta_b_w_out': 'delta_w', 'delta_mem_norm': 'delta_w', 'delta_w_mem_kv': 'delta_w', 'delta_g_mem_q': 'delta_w', 'delta_g_mem_k': 'delta_w', 'new_m_a_norm': 'new_m', 'new_m_a_w_in': 'new_m', 'new_m_a_w_out': 'new_m', 'new_m_kv_norm': 'new_m', 'new_m_w_dkv': 'new_m', 'new_m_g_ckv': 'new_m', 'new_m_w_ukv': 'new_m', 'new_m_g_k_nope': 'new_m', 'new_m_g_k_rope': 'new_m', 'new_m_b_norm': 'new_m', 'new_m_b_w_in': 'new_m', 'new_m_b_g_q_lat': 'new_m', 'new_m_b_w_uq': 'new_m', 'new_m_b_g_q_nope': 'new_m', 'new_m_b_g_q_rope': 'new_m', 'new_m_b_w_out': 'new_m', 'new_m_mem_norm': 'new_m', 'new_m_w_mem_kv': 'new_m', 'new_m_g_mem_q': 'new_m', 'new_m_g_mem_k': 'new_m', 'new_v_a_norm': 'new_v', 'new_v_a_w_in': 'new_v', 'new_v_a_w_out': 'new_v', 'new_v_kv_norm': 'new_v', 'new_v_w_dkv': 'new_v', 'new_v_g_ckv': 'new_v', 'new_v_w_ukv': 'new_v', 'new_v_g_k_nope': 'new_v', 'new_v_g_k_rope': 'new_v', 'new_v_b_norm': 'new_v', 'new_v_b_w_in': 'new_v', 'new_v_b_g_q_lat': 'new_v', 'new_v_b_w_uq': 'new_v', 'new_v_b_g_q_nope': 'new_v', 'new_v_b_g_q_rope': 'new_v', 'new_v_b_w_out': 'new_v', 'new_v_mem_norm': 'new_v', 'new_v_w_mem_kv': 'new_v', 'new_v_g_mem_q': 'new_v', 'new_v_g_mem_k': 'new_v'}


def _forward(args):
    return _fwd_reference(*[args[k] for k in FWD_PARAMS])


def _output_shape():
    def fwd():
        inp = _fwd_setup_inputs(0)
        return _fwd_reference(*[inp[k] for k in FWD_PARAMS])
    out = _jax.eval_shape(fwd)
    return out.shape, out.dtype

N_MICROBATCH = 1
ADAM_LR = 0.001
ADAM_B1 = 0.9
ADAM_B2 = 0.999
ADAM_EPS = 1e-08
ADAM_WD = 0.01
ADAM_STEP = 10
PER_EXAMPLE_BATCH_AXIS = {'x': 0, 'mem': 0, 'positions': 0, 'loss_target': 0}
SHARED_INPUTS = []
_WEIGHT_DTYPES = {'a_norm': _jnp.float32, 'a_w_in': _jnp.float32, 'a_w_out': _jnp.float32, 'kv_norm': _jnp.float32, 'w_dkv': _jnp.float32, 'g_ckv': _jnp.float32, 'w_ukv': _jnp.float32, 'g_k_nope': _jnp.float32, 'g_k_rope': _jnp.float32, 'b_norm': _jnp.float32, 'b_w_in': _jnp.float32, 'b_g_q_lat': _jnp.float32, 'b_w_uq': _jnp.float32, 'b_g_q_nope': _jnp.float32, 'b_g_q_rope': _jnp.float32, 'b_w_out': _jnp.float32, 'mem_norm': _jnp.float32, 'w_mem_kv': _jnp.float32, 'g_mem_q': _jnp.float32, 'g_mem_k': _jnp.float32}
MOMENT_SCALE = {'a_norm': 8.814302e+00, 'a_w_in': 8.139471e-02, 'a_w_out': 7.766639e-02, 'kv_norm': 3.427590e-02, 'w_dkv': 6.305355e-02, 'g_ckv': 2.936619e-01, 'w_ukv': 2.472277e-02, 'g_k_nope': 4.339775e-01, 'g_k_rope': 4.359353e-01, 'b_norm': 9.635388e-02, 'b_w_in': 2.678396e-02, 'b_g_q_lat': 4.336736e-02, 'b_w_uq': 1.906748e-02, 'b_g_q_nope': 4.333215e-01, 'b_g_q_rope': 4.351080e-01, 'b_w_out': 2.165412e-02, 'mem_norm': 2.267596e-02, 'w_mem_kv': 1.331284e-02, 'g_mem_q': 4.050265e-01, 'g_mem_k': 4.059331e-01}


def _to_microbatches(a, axis):
    t = _jnp.moveaxis(a, axis, 0)
    t = t.reshape((N_MICROBATCH, t.shape[0] // N_MICROBATCH) + t.shape[1:])
    return _jnp.moveaxis(t, 1, axis + 1)


def setup_inputs(seed: int = 0) -> dict:
    inp = _fwd_setup_inputs(seed)
    key = _jax.random.fold_in(_jax.random.key(seed), 7919)
    shape, _ = _output_shape()
    out = dict(inp)
    out["loss_target"] = _jax.random.normal(_jax.random.fold_in(key, 0), shape, _jnp.float32)
    for i, name in enumerate(TWIN_WEIGHTS):
        w = inp[name].astype(_jnp.float32)
        if MOMENT_SCALE is None:
            s = _jnp.sqrt(_jnp.mean(_jnp.square(w)) + 1e-30)
        else:
            s = MOMENT_SCALE[name]
        km, kv = _jax.random.split(_jax.random.fold_in(key, i + 1))
        out[name] = w
        out["m_" + name] = s * _jax.random.normal(km, w.shape, _jnp.float32)
        out["v_" + name] = (s * s) * _jax.random.uniform(kv, w.shape, _jnp.float32, 0.5, 1.5)
    if N_MICROBATCH > 1:
        for name, axis in PER_EXAMPLE_BATCH_AXIS.items():
            out[name] = _to_microbatches(out[name], axis)
    return {'x': out['x'], 'mem': out['mem'], 'positions': out['positions'], 'a_norm': out['a_norm'], 'a_w_in': out['a_w_in'], 'a_w_out': out['a_w_out'], 'kv_norm': out['kv_norm'], 'w_dkv': out['w_dkv'], 'g_ckv': out['g_ckv'], 'w_ukv': out['w_ukv'], 'g_k_nope': out['g_k_nope'], 'g_k_rope': out['g_k_rope'], 'b_norm': out['b_norm'], 'b_w_in': out['b_w_in'], 'b_g_q_lat': out['b_g_q_lat'], 'b_w_uq': out['b_w_uq'], 'b_g_q_nope': out['b_g_q_nope'], 'b_g_q_rope': out['b_g_q_rope'], 'b_w_out': out['b_w_out'], 'mem_norm': out['mem_norm'], 'w_mem_kv': out['w_mem_kv'], 'g_mem_q': out['g_mem_q'], 'g_mem_k': out['g_mem_k'], 'loss_target': out['loss_target'], 'm_a_norm': out['m_a_norm'], 'm_a_w_in': out['m_a_w_in'], 'm_a_w_out': out['m_a_w_out'], 'm_kv_norm': out['m_kv_norm'], 'm_w_dkv': out['m_w_dkv'], 'm_g_ckv': out['m_g_ckv'], 'm_w_ukv': out['m_w_ukv'], 'm_g_k_nope': out['m_g_k_nope'], 'm_g_k_rope': out['m_g_k_rope'], 'm_b_norm': out['m_b_norm'], 'm_b_w_in': out['m_b_w_in'], 'm_b_g_q_lat': out['m_b_g_q_lat'], 'm_b_w_uq': out['m_b_w_uq'], 'm_b_g_q_nope': out['m_b_g_q_nope'], 'm_b_g_q_rope': out['m_b_g_q_rope'], 'm_b_w_out': out['m_b_w_out'], 'm_mem_norm': out['m_mem_norm'], 'm_w_mem_kv': out['m_w_mem_kv'], 'm_g_mem_q': out['m_g_mem_q'], 'm_g_mem_k': out['m_g_mem_k'], 'v_a_norm': out['v_a_norm'], 'v_a_w_in': out['v_a_w_in'], 'v_a_w_out': out['v_a_w_out'], 'v_kv_norm': out['v_kv_norm'], 'v_w_dkv': out['v_w_dkv'], 'v_g_ckv': out['v_g_ckv'], 'v_w_ukv': out['v_w_ukv'], 'v_g_k_nope': out['v_g_k_nope'], 'v_g_k_rope': out['v_g_k_rope'], 'v_b_norm': out['v_b_norm'], 'v_b_w_in': out['v_b_w_in'], 'v_b_g_q_lat': out['v_b_g_q_lat'], 'v_b_w_uq': out['v_b_w_uq'], 'v_b_g_q_nope': out['v_b_g_q_nope'], 'v_b_g_q_rope': out['v_b_g_q_rope'], 'v_b_w_out': out['v_b_w_out'], 'v_mem_norm': out['v_mem_norm'], 'v_w_mem_kv': out['v_w_mem_kv'], 'v_g_mem_q': out['v_g_mem_q'], 'v_g_mem_k': out['v_g_mem_k']}


def _loss(weights, diff, rest, loss_target):
    with _jax.named_scope("forward"):
        args = {**rest, TWIN_DIFF_INPUT: diff, **{k: w.astype(_WEIGHT_DTYPES[k]) for k, w in weights.items()}}
        y = _forward(args)
    with _jax.named_scope("loss_head"):
        err = _jnp.square(y.astype(_jnp.float32) - loss_target)
        return 0.5 * _jnp.sum(_jnp.mean(err, axis=-1)) if err.ndim else 0.5 * err


def _adamw(w, g, m, v):
    m = ADAM_B1 * m + (1.0 - ADAM_B1) * g
    v = ADAM_B2 * v + (1.0 - ADAM_B2) * _jnp.square(g)
    m_hat = m / (1.0 - ADAM_B1 ** ADAM_STEP)
    v_hat = v / (1.0 - ADAM_B2 ** ADAM_STEP)
    delta = -ADAM_LR * (m_hat / (_jnp.sqrt(v_hat) + ADAM_EPS) + ADAM_WD * w)
    return delta, m, v


def reference(x, mem, positions, a_norm, a_w_in, a_w_out, kv_norm, w_dkv, g_ckv, w_ukv, g_k_nope, g_k_rope, b_norm, b_w_in, b_g_q_lat, b_w_uq, b_g_q_nope, b_g_q_rope, b_w_out, mem_norm, w_mem_kv, g_mem_q, g_mem_k, loss_target, m_a_norm, m_a_w_in, m_a_w_out, m_kv_norm, m_w_dkv, m_g_ckv, m_w_ukv, m_g_k_nope, m_g_k_rope, m_b_norm, m_b_w_in, m_b_g_q_lat, m_b_w_uq, m_b_g_q_nope, m_b_g_q_rope, m_b_w_out, m_mem_norm, m_w_mem_kv, m_g_mem_q, m_g_mem_k, v_a_norm, v_a_w_in, v_a_w_out, v_kv_norm, v_w_dkv, v_g_ckv, v_w_ukv, v_g_k_nope, v_g_k_rope, v_b_norm, v_b_w_in, v_b_g_q_lat, v_b_w_uq, v_b_g_q_nope, v_b_g_q_rope, v_b_w_out, v_mem_norm, v_w_mem_kv, v_g_mem_q, v_g_mem_k):
    given = dict(x=x, mem=mem, positions=positions, a_norm=a_norm, a_w_in=a_w_in, a_w_out=a_w_out, kv_norm=kv_norm, w_dkv=w_dkv, g_ckv=g_ckv, w_ukv=w_ukv, g_k_nope=g_k_nope, g_k_rope=g_k_rope, b_norm=b_norm, b_w_in=b_w_in, b_g_q_lat=b_g_q_lat, b_w_uq=b_w_uq, b_g_q_nope=b_g_q_nope, b_g_q_rope=b_g_q_rope, b_w_out=b_w_out, mem_norm=mem_norm, w_mem_kv=w_mem_kv, g_mem_q=g_mem_q, g_mem_k=g_mem_k, loss_target=loss_target, m_a_norm=m_a_norm, m_a_w_in=m_a_w_in, m_a_w_out=m_a_w_out, m_kv_norm=m_kv_norm, m_w_dkv=m_w_dkv, m_g_ckv=m_g_ckv, m_w_ukv=m_w_ukv, m_g_k_nope=m_g_k_nope, m_g_k_rope=m_g_k_rope, m_b_norm=m_b_norm, m_b_w_in=m_b_w_in, m_b_g_q_lat=m_b_g_q_lat, m_b_w_uq=m_b_w_uq, m_b_g_q_nope=m_b_g_q_nope, m_b_g_q_rope=m_b_g_q_rope, m_b_w_out=m_b_w_out, m_mem_norm=m_mem_norm, m_w_mem_kv=m_w_mem_kv, m_g_mem_q=m_g_mem_q, m_g_mem_k=m_g_mem_k, v_a_norm=v_a_norm, v_a_w_in=v_a_w_in, v_a_w_out=v_a_w_out, v_kv_norm=v_kv_norm, v_w_dkv=v_w_dkv, v_g_ckv=v_g_ckv, v_w_ukv=v_w_ukv, v_g_k_nope=v_g_k_nope, v_g_k_rope=v_g_k_rope, v_b_norm=v_b_norm, v_b_w_in=v_b_w_in, v_b_g_q_lat=v_b_g_q_lat, v_b_w_uq=v_b_w_uq, v_b_g_q_nope=v_b_g_q_nope, v_b_g_q_rope=v_b_g_q_rope, v_b_w_out=v_b_w_out, v_mem_norm=v_mem_norm, v_w_mem_kv=v_w_mem_kv, v_g_mem_q=v_g_mem_q, v_g_mem_k=v_g_mem_k)
    weights = {n: given[n] for n in TWIN_WEIGHTS}
    shared = {n: given[n] for n in SHARED_INPUTS}
    per_example = {n: given[n] for n in ['x', 'mem', 'positions']}
    grad_fn = _jax.value_and_grad(_loss, argnums=(0, 1))

    def one_microbatch(ex, loss_target):
        ex = dict(ex)
        diff = ex.pop(TWIN_DIFF_INPUT)
        return grad_fn(weights, diff, {**shared, **ex}, loss_target)

    if N_MICROBATCH == 1:
        loss, (grad_w, grad_x) = one_microbatch(per_example, given["loss_target"])
    else:
        def body(carry, xs):
            loss_sum, grad_sum = carry
            l_k, (gw_k, gx_k) = one_microbatch(xs[0], xs[1])
            with _jax.named_scope("update"):
                return (loss_sum + l_k, _jax.tree.map(_jnp.add, grad_sum, gw_k)), gx_k

        init = (_jnp.zeros((), _jnp.float32), _jax.tree.map(_jnp.zeros_like, weights))
        (loss, grad_w), grad_x = _jax.lax.scan(body, init, (per_example, given["loss_target"]))
    with _jax.named_scope("update"):
        delta_w, new_m, new_v = {}, {}, {}
        for n in TWIN_WEIGHTS:
            delta_w[n], new_m[n], new_v[n] = _adamw(weights[n], grad_w[n], given["m_" + n], given["v_" + n])
    return (loss, grad_x, *[grad_w[n] for n in TWIN_WEIGHTS], *[delta_w[n] for n in TWIN_WEIGHTS],
            *[new_m[n] for n in TWIN_WEIGHTS], *[new_v[n] for n in TWIN_WEIGHTS])
```

```python
import functools

import jax
import jax.numpy as jnp
from jax import lax
from jax.experimental import pallas as pl
from jax.experimental.pallas import tpu as pltpu

F32 = jnp.float32
BF16 = jnp.bfloat16
MESH = pl.DeviceIdType.MESH

D_MODEL = 2048
HEAD_DIM = 128
N_SB_HEADS = 12
N_MEM_HEADS = 4
N_MLA_HEADS = 12
MEM_LEN = 256
Q_LORA = 512
KV_LORA = 512
ROPE_DIM = 64
SB_W = N_SB_HEADS * HEAD_DIM
MEM_W = N_MEM_HEADS * HEAD_DIM
MLA_W = N_MLA_HEADS * HEAD_DIM
QKV_W = 3 * SB_W
GATE_W = SB_W + 2 * MEM_W
CAT_W = 2 * HEAD_DIM
ROPE_THETA = 10000.0
EPS = 1e-6
N_CHIPS = 4
N_DEV = 8

ADAM_LR = 0.001
ADAM_B1 = 0.9
ADAM_B2 = 0.999
ADAM_EPS = 1e-08
ADAM_WD = 0.01
ADAM_STEP = 10

VMEM_LIMIT_BYTES = 56 * 1024 * 1024
ROW_BLOCK = 256
ATT_BLOCK = 256


def _params(*sem):
    return pltpu.CompilerParams(dimension_semantics=sem, vmem_limit_bytes=VMEM_LIMIT_BYTES)


def _pick(n, cands):
    for c in cands:
        if n % c == 0:
            return c
    return n


def _mm(a, b, *, name, ta=False, tb=False, out_dtype=F32, res=None, n_split=1):
    if ta:
        k_dim, m_dim = a.shape
    else:
        m_dim, k_dim = a.shape
    if tb:
        n_dim, kb = b.shape
    else:
        kb, n_dim = b.shape
    assert kb == k_dim, (a.shape, b.shape)
    n_per = n_dim // n_split
    bm = m_dim if m_dim <= 1024 else _pick(m_dim, (1024, 512, 256))
    bn = n_per if n_per <= 1024 else _pick(n_per, (1024, 896, 768, 640, 512, 256, 128))
    bk = k_dim if k_dim <= 1024 else _pick(k_dim, (512, 256, 128))
    nk = k_dim // bk
    nb_per = n_per // bn
    grid = (m_dim // bm, n_dim // bn, nk)
    a_spec = (pl.BlockSpec((bk, bm), lambda i, j, k: (k, i)) if ta
              else pl.BlockSpec((bm, bk), lambda i, j, k: (i, k)))
    b_spec = (pl.BlockSpec((bn, bk), lambda i, j, k: (j, k)) if tb
              else pl.BlockSpec((bk, bn), lambda i, j, k: (k, j)))
    dims = (((0 if ta else 1,), (1 if tb else 0,)), ((), ()))
    in_specs = [a_spec, b_spec]
    args = [a, b]
    if res is not None:
        in_specs.append(pl.BlockSpec((bm, bn), lambda i, j, k: (i, j)))
        args.append(res)
    if n_split == 1:
        out_shape = jax.ShapeDtypeStruct((m_dim, n_dim), out_dtype)
        out_spec = pl.BlockSpec((bm, bn), lambda i, j, k: (i, j))
    else:
        out_shape = jax.ShapeDtypeStruct((n_split, m_dim, n_per), out_dtype)
        out_spec = pl.BlockSpec((None, bm, bn), lambda i, j, k: (j // nb_per, i, j % nb_per))

    def body(*refs):
        if res is None:
            a_ref, b_ref, o_ref, acc = refs
            r_ref = None
        else:
            a_ref, b_ref, r_ref, o_ref, acc = refs
        k = pl.program_id(2)

        @pl.when(k == 0)
        def _():
            acc[...] = jnp.zeros_like(acc)

        acc[...] += lax.dot_general(a_ref[...].astype(BF16), b_ref[...].astype(BF16), dims,
                                    preferred_element_type=F32)

        @pl.when(k == nk - 1)
        def _():
            r = acc[...]
            if r_ref is not None:
                r = r + r_ref[...]
            o_ref[...] = r.astype(out_dtype)

    return pl.pallas_call(
        body, out_shape=out_shape, grid=grid, in_specs=in_specs, out_specs=out_spec,
        scratch_shapes=[pltpu.VMEM((bm, bn), F32)], name=name,
        compiler_params=_params("parallel", "parallel", "arbitrary"),
    )(*args)


def _rowwise(body, n_rows, ins, outs, accs=(), *, name, block=ROW_BLOCK):
    blk = min(block, n_rows)
    assert n_rows % blk == 0
    in_specs = []
    for arr, is_row in ins:
        if is_row:
            assert arr.shape[0] == n_rows, (name, arr.shape, n_rows)
            in_specs.append(pl.BlockSpec((blk, arr.shape[1]), lambda i: (i, 0)))
        else:
            in_specs.append(pl.BlockSpec(arr.shape, lambda i, nd=arr.ndim: (0,) * nd))
    out_shape = [jax.ShapeDtypeStruct((n_rows, w), dt) for w, dt in outs]
    out_specs = [pl.BlockSpec((blk, w), lambda i: (i, 0)) for w, _ in outs]
    out_shape += [jax.ShapeDtypeStruct(s, dt) for s, dt in accs]
    out_specs += [pl.BlockSpec(s, lambda i, nd=len(s): (0,) * nd) for s, _ in accs]
    n_in, n_out, n_acc = len(ins), len(outs), len(accs)

    def kern(*refs):
        in_refs = refs[:n_in]
        out_refs = refs[n_in:n_in + n_out]
        acc_refs = refs[n_in + n_out:]
        if n_acc:
            @pl.when(pl.program_id(0) == 0)
            def _():
                for r in acc_refs:
                    r[...] = jnp.zeros_like(r)
        body(in_refs, out_refs, acc_refs)

    return pl.pallas_call(
        kern, out_shape=out_shape, grid=(n_rows // blk,), in_specs=in_specs, out_specs=out_specs,
        name=name, compiler_params=_params("arbitrary"),
    )(*[arr for arr, _ in ins])


def _rms(x, g, n=None):
    n = x.shape[-1] if n is None else n
    r = lax.rsqrt(jnp.sum(x * x, axis=-1, keepdims=True) / n + EPS)
    return x * r * g


def _rms_bwd(x, g, dy, n=None):
    n = x.shape[-1] if n is None else n
    r = lax.rsqrt(jnp.sum(x * x, axis=-1, keepdims=True) / n + EPS)
    gdy = dy * g
    dx = r * (gdy - x * ((r * r) * (jnp.sum(gdy * x, axis=-1, keepdims=True) / n)))
    dg = jnp.sum(dy * x * r, axis=0, keepdims=True)
    return dx, dg


def _swap_halves(x):
    lane = lax.broadcasted_iota(jnp.int32, x.shape, 1)
    return jnp.where(lane < ROPE_DIM // 2, pltpu.roll(x, 128 - ROPE_DIM // 2, 1),
                     pltpu.roll(x, ROPE_DIM // 2, 1))


def _rope(n, cos_t, sin_t):
    return n * cos_t + _swap_halves(n) * sin_t


def _rope_bwd(dy, cos_t, sin_t):
    return dy * cos_t - _swap_halves(dy) * sin_t


def _sigmoid(g):
    return 1.0 / (1.0 + jnp.exp(-g))


def _dot_t(a, b):
    return lax.dot_general(a, b, (((1,), (1,)), ((), ())), preferred_element_type=F32)


def _tdot(a, b):
    return lax.dot_general(a, b, (((0,), (0,)), ((), ())), preferred_element_type=F32)


def _dot(a, b):
    return jnp.dot(a, b, preferred_element_type=F32)


def _hs(h, w=HEAD_DIM, base=0):
    return slice(base + h * w, base + (h + 1) * w)


def _mem_head(qm, gq, mk_h, mv_h):
    qb = _rms(qm, gq).astype(BF16)
    s = _dot_t(qb, mk_h) * (HEAD_DIM ** -0.5)
    e = jnp.exp(s - jnp.max(s, axis=-1, keepdims=True))
    p = e / jnp.sum(e, axis=-1, keepdims=True)
    mo = _dot(p.astype(BF16), mv_h)
    return qb, p, mo


def _mix_fwd(att, gates, c0, mk, mv, gq, *, name):
    n_rows = att.shape[0]

    def body(ins, outs, _):
        att_ref, g_ref, mk_ref, mv_ref, gq_ref = ins
        (o_ref,) = outs
        g = g_ref[:, c0:c0 + SB_W]
        o_ref[:, :SB_W] = (att_ref[...] * (g * _sigmoid(g))).astype(BF16)
        for h in range(N_MEM_HEADS):
            qm = g_ref[:, _hs(h, base=c0 + SB_W)]
            gm = g_ref[:, _hs(h, base=c0 + SB_W + MEM_W)]
            _, _, mo = _mem_head(qm, gq_ref[...], mk_ref[:, _hs(h)], mv_ref[:, _hs(h)])
            o_ref[:, _hs(h, base=SB_W)] = (mo * (gm * _sigmoid(gm))).astype(BF16)

    (mixed,) = _rowwise(body, n_rows,
                        [(att, True), (gates, True), (mk, False), (mv, False), (gq, False)],
                        [(D_MODEL, BF16)], name=name)
    return mixed


def _mix_bwd(dmixed, att, gates, c0, mk, mv, gq, *, name):
    n_rows = att.shape[0]
    scale = HEAD_DIM ** -0.5

    def body(ins, outs, accs):
        dm_ref, att_ref, g_ref, mk_ref, mv_ref, gq_ref = ins
        datt_ref, dg_ref = outs
        dmk_ref, dmv_ref, dgq_ref = accs
        g = g_ref[:, c0:c0 + SB_W]
        sg = _sigmoid(g)
        dm = dm_ref[:, :SB_W]
        datt_ref[...] = dm * (g * sg)
        dg_ref[:, :SB_W] = (dm * att_ref[...] * (sg * (1.0 + g * (1.0 - sg)))).astype(BF16)
        for h in range(N_MEM_HEADS):
            qm = g_ref[:, _hs(h, base=c0 + SB_W)]
            gm = g_ref[:, _hs(h, base=c0 + SB_W + MEM_W)]
            mk_h = mk_ref[:, _hs(h)]
            mv_h = mv_ref[:, _hs(h)]
            qb, p, mo = _mem_head(qm, gq_ref[...], mk_h, mv_h)
            sgm = _sigmoid(gm)
            dmh = dm_ref[:, _hs(h, base=SB_W)]
            dmo = dmh * (gm * sgm)
            dg_ref[:, _hs(h, base=SB_W + MEM_W)] = (
                dmh * mo * (sgm * (1.0 + gm * (1.0 - sgm)))).astype(BF16)
            dmo_b = dmo.astype(BF16)
            pb = p.astype(BF16)
            dp = _dot_t(dmo_b, mv_h)
            dmv_ref[:, _hs(h)] += _tdot(pb, dmo_b)
            ds = (p * (dp - jnp.sum(dp * p, axis=-1, keepdims=True)) * scale).astype(BF16)
            dqn = _dot(ds, mk_h)
            dmk_ref[:, _hs(h)] += _tdot(ds, qb)
            dqm, dgq = _rms_bwd(qm, gq_ref[...], dqn)
            dg_ref[:, _hs(h, base=SB_W)] = dqm.astype(BF16)
            dgq_ref[...] += dgq

    return _rowwise(body, n_rows,
                    [(dmixed, True), (att, True), (gates, True), (mk, False), (mv, False), (gq, False)],
                    [(SB_W, F32), (GATE_W, BF16)],
                    [((MEM_LEN, MEM_W), F32), ((MEM_LEN, MEM_W), F32), ((1, HEAD_DIM), F32)],
                    name=name)


def _mem_side_fwd(mem, g_norm, w_kv, g_k, *, tag):
    def norm_body(ins, outs, _):
        outs[0][...] = _rms(ins[0][...], ins[1][...]).astype(BF16)

    (mn,) = _rowwise(norm_body, MEM_LEN, [(mem, True), (g_norm, False)], [(D_MODEL, BF16)],
                     name=f"mem_norm_{tag}")
    mkv = _mm(mn, w_kv, name=f"mem_kv_{tag}")

    def kv_body(ins, outs, _):
        mkv_ref, gk_ref = ins
        mk_ref, mv_ref = outs
        for h in range(N_MEM_HEADS):
            mk_ref[:, _hs(h)] = _rms(mkv_ref[:, _hs(h)], gk_ref[...]).astype(BF16)
        mv_ref[...] = mkv_ref[:, MEM_W:].astype(BF16)

    mk, mv = _rowwise(kv_body, MEM_LEN, [(mkv, True), (g_k, False)], [(MEM_W, BF16), (MEM_W, BF16)],
                      name=f"mem_kv_prep_{tag}")
    return mn, mkv, mk, mv


def _mem_side_bwd(mem, g_norm, w_kv, g_k, mn, mkv, dmk, dmv, *, tag):
    def kv_body(ins, outs, accs):
        mkv_ref, gk_ref, dmk_ref, dmv_ref = ins
        (d_ref,) = outs
        (dgk_ref,) = accs
        for h in range(N_MEM_HEADS):
            dx, dg = _rms_bwd(mkv_ref[:, _hs(h)], gk_ref[...], dmk_ref[:, _hs(h)])
            d_ref[:, _hs(h)] = dx.astype(BF16)
            dgk_ref[...] += dg
        d_ref[:, MEM_W:] = dmv_ref[...].astype(BF16)

    dmkv, dgk = _rowwise(kv_body, MEM_LEN, [(mkv, True), (g_k, False), (dmk, True), (dmv, True)],
                         [(2 * MEM_W, BF16)], [((1, HEAD_DIM), F32)], name=f"mem_kv_prep_bwd_{tag}")
    dmn = _mm(dmkv, w_kv, tb=True, name=f"mem_kv_dx_{tag}")
    dw = _mm(mn, dmkv, ta=True, name=f"mem_kv_dw_{tag}")

    def norm_body(ins, outs, accs):
        _, dg = _rms_bwd(ins[0][...], ins[1][...], ins[2][...])
        accs[0][...] += dg

    (dgn,) = _rowwise(norm_body, MEM_LEN, [(mem, True), (g_norm, False), (dmn, True)], [],
                      [((1, D_MODEL), F32)], name=f"mem_norm_bwd_{tag}")
    return dw, dgn, dgk


def _softplus_parts(z):
    u = jnp.exp(-jnp.abs(z))
    l = -(jnp.maximum(z, 0.0) + jnp.log1p(u))
    return u, l


def _split_dot(x, tri):
    hi = x.astype(BF16)
    lo = (x - hi.astype(F32)).astype(BF16)
    return _dot(hi, tri) + _dot(lo, tri)


def _sb_fwd(qkv, *, name):
    seq = qkv.shape[0]
    blk = min(ATT_BLOCK, seq)
    nq = seq // blk
    scale = HEAD_DIM ** -0.5

    def body(q_ref, k_ref, v_ref, o_ref):
        i = pl.program_id(1)
        q = q_ref[...]
        row = lax.broadcasted_iota(jnp.int32, (blk, blk), 0)
        col = lax.broadcasted_iota(jnp.int32, (blk, blk), 1)
        after = (row > col).astype(BF16)
        causal = col < row

        def step(j, carry, masked):
            run, acc = carry
            off = pl.multiple_of(j * blk, blk)
            kb = k_ref[pl.ds(off, blk), :]
            vb = v_ref[pl.ds(off, blk), :]
            z = _dot_t(q, kb) * scale
            _, l = _softplus_parts(z)
            if masked:
                l = jnp.where(causal, l, 0.0)
            a = jnp.exp(z + l + _split_dot(l, after) + run)
            if masked:
                a = jnp.where(causal, a, 0.0)
            acc = acc + _dot(a.astype(BF16), vb)
            run = run + jnp.sum(l, axis=-1, keepdims=True)
            return run, acc

        carry = (jnp.zeros((blk, 1), F32), jnp.zeros((blk, HEAD_DIM), F32))
        carry = step(i, carry, True)
        carry = lax.fori_loop(0, i, lambda jj, c: step(i - 1 - jj, c, False), carry)
        o_ref[...] = carry[1]

    return pl.pallas_call(
        body, out_shape=jax.ShapeDtypeStruct((seq, SB_W), F32), grid=(N_SB_HEADS, nq),
        in_specs=[pl.BlockSpec((blk, HEAD_DIM), lambda h, i: (i, h)),
                  pl.BlockSpec((seq, HEAD_DIM), lambda h, i: (0, N_SB_HEADS + h)),
                  pl.BlockSpec((seq, HEAD_DIM), lambda h, i: (0, 2 * N_SB_HEADS + h))],
        out_specs=pl.BlockSpec((blk, HEAD_DIM), lambda h, i: (i, h)),
        name=name, compiler_params=_params("parallel", "arbitrary"),
    )(qkv, qkv, qkv)


def _sb_bwd(qkv, dout, *, name):
    seq = qkv.shape[0]
    blk = min(ATT_BLOCK, seq)
    nq = seq // blk
    scale = HEAD_DIM ** -0.5

    def body(q_ref, k_ref, v_ref, do_ref, dq_ref, dk_ref, dv_ref, de_s, sg_s):
        i = pl.program_id(1)

        @pl.when(i == 0)
        def _():
            dk_ref[...] = jnp.zeros_like(dk_ref)
            dv_ref[...] = jnp.zeros_like(dv_ref)

        q = q_ref[...]
        do = do_ref[...].astype(BF16)
        row = lax.broadcasted_iota(jnp.int32, (blk, blk), 0)
        col = lax.broadcasted_iota(jnp.int32, (blk, blk), 1)
        after = (row > col).astype(BF16)
        before = (row < col).astype(BF16)
        causal = col < row

        def sweep1(j, run, masked):
            off = pl.multiple_of(j * blk, blk)
            kb = k_ref[pl.ds(off, blk), :]
            vb = v_ref[pl.ds(off, blk), :]
            z = _dot_t(q, kb) * scale
            u, l = _softplus_parts(z)
            sig = jnp.where(z >= 0.0, 1.0, u) / (1.0 + u)
            if masked:
                l = jnp.where(causal, l, 0.0)
            a = jnp.exp(z + l + _split_dot(l, after) + run)
            if masked:
                a = jnp.where(causal, a, 0.0)
            de = a * _dot_t(do, vb)
            dv_ref[pl.ds(off, blk), :] += _tdot(a.astype(BF16), do)
            de_s[j] = de
            sg_s[j] = sig
            return run + jnp.sum(l, axis=-1, keepdims=True)

        run = sweep1(i, jnp.zeros((blk, 1), F32), True)
        lax.fori_loop(0, i, lambda jj, r: sweep1(i - 1 - jj, r, False), run)

        def sweep2(j, carry, masked):
            left, dq = carry
            off = pl.multiple_of(j * blk, blk)
            kb = k_ref[pl.ds(off, blk), :]
            de = de_s[j]
            sig = sg_s[j]
            dz = de * (1.0 - sig) - (_split_dot(de, before) + left) * sig
            if masked:
                dz = jnp.where(causal, dz, 0.0)
            dzb = (dz * scale).astype(BF16)
            dq = dq + _dot(dzb, kb)
            dk_ref[pl.ds(off, blk), :] += _tdot(dzb, q)
            return left + jnp.sum(de, axis=-1, keepdims=True), dq

        carry = (jnp.zeros((blk, 1), F32), jnp.zeros((blk, HEAD_DIM), F32))
        carry = lax.fori_loop(0, i, lambda j, c: sweep2(j, c, False), carry)
        carry = sweep2(i, carry, True)
        dq_ref[...] = carry[1]

    out = jax.ShapeDtypeStruct((seq, SB_W), F32)
    return pl.pallas_call(
        body, out_shape=[out, out, out], grid=(N_SB_HEADS, nq),
        in_specs=[pl.BlockSpec((blk, HEAD_DIM), lambda h, i: (i, h)),
                  pl.BlockSpec((seq, HEAD_DIM), lambda h, i: (0, N_SB_HEADS + h)),
                  pl.BlockSpec((seq, HEAD_DIM), lambda h, i: (0, 2 * N_SB_HEADS + h)),
                  pl.BlockSpec((blk, HEAD_DIM), lambda h, i: (i, h))],
        out_specs=[pl.BlockSpec((blk, HEAD_DIM), lambda h, i: (i, h)),
                   pl.BlockSpec((seq, HEAD_DIM), lambda h, i: (0, h)),
                   pl.BlockSpec((seq, HEAD_DIM), lambda h, i: (0, h))],
        scratch_shapes=[pltpu.VMEM((nq, blk, blk), F32), pltpu.VMEM((nq, blk, blk), F32)],
        name=name, compiler_params=_params("parallel", "arbitrary"),
    )(qkv, qkv, qkv, dout)


MLA_SCALE = (HEAD_DIM + ROPE_DIM) ** -0.5


def _mla_fwd(q_cat, k_cat, v, *, name):
    seq = q_cat.shape[0]
    blk = min(ATT_BLOCK, seq)
    nq = seq // blk

    def body(q_ref, k_ref, v_ref, o_ref, lse_ref):
        i = pl.program_id(1)
        q = q_ref[...]
        row = lax.broadcasted_iota(jnp.int32, (blk, blk), 0)
        col = lax.broadcasted_iota(jnp.int32, (blk, blk), 1)
        causal = col <= row

        def step(j, carry, masked):
            m, l, acc = carry
            off = pl.multiple_of(j * blk, blk)
            s = _dot_t(q, k_ref[pl.ds(off, blk), :]) * MLA_SCALE
            if masked:
                s = jnp.where(causal, s, -jnp.inf)
            m_new = jnp.maximum(m, jnp.max(s, axis=-1, keepdims=True))
            p = jnp.exp(s - m_new)
            alpha = jnp.exp(m - m_new)
            l = alpha * l + jnp.sum(p, axis=-1, keepdims=True)
            acc = alpha * acc + _dot(p.astype(BF16), v_ref[pl.ds(off, blk), :])
            return m_new, l, acc

        carry = (jnp.full((blk, 1), -jnp.inf, F32), jnp.zeros((blk, 1), F32),
                 jnp.zeros((blk, HEAD_DIM), F32))
        carry = step(i, carry, True)
        m, l, acc = lax.fori_loop(0, i, lambda j, c: step(j, c, False), carry)
        o_ref[...] = acc / l
        lse_ref[...] = jnp.broadcast_to(m + jnp.log(l), (blk, HEAD_DIM))

    out = jax.ShapeDtypeStruct((seq, MLA_W), F32)
    return pl.pallas_call(
        body, out_shape=[out, out], grid=(N_MLA_HEADS, nq),
        in_specs=[pl.BlockSpec((blk, CAT_W), lambda h, i: (i, h)),
                  pl.BlockSpec((seq, CAT_W), lambda h, i: (0, h)),
                  pl.BlockSpec((seq, HEAD_DIM), lambda h, i: (0, h))],
        out_specs=[pl.BlockSpec((blk, HEAD_DIM), lambda h, i: (i, h)),
                   pl.BlockSpec((blk, HEAD_DIM), lambda h, i: (i, h))],
        name=name, compiler_params=_params("parallel", "arbitrary"),
    )(q_cat, k_cat, v)


def _mla_bwd(q_cat, k_cat, v, out, lse, dout, *, name):
    seq = q_cat.shape[0]
    blk = min(ATT_BLOCK, seq)
    nq = seq // blk

    def body(q_ref, k_ref, v_ref, o_ref, lse_ref, do_ref, dq_ref, dk_ref, dv_ref):
        i = pl.program_id(1)

        @pl.when(i == 0)
        def _():
            dk_ref[...] = jnp.zeros_like(dk_ref)
            dv_ref[...] = jnp.zeros_like(dv_ref)

        q = q_ref[...]
        do = do_ref[...]
        dob = do.astype(BF16)
        delta = jnp.sum(do * o_ref[...], axis=-1, keepdims=True)
        lse = lse_ref[:, :1]
        row = lax.broadcasted_iota(jnp.int32, (blk, blk), 0)
        col = lax.broadcasted_iota(jnp.int32, (blk, blk), 1)
        causal = col <= row

        def step(j, dq, masked):
            off = pl.multiple_of(j * blk, blk)
            kb = k_ref[pl.ds(off, blk), :]
            vb = v_ref[pl.ds(off, blk), :]
            p = jnp.exp(_dot_t(q, kb) * MLA_SCALE - lse)
            if masked:
                p = jnp.where(causal, p, 0.0)
            ds = (p * (_dot_t(dob, vb) - delta) * MLA_SCALE).astype(BF16)
            dv_ref[pl.ds(off, blk), :] += _tdot(p.astype(BF16), dob)
            dk_ref[pl.ds(off, blk), :] += _tdot(ds, q)
            return dq + _dot(ds, kb)

        dq = lax.fori_loop(0, i, lambda j, c: step(j, c, False), jnp.zeros((blk, CAT_W), F32))
        dq_ref[...] = step(i, dq, True)

    return pl.pallas_call(
        body,
        out_shape=[jax.ShapeDtypeStruct((seq, N_MLA_HEADS * CAT_W), F32),
                   jax.ShapeDtypeStruct((seq, N_MLA_HEADS * CAT_W), F32),
                   jax.ShapeDtypeStruct((seq, MLA_W), F32)],
        grid=(N_MLA_HEADS, nq),
        in_specs=[pl.BlockSpec((blk, CAT_W), lambda h, i: (i, h)),
                  pl.BlockSpec((seq, CAT_W), lambda h, i: (0, h)),
                  pl.BlockSpec((seq, HEAD_DIM), lambda h, i: (0, h)),
                  pl.BlockSpec((blk, HEAD_DIM), lambda h, i: (i, h)),
                  pl.BlockSpec((blk, HEAD_DIM), lambda h, i: (i, h)),
                  pl.BlockSpec((blk, HEAD_DIM), lambda h, i: (i, h))],
        out_specs=[pl.BlockSpec((blk, CAT_W), lambda h, i: (i, h)),
                   pl.BlockSpec((seq, CAT_W), lambda h, i: (0, h)),
                   pl.BlockSpec((seq, HEAD_DIM), lambda h, i: (0, h))],
        name=name, compiler_params=_params("parallel", "arbitrary"),
    )(q_cat, k_cat, v, out, lse, dout)


def _local_step(x, mem, positions, target, w, g):
    seq = x.shape[0]
    inv_freq = jnp.power(ROPE_THETA, -jnp.arange(0, ROPE_DIM, 2, dtype=F32) / ROPE_DIM)
    ang = positions.astype(F32)[:, None] * inv_freq
    cos, sin = jnp.cos(ang), jnp.sin(ang)
    lane_pad = jnp.zeros((seq, HEAD_DIM - ROPE_DIM), F32)
    cos_t = jnp.concatenate([cos, cos, lane_pad], axis=1)
    sin_t = jnp.concatenate([-sin, sin, lane_pad], axis=1)
    gain_pad = jnp.zeros((1, HEAD_DIM - ROPE_DIM), F32)
    g_k_rope = jnp.concatenate([g["g_k_rope"], gain_pad], axis=1)
    g_q_rope = jnp.concatenate([g["b_g_q_rope"], gain_pad], axis=1)

    def norm_to_bf16(src, gain, name):
        def body(ins, outs, _):
            outs[0][...] = _rms(ins[0][...], ins[1][...]).astype(BF16)
        return _rowwise(body, seq, [(src, True), (gain, False)], [(src.shape[1], BF16)], name=name)[0]

    h_a = norm_to_bf16(x, g["a_norm"], "a_norm_fwd")
    qkv = _mm(h_a, w["a_in_qkv"], out_dtype=BF16, name="a_in_qkv")
    gr = _mm(h_a, w["a_in_gate"], name="a_in_gate")
    sb = _sb_fwd(qkv, name="sb_fwd")
    mem0 = _mem_side_fwd(mem, g["mem_norm"][0:1], w["mem_kv"][0], g["g_mem_k"][0:1], tag="a")
    mixed_a = _mix_fwd(sb, gr, 0, mem0[2], mem0[3], g["g_mem_q"][0:1], name="a_mix_fwd")
    x1 = _mm(mixed_a, w["a_out"], res=x, name="a_out")

    def norms2_body(ins, outs, _):
        xv = ins[0][...]
        outs[0][...] = _rms(xv, ins[1][...]).astype(BF16)
        outs[1][...] = _rms(xv, ins[2][...]).astype(BF16)

    h_kv, h_b = _rowwise(norms2_body, seq, [(x1, True), (g["kv_norm"], False), (g["b_norm"], False)],
                         [(D_MODEL, BF16), (D_MODEL, BF16)], name="kv_b_norm_fwd")
    ckr = _mm(h_kv, w["dkv"], name="dkv")

    def ckr_body(ins, outs, _):
        ckr_ref, gc_ref, gr_ref, c_ref, s_ref = ins
        outs[0][...] = _rms(ckr_ref[:, :KV_LORA], gc_ref[...]).astype(BF16)
        kr = _rms(ckr_ref[:, KV_LORA:], gr_ref[...], n=ROPE_DIM)
        outs[1][...] = _rope(kr, c_ref[...], s_ref[...]).astype(BF16)

    c_n, k_r = _rowwise(ckr_body, seq,
                        [(ckr, True), (g["g_ckv"], False), (g_k_rope, False), (cos_t, True), (sin_t, True)],
                        [(KV_LORA, BF16), (HEAD_DIM, BF16)], name="ckv_prep_fwd")
    kv = _mm(c_n, w["ukv"], name="ukv")

    def kcat_body(ins, outs, _):
        kv_ref, kr_ref, gk_ref = ins
        kc_ref, v_ref = outs
        for h in range(N_MLA_HEADS):
            kc_ref[:, h * CAT_W:h * CAT_W + HEAD_DIM] = _rms(
                kv_ref[:, h * CAT_W:h * CAT_W + HEAD_DIM], gk_ref[...]).astype(BF16)
            kc_ref[:, h * CAT_W + HEAD_DIM:(h + 1) * CAT_W] = kr_ref[...]
            v_ref[:, _hs(h)] = kv_ref[:, h * CAT_W + HEAD_DIM:(h + 1) * CAT_W].astype(BF16)

    k_cat, v_mla = _rowwise(kcat_body, seq, [(kv, True), (k_r, True), (g["g_k_nope"], False)],
                            [(N_MLA_HEADS * CAT_W, BF16), (MLA_W, BF16)], name="k_prep_fwd")

    p2 = _mm(h_b, w["b_in"], name="b_in")

    def qlat_body(ins, outs, _):
        outs[0][...] = _rms(ins[0][:, :Q_LORA], ins[1][...]).astype(BF16)

    (q_l,) = _rowwise(qlat_body, seq, [(p2, True), (g["b_g_q_lat"], False)], [(Q_LORA, BF16)],
                      name="q_lat_norm_fwd")
    q_up = _mm(q_l, w["uq"], name="uq")

    def qcat_body(ins, outs, _):
        q_ref, gn_ref, gr_ref, c_ref, s_ref = ins
        (o_ref,) = outs
        for h in range(N_MLA_HEADS):
            o_ref[:, h * CAT_W:h * CAT_W + HEAD_DIM] = _rms(
                q_ref[:, h * CAT_W:h * CAT_W + HEAD_DIM], gn_ref[...]).astype(BF16)
            qr = _rms(q_ref[:, h * CAT_W + HEAD_DIM:(h + 1) * CAT_W], gr_ref[...], n=ROPE_DIM)
            o_ref[:, h * CAT_W + HEAD_DIM:(h + 1) * CAT_W] = _rope(qr, c_ref[...], s_ref[...]).astype(BF16)

    (q_cat,) = _rowwise(qcat_body, seq,
                        [(q_up, True), (g["b_g_q_nope"], False), (g_q_rope, False), (cos_t, True), (sin_t, True)],
                        [(N_MLA_HEADS * CAT_W, BF16)], name="q_prep_fwd")
    att, lse = _mla_fwd(q_cat, k_cat, v_mla, name="mla_fwd")
    mem1 = _mem_side_fwd(mem, g["mem_norm"][1:2], w["mem_kv"][1], g["g_mem_k"][1:2], tag="b")
    mixed_b = _mix_fwd(att, p2, Q_LORA, mem1[2], mem1[3], g["g_mem_q"][1:2], name="b_mix_fwd")
    y = _mm(mixed_b, w["b_out"], res=x1, name="b_out")

    def loss_body(ins, outs, accs):
        diff = ins[0][...] - ins[1][...]
        outs[0][...] = diff / D_MODEL
        col = jnp.sum(diff * diff, axis=0, keepdims=True)
        part = col[:, :HEAD_DIM]
        for c in range(1, D_MODEL // HEAD_DIM):
            part = part + col[:, _hs(c)]
        accs[0][...] += part * (0.5 / D_MODEL)

    dy, loss_part = _rowwise(loss_body, seq, [(y, True), (target, True)], [(D_MODEL, F32)],
                             [((1, HEAD_DIM), F32)], name="loss")

    gw, gg = {}, {}
    dmixed_b = _mm(dy, w["b_out"], tb=True, name="b_out_dx")
    gw["b_out"] = _mm(mixed_b, dy, ta=True, name="b_out_dw")
    datt, dgate_b, dmk1, dmv1, gq1 = _mix_bwd(dmixed_b, att, p2, Q_LORA, mem1[2], mem1[3],
                                              g["g_mem_q"][1:2], name="b_mix_bwd")
    dq_cat, dk_cat, dv_mla = _mla_bwd(q_cat, k_cat, v_mla, att, lse, datt, name="mla_bwd")

    def qcat_bwd_body(ins, outs, accs):
        q_ref, dq_ref, gn_ref, gr_ref, c_ref, s_ref = ins
        (o_ref,) = outs
        dgn_ref, dgr_ref = accs
        for h in range(N_MLA_HEADS):
            dx, dg = _rms_bwd(q_ref[:, h * CAT_W:h * CAT_W + HEAD_DIM], gn_ref[...],
                              dq_ref[:, h * CAT_W:h * CAT_W + HEAD_DIM])
            o_ref[:, h * CAT_W:h * CAT_W + HEAD_DIM] = dx.astype(BF16)
            dgn_ref[...] += dg
            dn = _rope_bwd(dq_ref[:, h * CAT_W + HEAD_DIM:(h + 1) * CAT_W], c_ref[...], s_ref[...])
            dx, dg = _rms_bwd(q_ref[:, h * CAT_W + HEAD_DIM:(h + 1) * CAT_W], gr_ref[...], dn, n=ROPE_DIM)
            o_ref[:, h * CAT_W + HEAD_DIM:(h + 1) * CAT_W] = dx.astype(BF16)
            dgr_ref[...] += dg

    dq_up, gg["b_g_q_nope"], dgqr = _rowwise(
        qcat_bwd_body, seq,
        [(q_up, True), (dq_cat, True), (g["b_g_q_nope"], False), (g_q_rope, False), (cos_t, True), (sin_t, True)],
        [(N_MLA_HEADS * CAT_W, BF16)], [((1, HEAD_DIM), F32), ((1, HEAD_DIM), F32)], name="q_prep_bwd")
    gg["b_g_q_rope"] = dgqr
    dq_l = _mm(dq_up, w["uq"], tb=True, name="uq_dx")
    gw["uq"] = _mm(q_l, dq_up, ta=True, n_split=N_CHIPS, name="uq_dw")

    def qlat_bwd_body(ins, outs, accs):
        p2_ref, dql_ref, dgate_ref, gl_ref = ins
        dx, dg = _rms_bwd(p2_ref[:, :Q_LORA], gl_ref[...], dql_ref[...])
        outs[0][:, :Q_LORA] = dx.astype(BF16)
        outs[0][:, Q_LORA:] = dgate_ref[...]
        accs[0][...] += dg

    dp2, gg["b_g_q_lat"] = _rowwise(
        qlat_bwd_body, seq, [(p2, True), (dq_l, True), (dgate_b, True), (g["b_g_q_lat"], False)],
        [(Q_LORA + GATE_W, BF16)], [((1, Q_LORA), F32)], name="q_lat_norm_bwd")
    dh_b = _mm(dp2, w["b_in"], tb=True, name="b_in_dx")
    gw["b_in"] = _mm(h_b, dp2, ta=True, n_split=N_CHIPS, name="b_in_dw")

    def kcat_bwd_body(ins, outs, accs):
        kv_ref, dkc_ref, dv_ref, gk_ref = ins
        dkv_ref, dkr_ref = outs
        (dgk_ref,) = accs
        dkr = jnp.zeros(dkr_ref.shape, F32)
        for h in range(N_MLA_HEADS):
            dx, dg = _rms_bwd(kv_ref[:, h * CAT_W:h * CAT_W + HEAD_DIM], gk_ref[...],
                              dkc_ref[:, h * CAT_W:h * CAT_W + HEAD_DIM])
            dkv_ref[:, h * CAT_W:h * CAT_W + HEAD_DIM] = dx.astype(BF16)
            dgk_ref[...] += dg
            dkv_ref[:, h * CAT_W + HEAD_DIM:(h + 1) * CAT_W] = dv_ref[:, _hs(h)].astype(BF16)
            dkr = dkr + dkc_ref[:, h * CAT_W + HEAD_DIM:(h + 1) * CAT_W]
        dkr_ref[...] = dkr

    dkv, dk_r, gg["g_k_nope"] = _rowwise(
        kcat_bwd_body, seq, [(kv, True), (dk_cat, True), (dv_mla, True), (g["g_k_nope"], False)],
        [(N_MLA_HEADS * CAT_W, BF16), (HEAD_DIM, F32)], [((1, HEAD_DIM), F32)], name="k_prep_bwd")
    dc_n = _mm(dkv, w["ukv"], tb=True, name="ukv_dx")
    gw["ukv"] = _mm(c_n, dkv, ta=True, n_split=N_CHIPS, name="ukv_dw")

    def ckr_bwd_body(ins, outs, accs):
        ckr_ref, dcn_ref, dkr_ref, gc_ref, gr_ref, c_ref, s_ref = ins
        dx, dg = _rms_bwd(ckr_ref[:, :KV_LORA], gc_ref[...], dcn_ref[...])
        outs[0][:, :KV_LORA] = dx.astype(BF16)
        accs[0][...] += dg
        dn = _rope_bwd(dkr_ref[...], c_ref[...], s_ref[...])
        dx, dg = _rms_bwd(ckr_ref[:, KV_LORA:], gr_ref[...], dn, n=ROPE_DIM)
        outs[0][:, KV_LORA:] = dx.astype(BF16)
        accs[1][...] += dg

    dckr, gg["g_ckv"], gg["g_k_rope"] = _rowwise(
        ckr_bwd_body, seq,
        [(ckr, True), (dc_n, True), (dk_r, True), (g["g_ckv"], False), (g_k_rope, False),
         (cos_t, True), (sin_t, True)],
        [(KV_LORA + HEAD_DIM, BF16)], [((1, KV_LORA), F32), ((1, HEAD_DIM), F32)], name="ckv_prep_bwd")
    dh_kv = _mm(dckr, w["dkv"], tb=True, name="dkv_dx")
    gw["dkv"] = _mm(h_kv, dckr, ta=True, name="dkv_dw")

    def norms2_bwd_body(ins, outs, accs):
        x_ref, dy_ref, dhk_ref, dhb_ref, gk_ref, gb_ref = ins
        xv = x_ref[...]
        dxk, dgk = _rms_bwd(xv, gk_ref[...], dhk_ref[...])
        dxb, dgb = _rms_bwd(xv, gb_ref[...], dhb_ref[...])
        outs[0][...] = dy_ref[...] + dxk + dxb
        accs[0][...] += dgk
        accs[1][...] += dgb

    dx1, gg["kv_norm"], gg["b_norm"] = _rowwise(
        norms2_bwd_body, seq,
        [(x1, True), (dy, True), (dh_kv, True), (dh_b, True), (g["kv_norm"], False), (g["b_norm"], False)],
        [(D_MODEL, F32)], [((1, D_MODEL), F32), ((1, D_MODEL), F32)], name="kv_b_norm_bwd")

    dmixed_a = _mm(dx1, w["a_out"], tb=True, name="a_out_dx")
    gw["a_out"] = _mm(mixed_a, dx1, ta=True, name="a_out_dw")
    dsb, dgate_a, dmk0, dmv0, gq0 = _mix_bwd(dmixed_a, sb, gr, 0, mem0[2], mem0[3],
                                             g["g_mem_q"][0:1], name="a_mix_bwd")
    dq, dk, dv = _sb_bwd(qkv, dsb, name="sb_bwd")
    dp_a = jnp.concatenate([dq.astype(BF16), dk.astype(BF16), dv.astype(BF16), dgate_a], axis=1)
    dh_a = _mm(dp_a, w["a_in"], tb=True, name="a_in_dx")
    gw["a_in"] = _mm(h_a, dp_a, ta=True, n_split=N_CHIPS, name="a_in_dw")

    def norm_a_bwd_body(ins, outs, accs):
        dx, dg = _rms_bwd(ins[0][...], ins[3][...], ins[2][...])
        outs[0][...] = ins[1][...] + dx
        accs[0][...] += dg

    grad_x, gg["a_norm"] = _rowwise(
        norm_a_bwd_body, seq, [(x, True), (dx1, True), (dh_a, True), (g["a_norm"], False)],
        [(D_MODEL, F32)], [((1, D_MODEL), F32)], name="a_norm_bwd")

    dw0, dgn0, dgk0 = _mem_side_bwd(mem, g["mem_norm"][0:1], w["mem_kv"][0], g["g_mem_k"][0:1],
                                    mem0[0], mem0[1], dmk0, dmv0, tag="a")
    dw1, dgn1, dgk1 = _mem_side_bwd(mem, g["mem_norm"][1:2], w["mem_kv"][1], g["g_mem_k"][1:2],
                                    mem1[0], mem1[1], dmk1, dmv1, tag="b")
    gw["mem_kv"] = (dw0, dw1)
    gg["mem_norm"] = jnp.concatenate([dgn0, dgn1], axis=0)
    gg["g_mem_q"] = jnp.concatenate([gq0, gq1], axis=0)
    gg["g_mem_k"] = jnp.concatenate([dgk0, dgk1], axis=0)
    return loss_part, grad_x, gw, gg


HBM_SPEC = pl.BlockSpec(memory_space=pl.ANY)


def _other_chips():
    x, y = lax.axis_index("x"), lax.axis_index("y")
    return [(1 - x, y), (x, 1 - y), (1 - x, 1 - y)]


def _allgather_chips(shards):
    n = len(shards)

    def body(*refs):
        ins, outs = refs[:n], refs[n:2 * n]
        send, recv, loc = refs[2 * n:]
        c = lax.axis_index("c")
        me = 2 * lax.axis_index("x") + lax.axis_index("y")
        copies = []
        for wi in range(n):
            own = pltpu.make_async_copy(ins[wi], outs[wi].at[me], loc.at[wi])
            own.start()
            copies.append(own)
            for k, (tx, ty) in enumerate(_other_chips()):
                cp = pltpu.make_async_remote_copy(
                    src_ref=ins[wi], dst_ref=outs[wi].at[me], send_sem=send.at[wi, k],
                    recv_sem=recv.at[wi, k], device_id=(tx, ty, c), device_id_type=MESH)
                cp.start()
                copies.append(cp)
        for cp in copies:
            cp.wait()

    return pl.pallas_call(
        body, out_shape=[jax.ShapeDtypeStruct((N_CHIPS,) + s.shape, s.dtype) for s in shards],
        in_specs=[HBM_SPEC] * n, out_specs=[HBM_SPEC] * n,
        scratch_shapes=[pltpu.SemaphoreType.DMA((n, 3)), pltpu.SemaphoreType.DMA((n, 3)),
                        pltpu.SemaphoreType.DMA((n,))],
        name="allgather_weights",
    )(*shards)


def _scatter_to_chips(grads):
    n = len(grads)

    def body(*refs):
        ins, outs = refs[:n], refs[n:2 * n]
        send, recv, loc = refs[2 * n:]
        c = lax.axis_index("c")
        me = 2 * lax.axis_index("x") + lax.axis_index("y")
        copies = []
        for wi in range(n):
            own = pltpu.make_async_copy(ins[wi].at[me], outs[wi].at[3], loc.at[wi])
            own.start()
            copies.append(own)
            for k, (tx, ty) in enumerate(_other_chips()):
                cp = pltpu.make_async_remote_copy(
                    src_ref=ins[wi].at[2 * tx + ty], dst_ref=outs[wi].at[k], send_sem=send.at[wi, k],
                    recv_sem=recv.at[wi, k], device_id=(tx, ty, c), device_id_type=MESH)
                cp.start()
                copies.append(cp)
        for cp in copies:
            cp.wait()

    return pl.pallas_call(
        body, out_shape=[jax.ShapeDtypeStruct(s.shape, s.dtype) for s in grads],
        in_specs=[HBM_SPEC] * n, out_specs=[HBM_SPEC] * n,
        scratch_shapes=[pltpu.SemaphoreType.DMA((n, 3)), pltpu.SemaphoreType.DMA((n, 3)),
                        pltpu.SemaphoreType.DMA((n,))],
        name="scatter_grads",
    )(*grads)


def _swap_with_sibling(parts):
    n = len(parts)

    def body(*refs):
        ins, outs = refs[:n], refs[n:2 * n]
        send, recv = refs[2 * n:]
        sib = (lax.axis_index("x"), lax.axis_index("y"), 1 - lax.axis_index("c"))
        copies = []
        for wi in range(n):
            cp = pltpu.make_async_remote_copy(
                src_ref=ins[wi], dst_ref=outs[wi], send_sem=send.at[wi], recv_sem=recv.at[wi],
                device_id=sib, device_id_type=MESH)
            cp.start()
            copies.append(cp)
        for cp in copies:
            cp.wait()

    return pl.pallas_call(
        body, out_shape=[jax.ShapeDtypeStruct(s.shape, s.dtype) for s in parts],
        in_specs=[HBM_SPEC] * n, out_specs=[HBM_SPEC] * n,
        scratch_shapes=[pltpu.SemaphoreType.DMA((n,)), pltpu.SemaphoreType.DMA((n,))],
        name="swap_partial_grads",
    )(*parts)


def _allreduce_small(vec, loss_row):
    rows = vec.shape[0]

    def body(v_ref, o_ref, buf, send, recv):
        x, y, c = lax.axis_index("x"), lax.axis_index("y"), lax.axis_index("c")
        me = 4 * x + 2 * y + c
        buf[me] = v_ref[...]
        copies = []
        for r in range(1, N_DEV):
            peer = (x ^ ((r >> 2) & 1), y ^ ((r >> 1) & 1), c ^ (r & 1))
            cp = pltpu.make_async_remote_copy(
                src_ref=v_ref, dst_ref=buf.at[me], send_sem=send.at[r - 1], recv_sem=recv.at[r - 1],
                device_id=peer, device_id_type=MESH)
            cp.start()
            copies.append(cp)
        for cp in copies:
            cp.wait()
        total = buf[0]
        for d in range(1, N_DEV):
            total = total + buf[d]
        o_ref[...] = total
        o_ref[loss_row:loss_row + 1, :] = jnp.broadcast_to(
            jnp.sum(total[loss_row:loss_row + 1, :], axis=-1, keepdims=True), (1, HEAD_DIM))

    return pl.pallas_call(
        body, out_shape=jax.ShapeDtypeStruct(vec.shape, F32),
        in_specs=[pl.BlockSpec(memory_space=pltpu.VMEM)], out_specs=pl.BlockSpec(memory_space=pltpu.VMEM),
        scratch_shapes=[pltpu.VMEM((N_DEV, rows, HEAD_DIM), F32),
                        pltpu.SemaphoreType.DMA((N_DEV - 1,)), pltpu.SemaphoreType.DMA((N_DEV - 1,))],
        name="allreduce_gains",
    )(vec)


def _sum_slots(r, *, name):
    _, rows, width = r.shape
    blk = _pick(rows, (256, 128, 64, 32, 16, 8))

    def body(r_ref, o_ref):
        o_ref[...] = ((r_ref[3] + r_ref[0]) + r_ref[1]) + r_ref[2]

    return pl.pallas_call(
        body, out_shape=jax.ShapeDtypeStruct((rows, width), F32), grid=(rows // blk,),
        in_specs=[pl.BlockSpec((N_CHIPS, blk, width), lambda i: (0, i, 0))],
        out_specs=pl.BlockSpec((blk, width), lambda i: (i, 0)),
        name=name, compiler_params=_params("parallel"),
    )(r)


def _adamw(wgt, grads, m, v, *, name):
    rows, width = wgt.shape
    blk = _pick(rows, (256, 128, 64, 32, 16, 8))
    n_g = len(grads)

    def body(*refs):
        w_ref, m_ref, v_ref = refs[0], refs[1 + n_g], refs[2 + n_g]
        g_out, d_out, m_out, v_out = refs[3 + n_g:]
        grad = refs[1][...]
        for t in range(1, n_g):
            grad = grad + refs[1 + t][...]
        m_new = ADAM_B1 * m_ref[...] + (1.0 - ADAM_B1) * grad
        v_new = ADAM_B2 * v_ref[...] + (1.0 - ADAM_B2) * (grad * grad)
        m_hat = m_new / (1.0 - ADAM_B1 ** ADAM_STEP)
        v_hat = v_new / (1.0 - ADAM_B2 ** ADAM_STEP)
        g_out[...] = grad
        d_out[...] = -ADAM_LR * (m_hat / (jnp.sqrt(v_hat) + ADAM_EPS) + ADAM_WD * w_ref[...])
        m_out[...] = m_new
        v_out[...] = v_new

    spec = pl.BlockSpec((blk, width), lambda i: (i, 0))
    out = jax.ShapeDtypeStruct((rows, width), F32)
    return pl.pallas_call(
        body, out_shape=[out] * 4, grid=(rows // blk,), in_specs=[spec] * (3 + n_g),
        out_specs=[spec] * 4, name=name, compiler_params=_params("parallel"),
    )(wgt, *grads, m, v)


_SMALL = (("a_norm", 2048), ("kv_norm", 2048), ("g_ckv", 512), ("g_k_nope", 128), ("g_k_rope", 64),
          ("b_norm", 2048), ("b_g_q_lat", 512), ("b_g_q_nope", 128), ("b_g_q_rope", 64),
          ("mem_norm", 4096), ("g_mem_q", 256), ("g_mem_k", 256))


def _lanes(n):
    return -(-n // HEAD_DIM) * HEAD_DIM


def _pack_rows(pieces, pad_rows_to=8):
    flat = jnp.concatenate(pieces, axis=1)
    rows = flat.shape[1] // HEAD_DIM
    pad = (-rows) % pad_rows_to
    if pad:
        flat = jnp.concatenate([flat, jnp.zeros((1, pad * HEAD_DIM), F32)], axis=1)
    return flat.reshape(rows + pad, HEAD_DIM)


def _pad_lanes(a):
    a = a.reshape(1, -1)
    pad = _lanes(a.shape[1]) - a.shape[1]
    if pad:
        a = jnp.concatenate([a, jnp.zeros((1, pad), F32)], axis=1)
    return a


def kernel(x, mem, positions, a_norm, a_w_in, a_w_out, kv_norm, w_dkv, g_ckv, w_ukv, g_k_nope, g_k_rope, b_norm, b_w_in, b_g_q_lat, b_w_uq, b_g_q_nope, b_g_q_rope, b_w_out, mem_norm, w_mem_kv, g_mem_q, g_mem_k, loss_target, m_a_norm, m_a_w_in, m_a_w_out, m_kv_norm, m_w_dkv, m_g_ckv, m_w_ukv, m_g_k_nope, m_g_k_rope, m_b_norm, m_b_w_in, m_b_g_q_lat, m_b_w_uq, m_b_g_q_nope, m_b_g_q_rope, m_b_w_out, m_mem_norm, m_w_mem_kv, m_g_mem_q, m_g_mem_k, v_a_norm, v_a_w_in, v_a_w_out, v_kv_norm, v_w_dkv, v_g_ckv, v_w_ukv, v_g_k_nope, v_g_k_rope, v_b_norm, v_b_w_in, v_b_g_q_lat, v_b_w_uq, v_b_g_q_nope, v_b_g_q_rope, v_b_w_out, v_mem_norm, v_w_mem_kv, v_g_mem_q, v_g_mem_k):
    chip = 2 * lax.axis_index("x") + lax.axis_index("y")
    rows_dkv = D_MODEL // N_CHIPS
    heads_per_chip = N_MLA_HEADS // N_CHIPS
    qk_w = HEAD_DIM + ROPE_DIM

    big = {"a_in": a_w_in[0], "a_out": a_w_out[0], "dkv": w_dkv, "ukv": w_ukv, "b_in": b_w_in[0],
           "uq": b_w_uq[0], "b_out": b_w_out[0], "mem_kv": w_mem_kv.reshape(2 * rows_dkv, 2 * MEM_W)}
    big_m = {"a_in": m_a_w_in[0], "a_out": m_a_w_out[0], "dkv": m_w_dkv, "ukv": m_w_ukv, "b_in": m_b_w_in[0],
             "uq": m_b_w_uq[0], "b_out": m_b_w_out[0], "mem_kv": m_w_mem_kv.reshape(2 * rows_dkv, 2 * MEM_W)}
    big_v = {"a_in": v_a_w_in[0], "a_out": v_a_w_out[0], "dkv": v_w_dkv, "ukv": v_w_ukv, "b_in": v_b_w_in[0],
             "uq": v_b_w_uq[0], "b_out": v_b_w_out[0], "mem_kv": v_w_mem_kv.reshape(2 * rows_dkv, 2 * MEM_W)}
    names = list(big)
    gathered = _allgather_chips([big[n].astype(BF16) for n in names] + [a_norm])
    st = dict(zip(names, gathered[:-1]))
    a_in_full = st["a_in"].transpose(1, 0, 2).reshape(D_MODEL, QKV_W + GATE_W)
    uq = st["uq"].reshape(N_CHIPS, Q_LORA, heads_per_chip, qk_w)
    uq = jnp.pad(uq, ((0, 0), (0, 0), (0, 0), (0, CAT_W - qk_w)))
    w = {
        "a_in": a_in_full,
        "a_in_qkv": a_in_full[:, :QKV_W],
        "a_in_gate": a_in_full[:, QKV_W:],
        "a_out": st["a_out"].reshape(D_MODEL, D_MODEL),
        "dkv": jnp.pad(st["dkv"].reshape(D_MODEL, KV_LORA + ROPE_DIM), ((0, 0), (0, HEAD_DIM - ROPE_DIM))),
        "ukv": st["ukv"].transpose(1, 0, 2).reshape(KV_LORA, N_MLA_HEADS * CAT_W),
        "b_in": st["b_in"].transpose(1, 0, 2).reshape(D_MODEL, Q_LORA + GATE_W),
        "uq": uq.transpose(1, 0, 2, 3).reshape(Q_LORA, N_MLA_HEADS * CAT_W),
        "b_out": st["b_out"].reshape(D_MODEL, D_MODEL),
        "mem_kv": st["mem_kv"].reshape(N_CHIPS, 2, rows_dkv, 2 * MEM_W).transpose(1, 0, 2, 3).reshape(
            2, D_MODEL, 2 * MEM_W),
    }
    gains = {
        "a_norm": gathered[-1].reshape(1, D_MODEL), "kv_norm": kv_norm.reshape(1, -1),
        "g_ckv": g_ckv.reshape(1, -1), "g_k_nope": g_k_nope.reshape(1, -1), "g_k_rope": g_k_rope.reshape(1, -1),
        "b_norm": b_norm, "b_g_q_lat": b_g_q_lat, "b_g_q_nope": b_g_q_nope, "b_g_q_rope": b_g_q_rope,
        "mem_norm": mem_norm, "g_mem_q": g_mem_q, "g_mem_k": g_mem_k,
    }

    loss_part, grad_x, gw, gg = _local_step(x[0], mem[0], positions[0], loss_target[0], w, gains)

    stacked = {
        "a_in": gw["a_in"],
        "a_out": gw["a_out"].reshape(N_CHIPS, rows_dkv, D_MODEL),
        "dkv": gw["dkv"][:, :KV_LORA + ROPE_DIM].reshape(N_CHIPS, rows_dkv, KV_LORA + ROPE_DIM),
        "ukv": gw["ukv"],
        "b_in": gw["b_in"],
        "uq": gw["uq"].reshape(N_CHIPS, Q_LORA, heads_per_chip, CAT_W)[..., :qk_w].reshape(
            N_CHIPS, Q_LORA, heads_per_chip * qk_w),
        "b_out": gw["b_out"].reshape(N_CHIPS, rows_dkv, D_MODEL),
        "mem_kv": jnp.stack([gw["mem_kv"][0].reshape(N_CHIPS, rows_dkv, 2 * MEM_W),
                             gw["mem_kv"][1].reshape(N_CHIPS, rows_dkv, 2 * MEM_W)], axis=1).reshape(
            N_CHIPS, 2 * rows_dkv, 2 * MEM_W),
    }
    received = _scatter_to_chips([stacked[n] for n in names])
    partial = [_sum_slots(r, name=f"sum_slots_{n}") for n, r in zip(names, received)]
    sibling = _swap_with_sibling(partial)
    big_out = {}
    for n, mine, theirs in zip(names, partial, sibling):
        big_out[n] = _adamw(big[n], [mine, theirs], big_m[n], big_v[n], name=f"adamw_{n}")

    pieces = [_pad_lanes(gg[n]) if n not in ("g_k_rope", "b_g_q_rope") else gg[n] for n, _ in _SMALL]
    pieces.append(loss_part)
    loss_row = sum(_lanes(size) for _, size in _SMALL) // HEAD_DIM
    summed = _allreduce_small(_pack_rows(pieces), loss_row)
    flat = summed.reshape(1, -1)
    small_g, off = {}, 0
    for n, size in _SMALL:
        small_g[n] = flat[:, off:off + size]
        off += _lanes(size)
    loss = flat[0, off]
    small_g["a_norm"] = lax.dynamic_slice(small_g["a_norm"], (0, chip * rows_dkv), (1, rows_dkv))

    small_w = {"a_norm": a_norm, "kv_norm": kv_norm, "g_ckv": g_ckv, "g_k_nope": g_k_nope, "g_k_rope": g_k_rope,
               "b_norm": b_norm, "b_g_q_lat": b_g_q_lat, "b_g_q_nope": b_g_q_nope, "b_g_q_rope": b_g_q_rope,
               "mem_norm": mem_norm, "g_mem_q": g_mem_q, "g_mem_k": g_mem_k}
    small_m = {"a_norm": m_a_norm, "kv_norm": m_kv_norm, "g_ckv": m_g_ckv, "g_k_nope": m_g_k_nope,
               "g_k_rope": m_g_k_rope, "b_norm": m_b_norm, "b_g_q_lat": m_b_g_q_lat, "b_g_q_nope": m_b_g_q_nope,
               "b_g_q_rope": m_b_g_q_rope, "mem_norm": m_mem_norm, "g_mem_q": m_g_mem_q, "g_mem_k": m_g_mem_k}
    small_v = {"a_norm": v_a_norm, "kv_norm": v_kv_norm, "g_ckv": v_g_ckv, "g_k_nope": v_g_k_nope,
               "g_k_rope": v_g_k_rope, "b_norm": v_b_norm, "b_g_q_lat": v_b_g_q_lat, "b_g_q_nope": v_b_g_q_nope,
               "b_g_q_rope": v_b_g_q_rope, "mem_norm": v_mem_norm, "g_mem_q": v_g_mem_q, "g_mem_k": v_g_mem_k}
    snames = [n for n, _ in _SMALL]
    packs = [_pack_rows([_pad_lanes(src[n]) for n in snames])
             for src in (small_w, small_g, small_m, small_v)]
    small_res = _adamw(packs[0], [packs[1]], packs[2], packs[3], name="adamw_gains")
    small_out = {n: [] for n in snames}
    for res in small_res:
        flat_r = res.reshape(1, -1)
        off = 0
        for n in snames:
            size = small_w[n].size
            small_out[n].append(flat_r[:, off:off + size].reshape(small_w[n].shape))
            off += _lanes(size)

    big_names = {"a_w_in": ("a_in", a_w_in), "a_w_out": ("a_out", a_w_out), "w_dkv": ("dkv", w_dkv),
                 "w_ukv": ("ukv", w_ukv), "b_w_in": ("b_in", b_w_in), "b_w_uq": ("uq", b_w_uq),
                 "b_w_out": ("b_out", b_w_out), "w_mem_kv": ("mem_kv", w_mem_kv)}
    order = ["a_norm", "a_w_in", "a_w_out", "kv_norm", "w_dkv", "g_ckv", "w_ukv", "g_k_nope", "g_k_rope",
             "b_norm", "b_w_in", "b_g_q_lat", "b_w_uq", "b_g_q_nope", "b_g_q_rope", "b_w_out", "mem_norm",
             "w_mem_kv", "g_mem_q", "g_mem_k"]
    groups = [[], [], [], []]
    for n in order:
        if n in big_names:
            key, ref_arr = big_names[n]
            for t in range(4):
                groups[t].append(big_out[key][t].reshape(ref_arr.shape))
        else:
            for t in range(4):
                groups[t].append(small_out[n][t])
    return (loss, grad_x[None], *groups[0], *groups[1], *groups[2], *groups[3])
```

```python
import functools

import jax
import jax.numpy as jnp
from jax import lax
from jax.experimental import pallas as pl
from jax.experimental.pallas import tpu as pltpu

F32 = jnp.float32
BF16 = jnp.bfloat16
MESH = pl.DeviceIdType.MESH

D_MODEL = 2048
HEAD_DIM = 128
N_SB_HEADS = 12
N_MEM_HEADS = 4
N_MLA_HEADS = 12
MEM_LEN = 256
Q_LORA = 512
KV_LORA = 512
ROPE_DIM = 64
SB_W = N_SB_HEADS * HEAD_DIM
MEM_W = N_MEM_HEADS * HEAD_DIM
MLA_W = N_MLA_HEADS * HEAD_DIM
QKV_W = 3 * SB_W
GATE_W = SB_W + 2 * MEM_W
CAT_W = 2 * HEAD_DIM
ROPE_THETA = 10000.0
EPS = 1e-6
N_CHIPS = 4
N_DEV = 8

ADAM_LR = 0.001
ADAM_B1 = 0.9
ADAM_B2 = 0.999
ADAM_EPS = 1e-08
ADAM_WD = 0.01
ADAM_STEP = 10

VMEM_LIMIT_BYTES = 56 * 1024 * 1024
ROW_BLOCK = 256
ATT_BLOCK = 256


def _params(*sem):
    return pltpu.CompilerParams(dimension_semantics=sem, vmem_limit_bytes=VMEM_LIMIT_BYTES)


def _pick(n, cands):
    for c in cands:
        if n % c == 0:
            return c
    return n


def _mm(a, b, *, name, ta=False, tb=False, out_dtype=F32, res=None, n_split=1):
    if ta:
        k_dim, m_dim = a.shape
    else:
        m_dim, k_dim = a.shape
    if tb:
        n_dim, kb = b.shape
    else:
        kb, n_dim = b.shape
    assert kb == k_dim, (a.shape, b.shape)
    n_per = n_dim // n_split
    bm = m_dim if m_dim <= 1024 else _pick(m_dim, (1024, 512, 256))
    bn = n_per if n_per <= 1024 else _pick(n_per, (1024, 896, 768, 640, 512, 256, 128))
    bk = k_dim if k_dim <= 1024 else _pick(k_dim, (512, 256, 128))
    nk = k_dim // bk
    nb_per = n_per // bn
    grid = (m_dim // bm, n_dim // bn, nk)
    a_spec = (pl.BlockSpec((bk, bm), lambda i, j, k: (k, i)) if ta
              else pl.BlockSpec((bm, bk), lambda i, j, k: (i, k)))
    b_spec = (pl.BlockSpec((bn, bk), lambda i, j, k: (j, k)) if tb
              else pl.BlockSpec((bk, bn), lambda i, j, k: (k, j)))
    dims = (((0 if ta else 1,), (1 if tb else 0,)), ((), ()))
    in_specs = [a_spec, b_spec]
    args = [a, b]
    if res is not None:
        in_specs.append(pl.BlockSpec((bm, bn), lambda i, j, k: (i, j)))
        args.append(res)
    if n_split == 1:
        out_shape = jax.ShapeDtypeStruct((m_dim, n_dim), out_dtype)
        out_spec = pl.BlockSpec((bm, bn), lambda i, j, k: (i, j))
    else:
        out_shape = jax.ShapeDtypeStruct((n_split, m_dim, n_per), out_dtype)
        out_spec = pl.BlockSpec((None, bm, bn), lambda i, j, k: (j // nb_per, i, j % nb_per))

    def body(*refs):
        if res is None:
            a_ref, b_ref, o_ref, acc = refs
            r_ref = None
        else:
            a_ref, b_ref, r_ref, o_ref, acc = refs
        k = pl.program_id(2)

        @pl.when(k == 0)
        def _():
            acc[...] = jnp.zeros_like(acc)

        acc[...] += lax.dot_general(a_ref[...].astype(BF16), b_ref[...].astype(BF16), dims,
                                    preferred_element_type=F32)

        @pl.when(k == nk - 1)
        def _():
            r = acc[...]
            if r_ref is not None:
                r = r + r_ref[...]
            o_ref[...] = r.astype(out_dtype)

    return pl.pallas_call(
        body, out_shape=out_shape, grid=grid, in_specs=in_specs, out_specs=out_spec,
        scratch_shapes=[pltpu.VMEM((bm, bn), F32)], name=name,
        compiler_params=_params("parallel", "parallel", "arbitrary"),
    )(*args)


def _rowwise(body, n_rows, ins, outs, accs=(), *, name, block=ROW_BLOCK):
    blk = min(block, n_rows)
    assert n_rows % blk == 0
    in_specs = []
    for arr, is_row in ins:
        if is_row:
            assert arr.shape[0] == n_rows, (name, arr.shape, n_rows)
            in_specs.append(pl.BlockSpec((blk, arr.shape[1]), lambda i: (i, 0)))
        else:
            in_specs.append(pl.BlockSpec(arr.shape, lambda i, nd=arr.ndim: (0,) * nd))
    out_shape = [jax.ShapeDtypeStruct((n_rows, w), dt) for w, dt in outs]
    out_specs = [pl.BlockSpec((blk, w), lambda i: (i, 0)) for w, _ in outs]
    out_shape += [jax.ShapeDtypeStruct(s, dt) for s, dt in accs]
    out_specs += [pl.BlockSpec(s, lambda i, nd=len(s): (0,) * nd) for s, _ in accs]
    n_in, n_out, n_acc = len(ins), len(outs), len(accs)

    def kern(*refs):
        in_refs = refs[:n_in]
        out_refs = refs[n_in:n_in + n_out]
        acc_refs = refs[n_in + n_out:]
        if n_acc:
            @pl.when(pl.program_id(0) == 0)
            def _():
                for r in acc_refs:
                    r[...] = jnp.zeros_like(r)
        body(in_refs, out_refs, acc_refs)

    return pl.pallas_call(
        kern, out_shape=out_shape, grid=(n_rows // blk,), in_specs=in_specs, out_specs=out_specs,
        name=name, compiler_params=_params("arbitrary"),
    )(*[arr for arr, _ in ins])


def _rms(x, g, n=None):
    n = x.shape[-1] if n is None else n
    r = lax.rsqrt(jnp.sum(x * x, axis=-1, keepdims=True) / n + EPS)
    return x * r * g


def _rms_bwd(x, g, dy, n=None):
    n = x.shape[-1] if n is None else n
    r = lax.rsqrt(jnp.sum(x * x, axis=-1, keepdims=True) / n + EPS)
    gdy = dy * g
    dx = r * (gdy - x * ((r * r) * (jnp.sum(gdy * x, axis=-1, keepdims=True) / n)))
    dg = jnp.sum(dy * x * r, axis=0, keepdims=True)
    return dx, dg


def _swap_halves(x):
    lane = lax.broadcasted_iota(jnp.int32, x.shape, 1)
    return jnp.where(lane < ROPE_DIM // 2, pltpu.roll(x, 128 - ROPE_DIM // 2, 1),
                     pltpu.roll(x, ROPE_DIM // 2, 1))


def _rope(n, cos_t, sin_t):
    return n * cos_t + _swap_halves(n) * sin_t


def _rope_bwd(dy, cos_t, sin_t):
    return dy * cos_t - _swap_halves(dy) * sin_t


def _sigmoid(g):
    return 1.0 / (1.0 + jnp.exp(-g))


def _dot_t(a, b):
    return lax.dot_general(a, b, (((1,), (1,)), ((), ())), preferred_element_type=F32)


def _tdot(a, b):
    return lax.dot_general(a, b, (((0,), (0,)), ((), ())), preferred_element_type=F32)


def _dot(a, b):
    return jnp.dot(a, b, preferred_element_type=F32)


def _hs(h, w=HEAD_DIM, base=0):
    return slice(base + h * w, base + (h + 1) * w)


def _mem_head(qm, gq, mk_h, mv_h):
    qb = _rms(qm, gq).astype(BF16)
    s = _dot_t(qb, mk_h) * (HEAD_DIM ** -0.5)
    e = jnp.exp(s - jnp.max(s, axis=-1, keepdims=True))
    p = e / jnp.sum(e, axis=-1, keepdims=True)
    mo = _dot(p.astype(BF16), mv_h)
    return qb, p, mo


def _mix_fwd(att, gates, c0, mk, mv, gq, *, name):
    n_rows = att.shape[0]

    def body(ins, outs, _):
        att_ref, g_ref, mk_ref, mv_ref, gq_ref = ins
        (o_ref,) = outs
        g = g_ref[:, c0:c0 + SB_W]
        o_ref[:, :SB_W] = (att_ref[...] * (g * _sigmoid(g))).astype(BF16)
        for h in range(N_MEM_HEADS):
            qm = g_ref[:, _hs(h, base=c0 + SB_W)]
            gm = g_ref[:, _hs(h, base=c0 + SB_W + MEM_W)]
            _, _, mo = _mem_head(qm, gq_ref[...], mk_ref[:, _hs(h)], mv_ref[:, _hs(h)])
            o_ref[:, _hs(h, base=SB_W)] = (mo * (gm * _sigmoid(gm))).astype(BF16)

    (mixed,) = _rowwise(body, n_rows,
                        [(att, True), (gates, True), (mk, False), (mv, False), (gq, False)],
                        [(D_MODEL, BF16)], name=name)
    return mixed


def _mix_bwd(dmixed, att, gates, c0, mk, mv, gq, *, name):
    n_rows = att.shape[0]
    scale = HEAD_DIM ** -0.5

    def body(ins, outs, accs):
        dm_ref, att_ref, g_ref, mk_ref, mv_ref, gq_ref = ins
        datt_ref, dg_ref = outs
        dmk_ref, dmv_ref, dgq_ref = accs
        g = g_ref[:, c0:c0 + SB_W]
        sg = _sigmoid(g)
        dm = dm_ref[:, :SB_W]
        datt_ref[...] = dm * (g * sg)
        dg_ref[:, :SB_W] = (dm * att_ref[...] * (sg * (1.0 + g * (1.0 - sg)))).astype(BF16)
        for h in range(N_MEM_HEADS):
            qm = g_ref[:, _hs(h, base=c0 + SB_W)]
            gm = g_ref[:, _hs(h, base=c0 + SB_W + MEM_W)]
            mk_h = mk_ref[:, _hs(h)]
            mv_h = mv_ref[:, _hs(h)]
            qb, p, mo = _mem_head(qm, gq_ref[...], mk_h, mv_h)
            sgm = _sigmoid(gm)
            dmh = dm_ref[:, _hs(h, base=SB_W)]
            dmo = dmh * (gm * sgm)
            dg_ref[:, _hs(h, base=SB_W + MEM_W)] = (
                dmh * mo * (sgm * (1.0 + gm * (1.0 - sgm)))).astype(BF16)
            dmo_b = dmo.astype(BF16)
            pb = p.astype(BF16)
            dp = _dot_t(dmo_b, mv_h)
            dmv_ref[:, _hs(h)] += _tdot(pb, dmo_b)
            ds = (p * (dp - jnp.sum(dp * p, axis=-1, keepdims=True)) * scale).astype(BF16)
            dqn = _dot(ds, mk_h)
            dmk_ref[:, _hs(h)] += _tdot(ds, qb)
            dqm, dgq = _rms_bwd(qm, gq_ref[...], dqn)
            dg_ref[:, _hs(h, base=SB_W)] = dqm.astype(BF16)
            dgq_ref[...] += dgq

    return _rowwise(body, n_rows,
                    [(dmixed, True), (att, True), (gates, True), (mk, False), (mv, False), (gq, False)],
                    [(SB_W, F32), (GATE_W, BF16)],
                    [((MEM_LEN, MEM_W), F32), ((MEM_LEN, MEM_W), F32), ((1, HEAD_DIM), F32)],
                    name=name)


def _mem_side_fwd(mem, g_norm, w_kv, g_k, *, tag):
    def norm_body(ins, outs, _):
        outs[0][...] = _rms(ins[0][...], ins[1][...]).astype(BF16)

    (mn,) = _rowwise(norm_body, MEM_LEN, [(mem, True), (g_norm, False)], [(D_MODEL, BF16)],
                     name=f"mem_norm_{tag}")
    mkv = _mm(mn, w_kv, name=f"mem_kv_{tag}")

    def kv_body(ins, outs, _):
        mkv_ref, gk_ref = ins
        mk_ref, mv_ref = outs
        for h in range(N_MEM_HEADS):
            mk_ref[:, _hs(h)] = _rms(mkv_ref[:, _hs(h)], gk_ref[...]).astype(BF16)
        mv_ref[...] = mkv_ref[:, MEM_W:].astype(BF16)

    mk, mv = _rowwise(kv_body, MEM_LEN, [(mkv, True), (g_k, False)], [(MEM_W, BF16), (MEM_W, BF16)],
                      name=f"mem_kv_prep_{tag}")
    return mn, mkv, mk, mv


def _mem_side_bwd(mem, g_norm, w_kv, g_k, mn, mkv, dmk, dmv, *, tag):
    def kv_body(ins, outs, accs):
        mkv_ref, gk_ref, dmk_ref, dmv_ref = ins
        (d_ref,) = outs
        (dgk_ref,) = accs
        for h in range(N_MEM_HEADS):
            dx, dg = _rms_bwd(mkv_ref[:, _hs(h)], gk_ref[...], dmk_ref[:, _hs(h)])
            d_ref[:, _hs(h)] = dx.astype(BF16)
            dgk_ref[...] += dg
        d_ref[:, MEM_W:] = dmv_ref[...].astype(BF16)

    dmkv, dgk = _rowwise(kv_body, MEM_LEN, [(mkv, True), (g_k, False), (dmk, True), (dmv, True)],
                         [(2 * MEM_W, BF16)], [((1, HEAD_DIM), F32)], name=f"mem_kv_prep_bwd_{tag}")
    dmn = _mm(dmkv, w_kv, tb=True, name=f"mem_kv_dx_{tag}")
    dw = _mm(mn, dmkv, ta=True, name=f"mem_kv_dw_{tag}")

    def norm_body(ins, outs, accs):
        _, dg = _rms_bwd(ins[0][...], ins[1][...], ins[2][...])
        accs[0][...] += dg

    (dgn,) = _rowwise(norm_body, MEM_LEN, [(mem, True), (g_norm, False), (dmn, True)], [],
                      [((1, D_MODEL), F32)], name=f"mem_norm_bwd_{tag}")
    return dw, dgn, dgk


LOG2_E = 1.4426950408889634


def _log2_one_minus_beta(z2):
    nz = -z2
    u = jnp.exp2(jnp.minimum(z2, nz))
    return jnp.minimum(nz, 0.0) - jnp.log2(1.0 + u)


def _split_dot(x, tri):
    hi = x.astype(BF16)
    lo = (x - hi.astype(F32)).astype(BF16)
    return _dot(hi, tri) + _dot(lo, tri)


def _sb_fwd(qkv, *, name, hp=2):
    seq = qkv.shape[0]
    blk = min(ATT_BLOCK, seq)
    nq = seq // blk
    scale = HEAD_DIM ** -0.5

    def body(q_ref, k_ref, v_ref, o_ref):
        i = pl.program_id(1)
        qs = [q_ref[:, _hs(t)] for t in range(hp)]
        row = lax.broadcasted_iota(jnp.int32, (blk, blk), 0)
        col = lax.broadcasted_iota(jnp.int32, (blk, blk), 1)
        after = (row > col).astype(BF16)
        causal = col < row

        def step(j, carry, masked):
            off = pl.multiple_of(j * blk, blk)
            zs = [_dot_t(qs[t], k_ref[pl.ds(off, blk), _hs(t)]) * (scale * LOG2_E) for t in range(hp)]
            ls = []
            for t in range(hp):
                l = _log2_one_minus_beta(zs[t])
                if masked:
                    l = jnp.where(causal, l, 0.0)
                ls.append(l)
            cs = [_split_dot(ls[t], after) for t in range(hp)]
            out = []
            for t in range(hp):
                run, acc = carry[t]
                a = jnp.exp2(zs[t] + ls[t] + cs[t] + run)
                if masked:
                    a = jnp.where(causal, a, 0.0)
                acc = acc + _dot(a.astype(BF16), v_ref[pl.ds(off, blk), _hs(t)])
                run = run + jnp.sum(ls[t], axis=-1, keepdims=True)
                out.append((run, acc))
            return tuple(out)

        init = (jnp.zeros((blk, 1), F32), jnp.zeros((blk, HEAD_DIM), F32))
        carry = step(i, (init,) * hp, True)
        carry = lax.fori_loop(0, i, lambda jj, c: step(i - 1 - jj, c, False), carry)
        for t in range(hp):
            o_ref[:, _hs(t)] = carry[t][1]

    nh = N_SB_HEADS // hp
    return pl.pallas_call(
        body, out_shape=jax.ShapeDtypeStruct((seq, SB_W), F32), grid=(nh, nq),
        in_specs=[pl.BlockSpec((blk, hp * HEAD_DIM), lambda h, i: (i, h)),
                  pl.BlockSpec((seq, hp * HEAD_DIM), lambda h, i: (0, nh + h)),
                  pl.BlockSpec((seq, hp * HEAD_DIM), lambda h, i: (0, 2 * nh + h))],
        out_specs=pl.BlockSpec((blk, hp * HEAD_DIM), lambda h, i: (i, h)),
        name=name, compiler_params=_params("parallel", "arbitrary"),
    )(qkv, qkv, qkv)


def _chain_modes(s, qb):
    return tuple(None if t < s else ("m" if t == s else "f") for t in range(qb))


def _sb_bwd(qkv, dout, *, name):
    seq = qkv.shape[0]
    blk = min(ATT_BLOCK, seq)
    nkb = seq // blk
    qb = 2 if nkb % 2 == 0 else 1
    rows = qb * blk
    scale = HEAD_DIM ** -0.5

    def body(q_ref, k_ref, v_ref, do_ref, dq_ref, dk_ref, dv_ref, de_s, sn_s):
        g = pl.program_id(1)
        base = g * qb

        @pl.when(g == 0)
        def _():
            dk_ref[...] = jnp.zeros_like(dk_ref)
            dv_ref[...] = jnp.zeros_like(dv_ref)

        qs = [q_ref[t * blk:(t + 1) * blk, :] for t in range(qb)]
        dos = [do_ref[t * blk:(t + 1) * blk, :].astype(BF16) for t in range(qb)]
        row = lax.broadcasted_iota(jnp.int32, (blk, blk), 0)
        col = lax.broadcasted_iota(jnp.int32, (blk, blk), 1)
        after = (row > col).astype(BF16)
        before = (row < col).astype(BF16)
        causal = col < row

        def sweep1(j, runs, modes):
            off = pl.multiple_of(j * blk, blk)
            kb = k_ref[pl.ds(off, blk), :]
            vb = v_ref[pl.ds(off, blk), :]
            act = [t for t in range(qb) if modes[t]]
            zs = {t: _dot_t(qs[t], kb) * (scale * LOG2_E) for t in act}
            ls = {}
            for t in act:
                l = _log2_one_minus_beta(zs[t])
                sn_s[t, j] = jnp.exp2(l).astype(BF16)
                ls[t] = jnp.where(causal, l, 0.0) if modes[t] == "m" else l
            cs = {t: _split_dot(ls[t], after) for t in act}
            runs = list(runs)
            dv_inc = None
            for t in act:
                a = jnp.exp2(zs[t] + ls[t] + cs[t] + runs[t])
                if modes[t] == "m":
                    a = jnp.where(causal, a, 0.0)
                de_s[t, j] = (a * _dot_t(dos[t], vb)).astype(BF16)
                inc = _tdot(a.astype(BF16), dos[t])
                dv_inc = inc if dv_inc is None else dv_inc + inc
                runs[t] = runs[t] + jnp.sum(ls[t], axis=-1, keepdims=True)
            dv_ref[pl.ds(off, blk), :] += dv_inc
            return tuple(runs)

        runs = (jnp.zeros((blk, 1), F32),) * qb
        for s in reversed(range(qb)):
            runs = sweep1(base + s, runs, _chain_modes(s, qb))
        lax.fori_loop(0, base, lambda jj, r: sweep1(base - 1 - jj, r, ("f",) * qb), runs)

        def sweep2(j, carry, modes):
            lefts, dqs = list(carry[0]), list(carry[1])
            off = pl.multiple_of(j * blk, blk)
            kb = k_ref[pl.ds(off, blk), :]
            dk_inc = None
            for t in range(qb):
                if not modes[t]:
                    continue
                deb = de_s[t, j]
                de = deb.astype(F32)
                pre = _dot(deb, before) + lefts[t]
                dz = (de + pre) * sn_s[t, j].astype(F32) - pre
                if modes[t] == "m":
                    dz = jnp.where(causal, dz, 0.0)
                dzb = (dz * scale).astype(BF16)
                dqs[t] = dqs[t] + _dot(dzb, kb)
                inc = _tdot(dzb, qs[t])
                dk_inc = inc if dk_inc is None else dk_inc + inc
                lefts[t] = lefts[t] + jnp.sum(de, axis=-1, keepdims=True)
            dk_ref[pl.ds(off, blk), :] += dk_inc
            return tuple(lefts), tuple(dqs)

        carry = ((jnp.zeros((blk, 1), F32),) * qb, (jnp.zeros((blk, HEAD_DIM), F32),) * qb)
        carry = lax.fori_loop(0, base, lambda j, c: sweep2(j, c, ("f",) * qb), carry)
        for s in range(qb):
            carry = sweep2(base + s, carry, _chain_modes(s, qb))
        for t in range(qb):
            dq_ref[t * blk:(t + 1) * blk, :] = carry[1][t]

    out = jax.ShapeDtypeStruct((seq, SB_W), F32)
    return pl.pallas_call(
        body, out_shape=[out, out, out], grid=(N_SB_HEADS, nkb // qb),
        in_specs=[pl.BlockSpec((rows, HEAD_DIM), lambda h, i: (i, h)),
                  pl.BlockSpec((seq, HEAD_DIM), lambda h, i: (0, N_SB_HEADS + h)),
                  pl.BlockSpec((seq, HEAD_DIM), lambda h, i: (0, 2 * N_SB_HEADS + h)),
                  pl.BlockSpec((rows, HEAD_DIM), lambda h, i: (i, h))],
        out_specs=[pl.BlockSpec((rows, HEAD_DIM), lambda h, i: (i, h)),
                   pl.BlockSpec((seq, HEAD_DIM), lambda h, i: (0, h)),
                   pl.BlockSpec((seq, HEAD_DIM), lambda h, i: (0, h))],
        scratch_shapes=[pltpu.VMEM((qb, nkb, blk, blk), BF16), pltpu.VMEM((qb, nkb, blk, blk), BF16)],
        name=name, compiler_params=_params("parallel", "arbitrary"),
    )(qkv, qkv, qkv, dout)


MLA_SCALE = (HEAD_DIM + ROPE_DIM) ** -0.5


def _mla_fwd(q_cat, k_cat, v, *, name, hp=2):
    seq = q_cat.shape[0]
    blk = min(ATT_BLOCK, seq)
    nq = seq // blk

    def body(q_ref, k_ref, v_ref, o_ref, lse_ref):
        i = pl.program_id(1)
        qs = [q_ref[:, t * CAT_W:(t + 1) * CAT_W] for t in range(hp)]
        row = lax.broadcasted_iota(jnp.int32, (blk, blk), 0)
        col = lax.broadcasted_iota(jnp.int32, (blk, blk), 1)
        causal = col <= row

        def step(j, carry, masked):
            off = pl.multiple_of(j * blk, blk)
            ss = [_dot_t(qs[t], k_ref[pl.ds(off, blk), t * CAT_W:(t + 1) * CAT_W]) * (MLA_SCALE * LOG2_E)
                  for t in range(hp)]
            out = []
            for t in range(hp):
                m, l, acc = carry[t]
                s = ss[t]
                if masked:
                    s = jnp.where(causal, s, -jnp.inf)
                m_new = jnp.maximum(m, jnp.max(s, axis=-1, keepdims=True))
                p = jnp.exp2(s - m_new)
                alpha = jnp.exp2(m - m_new)
                l = alpha * l + jnp.sum(p, axis=-1, keepdims=True)
                acc = alpha * acc + _dot(p.astype(BF16), v_ref[pl.ds(off, blk), _hs(t)])
                out.append((m_new, l, acc))
            return tuple(out)

        init = (jnp.full((blk, 1), -jnp.inf, F32), jnp.zeros((blk, 1), F32),
                jnp.zeros((blk, HEAD_DIM), F32))
        carry = step(i, (init,) * hp, True)
        carry = lax.fori_loop(0, i, lambda j, c: step(j, c, False), carry)
        for t in range(hp):
            m, l, acc = carry[t]
            o_ref[:, _hs(t)] = acc / l
            lse_ref[:, _hs(t)] = jnp.broadcast_to((m + jnp.log2(l)) * (1.0 / LOG2_E), (blk, HEAD_DIM))

    out = jax.ShapeDtypeStruct((seq, MLA_W), F32)
    return pl.pallas_call(
        body, out_shape=[out, out], grid=(N_MLA_HEADS // hp, nq),
        in_specs=[pl.BlockSpec((blk, hp * CAT_W), lambda h, i: (i, h)),
                  pl.BlockSpec((seq, hp * CAT_W), lambda h, i: (0, h)),
                  pl.BlockSpec((seq, hp * HEAD_DIM), lambda h, i: (0, h))],
        out_specs=[pl.BlockSpec((blk, hp * HEAD_DIM), lambda h, i: (i, h)),
                   pl.BlockSpec((blk, hp * HEAD_DIM), lambda h, i: (i, h))],
        name=name, compiler_params=_params("parallel", "arbitrary"),
    )(q_cat, k_cat, v)


def _mla_bwd(q_cat, k_cat, v, out, lse, dout, *, name):
    seq = q_cat.shape[0]
    blk = min(ATT_BLOCK, seq)
    nkb = seq // blk
    qb = 2 if nkb % 2 == 0 else 1
    rows = qb * blk

    def body(q_ref, k_ref, v_ref, o_ref, lse_ref, do_ref, dq_ref, dk_ref, dv_ref):
        g = pl.program_id(1)
        base = g * qb

        @pl.when(g == 0)
        def _():
            dk_ref[...] = jnp.zeros_like(dk_ref)
            dv_ref[...] = jnp.zeros_like(dv_ref)

        qs, dobs, deltas, lses = [], [], [], []
        for t in range(qb):
            rs = slice(t * blk, (t + 1) * blk)
            do = do_ref[rs, :]
            qs.append(q_ref[rs, :])
            dobs.append(do.astype(BF16))
            deltas.append(jnp.sum(do * o_ref[rs, :], axis=-1, keepdims=True))
            lses.append(lse_ref[rs, :1] * LOG2_E)
        row = lax.broadcasted_iota(jnp.int32, (blk, blk), 0)
        col = lax.broadcasted_iota(jnp.int32, (blk, blk), 1)
        causal = col <= row

        def step(j, dqs, modes):
            off = pl.multiple_of(j * blk, blk)
            kb = k_ref[pl.ds(off, blk), :]
            vb = v_ref[pl.ds(off, blk), :]
            act = [t for t in range(qb) if modes[t]]
            ss = {t: _dot_t(qs[t], kb) * (MLA_SCALE * LOG2_E) for t in act}
            dps = {t: _dot_t(dobs[t], vb) for t in act}
            dqs = list(dqs)
            dv_inc = dk_inc = None
            for t in act:
                p = jnp.exp2(ss[t] - lses[t])
                if modes[t] == "m":
                    p = jnp.where(causal, p, 0.0)
                ds = (p * (dps[t] - deltas[t]) * MLA_SCALE).astype(BF16)
                inc_v = _tdot(p.astype(BF16), dobs[t])
                inc_k = _tdot(ds, qs[t])
                dv_inc = inc_v if dv_inc is None else dv_inc + inc_v
                dk_inc = inc_k if dk_inc is None else dk_inc + inc_k
                dqs[t] = dqs[t] + _dot(ds, kb)
            dv_ref[pl.ds(off, blk), :] += dv_inc
            dk_ref[pl.ds(off, blk), :] += dk_inc
            return tuple(dqs)

        dqs = (jnp.zeros((blk, CAT_W), F32),) * qb
        dqs = lax.fori_loop(0, base, lambda j, c: step(j, c, ("f",) * qb), dqs)
        for s in range(qb):
            modes = tuple(None if t < s else ("m" if t == s else "f") for t in range(qb))
            dqs = step(base + s, dqs, modes)
        for t in range(qb):
            dq_ref[t * blk:(t + 1) * blk, :] = dqs[t]

    return pl.pallas_call(
        body,
        out_shape=[jax.ShapeDtypeStruct((seq, N_MLA_HEADS * CAT_W), F32),
                   jax.ShapeDtypeStruct((seq, N_MLA_HEADS * CAT_W), F32),
                   jax.ShapeDtypeStruct((seq, MLA_W), F32)],
        grid=(N_MLA_HEADS, nkb // qb),
        in_specs=[pl.BlockSpec((rows, CAT_W), lambda h, i: (i, h)),
                  pl.BlockSpec((seq, CAT_W), lambda h, i: (0, h)),
                  pl.BlockSpec((seq, HEAD_DIM), lambda h, i: (0, h)),
                  pl.BlockSpec((rows, HEAD_DIM), lambda h, i: (i, h)),
                  pl.BlockSpec((rows, HEAD_DIM), lambda h, i: (i, h)),
                  pl.BlockSpec((rows, HEAD_DIM), lambda h, i: (i, h))],
        out_specs=[pl.BlockSpec((rows, CAT_W), lambda h, i: (i, h)),
                   pl.BlockSpec((seq, CAT_W), lambda h, i: (0, h)),
                   pl.BlockSpec((seq, HEAD_DIM), lambda h, i: (0, h))],
        name=name, compiler_params=_params("parallel", "arbitrary"),
    )(q_cat, k_cat, v, out, lse, dout)


def _local_step(x, mem, positions, target, w, g):
    seq = x.shape[0]
    inv_freq = jnp.power(ROPE_THETA, -jnp.arange(0, ROPE_DIM, 2, dtype=F32) / ROPE_DIM)
    ang = positions.astype(F32)[:, None] * inv_freq
    cos, sin = jnp.cos(ang), jnp.sin(ang)
    lane_pad = jnp.zeros((seq, HEAD_DIM - ROPE_DIM), F32)
    cos_t = jnp.concatenate([cos, cos, lane_pad], axis=1)
    sin_t = jnp.concatenate([-sin, sin, lane_pad], axis=1)
    gain_pad = jnp.zeros((1, HEAD_DIM - ROPE_DIM), F32)
    g_k_rope = jnp.concatenate([g["g_k_rope"], gain_pad], axis=1)
    g_q_rope = jnp.concatenate([g["b_g_q_rope"], gain_pad], axis=1)

    def norm_to_bf16(src, gain, name):
        def body(ins, outs, _):
            outs[0][...] = _rms(ins[0][...], ins[1][...]).astype(BF16)
        return _rowwise(body, seq, [(src, True), (gain, False)], [(src.shape[1], BF16)], name=name)[0]

    h_a = norm_to_bf16(x, g["a_norm"], "a_norm_fwd")
    qkv = _mm(h_a, w["a_in_qkv"], out_dtype=BF16, name="a_in_qkv")
    gr = _mm(h_a, w["a_in_gate"], name="a_in_gate")
    sb = _sb_fwd(qkv, name="sb_fwd")
    mem0 = _mem_side_fwd(mem, g["mem_norm"][0:1], w["mem_kv"][0], g["g_mem_k"][0:1], tag="a")
    mixed_a = _mix_fwd(sb, gr, 0, mem0[2], mem0[3], g["g_mem_q"][0:1], name="a_mix_fwd")
    x1 = _mm(mixed_a, w["a_out"], res=x, name="a_out")

    def norms2_body(ins, outs, _):
        xv = ins[0][...]
        outs[0][...] = _rms(xv, ins[1][...]).astype(BF16)
        outs[1][...] = _rms(xv, ins[2][...]).astype(BF16)

    h_kv, h_b = _rowwise(norms2_body, seq, [(x1, True), (g["kv_norm"], False), (g["b_norm"], False)],
                         [(D_MODEL, BF16), (D_MODEL, BF16)], name="kv_b_norm_fwd")
    ckr = _mm(h_kv, w["dkv"], name="dkv")

    def ckr_body(ins, outs, _):
        ckr_ref, gc_ref, gr_ref, c_ref, s_ref = ins
        outs[0][...] = _rms(ckr_ref[:, :KV_LORA], gc_ref[...]).astype(BF16)
        kr = _rms(ckr_ref[:, KV_LORA:], gr_ref[...], n=ROPE_DIM)
        outs[1][...] = _rope(kr, c_ref[...], s_ref[...]).astype(BF16)

    c_n, k_r = _rowwise(ckr_body, seq,
                        [(ckr, True), (g["g_ckv"], False), (g_k_rope, False), (cos_t, True), (sin_t, True)],
                        [(KV_LORA, BF16), (HEAD_DIM, BF16)], name="ckv_prep_fwd")
    kv = _mm(c_n, w["ukv"], name="ukv")

    def kcat_body(ins, outs, _):
        kv_ref, kr_ref, gk_ref = ins
        kc_ref, v_ref = outs
        for h in range(N_MLA_HEADS):
            kc_ref[:, h * CAT_W:h * CAT_W + HEAD_DIM] = _rms(
                kv_ref[:, h * CAT_W:h * CAT_W + HEAD_DIM], gk_ref[...]).astype(BF16)
            kc_ref[:, h * CAT_W + HEAD_DIM:(h + 1) * CAT_W] = kr_ref[...]
            v_ref[:, _hs(h)] = kv_ref[:, h * CAT_W + HEAD_DIM:(h + 1) * CAT_W].astype(BF16)

    k_cat, v_mla = _rowwise(kcat_body, seq, [(kv, True), (k_r, True), (g["g_k_nope"], False)],
                            [(N_MLA_HEADS * CAT_W, BF16), (MLA_W, BF16)], name="k_prep_fwd")

    p2 = _mm(h_b, w["b_in"], name="b_in")

    def qlat_body(ins, outs, _):
        outs[0][...] = _rms(ins[0][:, :Q_LORA], ins[1][...]).astype(BF16)

    (q_l,) = _rowwise(qlat_body, seq, [(p2, True), (g["b_g_q_lat"], False)], [(Q_LORA, BF16)],
                      name="q_lat_norm_fwd")
    q_up = _mm(q_l, w["uq"], name="uq")

    def qcat_body(ins, outs, _):
        q_ref, gn_ref, gr_ref, c_ref, s_ref = ins
        (o_ref,) = outs
        for h in range(N_MLA_HEADS):
            o_ref[:, h * CAT_W:h * CAT_W + HEAD_DIM] = _rms(
                q_ref[:, h * CAT_W:h * CAT_W + HEAD_DIM], gn_ref[...]).astype(BF16)
            qr = _rms(q_ref[:, h * CAT_W + HEAD_DIM:(h + 1) * CAT_W], gr_ref[...], n=ROPE_DIM)
            o_ref[:, h * CAT_W + HEAD_DIM:(h + 1) * CAT_W] = _rope(qr, c_ref[...], s_ref[...]).astype(BF16)

    (q_cat,) = _rowwise(qcat_body, seq,
                        [(q_up, True), (g["b_g_q_nope"], False), (g_q_rope, False), (cos_t, True), (sin_t, True)],
                        [(N_MLA_HEADS * CAT_W, BF16)], name="q_prep_fwd")
    att, lse = _mla_fwd(q_cat, k_cat, v_mla, name="mla_fwd")
    mem1 = _mem_side_fwd(mem, g["mem_norm"][1:2], w["mem_kv"][1], g["g_mem_k"][1:2], tag="b")
    mixed_b = _mix_fwd(att, p2, Q_LORA, mem1[2], mem1[3], g["g_mem_q"][1:2], name="b_mix_fwd")
    y = _mm(mixed_b, w["b_out"], res=x1, name="b_out")

    def loss_body(ins, outs, accs):
        diff = ins[0][...] - ins[1][...]
        outs[0][...] = diff / D_MODEL
        col = jnp.sum(diff * diff, axis=0, keepdims=True)
        part = col[:, :HEAD_DIM]
        for c in range(1, D_MODEL // HEAD_DIM):
            part = part + col[:, _hs(c)]
        accs[0][...] += part * (0.5 / D_MODEL)

    dy, loss_part = _rowwise(loss_body, seq, [(y, True), (target, True)], [(D_MODEL, F32)],
                             [((1, HEAD_DIM), F32)], name="loss")

    gw, gg = {}, {}
    dmixed_b = _mm(dy, w["b_out"], tb=True, name="b_out_dx")
    gw["b_out"] = _mm(mixed_b, dy, ta=True, name="b_out_dw")
    datt, dgate_b, dmk1, dmv1, gq1 = _mix_bwd(dmixed_b, att, p2, Q_LORA, mem1[2], mem1[3],
                                              g["g_mem_q"][1:2], name="b_mix_bwd")
    dq_cat, dk_cat, dv_mla = _mla_bwd(q_cat, k_cat, v_mla, att, lse, datt, name="mla_bwd")

    def qcat_bwd_body(ins, outs, accs):
        q_ref, dq_ref, gn_ref, gr_ref, c_ref, s_ref = ins
        (o_ref,) = outs
        dgn_ref, dgr_ref = accs
        for h in range(N_MLA_HEADS):
            dx, dg = _rms_bwd(q_ref[:, h * CAT_W:h * CAT_W + HEAD_DIM], gn_ref[...],
                              dq_ref[:, h * CAT_W:h * CAT_W + HEAD_DIM])
            o_ref[:, h * CAT_W:h * CAT_W + HEAD_DIM] = dx.astype(BF16)
            dgn_ref[...] += dg
            dn = _rope_bwd(dq_ref[:, h * CAT_W + HEAD_DIM:(h + 1) * CAT_W], c_ref[...], s_ref[...])
            dx, dg = _rms_bwd(q_ref[:, h * CAT_W + HEAD_DIM:(h + 1) * CAT_W], gr_ref[...], dn, n=ROPE_DIM)
            o_ref[:, h * CAT_W + HEAD_DIM:(h + 1) * CAT_W] = dx.astype(BF16)
            dgr_ref[...] += dg

    dq_up, gg["b_g_q_nope"], dgqr = _rowwise(
        qcat_bwd_body, seq,
        [(q_up, True), (dq_cat, True), (g["b_g_q_nope"], False), (g_q_rope, False), (cos_t, True), (sin_t, True)],
        [(N_MLA_HEADS * CAT_W, BF16)], [((1, HEAD_DIM), F32), ((1, HEAD_DIM), F32)], name="q_prep_bwd")
    gg["b_g_q_rope"] = dgqr
    dq_l = _mm(dq_up, w["uq"], tb=True, name="uq_dx")
    gw["uq"] = _mm(q_l, dq_up, ta=True, n_split=N_CHIPS, name="uq_dw")

    def qlat_bwd_body(ins, outs, accs):
        p2_ref, dql_ref, dgate_ref, gl_ref = ins
        dx, dg = _rms_bwd(p2_ref[:, :Q_LORA], gl_ref[...], dql_ref[...])
        outs[0][:, :Q_LORA] = dx.astype(BF16)
        outs[0][:, Q_LORA:] = dgate_ref[...]
        accs[0][...] += dg

    dp2, gg["b_g_q_lat"] = _rowwise(
        qlat_bwd_body, seq, [(p2, True), (dq_l, True), (dgate_b, True), (g["b_g_q_lat"], False)],
        [(Q_LORA + GATE_W, BF16)], [((1, Q_LORA), F32)], name="q_lat_norm_bwd")
    dh_b = _mm(dp2, w["b_in"], tb=True, name="b_in_dx")
    gw["b_in"] = _mm(h_b, dp2, ta=True, n_split=N_CHIPS, name="b_in_dw")

    def kcat_bwd_body(ins, outs, accs):
        kv_ref, dkc_ref, dv_ref, gk_ref = ins
        dkv_ref, dkr_ref = outs
        (dgk_ref,) = accs
        dkr = jnp.zeros(dkr_ref.shape, F32)
        for h in range(N_MLA_HEADS):
            dx, dg = _rms_bwd(kv_ref[:, h * CAT_W:h * CAT_W + HEAD_DIM], gk_ref[...],
                              dkc_ref[:, h * CAT_W:h * CAT_W + HEAD_DIM])
            dkv_ref[:, h * CAT_W:h * CAT_W + HEAD_DIM] = dx.astype(BF16)
            dgk_ref[...] += dg
            dkv_ref[:, h * CAT_W + HEAD_DIM:(h + 1) * CAT_W] = dv_ref[:, _hs(h)].astype(BF16)
            dkr = dkr + dkc_ref[:, h * CAT_W + HEAD_DIM:(h + 1) * CAT_W]
        dkr_ref[...] = dkr

    dkv, dk_r, gg["g_k_nope"] = _rowwise(
        kcat_bwd_body, seq, [(kv, True), (dk_cat, True), (dv_mla, True), (g["g_k_nope"], False)],
        [(N_MLA_HEADS * CAT_W, BF16), (HEAD_DIM, F32)], [((1, HEAD_DIM), F32)], name="k_prep_bwd")
    dc_n = _mm(dkv, w["ukv"], tb=True, name="ukv_dx")
    gw["ukv"] = _mm(c_n, dkv, ta=True, n_split=N_CHIPS, name="ukv_dw")

    def ckr_bwd_body(ins, outs, accs):
        ckr_ref, dcn_ref, dkr_ref, gc_ref, gr_ref, c_ref, s_ref = ins
        dx, dg = _rms_bwd(ckr_ref[:, :KV_LORA], gc_ref[...], dcn_ref[...])
        outs[0][:, :KV_LORA] = dx.astype(BF16)
        accs[0][...] += dg
        dn = _rope_bwd(dkr_ref[...], c_ref[...], s_ref[...])
        dx, dg = _rms_bwd(ckr_ref[:, KV_LORA:], gr_ref[...], dn, n=ROPE_DIM)
        outs[0][:, KV_LORA:] = dx.astype(BF16)
        accs[1][...] += dg

    dckr, gg["g_ckv"], gg["g_k_rope"] = _rowwise(
        ckr_bwd_body, seq,
        [(ckr, True), (dc_n, True), (dk_r, True), (g["g_ckv"], False), (g_k_rope, False),
         (cos_t, True), (sin_t, True)],
        [(KV_LORA + HEAD_DIM, BF16)], [((1, KV_LORA), F32), ((1, HEAD_DIM), F32)], name="ckv_prep_bwd")
    dh_kv = _mm(dckr, w["dkv"], tb=True, name="dkv_dx")
    gw["dkv"] = _mm(h_kv, dckr, ta=True, name="dkv_dw")

    def norms2_bwd_body(ins, outs, accs):
        x_ref, dy_ref, dhk_ref, dhb_ref, gk_ref, gb_ref = ins
        xv = x_ref[...]
        dxk, dgk = _rms_bwd(xv, gk_ref[...], dhk_ref[...])
        dxb, dgb = _rms_bwd(xv, gb_ref[...], dhb_ref[...])
        outs[0][...] = dy_ref[...] + dxk + dxb
        accs[0][...] += dgk
        accs[1][...] += dgb

    dx1, gg["kv_norm"], gg["b_norm"] = _rowwise(
        norms2_bwd_body, seq,
        [(x1, True), (dy, True), (dh_kv, True), (dh_b, True), (g["kv_norm"], False), (g["b_norm"], False)],
        [(D_MODEL, F32)], [((1, D_MODEL), F32), ((1, D_MODEL), F32)], name="kv_b_norm_bwd")

    dmixed_a = _mm(dx1, w["a_out"], tb=True, name="a_out_dx")
    gw["a_out"] = _mm(mixed_a, dx1, ta=True, name="a_out_dw")
    dsb, dgate_a, dmk0, dmv0, gq0 = _mix_bwd(dmixed_a, sb, gr, 0, mem0[2], mem0[3],
                                             g["g_mem_q"][0:1], name="a_mix_bwd")
    dq, dk, dv = _sb_bwd(qkv, dsb, name="sb_bwd")
    dp_a = jnp.concatenate([dq.astype(BF16), dk.astype(BF16), dv.astype(BF16), dgate_a], axis=1)
    dh_a = _mm(dp_a, w["a_in"], tb=True, name="a_in_dx")
    gw["a_in"] = _mm(h_a, dp_a, ta=True, n_split=N_CHIPS, name="a_in_dw")

    def norm_a_bwd_body(ins, outs, accs):
        dx, dg = _rms_bwd(ins[0][...], ins[3][...], ins[2][...])
        outs[0][...] = ins[1][...] + dx
        accs[0][...] += dg

    grad_x, gg["a_norm"] = _rowwise(
        norm_a_bwd_body, seq, [(x, True), (dx1, True), (dh_a, True), (g["a_norm"], False)],
        [(D_MODEL, F32)], [((1, D_MODEL), F32)], name="a_norm_bwd")

    dw0, dgn0, dgk0 = _mem_side_bwd(mem, g["mem_norm"][0:1], w["mem_kv"][0], g["g_mem_k"][0:1],
                                    mem0[0], mem0[1], dmk0, dmv0, tag="a")
    dw1, dgn1, dgk1 = _mem_side_bwd(mem, g["mem_norm"][1:2], w["mem_kv"][1], g["g_mem_k"][1:2],
                                    mem1[0], mem1[1], dmk1, dmv1, tag="b")
    gw["mem_kv"] = (dw0, dw1)
    gg["mem_norm"] = jnp.concatenate([dgn0, dgn1], axis=0)
    gg["g_mem_q"] = jnp.concatenate([gq0, gq1], axis=0)
    gg["g_mem_k"] = jnp.concatenate([dgk0, dgk1], axis=0)
    return loss_part, grad_x, gw, gg


HBM_SPEC = pl.BlockSpec(memory_space=pl.ANY)


def _other_chips():
    x, y = lax.axis_index("x"), lax.axis_index("y")
    return [(1 - x, y), (x, 1 - y), (1 - x, 1 - y)]


def _allgather_chips(shards):
    n = len(shards)

    def body(*refs):
        ins, outs = refs[:n], refs[n:2 * n]
        send, recv, loc = refs[2 * n:]
        c = lax.axis_index("c")
        me = 2 * lax.axis_index("x") + lax.axis_index("y")
        copies = []
        for wi in range(n):
            own = pltpu.make_async_copy(ins[wi], outs[wi].at[me], loc.at[wi])
            own.start()
            copies.append(own)
            for k, (tx, ty) in enumerate(_other_chips()):
                cp = pltpu.make_async_remote_copy(
                    src_ref=ins[wi], dst_ref=outs[wi].at[me], send_sem=send.at[wi, k],
                    recv_sem=recv.at[wi, k], device_id=(tx, ty, c), device_id_type=MESH)
                cp.start()
                copies.append(cp)
        for cp in copies:
            cp.wait()

    return pl.pallas_call(
        body, out_shape=[jax.ShapeDtypeStruct((N_CHIPS,) + s.shape, s.dtype) for s in shards],
        in_specs=[HBM_SPEC] * n, out_specs=[HBM_SPEC] * n,
        scratch_shapes=[pltpu.SemaphoreType.DMA((n, 3)), pltpu.SemaphoreType.DMA((n, 3)),
                        pltpu.SemaphoreType.DMA((n,))],
        name="allgather_weights",
    )(*shards)


def _scatter_to_chips(grads):
    n = len(grads)

    def body(*refs):
        ins, outs = refs[:n], refs[n:2 * n]
        send, recv, loc = refs[2 * n:]
        c = lax.axis_index("c")
        me = 2 * lax.axis_index("x") + lax.axis_index("y")
        copies = []
        for wi in range(n):
            own = pltpu.make_async_copy(ins[wi].at[me], outs[wi].at[3], loc.at[wi])
            own.start()
            copies.append(own)
            for k, (tx, ty) in enumerate(_other_chips()):
                cp = pltpu.make_async_remote_copy(
                    src_ref=ins[wi].at[2 * tx + ty], dst_ref=outs[wi].at[k], send_sem=send.at[wi, k],
                    recv_sem=recv.at[wi, k], device_id=(tx, ty, c), device_id_type=MESH)
                cp.start()
                copies.append(cp)
        for cp in copies:
            cp.wait()

    return pl.pallas_call(
        body, out_shape=[jax.ShapeDtypeStruct(s.shape, s.dtype) for s in grads],
        in_specs=[HBM_SPEC] * n, out_specs=[HBM_SPEC] * n,
        scratch_shapes=[pltpu.SemaphoreType.DMA((n, 3)), pltpu.SemaphoreType.DMA((n, 3)),
                        pltpu.SemaphoreType.DMA((n,))],
        name="scatter_grads",
    )(*grads)


def _swap_with_sibling(parts):
    n = len(parts)

    def body(*refs):
        ins, outs = refs[:n], refs[n:2 * n]
        send, recv = refs[2 * n:]
        sib = (lax.axis_index("x"), lax.axis_index("y"), 1 - lax.axis_index("c"))
        copies = []
        for wi in range(n):
            cp = pltpu.make_async_remote_copy(
                src_ref=ins[wi], dst_ref=outs[wi], send_sem=send.at[wi], recv_sem=recv.at[wi],
                device_id=sib, device_id_type=MESH)
            cp.start()
            copies.append(cp)
        for cp in copies:
            cp.wait()

    return pl.pallas_call(
        body, out_shape=[jax.ShapeDtypeStruct(s.shape, s.dtype) for s in parts],
        in_specs=[HBM_SPEC] * n, out_specs=[HBM_SPEC] * n,
        scratch_shapes=[pltpu.SemaphoreType.DMA((n,)), pltpu.SemaphoreType.DMA((n,))],
        name="swap_partial_grads",
    )(*parts)


def _allreduce_small(vec, loss_row):
    rows = vec.shape[0]

    def body(v_ref, o_ref, buf, send, recv):
        x, y, c = lax.axis_index("x"), lax.axis_index("y"), lax.axis_index("c")
        me = 4 * x + 2 * y + c
        buf[me] = v_ref[...]
        copies = []
        for r in range(1, N_DEV):
            peer = (x ^ ((r >> 2) & 1), y ^ ((r >> 1) & 1), c ^ (r & 1))
            cp = pltpu.make_async_remote_copy(
                src_ref=v_ref, dst_ref=buf.at[me], send_sem=send.at[r - 1], recv_sem=recv.at[r - 1],
                device_id=peer, device_id_type=MESH)
            cp.start()
            copies.append(cp)
        for cp in copies:
            cp.wait()
        total = buf[0]
        for d in range(1, N_DEV):
            total = total + buf[d]
        o_ref[...] = total
        o_ref[loss_row:loss_row + 1, :] = jnp.broadcast_to(
            jnp.sum(total[loss_row:loss_row + 1, :], axis=-1, keepdims=True), (1, HEAD_DIM))

    return pl.pallas_call(
        body, out_shape=jax.ShapeDtypeStruct(vec.shape, F32),
        in_specs=[pl.BlockSpec(memory_space=pltpu.VMEM)], out_specs=pl.BlockSpec(memory_space=pltpu.VMEM),
        scratch_shapes=[pltpu.VMEM((N_DEV, rows, HEAD_DIM), F32),
                        pltpu.SemaphoreType.DMA((N_DEV - 1,)), pltpu.SemaphoreType.DMA((N_DEV - 1,))],
        name="allreduce_gains",
    )(vec)


def _sum_slots(r, *, name):
    _, rows, width = r.shape
    blk = _pick(rows, (256, 128, 64, 32, 16, 8))

    def body(r_ref, o_ref):
        o_ref[...] = ((r_ref[3] + r_ref[0]) + r_ref[1]) + r_ref[2]

    return pl.pallas_call(
        body, out_shape=jax.ShapeDtypeStruct((rows, width), F32), grid=(rows // blk,),
        in_specs=[pl.BlockSpec((N_CHIPS, blk, width), lambda i: (0, i, 0))],
        out_specs=pl.BlockSpec((blk, width), lambda i: (i, 0)),
        name=name, compiler_params=_params("parallel"),
    )(r)


def _adamw(wgt, grads, m, v, *, name):
    rows, width = wgt.shape
    blk = _pick(rows, (256, 128, 64, 32, 16, 8))
    n_g = len(grads)

    def body(*refs):
        w_ref, m_ref, v_ref = refs[0], refs[1 + n_g], refs[2 + n_g]
        g_out, d_out, m_out, v_out = refs[3 + n_g:]
        grad = refs[1][...]
        for t in range(1, n_g):
            grad = grad + refs[1 + t][...]
        m_new = ADAM_B1 * m_ref[...] + (1.0 - ADAM_B1) * grad
        v_new = ADAM_B2 * v_ref[...] + (1.0 - ADAM_B2) * (grad * grad)
        m_hat = m_new / (1.0 - ADAM_B1 ** ADAM_STEP)
        v_hat = v_new / (1.0 - ADAM_B2 ** ADAM_STEP)
        g_out[...] = grad
        d_out[...] = -ADAM_LR * (m_hat / (jnp.sqrt(v_hat) + ADAM_EPS) + ADAM_WD * w_ref[...])
        m_out[...] = m_new
        v_out[...] = v_new

    spec = pl.BlockSpec((blk, width), lambda i: (i, 0))
    out = jax.ShapeDtypeStruct((rows, width), F32)
    return pl.pallas_call(
        body, out_shape=[out] * 4, grid=(rows // blk,), in_specs=[spec] * (3 + n_g),
        out_specs=[spec] * 4, name=name, compiler_params=_params("parallel"),
    )(wgt, *grads, m, v)


_SMALL = (("a_norm", 2048), ("kv_norm", 2048), ("g_ckv", 512), ("g_k_nope", 128), ("g_k_rope", 64),
          ("b_norm", 2048), ("b_g_q_lat", 512), ("b_g_q_nope", 128), ("b_g_q_rope", 64),
          ("mem_norm", 4096), ("g_mem_q", 256), ("g_mem_k", 256))


def _lanes(n):
    return -(-n // HEAD_DIM) * HEAD_DIM


def _pack_rows(pieces, pad_rows_to=8):
    flat = jnp.concatenate(pieces, axis=1)
    rows = flat.shape[1] // HEAD_DIM
    pad = (-rows) % pad_rows_to
    if pad:
        flat = jnp.concatenate([flat, jnp.zeros((1, pad * HEAD_DIM), F32)], axis=1)
    return flat.reshape(rows + pad, HEAD_DIM)


def _pad_lanes(a):
    a = a.reshape(1, -1)
    pad = _lanes(a.shape[1]) - a.shape[1]
    if pad:
        a = jnp.concatenate([a, jnp.zeros((1, pad), F32)], axis=1)
    return a


def kernel(x, mem, positions, a_norm, a_w_in, a_w_out, kv_norm, w_dkv, g_ckv, w_ukv, g_k_nope, g_k_rope, b_norm, b_w_in, b_g_q_lat, b_w_uq, b_g_q_nope, b_g_q_rope, b_w_out, mem_norm, w_mem_kv, g_mem_q, g_mem_k, loss_target, m_a_norm, m_a_w_in, m_a_w_out, m_kv_norm, m_w_dkv, m_g_ckv, m_w_ukv, m_g_k_nope, m_g_k_rope, m_b_norm, m_b_w_in, m_b_g_q_lat, m_b_w_uq, m_b_g_q_nope, m_b_g_q_rope, m_b_w_out, m_mem_norm, m_w_mem_kv, m_g_mem_q, m_g_mem_k, v_a_norm, v_a_w_in, v_a_w_out, v_kv_norm, v_w_dkv, v_g_ckv, v_w_ukv, v_g_k_nope, v_g_k_rope, v_b_norm, v_b_w_in, v_b_g_q_lat, v_b_w_uq, v_b_g_q_nope, v_b_g_q_rope, v_b_w_out, v_mem_norm, v_w_mem_kv, v_g_mem_q, v_g_mem_k):
    chip = 2 * lax.axis_index("x") + lax.axis_index("y")
    rows_dkv = D_MODEL // N_CHIPS
    heads_per_chip = N_MLA_HEADS // N_CHIPS
    qk_w = HEAD_DIM + ROPE_DIM

    big = {"a_in": a_w_in[0], "a_out": a_w_out[0], "dkv": w_dkv, "ukv": w_ukv, "b_in": b_w_in[0],
           "uq": b_w_uq[0], "b_out": b_w_out[0], "mem_kv": w_mem_kv.reshape(2 * rows_dkv, 2 * MEM_W)}
    big_m = {"a_in": m_a_w_in[0], "a_out": m_a_w_out[0], "dkv": m_w_dkv, "ukv": m_w_ukv, "b_in": m_b_w_in[0],
             "uq": m_b_w_uq[0], "b_out": m_b_w_out[0], "mem_kv": m_w_mem_kv.reshape(2 * rows_dkv, 2 * MEM_W)}
    big_v = {"a_in": v_a_w_in[0], "a_out": v_a_w_out[0], "dkv": v_w_dkv, "ukv": v_w_ukv, "b_in": v_b_w_in[0],
             "uq": v_b_w_uq[0], "b_out": v_b_w_out[0], "mem_kv": v_w_mem_kv.reshape(2 * rows_dkv, 2 * MEM_W)}
    names = list(big)
    gathered = _allgather_chips([big[n].astype(BF16) for n in names] + [a_norm])
    st = dict(zip(names, gathered[:-1]))
    a_in_full = st["a_in"].transpose(1, 0, 2).reshape(D_MODEL, QKV_W + GATE_W)
    uq = st["uq"].reshape(N_CHIPS, Q_LORA, heads_per_chip, qk_w)
    uq = jnp.pad(uq, ((0, 0), (0, 0), (0, 0), (0, CAT_W - qk_w)))
    w = {
        "a_in": a_in_full,
        "a_in_qkv": a_in_full[:, :QKV_W],
        "a_in_gate": a_in_full[:, QKV_W:],
        "a_out": st["a_out"].reshape(D_MODEL, D_MODEL),
        "dkv": jnp.pad(st["dkv"].reshape(D_MODEL, KV_LORA + ROPE_DIM), ((0, 0), (0, HEAD_DIM - ROPE_DIM))),
        "ukv": st["ukv"].transpose(1, 0, 2).reshape(KV_LORA, N_MLA_HEADS * CAT_W),
        "b_in": st["b_in"].transpose(1, 0, 2).reshape(D_MODEL, Q_LORA + GATE_W),
        "uq": uq.transpose(1, 0, 2, 3).reshape(Q_LORA, N_MLA_HEADS * CAT_W),
        "b_out": st["b_out"].reshape(D_MODEL, D_MODEL),
        "mem_kv": st["mem_kv"].reshape(N_CHIPS, 2, rows_dkv, 2 * MEM_W).transpose(1, 0, 2, 3).reshape(
            2, D_MODEL, 2 * MEM_W),
    }
    gains = {
        "a_norm": gathered[-1].reshape(1, D_MODEL), "kv_norm": kv_norm.reshape(1, -1),
        "g_ckv": g_ckv.reshape(1, -1), "g_k_nope": g_k_nope.reshape(1, -1), "g_k_rope": g_k_rope.reshape(1, -1),
        "b_norm": b_norm, "b_g_q_lat": b_g_q_lat, "b_g_q_nope": b_g_q_nope, "b_g_q_rope": b_g_q_rope,
        "mem_norm": mem_norm, "g_mem_q": g_mem_q, "g_mem_k": g_mem_k,
    }

    loss_part, grad_x, gw, gg = _local_step(x[0], mem[0], positions[0], loss_target[0], w, gains)

    stacked = {
        "a_in": gw["a_in"],
        "a_out": gw["a_out"].reshape(N_CHIPS, rows_dkv, D_MODEL),
        "dkv": gw["dkv"][:, :KV_LORA + ROPE_DIM].reshape(N_CHIPS, rows_dkv, KV_LORA + ROPE_DIM),
        "ukv": gw["ukv"],
        "b_in": gw["b_in"],
        "uq": gw["uq"].reshape(N_CHIPS, Q_LORA, heads_per_chip, CAT_W)[..., :qk_w].reshape(
            N_CHIPS, Q_LORA, heads_per_chip * qk_w),
        "b_out": gw["b_out"].reshape(N_CHIPS, rows_dkv, D_MODEL),
        "mem_kv": jnp.stack([gw["mem_kv"][0].reshape(N_CHIPS, rows_dkv, 2 * MEM_W),
                             gw["mem_kv"][1].reshape(N_CHIPS, rows_dkv, 2 * MEM_W)], axis=1).reshape(
            N_CHIPS, 2 * rows_dkv, 2 * MEM_W),
    }
    received = _scatter_to_chips([stacked[n] for n in names])
    partial = [_sum_slots(r, name=f"sum_slots_{n}") for n, r in zip(names, received)]
    sibling = _swap_with_sibling(partial)
    big_out = {}
    for n, mine, theirs in zip(names, partial, sibling):
        big_out[n] = _adamw(big[n], [mine, theirs], big_m[n], big_v[n], name=f"adamw_{n}")

    pieces = [_pad_lanes(gg[n]) if n not in ("g_k_rope", "b_g_q_rope") else gg[n] for n, _ in _SMALL]
    pieces.append(loss_part)
    loss_row = sum(_lanes(size) for _, size in _SMALL) // HEAD_DIM
    summed = _allreduce_small(_pack_rows(pieces), loss_row)
    flat = summed.reshape(1, -1)
    small_g, off = {}, 0
    for n, size in _SMALL:
        small_g[n] = flat[:, off:off + size]
        off += _lanes(size)
    loss = flat[0, off]
    small_g["a_norm"] = lax.dynamic_slice(small_g["a_norm"], (0, chip * rows_dkv), (1, rows_dkv))

    small_w = {"a_norm": a_norm, "kv_norm": kv_norm, "g_ckv": g_ckv, "g_k_nope": g_k_nope, "g_k_rope": g_k_rope,
               "b_norm": b_norm, "b_g_q_lat": b_g_q_lat, "b_g_q_nope": b_g_q_nope, "b_g_q_rope": b_g_q_rope,
               "mem_norm": mem_norm, "g_mem_q": g_mem_q, "g_mem_k": g_mem_k}
    small_m = {"a_norm": m_a_norm, "kv_norm": m_kv_norm, "g_ckv": m_g_ckv, "g_k_nope": m_g_k_nope,
               "g_k_rope": m_g_k_rope, "b_norm": m_b_norm, "b_g_q_lat": m_b_g_q_lat, "b_g_q_nope": m_b_g_q_nope,
               "b_g_q_rope": m_b_g_q_rope, "mem_norm": m_mem_norm, "g_mem_q": m_g_mem_q, "g_mem_k": m_g_mem_k}
    small_v = {"a_norm": v_a_norm, "kv_norm": v_kv_norm, "g_ckv": v_g_ckv, "g_k_nope": v_g_k_nope,
               "g_k_rope": v_g_k_rope, "b_norm": v_b_norm, "b_g_q_lat": v_b_g_q_lat, "b_g_q_nope": v_b_g_q_nope,
               "b_g_q_rope": v_b_g_q_rope, "mem_norm": v_mem_norm, "g_mem_q": v_g_mem_q, "g_mem_k": v_g_mem_k}
    snames = [n for n, _ in _SMALL]
    packs = [_pack_rows([_pad_lanes(src[n]) for n in snames])
             for src in (small_w, small_g, small_m, small_v)]
    small_res = _adamw(packs[0], [packs[1]], packs[2], packs[3], name="adamw_gains")
    small_out = {n: [] for n in snames}
    for res in small_res:
        flat_r = res.reshape(1, -1)
        off = 0
        for n in snames:
            size = small_w[n].size
            small_out[n].append(flat_r[:, off:off + size].reshape(small_w[n].shape))
            off += _lanes(size)

    big_names = {"a_w_in": ("a_in", a_w_in), "a_w_out": ("a_out", a_w_out), "w_dkv": ("dkv", w_dkv),
                 "w_ukv": ("ukv", w_ukv), "b_w_in": ("b_in", b_w_in), "b_w_uq": ("uq", b_w_uq),
                 "b_w_out": ("b_out", b_w_out), "w_mem_kv": ("mem_kv", w_mem_kv)}
    order = ["a_norm", "a_w_in", "a_w_out", "kv_norm", "w_dkv", "g_ckv", "w_ukv", "g_k_nope", "g_k_rope",
             "b_norm", "b_w_in", "b_g_q_lat", "b_w_uq", "b_g_q_nope", "b_g_q_rope", "b_w_out", "mem_norm",
             "w_mem_kv", "g_mem_q", "g_mem_k"]
    groups = [[], [], [], []]
    for n in order:
        if n in big_names:
            key, ref_arr = big_names[n]
            for t in range(4):
                groups[t].append(big_out[key][t].reshape(ref_arr.shape))
        else:
            for t in range(4):
                groups[t].append(small_out[n][t])
    return (loss, grad_x[None], *groups[0], *groups[1], *groups[2], *groups[3])
```

```python
import functools

import jax
import jax.numpy as jnp
from jax import lax
from jax.experimental import pallas as pl
from jax.experimental.pallas import tpu as pltpu

F32 = jnp.float32
BF16 = jnp.bfloat16
MESH = pl.DeviceIdType.MESH

D_MODEL = 2048
HEAD_DIM = 128
N_SB_HEADS = 12
N_MEM_HEADS = 4
N_MLA_HEADS = 12
MEM_LEN = 256
Q_LORA = 512
KV_LORA = 512
ROPE_DIM = 64
SB_W = N_SB_HEADS * HEAD_DIM
MEM_W = N_MEM_HEADS * HEAD_DIM
MLA_W = N_MLA_HEADS * HEAD_DIM
QKV_W = 3 * SB_W
GATE_W = SB_W + 2 * MEM_W
CAT_W = 2 * HEAD_DIM
ROPE_THETA = 10000.0
EPS = 1e-6
N_CHIPS = 4
N_DEV = 8

ADAM_LR = 0.001
ADAM_B1 = 0.9
ADAM_B2 = 0.999
ADAM_EPS = 1e-08
ADAM_WD = 0.01
ADAM_STEP = 10

VMEM_LIMIT_BYTES = 56 * 1024 * 1024
ROW_BLOCK = 256
ATT_BLOCK = 256


def _params(*sem):
    return pltpu.CompilerParams(dimension_semantics=sem, vmem_limit_bytes=VMEM_LIMIT_BYTES)


def _pick(n, cands):
    for c in cands:
        if n % c == 0:
            return c
    return n


def _mm(a, b, *, name, ta=False, tb=False, out_dtype=F32, res=None, n_split=1, scale_cols=None):
    if ta:
        k_dim, m_dim = a.shape
    else:
        m_dim, k_dim = a.shape
    if tb:
        n_dim, kb = b.shape
    else:
        kb, n_dim = b.shape
    assert kb == k_dim, (a.shape, b.shape)
    n_per = n_dim // n_split
    bm = m_dim if m_dim <= 1024 else _pick(m_dim, (1024, 512, 256))
    bn = n_per if n_per <= 1024 else _pick(n_per, (1024, 896, 768, 640, 512, 256, 128))
    bk = k_dim if k_dim <= 1024 else _pick(k_dim, (512, 256, 128))
    nk = k_dim // bk
    nb_per = n_per // bn
    grid = (m_dim // bm, n_dim // bn, nk)
    a_spec = (pl.BlockSpec((bk, bm), lambda i, j, k: (k, i)) if ta
              else pl.BlockSpec((bm, bk), lambda i, j, k: (i, k)))
    b_spec = (pl.BlockSpec((bn, bk), lambda i, j, k: (j, k)) if tb
              else pl.BlockSpec((bk, bn), lambda i, j, k: (k, j)))
    dims = (((0 if ta else 1,), (1 if tb else 0,)), ((), ()))
    in_specs = [a_spec, b_spec]
    args = [a, b]
    if res is not None:
        in_specs.append(pl.BlockSpec((bm, bn), lambda i, j, k: (i, j)))
        args.append(res)
    if n_split == 1:
        out_shape = jax.ShapeDtypeStruct((m_dim, n_dim), out_dtype)
        out_spec = pl.BlockSpec((bm, bn), lambda i, j, k: (i, j))
    else:
        out_shape = jax.ShapeDtypeStruct((n_split, m_dim, n_per), out_dtype)
        out_spec = pl.BlockSpec((None, bm, bn), lambda i, j, k: (j // nb_per, i, j % nb_per))

    def body(*refs):
        if res is None:
            a_ref, b_ref, o_ref, acc = refs
            r_ref = None
        else:
            a_ref, b_ref, r_ref, o_ref, acc = refs
        k = pl.program_id(2)
        col_block = pl.program_id(1)

        @pl.when(k == 0)
        def _():
            acc[...] = jnp.zeros_like(acc)

        acc[...] += lax.dot_general(a_ref[...].astype(BF16), b_ref[...].astype(BF16), dims,
                                    preferred_element_type=F32)

        @pl.when(k == nk - 1)
        def _():
            r = acc[...]
            if r_ref is not None:
                r = r + r_ref[...]
            if scale_cols is not None:
                assert scale_cols[0] % bn == 0
                r = r * jnp.where(col_block < scale_cols[0] // bn, scale_cols[1], 1.0)
            o_ref[...] = r.astype(out_dtype)

    return pl.pallas_call(
        body, out_shape=out_shape, grid=grid, in_specs=in_specs, out_specs=out_spec,
        scratch_shapes=[pltpu.VMEM((bm, bn), F32)], name=name,
        compiler_params=_params("parallel", "parallel", "arbitrary"),
    )(*args)


def _rowwise(body, n_rows, ins, outs, accs=(), *, name, block=ROW_BLOCK):
    blk = min(block, n_rows)
    assert n_rows % blk == 0
    in_specs = []
    for arr, is_row in ins:
        if is_row:
            assert arr.shape[0] == n_rows, (name, arr.shape, n_rows)
            in_specs.append(pl.BlockSpec((blk, arr.shape[1]), lambda i: (i, 0)))
        else:
            in_specs.append(pl.BlockSpec(arr.shape, lambda i, nd=arr.ndim: (0,) * nd))
    out_shape = [jax.ShapeDtypeStruct((n_rows, w), dt) for w, dt in outs]
    out_specs = [pl.BlockSpec((blk, w), lambda i: (i, 0)) for w, _ in outs]
    out_shape += [jax.ShapeDtypeStruct(s, dt) for s, dt in accs]
    out_specs += [pl.BlockSpec(s, lambda i, nd=len(s): (0,) * nd) for s, _ in accs]
    n_in, n_out, n_acc = len(ins), len(outs), len(accs)

    def kern(*refs):
        in_refs = refs[:n_in]
        out_refs = refs[n_in:n_in + n_out]
        acc_refs = refs[n_in + n_out:]
        if n_acc:
            @pl.when(pl.program_id(0) == 0)
            def _():
                for r in acc_refs:
                    r[...] = jnp.zeros_like(r)
        body(in_refs, out_refs, acc_refs)

    return pl.pallas_call(
        kern, out_shape=out_shape, grid=(n_rows // blk,), in_specs=in_specs, out_specs=out_specs,
        name=name, compiler_params=_params("arbitrary"),
    )(*[arr for arr, _ in ins])


def _rms(x, g, n=None):
    n = x.shape[-1] if n is None else n
    r = lax.rsqrt(jnp.sum(x * x, axis=-1, keepdims=True) / n + EPS)
    return x * r * g


def _rms_bwd(x, g, dy, n=None):
    n = x.shape[-1] if n is None else n
    r = lax.rsqrt(jnp.sum(x * x, axis=-1, keepdims=True) / n + EPS)
    gdy = dy * g
    dx = r * (gdy - x * ((r * r) * (jnp.sum(gdy * x, axis=-1, keepdims=True) / n)))
    dg = jnp.sum(dy * x * r, axis=0, keepdims=True)
    return dx, dg


def _swap_halves(x):
    lane = lax.broadcasted_iota(jnp.int32, x.shape, 1)
    return jnp.where(lane < ROPE_DIM // 2, pltpu.roll(x, 128 - ROPE_DIM // 2, 1),
                     pltpu.roll(x, ROPE_DIM // 2, 1))


def _rope(n, cos_t, sin_t):
    return n * cos_t + _swap_halves(n) * sin_t


def _rope_bwd(dy, cos_t, sin_t):
    return dy * cos_t - _swap_halves(dy) * sin_t


def _sigmoid(g):
    return 1.0 / (1.0 + jnp.exp(-g))


def _dot_t(a, b):
    return lax.dot_general(a, b, (((1,), (1,)), ((), ())), preferred_element_type=F32)


def _tdot(a, b):
    return lax.dot_general(a, b, (((0,), (0,)), ((), ())), preferred_element_type=F32)


def _dot(a, b):
    return jnp.dot(a, b, preferred_element_type=F32)


def _hs(h, w=HEAD_DIM, base=0):
    return slice(base + h * w, base + (h + 1) * w)


def _mem_head(qm, gq, mk_h, mv_h):
    qb = _rms(qm, gq).astype(BF16)
    s = _dot_t(qb, mk_h) * (HEAD_DIM ** -0.5)
    e = jnp.exp(s - jnp.max(s, axis=-1, keepdims=True))
    p = e / jnp.sum(e, axis=-1, keepdims=True)
    mo = _dot(p.astype(BF16), mv_h)
    return qb, p, mo


def _mix_fwd(att, gates, c0, mk, mv, gq, *, name):
    n_rows = att.shape[0]

    def body(ins, outs, _):
        att_ref, g_ref, mk_ref, mv_ref, gq_ref = ins
        (o_ref,) = outs
        g = g_ref[:, c0:c0 + SB_W]
        o_ref[:, :SB_W] = (att_ref[...] * (g * _sigmoid(g))).astype(BF16)
        for h in range(N_MEM_HEADS):
            qm = g_ref[:, _hs(h, base=c0 + SB_W)]
            gm = g_ref[:, _hs(h, base=c0 + SB_W + MEM_W)]
            _, _, mo = _mem_head(qm, gq_ref[...], mk_ref[:, _hs(h)], mv_ref[:, _hs(h)])
            o_ref[:, _hs(h, base=SB_W)] = (mo * (gm * _sigmoid(gm))).astype(BF16)

    (mixed,) = _rowwise(body, n_rows,
                        [(att, True), (gates, True), (mk, False), (mv, False), (gq, False)],
                        [(D_MODEL, BF16)], name=name)
    return mixed


def _mix_bwd(dmixed, att, gates, c0, mk, mv, gq, *, name):
    n_rows = att.shape[0]
    scale = HEAD_DIM ** -0.5

    def body(ins, outs, accs):
        dm_ref, att_ref, g_ref, mk_ref, mv_ref, gq_ref = ins
        datt_ref, dg_ref = outs
        dmk_ref, dmv_ref, dgq_ref = accs
        g = g_ref[:, c0:c0 + SB_W]
        sg = _sigmoid(g)
        dm = dm_ref[:, :SB_W]
        datt_ref[...] = dm * (g * sg)
        dg_ref[:, :SB_W] = (dm * att_ref[...] * (sg * (1.0 + g * (1.0 - sg)))).astype(BF16)
        for h in range(N_MEM_HEADS):
            qm = g_ref[:, _hs(h, base=c0 + SB_W)]
            gm = g_ref[:, _hs(h, base=c0 + SB_W + MEM_W)]
            mk_h = mk_ref[:, _hs(h)]
            mv_h = mv_ref[:, _hs(h)]
            qb, p, mo = _mem_head(qm, gq_ref[...], mk_h, mv_h)
            sgm = _sigmoid(gm)
            dmh = dm_ref[:, _hs(h, base=SB_W)]
            dmo = dmh * (gm * sgm)
            dg_ref[:, _hs(h, base=SB_W + MEM_W)] = (
                dmh * mo * (sgm * (1.0 + gm * (1.0 - sgm)))).astype(BF16)
            dmo_b = dmo.astype(BF16)
            pb = p.astype(BF16)
            dp = _dot_t(dmo_b, mv_h)
            dmv_ref[:, _hs(h)] += _tdot(pb, dmo_b)
            ds = (p * (dp - jnp.sum(dp * p, axis=-1, keepdims=True)) * scale).astype(BF16)
            dqn = _dot(ds, mk_h)
            dmk_ref[:, _hs(h)] += _tdot(ds, qb)
            dqm, dgq = _rms_bwd(qm, gq_ref[...], dqn)
            dg_ref[:, _hs(h, base=SB_W)] = dqm.astype(BF16)
            dgq_ref[...] += dgq

    return _rowwise(body, n_rows,
                    [(dmixed, True), (att, True), (gates, True), (mk, False), (mv, False), (gq, False)],
                    [(SB_W, F32), (GATE_W, BF16)],
                    [((MEM_LEN, MEM_W), F32), ((MEM_LEN, MEM_W), F32), ((1, HEAD_DIM), F32)],
                    name=name)


def _mem_side_fwd(mem, g_norm, w_kv, g_k, *, tag):
    def norm_body(ins, outs, _):
        outs[0][...] = _rms(ins[0][...], ins[1][...]).astype(BF16)

    (mn,) = _rowwise(norm_body, MEM_LEN, [(mem, True), (g_norm, False)], [(D_MODEL, BF16)],
                     name=f"mem_norm_{tag}")
    mkv = _mm(mn, w_kv, name=f"mem_kv_{tag}")

    def kv_body(ins, outs, _):
        mkv_ref, gk_ref = ins
        mk_ref, mv_ref = outs
        for h in range(N_MEM_HEADS):
            mk_ref[:, _hs(h)] = _rms(mkv_ref[:, _hs(h)], gk_ref[...]).astype(BF16)
        mv_ref[...] = mkv_ref[:, MEM_W:].astype(BF16)

    mk, mv = _rowwise(kv_body, MEM_LEN, [(mkv, True), (g_k, False)], [(MEM_W, BF16), (MEM_W, BF16)],
                      name=f"mem_kv_prep_{tag}")
    return mn, mkv, mk, mv


def _mem_side_bwd(mem, g_norm, w_kv, g_k, mn, mkv, dmk, dmv, *, tag):
    def kv_body(ins, outs, accs):
        mkv_ref, gk_ref, dmk_ref, dmv_ref = ins
        (d_ref,) = outs
        (dgk_ref,) = accs
        for h in range(N_MEM_HEADS):
            dx, dg = _rms_bwd(mkv_ref[:, _hs(h)], gk_ref[...], dmk_ref[:, _hs(h)])
            d_ref[:, _hs(h)] = dx.astype(BF16)
            dgk_ref[...] += dg
        d_ref[:, MEM_W:] = dmv_ref[...].astype(BF16)

    dmkv, dgk = _rowwise(kv_body, MEM_LEN, [(mkv, True), (g_k, False), (dmk, True), (dmv, True)],
                         [(2 * MEM_W, BF16)], [((1, HEAD_DIM), F32)], name=f"mem_kv_prep_bwd_{tag}")
    dmn = _mm(dmkv, w_kv, tb=True, name=f"mem_kv_dx_{tag}")
    dw = _mm(mn, dmkv, ta=True, out_dtype=BF16, name=f"mem_kv_dw_{tag}")

    def norm_body(ins, outs, accs):
        _, dg = _rms_bwd(ins[0][...], ins[1][...], ins[2][...])
        accs[0][...] += dg

    (dgn,) = _rowwise(norm_body, MEM_LEN, [(mem, True), (g_norm, False), (dmn, True)], [],
                      [((1, D_MODEL), F32)], name=f"mem_norm_bwd_{tag}")
    return dw, dgn, dgk


LOG2_E = 1.4426950408889634
SB_Q_SCALE = HEAD_DIM ** -0.5 * LOG2_E


def _log2_one_minus_beta(z2):
    nz = -z2
    u = jnp.exp2(jnp.minimum(z2, nz))
    return jnp.minimum(nz, 0.0) - jnp.log2(1.0 + u)


def _chain_modes(s, qb):
    return tuple(None if t < s else ("m" if t == s else "f") for t in range(qb))


def _split_dot(x, tri2):
    hi = x.astype(BF16)
    lo = (x - hi.astype(F32)).astype(BF16)
    return _dot(jnp.concatenate([hi, lo], axis=1), tri2)


def _sb_fwd(qkv, *, name, hp=2):
    seq = qkv.shape[0]
    blk = min(ATT_BLOCK, seq)
    nkb = seq // blk
    qb = 2 if nkb % 2 == 0 else 1
    rows = qb * blk
    scale = HEAD_DIM ** -0.5
    chains = [(t, s) for t in range(hp) for s in range(qb)]

    def body(q_ref, k_ref, v_ref, o_ref):
        base = pl.program_id(1) * qb
        qs = {(t, s): q_ref[s * blk:(s + 1) * blk, _hs(t)] for t, s in chains}
        row = lax.broadcasted_iota(jnp.int32, (blk, blk), 0)
        col = lax.broadcasted_iota(jnp.int32, (blk, blk), 1)
        after = (row > col).astype(BF16)
        after2 = jnp.concatenate([after, after], axis=0)
        causal = col < row

        def step(j, carry, modes):
            off = pl.multiple_of(j * blk, blk)
            act = [c for c in chains if modes[c[1]]]
            zs = {c: _dot_t(qs[c], k_ref[pl.ds(off, blk), _hs(c[0])]) for c in act}
            ls = {}
            for c in act:
                l = _log2_one_minus_beta(zs[c])
                ls[c] = jnp.where(causal, l, 0.0) if modes[c[1]] == "m" else l
            cs = {c: _split_dot(ls[c], after2) for c in act}
            carry = dict(carry)
            for c in act:
                run, acc = carry[c]
                a = jnp.exp2(zs[c] + ls[c] + cs[c] + run)
                if modes[c[1]] == "m":
                    a = jnp.where(causal, a, 0.0)
                acc = acc + _dot(a.astype(BF16), v_ref[pl.ds(off, blk), _hs(c[0])])
                carry[c] = (run + jnp.sum(ls[c], axis=-1, keepdims=True), acc)
            return carry

        init = (jnp.zeros((blk, 1), F32), jnp.zeros((blk, HEAD_DIM), F32))
        carry = {c: init for c in chains}
        for s in reversed(range(qb)):
            carry = step(base + s, carry, _chain_modes(s, qb))
        carry = lax.fori_loop(0, base, lambda jj, c: step(base - 1 - jj, c, ("f",) * qb), carry)
        for t, s in chains:
            o_ref[s * blk:(s + 1) * blk, _hs(t)] = carry[(t, s)][1]

    nh = N_SB_HEADS // hp
    return pl.pallas_call(
        body, out_shape=jax.ShapeDtypeStruct((seq, SB_W), F32), grid=(nh, nkb // qb),
        in_specs=[pl.BlockSpec((rows, hp * HEAD_DIM), lambda h, i: (i, h)),
                  pl.BlockSpec((seq, hp * HEAD_DIM), lambda h, i: (0, nh + h)),
                  pl.BlockSpec((seq, hp * HEAD_DIM), lambda h, i: (0, 2 * nh + h))],
        out_specs=pl.BlockSpec((rows, hp * HEAD_DIM), lambda h, i: (i, h)),
        name=name, compiler_params=_params("parallel", "arbitrary"),
    )(qkv, qkv, qkv)


def _sb_bwd(qkv, dout, *, name):
    seq = qkv.shape[0]
    blk = min(ATT_BLOCK, seq)
    nkb = seq // blk
    qb = 2 if nkb % 2 == 0 else 1
    rows = qb * blk
    scale = HEAD_DIM ** -0.5

    def body(q_ref, k_ref, v_ref, do_ref, dq_ref, dk_ref, dv_ref, de_s, sn_s):
        g = pl.program_id(1)
        base = g * qb

        @pl.when(g == 0)
        def _():
            dk_ref[...] = jnp.zeros_like(dk_ref)
            dv_ref[...] = jnp.zeros_like(dv_ref)

        qs = [q_ref[t * blk:(t + 1) * blk, :] for t in range(qb)]
        dos = [do_ref[t * blk:(t + 1) * blk, :].astype(BF16) for t in range(qb)]
        row = lax.broadcasted_iota(jnp.int32, (blk, blk), 0)
        col = lax.broadcasted_iota(jnp.int32, (blk, blk), 1)
        after = (row > col).astype(BF16)
        after2 = jnp.concatenate([after, after], axis=0)
        before = (row < col).astype(BF16)
        causal = col < row

        def sweep1(j, runs, modes):
            off = pl.multiple_of(j * blk, blk)
            kb = k_ref[pl.ds(off, blk), :]
            vb = v_ref[pl.ds(off, blk), :]
            act = [t for t in range(qb) if modes[t]]
            zs = {t: _dot_t(qs[t], kb) for t in act}
            ls = {}
            for t in act:
                l = _log2_one_minus_beta(zs[t])
                sn_s[t, j] = jnp.exp2(l).astype(BF16)
                ls[t] = jnp.where(causal, l, 0.0) if modes[t] == "m" else l
            cs = {t: _split_dot(ls[t], after2) for t in act}
            runs = list(runs)
            dv_inc = None
            for t in act:
                a = jnp.exp2(zs[t] + ls[t] + cs[t] + runs[t])
                if modes[t] == "m":
                    a = jnp.where(causal, a, 0.0)
                de_s[t, j] = (a * _dot_t(dos[t], vb)).astype(BF16)
                inc = _tdot(a.astype(BF16), dos[t])
                dv_inc = inc if dv_inc is None else dv_inc + inc
                runs[t] = runs[t] + jnp.sum(ls[t], axis=-1, keepdims=True)
            dv_ref[pl.ds(off, blk), :] += dv_inc
            return tuple(runs)

        runs = (jnp.zeros((blk, 1), F32),) * qb
        for s in reversed(range(qb)):
            runs = sweep1(base + s, runs, _chain_modes(s, qb))
        lax.fori_loop(0, base, lambda jj, r: sweep1(base - 1 - jj, r, ("f",) * qb), runs)

        def sweep2(j, carry, modes):
            lefts, dqs = list(carry[0]), list(carry[1])
            off = pl.multiple_of(j * blk, blk)
            kb = k_ref[pl.ds(off, blk), :]
            dk_inc = None
            for t in range(qb):
                if not modes[t]:
                    continue
                deb = de_s[t, j]
                de = deb.astype(F32)
                pre = _dot(deb, before) + lefts[t]
                dz = (de + pre) * sn_s[t, j].astype(F32) - pre
                if modes[t] == "m":
                    dz = jnp.where(causal, dz, 0.0)
                dzb = dz.astype(BF16)
                dqs[t] = dqs[t] + _dot(dzb, kb)
                inc = _tdot(dzb, qs[t])
                dk_inc = inc if dk_inc is None else dk_inc + inc
                lefts[t] = lefts[t] + jnp.sum(de, axis=-1, keepdims=True)
            dk_ref[pl.ds(off, blk), :] += dk_inc
            return tuple(lefts), tuple(dqs)

        carry = ((jnp.zeros((blk, 1), F32),) * qb, (jnp.zeros((blk, HEAD_DIM), F32),) * qb)
        carry = lax.fori_loop(0, base, lambda j, c: sweep2(j, c, ("f",) * qb), carry)
        for s in range(qb):
            carry = sweep2(base + s, carry, _chain_modes(s, qb))
        for t in range(qb):
            dq_ref[t * blk:(t + 1) * blk, :] = carry[1][t] * scale

        @pl.when(g == pl.num_programs(1) - 1)
        def _():
            dk_ref[...] = dk_ref[...] * (1.0 / LOG2_E)

    out = jax.ShapeDtypeStruct((seq, SB_W), F32)
    return pl.pallas_call(
        body, out_shape=[out, out, out], grid=(N_SB_HEADS, nkb // qb),
        in_specs=[pl.BlockSpec((rows, HEAD_DIM), lambda h, i: (i, h)),
                  pl.BlockSpec((seq, HEAD_DIM), lambda h, i: (0, N_SB_HEADS + h)),
                  pl.BlockSpec((seq, HEAD_DIM), lambda h, i: (0, 2 * N_SB_HEADS + h)),
                  pl.BlockSpec((rows, HEAD_DIM), lambda h, i: (i, h))],
        out_specs=[pl.BlockSpec((rows, HEAD_DIM), lambda h, i: (i, h)),
                   pl.BlockSpec((seq, HEAD_DIM), lambda h, i: (0, h)),
                   pl.BlockSpec((seq, HEAD_DIM), lambda h, i: (0, h))],
        scratch_shapes=[pltpu.VMEM((qb, nkb, blk, blk), BF16), pltpu.VMEM((qb, nkb, blk, blk), BF16)],
        name=name, compiler_params=_params("parallel", "arbitrary"),
    )(qkv, qkv, qkv, dout)


MLA_SCALE = (HEAD_DIM + ROPE_DIM) ** -0.5
MLA_Q_SCALE = MLA_SCALE * LOG2_E


def _mla_fwd(q_cat, k_cat, v, *, name, hp=2):
    seq = q_cat.shape[0]
    blk = min(ATT_BLOCK, seq)
    nkb = seq // blk
    qb = 2 if nkb % 2 == 0 else 1
    rows = qb * blk
    chains = [(t, s) for t in range(hp) for s in range(qb)]

    def body(q_ref, k_ref, v_ref, o_ref, lse_ref):
        base = pl.program_id(1) * qb
        qs = {(t, s): q_ref[s * blk:(s + 1) * blk, t * CAT_W:(t + 1) * CAT_W] for t, s in chains}
        row = lax.broadcasted_iota(jnp.int32, (blk, blk), 0)
        col = lax.broadcasted_iota(jnp.int32, (blk, blk), 1)
        causal = col <= row

        def step(j, carry, modes):
            off = pl.multiple_of(j * blk, blk)
            act = [c for c in chains if modes[c[1]]]
            ss = {c: _dot_t(qs[c], k_ref[pl.ds(off, blk), c[0] * CAT_W:(c[0] + 1) * CAT_W]) for c in act}
            carry = dict(carry)
            for c in act:
                m, l, acc = carry[c]
                s = ss[c]
                if modes[c[1]] == "m":
                    s = jnp.where(causal, s, -jnp.inf)
                m_new = jnp.maximum(m, jnp.max(s, axis=-1, keepdims=True))
                p = jnp.exp2(s - m_new)
                alpha = jnp.exp2(m - m_new)
                l = alpha * l + jnp.sum(p, axis=-1, keepdims=True)
                acc = alpha * acc + _dot(p.astype(BF16), v_ref[pl.ds(off, blk), _hs(c[0])])
                carry[c] = (m_new, l, acc)
            return carry

        init = (jnp.full((blk, 1), -jnp.inf, F32), jnp.zeros((blk, 1), F32),
                jnp.zeros((blk, HEAD_DIM), F32))
        carry = {c: init for c in chains}
        carry = lax.fori_loop(0, base, lambda j, c: step(j, c, ("f",) * qb), carry)
        for s in range(qb):
            carry = step(base + s, carry, _chain_modes(s, qb))
        for t, s in chains:
            m, l, acc = carry[(t, s)]
            o_ref[s * blk:(s + 1) * blk, _hs(t)] = acc / l
            lse_ref[s * blk:(s + 1) * blk, _hs(t)] = jnp.broadcast_to(
                (m + jnp.log2(l)) * (1.0 / LOG2_E), (blk, HEAD_DIM))

    out = jax.ShapeDtypeStruct((seq, MLA_W), F32)
    return pl.pallas_call(
        body, out_shape=[out, out], grid=(N_MLA_HEADS // hp, nkb // qb),
        in_specs=[pl.BlockSpec((rows, hp * CAT_W), lambda h, i: (i, h)),
                  pl.BlockSpec((seq, hp * CAT_W), lambda h, i: (0, h)),
                  pl.BlockSpec((seq, hp * HEAD_DIM), lambda h, i: (0, h))],
        out_specs=[pl.BlockSpec((rows, hp * HEAD_DIM), lambda h, i: (i, h)),
                   pl.BlockSpec((rows, hp * HEAD_DIM), lambda h, i: (i, h))],
        name=name, compiler_params=_params("parallel", "arbitrary"),
    )(q_cat, k_cat, v)


def _mla_bwd(q_cat, k_cat, v, out, lse, dout, *, name):
    seq = q_cat.shape[0]
    blk = min(ATT_BLOCK, seq)
    nkb = seq // blk
    qb = 2 if nkb % 2 == 0 else 1
    rows = qb * blk

    def body(q_ref, k_ref, v_ref, o_ref, lse_ref, do_ref, dq_ref, dk_ref, dv_ref):
        g = pl.program_id(1)
        base = g * qb

        @pl.when(g == 0)
        def _():
            dk_ref[...] = jnp.zeros_like(dk_ref)
            dv_ref[...] = jnp.zeros_like(dv_ref)

        qs, dobs, deltas, lses = [], [], [], []
        for t in range(qb):
            rs = slice(t * blk, (t + 1) * blk)
            do = do_ref[rs, :]
            qs.append(q_ref[rs, :])
            dobs.append(do.astype(BF16))
            deltas.append(jnp.sum(do * o_ref[rs, :], axis=-1, keepdims=True))
            lses.append(lse_ref[rs, :1] * LOG2_E)
        row = lax.broadcasted_iota(jnp.int32, (blk, blk), 0)
        col = lax.broadcasted_iota(jnp.int32, (blk, blk), 1)
        causal = col <= row

        def step(j, dqs, modes):
            off = pl.multiple_of(j * blk, blk)
            kb = k_ref[pl.ds(off, blk), :]
            vb = v_ref[pl.ds(off, blk), :]
            act = [t for t in range(qb) if modes[t]]
            ss = {t: _dot_t(qs[t], kb) for t in act}
            dps = {t: _dot_t(dobs[t], vb) for t in act}
            dqs = list(dqs)
            dv_inc = dk_inc = None
            for t in act:
                p = jnp.exp2(ss[t] - lses[t])
                if modes[t] == "m":
                    p = jnp.where(causal, p, 0.0)
                ds = (p * (dps[t] - deltas[t])).astype(BF16)
                inc_v = _tdot(p.astype(BF16), dobs[t])
                inc_k = _tdot(ds, qs[t])
                dv_inc = inc_v if dv_inc is None else dv_inc + inc_v
                dk_inc = inc_k if dk_inc is None else dk_inc + inc_k
                dqs[t] = dqs[t] + _dot(ds, kb)
            dv_ref[pl.ds(off, blk), :] += dv_inc
            dk_ref[pl.ds(off, blk), :] += dk_inc
            return tuple(dqs)

        dqs = (jnp.zeros((blk, CAT_W), F32),) * qb
        dqs = lax.fori_loop(0, base, lambda j, c: step(j, c, ("f",) * qb), dqs)
        for s in range(qb):
            modes = tuple(None if t < s else ("m" if t == s else "f") for t in range(qb))
            dqs = step(base + s, dqs, modes)
        for t in range(qb):
            dq_ref[t * blk:(t + 1) * blk, :] = dqs[t] * MLA_SCALE

        @pl.when(g == pl.num_programs(1) - 1)
        def _():
            dk_ref[...] = dk_ref[...] * (1.0 / LOG2_E)

    return pl.pallas_call(
        body,
        out_shape=[jax.ShapeDtypeStruct((seq, N_MLA_HEADS * CAT_W), F32),
                   jax.ShapeDtypeStruct((seq, N_MLA_HEADS * CAT_W), F32),
                   jax.ShapeDtypeStruct((seq, MLA_W), F32)],
        grid=(N_MLA_HEADS, nkb // qb),
        in_specs=[pl.BlockSpec((rows, CAT_W), lambda h, i: (i, h)),
                  pl.BlockSpec((seq, CAT_W), lambda h, i: (0, h)),
                  pl.BlockSpec((seq, HEAD_DIM), lambda h, i: (0, h)),
                  pl.BlockSpec((rows, HEAD_DIM), lambda h, i: (i, h)),
                  pl.BlockSpec((rows, HEAD_DIM), lambda h, i: (i, h)),
                  pl.BlockSpec((rows, HEAD_DIM), lambda h, i: (i, h))],
        out_specs=[pl.BlockSpec((rows, CAT_W), lambda h, i: (i, h)),
                   pl.BlockSpec((seq, CAT_W), lambda h, i: (0, h)),
                   pl.BlockSpec((seq, HEAD_DIM), lambda h, i: (0, h))],
        name=name, compiler_params=_params("parallel", "arbitrary"),
    )(q_cat, k_cat, v, out, lse, dout)


def _local_step(x, mem, positions, target, w, g):
    seq = x.shape[0]
    inv_freq = jnp.power(ROPE_THETA, -jnp.arange(0, ROPE_DIM, 2, dtype=F32) / ROPE_DIM)
    ang = positions.astype(F32)[:, None] * inv_freq
    cos, sin = jnp.cos(ang), jnp.sin(ang)
    lane_pad = jnp.zeros((seq, HEAD_DIM - ROPE_DIM), F32)
    cos_t = jnp.concatenate([cos, cos, lane_pad], axis=1)
    sin_t = jnp.concatenate([-sin, sin, lane_pad], axis=1)
    gain_pad = jnp.zeros((1, HEAD_DIM - ROPE_DIM), F32)
    g_k_rope = jnp.concatenate([g["g_k_rope"], gain_pad], axis=1)
    g_q_rope = jnp.concatenate([g["b_g_q_rope"], gain_pad], axis=1)

    def norm_to_bf16(src, gain, name):
        def body(ins, outs, _):
            outs[0][...] = _rms(ins[0][...], ins[1][...]).astype(BF16)
        return _rowwise(body, seq, [(src, True), (gain, False)], [(src.shape[1], BF16)], name=name)[0]

    h_a = norm_to_bf16(x, g["a_norm"], "a_norm_fwd")
    qkv = _mm(h_a, w["a_in_qkv"], out_dtype=BF16, scale_cols=(SB_W, SB_Q_SCALE), name="a_in_qkv")
    gr = _mm(h_a, w["a_in_gate"], name="a_in_gate")
    sb = _sb_fwd(qkv, name="sb_fwd")
    mem0 = _mem_side_fwd(mem, g["mem_norm"][0:1], w["mem_kv"][0], g["g_mem_k"][0:1], tag="a")
    mixed_a = _mix_fwd(sb, gr, 0, mem0[2], mem0[3], g["g_mem_q"][0:1], name="a_mix_fwd")
    x1 = _mm(mixed_a, w["a_out"], res=x, name="a_out")

    def norms2_body(ins, outs, _):
        xv = ins[0][...]
        outs[0][...] = _rms(xv, ins[1][...]).astype(BF16)
        outs[1][...] = _rms(xv, ins[2][...]).astype(BF16)

    h_kv, h_b = _rowwise(norms2_body, seq, [(x1, True), (g["kv_norm"], False), (g["b_norm"], False)],
                         [(D_MODEL, BF16), (D_MODEL, BF16)], name="kv_b_norm_fwd")
    ckr = _mm(h_kv, w["dkv"], name="dkv")

    def ckr_body(ins, outs, _):
        ckr_ref, gc_ref, gr_ref, c_ref, s_ref = ins
        outs[0][...] = _rms(ckr_ref[:, :KV_LORA], gc_ref[...]).astype(BF16)
        kr = _rms(ckr_ref[:, KV_LORA:], gr_ref[...], n=ROPE_DIM)
        outs[1][...] = _rope(kr, c_ref[...], s_ref[...]).astype(BF16)

    c_n, k_r = _rowwise(ckr_body, seq,
                        [(ckr, True), (g["g_ckv"], False), (g_k_rope, False), (cos_t, True), (sin_t, True)],
                        [(KV_LORA, BF16), (HEAD_DIM, BF16)], name="ckv_prep_fwd")
    kv = _mm(c_n, w["ukv"], name="ukv")

    def kcat_body(ins, outs, _):
        kv_ref, kr_ref, gk_ref = ins
        kc_ref, v_ref = outs
        for h in range(N_MLA_HEADS):
            kc_ref[:, h * CAT_W:h * CAT_W + HEAD_DIM] = _rms(
                kv_ref[:, h * CAT_W:h * CAT_W + HEAD_DIM], gk_ref[...]).astype(BF16)
            kc_ref[:, h * CAT_W + HEAD_DIM:(h + 1) * CAT_W] = kr_ref[...]
            v_ref[:, _hs(h)] = kv_ref[:, h * CAT_W + HEAD_DIM:(h + 1) * CAT_W].astype(BF16)

    k_cat, v_mla = _rowwise(kcat_body, seq, [(kv, True), (k_r, True), (g["g_k_nope"], False)],
                            [(N_MLA_HEADS * CAT_W, BF16), (MLA_W, BF16)], name="k_prep_fwd")

    p2 = _mm(h_b, w["b_in"], name="b_in")

    def qlat_body(ins, outs, _):
        outs[0][...] = _rms(ins[0][:, :Q_LORA], ins[1][...]).astype(BF16)

    (q_l,) = _rowwise(qlat_body, seq, [(p2, True), (g["b_g_q_lat"], False)], [(Q_LORA, BF16)],
                      name="q_lat_norm_fwd")
    q_up = _mm(q_l, w["uq"], name="uq")

    def qcat_body(ins, outs, _):
        q_ref, gn_ref, gr_ref, c_ref, s_ref = ins
        (o_ref,) = outs
        for h in range(N_MLA_HEADS):
            o_ref[:, h * CAT_W:h * CAT_W + HEAD_DIM] = (MLA_Q_SCALE * _rms(
                q_ref[:, h * CAT_W:h * CAT_W + HEAD_DIM], gn_ref[...])).astype(BF16)
            qr = _rms(q_ref[:, h * CAT_W + HEAD_DIM:(h + 1) * CAT_W], gr_ref[...], n=ROPE_DIM)
            o_ref[:, h * CAT_W + HEAD_DIM:(h + 1) * CAT_W] = (
                MLA_Q_SCALE * _rope(qr, c_ref[...], s_ref[...])).astype(BF16)

    (q_cat,) = _rowwise(qcat_body, seq,
                        [(q_up, True), (g["b_g_q_nope"], False), (g_q_rope, False), (cos_t, True), (sin_t, True)],
                        [(N_MLA_HEADS * CAT_W, BF16)], name="q_prep_fwd")
    att, lse = _mla_fwd(q_cat, k_cat, v_mla, name="mla_fwd")
    mem1 = _mem_side_fwd(mem, g["mem_norm"][1:2], w["mem_kv"][1], g["g_mem_k"][1:2], tag="b")
    mixed_b = _mix_fwd(att, p2, Q_LORA, mem1[2], mem1[3], g["g_mem_q"][1:2], name="b_mix_fwd")
    y = _mm(mixed_b, w["b_out"], res=x1, name="b_out")

    def loss_body(ins, outs, accs):
        diff = ins[0][...] - ins[1][...]
        outs[0][...] = diff / D_MODEL
        col = jnp.sum(diff * diff, axis=0, keepdims=True)
        part = col[:, :HEAD_DIM]
        for c in range(1, D_MODEL // HEAD_DIM):
            part = part + col[:, _hs(c)]
        accs[0][...] += part * (0.5 / D_MODEL)

    dy, loss_part = _rowwise(loss_body, seq, [(y, True), (target, True)], [(D_MODEL, F32)],
                             [((1, HEAD_DIM), F32)], name="loss")

    gw, gg = {}, {}
    dmixed_b = _mm(dy, w["b_out"], tb=True, name="b_out_dx")
    gw["b_out"] = _mm(mixed_b, dy, ta=True, out_dtype=BF16, name="b_out_dw")
    datt, dgate_b, dmk1, dmv1, gq1 = _mix_bwd(dmixed_b, att, p2, Q_LORA, mem1[2], mem1[3],
                                              g["g_mem_q"][1:2], name="b_mix_bwd")
    dq_cat, dk_cat, dv_mla = _mla_bwd(q_cat, k_cat, v_mla, att, lse, datt, name="mla_bwd")

    def qcat_bwd_body(ins, outs, accs):
        q_ref, dq_ref, gn_ref, gr_ref, c_ref, s_ref = ins
        (o_ref,) = outs
        dgn_ref, dgr_ref = accs
        for h in range(N_MLA_HEADS):
            dx, dg = _rms_bwd(q_ref[:, h * CAT_W:h * CAT_W + HEAD_DIM], gn_ref[...],
                              dq_ref[:, h * CAT_W:h * CAT_W + HEAD_DIM])
            o_ref[:, h * CAT_W:h * CAT_W + HEAD_DIM] = dx.astype(BF16)
            dgn_ref[...] += dg
            dn = _rope_bwd(dq_ref[:, h * CAT_W + HEAD_DIM:(h + 1) * CAT_W], c_ref[...], s_ref[...])
            dx, dg = _rms_bwd(q_ref[:, h * CAT_W + HEAD_DIM:(h + 1) * CAT_W], gr_ref[...], dn, n=ROPE_DIM)
            o_ref[:, h * CAT_W + HEAD_DIM:(h + 1) * CAT_W] = dx.astype(BF16)
            dgr_ref[...] += dg

    dq_up, gg["b_g_q_nope"], dgqr = _rowwise(
        qcat_bwd_body, seq,
        [(q_up, True), (dq_cat, True), (g["b_g_q_nope"], False), (g_q_rope, False), (cos_t, True), (sin_t, True)],
        [(N_MLA_HEADS * CAT_W, BF16)], [((1, HEAD_DIM), F32), ((1, HEAD_DIM), F32)], name="q_prep_bwd")
    gg["b_g_q_rope"] = dgqr
    dq_l = _mm(dq_up, w["uq"], tb=True, name="uq_dx")
    gw["uq"] = _mm(q_l, dq_up, ta=True, out_dtype=BF16, n_split=N_CHIPS, name="uq_dw")

    def qlat_bwd_body(ins, outs, accs):
        p2_ref, dql_ref, dgate_ref, gl_ref = ins
        dx, dg = _rms_bwd(p2_ref[:, :Q_LORA], gl_ref[...], dql_ref[...])
        outs[0][:, :Q_LORA] = dx.astype(BF16)
        outs[0][:, Q_LORA:] = dgate_ref[...]
        accs[0][...] += dg

    dp2, gg["b_g_q_lat"] = _rowwise(
        qlat_bwd_body, seq, [(p2, True), (dq_l, True), (dgate_b, True), (g["b_g_q_lat"], False)],
        [(Q_LORA + GATE_W, BF16)], [((1, Q_LORA), F32)], name="q_lat_norm_bwd")
    dh_b = _mm(dp2, w["b_in"], tb=True, name="b_in_dx")
    gw["b_in"] = _mm(h_b, dp2, ta=True, out_dtype=BF16, n_split=N_CHIPS, name="b_in_dw")

    def kcat_bwd_body(ins, outs, accs):
        kv_ref, dkc_ref, dv_ref, gk_ref = ins
        dkv_ref, dkr_ref = outs
        (dgk_ref,) = accs
        dkr = jnp.zeros(dkr_ref.shape, F32)
        for h in range(N_MLA_HEADS):
            dx, dg = _rms_bwd(kv_ref[:, h * CAT_W:h * CAT_W + HEAD_DIM], gk_ref[...],
                              dkc_ref[:, h * CAT_W:h * CAT_W + HEAD_DIM])
            dkv_ref[:, h * CAT_W:h * CAT_W + HEAD_DIM] = dx.astype(BF16)
            dgk_ref[...] += dg
            dkv_ref[:, h * CAT_W + HEAD_DIM:(h + 1) * CAT_W] = dv_ref[:, _hs(h)].astype(BF16)
            dkr = dkr + dkc_ref[:, h * CAT_W + HEAD_DIM:(h + 1) * CAT_W]
        dkr_ref[...] = dkr

    dkv, dk_r, gg["g_k_nope"] = _rowwise(
        kcat_bwd_body, seq, [(kv, True), (dk_cat, True), (dv_mla, True), (g["g_k_nope"], False)],
        [(N_MLA_HEADS * CAT_W, BF16), (HEAD_DIM, F32)], [((1, HEAD_DIM), F32)], name="k_prep_bwd")
    dc_n = _mm(dkv, w["ukv"], tb=True, name="ukv_dx")
    gw["ukv"] = _mm(c_n, dkv, ta=True, out_dtype=BF16, n_split=N_CHIPS, name="ukv_dw")

    def ckr_bwd_body(ins, outs, accs):
        ckr_ref, dcn_ref, dkr_ref, gc_ref, gr_ref, c_ref, s_ref = ins
        dx, dg = _rms_bwd(ckr_ref[:, :KV_LORA], gc_ref[...], dcn_ref[...])
        outs[0][:, :KV_LORA] = dx.astype(BF16)
        accs[0][...] += dg
        dn = _rope_bwd(dkr_ref[...], c_ref[...], s_ref[...])
        dx, dg = _rms_bwd(ckr_ref[:, KV_LORA:], gr_ref[...], dn, n=ROPE_DIM)
        outs[0][:, KV_LORA:] = dx.astype(BF16)
        accs[1][...] += dg

    dckr, gg["g_ckv"], gg["g_k_rope"] = _rowwise(
        ckr_bwd_body, seq,
        [(ckr, True), (dc_n, True), (dk_r, True), (g["g_ckv"], False), (g_k_rope, False),
         (cos_t, True), (sin_t, True)],
        [(KV_LORA + HEAD_DIM, BF16)], [((1, KV_LORA), F32), ((1, HEAD_DIM), F32)], name="ckv_prep_bwd")
    dh_kv = _mm(dckr, w["dkv"], tb=True, name="dkv_dx")
    gw["dkv"] = _mm(h_kv, dckr, ta=True, out_dtype=BF16, name="dkv_dw")

    def norms2_bwd_body(ins, outs, accs):
        x_ref, dy_ref, dhk_ref, dhb_ref, gk_ref, gb_ref = ins
        xv = x_ref[...]
        dxk, dgk = _rms_bwd(xv, gk_ref[...], dhk_ref[...])
        dxb, dgb = _rms_bwd(xv, gb_ref[...], dhb_ref[...])
        outs[0][...] = dy_ref[...] + dxk + dxb
        accs[0][...] += dgk
        accs[1][...] += dgb

    dx1, gg["kv_norm"], gg["b_norm"] = _rowwise(
        norms2_bwd_body, seq,
        [(x1, True), (dy, True), (dh_kv, True), (dh_b, True), (g["kv_norm"], False), (g["b_norm"], False)],
        [(D_MODEL, F32)], [((1, D_MODEL), F32), ((1, D_MODEL), F32)], name="kv_b_norm_bwd")

    dmixed_a = _mm(dx1, w["a_out"], tb=True, name="a_out_dx")
    gw["a_out"] = _mm(mixed_a, dx1, ta=True, out_dtype=BF16, name="a_out_dw")
    dsb, dgate_a, dmk0, dmv0, gq0 = _mix_bwd(dmixed_a, sb, gr, 0, mem0[2], mem0[3],
                                             g["g_mem_q"][0:1], name="a_mix_bwd")
    dq, dk, dv = _sb_bwd(qkv, dsb, name="sb_bwd")
    dp_a = jnp.concatenate([dq.astype(BF16), dk.astype(BF16), dv.astype(BF16), dgate_a], axis=1)
    dh_a = _mm(dp_a, w["a_in"], tb=True, name="a_in_dx")
    gw["a_in"] = _mm(h_a, dp_a, ta=True, out_dtype=BF16, n_split=N_CHIPS, name="a_in_dw")

    def norm_a_bwd_body(ins, outs, accs):
        dx, dg = _rms_bwd(ins[0][...], ins[3][...], ins[2][...])
        outs[0][...] = ins[1][...] + dx
        accs[0][...] += dg

    grad_x, gg["a_norm"] = _rowwise(
        norm_a_bwd_body, seq, [(x, True), (dx1, True), (dh_a, True), (g["a_norm"], False)],
        [(D_MODEL, F32)], [((1, D_MODEL), F32)], name="a_norm_bwd")

    dw0, dgn0, dgk0 = _mem_side_bwd(mem, g["mem_norm"][0:1], w["mem_kv"][0], g["g_mem_k"][0:1],
                                    mem0[0], mem0[1], dmk0, dmv0, tag="a")
    dw1, dgn1, dgk1 = _mem_side_bwd(mem, g["mem_norm"][1:2], w["mem_kv"][1], g["g_mem_k"][1:2],
                                    mem1[0], mem1[1], dmk1, dmv1, tag="b")
    gw["mem_kv"] = (dw0, dw1)
    gg["mem_norm"] = jnp.concatenate([dgn0, dgn1], axis=0)
    gg["g_mem_q"] = jnp.concatenate([gq0, gq1], axis=0)
    gg["g_mem_k"] = jnp.concatenate([dgk0, dgk1], axis=0)
    return loss_part, grad_x, gw, gg


HBM_SPEC = pl.BlockSpec(memory_space=pl.ANY)


def _other_chips():
    x, y = lax.axis_index("x"), lax.axis_index("y")
    return [(1 - x, y), (x, 1 - y), (1 - x, 1 - y)]


def _allgather_chips(shards):
    n = len(shards)
    split = [s.shape[0] % 32 == 0 for s in shards]

    def body(*refs):
        ins, outs = refs[:n], refs[n:2 * n]
        send, recv, fsend, frecv, loc = refs[2 * n:]
        x, y, c = lax.axis_index("x"), lax.axis_index("y"), lax.axis_index("c")
        me = 2 * x + y
        chips = _other_chips()

        def part(ref, wi):
            if not split[wi]:
                return ref
            half = shards[wi].shape[0] // 2
            return ref.at[pl.ds(pl.multiple_of(c * half, 16), half)]

        def ici(wi, k, src_chip, to):
            return pltpu.make_async_remote_copy(
                src_ref=part(ins[wi], wi), dst_ref=part(outs[wi].at[src_chip], wi),
                send_sem=send.at[wi, k], recv_sem=recv.at[wi, k], device_id=to, device_id_type=MESH)

        def d2d(wi, k, src_chip):
            rows = part(outs[wi].at[src_chip], wi)
            return pltpu.make_async_remote_copy(
                src_ref=rows, dst_ref=rows, send_sem=fsend.at[wi, k], recv_sem=frecv.at[wi, k],
                device_id=(x, y, 1 - c), device_id_type=MESH)

        started = []
        for wi in range(n):
            own = pltpu.make_async_copy(ins[wi], outs[wi].at[me], loc.at[wi])
            own.start()
            started.append(own)
            for k, (tx, ty) in enumerate(chips):
                ici(wi, k, me, (tx, ty, c)).start()
        for wi in range(n):
            for k, (tx, ty) in enumerate(chips):
                landed = ici(wi, k, 2 * tx + ty, (tx, ty, c))
                landed.wait_recv()
                if split[wi]:
                    d2d(wi, k, 2 * tx + ty).start()
        for wi in range(n):
            for k, (tx, ty) in enumerate(chips):
                ici(wi, k, me, (tx, ty, c)).wait_send()
                if split[wi]:
                    fwd = d2d(wi, k, 2 * tx + ty)
                    fwd.wait_send()
                    fwd.wait_recv()
        for own in started:
            own.wait()

    return pl.pallas_call(
        body, out_shape=[jax.ShapeDtypeStruct((N_CHIPS,) + s.shape, s.dtype) for s in shards],
        in_specs=[HBM_SPEC] * n, out_specs=[HBM_SPEC] * n,
        scratch_shapes=[pltpu.SemaphoreType.DMA((n, 3)), pltpu.SemaphoreType.DMA((n, 3)),
                        pltpu.SemaphoreType.DMA((n, 3)), pltpu.SemaphoreType.DMA((n, 3)),
                        pltpu.SemaphoreType.DMA((n,))],
        name="allgather_weights",
    )(*shards)


def _scatter_to_chips(grads):
    n = len(grads)

    def body(*refs):
        ins, outs = refs[:n], refs[n:2 * n]
        send, recv, loc = refs[2 * n:]
        c = lax.axis_index("c")
        me = 2 * lax.axis_index("x") + lax.axis_index("y")
        copies = []
        for wi in range(n):
            own = pltpu.make_async_copy(ins[wi].at[me], outs[wi].at[3], loc.at[wi])
            own.start()
            copies.append(own)
            for k, (tx, ty) in enumerate(_other_chips()):
                cp = pltpu.make_async_remote_copy(
                    src_ref=ins[wi].at[2 * tx + ty], dst_ref=outs[wi].at[k], send_sem=send.at[wi, k],
                    recv_sem=recv.at[wi, k], device_id=(tx, ty, c), device_id_type=MESH)
                cp.start()
                copies.append(cp)
        for cp in copies:
            cp.wait()

    return pl.pallas_call(
        body, out_shape=[jax.ShapeDtypeStruct(s.shape, s.dtype) for s in grads],
        in_specs=[HBM_SPEC] * n, out_specs=[HBM_SPEC] * n,
        scratch_shapes=[pltpu.SemaphoreType.DMA((n, 3)), pltpu.SemaphoreType.DMA((n, 3)),
                        pltpu.SemaphoreType.DMA((n,))],
        name="scatter_grads",
    )(*grads)


def _swap_with_sibling(parts):
    n = len(parts)

    def body(*refs):
        ins, outs = refs[:n], refs[n:2 * n]
        send, recv = refs[2 * n:]
        sib = (lax.axis_index("x"), lax.axis_index("y"), 1 - lax.axis_index("c"))
        copies = []
        for wi in range(n):
            cp = pltpu.make_async_remote_copy(
                src_ref=ins[wi], dst_ref=outs[wi], send_sem=send.at[wi], recv_sem=recv.at[wi],
                device_id=sib, device_id_type=MESH)
            cp.start()
            copies.append(cp)
        for cp in copies:
            cp.wait()

    return pl.pallas_call(
        body, out_shape=[jax.ShapeDtypeStruct(s.shape, s.dtype) for s in parts],
        in_specs=[HBM_SPEC] * n, out_specs=[HBM_SPEC] * n,
        scratch_shapes=[pltpu.SemaphoreType.DMA((n,)), pltpu.SemaphoreType.DMA((n,))],
        name="swap_partial_grads",
    )(*parts)


def _allreduce_small(vec, loss_row):
    rows = vec.shape[0]

    def body(v_ref, o_ref, buf, send, recv):
        x, y, c = lax.axis_index("x"), lax.axis_index("y"), lax.axis_index("c")
        me = 4 * x + 2 * y + c
        buf[me] = v_ref[...]
        copies = []
        for r in range(1, N_DEV):
            peer = (x ^ ((r >> 2) & 1), y ^ ((r >> 1) & 1), c ^ (r & 1))
            cp = pltpu.make_async_remote_copy(
                src_ref=v_ref, dst_ref=buf.at[me], send_sem=send.at[r - 1], recv_sem=recv.at[r - 1],
                device_id=peer, device_id_type=MESH)
            cp.start()
            copies.append(cp)
        for cp in copies:
            cp.wait()
        total = buf[0]
        for d in range(1, N_DEV):
            total = total + buf[d]
        o_ref[...] = total
        o_ref[loss_row:loss_row + 1, :] = jnp.broadcast_to(
            jnp.sum(total[loss_row:loss_row + 1, :], axis=-1, keepdims=True), (1, HEAD_DIM))

    return pl.pallas_call(
        body, out_shape=jax.ShapeDtypeStruct(vec.shape, F32),
        in_specs=[pl.BlockSpec(memory_space=pltpu.VMEM)], out_specs=pl.BlockSpec(memory_space=pltpu.VMEM),
        scratch_shapes=[pltpu.VMEM((N_DEV, rows, HEAD_DIM), F32),
                        pltpu.SemaphoreType.DMA((N_DEV - 1,)), pltpu.SemaphoreType.DMA((N_DEV - 1,))],
        name="allreduce_gains",
    )(vec)


def _sum_slots(r, *, name):
    _, rows, width = r.shape
    blk = _pick(rows, (256, 128, 64, 32, 16, 8))

    def body(r_ref, o_ref):
        o_ref[...] = ((r_ref[3].astype(F32) + r_ref[0].astype(F32)) + r_ref[1].astype(F32)) + r_ref[2].astype(F32)

    return pl.pallas_call(
        body, out_shape=jax.ShapeDtypeStruct((rows, width), F32), grid=(rows // blk,),
        in_specs=[pl.BlockSpec((N_CHIPS, blk, width), lambda i: (0, i, 0))],
        out_specs=pl.BlockSpec((blk, width), lambda i: (i, 0)),
        name=name, compiler_params=_params("parallel"),
    )(r)


def _adamw(wgt, grads, m, v, *, name):
    rows, width = wgt.shape
    blk = _pick(rows, (256, 128, 64, 32, 16, 8))
    n_g = len(grads)

    def body(*refs):
        w_ref, m_ref, v_ref = refs[0], refs[1 + n_g], refs[2 + n_g]
        g_out, d_out, m_out, v_out = refs[3 + n_g:]
        grad = refs[1][...]
        for t in range(1, n_g):
            grad = grad + refs[1 + t][...]
        m_new = ADAM_B1 * m_ref[...] + (1.0 - ADAM_B1) * grad
        v_new = ADAM_B2 * v_ref[...] + (1.0 - ADAM_B2) * (grad * grad)
        m_hat = m_new / (1.0 - ADAM_B1 ** ADAM_STEP)
        v_hat = v_new / (1.0 - ADAM_B2 ** ADAM_STEP)
        g_out[...] = grad
        d_out[...] = -ADAM_LR * (m_hat / (jnp.sqrt(v_hat) + ADAM_EPS) + ADAM_WD * w_ref[...])
        m_out[...] = m_new
        v_out[...] = v_new

    spec = pl.BlockSpec((blk, width), lambda i: (i, 0))
    out = jax.ShapeDtypeStruct((rows, width), F32)
    return pl.pallas_call(
        body, out_shape=[out] * 4, grid=(rows // blk,), in_specs=[spec] * (3 + n_g),
        out_specs=[spec] * 4, name=name, compiler_params=_params("parallel"),
    )(wgt, *grads, m, v)


_SMALL = (("a_norm", 2048), ("kv_norm", 2048), ("g_ckv", 512), ("g_k_nope", 128), ("g_k_rope", 64),
          ("b_norm", 2048), ("b_g_q_lat", 512), ("b_g_q_nope", 128), ("b_g_q_rope", 64),
          ("mem_norm", 4096), ("g_mem_q", 256), ("g_mem_k", 256))


def _lanes(n):
    return -(-n // HEAD_DIM) * HEAD_DIM


def _pack_rows(pieces, pad_rows_to=8):
    flat = jnp.concatenate(pieces, axis=1)
    rows = flat.shape[1] // HEAD_DIM
    pad = (-rows) % pad_rows_to
    if pad:
        flat = jnp.concatenate([flat, jnp.zeros((1, pad * HEAD_DIM), F32)], axis=1)
    return flat.reshape(rows + pad, HEAD_DIM)


def _pad_lanes(a):
    a = a.reshape(1, -1)
    pad = _lanes(a.shape[1]) - a.shape[1]
    if pad:
        a = jnp.concatenate([a, jnp.zeros((1, pad), F32)], axis=1)
    return a


def kernel(x, mem, positions, a_norm, a_w_in, a_w_out, kv_norm, w_dkv, g_ckv, w_ukv, g_k_nope, g_k_rope, b_norm, b_w_in, b_g_q_lat, b_w_uq, b_g_q_nope, b_g_q_rope, b_w_out, mem_norm, w_mem_kv, g_mem_q, g_mem_k, loss_target, m_a_norm, m_a_w_in, m_a_w_out, m_kv_norm, m_w_dkv, m_g_ckv, m_w_ukv, m_g_k_nope, m_g_k_rope, m_b_norm, m_b_w_in, m_b_g_q_lat, m_b_w_uq, m_b_g_q_nope, m_b_g_q_rope, m_b_w_out, m_mem_norm, m_w_mem_kv, m_g_mem_q, m_g_mem_k, v_a_norm, v_a_w_in, v_a_w_out, v_kv_norm, v_w_dkv, v_g_ckv, v_w_ukv, v_g_k_nope, v_g_k_rope, v_b_norm, v_b_w_in, v_b_g_q_lat, v_b_w_uq, v_b_g_q_nope, v_b_g_q_rope, v_b_w_out, v_mem_norm, v_w_mem_kv, v_g_mem_q, v_g_mem_k):
    chip = 2 * lax.axis_index("x") + lax.axis_index("y")
    rows_dkv = D_MODEL // N_CHIPS
    heads_per_chip = N_MLA_HEADS // N_CHIPS
    qk_w = HEAD_DIM + ROPE_DIM

    big = {"a_in": a_w_in[0], "a_out": a_w_out[0], "dkv": w_dkv, "ukv": w_ukv, "b_in": b_w_in[0],
           "uq": b_w_uq[0], "b_out": b_w_out[0], "mem_kv": w_mem_kv.reshape(2 * rows_dkv, 2 * MEM_W)}
    big_m = {"a_in": m_a_w_in[0], "a_out": m_a_w_out[0], "dkv": m_w_dkv, "ukv": m_w_ukv, "b_in": m_b_w_in[0],
             "uq": m_b_w_uq[0], "b_out": m_b_w_out[0], "mem_kv": m_w_mem_kv.reshape(2 * rows_dkv, 2 * MEM_W)}
    big_v = {"a_in": v_a_w_in[0], "a_out": v_a_w_out[0], "dkv": v_w_dkv, "ukv": v_w_ukv, "b_in": v_b_w_in[0],
             "uq": v_b_w_uq[0], "b_out": v_b_w_out[0], "mem_kv": v_w_mem_kv.reshape(2 * rows_dkv, 2 * MEM_W)}
    names = list(big)
    gathered = _allgather_chips([big[n].astype(BF16) for n in names] + [a_norm])
    st = dict(zip(names, gathered[:-1]))
    a_in_full = st["a_in"].transpose(1, 0, 2).reshape(D_MODEL, QKV_W + GATE_W)
    uq = st["uq"].reshape(N_CHIPS, Q_LORA, heads_per_chip, qk_w)
    uq = jnp.pad(uq, ((0, 0), (0, 0), (0, 0), (0, CAT_W - qk_w)))
    w = {
        "a_in": a_in_full,
        "a_in_qkv": a_in_full[:, :QKV_W],
        "a_in_gate": a_in_full[:, QKV_W:],
        "a_out": st["a_out"].reshape(D_MODEL, D_MODEL),
        "dkv": jnp.pad(st["dkv"].reshape(D_MODEL, KV_LORA + ROPE_DIM), ((0, 0), (0, HEAD_DIM - ROPE_DIM))),
        "ukv": st["ukv"].transpose(1, 0, 2).reshape(KV_LORA, N_MLA_HEADS * CAT_W),
        "b_in": st["b_in"].transpose(1, 0, 2).reshape(D_MODEL, Q_LORA + GATE_W),
        "uq": uq.transpose(1, 0, 2, 3).reshape(Q_LORA, N_MLA_HEADS * CAT_W),
        "b_out": st["b_out"].reshape(D_MODEL, D_MODEL),
        "mem_kv": st["mem_kv"].reshape(N_CHIPS, 2, rows_dkv, 2 * MEM_W).transpose(1, 0, 2, 3).reshape(
            2, D_MODEL, 2 * MEM_W),
    }
    gains = {
        "a_norm": gathered[-1].reshape(1, D_MODEL), "kv_norm": kv_norm.reshape(1, -1),
        "g_ckv": g_ckv.reshape(1, -1), "g_k_nope": g_k_nope.reshape(1, -1), "g_k_rope": g_k_rope.reshape(1, -1),
        "b_norm": b_norm, "b_g_q_lat": b_g_q_lat, "b_g_q_nope": b_g_q_nope, "b_g_q_rope": b_g_q_rope,
        "mem_norm": mem_norm, "g_mem_q": g_mem_q, "g_mem_k": g_mem_k,
    }

    loss_part, grad_x, gw, gg = _local_step(x[0], mem[0], positions[0], loss_target[0], w, gains)

    stacked = {
        "a_in": gw["a_in"],
        "a_out": gw["a_out"].reshape(N_CHIPS, rows_dkv, D_MODEL),
        "dkv": gw["dkv"][:, :KV_LORA + ROPE_DIM].reshape(N_CHIPS, rows_dkv, KV_LORA + ROPE_DIM),
        "ukv": gw["ukv"],
        "b_in": gw["b_in"],
        "uq": gw["uq"].reshape(N_CHIPS, Q_LORA, heads_per_chip, CAT_W)[..., :qk_w].reshape(
            N_CHIPS, Q_LORA, heads_per_chip * qk_w),
        "b_out": gw["b_out"].reshape(N_CHIPS, rows_dkv, D_MODEL),
        "mem_kv": jnp.stack([gw["mem_kv"][0].reshape(N_CHIPS, rows_dkv, 2 * MEM_W),
                             gw["mem_kv"][1].reshape(N_CHIPS, rows_dkv, 2 * MEM_W)], axis=1).reshape(
            N_CHIPS, 2 * rows_dkv, 2 * MEM_W),
    }
    received = _scatter_to_chips([stacked[n] for n in names])
    partial = [_sum_slots(r, name=f"sum_slots_{n}") for n, r in zip(names, received)]
    sibling = _swap_with_sibling(partial)
    big_out = {}
    for n, mine, theirs in zip(names, partial, sibling):
        big_out[n] = _adamw(big[n], [mine, theirs], big_m[n], big_v[n], name=f"adamw_{n}")

    pieces = [_pad_lanes(gg[n]) if n not in ("g_k_rope", "b_g_q_rope") else gg[n] for n, _ in _SMALL]
    pieces.append(loss_part)
    loss_row = sum(_lanes(size) for _, size in _SMALL) // HEAD_DIM
    summed = _allreduce_small(_pack_rows(pieces), loss_row)
    flat = summed.reshape(1, -1)
    small_g, off = {}, 0
    for n, size in _SMALL:
        small_g[n] = flat[:, off:off + size]
        off += _lanes(size)
    loss = flat[0, off]
    small_g["a_norm"] = lax.dynamic_slice(small_g["a_norm"], (0, chip * rows_dkv), (1, rows_dkv))

    small_w = {"a_norm": a_norm, "kv_norm": kv_norm, "g_ckv": g_ckv, "g_k_nope": g_k_nope, "g_k_rope": g_k_rope,
               "b_norm": b_norm, "b_g_q_lat": b_g_q_lat, "b_g_q_nope": b_g_q_nope, "b_g_q_rope": b_g_q_rope,
               "mem_norm": mem_norm, "g_mem_q": g_mem_q, "g_mem_k": g_mem_k}
    small_m = {"a_norm": m_a_norm, "kv_norm": m_kv_norm, "g_ckv": m_g_ckv, "g_k_nope": m_g_k_nope,
               "g_k_rope": m_g_k_rope, "b_norm": m_b_norm, "b_g_q_lat": m_b_g_q_lat, "b_g_q_nope": m_b_g_q_nope,
               "b_g_q_rope": m_b_g_q_rope, "mem_norm": m_mem_norm, "g_mem_q": m_g_mem_q, "g_mem_k": m_g_mem_k}
    small_v = {"a_norm": v_a_norm, "kv_norm": v_kv_norm, "g_ckv": v_g_ckv, "g_k_nope": v_g_k_nope,
               "g_k_rope": v_g_k_rope, "b_norm": v_b_norm, "b_g_q_lat": v_b_g_q_lat, "b_g_q_nope": v_b_g_q_nope,
               "b_g_q_rope": v_b_g_q_rope, "mem_norm": v_mem_norm, "g_mem_q": v_g_mem_q, "g_mem_k": v_g_mem_k}
    snames = [n for n, _ in _SMALL]
    packs = [_pack_rows([_pad_lanes(src[n]) for n in snames])
             for src in (small_w, small_g, small_m, small_v)]
    small_res = _adamw(packs[0], [packs[1]], packs[2], packs[3], name="adamw_gains")
    small_out = {n: [] for n in snames}
    for res in small_res:
        flat_r = res.reshape(1, -1)
        off = 0
        for n in snames:
            size = small_w[n].size
            small_out[n].append(flat_r[:, off:off + size].reshape(small_w[n].shape))
            off += _lanes(size)

    big_names = {"a_w_in": ("a_in", a_w_in), "a_w_out": ("a_out", a_w_out), "w_dkv": ("dkv", w_dkv),
                 "w_ukv": ("ukv", w_ukv), "b_w_in": ("b_in", b_w_in), "b_w_uq": ("uq", b_w_uq),
                 "b_w_out": ("b_out", b_w_out), "w_mem_kv": ("mem_kv", w_mem_kv)}
    order = ["a_norm", "a_w_in", "a_w_out", "kv_norm", "w_dkv", "g_ckv", "w_ukv", "g_k_nope", "g_k_rope",
             "b_norm", "b_w_in", "b_g_q_lat", "b_w_uq", "b_g_q_nope", "b_g_q_rope", "b_w_out", "mem_norm",
             "w_mem_kv", "g_mem_q", "g_mem_k"]
    groups = [[], [], [], []]
    for n in order:
        if n in big_names:
            key, ref_arr = big_names[n]
            for t in range(4):
                groups[t].append(big_out[key][t].reshape(ref_arr.shape))
        else:
            for t in range(4):
                groups[t].append(small_out[n][t])
    return (loss, grad_x[None], *groups[0], *groups[1], *groups[2], *groups[3])
```

```python
import functools

import jax
import jax.numpy as jnp
from jax import lax
from jax.experimental import pallas as pl
from jax.experimental.pallas import tpu as pltpu

F32 = jnp.float32
BF16 = jnp.bfloat16
MESH = pl.DeviceIdType.MESH

D_MODEL = 2048
HEAD_DIM = 128
N_SB_HEADS = 12
N_MEM_HEADS = 4
N_MLA_HEADS = 12
MEM_LEN = 256
Q_LORA = 512
KV_LORA = 512
ROPE_DIM = 64
SB_W = N_SB_HEADS * HEAD_DIM
MEM_W = N_MEM_HEADS * HEAD_DIM
MLA_W = N_MLA_HEADS * HEAD_DIM
QKV_W = 3 * SB_W
GATE_W = SB_W + 2 * MEM_W
CAT_W = 2 * HEAD_DIM
ROPE_THETA = 10000.0
EPS = 1e-6
N_CHIPS = 4
N_DEV = 8

ADAM_LR = 0.001
ADAM_B1 = 0.9
ADAM_B2 = 0.999
ADAM_EPS = 1e-08
ADAM_WD = 0.01
ADAM_STEP = 10

VMEM_LIMIT_BYTES = 56 * 1024 * 1024
MM_OPERAND_VMEM_BYTES = 24 * 1024 * 1024
ROW_BLOCK = 256
ATT_BLOCK = 256


def _params(*sem):
    return pltpu.CompilerParams(dimension_semantics=sem, vmem_limit_bytes=VMEM_LIMIT_BYTES)


def _pick(n, cands):
    for c in cands:
        if n % c == 0:
            return c
    return n


def _mm(a, b, *, name, ta=False, tb=False, out_dtype=F32, res=None, n_split=1, scale_cols=None):
    if ta:
        k_dim, m_dim = a.shape
    else:
        m_dim, k_dim = a.shape
    if tb:
        n_dim, kb = b.shape
    else:
        kb, n_dim = b.shape
    assert kb == k_dim, (a.shape, b.shape)
    n_per = n_dim // n_split
    bm = m_dim if m_dim <= 1024 else _pick(m_dim, (1024, 512, 256))
    bn = n_per if n_per <= 1024 else _pick(n_per, (1024, 896, 768, 640, 512, 256, 128))
    per_k = (bm * a.dtype.itemsize + bn * b.dtype.itemsize) * 2
    bk = next((c for c in (k_dim, 2048, 1024, 512, 256, 128)
               if c <= k_dim and k_dim % c == 0 and c * per_k <= MM_OPERAND_VMEM_BYTES), 128)
    nk = k_dim // bk
    nb_per = n_per // bn
    grid = (m_dim // bm, n_dim // bn, nk)
    a_spec = (pl.BlockSpec((bk, bm), lambda i, j, k: (k, i)) if ta
              else pl.BlockSpec((bm, bk), lambda i, j, k: (i, k)))
    b_spec = (pl.BlockSpec((bn, bk), lambda i, j, k: (j, k)) if tb
              else pl.BlockSpec((bk, bn), lambda i, j, k: (k, j)))
    dims = (((0 if ta else 1,), (1 if tb else 0,)), ((), ()))
    in_specs = [a_spec, b_spec]
    args = [a, b]
    if res is not None:
        in_specs.append(pl.BlockSpec((bm, bn), lambda i, j, k: (i, j)))
        args.append(res)
    if n_split == 1:
        out_shape = jax.ShapeDtypeStruct((m_dim, n_dim), out_dtype)
        out_spec = pl.BlockSpec((bm, bn), lambda i, j, k: (i, j))
    else:
        out_shape = jax.ShapeDtypeStruct((n_split, m_dim, n_per), out_dtype)
        out_spec = pl.BlockSpec((None, bm, bn), lambda i, j, k: (j // nb_per, i, j % nb_per))

    def body(*refs):
        if res is None:
            a_ref, b_ref, o_ref, acc = refs
            r_ref = None
        else:
            a_ref, b_ref, r_ref, o_ref, acc = refs
        k = pl.program_id(2)
        col_block = pl.program_id(1)

        @pl.when(k == 0)
        def _():
            acc[...] = jnp.zeros_like(acc)

        acc[...] += lax.dot_general(a_ref[...].astype(BF16), b_ref[...].astype(BF16), dims,
                                    preferred_element_type=F32)

        @pl.when(k == nk - 1)
        def _():
            r = acc[...]
            if r_ref is not None:
                r = r + r_ref[...]
            if scale_cols is not None:
                assert scale_cols[0] % bn == 0
                r = r * jnp.where(col_block < scale_cols[0] // bn, scale_cols[1], 1.0)
            o_ref[...] = r.astype(out_dtype)

    return pl.pallas_call(
        body, out_shape=out_shape, grid=grid, in_specs=in_specs, out_specs=out_spec,
        scratch_shapes=[pltpu.VMEM((bm, bn), F32)], name=name,
        compiler_params=_params("parallel", "parallel", "arbitrary"),
    )(*args)


def _rowwise(body, n_rows, ins, outs, accs=(), *, name, block=ROW_BLOCK):
    blk = min(block, n_rows)
    assert n_rows % blk == 0
    in_specs = []
    for arr, is_row in ins:
        if is_row:
            assert arr.shape[0] == n_rows, (name, arr.shape, n_rows)
            in_specs.append(pl.BlockSpec((blk, arr.shape[1]), lambda i: (i, 0)))
        else:
            in_specs.append(pl.BlockSpec(arr.shape, lambda i, nd=arr.ndim: (0,) * nd))
    out_shape = [jax.ShapeDtypeStruct((n_rows, w), dt) for w, dt in outs]
    out_specs = [pl.BlockSpec((blk, w), lambda i: (i, 0)) for w, _ in outs]
    out_shape += [jax.ShapeDtypeStruct(s, dt) for s, dt in accs]
    out_specs += [pl.BlockSpec(s, lambda i, nd=len(s): (0,) * nd) for s, _ in accs]
    n_in, n_out, n_acc = len(ins), len(outs), len(accs)

    def kern(*refs):
        in_refs = refs[:n_in]
        out_refs = refs[n_in:n_in + n_out]
        acc_refs = refs[n_in + n_out:]
        if n_acc:
            @pl.when(pl.program_id(0) == 0)
            def _():
                for r in acc_refs:
                    r[...] = jnp.zeros_like(r)
        body(in_refs, out_refs, acc_refs)

    return pl.pallas_call(
        kern, out_shape=out_shape, grid=(n_rows // blk,), in_specs=in_specs, out_specs=out_specs,
        name=name, compiler_params=_params("arbitrary"),
    )(*[arr for arr, _ in ins])


def _rms(x, g, n=None):
    n = x.shape[-1] if n is None else n
    r = lax.rsqrt(jnp.sum(x * x, axis=-1, keepdims=True) / n + EPS)
    return x * r * g


def _rms_bwd(x, g, dy, n=None):
    n = x.shape[-1] if n is None else n
    r = lax.rsqrt(jnp.sum(x * x, axis=-1, keepdims=True) / n + EPS)
    gdy = dy * g
    dx = r * (gdy - x * ((r * r) * (jnp.sum(gdy * x, axis=-1, keepdims=True) / n)))
    dg = jnp.sum(dy * x * r, axis=0, keepdims=True)
    return dx, dg


def _swap_halves(x):
    lane = lax.broadcasted_iota(jnp.int32, x.shape, 1)
    return jnp.where(lane < ROPE_DIM // 2, pltpu.roll(x, 128 - ROPE_DIM // 2, 1),
                     pltpu.roll(x, ROPE_DIM // 2, 1))


def _rope(n, cos_t, sin_t):
    return n * cos_t + _swap_halves(n) * sin_t


def _rope_bwd(dy, cos_t, sin_t):
    return dy * cos_t - _swap_halves(dy) * sin_t


def _sigmoid(g):
    return 1.0 / (1.0 + jnp.exp(-g))


def _dot_t(a, b):
    return lax.dot_general(a, b, (((1,), (1,)), ((), ())), preferred_element_type=F32)


def _tdot(a, b):
    return lax.dot_general(a, b, (((0,), (0,)), ((), ())), preferred_element_type=F32)


def _dot(a, b):
    return jnp.dot(a, b, preferred_element_type=F32)


def _hs(h, w=HEAD_DIM, base=0):
    return slice(base + h * w, base + (h + 1) * w)


def _mem_head(qm, gq, mk_h, mv_h):
    qb = _rms(qm, gq).astype(BF16)
    s = _dot_t(qb, mk_h) * (HEAD_DIM ** -0.5)
    e = jnp.exp(s - jnp.max(s, axis=-1, keepdims=True))
    p = e / jnp.sum(e, axis=-1, keepdims=True)
    mo = _dot(p.astype(BF16), mv_h)
    return qb, p, mo


def _mix_fwd(att, gates, c0, mk, mv, gq, *, name):
    n_rows = att.shape[0]

    def body(ins, outs, _):
        att_ref, g_ref, mk_ref, mv_ref, gq_ref = ins
        (o_ref,) = outs
        g = g_ref[:, c0:c0 + SB_W]
        o_ref[:, :SB_W] = (att_ref[...] * (g * _sigmoid(g))).astype(BF16)
        for h in range(N_MEM_HEADS):
            qm = g_ref[:, _hs(h, base=c0 + SB_W)]
            gm = g_ref[:, _hs(h, base=c0 + SB_W + MEM_W)]
            _, _, mo = _mem_head(qm, gq_ref[...], mk_ref[:, _hs(h)], mv_ref[:, _hs(h)])
            o_ref[:, _hs(h, base=SB_W)] = (mo * (gm * _sigmoid(gm))).astype(BF16)

    (mixed,) = _rowwise(body, n_rows,
                        [(att, True), (gates, True), (mk, False), (mv, False), (gq, False)],
                        [(D_MODEL, BF16)], name=name)
    return mixed


def _mix_bwd(dmixed, att, gates, c0, mk, mv, gq, *, name):
    n_rows = att.shape[0]
    scale = HEAD_DIM ** -0.5

    def body(ins, outs, accs):
        dm_ref, att_ref, g_ref, mk_ref, mv_ref, gq_ref = ins
        datt_ref, dg_ref = outs
        dmk_ref, dmv_ref, dgq_ref = accs
        g = g_ref[:, c0:c0 + SB_W]
        sg = _sigmoid(g)
        dm = dm_ref[:, :SB_W]
        datt_ref[...] = dm * (g * sg)
        dg_ref[:, :SB_W] = (dm * att_ref[...] * (sg * (1.0 + g * (1.0 - sg)))).astype(BF16)
        for h in range(N_MEM_HEADS):
            qm = g_ref[:, _hs(h, base=c0 + SB_W)]
            gm = g_ref[:, _hs(h, base=c0 + SB_W + MEM_W)]
            mk_h = mk_ref[:, _hs(h)]
            mv_h = mv_ref[:, _hs(h)]
            qb, p, mo = _mem_head(qm, gq_ref[...], mk_h, mv_h)
            sgm = _sigmoid(gm)
            dmh = dm_ref[:, _hs(h, base=SB_W)]
            dmo = dmh * (gm * sgm)
            dg_ref[:, _hs(h, base=SB_W + MEM_W)] = (
                dmh * mo * (sgm * (1.0 + gm * (1.0 - sgm)))).astype(BF16)
            dmo_b = dmo.astype(BF16)
            pb = p.astype(BF16)
            dp = _dot_t(dmo_b, mv_h)
            dmv_ref[:, _hs(h)] += _tdot(pb, dmo_b)
            ds = (p * (dp - jnp.sum(dp * p, axis=-1, keepdims=True)) * scale).astype(BF16)
            dqn = _dot(ds, mk_h)
            dmk_ref[:, _hs(h)] += _tdot(ds, qb)
            dqm, dgq = _rms_bwd(qm, gq_ref[...], dqn)
            dg_ref[:, _hs(h, base=SB_W)] = dqm.astype(BF16)
            dgq_ref[...] += dgq

    return _rowwise(body, n_rows,
                    [(dmixed, True), (att, True), (gates, True), (mk, False), (mv, False), (gq, False)],
                    [(SB_W, F32), (GATE_W, BF16)],
                    [((MEM_LEN, MEM_W), F32), ((MEM_LEN, MEM_W), F32), ((1, HEAD_DIM), F32)],
                    name=name)


def _mem_side_fwd(mem, g_norm, w_kv, g_k, *, tag):
    def norm_body(ins, outs, _):
        outs[0][...] = _rms(ins[0][...], ins[1][...]).astype(BF16)

    (mn,) = _rowwise(norm_body, MEM_LEN, [(mem, True), (g_norm, False)], [(D_MODEL, BF16)],
                     name=f"mem_norm_{tag}")
    mkv = _mm(mn, w_kv, name=f"mem_kv_{tag}")

    def kv_body(ins, outs, _):
        mkv_ref, gk_ref = ins
        mk_ref, mv_ref = outs
        for h in range(N_MEM_HEADS):
            mk_ref[:, _hs(h)] = _rms(mkv_ref[:, _hs(h)], gk_ref[...]).astype(BF16)
        mv_ref[...] = mkv_ref[:, MEM_W:].astype(BF16)

    mk, mv = _rowwise(kv_body, MEM_LEN, [(mkv, True), (g_k, False)], [(MEM_W, BF16), (MEM_W, BF16)],
                      name=f"mem_kv_prep_{tag}")
    return mn, mkv, mk, mv


def _mem_side_bwd(mem, g_norm, w_kv, g_k, mn, mkv, dmk, dmv, *, tag):
    def kv_body(ins, outs, accs):
        mkv_ref, gk_ref, dmk_ref, dmv_ref = ins
        (d_ref,) = outs
        (dgk_ref,) = accs
        for h in range(N_MEM_HEADS):
            dx, dg = _rms_bwd(mkv_ref[:, _hs(h)], gk_ref[...], dmk_ref[:, _hs(h)])
            d_ref[:, _hs(h)] = dx.astype(BF16)
            dgk_ref[...] += dg
        d_ref[:, MEM_W:] = dmv_ref[...].astype(BF16)

    dmkv, dgk = _rowwise(kv_body, MEM_LEN, [(mkv, True), (g_k, False), (dmk, True), (dmv, True)],
                         [(2 * MEM_W, BF16)], [((1, HEAD_DIM), F32)], name=f"mem_kv_prep_bwd_{tag}")
    dmn = _mm(dmkv, w_kv, tb=True, name=f"mem_kv_dx_{tag}")
    dw = _mm(mn, dmkv, ta=True, out_dtype=BF16, name=f"mem_kv_dw_{tag}")

    def norm_body(ins, outs, accs):
        _, dg = _rms_bwd(ins[0][...], ins[1][...], ins[2][...])
        accs[0][...] += dg

    (dgn,) = _rowwise(norm_body, MEM_LEN, [(mem, True), (g_norm, False), (dmn, True)], [],
                      [((1, D_MODEL), F32)], name=f"mem_norm_bwd_{tag}")
    return dw, dgn, dgk


LOG2_E = 1.4426950408889634
SB_Q_SCALE = HEAD_DIM ** -0.5 * LOG2_E


def _log2_one_minus_beta(z2):
    nz = -z2
    u = jnp.exp2(jnp.minimum(z2, nz))
    return jnp.minimum(nz, 0.0) - jnp.log2(1.0 + u)


def _chain_modes(s, qb):
    return tuple(None if t < s else ("m" if t == s else "f") for t in range(qb))


def _split_dot(x, tri2):
    hi = x.astype(BF16)
    lo = (x - hi.astype(F32)).astype(BF16)
    return _dot(jnp.concatenate([hi, lo], axis=1), tri2)


def _sb_fwd(qkv, *, name, hp=2):
    seq = qkv.shape[0]
    blk = min(ATT_BLOCK, seq)
    nkb = seq // blk
    qb = 2 if nkb % 2 == 0 else 1
    rows = qb * blk
    scale = HEAD_DIM ** -0.5
    chains = [(t, s) for t in range(hp) for s in range(qb)]

    def body(q_ref, k_ref, v_ref, o_ref):
        base = pl.program_id(1) * qb
        qs = {(t, s): q_ref[s * blk:(s + 1) * blk, _hs(t)] for t, s in chains}
        row = lax.broadcasted_iota(jnp.int32, (blk, blk), 0)
        col = lax.broadcasted_iota(jnp.int32, (blk, blk), 1)
        after = (row > col).astype(BF16)
        after2 = jnp.concatenate([after, after], axis=0)
        causal = col < row

        def step(j, carry, modes):
            off = pl.multiple_of(j * blk, blk)
            act = [c for c in chains if modes[c[1]]]
            zs = {c: _dot_t(qs[c], k_ref[pl.ds(off, blk), _hs(c[0])]) for c in act}
            ls = {}
            for c in act:
                l = _log2_one_minus_beta(zs[c])
                ls[c] = jnp.where(causal, l, 0.0) if modes[c[1]] == "m" else l
            cs = {c: _split_dot(ls[c], after2) for c in act}
            carry = dict(carry)
            for c in act:
                run, acc = carry[c]
                a = jnp.exp2(zs[c] + ls[c] + cs[c] + run)
                if modes[c[1]] == "m":
                    a = jnp.where(causal, a, 0.0)
                acc = acc + _dot(a.astype(BF16), v_ref[pl.ds(off, blk), _hs(c[0])])
                carry[c] = (run + jnp.sum(ls[c], axis=-1, keepdims=True), acc)
            return carry

        init = (jnp.zeros((blk, 1), F32), jnp.zeros((blk, HEAD_DIM), F32))
        carry = {c: init for c in chains}
        for s in reversed(range(qb)):
            carry = step(base + s, carry, _chain_modes(s, qb))
        carry = lax.fori_loop(0, base, lambda jj, c: step(base - 1 - jj, c, ("f",) * qb), carry)
        for t, s in chains:
            o_ref[s * blk:(s + 1) * blk, _hs(t)] = carry[(t, s)][1]

    nh = N_SB_HEADS // hp
    return pl.pallas_call(
        body, out_shape=jax.ShapeDtypeStruct((seq, SB_W), F32), grid=(nh, nkb // qb),
        in_specs=[pl.BlockSpec((rows, hp * HEAD_DIM), lambda h, i: (i, h)),
                  pl.BlockSpec((seq, hp * HEAD_DIM), lambda h, i: (0, nh + h)),
                  pl.BlockSpec((seq, hp * HEAD_DIM), lambda h, i: (0, 2 * nh + h))],
        out_specs=pl.BlockSpec((rows, hp * HEAD_DIM), lambda h, i: (i, h)),
        name=name, compiler_params=_params("parallel", "arbitrary"),
    )(qkv, qkv, qkv)


def _sb_bwd(qkv, dout, *, name):
    seq = qkv.shape[0]
    blk = min(ATT_BLOCK, seq)
    nkb = seq // blk
    qb = _pick(nkb, (4, 2, 1))
    rows = qb * blk
    scale = HEAD_DIM ** -0.5

    def body(q_ref, k_ref, v_ref, do_ref, dq_ref, dk_ref, dv_ref, de_s):
        g = pl.program_id(1)
        base = g * qb

        @pl.when(g == 0)
        def _():
            dk_ref[...] = jnp.zeros_like(dk_ref)
            dv_ref[...] = jnp.zeros_like(dv_ref)

        qs = [q_ref[t * blk:(t + 1) * blk, :] for t in range(qb)]
        dos = [do_ref[t * blk:(t + 1) * blk, :].astype(BF16) for t in range(qb)]
        row = lax.broadcasted_iota(jnp.int32, (blk, blk), 0)
        col = lax.broadcasted_iota(jnp.int32, (blk, blk), 1)
        after = (row > col).astype(BF16)
        after2 = jnp.concatenate([after, after], axis=0)
        before = (row < col).astype(BF16)
        causal = col < row

        def sweep1(j, runs, modes):
            off = pl.multiple_of(j * blk, blk)
            kb = k_ref[pl.ds(off, blk), :]
            vb = v_ref[pl.ds(off, blk), :]
            act = [t for t in range(qb) if modes[t]]
            zs = {t: _dot_t(qs[t], kb) for t in act}
            ls = {}
            for t in act:
                l = _log2_one_minus_beta(zs[t])
                ls[t] = jnp.where(causal, l, 0.0) if modes[t] == "m" else l
            cs = {t: _split_dot(ls[t], after2) for t in act}
            runs = list(runs)
            dv_inc = None
            for t in act:
                a = jnp.exp2(zs[t] + ls[t] + cs[t] + runs[t])
                if modes[t] == "m":
                    a = jnp.where(causal, a, 0.0)
                de_s[t, j] = (a * _dot_t(dos[t], vb)).astype(BF16)
                inc = _tdot(a.astype(BF16), dos[t])
                dv_inc = inc if dv_inc is None else dv_inc + inc
                runs[t] = runs[t] + jnp.sum(ls[t], axis=-1, keepdims=True)
            dv_ref[pl.ds(off, blk), :] += dv_inc
            return tuple(runs)

        runs = (jnp.zeros((blk, 1), F32),) * qb
        for s in reversed(range(qb)):
            runs = sweep1(base + s, runs, _chain_modes(s, qb))
        lax.fori_loop(0, base, lambda jj, r: sweep1(base - 1 - jj, r, ("f",) * qb), runs)

        def sweep2(j, carry, modes):
            lefts, dqs = list(carry[0]), list(carry[1])
            off = pl.multiple_of(j * blk, blk)
            kb = k_ref[pl.ds(off, blk), :]
            act = [t for t in range(qb) if modes[t]]
            zs = {t: _dot_t(qs[t], kb) for t in act}
            debs = {t: de_s[t, j] for t in act}
            pres = {t: _dot(debs[t], before) for t in act}
            dk_inc = None
            for t in act:
                de = debs[t].astype(F32)
                pre = pres[t] + lefts[t]
                sn = pl.reciprocal(1.0 + jnp.exp2(zs[t]), approx=True)
                dz = (de + pre) * sn - pre
                if modes[t] == "m":
                    dz = jnp.where(causal, dz, 0.0)
                dzb = dz.astype(BF16)
                dqs[t] = dqs[t] + _dot(dzb, kb)
                inc = _tdot(dzb, qs[t])
                dk_inc = inc if dk_inc is None else dk_inc + inc
                lefts[t] = lefts[t] + jnp.sum(de, axis=-1, keepdims=True)
            dk_ref[pl.ds(off, blk), :] += dk_inc
            return tuple(lefts), tuple(dqs)

        carry = ((jnp.zeros((blk, 1), F32),) * qb, (jnp.zeros((blk, HEAD_DIM), F32),) * qb)
        carry = lax.fori_loop(0, base, lambda j, c: sweep2(j, c, ("f",) * qb), carry)
        for s in range(qb):
            carry = sweep2(base + s, carry, _chain_modes(s, qb))
        for t in range(qb):
            dq_ref[t * blk:(t + 1) * blk, :] = carry[1][t] * scale

        @pl.when(g == pl.num_programs(1) - 1)
        def _():
            dk_ref[...] = dk_ref[...] * (1.0 / LOG2_E)

    out = jax.ShapeDtypeStruct((seq, SB_W), F32)
    return pl.pallas_call(
        body, out_shape=[out, out, out], grid=(N_SB_HEADS, nkb // qb),
        in_specs=[pl.BlockSpec((rows, HEAD_DIM), lambda h, i: (i, h)),
                  pl.BlockSpec((seq, HEAD_DIM), lambda h, i: (0, N_SB_HEADS + h)),
                  pl.BlockSpec((seq, HEAD_DIM), lambda h, i: (0, 2 * N_SB_HEADS + h)),
                  pl.BlockSpec((rows, HEAD_DIM), lambda h, i: (i, h))],
        out_specs=[pl.BlockSpec((rows, HEAD_DIM), lambda h, i: (i, h)),
                   pl.BlockSpec((seq, HEAD_DIM), lambda h, i: (0, h)),
                   pl.BlockSpec((seq, HEAD_DIM), lambda h, i: (0, h))],
        scratch_shapes=[pltpu.VMEM((qb, nkb, blk, blk), BF16)],
        name=name, compiler_params=_params("parallel", "arbitrary"),
    )(qkv, qkv, qkv, dout)


MLA_SCALE = (HEAD_DIM + ROPE_DIM) ** -0.5
MLA_Q_SCALE = MLA_SCALE * LOG2_E


def _mla_fwd(q_cat, k_cat, v, *, name, hp=2):
    seq = q_cat.shape[0]
    blk = min(ATT_BLOCK, seq)
    nkb = seq // blk
    qb = 2 if nkb % 2 == 0 else 1
    rows = qb * blk
    chains = [(t, s) for t in range(hp) for s in range(qb)]

    def body(q_ref, k_ref, v_ref, o_ref, lse_ref):
        base = pl.program_id(1) * qb
        qs = {(t, s): q_ref[s * blk:(s + 1) * blk, t * CAT_W:(t + 1) * CAT_W] for t, s in chains}
        row = lax.broadcasted_iota(jnp.int32, (blk, blk), 0)
        col = lax.broadcasted_iota(jnp.int32, (blk, blk), 1)
        causal = col <= row

        def step(j, carry, modes):
            off = pl.multiple_of(j * blk, blk)
            act = [c for c in chains if modes[c[1]]]
            ss = {c: _dot_t(qs[c], k_ref[pl.ds(off, blk), c[0] * CAT_W:(c[0] + 1) * CAT_W]) for c in act}
            carry = dict(carry)
            for c in act:
                m, l, acc = carry[c]
                s = ss[c]
                if modes[c[1]] == "m":
                    s = jnp.where(causal, s, -jnp.inf)
                m_new = jnp.maximum(m, jnp.max(s, axis=-1, keepdims=True))
                p = jnp.exp2(s - m_new)
                alpha = jnp.exp2(m - m_new)
                l = alpha * l + jnp.sum(p, axis=-1, keepdims=True)
                acc = alpha * acc + _dot(p.astype(BF16), v_ref[pl.ds(off, blk), _hs(c[0])])
                carry[c] = (m_new, l, acc)
            return carry

        init = (jnp.full((blk, 1), -jnp.inf, F32), jnp.zeros((blk, 1), F32),
                jnp.zeros((blk, HEAD_DIM), F32))
        carry = {c: init for c in chains}
        carry = lax.fori_loop(0, base, lambda j, c: step(j, c, ("f",) * qb), carry)
        for s in range(qb):
            carry = step(base + s, carry, _chain_modes(s, qb))
        for t, s in chains:
            m, l, acc = carry[(t, s)]
            o_ref[s * blk:(s + 1) * blk, _hs(t)] = acc / l
            lse_ref[s * blk:(s + 1) * blk, _hs(t)] = jnp.broadcast_to(
                (m + jnp.log2(l)) * (1.0 / LOG2_E), (blk, HEAD_DIM))

    out = jax.ShapeDtypeStruct((seq, MLA_W), F32)
    return pl.pallas_call(
        body, out_shape=[out, out], grid=(N_MLA_HEADS // hp, nkb // qb),
        in_specs=[pl.BlockSpec((rows, hp * CAT_W), lambda h, i: (i, h)),
                  pl.BlockSpec((seq, hp * CAT_W), lambda h, i: (0, h)),
                  pl.BlockSpec((seq, hp * HEAD_DIM), lambda h, i: (0, h))],
        out_specs=[pl.BlockSpec((rows, hp * HEAD_DIM), lambda h, i: (i, h)),
                   pl.BlockSpec((rows, hp * HEAD_DIM), lambda h, i: (i, h))],
        name=name, compiler_params=_params("parallel", "arbitrary"),
    )(q_cat, k_cat, v)


def _mla_bwd(q_cat, k_cat, v, out, lse, dout, *, name):
    seq = q_cat.shape[0]
    blk = min(ATT_BLOCK, seq)
    nkb = seq // blk
    qb = _pick(nkb, (4, 2, 1))
    rows = qb * blk

    def body(q_ref, k_ref, v_ref, o_ref, lse_ref, do_ref, dq_ref, dk_ref, dv_ref):
        g = pl.program_id(1)
        base = g * qb

        @pl.when(g == 0)
        def _():
            dk_ref[...] = jnp.zeros_like(dk_ref)
            dv_ref[...] = jnp.zeros_like(dv_ref)

        qs, dobs, deltas, lses = [], [], [], []
        for t in range(qb):
            rs = slice(t * blk, (t + 1) * blk)
            do = do_ref[rs, :]
            qs.append(q_ref[rs, :])
            dobs.append(do.astype(BF16))
            deltas.append(jnp.sum(do * o_ref[rs, :], axis=-1, keepdims=True))
            lses.append(lse_ref[rs, :1] * LOG2_E)
        row = lax.broadcasted_iota(jnp.int32, (blk, blk), 0)
        col = lax.broadcasted_iota(jnp.int32, (blk, blk), 1)
        causal = col <= row

        def step(j, dqs, modes):
            off = pl.multiple_of(j * blk, blk)
            kb = k_ref[pl.ds(off, blk), :]
            vb = v_ref[pl.ds(off, blk), :]
            act = [t for t in range(qb) if modes[t]]
            ss = {t: _dot_t(qs[t], kb) for t in act}
            dps = {t: _dot_t(dobs[t], vb) for t in act}
            dqs = list(dqs)
            dv_inc = dk_inc = None
            for t in act:
                p = jnp.exp2(ss[t] - lses[t])
                if modes[t] == "m":
                    p = jnp.where(causal, p, 0.0)
                ds = (p * (dps[t] - deltas[t])).astype(BF16)
                inc_v = _tdot(p.astype(BF16), dobs[t])
                inc_k = _tdot(ds, qs[t])
                dv_inc = inc_v if dv_inc is None else dv_inc + inc_v
                dk_inc = inc_k if dk_inc is None else dk_inc + inc_k
                dqs[t] = dqs[t] + _dot(ds, kb)
            dv_ref[pl.ds(off, blk), :] += dv_inc
            dk_ref[pl.ds(off, blk), :] += dk_inc
            return tuple(dqs)

        dqs = (jnp.zeros((blk, CAT_W), F32),) * qb
        dqs = lax.fori_loop(0, base, lambda j, c: step(j, c, ("f",) * qb), dqs)
        for s in range(qb):
            modes = tuple(None if t < s else ("m" if t == s else "f") for t in range(qb))
            dqs = step(base + s, dqs, modes)
        for t in range(qb):
            dq_ref[t * blk:(t + 1) * blk, :] = dqs[t] * MLA_SCALE

        @pl.when(g == pl.num_programs(1) - 1)
        def _():
            dk_ref[...] = dk_ref[...] * (1.0 / LOG2_E)

    return pl.pallas_call(
        body,
        out_shape=[jax.ShapeDtypeStruct((seq, N_MLA_HEADS * CAT_W), F32),
                   jax.ShapeDtypeStruct((seq, N_MLA_HEADS * CAT_W), F32),
                   jax.ShapeDtypeStruct((seq, MLA_W), F32)],
        grid=(N_MLA_HEADS, nkb // qb),
        in_specs=[pl.BlockSpec((rows, CAT_W), lambda h, i: (i, h)),
                  pl.BlockSpec((seq, CAT_W), lambda h, i: (0, h)),
                  pl.BlockSpec((seq, HEAD_DIM), lambda h, i: (0, h)),
                  pl.BlockSpec((rows, HEAD_DIM), lambda h, i: (i, h)),
                  pl.BlockSpec((rows, HEAD_DIM), lambda h, i: (i, h)),
                  pl.BlockSpec((rows, HEAD_DIM), lambda h, i: (i, h))],
        out_specs=[pl.BlockSpec((rows, CAT_W), lambda h, i: (i, h)),
                   pl.BlockSpec((seq, CAT_W), lambda h, i: (0, h)),
                   pl.BlockSpec((seq, HEAD_DIM), lambda h, i: (0, h))],
        name=name, compiler_params=_params("parallel", "arbitrary"),
    )(q_cat, k_cat, v, out, lse, dout)


def _local_step(x, mem, positions, target, w, g):
    seq = x.shape[0]
    inv_freq = jnp.power(ROPE_THETA, -jnp.arange(0, ROPE_DIM, 2, dtype=F32) / ROPE_DIM)
    ang = positions.astype(F32)[:, None] * inv_freq
    cos, sin = jnp.cos(ang), jnp.sin(ang)
    lane_pad = jnp.zeros((seq, HEAD_DIM - ROPE_DIM), F32)
    cos_t = jnp.concatenate([cos, cos, lane_pad], axis=1)
    sin_t = jnp.concatenate([-sin, sin, lane_pad], axis=1)
    gain_pad = jnp.zeros((1, HEAD_DIM - ROPE_DIM), F32)
    g_k_rope = jnp.concatenate([g["g_k_rope"], gain_pad], axis=1)
    g_q_rope = jnp.concatenate([g["b_g_q_rope"], gain_pad], axis=1)

    def norm_to_bf16(src, gain, name):
        def body(ins, outs, _):
            outs[0][...] = _rms(ins[0][...], ins[1][...]).astype(BF16)
        return _rowwise(body, seq, [(src, True), (gain, False)], [(src.shape[1], BF16)], name=name)[0]

    h_a = norm_to_bf16(x, g["a_norm"], "a_norm_fwd")
    qkv = _mm(h_a, w["a_in_qkv"], out_dtype=BF16, scale_cols=(SB_W, SB_Q_SCALE), name="a_in_qkv")
    gr = _mm(h_a, w["a_in_gate"], name="a_in_gate")
    sb = _sb_fwd(qkv, name="sb_fwd")
    mem0 = _mem_side_fwd(mem, g["mem_norm"][0:1], w["mem_kv"][0], g["g_mem_k"][0:1], tag="a")
    mixed_a = _mix_fwd(sb, gr, 0, mem0[2], mem0[3], g["g_mem_q"][0:1], name="a_mix_fwd")
    x1 = _mm(mixed_a, w["a_out"], res=x, name="a_out")

    def norms2_body(ins, outs, _):
        xv = ins[0][...]
        outs[0][...] = _rms(xv, ins[1][...]).astype(BF16)
        outs[1][...] = _rms(xv, ins[2][...]).astype(BF16)

    h_kv, h_b = _rowwise(norms2_body, seq, [(x1, True), (g["kv_norm"], False), (g["b_norm"], False)],
                         [(D_MODEL, BF16), (D_MODEL, BF16)], name="kv_b_norm_fwd")
    ckr = _mm(h_kv, w["dkv"], name="dkv")

    def ckr_body(ins, outs, _):
        ckr_ref, gc_ref, gr_ref, c_ref, s_ref = ins
        outs[0][...] = _rms(ckr_ref[:, :KV_LORA], gc_ref[...]).astype(BF16)
        kr = _rms(ckr_ref[:, KV_LORA:], gr_ref[...], n=ROPE_DIM)
        outs[1][...] = _rope(kr, c_ref[...], s_ref[...]).astype(BF16)

    c_n, k_r = _rowwise(ckr_body, seq,
                        [(ckr, True), (g["g_ckv"], False), (g_k_rope, False), (cos_t, True), (sin_t, True)],
                        [(KV_LORA, BF16), (HEAD_DIM, BF16)], name="ckv_prep_fwd")
    kv = _mm(c_n, w["ukv"], name="ukv")

    def kcat_body(ins, outs, _):
        kv_ref, kr_ref, gk_ref = ins
        kc_ref, v_ref = outs
        for h in range(N_MLA_HEADS):
            kc_ref[:, h * CAT_W:h * CAT_W + HEAD_DIM] = _rms(
                kv_ref[:, h * CAT_W:h * CAT_W + HEAD_DIM], gk_ref[...]).astype(BF16)
            kc_ref[:, h * CAT_W + HEAD_DIM:(h + 1) * CAT_W] = kr_ref[...]
            v_ref[:, _hs(h)] = kv_ref[:, h * CAT_W + HEAD_DIM:(h + 1) * CAT_W].astype(BF16)

    k_cat, v_mla = _rowwise(kcat_body, seq, [(kv, True), (k_r, True), (g["g_k_nope"], False)],
                            [(N_MLA_HEADS * CAT_W, BF16), (MLA_W, BF16)], name="k_prep_fwd")

    p2 = _mm(h_b, w["b_in"], name="b_in")

    def qlat_body(ins, outs, _):
        outs[0][...] = _rms(ins[0][:, :Q_LORA], ins[1][...]).astype(BF16)

    (q_l,) = _rowwise(qlat_body, seq, [(p2, True), (g["b_g_q_lat"], False)], [(Q_LORA, BF16)],
                      name="q_lat_norm_fwd")
    q_up = _mm(q_l, w["uq"], name="uq")

    def qcat_body(ins, outs, _):
        q_ref, gn_ref, gr_ref, c_ref, s_ref = ins
        (o_ref,) = outs
        for h in range(N_MLA_HEADS):
            o_ref[:, h * CAT_W:h * CAT_W + HEAD_DIM] = (MLA_Q_SCALE * _rms(
                q_ref[:, h * CAT_W:h * CAT_W + HEAD_DIM], gn_ref[...])).astype(BF16)
            qr = _rms(q_ref[:, h * CAT_W + HEAD_DIM:(h + 1) * CAT_W], gr_ref[...], n=ROPE_DIM)
            o_ref[:, h * CAT_W + HEAD_DIM:(h + 1) * CAT_W] = (
                MLA_Q_SCALE * _rope(qr, c_ref[...], s_ref[...])).astype(BF16)

    (q_cat,) = _rowwise(qcat_body, seq,
                        [(q_up, True), (g["b_g_q_nope"], False), (g_q_rope, False), (cos_t, True), (sin_t, True)],
                        [(N_MLA_HEADS * CAT_W, BF16)], name="q_prep_fwd")
    att, lse = _mla_fwd(q_cat, k_cat, v_mla, name="mla_fwd")
    mem1 = _mem_side_fwd(mem, g["mem_norm"][1:2], w["mem_kv"][1], g["g_mem_k"][1:2], tag="b")
    mixed_b = _mix_fwd(att, p2, Q_LORA, mem1[2], mem1[3], g["g_mem_q"][1:2], name="b_mix_fwd")
    y = _mm(mixed_b, w["b_out"], res=x1, name="b_out")

    def loss_body(ins, outs, accs):
        diff = ins[0][...] - ins[1][...]
        outs[0][...] = diff / D_MODEL
        col = jnp.sum(diff * diff, axis=0, keepdims=True)
        part = col[:, :HEAD_DIM]
        for c in range(1, D_MODEL // HEAD_DIM):
            part = part + col[:, _hs(c)]
        accs[0][...] += part * (0.5 / D_MODEL)

    dy, loss_part = _rowwise(loss_body, seq, [(y, True), (target, True)], [(D_MODEL, F32)],
                             [((1, HEAD_DIM), F32)], name="loss")

    gw, gg = {}, {}
    dmixed_b = _mm(dy, w["b_out"], tb=True, name="b_out_dx")
    gw["b_out"] = _mm(mixed_b, dy, ta=True, out_dtype=BF16, name="b_out_dw")
    datt, dgate_b, dmk1, dmv1, gq1 = _mix_bwd(dmixed_b, att, p2, Q_LORA, mem1[2], mem1[3],
                                              g["g_mem_q"][1:2], name="b_mix_bwd")
    dq_cat, dk_cat, dv_mla = _mla_bwd(q_cat, k_cat, v_mla, att, lse, datt, name="mla_bwd")

    def qcat_bwd_body(ins, outs, accs):
        q_ref, dq_ref, gn_ref, gr_ref, c_ref, s_ref = ins
        (o_ref,) = outs
        dgn_ref, dgr_ref = accs
        for h in range(N_MLA_HEADS):
            dx, dg = _rms_bwd(q_ref[:, h * CAT_W:h * CAT_W + HEAD_DIM], gn_ref[...],
                              dq_ref[:, h * CAT_W:h * CAT_W + HEAD_DIM])
            o_ref[:, h * CAT_W:h * CAT_W + HEAD_DIM] = dx.astype(BF16)
            dgn_ref[...] += dg
            dn = _rope_bwd(dq_ref[:, h * CAT_W + HEAD_DIM:(h + 1) * CAT_W], c_ref[...], s_ref[...])
            dx, dg = _rms_bwd(q_ref[:, h * CAT_W + HEAD_DIM:(h + 1) * CAT_W], gr_ref[...], dn, n=ROPE_DIM)
            o_ref[:, h * CAT_W + HEAD_DIM:(h + 1) * CAT_W] = dx.astype(BF16)
            dgr_ref[...] += dg

    dq_up, gg["b_g_q_nope"], dgqr = _rowwise(
        qcat_bwd_body, seq,
        [(q_up, True), (dq_cat, True), (g["b_g_q_nope"], False), (g_q_rope, False), (cos_t, True), (sin_t, True)],
        [(N_MLA_HEADS * CAT_W, BF16)], [((1, HEAD_DIM), F32), ((1, HEAD_DIM), F32)], name="q_prep_bwd")
    gg["b_g_q_rope"] = dgqr
    dq_l = _mm(dq_up, w["uq"], tb=True, name="uq_dx")
    gw["uq"] = _mm(q_l, dq_up, ta=True, out_dtype=BF16, n_split=N_CHIPS, name="uq_dw")

    def qlat_bwd_body(ins, outs, accs):
        p2_ref, dql_ref, dgate_ref, gl_ref = ins
        dx, dg = _rms_bwd(p2_ref[:, :Q_LORA], gl_ref[...], dql_ref[...])
        outs[0][:, :Q_LORA] = dx.astype(BF16)
        outs[0][:, Q_LORA:] = dgate_ref[...]
        accs[0][...] += dg

    dp2, gg["b_g_q_lat"] = _rowwise(
        qlat_bwd_body, seq, [(p2, True), (dq_l, True), (dgate_b, True), (g["b_g_q_lat"], False)],
        [(Q_LORA + GATE_W, BF16)], [((1, Q_LORA), F32)], name="q_lat_norm_bwd")
    dh_b = _mm(dp2, w["b_in"], tb=True, name="b_in_dx")
    gw["b_in"] = _mm(h_b, dp2, ta=True, out_dtype=BF16, n_split=N_CHIPS, name="b_in_dw")

    def kcat_bwd_body(ins, outs, accs):
        kv_ref, dkc_ref, dv_ref, gk_ref = ins
        dkv_ref, dkr_ref = outs
        (dgk_ref,) = accs
        dkr = jnp.zeros(dkr_ref.shape, F32)
        for h in range(N_MLA_HEADS):
            dx, dg = _rms_bwd(kv_ref[:, h * CAT_W:h * CAT_W + HEAD_DIM], gk_ref[...],
                              dkc_ref[:, h * CAT_W:h * CAT_W + HEAD_DIM])
            dkv_ref[:, h * CAT_W:h * CAT_W + HEAD_DIM] = dx.astype(BF16)
            dgk_ref[...] += dg
            dkv_ref[:, h * CAT_W + HEAD_DIM:(h + 1) * CAT_W] = dv_ref[:, _hs(h)].astype(BF16)
            dkr = dkr + dkc_ref[:, h * CAT_W + HEAD_DIM:(h + 1) * CAT_W]
        dkr_ref[...] = dkr

    dkv, dk_r, gg["g_k_nope"] = _rowwise(
        kcat_bwd_body, seq, [(kv, True), (dk_cat, True), (dv_mla, True), (g["g_k_nope"], False)],
        [(N_MLA_HEADS * CAT_W, BF16), (HEAD_DIM, F32)], [((1, HEAD_DIM), F32)], name="k_prep_bwd")
    dc_n = _mm(dkv, w["ukv"], tb=True, name="ukv_dx")
    gw["ukv"] = _mm(c_n, dkv, ta=True, out_dtype=BF16, n_split=N_CHIPS, name="ukv_dw")

    def ckr_bwd_body(ins, outs, accs):
        ckr_ref, dcn_ref, dkr_ref, gc_ref, gr_ref, c_ref, s_ref = ins
        dx, dg = _rms_bwd(ckr_ref[:, :KV_LORA], gc_ref[...], dcn_ref[...])
        outs[0][:, :KV_LORA] = dx.astype(BF16)
        accs[0][...] += dg
        dn = _rope_bwd(dkr_ref[...], c_ref[...], s_ref[...])
        dx, dg = _rms_bwd(ckr_ref[:, KV_LORA:], gr_ref[...], dn, n=ROPE_DIM)
        outs[0][:, KV_LORA:] = dx.astype(BF16)
        accs[1][...] += dg

    dckr, gg["g_ckv"], gg["g_k_rope"] = _rowwise(
        ckr_bwd_body, seq,
        [(ckr, True), (dc_n, True), (dk_r, True), (g["g_ckv"], False), (g_k_rope, False),
         (cos_t, True), (sin_t, True)],
        [(KV_LORA + HEAD_DIM, BF16)], [((1, KV_LORA), F32), ((1, HEAD_DIM), F32)], name="ckv_prep_bwd")
    dh_kv = _mm(dckr, w["dkv"], tb=True, name="dkv_dx")
    gw["dkv"] = _mm(h_kv, dckr, ta=True, out_dtype=BF16, name="dkv_dw")

    def norms2_bwd_body(ins, outs, accs):
        x_ref, dy_ref, dhk_ref, dhb_ref, gk_ref, gb_ref = ins
        xv = x_ref[...]
        dxk, dgk = _rms_bwd(xv, gk_ref[...], dhk_ref[...])
        dxb, dgb = _rms_bwd(xv, gb_ref[...], dhb_ref[...])
        outs[0][...] = dy_ref[...] + dxk + dxb
        accs[0][...] += dgk
        accs[1][...] += dgb

    dx1, gg["kv_norm"], gg["b_norm"] = _rowwise(
        norms2_bwd_body, seq,
        [(x1, True), (dy, True), (dh_kv, True), (dh_b, True), (g["kv_norm"], False), (g["b_norm"], False)],
        [(D_MODEL, F32)], [((1, D_MODEL), F32), ((1, D_MODEL), F32)], name="kv_b_norm_bwd")

    dmixed_a = _mm(dx1, w["a_out"], tb=True, name="a_out_dx")
    gw["a_out"] = _mm(mixed_a, dx1, ta=True, out_dtype=BF16, name="a_out_dw")
    dsb, dgate_a, dmk0, dmv0, gq0 = _mix_bwd(dmixed_a, sb, gr, 0, mem0[2], mem0[3],
                                             g["g_mem_q"][0:1], name="a_mix_bwd")
    dq, dk, dv = _sb_bwd(qkv, dsb, name="sb_bwd")
    dp_a = jnp.concatenate([dq.astype(BF16), dk.astype(BF16), dv.astype(BF16), dgate_a], axis=1)
    dh_a = _mm(dp_a, w["a_in"], tb=True, name="a_in_dx")
    gw["a_in"] = _mm(h_a, dp_a, ta=True, out_dtype=BF16, n_split=N_CHIPS, name="a_in_dw")

    def norm_a_bwd_body(ins, outs, accs):
        dx, dg = _rms_bwd(ins[0][...], ins[3][...], ins[2][...])
        outs[0][...] = ins[1][...] + dx
        accs[0][...] += dg

    grad_x, gg["a_norm"] = _rowwise(
        norm_a_bwd_body, seq, [(x, True), (dx1, True), (dh_a, True), (g["a_norm"], False)],
        [(D_MODEL, F32)], [((1, D_MODEL), F32)], name="a_norm_bwd")

    dw0, dgn0, dgk0 = _mem_side_bwd(mem, g["mem_norm"][0:1], w["mem_kv"][0], g["g_mem_k"][0:1],
                                    mem0[0], mem0[1], dmk0, dmv0, tag="a")
    dw1, dgn1, dgk1 = _mem_side_bwd(mem, g["mem_norm"][1:2], w["mem_kv"][1], g["g_mem_k"][1:2],
                                    mem1[0], mem1[1], dmk1, dmv1, tag="b")
    gw["mem_kv"] = (dw0, dw1)
    gg["mem_norm"] = jnp.concatenate([dgn0, dgn1], axis=0)
    gg["g_mem_q"] = jnp.concatenate([gq0, gq1], axis=0)
    gg["g_mem_k"] = jnp.concatenate([dgk0, dgk1], axis=0)
    return loss_part, grad_x, gw, gg


HBM_SPEC = pl.BlockSpec(memory_space=pl.ANY)


def _other_chips():
    x, y = lax.axis_index("x"), lax.axis_index("y")
    return [(1 - x, y), (x, 1 - y), (1 - x, 1 - y)]


def _allgather_chips(shards):
    n = len(shards)
    split = [s.shape[0] % 32 == 0 for s in shards]

    def body(*refs):
        ins, outs = refs[:n], refs[n:2 * n]
        send, recv, fsend, frecv, loc = refs[2 * n:]
        x, y, c = lax.axis_index("x"), lax.axis_index("y"), lax.axis_index("c")
        me = 2 * x + y
        chips = _other_chips()

        def part(ref, wi):
            if not split[wi]:
                return ref
            half = shards[wi].shape[0] // 2
            return ref.at[pl.ds(pl.multiple_of(c * half, 16), half)]

        def ici(wi, k, src_chip, to):
            return pltpu.make_async_remote_copy(
                src_ref=part(ins[wi], wi), dst_ref=part(outs[wi].at[src_chip], wi),
                send_sem=send.at[wi, k], recv_sem=recv.at[wi, k], device_id=to, device_id_type=MESH)

        def d2d(wi, k, src_chip):
            rows = part(outs[wi].at[src_chip], wi)
            return pltpu.make_async_remote_copy(
                src_ref=rows, dst_ref=rows, send_sem=fsend.at[wi, k], recv_sem=frecv.at[wi, k],
                device_id=(x, y, 1 - c), device_id_type=MESH)

        started = []
        for wi in range(n):
            own = pltpu.make_async_copy(ins[wi], outs[wi].at[me], loc.at[wi])
            own.start()
            started.append(own)
            for k, (tx, ty) in enumerate(chips):
                ici(wi, k, me, (tx, ty, c)).start()
        for wi in range(n):
            for k, (tx, ty) in enumerate(chips):
                landed = ici(wi, k, 2 * tx + ty, (tx, ty, c))
                landed.wait_recv()
                if split[wi]:
                    d2d(wi, k, 2 * tx + ty).start()
        for wi in range(n):
            for k, (tx, ty) in enumerate(chips):
                ici(wi, k, me, (tx, ty, c)).wait_send()
                if split[wi]:
                    fwd = d2d(wi, k, 2 * tx + ty)
                    fwd.wait_send()
                    fwd.wait_recv()
        for own in started:
            own.wait()

    return pl.pallas_call(
        body, out_shape=[jax.ShapeDtypeStruct((N_CHIPS,) + s.shape, s.dtype) for s in shards],
        in_specs=[HBM_SPEC] * n, out_specs=[HBM_SPEC] * n,
        scratch_shapes=[pltpu.SemaphoreType.DMA((n, 3)), pltpu.SemaphoreType.DMA((n, 3)),
                        pltpu.SemaphoreType.DMA((n, 3)), pltpu.SemaphoreType.DMA((n, 3)),
                        pltpu.SemaphoreType.DMA((n,))],
        name="allgather_weights",
    )(*shards)


def _scatter_to_chips(grads):
    n = len(grads)

    def body(*refs):
        ins, outs = refs[:n], refs[n:2 * n]
        send, recv, loc = refs[2 * n:]
        c = lax.axis_index("c")
        me = 2 * lax.axis_index("x") + lax.axis_index("y")
        copies = []
        for wi in range(n):
            own = pltpu.make_async_copy(ins[wi].at[me], outs[wi].at[3], loc.at[wi])
            own.start()
            copies.append(own)
            for k, (tx, ty) in enumerate(_other_chips()):
                cp = pltpu.make_async_remote_copy(
                    src_ref=ins[wi].at[2 * tx + ty], dst_ref=outs[wi].at[k], send_sem=send.at[wi, k],
                    recv_sem=recv.at[wi, k], device_id=(tx, ty, c), device_id_type=MESH)
                cp.start()
                copies.append(cp)
        for cp in copies:
            cp.wait()

    return pl.pallas_call(
        body, out_shape=[jax.ShapeDtypeStruct(s.shape, s.dtype) for s in grads],
        in_specs=[HBM_SPEC] * n, out_specs=[HBM_SPEC] * n,
        scratch_shapes=[pltpu.SemaphoreType.DMA((n, 3)), pltpu.SemaphoreType.DMA((n, 3)),
                        pltpu.SemaphoreType.DMA((n,))],
        name="scatter_grads",
    )(*grads)


def _swap_with_sibling(parts):
    n = len(parts)

    def body(*refs):
        ins, outs = refs[:n], refs[n:2 * n]
        send, recv = refs[2 * n:]
        sib = (lax.axis_index("x"), lax.axis_index("y"), 1 - lax.axis_index("c"))
        copies = []
        for wi in range(n):
            cp = pltpu.make_async_remote_copy(
                src_ref=ins[wi], dst_ref=outs[wi], send_sem=send.at[wi], recv_sem=recv.at[wi],
                device_id=sib, device_id_type=MESH)
            cp.start()
            copies.append(cp)
        for cp in copies:
            cp.wait()

    return pl.pallas_call(
        body, out_shape=[jax.ShapeDtypeStruct(s.shape, s.dtype) for s in parts],
        in_specs=[HBM_SPEC] * n, out_specs=[HBM_SPEC] * n,
        scratch_shapes=[pltpu.SemaphoreType.DMA((n,)), pltpu.SemaphoreType.DMA((n,))],
        name="swap_partial_grads",
    )(*parts)


def _allreduce_small(vec, loss_row):
    rows = vec.shape[0]

    def body(v_ref, o_ref, buf, send, recv):
        x, y, c = lax.axis_index("x"), lax.axis_index("y"), lax.axis_index("c")
        me = 4 * x + 2 * y + c
        buf[me] = v_ref[...]
        copies = []
        for r in range(1, N_DEV):
            peer = (x ^ ((r >> 2) & 1), y ^ ((r >> 1) & 1), c ^ (r & 1))
            cp = pltpu.make_async_remote_copy(
                src_ref=v_ref, dst_ref=buf.at[me], send_sem=send.at[r - 1], recv_sem=recv.at[r - 1],
                device_id=peer, device_id_type=MESH)
            cp.start()
            copies.append(cp)
        for cp in copies:
            cp.wait()
        total = buf[0]
        for d in range(1, N_DEV):
            total = total + buf[d]
        o_ref[...] = total
        o_ref[loss_row:loss_row + 1, :] = jnp.broadcast_to(
            jnp.sum(total[loss_row:loss_row + 1, :], axis=-1, keepdims=True), (1, HEAD_DIM))

    return pl.pallas_call(
        body, out_shape=jax.ShapeDtypeStruct(vec.shape, F32),
        in_specs=[pl.BlockSpec(memory_space=pltpu.VMEM)], out_specs=pl.BlockSpec(memory_space=pltpu.VMEM),
        scratch_shapes=[pltpu.VMEM((N_DEV, rows, HEAD_DIM), F32),
                        pltpu.SemaphoreType.DMA((N_DEV - 1,)), pltpu.SemaphoreType.DMA((N_DEV - 1,))],
        name="allreduce_gains",
    )(vec)


def _sum_slots(r, *, name):
    _, rows, width = r.shape
    blk = _pick(rows, (256, 128, 64, 32, 16, 8))

    def body(r_ref, o_ref):
        o_ref[...] = ((r_ref[3].astype(F32) + r_ref[0].astype(F32)) + r_ref[1].astype(F32)) + r_ref[2].astype(F32)

    return pl.pallas_call(
        body, out_shape=jax.ShapeDtypeStruct((rows, width), F32), grid=(rows // blk,),
        in_specs=[pl.BlockSpec((N_CHIPS, blk, width), lambda i: (0, i, 0))],
        out_specs=pl.BlockSpec((blk, width), lambda i: (i, 0)),
        name=name, compiler_params=_params("parallel"),
    )(r)


def _adamw(wgt, grads, m, v, *, name):
    rows, width = wgt.shape
    blk = _pick(rows, (256, 128, 64, 32, 16, 8))
    n_g = len(grads)

    def body(*refs):
        w_ref, m_ref, v_ref = refs[0], refs[1 + n_g], refs[2 + n_g]
        g_out, d_out, m_out, v_out = refs[3 + n_g:]
        grad = refs[1][...]
        for t in range(1, n_g):
            grad = grad + refs[1 + t][...]
        m_new = ADAM_B1 * m_ref[...] + (1.0 - ADAM_B1) * grad
        v_new = ADAM_B2 * v_ref[...] + (1.0 - ADAM_B2) * (grad * grad)
        m_hat = m_new / (1.0 - ADAM_B1 ** ADAM_STEP)
        v_hat = v_new / (1.0 - ADAM_B2 ** ADAM_STEP)
        g_out[...] = grad
        d_out[...] = -ADAM_LR * (m_hat / (jnp.sqrt(v_hat) + ADAM_EPS) + ADAM_WD * w_ref[...])
        m_out[...] = m_new
        v_out[...] = v_new

    spec = pl.BlockSpec((blk, width), lambda i: (i, 0))
    out = jax.ShapeDtypeStruct((rows, width), F32)
    return pl.pallas_call(
        body, out_shape=[out] * 4, grid=(rows // blk,), in_specs=[spec] * (3 + n_g),
        out_specs=[spec] * 4, name=name, compiler_params=_params("parallel"),
    )(wgt, *grads, m, v)


_SMALL = (("a_norm", 2048), ("kv_norm", 2048), ("g_ckv", 512), ("g_k_nope", 128), ("g_k_rope", 64),
          ("b_norm", 2048), ("b_g_q_lat", 512), ("b_g_q_nope", 128), ("b_g_q_rope", 64),
          ("mem_norm", 4096), ("g_mem_q", 256), ("g_mem_k", 256))


def _lanes(n):
    return -(-n // HEAD_DIM) * HEAD_DIM


def _pack_rows(pieces, pad_rows_to=8):
    flat = jnp.concatenate(pieces, axis=1)
    rows = flat.shape[1] // HEAD_DIM
    pad = (-rows) % pad_rows_to
    if pad:
        flat = jnp.concatenate([flat, jnp.zeros((1, pad * HEAD_DIM), F32)], axis=1)
    return flat.reshape(rows + pad, HEAD_DIM)


def _pad_lanes(a):
    a = a.reshape(1, -1)
    pad = _lanes(a.shape[1]) - a.shape[1]
    if pad:
        a = jnp.concatenate([a, jnp.zeros((1, pad), F32)], axis=1)
    return a


def kernel(x, mem, positions, a_norm, a_w_in, a_w_out, kv_norm, w_dkv, g_ckv, w_ukv, g_k_nope, g_k_rope, b_norm, b_w_in, b_g_q_lat, b_w_uq, b_g_q_nope, b_g_q_rope, b_w_out, mem_norm, w_mem_kv, g_mem_q, g_mem_k, loss_target, m_a_norm, m_a_w_in, m_a_w_out, m_kv_norm, m_w_dkv, m_g_ckv, m_w_ukv, m_g_k_nope, m_g_k_rope, m_b_norm, m_b_w_in, m_b_g_q_lat, m_b_w_uq, m_b_g_q_nope, m_b_g_q_rope, m_b_w_out, m_mem_norm, m_w_mem_kv, m_g_mem_q, m_g_mem_k, v_a_norm, v_a_w_in, v_a_w_out, v_kv_norm, v_w_dkv, v_g_ckv, v_w_ukv, v_g_k_nope, v_g_k_rope, v_b_norm, v_b_w_in, v_b_g_q_lat, v_b_w_uq, v_b_g_q_nope, v_b_g_q_rope, v_b_w_out, v_mem_norm, v_w_mem_kv, v_g_mem_q, v_g_mem_k):
    chip = 2 * lax.axis_index("x") + lax.axis_index("y")
    rows_dkv = D_MODEL // N_CHIPS
    heads_per_chip = N_MLA_HEADS // N_CHIPS
    qk_w = HEAD_DIM + ROPE_DIM

    big = {"a_in": a_w_in[0], "a_out": a_w_out[0], "dkv": w_dkv, "ukv": w_ukv, "b_in": b_w_in[0],
           "uq": b_w_uq[0], "b_out": b_w_out[0], "mem_kv": w_mem_kv.reshape(2 * rows_dkv, 2 * MEM_W)}
    big_m = {"a_in": m_a_w_in[0], "a_out": m_a_w_out[0], "dkv": m_w_dkv, "ukv": m_w_ukv, "b_in": m_b_w_in[0],
             "uq": m_b_w_uq[0], "b_out": m_b_w_out[0], "mem_kv": m_w_mem_kv.reshape(2 * rows_dkv, 2 * MEM_W)}
    big_v = {"a_in": v_a_w_in[0], "a_out": v_a_w_out[0], "dkv": v_w_dkv, "ukv": v_w_ukv, "b_in": v_b_w_in[0],
             "uq": v_b_w_uq[0], "b_out": v_b_w_out[0], "mem_kv": v_w_mem_kv.reshape(2 * rows_dkv, 2 * MEM_W)}
    names = list(big)
    gathered = _allgather_chips([big[n].astype(BF16) for n in names] + [a_norm])
    st = dict(zip(names, gathered[:-1]))
    a_in_full = st["a_in"].transpose(1, 0, 2).reshape(D_MODEL, QKV_W + GATE_W)
    uq = st["uq"].reshape(N_CHIPS, Q_LORA, heads_per_chip, qk_w)
    uq = jnp.pad(uq, ((0, 0), (0, 0), (0, 0), (0, CAT_W - qk_w)))
    w = {
        "a_in": a_in_full,
        "a_in_qkv": a_in_full[:, :QKV_W],
        "a_in_gate": a_in_full[:, QKV_W:],
        "a_out": st["a_out"].reshape(D_MODEL, D_MODEL),
        "dkv": jnp.pad(st["dkv"].reshape(D_MODEL, KV_LORA + ROPE_DIM), ((0, 0), (0, HEAD_DIM - ROPE_DIM))),
        "ukv": st["ukv"].transpose(1, 0, 2).reshape(KV_LORA, N_MLA_HEADS * CAT_W),
        "b_in": st["b_in"].transpose(1, 0, 2).reshape(D_MODEL, Q_LORA + GATE_W),
        "uq": uq.transpose(1, 0, 2, 3).reshape(Q_LORA, N_MLA_HEADS * CAT_W),
        "b_out": st["b_out"].reshape(D_MODEL, D_MODEL),
        "mem_kv": st["mem_kv"].reshape(N_CHIPS, 2, rows_dkv, 2 * MEM_W).transpose(1, 0, 2, 3).reshape(
            2, D_MODEL, 2 * MEM_W),
    }
    gains = {
        "a_norm": gathered[-1].reshape(1, D_MODEL), "kv_norm": kv_norm.reshape(1, -1),
        "g_ckv": g_ckv.reshape(1, -1), "g_k_nope": g_k_nope.reshape(1, -1), "g_k_rope": g_k_rope.reshape(1, -1),
        "b_norm": b_norm, "b_g_q_lat": b_g_q_lat, "b_g_q_nope": b_g_q_nope, "b_g_q_rope": b_g_q_rope,
        "mem_norm": mem_norm, "g_mem_q": g_mem_q, "g_mem_k": g_mem_k,
    }

    loss_part, grad_x, gw, gg = _local_step(x[0], mem[0], positions[0], loss_target[0], w, gains)

    stacked = {
        "a_in": gw["a_in"],
        "a_out": gw["a_out"].reshape(N_CHIPS, rows_dkv, D_MODEL),
        "dkv": gw["dkv"][:, :KV_LORA + ROPE_DIM].reshape(N_CHIPS, rows_dkv, KV_LORA + ROPE_DIM),
        "ukv": gw["ukv"],
        "b_in": gw["b_in"],
        "uq": gw["uq"].reshape(N_CHIPS, Q_LORA, heads_per_chip, CAT_W)[..., :qk_w].reshape(
            N_CHIPS, Q_LORA, heads_per_chip * qk_w),
        "b_out": gw["b_out"].reshape(N_CHIPS, rows_dkv, D_MODEL),
        "mem_kv": jnp.stack([gw["mem_kv"][0].reshape(N_CHIPS, rows_dkv, 2 * MEM_W),
                             gw["mem_kv"][1].reshape(N_CHIPS, rows_dkv, 2 * MEM_W)], axis=1).reshape(
            N_CHIPS, 2 * rows_dkv, 2 * MEM_W),
    }
    received = _scatter_to_chips([stacked[n] for n in names])
    partial = [_sum_slots(r, name=f"sum_slots_{n}") for n, r in zip(names, received)]
    sibling = _swap_with_sibling(partial)
    big_out = {}
    for n, mine, theirs in zip(names, partial, sibling):
        big_out[n] = _adamw(big[n], [mine, theirs], big_m[n], big_v[n], name=f"adamw_{n}")

    pieces = [_pad_lanes(gg[n]) if n not in ("g_k_rope", "b_g_q_rope") else gg[n] for n, _ in _SMALL]
    pieces.append(loss_part)
    loss_row = sum(_lanes(size) for _, size in _SMALL) // HEAD_DIM
    summed = _allreduce_small(_pack_rows(pieces), loss_row)
    flat = summed.reshape(1, -1)
    small_g, off = {}, 0
    for n, size in _SMALL:
        small_g[n] = flat[:, off:off + size]
        off += _lanes(size)
    loss = flat[0, off]
    small_g["a_norm"] = lax.dynamic_slice(small_g["a_norm"], (0, chip * rows_dkv), (1, rows_dkv))

    small_w = {"a_norm": a_norm, "kv_norm": kv_norm, "g_ckv": g_ckv, "g_k_nope": g_k_nope, "g_k_rope": g_k_rope,
               "b_norm": b_norm, "b_g_q_lat": b_g_q_lat, "b_g_q_nope": b_g_q_nope, "b_g_q_rope": b_g_q_rope,
               "mem_norm": mem_norm, "g_mem_q": g_mem_q, "g_mem_k": g_mem_k}
    small_m = {"a_norm": m_a_norm, "kv_norm": m_kv_norm, "g_ckv": m_g_ckv, "g_k_nope": m_g_k_nope,
               "g_k_rope": m_g_k_rope, "b_norm": m_b_norm, "b_g_q_lat": m_b_g_q_lat, "b_g_q_nope": m_b_g_q_nope,
               "b_g_q_rope": m_b_g_q_rope, "mem_norm": m_mem_norm, "g_mem_q": m_g_mem_q, "g_mem_k": m_g_mem_k}
    small_v = {"a_norm": v_a_norm, "kv_norm": v_kv_norm, "g_ckv": v_g_ckv, "g_k_nope": v_g_k_nope,
               "g_k_rope": v_g_k_rope, "b_norm": v_b_norm, "b_g_q_lat": v_b_g_q_lat, "b_g_q_nope": v_b_g_q_nope,
               "b_g_q_rope": v_b_g_q_rope, "mem_norm": v_mem_norm, "g_mem_q": v_g_mem_q, "g_mem_k": v_g_mem_k}
    snames = [n for n, _ in _SMALL]
    packs = [_pack_rows([_pad_lanes(src[n]) for n in snames])
             for src in (small_w, small_g, small_m, small_v)]
    small_res = _adamw(packs[0], [packs[1]], packs[2], packs[3], name="adamw_gains")
    small_out = {n: [] for n in snames}
    for res in small_res:
        flat_r = res.reshape(1, -1)
        off = 0
        for n in snames:
            size = small_w[n].size
            small_out[n].append(flat_r[:, off:off + size].reshape(small_w[n].shape))
            off += _lanes(size)

    big_names = {"a_w_in": ("a_in", a_w_in), "a_w_out": ("a_out", a_w_out), "w_dkv": ("dkv", w_dkv),
                 "w_ukv": ("ukv", w_ukv), "b_w_in": ("b_in", b_w_in), "b_w_uq": ("uq", b_w_uq),
                 "b_w_out": ("b_out", b_w_out), "w_mem_kv": ("mem_kv", w_mem_kv)}
    order = ["a_norm", "a_w_in", "a_w_out", "kv_norm", "w_dkv", "g_ckv", "w_ukv", "g_k_nope", "g_k_rope",
             "b_norm", "b_w_in", "b_g_q_lat", "b_w_uq", "b_g_q_nope", "b_g_q_rope", "b_w_out", "mem_norm",
             "w_mem_kv", "g_mem_q", "g_mem_k"]
    groups = [[], [], [], []]
    for n in order:
        if n in big_names:
            key, ref_arr = big_names[n]
            for t in range(4):
                groups[t].append(big_out[key][t].reshape(ref_arr.shape))
        else:
            for t in range(4):
                groups[t].append(small_out[n][t])
    return (loss, grad_x[None], *groups[0], *groups[1], *groups[2], *groups[3])
```

```python
import functools

import jax
import jax.numpy as jnp
from jax import lax
from jax.experimental import pallas as pl
from jax.experimental.pallas import tpu as pltpu

F32 = jnp.float32
BF16 = jnp.bfloat16
MESH = pl.DeviceIdType.MESH

D_MODEL = 2048
HEAD_DIM = 128
N_SB_HEADS = 12
N_MEM_HEADS = 4
N_MLA_HEADS = 12
MEM_LEN = 256
Q_LORA = 512
KV_LORA = 512
ROPE_DIM = 64
SB_W = N_SB_HEADS * HEAD_DIM
MEM_W = N_MEM_HEADS * HEAD_DIM
MLA_W = N_MLA_HEADS * HEAD_DIM
QKV_W = 3 * SB_W
GATE_W = SB_W + 2 * MEM_W
CAT_W = 2 * HEAD_DIM
ROPE_THETA = 10000.0
EPS = 1e-6
N_CHIPS = 4
N_DEV = 8

ADAM_LR = 0.001
ADAM_B1 = 0.9
ADAM_B2 = 0.999
ADAM_EPS = 1e-08
ADAM_WD = 0.01
ADAM_STEP = 10

VMEM_LIMIT_BYTES = 56 * 1024 * 1024
MM_OPERAND_VMEM_BYTES = 24 * 1024 * 1024
ROW_BLOCK = 256
ATT_BLOCK = 256


def _params(*sem):
    return pltpu.CompilerParams(dimension_semantics=sem, vmem_limit_bytes=VMEM_LIMIT_BYTES)


def _pick(n, cands):
    for c in cands:
        if n % c == 0:
            return c
    return n


def _mm(a, b, *, name, ta=False, tb=False, out_dtype=F32, res=None, n_split=1, scale_cols=None):
    if ta:
        k_dim, m_dim = a.shape
    else:
        m_dim, k_dim = a.shape
    if tb:
        n_dim, kb = b.shape
    else:
        kb, n_dim = b.shape
    assert kb == k_dim, (a.shape, b.shape)
    n_per = n_dim // n_split
    bm = m_dim if m_dim <= 1024 else _pick(m_dim, (1024, 512, 256))
    bn = n_per if n_per <= 1024 else _pick(n_per, (1024, 896, 768, 640, 512, 256, 128))
    per_k = (bm * a.dtype.itemsize + bn * b.dtype.itemsize) * 2
    bk = next((c for c in (k_dim, 2048, 1024, 512, 256, 128)
               if c <= k_dim and k_dim % c == 0 and c * per_k <= MM_OPERAND_VMEM_BYTES), 128)
    nk = k_dim // bk
    nb_per = n_per // bn
    grid = (m_dim // bm, n_dim // bn, nk)
    a_spec = (pl.BlockSpec((bk, bm), lambda i, j, k: (k, i)) if ta
              else pl.BlockSpec((bm, bk), lambda i, j, k: (i, k)))
    b_spec = (pl.BlockSpec((bn, bk), lambda i, j, k: (j, k)) if tb
              else pl.BlockSpec((bk, bn), lambda i, j, k: (k, j)))
    dims = (((0 if ta else 1,), (1 if tb else 0,)), ((), ()))
    in_specs = [a_spec, b_spec]
    args = [a, b]
    if res is not None:
        in_specs.append(pl.BlockSpec((bm, bn), lambda i, j, k: (i, j)))
        args.append(res)
    if n_split == 1:
        out_shape = jax.ShapeDtypeStruct((m_dim, n_dim), out_dtype)
        out_spec = pl.BlockSpec((bm, bn), lambda i, j, k: (i, j))
    else:
        out_shape = jax.ShapeDtypeStruct((n_split, m_dim, n_per), out_dtype)
        out_spec = pl.BlockSpec((None, bm, bn), lambda i, j, k: (j // nb_per, i, j % nb_per))

    def body(*refs):
        if res is None:
            a_ref, b_ref, o_ref, acc = refs
            r_ref = None
        else:
            a_ref, b_ref, r_ref, o_ref, acc = refs
        k = pl.program_id(2)
        col_block = pl.program_id(1)

        @pl.when(k == 0)
        def _():
            acc[...] = jnp.zeros_like(acc)

        acc[...] += lax.dot_general(a_ref[...].astype(BF16), b_ref[...].astype(BF16), dims,
                                    preferred_element_type=F32)

        @pl.when(k == nk - 1)
        def _():
            r = acc[...]
            if r_ref is not None:
                r = r + r_ref[...]
            if scale_cols is not None:
                assert scale_cols[0] % bn == 0
                r = r * jnp.where(col_block < scale_cols[0] // bn, scale_cols[1], 1.0)
            o_ref[...] = r.astype(out_dtype)

    return pl.pallas_call(
        body, out_shape=out_shape, grid=grid, in_specs=in_specs, out_specs=out_spec,
        scratch_shapes=[pltpu.VMEM((bm, bn), F32)], name=name,
        compiler_params=_params("parallel", "parallel", "arbitrary"),
    )(*args)


def _rowwise(body, n_rows, ins, outs, accs=(), *, name, block=ROW_BLOCK):
    blk = min(block, n_rows)
    assert n_rows % blk == 0
    in_specs = []
    for arr, is_row in ins:
        if is_row:
            assert arr.shape[0] == n_rows, (name, arr.shape, n_rows)
            in_specs.append(pl.BlockSpec((blk, arr.shape[1]), lambda i: (i, 0)))
        else:
            in_specs.append(pl.BlockSpec(arr.shape, lambda i, nd=arr.ndim: (0,) * nd))
    out_shape = [jax.ShapeDtypeStruct((n_rows, w), dt) for w, dt in outs]
    out_specs = [pl.BlockSpec((blk, w), lambda i: (i, 0)) for w, _ in outs]
    out_shape += [jax.ShapeDtypeStruct(s, dt) for s, dt in accs]
    out_specs += [pl.BlockSpec(s, lambda i, nd=len(s): (0,) * nd) for s, _ in accs]
    n_in, n_out, n_acc = len(ins), len(outs), len(accs)

    def kern(*refs):
        in_refs = refs[:n_in]
        out_refs = refs[n_in:n_in + n_out]
        acc_refs = refs[n_in + n_out:]
        if n_acc:
            @pl.when(pl.program_id(0) == 0)
            def _():
                for r in acc_refs:
                    r[...] = jnp.zeros_like(r)
        body(in_refs, out_refs, acc_refs)

    return pl.pallas_call(
        kern, out_shape=out_shape, grid=(n_rows // blk,), in_specs=in_specs, out_specs=out_specs,
        name=name, compiler_params=_params("arbitrary"),
    )(*[arr for arr, _ in ins])


def _rms(x, g, n=None):
    n = x.shape[-1] if n is None else n
    r = lax.rsqrt(jnp.sum(x * x, axis=-1, keepdims=True) / n + EPS)
    return x * r * g


def _rms_bwd(x, g, dy, n=None):
    n = x.shape[-1] if n is None else n
    r = lax.rsqrt(jnp.sum(x * x, axis=-1, keepdims=True) / n + EPS)
    gdy = dy * g
    dx = r * (gdy - x * ((r * r) * (jnp.sum(gdy * x, axis=-1, keepdims=True) / n)))
    dg = jnp.sum(dy * x * r, axis=0, keepdims=True)
    return dx, dg


def _swap_halves(x):
    lane = lax.broadcasted_iota(jnp.int32, x.shape, 1)
    return jnp.where(lane < ROPE_DIM // 2, pltpu.roll(x, 128 - ROPE_DIM // 2, 1),
                     pltpu.roll(x, ROPE_DIM // 2, 1))


def _rope(n, cos_t, sin_t):
    return n * cos_t + _swap_halves(n) * sin_t


def _rope_bwd(dy, cos_t, sin_t):
    return dy * cos_t - _swap_halves(dy) * sin_t


def _sigmoid(g):
    return 1.0 / (1.0 + jnp.exp(-g))


def _dot_t(a, b):
    return lax.dot_general(a, b, (((1,), (1,)), ((), ())), preferred_element_type=F32)


def _tdot(a, b):
    return lax.dot_general(a, b, (((0,), (0,)), ((), ())), preferred_element_type=F32)


def _dot(a, b):
    return jnp.dot(a, b, preferred_element_type=F32)


def _hs(h, w=HEAD_DIM, base=0):
    return slice(base + h * w, base + (h + 1) * w)


def _mem_head(qm, gq, mk_h, mv_h):
    qb = _rms(qm, gq).astype(BF16)
    s = _dot_t(qb, mk_h) * (HEAD_DIM ** -0.5)
    e = jnp.exp(s - jnp.max(s, axis=-1, keepdims=True))
    p = e / jnp.sum(e, axis=-1, keepdims=True)
    mo = _dot(p.astype(BF16), mv_h)
    return qb, p, mo


def _mix_fwd(att, gates, c0, mk, mv, gq, *, name):
    n_rows = att.shape[0]

    def body(ins, outs, _):
        att_ref, g_ref, mk_ref, mv_ref, gq_ref = ins
        (o_ref,) = outs
        g = g_ref[:, c0:c0 + SB_W]
        o_ref[:, :SB_W] = (att_ref[...] * (g * _sigmoid(g))).astype(BF16)
        for h in range(N_MEM_HEADS):
            qm = g_ref[:, _hs(h, base=c0 + SB_W)]
            gm = g_ref[:, _hs(h, base=c0 + SB_W + MEM_W)]
            _, _, mo = _mem_head(qm, gq_ref[...], mk_ref[:, _hs(h)], mv_ref[:, _hs(h)])
            o_ref[:, _hs(h, base=SB_W)] = (mo * (gm * _sigmoid(gm))).astype(BF16)

    (mixed,) = _rowwise(body, n_rows,
                        [(att, True), (gates, True), (mk, False), (mv, False), (gq, False)],
                        [(D_MODEL, BF16)], name=name)
    return mixed


def _mix_bwd(dmixed, att, gates, c0, mk, mv, gq, *, name):
    n_rows = att.shape[0]
    scale = HEAD_DIM ** -0.5

    def body(ins, outs, accs):
        dm_ref, att_ref, g_ref, mk_ref, mv_ref, gq_ref = ins
        datt_ref, dg_ref = outs
        dmk_ref, dmv_ref, dgq_ref = accs
        g = g_ref[:, c0:c0 + SB_W]
        sg = _sigmoid(g)
        dm = dm_ref[:, :SB_W]
        datt_ref[...] = dm * (g * sg)
        dg_ref[:, :SB_W] = (dm * att_ref[...] * (sg * (1.0 + g * (1.0 - sg)))).astype(BF16)
        for h in range(N_MEM_HEADS):
            qm = g_ref[:, _hs(h, base=c0 + SB_W)]
            gm = g_ref[:, _hs(h, base=c0 + SB_W + MEM_W)]
            mk_h = mk_ref[:, _hs(h)]
            mv_h = mv_ref[:, _hs(h)]
            qb, p, mo = _mem_head(qm, gq_ref[...], mk_h, mv_h)
            sgm = _sigmoid(gm)
            dmh = dm_ref[:, _hs(h, base=SB_W)]
            dmo = dmh * (gm * sgm)
            dg_ref[:, _hs(h, base=SB_W + MEM_W)] = (
                dmh * mo * (sgm * (1.0 + gm * (1.0 - sgm)))).astype(BF16)
            dmo_b = dmo.astype(BF16)
            pb = p.astype(BF16)
            dp = _dot_t(dmo_b, mv_h)
            dmv_ref[:, _hs(h)] += _tdot(pb, dmo_b)
            ds = (p * (dp - jnp.sum(dp * p, axis=-1, keepdims=True)) * scale).astype(BF16)
            dqn = _dot(ds, mk_h)
            dmk_ref[:, _hs(h)] += _tdot(ds, qb)
            dqm, dgq = _rms_bwd(qm, gq_ref[...], dqn)
            dg_ref[:, _hs(h, base=SB_W)] = dqm.astype(BF16)
            dgq_ref[...] += dgq

    return _rowwise(body, n_rows,
                    [(dmixed, True), (att, True), (gates, True), (mk, False), (mv, False), (gq, False)],
                    [(SB_W, F32), (GATE_W, BF16)],
                    [((MEM_LEN, MEM_W), F32), ((MEM_LEN, MEM_W), F32), ((1, HEAD_DIM), F32)],
                    name=name)


def _mem_side_fwd(mem, g_norm, w_kv, g_k, *, tag):
    def norm_body(ins, outs, _):
        outs[0][...] = _rms(ins[0][...], ins[1][...]).astype(BF16)

    (mn,) = _rowwise(norm_body, MEM_LEN, [(mem, True), (g_norm, False)], [(D_MODEL, BF16)],
                     name=f"mem_norm_{tag}")
    mkv = _mm(mn, w_kv, name=f"mem_kv_{tag}")

    def kv_body(ins, outs, _):
        mkv_ref, gk_ref = ins
        mk_ref, mv_ref = outs
        for h in range(N_MEM_HEADS):
            mk_ref[:, _hs(h)] = _rms(mkv_ref[:, _hs(h)], gk_ref[...]).astype(BF16)
        mv_ref[...] = mkv_ref[:, MEM_W:].astype(BF16)

    mk, mv = _rowwise(kv_body, MEM_LEN, [(mkv, True), (g_k, False)], [(MEM_W, BF16), (MEM_W, BF16)],
                      name=f"mem_kv_prep_{tag}")
    return mn, mkv, mk, mv


def _mem_side_bwd(mem, g_norm, w_kv, g_k, mn, mkv, dmk, dmv, *, tag):
    def kv_body(ins, outs, accs):
        mkv_ref, gk_ref, dmk_ref, dmv_ref = ins
        (d_ref,) = outs
        (dgk_ref,) = accs
        for h in range(N_MEM_HEADS):
            dx, dg = _rms_bwd(mkv_ref[:, _hs(h)], gk_ref[...], dmk_ref[:, _hs(h)])
            d_ref[:, _hs(h)] = dx.astype(BF16)
            dgk_ref[...] += dg
        d_ref[:, MEM_W:] = dmv_ref[...].astype(BF16)

    dmkv, dgk = _rowwise(kv_body, MEM_LEN, [(mkv, True), (g_k, False), (dmk, True), (dmv, True)],
                         [(2 * MEM_W, BF16)], [((1, HEAD_DIM), F32)], name=f"mem_kv_prep_bwd_{tag}")
    dmn = _mm(dmkv, w_kv, tb=True, name=f"mem_kv_dx_{tag}")
    dw = _mm(mn, dmkv, ta=True, out_dtype=BF16, name=f"mem_kv_dw_{tag}")

    def norm_body(ins, outs, accs):
        _, dg = _rms_bwd(ins[0][...], ins[1][...], ins[2][...])
        accs[0][...] += dg

    (dgn,) = _rowwise(norm_body, MEM_LEN, [(mem, True), (g_norm, False), (dmn, True)], [],
                      [((1, D_MODEL), F32)], name=f"mem_norm_bwd_{tag}")
    return dw, dgn, dgk


LOG2_E = 1.4426950408889634
SB_Q_SCALE = HEAD_DIM ** -0.5 * LOG2_E


Z2_CAP = 126.0


def _sb_terms(z2):
    zc = jnp.minimum(z2, Z2_CAP)
    w = 1.0 + jnp.exp2(zc)
    return zc, w, jnp.log2(w)


def _chain_modes(s, qb):
    return tuple(None if t < s else ("m" if t == s else "f") for t in range(qb))


def _split_dot(x, tri2):
    hi = x.astype(BF16)
    lo = (x - hi.astype(F32)).astype(BF16)
    return _dot(jnp.concatenate([hi, lo], axis=1), tri2)


def _sb_fwd(qkv, *, name, hp=2):
    seq = qkv.shape[0]
    blk = min(ATT_BLOCK, seq)
    nkb = seq // blk
    qb = _pick(nkb, (2, 1))
    rows = qb * blk
    chains =[(t, s) for t in range(hp) for s in range(qb)]

    def body(q_ref, k_ref, v_ref, o_ref):
        base = pl.program_id(1) * qb
        qs = {(t, s): q_ref[s * blk:(s + 1) * blk, _hs(t)] for t, s in chains}
        row = lax.broadcasted_iota(jnp.int32, (blk, blk), 0)
        col = lax.broadcasted_iota(jnp.int32, (blk, blk), 1)
        after = (row > col).astype(BF16)
        after2 = jnp.concatenate([after, after], axis=0)
        causal = col < row

        def step(j, carry, modes):
            off = pl.multiple_of(j * blk, blk)
            act = [c for c in chains if modes[c[1]]]
            zs, ls = {}, {}
            for c in act:
                zs[c], _, l = _sb_terms(_dot_t(qs[c], k_ref[pl.ds(off, blk), _hs(c[0])]))
                ls[c] = jnp.where(causal, l, 0.0) if modes[c[1]] == "m" else l
            cs = {c: _split_dot(ls[c], after2) for c in act}
            carry = dict(carry)
            for c in act:
                run, acc = carry[c]
                a = jnp.exp2(zs[c] - ls[c] - cs[c] - run)
                if modes[c[1]] == "m":
                    a = jnp.where(causal, a, 0.0)
                acc = acc + _dot(a.astype(BF16), v_ref[pl.ds(off, blk), _hs(c[0])])
                carry[c] = (run + jnp.sum(ls[c], axis=-1, keepdims=True), acc)
            return carry

        init = (jnp.zeros((blk, 1), F32), jnp.zeros((blk, HEAD_DIM), F32))
        carry = {c: init for c in chains}
        for s in reversed(range(qb)):
            carry = step(base + s, carry, _chain_modes(s, qb))
        carry = lax.fori_loop(0, base, lambda jj, c: step(base - 1 - jj, c, ("f",) * qb), carry)
        for t, s in chains:
            o_ref[s * blk:(s + 1) * blk, _hs(t)] = carry[(t, s)][1]

    nh = N_SB_HEADS // hp
    return pl.pallas_call(
        body, out_shape=jax.ShapeDtypeStruct((seq, SB_W), F32), grid=(nh, nkb // qb),
        in_specs=[pl.BlockSpec((rows, hp * HEAD_DIM), lambda h, i: (i, h)),
                  pl.BlockSpec((seq, hp * HEAD_DIM), lambda h, i: (0, nh + h)),
                  pl.BlockSpec((seq, hp * HEAD_DIM), lambda h, i: (0, 2 * nh + h))],
        out_specs=pl.BlockSpec((rows, hp * HEAD_DIM), lambda h, i: (i, h)),
        name=name, compiler_params=_params("parallel", "arbitrary"),
    )(qkv, qkv, qkv)


SB_BWD_GROUP = 4


def _sb_bwd(qkv, out, dout, *, name):
    seq = qkv.shape[0]
    blk = min(ATT_BLOCK, seq)
    nkb = seq // blk
    qb = _pick(nkb, (4, 2, 1))
    rows = qb * blk
    scale = HEAD_DIM ** -0.5

    def body(q_ref, k_ref, v_ref, do_ref, o_ref, dq_ref, dk_ref, dv_ref):
        g = pl.program_id(1)
        base = g * qb

        @pl.when(g == 0)
        def _():
            dk_ref[...] = jnp.zeros_like(dk_ref)
            dv_ref[...] = jnp.zeros_like(dv_ref)

        qs = [q_ref[t * blk:(t + 1) * blk, :] for t in range(qb)]
        dos = [do_ref[t * blk:(t + 1) * blk, :].astype(BF16) for t in range(qb)]
        totals = [jnp.sum(dos[t].astype(F32) * o_ref[t * blk:(t + 1) * blk, :], axis=-1, keepdims=True)
                  for t in range(qb)]
        row = lax.broadcasted_iota(jnp.int32, (blk, blk), 0)
        col = lax.broadcasted_iota(jnp.int32, (blk, blk), 1)
        after = (row > col).astype(BF16)
        after2 = jnp.concatenate([after, after], axis=0)
        from_s = (row >= col).astype(BF16)
        from_s2 = jnp.concatenate([from_s, from_s], axis=0)
        causal = col < row

        def step(j, carry, modes):
            runs, rights, dqs = list(carry[0]), list(carry[1]), list(carry[2])
            off = pl.multiple_of(j * blk, blk)
            kb = k_ref[pl.ds(off, blk), :]
            vb = v_ref[pl.ds(off, blk), :]
            dv_inc = dk_inc = None
            for first in range(0, qb, SB_BWD_GROUP):
                act = [t for t in range(first, min(first + SB_BWD_GROUP, qb)) if modes[t]]
                das = {t: _dot_t(dos[t], vb) for t in act}
                zs, ls, sns = {}, {}, {}
                for t in act:
                    zs[t], w, l = _sb_terms(_dot_t(qs[t], kb))
                    sns[t] = pl.reciprocal(w, approx=True)
                    ls[t] = jnp.where(causal, l, 0.0) if modes[t] == "m" else l
                cs = {t: _split_dot(ls[t], after2) for t in act}
                abs_, des = {}, {}
                for t in act:
                    a = jnp.exp2(zs[t] - ls[t] - cs[t] - runs[t])
                    if modes[t] == "m":
                        a = jnp.where(causal, a, 0.0)
                    abs_[t] = a.astype(BF16)
                    des[t] = abs_[t].astype(F32) * das[t]
                sufs = {t: _split_dot(des[t], from_s2) for t in act}
                for t in act:
                    left = totals[t] - (sufs[t] + rights[t])
                    dz = (des[t] + left) * sns[t] - left
                    if modes[t] == "m":
                        dz = jnp.where(causal, dz, 0.0)
                    dzb = dz.astype(BF16)
                    dqs[t] = dqs[t] + _dot(dzb, kb)
                    inc_v = _tdot(abs_[t], dos[t])
                    inc_k = _tdot(dzb, qs[t])
                    dv_inc = inc_v if dv_inc is None else dv_inc + inc_v
                    dk_inc = inc_k if dk_inc is None else dk_inc + inc_k
                    runs[t] = runs[t] + jnp.sum(ls[t], axis=-1, keepdims=True)
                    rights[t] = rights[t] + jnp.sum(des[t], axis=-1, keepdims=True)
            dv_ref[pl.ds(off, blk), :] += dv_inc
            dk_ref[pl.ds(off, blk), :] += dk_inc
            return tuple(runs), tuple(rights), tuple(dqs)

        zero = (jnp.zeros((blk, 1), F32),) * qb
        carry = (zero, zero, (jnp.zeros((blk, HEAD_DIM), F32),) * qb)
        for s in reversed(range(qb)):
            carry = step(base + s, carry, _chain_modes(s, qb))
        carry = lax.fori_loop(0, base, lambda jj, c: step(base - 1 - jj, c, ("f",) * qb), carry)
        for t in range(qb):
            dq_ref[t * blk:(t + 1) * blk, :] = carry[2][t] * scale

        @pl.when(g == pl.num_programs(1) - 1)
        def _():
            dk_ref[...] = dk_ref[...] * (1.0 / LOG2_E)

    out_sd = jax.ShapeDtypeStruct((seq, SB_W), F32)
    return pl.pallas_call(
        body, out_shape=[out_sd, out_sd, out_sd], grid=(N_SB_HEADS, nkb // qb),
        in_specs=[pl.BlockSpec((rows, HEAD_DIM), lambda h, i: (i, h)),
                  pl.BlockSpec((seq, HEAD_DIM), lambda h, i: (0, N_SB_HEADS + h)),
                  pl.BlockSpec((seq, HEAD_DIM), lambda h, i: (0, 2 * N_SB_HEADS + h)),
                  pl.BlockSpec((rows, HEAD_DIM), lambda h, i: (i, h)),
                  pl.BlockSpec((rows, HEAD_DIM), lambda h, i: (i, h))],
        out_specs=[pl.BlockSpec((rows, HEAD_DIM), lambda h, i: (i, h)),
                   pl.BlockSpec((seq, HEAD_DIM), lambda h, i: (0, h)),
                   pl.BlockSpec((seq, HEAD_DIM), lambda h, i: (0, h))],
        name=name, compiler_params=_params("parallel", "arbitrary"),
    )(qkv, qkv, qkv, dout, out)


MLA_SCALE = (HEAD_DIM + ROPE_DIM) ** -0.5
MLA_Q_SCALE = MLA_SCALE * LOG2_E


def _mla_fwd(q_cat, k_cat, v, *, name, hp=2):
    seq = q_cat.shape[0]
    blk = min(ATT_BLOCK, seq)
    nkb = seq // blk
    qb = _pick(nkb, (2, 1))
    rows = qb * blk
    chains = [(t, s) for t in range(hp) for s in range(qb)]

    def body(q_ref, k_ref, v_ref, o_ref, lse_ref):
        base = pl.program_id(1) * qb
        qs = {(t, s): q_ref[s * blk:(s + 1) * blk, t * CAT_W:(t + 1) * CAT_W] for t, s in chains}
        row = lax.broadcasted_iota(jnp.int32, (blk, blk), 0)
        col = lax.broadcasted_iota(jnp.int32, (blk, blk), 1)
        causal = col <= row

        def step(j, carry, modes):
            off = pl.multiple_of(j * blk, blk)
            act = [c for c in chains if modes[c[1]]]
            ss = {c: _dot_t(qs[c], k_ref[pl.ds(off, blk), c[0] * CAT_W:(c[0] + 1) * CAT_W]) for c in act}
            carry = dict(carry)
            for c in act:
                m, l, acc = carry[c]
                s = ss[c]
                if modes[c[1]] == "m":
                    s = jnp.where(causal, s, -jnp.inf)
                m_new = jnp.maximum(m, jnp.max(s, axis=-1, keepdims=True))
                p = jnp.exp2(s - m_new)
                alpha = jnp.exp2(m - m_new)
                l = alpha * l + jnp.sum(p, axis=-1, keepdims=True)
                acc = alpha * acc + _dot(p.astype(BF16), v_ref[pl.ds(off, blk), _hs(c[0])])
                carry[c] = (m_new, l, acc)
            return carry

        init = (jnp.full((blk, 1), -jnp.inf, F32), jnp.zeros((blk, 1), F32),
                jnp.zeros((blk, HEAD_DIM), F32))
        carry = {c: init for c in chains}
        carry = lax.fori_loop(0, base, lambda j, c: step(j, c, ("f",) * qb), carry)
        for s in range(qb):
            carry = step(base + s, carry, _chain_modes(s, qb))
        for t, s in chains:
            m, l, acc = carry[(t, s)]
            o_ref[s * blk:(s + 1) * blk, _hs(t)] = acc / l
            lse_ref[s * blk:(s + 1) * blk, _hs(t)] = jnp.broadcast_to(
                (m + jnp.log2(l)) * (1.0 / LOG2_E), (blk, HEAD_DIM))

    out = jax.ShapeDtypeStruct((seq, MLA_W), F32)
    return pl.pallas_call(
        body, out_shape=[out, out], grid=(N_MLA_HEADS // hp, nkb // qb),
        in_specs=[pl.BlockSpec((rows, hp * CAT_W), lambda h, i: (i, h)),
                  pl.BlockSpec((seq, hp * CAT_W), lambda h, i: (0, h)),
                  pl.BlockSpec((seq, hp * HEAD_DIM), lambda h, i: (0, h))],
        out_specs=[pl.BlockSpec((rows, hp * HEAD_DIM), lambda h, i: (i, h)),
                   pl.BlockSpec((rows, hp * HEAD_DIM), lambda h, i: (i, h))],
        name=name, compiler_params=_params("parallel", "arbitrary"),
    )(q_cat, k_cat, v)


def _mla_bwd(q_cat, k_cat, v, out, lse, dout, *, name):
    seq = q_cat.shape[0]
    blk = min(ATT_BLOCK, seq)
    nkb = seq // blk
    qb = _pick(nkb, (4, 2, 1))
    rows = qb * blk

    def body(q_ref, k_ref, v_ref, o_ref, lse_ref, do_ref, dq_ref, dk_ref, dv_ref):
        g = pl.program_id(1)
        base = g * qb

        @pl.when(g == 0)
        def _():
            dk_ref[...] = jnp.zeros_like(dk_ref)
            dv_ref[...] = jnp.zeros_like(dv_ref)

        qs, dobs, deltas, lses = [], [], [], []
        for t in range(qb):
            rs = slice(t * blk, (t + 1) * blk)
            do = do_ref[rs, :]
            qs.append(q_ref[rs, :])
            dobs.append(do.astype(BF16))
            deltas.append(jnp.sum(do * o_ref[rs, :], axis=-1, keepdims=True))
            lses.append(lse_ref[rs, :1] * LOG2_E)
        row = lax.broadcasted_iota(jnp.int32, (blk, blk), 0)
        col = lax.broadcasted_iota(jnp.int32, (blk, blk), 1)
        causal = col <= row

        def step(j, dqs, modes):
            off = pl.multiple_of(j * blk, blk)
            kb = k_ref[pl.ds(off, blk), :]
            vb = v_ref[pl.ds(off, blk), :]
            act = [t for t in range(qb) if modes[t]]
            ss = {t: _dot_t(qs[t], kb) for t in act}
            dps = {t: _dot_t(dobs[t], vb) for t in act}
            dqs = list(dqs)
            dv_inc = dk_inc = None
            for t in act:
                p = jnp.exp2(ss[t] - lses[t])
                if modes[t] == "m":
                    p = jnp.where(causal, p, 0.0)
                ds = (p * (dps[t] - deltas[t])).astype(BF16)
                inc_v = _tdot(p.astype(BF16), dobs[t])
                inc_k = _tdot(ds, qs[t])
                dv_inc = inc_v if dv_inc is None else dv_inc + inc_v
                dk_inc = inc_k if dk_inc is None else dk_inc + inc_k
                dqs[t] = dqs[t] + _dot(ds, kb)
            dv_ref[pl.ds(off, blk), :] += dv_inc
            dk_ref[pl.ds(off, blk), :] += dk_inc
            return tuple(dqs)

        dqs = (jnp.zeros((blk, CAT_W), F32),) * qb
        dqs = lax.fori_loop(0, base, lambda j, c: step(j, c, ("f",) * qb), dqs)
        for s in range(qb):
            modes = tuple(None if t < s else ("m" if t == s else "f") for t in range(qb))
            dqs = step(base + s, dqs, modes)
        for t in range(qb):
            dq_ref[t * blk:(t + 1) * blk, :] = dqs[t] * MLA_SCALE

        @pl.when(g == pl.num_programs(1) - 1)
        def _():
            dk_ref[...] = dk_ref[...] * (1.0 / LOG2_E)

    return pl.pallas_call(
        body,
        out_shape=[jax.ShapeDtypeStruct((seq, N_MLA_HEADS * CAT_W), F32),
                   jax.ShapeDtypeStruct((seq, N_MLA_HEADS * CAT_W), F32),
                   jax.ShapeDtypeStruct((seq, MLA_W), F32)],
        grid=(N_MLA_HEADS, nkb // qb),
        in_specs=[pl.BlockSpec((rows, CAT_W), lambda h, i: (i, h)),
                  pl.BlockSpec((seq, CAT_W), lambda h, i: (0, h)),
                  pl.BlockSpec((seq, HEAD_DIM), lambda h, i: (0, h)),
                  pl.BlockSpec((rows, HEAD_DIM), lambda h, i: (i, h)),
                  pl.BlockSpec((rows, HEAD_DIM), lambda h, i: (i, h)),
                  pl.BlockSpec((rows, HEAD_DIM), lambda h, i: (i, h))],
        out_specs=[pl.BlockSpec((rows, CAT_W), lambda h, i: (i, h)),
                   pl.BlockSpec((seq, CAT_W), lambda h, i: (0, h)),
                   pl.BlockSpec((seq, HEAD_DIM), lambda h, i: (0, h))],
        name=name, compiler_params=_params("parallel", "arbitrary"),
    )(q_cat, k_cat, v, out, lse, dout)


def _local_step(x, mem, positions, target, w, g):
    seq = x.shape[0]
    inv_freq = jnp.power(ROPE_THETA, -jnp.arange(0, ROPE_DIM, 2, dtype=F32) / ROPE_DIM)
    ang = positions.astype(F32)[:, None] * inv_freq
    cos, sin = jnp.cos(ang), jnp.sin(ang)
    lane_pad = jnp.zeros((seq, HEAD_DIM - ROPE_DIM), F32)
    cos_t = jnp.concatenate([cos, cos, lane_pad], axis=1)
    sin_t = jnp.concatenate([-sin, sin, lane_pad], axis=1)
    gain_pad = jnp.zeros((1, HEAD_DIM - ROPE_DIM), F32)
    g_k_rope = jnp.concatenate([g["g_k_rope"], gain_pad], axis=1)
    g_q_rope = jnp.concatenate([g["b_g_q_rope"], gain_pad], axis=1)

    def norm_to_bf16(src, gain, name):
        def body(ins, outs, _):
            outs[0][...] = _rms(ins[0][...], ins[1][...]).astype(BF16)
        return _rowwise(body, seq, [(src, True), (gain, False)], [(src.shape[1], BF16)], name=name)[0]

    h_a = norm_to_bf16(x, g["a_norm"], "a_norm_fwd")
    qkv = _mm(h_a, w["a_in_qkv"], out_dtype=BF16, scale_cols=(SB_W, SB_Q_SCALE), name="a_in_qkv")
    gr = _mm(h_a, w["a_in_gate"], name="a_in_gate")
    sb = _sb_fwd(qkv, name="sb_fwd")
    mem0 = _mem_side_fwd(mem, g["mem_norm"][0:1], w["mem_kv"][0], g["g_mem_k"][0:1], tag="a")
    mixed_a = _mix_fwd(sb, gr, 0, mem0[2], mem0[3], g["g_mem_q"][0:1], name="a_mix_fwd")
    x1 = _mm(mixed_a, w["a_out"], res=x, name="a_out")

    def norms2_body(ins, outs, _):
        xv = ins[0][...]
        outs[0][...] = _rms(xv, ins[1][...]).astype(BF16)
        outs[1][...] = _rms(xv, ins[2][...]).astype(BF16)

    h_kv, h_b = _rowwise(norms2_body, seq, [(x1, True), (g["kv_norm"], False), (g["b_norm"], False)],
                         [(D_MODEL, BF16), (D_MODEL, BF16)], name="kv_b_norm_fwd")
    ckr = _mm(h_kv, w["dkv"], name="dkv")

    def ckr_body(ins, outs, _):
        ckr_ref, gc_ref, gr_ref, c_ref, s_ref = ins
        outs[0][...] = _rms(ckr_ref[:, :KV_LORA], gc_ref[...]).astype(BF16)
        kr = _rms(ckr_ref[:, KV_LORA:], gr_ref[...], n=ROPE_DIM)
        outs[1][...] = _rope(kr, c_ref[...], s_ref[...]).astype(BF16)

    c_n, k_r = _rowwise(ckr_body, seq,
                        [(ckr, True), (g["g_ckv"], False), (g_k_rope, False), (cos_t, True), (sin_t, True)],
                        [(KV_LORA, BF16), (HEAD_DIM, BF16)], name="ckv_prep_fwd")
    kv = _mm(c_n, w["ukv"], name="ukv")

    def kcat_body(ins, outs, _):
        kv_ref, kr_ref, gk_ref = ins
        kc_ref, v_ref = outs
        for h in range(N_MLA_HEADS):
            kc_ref[:, h * CAT_W:h * CAT_W + HEAD_DIM] = _rms(
                kv_ref[:, h * CAT_W:h * CAT_W + HEAD_DIM], gk_ref[...]).astype(BF16)
            kc_ref[:, h * CAT_W + HEAD_DIM:(h + 1) * CAT_W] = kr_ref[...]
            v_ref[:, _hs(h)] = kv_ref[:, h * CAT_W + HEAD_DIM:(h + 1) * CAT_W].astype(BF16)

    k_cat, v_mla = _rowwise(kcat_body, seq, [(kv, True), (k_r, True), (g["g_k_nope"], False)],
                            [(N_MLA_HEADS * CAT_W, BF16), (MLA_W, BF16)], name="k_prep_fwd")

    p2 = _mm(h_b, w["b_in"], name="b_in")

    def qlat_body(ins, outs, _):
        outs[0][...] = _rms(ins[0][:, :Q_LORA], ins[1][...]).astype(BF16)

    (q_l,) = _rowwise(qlat_body, seq, [(p2, True), (g["b_g_q_lat"], False)], [(Q_LORA, BF16)],
                      name="q_lat_norm_fwd")
    q_up = _mm(q_l, w["uq"], name="uq")

    def qcat_body(ins, outs, _):
        q_ref, gn_ref, gr_ref, c_ref, s_ref = ins
        (o_ref,) = outs
        for h in range(N_MLA_HEADS):
            o_ref[:, h * CAT_W:h * CAT_W + HEAD_DIM] = (MLA_Q_SCALE * _rms(
                q_ref[:, h * CAT_W:h * CAT_W + HEAD_DIM], gn_ref[...])).astype(BF16)
            qr = _rms(q_ref[:, h * CAT_W + HEAD_DIM:(h + 1) * CAT_W], gr_ref[...], n=ROPE_DIM)
            o_ref[:, h * CAT_W + HEAD_DIM:(h + 1) * CAT_W] = (
                MLA_Q_SCALE * _rope(qr, c_ref[...], s_ref[...])).astype(BF16)

    (q_cat,) = _rowwise(qcat_body, seq,
                        [(q_up, True), (g["b_g_q_nope"], False), (g_q_rope, False), (cos_t, True), (sin_t, True)],
                        [(N_MLA_HEADS * CAT_W, BF16)], name="q_prep_fwd")
    att, lse = _mla_fwd(q_cat, k_cat, v_mla, name="mla_fwd")
    mem1 = _mem_side_fwd(mem, g["mem_norm"][1:2], w["mem_kv"][1], g["g_mem_k"][1:2], tag="b")
    mixed_b = _mix_fwd(att, p2, Q_LORA, mem1[2], mem1[3], g["g_mem_q"][1:2], name="b_mix_fwd")
    y = _mm(mixed_b, w["b_out"], res=x1, name="b_out")

    def loss_body(ins, outs, accs):
        diff = ins[0][...] - ins[1][...]
        outs[0][...] = diff / D_MODEL
        col = jnp.sum(diff * diff, axis=0, keepdims=True)
        part = col[:, :HEAD_DIM]
        for c in range(1, D_MODEL // HEAD_DIM):
            part = part + col[:, _hs(c)]
        accs[0][...] += part * (0.5 / D_MODEL)

    dy, loss_part = _rowwise(loss_body, seq, [(y, True), (target, True)], [(D_MODEL, F32)],
                             [((1, HEAD_DIM), F32)], name="loss")

    gw, gg = {}, {}
    dmixed_b = _mm(dy, w["b_out"], tb=True, name="b_out_dx")
    gw["b_out"] = _mm(mixed_b, dy, ta=True, out_dtype=BF16, name="b_out_dw")
    datt, dgate_b, dmk1, dmv1, gq1 = _mix_bwd(dmixed_b, att, p2, Q_LORA, mem1[2], mem1[3],
                                              g["g_mem_q"][1:2], name="b_mix_bwd")
    dq_cat, dk_cat, dv_mla = _mla_bwd(q_cat, k_cat, v_mla, att, lse, datt, name="mla_bwd")

    def qcat_bwd_body(ins, outs, accs):
        q_ref, dq_ref, gn_ref, gr_ref, c_ref, s_ref = ins
        (o_ref,) = outs
        dgn_ref, dgr_ref = accs
        for h in range(N_MLA_HEADS):
            dx, dg = _rms_bwd(q_ref[:, h * CAT_W:h * CAT_W + HEAD_DIM], gn_ref[...],
                              dq_ref[:, h * CAT_W:h * CAT_W + HEAD_DIM])
            o_ref[:, h * CAT_W:h * CAT_W + HEAD_DIM] = dx.astype(BF16)
            dgn_ref[...] += dg
            dn = _rope_bwd(dq_ref[:, h * CAT_W + HEAD_DIM:(h + 1) * CAT_W], c_ref[...], s_ref[...])
            dx, dg = _rms_bwd(q_ref[:, h * CAT_W + HEAD_DIM:(h + 1) * CAT_W], gr_ref[...], dn, n=ROPE_DIM)
            o_ref[:, h * CAT_W + HEAD_DIM:(h + 1) * CAT_W] = dx.astype(BF16)
            dgr_ref[...] += dg

    dq_up, gg["b_g_q_nope"], dgqr = _rowwise(
        qcat_bwd_body, seq,
        [(q_up, True), (dq_cat, True), (g["b_g_q_nope"], False), (g_q_rope, False), (cos_t, True), (sin_t, True)],
        [(N_MLA_HEADS * CAT_W, BF16)], [((1, HEAD_DIM), F32), ((1, HEAD_DIM), F32)], name="q_prep_bwd")
    gg["b_g_q_rope"] = dgqr
    dq_l = _mm(dq_up, w["uq"], tb=True, name="uq_dx")
    gw["uq"] = _mm(q_l, dq_up, ta=True, out_dtype=BF16, n_split=N_CHIPS, name="uq_dw")

    def qlat_bwd_body(ins, outs, accs):
        p2_ref, dql_ref, dgate_ref, gl_ref = ins
        dx, dg = _rms_bwd(p2_ref[:, :Q_LORA], gl_ref[...], dql_ref[...])
        outs[0][:, :Q_LORA] = dx.astype(BF16)
        outs[0][:, Q_LORA:] = dgate_ref[...]
        accs[0][...] += dg

    dp2, gg["b_g_q_lat"] = _rowwise(
        qlat_bwd_body, seq, [(p2, True), (dq_l, True), (dgate_b, True), (g["b_g_q_lat"], False)],
        [(Q_LORA + GATE_W, BF16)], [((1, Q_LORA), F32)], name="q_lat_norm_bwd")
    dh_b = _mm(dp2, w["b_in"], tb=True, name="b_in_dx")
    gw["b_in"] = _mm(h_b, dp2, ta=True, out_dtype=BF16, n_split=N_CHIPS, name="b_in_dw")

    def kcat_bwd_body(ins, outs, accs):
        kv_ref, dkc_ref, dv_ref, gk_ref = ins
        dkv_ref, dkr_ref = outs
        (dgk_ref,) = accs
        dkr = jnp.zeros(dkr_ref.shape, F32)
        for h in range(N_MLA_HEADS):
            dx, dg = _rms_bwd(kv_ref[:, h * CAT_W:h * CAT_W + HEAD_DIM], gk_ref[...],
                              dkc_ref[:, h * CAT_W:h * CAT_W + HEAD_DIM])
            dkv_ref[:, h * CAT_W:h * CAT_W + HEAD_DIM] = dx.astype(BF16)
            dgk_ref[...] += dg
            dkv_ref[:, h * CAT_W + HEAD_DIM:(h + 1) * CAT_W] = dv_ref[:, _hs(h)].astype(BF16)
            dkr = dkr + dkc_ref[:, h * CAT_W + HEAD_DIM:(h + 1) * CAT_W]
        dkr_ref[...] = dkr

    dkv, dk_r, gg["g_k_nope"] = _rowwise(
        kcat_bwd_body, seq, [(kv, True), (dk_cat, True), (dv_mla, True), (g["g_k_nope"], False)],
        [(N_MLA_HEADS * CAT_W, BF16), (HEAD_DIM, F32)], [((1, HEAD_DIM), F32)], name="k_prep_bwd")
    dc_n = _mm(dkv, w["ukv"], tb=True, name="ukv_dx")
    gw["ukv"] = _mm(c_n, dkv, ta=True, out_dtype=BF16, n_split=N_CHIPS, name="ukv_dw")

    def ckr_bwd_body(ins, outs, accs):
        ckr_ref, dcn_ref, dkr_ref, gc_ref, gr_ref, c_ref, s_ref = ins
        dx, dg = _rms_bwd(ckr_ref[:, :KV_LORA], gc_ref[...], dcn_ref[...])
        outs[0][:, :KV_LORA] = dx.astype(BF16)
        accs[0][...] += dg
        dn = _rope_bwd(dkr_ref[...], c_ref[...], s_ref[...])
        dx, dg = _rms_bwd(ckr_ref[:, KV_LORA:], gr_ref[...], dn, n=ROPE_DIM)
        outs[0][:, KV_LORA:] = dx.astype(BF16)
        accs[1][...] += dg

    dckr, gg["g_ckv"], gg["g_k_rope"] = _rowwise(
        ckr_bwd_body, seq,
        [(ckr, True), (dc_n, True), (dk_r, True), (g["g_ckv"], False), (g_k_rope, False),
         (cos_t, True), (sin_t, True)],
        [(KV_LORA + HEAD_DIM, BF16)], [((1, KV_LORA), F32), ((1, HEAD_DIM), F32)], name="ckv_prep_bwd")
    dh_kv = _mm(dckr, w["dkv"], tb=True, name="dkv_dx")
    gw["dkv"] = _mm(h_kv, dckr, ta=True, out_dtype=BF16, name="dkv_dw")

    def norms2_bwd_body(ins, outs, accs):
        x_ref, dy_ref, dhk_ref, dhb_ref, gk_ref, gb_ref = ins
        xv = x_ref[...]
        dxk, dgk = _rms_bwd(xv, gk_ref[...], dhk_ref[...])
        dxb, dgb = _rms_bwd(xv, gb_ref[...], dhb_ref[...])
        outs[0][...] = dy_ref[...] + dxk + dxb
        accs[0][...] += dgk
        accs[1][...] += dgb

    dx1, gg["kv_norm"], gg["b_norm"] = _rowwise(
        norms2_bwd_body, seq,
        [(x1, True), (dy, True), (dh_kv, True), (dh_b, True), (g["kv_norm"], False), (g["b_norm"], False)],
        [(D_MODEL, F32)], [((1, D_MODEL), F32), ((1, D_MODEL), F32)], name="kv_b_norm_bwd")

    dmixed_a = _mm(dx1, w["a_out"], tb=True, name="a_out_dx")
    gw["a_out"] = _mm(mixed_a, dx1, ta=True, out_dtype=BF16, name="a_out_dw")
    dsb, dgate_a, dmk0, dmv0, gq0 = _mix_bwd(dmixed_a, sb, gr, 0, mem0[2], mem0[3],
                                             g["g_mem_q"][0:1], name="a_mix_bwd")
    dq, dk, dv = _sb_bwd(qkv, sb, dsb, name="sb_bwd")
    dp_a = jnp.concatenate([dq.astype(BF16), dk.astype(BF16), dv.astype(BF16), dgate_a], axis=1)
    dh_a = _mm(dp_a, w["a_in"], tb=True, name="a_in_dx")
    gw["a_in"] = _mm(h_a, dp_a, ta=True, out_dtype=BF16, n_split=N_CHIPS, name="a_in_dw")

    def norm_a_bwd_body(ins, outs, accs):
        dx, dg = _rms_bwd(ins[0][...], ins[3][...], ins[2][...])
        outs[0][...] = ins[1][...] + dx
        accs[0][...] += dg

    grad_x, gg["a_norm"] = _rowwise(
        norm_a_bwd_body, seq, [(x, True), (dx1, True), (dh_a, True), (g["a_norm"], False)],
        [(D_MODEL, F32)], [((1, D_MODEL), F32)], name="a_norm_bwd")

    dw0, dgn0, dgk0 = _mem_side_bwd(mem, g["mem_norm"][0:1], w["mem_kv"][0], g["g_mem_k"][0:1],
                                    mem0[0], mem0[1], dmk0, dmv0, tag="a")
    dw1, dgn1, dgk1 = _mem_side_bwd(mem, g["mem_norm"][1:2], w["mem_kv"][1], g["g_mem_k"][1:2],
                                    mem1[0], mem1[1], dmk1, dmv1, tag="b")
    gw["mem_kv"] = (dw0, dw1)
    gg["mem_norm"] = jnp.concatenate([dgn0, dgn1], axis=0)
    gg["g_mem_q"] = jnp.concatenate([gq0, gq1], axis=0)
    gg["g_mem_k"] = jnp.concatenate([dgk0, dgk1], axis=0)
    return loss_part, grad_x, gw, gg


HBM_SPEC = pl.BlockSpec(memory_space=pl.ANY)


def _other_chips():
    x, y = lax.axis_index("x"), lax.axis_index("y")
    return [(1 - x, y), (x, 1 - y), (1 - x, 1 - y)]


def _allgather_chips(shards):
    n = len(shards)
    split = [s.shape[0] % 32 == 0 for s in shards]

    def body(*refs):
        ins, outs = refs[:n], refs[n:2 * n]
        send, recv, fsend, frecv, loc = refs[2 * n:]
        x, y, c = lax.axis_index("x"), lax.axis_index("y"), lax.axis_index("c")
        me = 2 * x + y
        chips = _other_chips()

        def part(ref, wi):
            if not split[wi]:
                return ref
            half = shards[wi].shape[0] // 2
            return ref.at[pl.ds(pl.multiple_of(c * half, 16), half)]

        def ici(wi, k, src_chip, to):
            return pltpu.make_async_remote_copy(
                src_ref=part(ins[wi], wi), dst_ref=part(outs[wi].at[src_chip], wi),
                send_sem=send.at[wi, k], recv_sem=recv.at[wi, k], device_id=to, device_id_type=MESH)

        def d2d(wi, k, src_chip):
            rows = part(outs[wi].at[src_chip], wi)
            return pltpu.make_async_remote_copy(
                src_ref=rows, dst_ref=rows, send_sem=fsend.at[wi, k], recv_sem=frecv.at[wi, k],
                device_id=(x, y, 1 - c), device_id_type=MESH)

        started = []
        for wi in range(n):
            own = pltpu.make_async_copy(ins[wi], outs[wi].at[me], loc.at[wi])
            own.start()
            started.append(own)
            for k, (tx, ty) in enumerate(chips):
                ici(wi, k, me, (tx, ty, c)).start()
        for wi in range(n):
            for k, (tx, ty) in enumerate(chips):
                landed = ici(wi, k, 2 * tx + ty, (tx, ty, c))
                landed.wait_recv()
                if split[wi]:
                    d2d(wi, k, 2 * tx + ty).start()
        for wi in range(n):
            for k, (tx, ty) in enumerate(chips):
                ici(wi, k, me, (tx, ty, c)).wait_send()
                if split[wi]:
                    fwd = d2d(wi, k, 2 * tx + ty)
                    fwd.wait_send()
                    fwd.wait_recv()
        for own in started:
            own.wait()

    return pl.pallas_call(
        body, out_shape=[jax.ShapeDtypeStruct((N_CHIPS,) + s.shape, s.dtype) for s in shards],
        in_specs=[HBM_SPEC] * n, out_specs=[HBM_SPEC] * n,
        scratch_shapes=[pltpu.SemaphoreType.DMA((n, 3)), pltpu.SemaphoreType.DMA((n, 3)),
                        pltpu.SemaphoreType.DMA((n, 3)), pltpu.SemaphoreType.DMA((n, 3)),
                        pltpu.SemaphoreType.DMA((n,))],
        name="allgather_weights",
    )(*shards)


def _scatter_to_chips(grads):
    n = len(grads)

    def body(*refs):
        ins, outs = refs[:n], refs[n:2 * n]
        send, recv, loc = refs[2 * n:]
        c = lax.axis_index("c")
        me = 2 * lax.axis_index("x") + lax.axis_index("y")
        copies = []
        for wi in range(n):
            own = pltpu.make_async_copy(ins[wi].at[me], outs[wi].at[3], loc.at[wi])
            own.start()
            copies.append(own)
            for k, (tx, ty) in enumerate(_other_chips()):
                cp = pltpu.make_async_remote_copy(
                    src_ref=ins[wi].at[2 * tx + ty], dst_ref=outs[wi].at[k], send_sem=send.at[wi, k],
                    recv_sem=recv.at[wi, k], device_id=(tx, ty, c), device_id_type=MESH)
                cp.start()
                copies.append(cp)
        for cp in copies:
            cp.wait()

    return pl.pallas_call(
        body, out_shape=[jax.ShapeDtypeStruct(s.shape, s.dtype) for s in grads],
        in_specs=[HBM_SPEC] * n, out_specs=[HBM_SPEC] * n,
        scratch_shapes=[pltpu.SemaphoreType.DMA((n, 3)), pltpu.SemaphoreType.DMA((n, 3)),
                        pltpu.SemaphoreType.DMA((n,))],
        name="scatter_grads",
    )(*grads)


def _swap_with_sibling(parts):
    n = len(parts)

    def body(*refs):
        ins, outs = refs[:n], refs[n:2 * n]
        send, recv = refs[2 * n:]
        sib = (lax.axis_index("x"), lax.axis_index("y"), 1 - lax.axis_index("c"))
        copies = []
        for wi in range(n):
            cp = pltpu.make_async_remote_copy(
                src_ref=ins[wi], dst_ref=outs[wi], send_sem=send.at[wi], recv_sem=recv.at[wi],
                device_id=sib, device_id_type=MESH)
            cp.start()
            copies.append(cp)
        for cp in copies:
            cp.wait()

    return pl.pallas_call(
        body, out_shape=[jax.ShapeDtypeStruct(s.shape, s.dtype) for s in parts],
        in_specs=[HBM_SPEC] * n, out_specs=[HBM_SPEC] * n,
        scratch_shapes=[pltpu.SemaphoreType.DMA((n,)), pltpu.SemaphoreType.DMA((n,))],
        name="swap_partial_grads",
    )(*parts)


def _allreduce_small(vec, loss_row):
    rows = vec.shape[0]

    def body(v_ref, o_ref, buf, send, recv):
        x, y, c = lax.axis_index("x"), lax.axis_index("y"), lax.axis_index("c")
        me = 4 * x + 2 * y + c
        buf[me] = v_ref[...]
        copies = []
        for r in range(1, N_DEV):
            peer = (x ^ ((r >> 2) & 1), y ^ ((r >> 1) & 1), c ^ (r & 1))
            cp = pltpu.make_async_remote_copy(
                src_ref=v_ref, dst_ref=buf.at[me], send_sem=send.at[r - 1], recv_sem=recv.at[r - 1],
                device_id=peer, device_id_type=MESH)
            cp.start()
            copies.append(cp)
        for cp in copies:
            cp.wait()
        total = buf[0]
        for d in range(1, N_DEV):
            total = total + buf[d]
        o_ref[...] = total
        o_ref[loss_row:loss_row + 1, :] = jnp.broadcast_to(
            jnp.sum(total[loss_row:loss_row + 1, :], axis=-1, keepdims=True), (1, HEAD_DIM))

    return pl.pallas_call(
        body, out_shape=jax.ShapeDtypeStruct(vec.shape, F32),
        in_specs=[pl.BlockSpec(memory_space=pltpu.VMEM)], out_specs=pl.BlockSpec(memory_space=pltpu.VMEM),
        scratch_shapes=[pltpu.VMEM((N_DEV, rows, HEAD_DIM), F32),
                        pltpu.SemaphoreType.DMA((N_DEV - 1,)), pltpu.SemaphoreType.DMA((N_DEV - 1,))],
        name="allreduce_gains",
    )(vec)


def _sum_slots(r, *, name):
    _, rows, width = r.shape
    blk = _pick(rows, (256, 128, 64, 32, 16, 8))

    def body(r_ref, o_ref):
        o_ref[...] = ((r_ref[3].astype(F32) + r_ref[0].astype(F32)) + r_ref[1].astype(F32)) + r_ref[2].astype(F32)

    return pl.pallas_call(
        body, out_shape=jax.ShapeDtypeStruct((rows, width), F32), grid=(rows // blk,),
        in_specs=[pl.BlockSpec((N_CHIPS, blk, width), lambda i: (0, i, 0))],
        out_specs=pl.BlockSpec((blk, width), lambda i: (i, 0)),
        name=name, compiler_params=_params("parallel"),
    )(r)


def _adamw(wgt, grads, m, v, *, name):
    rows, width = wgt.shape
    blk = _pick(rows, (256, 128, 64, 32, 16, 8))
    n_g = len(grads)

    def body(*refs):
        w_ref, m_ref, v_ref = refs[0], refs[1 + n_g], refs[2 + n_g]
        g_out, d_out, m_out, v_out = refs[3 + n_g:]
        grad = refs[1][...]
        for t in range(1, n_g):
            grad = grad + refs[1 + t][...]
        m_new = ADAM_B1 * m_ref[...] + (1.0 - ADAM_B1) * grad
        v_new = ADAM_B2 * v_ref[...] + (1.0 - ADAM_B2) * (grad * grad)
        m_hat = m_new / (1.0 - ADAM_B1 ** ADAM_STEP)
        v_hat = v_new / (1.0 - ADAM_B2 ** ADAM_STEP)
        g_out[...] = grad
        d_out[...] = -ADAM_LR * (m_hat / (jnp.sqrt(v_hat) + ADAM_EPS) + ADAM_WD * w_ref[...])
        m_out[...] = m_new
        v_out[...] = v_new

    spec = pl.BlockSpec((blk, width), lambda i: (i, 0))
    out = jax.ShapeDtypeStruct((rows, width), F32)
    return pl.pallas_call(
        body, out_shape=[out] * 4, grid=(rows // blk,), in_specs=[spec] * (3 + n_g),
        out_specs=[spec] * 4, name=name, compiler_params=_params("parallel"),
    )(wgt, *grads, m, v)


_SMALL = (("a_norm", 2048), ("kv_norm", 2048), ("g_ckv", 512), ("g_k_nope", 128), ("g_k_rope", 64),
          ("b_norm", 2048), ("b_g_q_lat", 512), ("b_g_q_nope", 128), ("b_g_q_rope", 64),
          ("mem_norm", 4096), ("g_mem_q", 256), ("g_mem_k", 256))


def _lanes(n):
    return -(-n // HEAD_DIM) * HEAD_DIM


def _pack_rows(pieces, pad_rows_to=8):
    flat = jnp.concatenate(pieces, axis=1)
    rows = flat.shape[1] // HEAD_DIM
    pad = (-rows) % pad_rows_to
    if pad:
        flat = jnp.concatenate([flat, jnp.zeros((1, pad * HEAD_DIM), F32)], axis=1)
    return flat.reshape(rows + pad, HEAD_DIM)


def _pad_lanes(a):
    a = a.reshape(1, -1)
    pad = _lanes(a.shape[1]) - a.shape[1]
    if pad:
        a = jnp.concatenate([a, jnp.zeros((1, pad), F32)], axis=1)
    return a


def kernel(x, mem, positions, a_norm, a_w_in, a_w_out, kv_norm, w_dkv, g_ckv, w_ukv, g_k_nope, g_k_rope, b_norm, b_w_in, b_g_q_lat, b_w_uq, b_g_q_nope, b_g_q_rope, b_w_out, mem_norm, w_mem_kv, g_mem_q, g_mem_k, loss_target, m_a_norm, m_a_w_in, m_a_w_out, m_kv_norm, m_w_dkv, m_g_ckv, m_w_ukv, m_g_k_nope, m_g_k_rope, m_b_norm, m_b_w_in, m_b_g_q_lat, m_b_w_uq, m_b_g_q_nope, m_b_g_q_rope, m_b_w_out, m_mem_norm, m_w_mem_kv, m_g_mem_q, m_g_mem_k, v_a_norm, v_a_w_in, v_a_w_out, v_kv_norm, v_w_dkv, v_g_ckv, v_w_ukv, v_g_k_nope, v_g_k_rope, v_b_norm, v_b_w_in, v_b_g_q_lat, v_b_w_uq, v_b_g_q_nope, v_b_g_q_rope, v_b_w_out, v_mem_norm, v_w_mem_kv, v_g_mem_q, v_g_mem_k):
    chip = 2 * lax.axis_index("x") + lax.axis_index("y")
    rows_dkv = D_MODEL // N_CHIPS
    heads_per_chip = N_MLA_HEADS // N_CHIPS
    qk_w = HEAD_DIM + ROPE_DIM

    big = {"a_in": a_w_in[0], "a_out": a_w_out[0], "dkv": w_dkv, "ukv": w_ukv, "b_in": b_w_in[0],
           "uq": b_w_uq[0], "b_out": b_w_out[0], "mem_kv": w_mem_kv.reshape(2 * rows_dkv, 2 * MEM_W)}
    big_m = {"a_in": m_a_w_in[0], "a_out": m_a_w_out[0], "dkv": m_w_dkv, "ukv": m_w_ukv, "b_in": m_b_w_in[0],
             "uq": m_b_w_uq[0], "b_out": m_b_w_out[0], "mem_kv": m_w_mem_kv.reshape(2 * rows_dkv, 2 * MEM_W)}
    big_v = {"a_in": v_a_w_in[0], "a_out": v_a_w_out[0], "dkv": v_w_dkv, "ukv": v_w_ukv, "b_in": v_b_w_in[0],
             "uq": v_b_w_uq[0], "b_out": v_b_w_out[0], "mem_kv": v_w_mem_kv.reshape(2 * rows_dkv, 2 * MEM_W)}
    names = list(big)
    gathered = _allgather_chips([big[n].astype(BF16) for n in names] + [a_norm])
    st = dict(zip(names, gathered[:-1]))
    a_in_full = st["a_in"].transpose(1, 0, 2).reshape(D_MODEL, QKV_W + GATE_W)
    uq = st["uq"].reshape(N_CHIPS, Q_LORA, heads_per_chip, qk_w)
    uq = jnp.pad(uq, ((0, 0), (0, 0), (0, 0), (0, CAT_W - qk_w)))
    w = {
        "a_in": a_in_full,
        "a_in_qkv": a_in_full[:, :QKV_W],
        "a_in_gate": a_in_full[:, QKV_W:],
        "a_out": st["a_out"].reshape(D_MODEL, D_MODEL),
        "dkv": jnp.pad(st["dkv"].reshape(D_MODEL, KV_LORA + ROPE_DIM), ((0, 0), (0, HEAD_DIM - ROPE_DIM))),
        "ukv": st["ukv"].transpose(1, 0, 2).reshape(KV_LORA, N_MLA_HEADS * CAT_W),
        "b_in": st["b_in"].transpose(1, 0, 2).reshape(D_MODEL, Q_LORA + GATE_W),
        "uq": uq.transpose(1, 0, 2, 3).reshape(Q_LORA, N_MLA_HEADS * CAT_W),
        "b_out": st["b_out"].reshape(D_MODEL, D_MODEL),
        "mem_kv": st["mem_kv"].reshape(N_CHIPS, 2, rows_dkv, 2 * MEM_W).transpose(1, 0, 2, 3).reshape(
            2, D_MODEL, 2 * MEM_W),
    }
    gains = {
        "a_norm": gathered[-1].reshape(1, D_MODEL), "kv_norm": kv_norm.reshape(1, -1),
        "g_ckv": g_ckv.reshape(1, -1), "g_k_nope": g_k_nope.reshape(1, -1), "g_k_rope": g_k_rope.reshape(1, -1),
        "b_norm": b_norm, "b_g_q_lat": b_g_q_lat, "b_g_q_nope": b_g_q_nope, "b_g_q_rope": b_g_q_rope,
        "mem_norm": mem_norm, "g_mem_q": g_mem_q, "g_mem_k": g_mem_k,
    }

    loss_part, grad_x, gw, gg = _local_step(x[0], mem[0], positions[0], loss_target[0], w, gains)

    stacked = {
        "a_in": gw["a_in"],
        "a_out": gw["a_out"].reshape(N_CHIPS, rows_dkv, D_MODEL),
        "dkv": gw["dkv"][:, :KV_LORA + ROPE_DIM].reshape(N_CHIPS, rows_dkv, KV_LORA + ROPE_DIM),
        "ukv": gw["ukv"],
        "b_in": gw["b_in"],
        "uq": gw["uq"].reshape(N_CHIPS, Q_LORA, heads_per_chip, CAT_W)[..., :qk_w].reshape(
            N_CHIPS, Q_LORA, heads_per_chip * qk_w),
        "b_out": gw["b_out"].reshape(N_CHIPS, rows_dkv, D_MODEL),
        "mem_kv": jnp.stack([gw["mem_kv"][0].reshape(N_CHIPS, rows_dkv, 2 * MEM_W),
                             gw["mem_kv"][1].reshape(N_CHIPS, rows_dkv, 2 * MEM_W)], axis=1).reshape(
            N_CHIPS, 2 * rows_dkv, 2 * MEM_W),
    }
    received = _scatter_to_chips([stacked[n] for n in names])
    partial = [_sum_slots(r, name=f"sum_slots_{n}") for n, r in zip(names, received)]
    sibling = _swap_with_sibling(partial)
    big_out = {}
    for n, mine, theirs in zip(names, partial, sibling):
        big_out[n] = _adamw(big[n], [mine, theirs], big_m[n], big_v[n], name=f"adamw_{n}")

    pieces = [_pad_lanes(gg[n]) if n not in ("g_k_rope", "b_g_q_rope") else gg[n] for n, _ in _SMALL]
    pieces.append(loss_part)
    loss_row = sum(_lanes(size) for _, size in _SMALL) // HEAD_DIM
    summed = _allreduce_small(_pack_rows(pieces), loss_row)
    flat = summed.reshape(1, -1)
    small_g, off = {}, 0
    for n, size in _SMALL:
        small_g[n] = flat[:, off:off + size]
        off += _lanes(size)
    loss = flat[0, off]
    small_g["a_norm"] = lax.dynamic_slice(small_g["a_norm"], (0, chip * rows_dkv), (1, rows_dkv))

    small_w = {"a_norm": a_norm, "kv_norm": kv_norm, "g_ckv": g_ckv, "g_k_nope": g_k_nope, "g_k_rope": g_k_rope,
               "b_norm": b_norm, "b_g_q_lat": b_g_q_lat, "b_g_q_nope": b_g_q_nope, "b_g_q_rope": b_g_q_rope,
               "mem_norm": mem_norm, "g_mem_q": g_mem_q, "g_mem_k": g_mem_k}
    small_m = {"a_norm": m_a_norm, "kv_norm": m_kv_norm, "g_ckv": m_g_ckv, "g_k_nope": m_g_k_nope,
               "g_k_rope": m_g_k_rope, "b_norm": m_b_norm, "b_g_q_lat": m_b_g_q_lat, "b_g_q_nope": m_b_g_q_nope,
               "b_g_q_rope": m_b_g_q_rope, "mem_norm": m_mem_norm, "g_mem_q": m_g_mem_q, "g_mem_k": m_g_mem_k}
    small_v = {"a_norm": v_a_norm, "kv_norm": v_kv_norm, "g_ckv": v_g_ckv, "g_k_nope": v_g_k_nope,
               "g_k_rope": v_g_k_rope, "b_norm": v_b_norm, "b_g_q_lat": v_b_g_q_lat, "b_g_q_nope": v_b_g_q_nope,
               "b_g_q_rope": v_b_g_q_rope, "mem_norm": v_mem_norm, "g_mem_q": v_g_mem_q, "g_mem_k": v_g_mem_k}
    snames = [n for n, _ in _SMALL]
    packs = [_pack_rows([_pad_lanes(src[n]) for n in snames])
             for src in (small_w, small_g, small_m, small_v)]
    small_res = _adamw(packs[0], [packs[1]], packs[2], packs[3], name="adamw_gains")
    small_out = {n: [] for n in snames}
    for res in small_res:
        flat_r = res.reshape(1, -1)
        off = 0
        for n in snames:
            size = small_w[n].size
            small_out[n].append(flat_r[:, off:off + size].reshape(small_w[n].shape))
            off += _lanes(size)

    big_names = {"a_w_in": ("a_in", a_w_in), "a_w_out": ("a_out", a_w_out), "w_dkv": ("dkv", w_dkv),
                 "w_ukv": ("ukv", w_ukv), "b_w_in": ("b_in", b_w_in), "b_w_uq": ("uq", b_w_uq),
                 "b_w_out": ("b_out", b_w_out), "w_mem_kv": ("mem_kv", w_mem_kv)}
    order = ["a_norm", "a_w_in", "a_w_out", "kv_norm", "w_dkv", "g_ckv", "w_ukv", "g_k_nope", "g_k_rope",
             "b_norm", "b_w_in", "b_g_q_lat", "b_w_uq", "b_g_q_nope", "b_g_q_rope", "b_w_out", "mem_norm",
             "w_mem_kv", "g_mem_q", "g_mem_k"]
    groups = [[], [], [], []]
    for n in order:
        if n in big_names:
            key, ref_arr = big_names[n]
            for t in range(4):
                groups[t].append(big_out[key][t].reshape(ref_arr.shape))
        else:
            for t in range(4):
                groups[t].append(small_out[n][t])
    return (loss, grad_x[None], *groups[0], *groups[1], *groups[2], *groups[3])
```

```python
import functools

import jax
import jax.numpy as jnp
from jax import lax
from jax.experimental import pallas as pl
from jax.experimental.pallas import tpu as pltpu

F32 = jnp.float32
BF16 = jnp.bfloat16
MESH = pl.DeviceIdType.MESH

D_MODEL = 2048
HEAD_DIM = 128
N_SB_HEADS = 12
N_MEM_HEADS = 4
N_MLA_HEADS = 12
MEM_LEN = 256
Q_LORA = 512
KV_LORA = 512
ROPE_DIM = 64
SB_W = N_SB_HEADS * HEAD_DIM
MEM_W = N_MEM_HEADS * HEAD_DIM
MLA_W = N_MLA_HEADS * HEAD_DIM
QKV_W = 3 * SB_W
GATE_W = SB_W + 2 * MEM_W
CAT_W = 2 * HEAD_DIM
ROPE_THETA = 10000.0
EPS = 1e-6
N_CHIPS = 4
N_DEV = 8

ADAM_LR = 0.001
ADAM_B1 = 0.9
ADAM_B2 = 0.999
ADAM_EPS = 1e-08
ADAM_WD = 0.01
ADAM_STEP = 10

VMEM_LIMIT_BYTES = 56 * 1024 * 1024
MM_OPERAND_VMEM_BYTES = 24 * 1024 * 1024
ROW_BLOCK = 256
ATT_BLOCK = 256


def _params(*sem):
    return pltpu.CompilerParams(dimension_semantics=sem, vmem_limit_bytes=VMEM_LIMIT_BYTES)


def _pick(n, cands):
    for c in cands:
        if n % c == 0:
            return c
    return n


def _mm(a, b, *, name, ta=False, tb=False, out_dtype=F32, res=None, n_split=1, scale_cols=None):
    if ta:
        k_dim, m_dim = a.shape
    else:
        m_dim, k_dim = a.shape
    if tb:
        n_dim, kb = b.shape
    else:
        kb, n_dim = b.shape
    assert kb == k_dim, (a.shape, b.shape)
    n_per = n_dim // n_split
    bm = m_dim if m_dim <= 1024 else _pick(m_dim, (1024, 512, 256))
    bn = n_per if n_per <= 1024 else _pick(n_per, (1024, 896, 768, 640, 512, 256, 128))
    per_k = (bm * a.dtype.itemsize + bn * b.dtype.itemsize) * 2
    bk = next((c for c in (k_dim, 2048, 1024, 512, 256, 128)
               if c <= k_dim and k_dim % c == 0 and c * per_k <= MM_OPERAND_VMEM_BYTES), 128)
    nk = k_dim // bk
    nb_per = n_per // bn
    grid = (m_dim // bm, n_dim // bn, nk)
    a_spec = (pl.BlockSpec((bk, bm), lambda i, j, k: (k, i)) if ta
              else pl.BlockSpec((bm, bk), lambda i, j, k: (i, k)))
    b_spec = (pl.BlockSpec((bn, bk), lambda i, j, k: (j, k)) if tb
              else pl.BlockSpec((bk, bn), lambda i, j, k: (k, j)))
    dims = (((0 if ta else 1,), (1 if tb else 0,)), ((), ()))
    in_specs = [a_spec, b_spec]
    args = [a, b]
    if res is not None:
        in_specs.append(pl.BlockSpec((bm, bn), lambda i, j, k: (i, j)))
        args.append(res)
    if n_split == 1:
        out_shape = jax.ShapeDtypeStruct((m_dim, n_dim), out_dtype)
        out_spec = pl.BlockSpec((bm, bn), lambda i, j, k: (i, j))
    else:
        out_shape = jax.ShapeDtypeStruct((n_split, m_dim, n_per), out_dtype)
        out_spec = pl.BlockSpec((None, bm, bn), lambda i, j, k: (j // nb_per, i, j % nb_per))

    def body(*refs):
        if res is None:
            a_ref, b_ref, o_ref, acc = refs
            r_ref = None
        else:
            a_ref, b_ref, r_ref, o_ref, acc = refs
        k = pl.program_id(2)
        col_block = pl.program_id(1)

        @pl.when(k == 0)
        def _():
            acc[...] = jnp.zeros_like(acc)

        acc[...] += lax.dot_general(a_ref[...].astype(BF16), b_ref[...].astype(BF16), dims,
                                    preferred_element_type=F32)

        @pl.when(k == nk - 1)
        def _():
            r = acc[...]
            if r_ref is not None:
                r = r + r_ref[...]
            if scale_cols is not None:
                assert scale_cols[0] % bn == 0
                r = r * jnp.where(col_block < scale_cols[0] // bn, scale_cols[1], 1.0)
            o_ref[...] = r.astype(out_dtype)

    return pl.pallas_call(
        body, out_shape=out_shape, grid=grid, in_specs=in_specs, out_specs=out_spec,
        scratch_shapes=[pltpu.VMEM((bm, bn), F32)], name=name,
        compiler_params=_params("parallel", "parallel", "arbitrary"),
    )(*args)


def _rowwise(body, n_rows, ins, outs, accs=(), *, name, block=ROW_BLOCK):
    blk = min(block, n_rows)
    assert n_rows % blk == 0
    in_specs = []
    for arr, is_row in ins:
        if is_row:
            assert arr.shape[0] == n_rows, (name, arr.shape, n_rows)
            in_specs.append(pl.BlockSpec((blk, arr.shape[1]), lambda i: (i, 0)))
        else:
            in_specs.append(pl.BlockSpec(arr.shape, lambda i, nd=arr.ndim: (0,) * nd))
    out_shape = [jax.ShapeDtypeStruct((n_rows, w), dt) for w, dt in outs]
    out_specs = [pl.BlockSpec((blk, w), lambda i: (i, 0)) for w, _ in outs]
    out_shape += [jax.ShapeDtypeStruct(s, dt) for s, dt in accs]
    out_specs += [pl.BlockSpec(s, lambda i, nd=len(s): (0,) * nd) for s, _ in accs]
    n_in, n_out, n_acc = len(ins), len(outs), len(accs)

    def kern(*refs):
        in_refs = refs[:n_in]
        out_refs = refs[n_in:n_in + n_out]
        acc_refs = refs[n_in + n_out:]
        if n_acc:
            @pl.when(pl.program_id(0) == 0)
            def _():
                for r in acc_refs:
                    r[...] = jnp.zeros_like(r)
        body(in_refs, out_refs, acc_refs)

    return pl.pallas_call(
        kern, out_shape=out_shape, grid=(n_rows // blk,), in_specs=in_specs, out_specs=out_specs,
        name=name, compiler_params=_params("arbitrary"),
    )(*[arr for arr, _ in ins])


def _rms(x, g, n=None):
    n = x.shape[-1] if n is None else n
    r = lax.rsqrt(jnp.sum(x * x, axis=-1, keepdims=True) / n + EPS)
    return x * r * g


def _rms_bwd(x, g, dy, n=None):
    n = x.shape[-1] if n is None else n
    r = lax.rsqrt(jnp.sum(x * x, axis=-1, keepdims=True) / n + EPS)
    gdy = dy * g
    dx = r * (gdy - x * ((r * r) * (jnp.sum(gdy * x, axis=-1, keepdims=True) / n)))
    dg = jnp.sum(dy * x * r, axis=0, keepdims=True)
    return dx, dg


def _swap_halves(x):
    lane = lax.broadcasted_iota(jnp.int32, x.shape, 1)
    return jnp.where(lane < ROPE_DIM // 2, pltpu.roll(x, 128 - ROPE_DIM // 2, 1),
                     pltpu.roll(x, ROPE_DIM // 2, 1))


def _rope(n, cos_t, sin_t):
    return n * cos_t + _swap_halves(n) * sin_t


def _rope_bwd(dy, cos_t, sin_t):
    return dy * cos_t - _swap_halves(dy) * sin_t


def _sigmoid(g):
    return 1.0 / (1.0 + jnp.exp(-g))


def _dot_t(a, b):
    return lax.dot_general(a, b, (((1,), (1,)), ((), ())), preferred_element_type=F32)


def _tdot(a, b):
    return lax.dot_general(a, b, (((0,), (0,)), ((), ())), preferred_element_type=F32)


def _dot(a, b):
    return jnp.dot(a, b, preferred_element_type=F32)


def _hs(h, w=HEAD_DIM, base=0):
    return slice(base + h * w, base + (h + 1) * w)


def _mem_head(qm, gq, mk_h, mv_h):
    qb = _rms(qm, gq).astype(BF16)
    s = _dot_t(qb, mk_h) * (HEAD_DIM ** -0.5)
    e = jnp.exp(s - jnp.max(s, axis=-1, keepdims=True))
    p = e / jnp.sum(e, axis=-1, keepdims=True)
    mo = _dot(p.astype(BF16), mv_h)
    return qb, p, mo


def _mix_fwd(att, gates, c0, mk, mv, gq, *, name):
    n_rows = att.shape[0]

    def body(ins, outs, _):
        att_ref, g_ref, mk_ref, mv_ref, gq_ref = ins
        (o_ref,) = outs
        g = g_ref[:, c0:c0 + SB_W]
        o_ref[:, :SB_W] = (att_ref[...] * (g * _sigmoid(g))).astype(BF16)
        for h in range(N_MEM_HEADS):
            qm = g_ref[:, _hs(h, base=c0 + SB_W)]
            gm = g_ref[:, _hs(h, base=c0 + SB_W + MEM_W)]
            _, _, mo = _mem_head(qm, gq_ref[...], mk_ref[:, _hs(h)], mv_ref[:, _hs(h)])
            o_ref[:, _hs(h, base=SB_W)] = (mo * (gm * _sigmoid(gm))).astype(BF16)

    (mixed,) = _rowwise(body, n_rows,
                        [(att, True), (gates, True), (mk, False), (mv, False), (gq, False)],
                        [(D_MODEL, BF16)], name=name)
    return mixed


def _mix_bwd(dmixed, att, gates, c0, mk, mv, gq, *, name):
    n_rows = att.shape[0]
    scale = HEAD_DIM ** -0.5

    def body(ins, outs, accs):
        dm_ref, att_ref, g_ref, mk_ref, mv_ref, gq_ref = ins
        datt_ref, dg_ref = outs
        dmk_ref, dmv_ref, dgq_ref = accs
        g = g_ref[:, c0:c0 + SB_W]
        sg = _sigmoid(g)
        dm = dm_ref[:, :SB_W]
        datt_ref[...] = dm * (g * sg)
        dg_ref[:, :SB_W] = (dm * att_ref[...] * (sg * (1.0 + g * (1.0 - sg)))).astype(BF16)
        for h in range(N_MEM_HEADS):
            qm = g_ref[:, _hs(h, base=c0 + SB_W)]
            gm = g_ref[:, _hs(h, base=c0 + SB_W + MEM_W)]
            mk_h = mk_ref[:, _hs(h)]
            mv_h = mv_ref[:, _hs(h)]
            qb, p, mo = _mem_head(qm, gq_ref[...], mk_h, mv_h)
            sgm = _sigmoid(gm)
            dmh = dm_ref[:, _hs(h, base=SB_W)]
            dmo = dmh * (gm * sgm)
            dg_ref[:, _hs(h, base=SB_W + MEM_W)] = (
                dmh * mo * (sgm * (1.0 + gm * (1.0 - sgm)))).astype(BF16)
            dmo_b = dmo.astype(BF16)
            pb = p.astype(BF16)
            dp = _dot_t(dmo_b, mv_h)
            dmv_ref[:, _hs(h)] += _tdot(pb, dmo_b)
            ds = (p * (dp - jnp.sum(dp * p, axis=-1, keepdims=True)) * scale).astype(BF16)
            dqn = _dot(ds, mk_h)
            dmk_ref[:, _hs(h)] += _tdot(ds, qb)
            dqm, dgq = _rms_bwd(qm, gq_ref[...], dqn)
            dg_ref[:, _hs(h, base=SB_W)] = dqm.astype(BF16)
            dgq_ref[...] += dgq

    return _rowwise(body, n_rows,
                    [(dmixed, True), (att, True), (gates, True), (mk, False), (mv, False), (gq, False)],
                    [(SB_W, F32), (GATE_W, BF16)],
                    [((MEM_LEN, MEM_W), F32), ((MEM_LEN, MEM_W), F32), ((1, HEAD_DIM), F32)],
                    name=name)


def _mem_side_fwd(mem, g_norm, w_kv, g_k, *, tag):
    def norm_body(ins, outs, _):
        outs[0][...] = _rms(ins[0][...], ins[1][...]).astype(BF16)

    (mn,) = _rowwise(norm_body, MEM_LEN, [(mem, True), (g_norm, False)], [(D_MODEL, BF16)],
                     name=f"mem_norm_{tag}")
    mkv = _mm(mn, w_kv, name=f"mem_kv_{tag}")

    def kv_body(ins, outs, _):
        mkv_ref, gk_ref = ins
        mk_ref, mv_ref = outs
        for h in range(N_MEM_HEADS):
            mk_ref[:, _hs(h)] = _rms(mkv_ref[:, _hs(h)], gk_ref[...]).astype(BF16)
        mv_ref[...] = mkv_ref[:, MEM_W:].astype(BF16)

    mk, mv = _rowwise(kv_body, MEM_LEN, [(mkv, True), (g_k, False)], [(MEM_W, BF16), (MEM_W, BF16)],
                      name=f"mem_kv_prep_{tag}")
    return mn, mkv, mk, mv


def _mem_side_bwd(mem, g_norm, w_kv, g_k, mn, mkv, dmk, dmv, *, tag):
    def kv_body(ins, outs, accs):
        mkv_ref, gk_ref, dmk_ref, dmv_ref = ins
        (d_ref,) = outs
        (dgk_ref,) = accs
        for h in range(N_MEM_HEADS):
            dx, dg = _rms_bwd(mkv_ref[:, _hs(h)], gk_ref[...], dmk_ref[:, _hs(h)])
            d_ref[:, _hs(h)] = dx.astype(BF16)
            dgk_ref[...] += dg
        d_ref[:, MEM_W:] = dmv_ref[...].astype(BF16)

    dmkv, dgk = _rowwise(kv_body, MEM_LEN, [(mkv, True), (g_k, False), (dmk, True), (dmv, True)],
                         [(2 * MEM_W, BF16)], [((1, HEAD_DIM), F32)], name=f"mem_kv_prep_bwd_{tag}")
    dmn = _mm(dmkv, w_kv, tb=True, name=f"mem_kv_dx_{tag}")
    dw = _mm(mn, dmkv, ta=True, out_dtype=BF16, name=f"mem_kv_dw_{tag}")

    def norm_body(ins, outs, accs):
        _, dg = _rms_bwd(ins[0][...], ins[1][...], ins[2][...])
        accs[0][...] += dg

    (dgn,) = _rowwise(norm_body, MEM_LEN, [(mem, True), (g_norm, False), (dmn, True)], [],
                      [((1, D_MODEL), F32)], name=f"mem_norm_bwd_{tag}")
    return dw, dgn, dgk


LOG2_E = 1.4426950408889634
SB_Q_SCALE = HEAD_DIM ** -0.5 * LOG2_E


Z2_CAP = 126.0


def _sb_terms(z2):
    zc = jnp.minimum(z2, Z2_CAP)
    w = 1.0 + jnp.exp2(zc)
    return zc, w, jnp.log2(w)


def _chain_modes(s, qb):
    return tuple(None if t < s else ("m" if t == s else "f") for t in range(qb))


def _split_dot(x, tri2):
    hi = x.astype(BF16)
    lo = (x - hi.astype(F32)).astype(BF16)
    return _dot(jnp.concatenate([hi, lo], axis=1), tri2)


def _sb_fwd(qkv, *, name, hp=2):
    seq = qkv.shape[0]
    blk = min(ATT_BLOCK, seq)
    nkb = seq // blk
    qb = _pick(nkb, (2, 1))
    rows = qb * blk
    chains =[(t, s) for t in range(hp) for s in range(qb)]

    def body(q_ref, k_ref, v_ref, o_ref):
        base = pl.program_id(1) * qb
        qs = {(t, s): q_ref[s * blk:(s + 1) * blk, _hs(t)] for t, s in chains}
        row = lax.broadcasted_iota(jnp.int32, (blk, blk), 0)
        col = lax.broadcasted_iota(jnp.int32, (blk, blk), 1)
        after = (row > col).astype(BF16)
        after2 = jnp.concatenate([after, after], axis=0)
        causal = col < row

        def step(j, carry, modes):
            off = pl.multiple_of(j * blk, blk)
            act = [c for c in chains if modes[c[1]]]
            zs, ls = {}, {}
            for c in act:
                zs[c], _, l = _sb_terms(_dot_t(qs[c], k_ref[pl.ds(off, blk), _hs(c[0])]))
                ls[c] = jnp.where(causal, l, 0.0) if modes[c[1]] == "m" else l
            cs = {c: _split_dot(ls[c], after2) for c in act}
            carry = dict(carry)
            for c in act:
                run, acc = carry[c]
                a = jnp.exp2(zs[c] - ls[c] - cs[c] - run)
                if modes[c[1]] == "m":
                    a = jnp.where(causal, a, 0.0)
                acc = acc + _dot(a.astype(BF16), v_ref[pl.ds(off, blk), _hs(c[0])])
                carry[c] = (run + jnp.sum(ls[c], axis=-1, keepdims=True), acc)
            return carry

        init = (jnp.zeros((blk, 1), F32), jnp.zeros((blk, HEAD_DIM), F32))
        carry = {c: init for c in chains}
        for s in reversed(range(qb)):
            carry = step(base + s, carry, _chain_modes(s, qb))
        carry = lax.fori_loop(0, base, lambda jj, c: step(base - 1 - jj, c, ("f",) * qb), carry)
        for t, s in chains:
            o_ref[s * blk:(s + 1) * blk, _hs(t)] = carry[(t, s)][1]

    nh = N_SB_HEADS // hp
    return pl.pallas_call(
        body, out_shape=jax.ShapeDtypeStruct((seq, SB_W), F32), grid=(nh, nkb // qb),
        in_specs=[pl.BlockSpec((rows, hp * HEAD_DIM), lambda h, i: (i, h)),
                  pl.BlockSpec((seq, hp * HEAD_DIM), lambda h, i: (0, nh + h)),
                  pl.BlockSpec((seq, hp * HEAD_DIM), lambda h, i: (0, 2 * nh + h))],
        out_specs=pl.BlockSpec((rows, hp * HEAD_DIM), lambda h, i: (i, h)),
        name=name, compiler_params=_params("parallel", "arbitrary"),
    )(qkv, qkv, qkv)


SB_BWD_GROUP = 4


def _sb_bwd(qkv, out, dout, *, name):
    seq = qkv.shape[0]
    blk = min(ATT_BLOCK, seq)
    nkb = seq // blk
    qb = _pick(nkb, (4, 2, 1))
    rows = qb * blk
    scale = HEAD_DIM ** -0.5

    def body(q_ref, k_ref, v_ref, do_ref, o_ref, dq_ref, dk_ref, dv_ref):
        g = pl.program_id(1)
        base = g * qb

        @pl.when(g == 0)
        def _():
            dk_ref[...] = jnp.zeros_like(dk_ref)
            dv_ref[...] = jnp.zeros_like(dv_ref)

        qs = [q_ref[t * blk:(t + 1) * blk, :] for t in range(qb)]
        dos = [do_ref[t * blk:(t + 1) * blk, :].astype(BF16) for t in range(qb)]
        totals = [jnp.sum(dos[t].astype(F32) * o_ref[t * blk:(t + 1) * blk, :], axis=-1, keepdims=True)
                  for t in range(qb)]
        row = lax.broadcasted_iota(jnp.int32, (blk, blk), 0)
        col = lax.broadcasted_iota(jnp.int32, (blk, blk), 1)
        after = (row > col).astype(BF16)
        after2 = jnp.concatenate([after, after], axis=0)
        from_s = (row >= col).astype(BF16)
        from_s2 = jnp.concatenate([from_s, from_s], axis=0)
        causal = col < row

        def step(j, carry, modes):
            runs, rights, dqs = list(carry[0]), list(carry[1]), list(carry[2])
            off = pl.multiple_of(j * blk, blk)
            kb = k_ref[pl.ds(off, blk), :]
            vb = v_ref[pl.ds(off, blk), :]
            dv_inc = dk_inc = None
            for first in range(0, qb, SB_BWD_GROUP):
                act = [t for t in range(first, min(first + SB_BWD_GROUP, qb)) if modes[t]]
                das = {t: _dot_t(dos[t], vb) for t in act}
                zs, ls, sns = {}, {}, {}
                for t in act:
                    zs[t], w, l = _sb_terms(_dot_t(qs[t], kb))
                    sns[t] = pl.reciprocal(w, approx=True)
                    ls[t] = jnp.where(causal, l, 0.0) if modes[t] == "m" else l
                cs = {t: _split_dot(ls[t], after2) for t in act}
                abs_, des = {}, {}
                for t in act:
                    a = jnp.exp2(zs[t] - ls[t] - cs[t] - runs[t])
                    if modes[t] == "m":
                        a = jnp.where(causal, a, 0.0)
                    abs_[t] = a.astype(BF16)
                    des[t] = abs_[t].astype(F32) * das[t]
                sufs = {t: _split_dot(des[t], from_s2) for t in act}
                for t in act:
                    left = totals[t] - (sufs[t] + rights[t])
                    dz = (des[t] + left) * sns[t] - left
                    if modes[t] == "m":
                        dz = jnp.where(causal, dz, 0.0)
                    dzb = dz.astype(BF16)
                    dqs[t] = dqs[t] + _dot(dzb, kb)
                    inc_v = _tdot(abs_[t], dos[t])
                    inc_k = _tdot(dzb, qs[t])
                    dv_inc = inc_v if dv_inc is None else dv_inc + inc_v
                    dk_inc = inc_k if dk_inc is None else dk_inc + inc_k
                    runs[t] = runs[t] + jnp.sum(ls[t], axis=-1, keepdims=True)
                    rights[t] = rights[t] + jnp.sum(des[t], axis=-1, keepdims=True)
            dv_ref[pl.ds(off, blk), :] += dv_inc
            dk_ref[pl.ds(off, blk), :] += dk_inc
            return tuple(runs), tuple(rights), tuple(dqs)

        zero = (jnp.zeros((blk, 1), F32),) * qb
        carry = (zero, zero, (jnp.zeros((blk, HEAD_DIM), F32),) * qb)
        for s in reversed(range(qb)):
            carry = step(base + s, carry, _chain_modes(s, qb))
        carry = lax.fori_loop(0, base, lambda jj, c: step(base - 1 - jj, c, ("f",) * qb), carry)
        for t in range(qb):
            dq_ref[t * blk:(t + 1) * blk, :] = carry[2][t] * scale

        @pl.when(g == pl.num_programs(1) - 1)
        def _():
            dk_ref[...] = dk_ref[...] * (1.0 / LOG2_E)

    out_sd = jax.ShapeDtypeStruct((seq, SB_W), F32)
    return pl.pallas_call(
        body, out_shape=[out_sd, out_sd, out_sd], grid=(N_SB_HEADS, nkb // qb),
        in_specs=[pl.BlockSpec((rows, HEAD_DIM), lambda h, i: (i, h)),
                  pl.BlockSpec((seq, HEAD_DIM), lambda h, i: (0, N_SB_HEADS + h)),
                  pl.BlockSpec((seq, HEAD_DIM), lambda h, i: (0, 2 * N_SB_HEADS + h)),
                  pl.BlockSpec((rows, HEAD_DIM), lambda h, i: (i, h)),
                  pl.BlockSpec((rows, HEAD_DIM), lambda h, i: (i, h))],
        out_specs=[pl.BlockSpec((rows, HEAD_DIM), lambda h, i: (i, h)),
                   pl.BlockSpec((seq, HEAD_DIM), lambda h, i: (0, h)),
                   pl.BlockSpec((seq, HEAD_DIM), lambda h, i: (0, h))],
        name=name, compiler_params=_params("parallel", "arbitrary"),
    )(qkv, qkv, qkv, dout, out)


MLA_SCALE = (HEAD_DIM + ROPE_DIM) ** -0.5
MLA_Q_SCALE = MLA_SCALE * LOG2_E


def _mla_fwd(q_cat, k_cat, v, *, name, hp=2):
    seq = q_cat.shape[0]
    blk = min(ATT_BLOCK, seq)
    nkb = seq // blk
    qb = _pick(nkb, (2, 1))
    rows = qb * blk
    chains = [(t, s) for t in range(hp) for s in range(qb)]

    def body(q_ref, k_ref, v_ref, o_ref, lse_ref):
        base = pl.program_id(1) * qb
        qs = {(t, s): q_ref[s * blk:(s + 1) * blk, t * CAT_W:(t + 1) * CAT_W] for t, s in chains}
        row = lax.broadcasted_iota(jnp.int32, (blk, blk), 0)
        col = lax.broadcasted_iota(jnp.int32, (blk, blk), 1)
        causal = col <= row

        def step(j, carry, modes):
            off = pl.multiple_of(j * blk, blk)
            act = [c for c in chains if modes[c[1]]]
            ss = {c: _dot_t(qs[c], k_ref[pl.ds(off, blk), c[0] * CAT_W:(c[0] + 1) * CAT_W]) for c in act}
            carry = dict(carry)
            for c in act:
                m, l, acc = carry[c]
                s = ss[c]
                if modes[c[1]] == "m":
                    s = jnp.where(causal, s, -jnp.inf)
                m_new = jnp.maximum(m, jnp.max(s, axis=-1, keepdims=True))
                p = jnp.exp2(s - m_new)
                alpha = jnp.exp2(m - m_new)
                l = alpha * l + jnp.sum(p, axis=-1, keepdims=True)
                acc = alpha * acc + _dot(p.astype(BF16), v_ref[pl.ds(off, blk), _hs(c[0])])
                carry[c] = (m_new, l, acc)
            return carry

        init = (jnp.full((blk, 1), -jnp.inf, F32), jnp.zeros((blk, 1), F32),
                jnp.zeros((blk, HEAD_DIM), F32))
        carry = {c: init for c in chains}
        carry = lax.fori_loop(0, base, lambda j, c: step(j, c, ("f",) * qb), carry)
        for s in range(qb):
            carry = step(base + s, carry, _chain_modes(s, qb))
        for t, s in chains:
            m, l, acc = carry[(t, s)]
            o_ref[s * blk:(s + 1) * blk, _hs(t)] = acc / l
            lse_ref[s * blk:(s + 1) * blk, _hs(t)] = jnp.broadcast_to(
                (m + jnp.log2(l)) * (1.0 / LOG2_E), (blk, HEAD_DIM))

    out = jax.ShapeDtypeStruct((seq, MLA_W), F32)
    return pl.pallas_call(
        body, out_shape=[out, out], grid=(N_MLA_HEADS // hp, nkb // qb),
        in_specs=[pl.BlockSpec((rows, hp * CAT_W), lambda h, i: (i, h)),
                  pl.BlockSpec((seq, hp * CAT_W), lambda h, i: (0, h)),
                  pl.BlockSpec((seq, hp * HEAD_DIM), lambda h, i: (0, h))],
        out_specs=[pl.BlockSpec((rows, hp * HEAD_DIM), lambda h, i: (i, h)),
                   pl.BlockSpec((rows, hp * HEAD_DIM), lambda h, i: (i, h))],
        name=name, compiler_params=_params("parallel", "arbitrary"),
    )(q_cat, k_cat, v)


def _mla_bwd(q_cat, k_cat, v, out, lse, dout, *, name):
    seq = q_cat.shape[0]
    blk = min(ATT_BLOCK, seq)
    nkb = seq // blk
    qb = _pick(nkb, (4, 2, 1))
    rows = qb * blk

    def body(q_ref, k_ref, v_ref, o_ref, lse_ref, do_ref, dq_ref, dk_ref, dv_ref):
        g = pl.program_id(1)
        base = g * qb

        @pl.when(g == 0)
        def _():
            dk_ref[...] = jnp.zeros_like(dk_ref)
            dv_ref[...] = jnp.zeros_like(dv_ref)

        qs, dobs, deltas, lses = [], [], [], []
        for t in range(qb):
            rs = slice(t * blk, (t + 1) * blk)
            do = do_ref[rs, :]
            qs.append(q_ref[rs, :])
            dobs.append(do.astype(BF16))
            deltas.append(jnp.sum(do * o_ref[rs, :], axis=-1, keepdims=True))
            lses.append(lse_ref[rs, :1] * LOG2_E)
        row = lax.broadcasted_iota(jnp.int32, (blk, blk), 0)
        col = lax.broadcasted_iota(jnp.int32, (blk, blk), 1)
        causal = col <= row

        def step(j, dqs, modes):
            off = pl.multiple_of(j * blk, blk)
            kb = k_ref[pl.ds(off, blk), :]
            vb = v_ref[pl.ds(off, blk), :]
            act = [t for t in range(qb) if modes[t]]
            ss = {t: _dot_t(qs[t], kb) for t in act}
            dps = {t: _dot_t(dobs[t], vb) for t in act}
            dqs = list(dqs)
            dv_inc = dk_inc = None
            for t in act:
                p = jnp.exp2(ss[t] - lses[t])
                if modes[t] == "m":
                    p = jnp.where(causal, p, 0.0)
                ds = (p * (dps[t] - deltas[t])).astype(BF16)
                inc_v = _tdot(p.astype(BF16), dobs[t])
                inc_k = _tdot(ds, qs[t])
                dv_inc = inc_v if dv_inc is None else dv_inc + inc_v
                dk_inc = inc_k if dk_inc is None else dk_inc + inc_k
                dqs[t] = dqs[t] + _dot(ds, kb)
            dv_ref[pl.ds(off, blk), :] += dv_inc
            dk_ref[pl.ds(off, blk), :] += dk_inc
            return tuple(dqs)

        dqs = (jnp.zeros((blk, CAT_W), F32),) * qb
        dqs = lax.fori_loop(0, base, lambda j, c: step(j, c, ("f",) * qb), dqs)
        for s in range(qb):
            modes = tuple(None if t < s else ("m" if t == s else "f") for t in range(qb))
            dqs = step(base + s, dqs, modes)
        for t in range(qb):
            dq_ref[t * blk:(t + 1) * blk, :] = dqs[t] * MLA_SCALE

        @pl.when(g == pl.num_programs(1) - 1)
        def _():
            dk_ref[...] = dk_ref[...] * (1.0 / LOG2_E)

    return pl.pallas_call(
        body,
        out_shape=[jax.ShapeDtypeStruct((seq, N_MLA_HEADS * CAT_W), F32),
                   jax.ShapeDtypeStruct((seq, N_MLA_HEADS * CAT_W), F32),
                   jax.ShapeDtypeStruct((seq, MLA_W), F32)],
        grid=(N_MLA_HEADS, nkb // qb),
        in_specs=[pl.BlockSpec((rows, CAT_W), lambda h, i: (i, h)),
                  pl.BlockSpec((seq, CAT_W), lambda h, i: (0, h)),
                  pl.BlockSpec((seq, HEAD_DIM), lambda h, i: (0, h)),
                  pl.BlockSpec((rows, HEAD_DIM), lambda h, i: (i, h)),
                  pl.BlockSpec((rows, HEAD_DIM), lambda h, i: (i, h)),
                  pl.BlockSpec((rows, HEAD_DIM), lambda h, i: (i, h))],
        out_specs=[pl.BlockSpec((rows, CAT_W), lambda h, i: (i, h)),
                   pl.BlockSpec((seq, CAT_W), lambda h, i: (0, h)),
                   pl.BlockSpec((seq, HEAD_DIM), lambda h, i: (0, h))],
        name=name, compiler_params=_params("parallel", "arbitrary"),
    )(q_cat, k_cat, v, out, lse, dout)


def _local_step(x, mem, positions, target, w, g):
    seq = x.shape[0]
    inv_freq = jnp.power(ROPE_THETA, -jnp.arange(0, ROPE_DIM, 2, dtype=F32) / ROPE_DIM)
    ang = positions.astype(F32)[:, None] * inv_freq
    cos, sin = jnp.cos(ang), jnp.sin(ang)
    lane_pad = jnp.zeros((seq, HEAD_DIM - ROPE_DIM), F32)
    cos_t = jnp.concatenate([cos, cos, lane_pad], axis=1)
    sin_t = jnp.concatenate([-sin, sin, lane_pad], axis=1)
    gain_pad = jnp.zeros((1, HEAD_DIM - ROPE_DIM), F32)
    g_k_rope = jnp.concatenate([g["g_k_rope"], gain_pad], axis=1)
    g_q_rope = jnp.concatenate([g["b_g_q_rope"], gain_pad], axis=1)

    def norm_to_bf16(src, gain, name):
        def body(ins, outs, _):
            outs[0][...] = _rms(ins[0][...], ins[1][...]).astype(BF16)
        return _rowwise(body, seq, [(src, True), (gain, False)], [(src.shape[1], BF16)], name=name)[0]

    h_a = norm_to_bf16(x, g["a_norm"], "a_norm_fwd")
    qkv = _mm(h_a, w["a_in_qkv"], out_dtype=BF16, scale_cols=(SB_W, SB_Q_SCALE), name="a_in_qkv")
    gr = _mm(h_a, w["a_in_gate"], name="a_in_gate")
    sb = _sb_fwd(qkv, name="sb_fwd")
    mem0 = _mem_side_fwd(mem, g["mem_norm"][0:1], w["mem_kv"][0], g["g_mem_k"][0:1], tag="a")
    mixed_a = _mix_fwd(sb, gr, 0, mem0[2], mem0[3], g["g_mem_q"][0:1], name="a_mix_fwd")
    x1 = _mm(mixed_a, w["a_out"], res=x, name="a_out")

    def norms2_body(ins, outs, _):
        xv = ins[0][...]
        outs[0][...] = _rms(xv, ins[1][...]).astype(BF16)
        outs[1][...] = _rms(xv, ins[2][...]).astype(BF16)

    h_kv, h_b = _rowwise(norms2_body, seq, [(x1, True), (g["kv_norm"], False), (g["b_norm"], False)],
                         [(D_MODEL, BF16), (D_MODEL, BF16)], name="kv_b_norm_fwd")
    ckr = _mm(h_kv, w["dkv"], name="dkv")

    def ckr_body(ins, outs, _):
        ckr_ref, gc_ref, gr_ref, c_ref, s_ref = ins
        outs[0][...] = _rms(ckr_ref[:, :KV_LORA], gc_ref[...]).astype(BF16)
        kr = _rms(ckr_ref[:, KV_LORA:], gr_ref[...], n=ROPE_DIM)
        outs[1][...] = _rope(kr, c_ref[...], s_ref[...]).astype(BF16)

    c_n, k_r = _rowwise(ckr_body, seq,
                        [(ckr, True), (g["g_ckv"], False), (g_k_rope, False), (cos_t, True), (sin_t, True)],
                        [(KV_LORA, BF16), (HEAD_DIM, BF16)], name="ckv_prep_fwd")
    kv = _mm(c_n, w["ukv"], name="ukv")

    def kcat_body(ins, outs, _):
        kv_ref, kr_ref, gk_ref = ins
        kc_ref, v_ref = outs
        for h in range(N_MLA_HEADS):
            kc_ref[:, h * CAT_W:h * CAT_W + HEAD_DIM] = _rms(
                kv_ref[:, h * CAT_W:h * CAT_W + HEAD_DIM], gk_ref[...]).astype(BF16)
            kc_ref[:, h * CAT_W + HEAD_DIM:(h + 1) * CAT_W] = kr_ref[...]
            v_ref[:, _hs(h)] = kv_ref[:, h * CAT_W + HEAD_DIM:(h + 1) * CAT_W].astype(BF16)

    k_cat, v_mla = _rowwise(kcat_body, seq, [(kv, True), (k_r, True), (g["g_k_nope"], False)],
                            [(N_MLA_HEADS * CAT_W, BF16), (MLA_W, BF16)], name="k_prep_fwd")

    p2 = _mm(h_b, w["b_in"], name="b_in")

    def qlat_body(ins, outs, _):
        outs[0][...] = _rms(ins[0][:, :Q_LORA], ins[1][...]).astype(BF16)

    (q_l,) = _rowwise(qlat_body, seq, [(p2, True), (g["b_g_q_lat"], False)], [(Q_LORA, BF16)],
                      name="q_lat_norm_fwd")
    q_up = _mm(q_l, w["uq"], name="uq")

    def qcat_body(ins, outs, _):
        q_ref, gn_ref, gr_ref, c_ref, s_ref = ins
        (o_ref,) = outs
        for h in range(N_MLA_HEADS):
            o_ref[:, h * CAT_W:h * CAT_W + HEAD_DIM] = (MLA_Q_SCALE * _rms(
                q_ref[:, h * CAT_W:h * CAT_W + HEAD_DIM], gn_ref[...])).astype(BF16)
            qr = _rms(q_ref[:, h * CAT_W + HEAD_DIM:(h + 1) * CAT_W], gr_ref[...], n=ROPE_DIM)
            o_ref[:, h * CAT_W + HEAD_DIM:(h + 1) * CAT_W] = (
                MLA_Q_SCALE * _rope(qr, c_ref[...], s_ref[...])).astype(BF16)

    (q_cat,) = _rowwise(qcat_body, seq,
                        [(q_up, True), (g["b_g_q_nope"], False), (g_q_rope, False), (cos_t, True), (sin_t, True)],
                        [(N_MLA_HEADS * CAT_W, BF16)], name="q_prep_fwd")
    att, lse = _mla_fwd(q_cat, k_cat, v_mla, name="mla_fwd")
    mem1 = _mem_side_fwd(mem, g["mem_norm"][1:2], w["mem_kv"][1], g["g_mem_k"][1:2], tag="b")
    mixed_b = _mix_fwd(att, p2, Q_LORA, mem1[2], mem1[3], g["g_mem_q"][1:2], name="b_mix_fwd")
    y = _mm(mixed_b, w["b_out"], res=x1, name="b_out")

    def loss_body(ins, outs, accs):
        diff = ins[0][...] - ins[1][...]
        outs[0][...] = diff / D_MODEL
        col = jnp.sum(diff * diff, axis=0, keepdims=True)
        part = col[:, :HEAD_DIM]
        for c in range(1, D_MODEL // HEAD_DIM):
            part = part + col[:, _hs(c)]
        accs[0][...] += part * (0.5 / D_MODEL)

    dy, loss_part = _rowwise(loss_body, seq, [(y, True), (target, True)], [(D_MODEL, F32)],
                             [((1, HEAD_DIM), F32)], name="loss")

    gw, gg = {}, {}
    dmixed_b = _mm(dy, w["b_out"], tb=True, name="b_out_dx")
    gw["b_out"] = _mm(mixed_b, dy, ta=True, out_dtype=BF16, name="b_out_dw")
    datt, dgate_b, dmk1, dmv1, gq1 = _mix_bwd(dmixed_b, att, p2, Q_LORA, mem1[2], mem1[3],
                                              g["g_mem_q"][1:2], name="b_mix_bwd")
    dq_cat, dk_cat, dv_mla = _mla_bwd(q_cat, k_cat, v_mla, att, lse, datt, name="mla_bwd")

    def qcat_bwd_body(ins, outs, accs):
        q_ref, dq_ref, gn_ref, gr_ref, c_ref, s_ref = ins
        (o_ref,) = outs
        dgn_ref, dgr_ref = accs
        for h in range(N_MLA_HEADS):
            dx, dg = _rms_bwd(q_ref[:, h * CAT_W:h * CAT_W + HEAD_DIM], gn_ref[...],
                              dq_ref[:, h * CAT_W:h * CAT_W + HEAD_DIM])
            o_ref[:, h * CAT_W:h * CAT_W + HEAD_DIM] = dx.astype(BF16)
            dgn_ref[...] += dg
            dn = _rope_bwd(dq_ref[:, h * CAT_W + HEAD_DIM:(h + 1) * CAT_W], c_ref[...], s_ref[...])
            dx, dg = _rms_bwd(q_ref[:, h * CAT_W + HEAD_DIM:(h + 1) * CAT_W], gr_ref[...], dn, n=ROPE_DIM)
            o_ref[:, h * CAT_W + HEAD_DIM:(h + 1) * CAT_W] = dx.astype(BF16)
            dgr_ref[...] += dg

    dq_up, gg["b_g_q_nope"], dgqr = _rowwise(
        qcat_bwd_body, seq,
        [(q_up, True), (dq_cat, True), (g["b_g_q_nope"], False), (g_q_rope, False), (cos_t, True), (sin_t, True)],
        [(N_MLA_HEADS * CAT_W, BF16)], [((1, HEAD_DIM), F32), ((1, HEAD_DIM), F32)], name="q_prep_bwd")
    gg["b_g_q_rope"] = dgqr
    dq_l = _mm(dq_up, w["uq"], tb=True, name="uq_dx")
    gw["uq"] = _mm(q_l, dq_up, ta=True, out_dtype=BF16, n_split=N_CHIPS, name="uq_dw")

    def qlat_bwd_body(ins, outs, accs):
        p2_ref, dql_ref, dgate_ref, gl_ref = ins
        dx, dg = _rms_bwd(p2_ref[:, :Q_LORA], gl_ref[...], dql_ref[...])
        outs[0][:, :Q_LORA] = dx.astype(BF16)
        outs[0][:, Q_LORA:] = dgate_ref[...]
        accs[0][...] += dg

    dp2, gg["b_g_q_lat"] = _rowwise(
        qlat_bwd_body, seq, [(p2, True), (dq_l, True), (dgate_b, True), (g["b_g_q_lat"], False)],
        [(Q_LORA + GATE_W, BF16)], [((1, Q_LORA), F32)], name="q_lat_norm_bwd")
    dh_b = _mm(dp2, w["b_in"], tb=True, name="b_in_dx")
    gw["b_in"] = _mm(h_b, dp2, ta=True, out_dtype=BF16, n_split=N_CHIPS, name="b_in_dw")

    def kcat_bwd_body(ins, outs, accs):
        kv_ref, dkc_ref, dv_ref, gk_ref = ins
        dkv_ref, dkr_ref = outs
        (dgk_ref,) = accs
        dkr = jnp.zeros(dkr_ref.shape, F32)
        for h in range(N_MLA_HEADS):
            dx, dg = _rms_bwd(kv_ref[:, h * CAT_W:h * CAT_W + HEAD_DIM], gk_ref[...],
                              dkc_ref[:, h * CAT_W:h * CAT_W + HEAD_DIM])
            dkv_ref[:, h * CAT_W:h * CAT_W + HEAD_DIM] = dx.astype(BF16)
            dgk_ref[...] += dg
            dkv_ref[:, h * CAT_W + HEAD_DIM:(h + 1) * CAT_W] = dv_ref[:, _hs(h)].astype(BF16)
            dkr = dkr + dkc_ref[:, h * CAT_W + HEAD_DIM:(h + 1) * CAT_W]
        dkr_ref[...] = dkr

    dkv, dk_r, gg["g_k_nope"] = _rowwise(
        kcat_bwd_body, seq, [(kv, True), (dk_cat, True), (dv_mla, True), (g["g_k_nope"], False)],
        [(N_MLA_HEADS * CAT_W, BF16), (HEAD_DIM, F32)], [((1, HEAD_DIM), F32)], name="k_prep_bwd")
    dc_n = _mm(dkv, w["ukv"], tb=True, name="ukv_dx")
    gw["ukv"] = _mm(c_n, dkv, ta=True, out_dtype=BF16, n_split=N_CHIPS, name="ukv_dw")

    def ckr_bwd_body(ins, outs, accs):
        ckr_ref, dcn_ref, dkr_ref, gc_ref, gr_ref, c_ref, s_ref = ins
        dx, dg = _rms_bwd(ckr_ref[:, :KV_LORA], gc_ref[...], dcn_ref[...])
        outs[0][:, :KV_LORA] = dx.astype(BF16)
        accs[0][...] += dg
        dn = _rope_bwd(dkr_ref[...], c_ref[...], s_ref[...])
        dx, dg = _rms_bwd(ckr_ref[:, KV_LORA:], gr_ref[...], dn, n=ROPE_DIM)
        outs[0][:, KV_LORA:] = dx.astype(BF16)
        accs[1][...] += dg

    dckr, gg["g_ckv"], gg["g_k_rope"] = _rowwise(
        ckr_bwd_body, seq,
        [(ckr, True), (dc_n, True), (dk_r, True), (g["g_ckv"], False), (g_k_rope, False),
         (cos_t, True), (sin_t, True)],
        [(KV_LORA + HEAD_DIM, BF16)], [((1, KV_LORA), F32), ((1, HEAD_DIM), F32)], name="ckv_prep_bwd")
    dh_kv = _mm(dckr, w["dkv"], tb=True, name="dkv_dx")
    gw["dkv"] = _mm(h_kv, dckr, ta=True, out_dtype=BF16, name="dkv_dw")

    def norms2_bwd_body(ins, outs, accs):
        x_ref, dy_ref, dhk_ref, dhb_ref, gk_ref, gb_ref = ins
        xv = x_ref[...]
        dxk, dgk = _rms_bwd(xv, gk_ref[...], dhk_ref[...])
        dxb, dgb = _rms_bwd(xv, gb_ref[...], dhb_ref[...])
        outs[0][...] = dy_ref[...] + dxk + dxb
        accs[0][...] += dgk
        accs[1][...] += dgb

    dx1, gg["kv_norm"], gg["b_norm"] = _rowwise(
        norms2_bwd_body, seq,
        [(x1, True), (dy, True), (dh_kv, True), (dh_b, True), (g["kv_norm"], False), (g["b_norm"], False)],
        [(D_MODEL, F32)], [((1, D_MODEL), F32), ((1, D_MODEL), F32)], name="kv_b_norm_bwd")

    dmixed_a = _mm(dx1, w["a_out"], tb=True, name="a_out_dx")
    gw["a_out"] = _mm(mixed_a, dx1, ta=True, out_dtype=BF16, name="a_out_dw")
    dsb, dgate_a, dmk0, dmv0, gq0 = _mix_bwd(dmixed_a, sb, gr, 0, mem0[2], mem0[3],
                                             g["g_mem_q"][0:1], name="a_mix_bwd")
    dq, dk, dv = _sb_bwd(qkv, sb, dsb, name="sb_bwd")
    dp_a = jnp.concatenate([dq.astype(BF16), dk.astype(BF16), dv.astype(BF16), dgate_a], axis=1)
    dh_a = _mm(dp_a, w["a_in"], tb=True, name="a_in_dx")
    gw["a_in"] = _mm(h_a, dp_a, ta=True, out_dtype=BF16, n_split=N_CHIPS, name="a_in_dw")

    def norm_a_bwd_body(ins, outs, accs):
        dx, dg = _rms_bwd(ins[0][...], ins[3][...], ins[2][...])
        outs[0][...] = ins[1][...] + dx
        accs[0][...] += dg

    grad_x, gg["a_norm"] = _rowwise(
        norm_a_bwd_body, seq, [(x, True), (dx1, True), (dh_a, True), (g["a_norm"], False)],
        [(D_MODEL, F32)], [((1, D_MODEL), F32)], name="a_norm_bwd")

    dw0, dgn0, dgk0 = _mem_side_bwd(mem, g["mem_norm"][0:1], w["mem_kv"][0], g["g_mem_k"][0:1],
                                    mem0[0], mem0[1], dmk0, dmv0, tag="a")
    dw1, dgn1, dgk1 = _mem_side_bwd(mem, g["mem_norm"][1:2], w["mem_kv"][1], g["g_mem_k"][1:2],
                                    mem1[0], mem1[1], dmk1, dmv1, tag="b")
    gw["mem_kv"] = (dw0, dw1)
    gg["mem_norm"] = jnp.concatenate([dgn0, dgn1], axis=0)
    gg["g_mem_q"] = jnp.concatenate([gq0, gq1], axis=0)
    gg["g_mem_k"] = jnp.concatenate([dgk0, dgk1], axis=0)
    return loss_part, grad_x, gw, gg


HBM_SPEC = pl.BlockSpec(memory_space=pl.ANY)


def _other_chips():
    x, y = lax.axis_index("x"), lax.axis_index("y")
    return [(1 - x, y), (x, 1 - y), (1 - x, 1 - y)]


def _allgather_chips(shards):
    n = len(shards)
    split = [s.shape[0] % 32 == 0 for s in shards]

    def body(*refs):
        ins, outs = refs[:n], refs[n:2 * n]
        send, recv, fsend, frecv, loc = refs[2 * n:]
        x, y, c = lax.axis_index("x"), lax.axis_index("y"), lax.axis_index("c")
        me = 2 * x + y
        chips = _other_chips()

        def part(ref, wi):
            if not split[wi]:
                return ref
            half = shards[wi].shape[0] // 2
            return ref.at[pl.ds(pl.multiple_of(c * half, 16), half)]

        def ici(wi, k, src_chip, to):
            return pltpu.make_async_remote_copy(
                src_ref=part(ins[wi], wi), dst_ref=part(outs[wi].at[src_chip], wi),
                send_sem=send.at[wi, k], recv_sem=recv.at[wi, k], device_id=to, device_id_type=MESH)

        def d2d(wi, k, src_chip):
            rows = part(outs[wi].at[src_chip], wi)
            return pltpu.make_async_remote_copy(
                src_ref=rows, dst_ref=rows, send_sem=fsend.at[wi, k], recv_sem=frecv.at[wi, k],
                device_id=(x, y, 1 - c), device_id_type=MESH)

        started = []
        for wi in range(n):
            own = pltpu.make_async_copy(ins[wi], outs[wi].at[me], loc.at[wi])
            own.start()
            started.append(own)
            for k, (tx, ty) in enumerate(chips):
                ici(wi, k, me, (tx, ty, c)).start()
        for wi in range(n):
            for k, (tx, ty) in enumerate(chips):
                landed = ici(wi, k, 2 * tx + ty, (tx, ty, c))
                landed.wait_recv()
                if split[wi]:
                    d2d(wi, k, 2 * tx + ty).start()
        for wi in range(n):
            for k, (tx, ty) in enumerate(chips):
                ici(wi, k, me, (tx, ty, c)).wait_send()
                if split[wi]:
                    fwd = d2d(wi, k, 2 * tx + ty)
                    fwd.wait_send()
                    fwd.wait_recv()
        for own in started:
            own.wait()

    return pl.pallas_call(
        body, out_shape=[jax.ShapeDtypeStruct((N_CHIPS,) + s.shape, s.dtype) for s in shards],
        in_specs=[HBM_SPEC] * n, out_specs=[HBM_SPEC] * n,
        scratch_shapes=[pltpu.SemaphoreType.DMA((n, 3)), pltpu.SemaphoreType.DMA((n, 3)),
                        pltpu.SemaphoreType.DMA((n, 3)), pltpu.SemaphoreType.DMA((n, 3)),
                        pltpu.SemaphoreType.DMA((n,))],
        name="allgather_weights",
    )(*shards)


def _scatter_to_chips(grads):
    n = len(grads)

    def body(*refs):
        ins, outs = refs[:n], refs[n:2 * n]
        send, recv, loc = refs[2 * n:]
        c = lax.axis_index("c")
        me = 2 * lax.axis_index("x") + lax.axis_index("y")
        copies = []
        for wi in range(n):
            own = pltpu.make_async_copy(ins[wi].at[me], outs[wi].at[3], loc.at[wi])
            own.start()
            copies.append(own)
            for k, (tx, ty) in enumerate(_other_chips()):
                cp = pltpu.make_async_remote_copy(
                    src_ref=ins[wi].at[2 * tx + ty], dst_ref=outs[wi].at[k], send_sem=send.at[wi, k],
                    recv_sem=recv.at[wi, k], device_id=(tx, ty, c), device_id_type=MESH)
                cp.start()
                copies.append(cp)
        for cp in copies:
            cp.wait()

    return pl.pallas_call(
        body, out_shape=[jax.ShapeDtypeStruct(s.shape, s.dtype) for s in grads],
        in_specs=[HBM_SPEC] * n, out_specs=[HBM_SPEC] * n,
        scratch_shapes=[pltpu.SemaphoreType.DMA((n, 3)), pltpu.SemaphoreType.DMA((n, 3)),
                        pltpu.SemaphoreType.DMA((n,))],
        name="scatter_grads",
    )(*grads)


def _halve_with_sibling(grads):
    n = len(grads)
    n_slots = grads[0].shape[0]

    def body(*refs):
        ins, own, got = refs[:n], refs[n:2 * n], refs[2 * n:3 * n]
        send, recv, loc = refs[3 * n:]
        c = lax.axis_index("c")
        sib = (lax.axis_index("x"), lax.axis_index("y"), 1 - c)
        copies = []
        for wi in range(n):
            half = grads[wi].shape[1] // 2
            for s in range(n_slots):
                mine = ins[wi].at[s, pl.ds(pl.multiple_of(c * half, 16), half)]
                theirs = ins[wi].at[s, pl.ds(pl.multiple_of((1 - c) * half, 16), half)]
                keep = pltpu.make_async_copy(mine, own[wi].at[s], loc.at[wi, s])
                give = pltpu.make_async_remote_copy(
                    src_ref=theirs, dst_ref=got[wi].at[s], send_sem=send.at[wi, s], recv_sem=recv.at[wi, s],
                    device_id=sib, device_id_type=MESH)
                keep.start()
                give.start()
                copies += [keep, give]
        for cp in copies:
            cp.wait()

    halves = [jax.ShapeDtypeStruct((s.shape[0], s.shape[1] // 2) + s.shape[2:], s.dtype) for s in grads]
    res = pl.pallas_call(
        body, out_shape=halves + halves, in_specs=[HBM_SPEC] * n, out_specs=[HBM_SPEC] * (2 * n),
        scratch_shapes=[pltpu.SemaphoreType.DMA((n, n_slots)), pltpu.SemaphoreType.DMA((n, n_slots)),
                        pltpu.SemaphoreType.DMA((n, n_slots))],
        name="halve_grads_with_sibling",
    )(*grads)
    return res[:n], res[n:]


def _gather_halves(parts):
    n = len(parts)

    def body(*refs):
        ins, outs = refs[:n], refs[n:2 * n]
        send, recv, loc = refs[2 * n:]
        c = lax.axis_index("c")
        sib = (lax.axis_index("x"), lax.axis_index("y"), 1 - c)
        copies = []
        for wi in range(n):
            keep = pltpu.make_async_copy(ins[wi], outs[wi].at[c], loc.at[wi])
            give = pltpu.make_async_remote_copy(
                src_ref=ins[wi], dst_ref=outs[wi].at[c], send_sem=send.at[wi], recv_sem=recv.at[wi],
                device_id=sib, device_id_type=MESH)
            keep.start()
            give.start()
            copies += [keep, give]
        for cp in copies:
            cp.wait()

    return pl.pallas_call(
        body, out_shape=[jax.ShapeDtypeStruct((2,) + s.shape, s.dtype) for s in parts],
        in_specs=[HBM_SPEC] * n, out_specs=[HBM_SPEC] * n,
        scratch_shapes=[pltpu.SemaphoreType.DMA((n,)), pltpu.SemaphoreType.DMA((n,)),
                        pltpu.SemaphoreType.DMA((n,))],
        name="gather_grad_halves",
    )(*parts)


def _allreduce_small(vec, loss_row):
    rows = vec.shape[0]

    def body(v_ref, o_ref, buf, send, recv):
        x, y, c = lax.axis_index("x"), lax.axis_index("y"), lax.axis_index("c")
        me = 4 * x + 2 * y + c
        buf[me] = v_ref[...]
        copies = []
        for r in range(1, N_DEV):
            peer = (x ^ ((r >> 2) & 1), y ^ ((r >> 1) & 1), c ^ (r & 1))
            cp = pltpu.make_async_remote_copy(
                src_ref=v_ref, dst_ref=buf.at[me], send_sem=send.at[r - 1], recv_sem=recv.at[r - 1],
                device_id=peer, device_id_type=MESH)
            cp.start()
            copies.append(cp)
        for cp in copies:
            cp.wait()
        total = buf[0]
        for d in range(1, N_DEV):
            total = total + buf[d]
        o_ref[...] = total
        o_ref[loss_row:loss_row + 1, :] = jnp.broadcast_to(
            jnp.sum(total[loss_row:loss_row + 1, :], axis=-1, keepdims=True), (1, HEAD_DIM))

    return pl.pallas_call(
        body, out_shape=jax.ShapeDtypeStruct(vec.shape, F32),
        in_specs=[pl.BlockSpec(memory_space=pltpu.VMEM)], out_specs=pl.BlockSpec(memory_space=pltpu.VMEM),
        scratch_shapes=[pltpu.VMEM((N_DEV, rows, HEAD_DIM), F32),
                        pltpu.SemaphoreType.DMA((N_DEV - 1,)), pltpu.SemaphoreType.DMA((N_DEV - 1,))],
        name="allreduce_gains",
    )(vec)


def _pair_sum(a, b, *, name):
    slots, rows, width = a.shape
    blk = _pick(rows, (256, 128, 64, 32, 16))

    def body(a_ref, b_ref, o_ref):
        o_ref[...] = (a_ref[...].astype(F32) + b_ref[...].astype(F32)).astype(BF16)

    spec = pl.BlockSpec((None, blk, width), lambda s, i: (s, i, 0))
    return pl.pallas_call(
        body, out_shape=jax.ShapeDtypeStruct(a.shape, BF16), grid=(slots, rows // blk),
        in_specs=[spec, spec], out_specs=spec, name=name, compiler_params=_params("parallel", "parallel"),
    )(a, b)


def _sum_slots(r, *, name):
    _, rows, width = r.shape
    blk = _pick(rows, (256, 128, 64, 32, 16, 8))

    def body(r_ref, o_ref):
        o_ref[...] = ((r_ref[3].astype(F32) + r_ref[0].astype(F32)) + r_ref[1].astype(F32)) + r_ref[2].astype(F32)

    return pl.pallas_call(
        body, out_shape=jax.ShapeDtypeStruct((rows, width), F32), grid=(rows // blk,),
        in_specs=[pl.BlockSpec((N_CHIPS, blk, width), lambda i: (0, i, 0))],
        out_specs=pl.BlockSpec((blk, width), lambda i: (i, 0)),
        name=name, compiler_params=_params("parallel"),
    )(r)


def _adamw(wgt, grads, m, v, *, name):
    rows, width = wgt.shape
    blk = _pick(rows, (256, 128, 64, 32, 16, 8))
    n_g = len(grads)

    def body(*refs):
        w_ref, m_ref, v_ref = refs[0], refs[1 + n_g], refs[2 + n_g]
        g_out, d_out, m_out, v_out = refs[3 + n_g:]
        grad = refs[1][...]
        for t in range(1, n_g):
            grad = grad + refs[1 + t][...]
        m_new = ADAM_B1 * m_ref[...] + (1.0 - ADAM_B1) * grad
        v_new = ADAM_B2 * v_ref[...] + (1.0 - ADAM_B2) * (grad * grad)
        m_hat = m_new / (1.0 - ADAM_B1 ** ADAM_STEP)
        v_hat = v_new / (1.0 - ADAM_B2 ** ADAM_STEP)
        g_out[...] = grad
        d_out[...] = -ADAM_LR * (m_hat / (jnp.sqrt(v_hat) + ADAM_EPS) + ADAM_WD * w_ref[...])
        m_out[...] = m_new
        v_out[...] = v_new

    spec = pl.BlockSpec((blk, width), lambda i: (i, 0))
    out = jax.ShapeDtypeStruct((rows, width), F32)
    return pl.pallas_call(
        body, out_shape=[out] * 4, grid=(rows // blk,), in_specs=[spec] * (3 + n_g),
        out_specs=[spec] * 4, name=name, compiler_params=_params("parallel"),
    )(wgt, *grads, m, v)


_SMALL = (("a_norm", 2048), ("kv_norm", 2048), ("g_ckv", 512), ("g_k_nope", 128), ("g_k_rope", 64),
          ("b_norm", 2048), ("b_g_q_lat", 512), ("b_g_q_nope", 128), ("b_g_q_rope", 64),
          ("mem_norm", 4096), ("g_mem_q", 256), ("g_mem_k", 256))


def _lanes(n):
    return -(-n // HEAD_DIM) * HEAD_DIM


def _pack_rows(pieces, pad_rows_to=8):
    flat = jnp.concatenate(pieces, axis=1)
    rows = flat.shape[1] // HEAD_DIM
    pad = (-rows) % pad_rows_to
    if pad:
        flat = jnp.concatenate([flat, jnp.zeros((1, pad * HEAD_DIM), F32)], axis=1)
    return flat.reshape(rows + pad, HEAD_DIM)


def _pad_lanes(a):
    a = a.reshape(1, -1)
    pad = _lanes(a.shape[1]) - a.shape[1]
    if pad:
        a = jnp.concatenate([a, jnp.zeros((1, pad), F32)], axis=1)
    return a


def kernel(x, mem, positions, a_norm, a_w_in, a_w_out, kv_norm, w_dkv, g_ckv, w_ukv, g_k_nope, g_k_rope, b_norm, b_w_in, b_g_q_lat, b_w_uq, b_g_q_nope, b_g_q_rope, b_w_out, mem_norm, w_mem_kv, g_mem_q, g_mem_k, loss_target, m_a_norm, m_a_w_in, m_a_w_out, m_kv_norm, m_w_dkv, m_g_ckv, m_w_ukv, m_g_k_nope, m_g_k_rope, m_b_norm, m_b_w_in, m_b_g_q_lat, m_b_w_uq, m_b_g_q_nope, m_b_g_q_rope, m_b_w_out, m_mem_norm, m_w_mem_kv, m_g_mem_q, m_g_mem_k, v_a_norm, v_a_w_in, v_a_w_out, v_kv_norm, v_w_dkv, v_g_ckv, v_w_ukv, v_g_k_nope, v_g_k_rope, v_b_norm, v_b_w_in, v_b_g_q_lat, v_b_w_uq, v_b_g_q_nope, v_b_g_q_rope, v_b_w_out, v_mem_norm, v_w_mem_kv, v_g_mem_q, v_g_mem_k):
    chip = 2 * lax.axis_index("x") + lax.axis_index("y")
    rows_dkv = D_MODEL // N_CHIPS
    heads_per_chip = N_MLA_HEADS // N_CHIPS
    qk_w = HEAD_DIM + ROPE_DIM

    big = {"a_in": a_w_in[0], "a_out": a_w_out[0], "dkv": w_dkv, "ukv": w_ukv, "b_in": b_w_in[0],
           "uq": b_w_uq[0], "b_out": b_w_out[0], "mem_kv": w_mem_kv.reshape(2 * rows_dkv, 2 * MEM_W)}
    big_m = {"a_in": m_a_w_in[0], "a_out": m_a_w_out[0], "dkv": m_w_dkv, "ukv": m_w_ukv, "b_in": m_b_w_in[0],
             "uq": m_b_w_uq[0], "b_out": m_b_w_out[0], "mem_kv": m_w_mem_kv.reshape(2 * rows_dkv, 2 * MEM_W)}
    big_v = {"a_in": v_a_w_in[0], "a_out": v_a_w_out[0], "dkv": v_w_dkv, "ukv": v_w_ukv, "b_in": v_b_w_in[0],
             "uq": v_b_w_uq[0], "b_out": v_b_w_out[0], "mem_kv": v_w_mem_kv.reshape(2 * rows_dkv, 2 * MEM_W)}
    names = list(big)
    gathered = _allgather_chips([big[n].astype(BF16) for n in names] + [a_norm])
    st = dict(zip(names, gathered[:-1]))
    a_in_full = st["a_in"].transpose(1, 0, 2).reshape(D_MODEL, QKV_W + GATE_W)
    uq = st["uq"].reshape(N_CHIPS, Q_LORA, heads_per_chip, qk_w)
    uq = jnp.pad(uq, ((0, 0), (0, 0), (0, 0), (0, CAT_W - qk_w)))
    w = {
        "a_in": a_in_full,
        "a_in_qkv": a_in_full[:, :QKV_W],
        "a_in_gate": a_in_full[:, QKV_W:],
        "a_out": st["a_out"].reshape(D_MODEL, D_MODEL),
        "dkv": jnp.pad(st["dkv"].reshape(D_MODEL, KV_LORA + ROPE_DIM), ((0, 0), (0, HEAD_DIM - ROPE_DIM))),
        "ukv": st["ukv"].transpose(1, 0, 2).reshape(KV_LORA, N_MLA_HEADS * CAT_W),
        "b_in": st["b_in"].transpose(1, 0, 2).reshape(D_MODEL, Q_LORA + GATE_W),
        "uq": uq.transpose(1, 0, 2, 3).reshape(Q_LORA, N_MLA_HEADS * CAT_W),
        "b_out": st["b_out"].reshape(D_MODEL, D_MODEL),
        "mem_kv": st["mem_kv"].reshape(N_CHIPS, 2, rows_dkv, 2 * MEM_W).transpose(1, 0, 2, 3).reshape(
            2, D_MODEL, 2 * MEM_W),
    }
    gains = {
        "a_norm": gathered[-1].reshape(1, D_MODEL), "kv_norm": kv_norm.reshape(1, -1),
        "g_ckv": g_ckv.reshape(1, -1), "g_k_nope": g_k_nope.reshape(1, -1), "g_k_rope": g_k_rope.reshape(1, -1),
        "b_norm": b_norm, "b_g_q_lat": b_g_q_lat, "b_g_q_nope": b_g_q_nope, "b_g_q_rope": b_g_q_rope,
        "mem_norm": mem_norm, "g_mem_q": g_mem_q, "g_mem_k": g_mem_k,
    }

    loss_part, grad_x, gw, gg = _local_step(x[0], mem[0], positions[0], loss_target[0], w, gains)

    stacked = {
        "a_in": gw["a_in"],
        "a_out": gw["a_out"].reshape(N_CHIPS, rows_dkv, D_MODEL),
        "dkv": gw["dkv"][:, :KV_LORA + ROPE_DIM].reshape(N_CHIPS, rows_dkv, KV_LORA + ROPE_DIM),
        "ukv": gw["ukv"],
        "b_in": gw["b_in"],
        "uq": gw["uq"].reshape(N_CHIPS, Q_LORA, heads_per_chip, CAT_W)[..., :qk_w].reshape(
            N_CHIPS, Q_LORA, heads_per_chip * qk_w),
        "b_out": gw["b_out"].reshape(N_CHIPS, rows_dkv, D_MODEL),
        "mem_kv": jnp.stack([gw["mem_kv"][0].reshape(N_CHIPS, rows_dkv, 2 * MEM_W),
                             gw["mem_kv"][1].reshape(N_CHIPS, rows_dkv, 2 * MEM_W)], axis=1).reshape(
            N_CHIPS, 2 * rows_dkv, 2 * MEM_W),
    }
    own, got = _halve_with_sibling([stacked[n] for n in names])
    chip_sum = [_pair_sum(a, b, name=f"pair_sum_{n}") for n, a, b in zip(names, own, got)]
    received = _scatter_to_chips(chip_sum)
    half_total = [_sum_slots(r, name=f"sum_slots_{n}") for n, r in zip(names, received)]
    total = _gather_halves(half_total)
    big_out = {}
    for n, grad in zip(names, total):
        big_out[n] = _adamw(big[n], [grad.reshape(big[n].shape)], big_m[n], big_v[n], name=f"adamw_{n}")

    pieces = [_pad_lanes(gg[n]) if n not in ("g_k_rope", "b_g_q_rope") else gg[n] for n, _ in _SMALL]
    pieces.append(loss_part)
    loss_row = sum(_lanes(size) for _, size in _SMALL) // HEAD_DIM
    summed = _allreduce_small(_pack_rows(pieces), loss_row)
    flat = summed.reshape(1, -1)
    small_g, off = {}, 0
    for n, size in _SMALL:
        small_g[n] = flat[:, off:off + size]
        off += _lanes(size)
    loss = flat[0, off]
    small_g["a_norm"] = lax.dynamic_slice(small_g["a_norm"], (0, chip * rows_dkv), (1, rows_dkv))

    small_w = {"a_norm": a_norm, "kv_norm": kv_norm, "g_ckv": g_ckv, "g_k_nope": g_k_nope, "g_k_rope": g_k_rope,
               "b_norm": b_norm, "b_g_q_lat": b_g_q_lat, "b_g_q_nope": b_g_q_nope, "b_g_q_rope": b_g_q_rope,
               "mem_norm": mem_norm, "g_mem_q": g_mem_q, "g_mem_k": g_mem_k}
    small_m = {"a_norm": m_a_norm, "kv_norm": m_kv_norm, "g_ckv": m_g_ckv, "g_k_nope": m_g_k_nope,
               "g_k_rope": m_g_k_rope, "b_norm": m_b_norm, "b_g_q_lat": m_b_g_q_lat, "b_g_q_nope": m_b_g_q_nope,
               "b_g_q_rope": m_b_g_q_rope, "mem_norm": m_mem_norm, "g_mem_q": m_g_mem_q, "g_mem_k": m_g_mem_k}
    small_v = {"a_norm": v_a_norm, "kv_norm": v_kv_norm, "g_ckv": v_g_ckv, "g_k_nope": v_g_k_nope,
               "g_k_rope": v_g_k_rope, "b_norm": v_b_norm, "b_g_q_lat": v_b_g_q_lat, "b_g_q_nope": v_b_g_q_nope,
               "b_g_q_rope": v_b_g_q_rope, "mem_norm": v_mem_norm, "g_mem_q": v_g_mem_q, "g_mem_k": v_g_mem_k}
    snames = [n for n, _ in _SMALL]
    packs = [_pack_rows([_pad_lanes(src[n]) for n in snames])
             for src in (small_w, small_g, small_m, small_v)]
    small_res = _adamw(packs[0], [packs[1]], packs[2], packs[3], name="adamw_gains")
    small_out = {n: [] for n in snames}
    for res in small_res:
        flat_r = res.reshape(1, -1)
        off = 0
        for n in snames:
            size = small_w[n].size
            small_out[n].append(flat_r[:, off:off + size].reshape(small_w[n].shape))
            off += _lanes(size)

    big_names = {"a_w_in": ("a_in", a_w_in), "a_w_out": ("a_out", a_w_out), "w_dkv": ("dkv", w_dkv),
                 "w_ukv": ("ukv", w_ukv), "b_w_in": ("b_in", b_w_in), "b_w_uq": ("uq", b_w_uq),
                 "b_w_out": ("b_out", b_w_out), "w_mem_kv": ("mem_kv", w_mem_kv)}
    order = ["a_norm", "a_w_in", "a_w_out", "kv_norm", "w_dkv", "g_ckv", "w_ukv", "g_k_nope", "g_k_rope",
             "b_norm", "b_w_in", "b_g_q_lat", "b_w_uq", "b_g_q_nope", "b_g_q_rope", "b_w_out", "mem_norm",
             "w_mem_kv", "g_mem_q", "g_mem_k"]
    groups = [[], [], [], []]
    for n in order:
        if n in big_names:
            key, ref_arr = big_names[n]
            for t in range(4):
                groups[t].append(big_out[key][t].reshape(ref_arr.shape))
        else:
            for t in range(4):
                groups[t].append(small_out[n][t])
    return (loss, grad_x[None], *groups[0], *groups[1], *groups[2], *groups[3])
```

```python
import functools

import jax
import jax.numpy as jnp
from jax import lax
from jax.experimental import pallas as pl
from jax.experimental.pallas import tpu as pltpu

F32 = jnp.float32
BF16 = jnp.bfloat16
MESH = pl.DeviceIdType.MESH

D_MODEL = 2048
HEAD_DIM = 128
N_SB_HEADS = 12
N_MEM_HEADS = 4
N_MLA_HEADS = 12
MEM_LEN = 256
Q_LORA = 512
KV_LORA = 512
ROPE_DIM = 64
SB_W = N_SB_HEADS * HEAD_DIM
MEM_W = N_MEM_HEADS * HEAD_DIM
MLA_W = N_MLA_HEADS * HEAD_DIM
QKV_W = 3 * SB_W
GATE_W = SB_W + 2 * MEM_W
CAT_W = 2 * HEAD_DIM
ROPE_THETA = 10000.0
EPS = 1e-6
N_CHIPS = 4
N_DEV = 8

ADAM_LR = 0.001
ADAM_B1 = 0.9
ADAM_B2 = 0.999
ADAM_EPS = 1e-08
ADAM_WD = 0.01
ADAM_STEP = 10

VMEM_LIMIT_BYTES = 56 * 1024 * 1024
MM_OPERAND_VMEM_BYTES = 24 * 1024 * 1024
ROW_BLOCK = 256
ATT_BLOCK = 256


def _params(*sem):
    return pltpu.CompilerParams(dimension_semantics=sem, vmem_limit_bytes=VMEM_LIMIT_BYTES)


def _pick(n, cands):
    for c in cands:
        if n % c == 0:
            return c
    return n


def _mm(a, b, *, name, ta=False, tb=False, out_dtype=F32, res=None, n_split=1, scale_cols=None):
    if ta:
        k_dim, m_dim = a.shape
    else:
        m_dim, k_dim = a.shape
    if tb:
        n_dim, kb = b.shape
    else:
        kb, n_dim = b.shape
    assert kb == k_dim, (a.shape, b.shape)
    n_per = n_dim // n_split
    bm = m_dim if m_dim <= 1024 else _pick(m_dim, (1024, 512, 256))
    bn = n_per if n_per <= 1024 else _pick(n_per, (1024, 896, 768, 640, 512, 256, 128))
    per_k = (bm * a.dtype.itemsize + bn * b.dtype.itemsize) * 2
    bk = next((c for c in (k_dim, 2048, 1024, 512, 256, 128)
               if c <= k_dim and k_dim % c == 0 and c * per_k <= MM_OPERAND_VMEM_BYTES), 128)
    nk = k_dim // bk
    nb_per = n_per // bn
    grid = (m_dim // bm, n_dim // bn, nk)
    a_spec = (pl.BlockSpec((bk, bm), lambda i, j, k: (k, i)) if ta
              else pl.BlockSpec((bm, bk), lambda i, j, k: (i, k)))
    b_spec = (pl.BlockSpec((bn, bk), lambda i, j, k: (j, k)) if tb
              else pl.BlockSpec((bk, bn), lambda i, j, k: (k, j)))
    dims = (((0 if ta else 1,), (1 if tb else 0,)), ((), ()))
    in_specs = [a_spec, b_spec]
    args = [a, b]
    if res is not None:
        in_specs.append(pl.BlockSpec((bm, bn), lambda i, j, k: (i, j)))
        args.append(res)
    if n_split == 1:
        out_shape = jax.ShapeDtypeStruct((m_dim, n_dim), out_dtype)
        out_spec = pl.BlockSpec((bm, bn), lambda i, j, k: (i, j))
    else:
        out_shape = jax.ShapeDtypeStruct((n_split, m_dim, n_per), out_dtype)
        out_spec = pl.BlockSpec((None, bm, bn), lambda i, j, k: (j // nb_per, i, j % nb_per))

    def body(*refs):
        if res is None:
            a_ref, b_ref, o_ref, acc = refs
            r_ref = None
        else:
            a_ref, b_ref, r_ref, o_ref, acc = refs
        k = pl.program_id(2)
        col_block = pl.program_id(1)

        @pl.when(k == 0)
        def _():
            acc[...] = jnp.zeros_like(acc)

        acc[...] += lax.dot_general(a_ref[...].astype(BF16), b_ref[...].astype(BF16), dims,
                                    preferred_element_type=F32)

        @pl.when(k == nk - 1)
        def _():
            r = acc[...]
            if r_ref is not None:
                r = r + r_ref[...]
            if scale_cols is not None:
                assert scale_cols[0] % bn == 0
                r = r * jnp.where(col_block < scale_cols[0] // bn, scale_cols[1], 1.0)
            o_ref[...] = r.astype(out_dtype)

    return pl.pallas_call(
        body, out_shape=out_shape, grid=grid, in_specs=in_specs, out_specs=out_spec,
        scratch_shapes=[pltpu.VMEM((bm, bn), F32)], name=name,
        compiler_params=_params("parallel", "parallel", "arbitrary"),
    )(*args)


def _rowwise(body, n_rows, ins, outs, accs=(), *, name, block=ROW_BLOCK):
    blk = min(block, n_rows)
    assert n_rows % blk == 0
    in_specs = []
    for arr, is_row in ins:
        if is_row:
            assert arr.shape[0] == n_rows, (name, arr.shape, n_rows)
            in_specs.append(pl.BlockSpec((blk, arr.shape[1]), lambda i: (i, 0)))
        else:
            in_specs.append(pl.BlockSpec(arr.shape, lambda i, nd=arr.ndim: (0,) * nd))
    out_shape = [jax.ShapeDtypeStruct((n_rows, w), dt) for w, dt in outs]
    out_specs = [pl.BlockSpec((blk, w), lambda i: (i, 0)) for w, _ in outs]
    out_shape += [jax.ShapeDtypeStruct(s, dt) for s, dt in accs]
    out_specs += [pl.BlockSpec(s, lambda i, nd=len(s): (0,) * nd) for s, _ in accs]
    n_in, n_out, n_acc = len(ins), len(outs), len(accs)

    def kern(*refs):
        in_refs = refs[:n_in]
        out_refs = refs[n_in:n_in + n_out]
        acc_refs = refs[n_in + n_out:]
        if n_acc:
            @pl.when(pl.program_id(0) == 0)
            def _():
                for r in acc_refs:
                    r[...] = jnp.zeros_like(r)
        body(in_refs, out_refs, acc_refs)

    return pl.pallas_call(
        kern, out_shape=out_shape, grid=(n_rows // blk,), in_specs=in_specs, out_specs=out_specs,
        name=name, compiler_params=_params("arbitrary"),
    )(*[arr for arr, _ in ins])


def _rms(x, g, n=None):
    n = x.shape[-1] if n is None else n
    r = lax.rsqrt(jnp.sum(x * x, axis=-1, keepdims=True) / n + EPS)
    return x * r * g


def _rms_bwd(x, g, dy, n=None):
    n = x.shape[-1] if n is None else n
    r = lax.rsqrt(jnp.sum(x * x, axis=-1, keepdims=True) / n + EPS)
    gdy = dy * g
    dx = r * (gdy - x * ((r * r) * (jnp.sum(gdy * x, axis=-1, keepdims=True) / n)))
    dg = jnp.sum(dy * x * r, axis=0, keepdims=True)
    return dx, dg


def _swap_halves(x):
    lane = lax.broadcasted_iota(jnp.int32, x.shape, 1)
    return jnp.where(lane < ROPE_DIM // 2, pltpu.roll(x, 128 - ROPE_DIM // 2, 1),
                     pltpu.roll(x, ROPE_DIM // 2, 1))


def _rope(n, cos_t, sin_t):
    return n * cos_t + _swap_halves(n) * sin_t


def _rope_bwd(dy, cos_t, sin_t):
    return dy * cos_t - _swap_halves(dy) * sin_t


def _sigmoid(g):
    return 1.0 / (1.0 + jnp.exp(-g))


def _dot_t(a, b):
    return lax.dot_general(a, b, (((1,), (1,)), ((), ())), preferred_element_type=F32)


def _tdot(a, b):
    return lax.dot_general(a, b, (((0,), (0,)), ((), ())), preferred_element_type=F32)


def _dot(a, b):
    return jnp.dot(a, b, preferred_element_type=F32)


def _hs(h, w=HEAD_DIM, base=0):
    return slice(base + h * w, base + (h + 1) * w)


def _mem_head(qm, gq, mk_h, mv_h):
    qb = _rms(qm, gq).astype(BF16)
    s = _dot_t(qb, mk_h) * (HEAD_DIM ** -0.5)
    e = jnp.exp(s - jnp.max(s, axis=-1, keepdims=True))
    p = e / jnp.sum(e, axis=-1, keepdims=True)
    mo = _dot(p.astype(BF16), mv_h)
    return qb, p, mo


def _mix_fwd(att, gates, c0, mk, mv, gq, *, name):
    n_rows = att.shape[0]

    def body(ins, outs, _):
        att_ref, g_ref, mk_ref, mv_ref, gq_ref = ins
        (o_ref,) = outs
        g = g_ref[:, c0:c0 + SB_W]
        o_ref[:, :SB_W] = (att_ref[...] * (g * _sigmoid(g))).astype(BF16)
        for h in range(N_MEM_HEADS):
            qm = g_ref[:, _hs(h, base=c0 + SB_W)]
            gm = g_ref[:, _hs(h, base=c0 + SB_W + MEM_W)]
            _, _, mo = _mem_head(qm, gq_ref[...], mk_ref[:, _hs(h)], mv_ref[:, _hs(h)])
            o_ref[:, _hs(h, base=SB_W)] = (mo * (gm * _sigmoid(gm))).astype(BF16)

    (mixed,) = _rowwise(body, n_rows,
                        [(att, True), (gates, True), (mk, False), (mv, False), (gq, False)],
                        [(D_MODEL, BF16)], name=name)
    return mixed


def _mix_bwd(dmixed, att, gates, c0, mk, mv, gq, *, name):
    n_rows = att.shape[0]
    scale = HEAD_DIM ** -0.5

    def body(ins, outs, accs):
        dm_ref, att_ref, g_ref, mk_ref, mv_ref, gq_ref = ins
        datt_ref, dg_ref = outs
        dmk_ref, dmv_ref, dgq_ref = accs
        g = g_ref[:, c0:c0 + SB_W]
        sg = _sigmoid(g)
        dm = dm_ref[:, :SB_W]
        datt_ref[...] = dm * (g * sg)
        dg_ref[:, :SB_W] = (dm * att_ref[...] * (sg * (1.0 + g * (1.0 - sg)))).astype(BF16)
        for h in range(N_MEM_HEADS):
            qm = g_ref[:, _hs(h, base=c0 + SB_W)]
            gm = g_ref[:, _hs(h, base=c0 + SB_W + MEM_W)]
            mk_h = mk_ref[:, _hs(h)]
            mv_h = mv_ref[:, _hs(h)]
            qb, p, mo = _mem_head(qm, gq_ref[...], mk_h, mv_h)
            sgm = _sigmoid(gm)
            dmh = dm_ref[:, _hs(h, base=SB_W)]
            dmo = dmh * (gm * sgm)
            dg_ref[:, _hs(h, base=SB_W + MEM_W)] = (
                dmh * mo * (sgm * (1.0 + gm * (1.0 - sgm)))).astype(BF16)
            dmo_b = dmo.astype(BF16)
            pb = p.astype(BF16)
            dp = _dot_t(dmo_b, mv_h)
            dmv_ref[:, _hs(h)] += _tdot(pb, dmo_b)
            ds = (p * (dp - jnp.sum(dp * p, axis=-1, keepdims=True)) * scale).astype(BF16)
            dqn = _dot(ds, mk_h)
            dmk_ref[:, _hs(h)] += _tdot(ds, qb)
            dqm, dgq = _rms_bwd(qm, gq_ref[...], dqn)
            dg_ref[:, _hs(h, base=SB_W)] = dqm.astype(BF16)
            dgq_ref[...] += dgq

    return _rowwise(body, n_rows,
                    [(dmixed, True), (att, True), (gates, True), (mk, False), (mv, False), (gq, False)],
                    [(SB_W, F32), (GATE_W, BF16)],
                    [((MEM_LEN, MEM_W), F32), ((MEM_LEN, MEM_W), F32), ((1, HEAD_DIM), F32)],
                    name=name)


def _mem_side_fwd(mem, g_norm, w_kv, g_k, *, tag):
    def norm_body(ins, outs, _):
        outs[0][...] = _rms(ins[0][...], ins[1][...]).astype(BF16)

    (mn,) = _rowwise(norm_body, MEM_LEN, [(mem, True), (g_norm, False)], [(D_MODEL, BF16)],
                     name=f"mem_norm_{tag}")
    mkv = _mm(mn, w_kv, name=f"mem_kv_{tag}")

    def kv_body(ins, outs, _):
        mkv_ref, gk_ref = ins
        mk_ref, mv_ref = outs
        for h in range(N_MEM_HEADS):
            mk_ref[:, _hs(h)] = _rms(mkv_ref[:, _hs(h)], gk_ref[...]).astype(BF16)
        mv_ref[...] = mkv_ref[:, MEM_W:].astype(BF16)

    mk, mv = _rowwise(kv_body, MEM_LEN, [(mkv, True), (g_k, False)], [(MEM_W, BF16), (MEM_W, BF16)],
                      name=f"mem_kv_prep_{tag}")
    return mn, mkv, mk, mv


def _mem_side_bwd(mem, g_norm, w_kv, g_k, mn, mkv, dmk, dmv, *, tag):
    def kv_body(ins, outs, accs):
        mkv_ref, gk_ref, dmk_ref, dmv_ref = ins
        (d_ref,) = outs
        (dgk_ref,) = accs
        for h in range(N_MEM_HEADS):
            dx, dg = _rms_bwd(mkv_ref[:, _hs(h)], gk_ref[...], dmk_ref[:, _hs(h)])
            d_ref[:, _hs(h)] = dx.astype(BF16)
            dgk_ref[...] += dg
        d_ref[:, MEM_W:] = dmv_ref[...].astype(BF16)

    dmkv, dgk = _rowwise(kv_body, MEM_LEN, [(mkv, True), (g_k, False), (dmk, True), (dmv, True)],
                         [(2 * MEM_W, BF16)], [((1, HEAD_DIM), F32)], name=f"mem_kv_prep_bwd_{tag}")
    dmn = _mm(dmkv, w_kv, tb=True, name=f"mem_kv_dx_{tag}")
    dw = _mm(mn, dmkv, ta=True, out_dtype=BF16, name=f"mem_kv_dw_{tag}")

    def norm_body(ins, outs, accs):
        _, dg = _rms_bwd(ins[0][...], ins[1][...], ins[2][...])
        accs[0][...] += dg

    (dgn,) = _rowwise(norm_body, MEM_LEN, [(mem, True), (g_norm, False), (dmn, True)], [],
                      [((1, D_MODEL), F32)], name=f"mem_norm_bwd_{tag}")
    return dw, dgn, dgk


LOG2_E = 1.4426950408889634
SB_Q_SCALE = HEAD_DIM ** -0.5 * LOG2_E


Z2_CAP = 126.0


def _sb_terms(z2):
    zc = jnp.minimum(z2, Z2_CAP)
    w = 1.0 + jnp.exp2(zc)
    return zc, w, jnp.log2(w)


def _chain_modes(s, qb):
    return tuple(None if t < s else ("m" if t == s else "f") for t in range(qb))


def _split_dot(x, tri2):
    hi = x.astype(BF16)
    lo = (x - hi.astype(F32)).astype(BF16)
    return _dot(jnp.concatenate([hi, lo], axis=1), tri2)


def _sb_fwd(qkv, *, name, hp=2):
    seq = qkv.shape[0]
    blk = min(ATT_BLOCK, seq)
    nkb = seq // blk
    qb = _pick(nkb, (2, 1))
    rows = qb * blk
    chains =[(t, s) for t in range(hp) for s in range(qb)]

    def body(q_ref, k_ref, v_ref, o_ref):
        base = pl.program_id(1) * qb
        qs = {(t, s): q_ref[s * blk:(s + 1) * blk, _hs(t)] for t, s in chains}
        row = lax.broadcasted_iota(jnp.int32, (blk, blk), 0)
        col = lax.broadcasted_iota(jnp.int32, (blk, blk), 1)
        after = (row > col).astype(BF16)
        after2 = jnp.concatenate([after, after], axis=0)
        causal = col < row

        def step(j, carry, modes):
            off = pl.multiple_of(j * blk, blk)
            act = [c for c in chains if modes[c[1]]]
            zs, ls = {}, {}
            for c in act:
                zs[c], _, l = _sb_terms(_dot_t(qs[c], k_ref[pl.ds(off, blk), _hs(c[0])]))
                ls[c] = jnp.where(causal, l, 0.0) if modes[c[1]] == "m" else l
            cs = {c: _split_dot(ls[c], after2) for c in act}
            carry = dict(carry)
            for c in act:
                run, acc = carry[c]
                a = jnp.exp2(zs[c] - ls[c] - cs[c] - run)
                if modes[c[1]] == "m":
                    a = jnp.where(causal, a, 0.0)
                acc = acc + _dot(a.astype(BF16), v_ref[pl.ds(off, blk), _hs(c[0])])
                carry[c] = (run + jnp.sum(ls[c], axis=-1, keepdims=True), acc)
            return carry

        init = (jnp.zeros((blk, 1), F32), jnp.zeros((blk, HEAD_DIM), F32))
        carry = {c: init for c in chains}
        for s in reversed(range(qb)):
            carry = step(base + s, carry, _chain_modes(s, qb))
        carry = lax.fori_loop(0, base, lambda jj, c: step(base - 1 - jj, c, ("f",) * qb), carry)
        for t, s in chains:
            o_ref[s * blk:(s + 1) * blk, _hs(t)] = carry[(t, s)][1]

    nh = N_SB_HEADS // hp
    return pl.pallas_call(
        body, out_shape=jax.ShapeDtypeStruct((seq, SB_W), F32), grid=(nh, nkb // qb),
        in_specs=[pl.BlockSpec((rows, hp * HEAD_DIM), lambda h, i: (i, h)),
                  pl.BlockSpec((seq, hp * HEAD_DIM), lambda h, i: (0, nh + h)),
                  pl.BlockSpec((seq, hp * HEAD_DIM), lambda h, i: (0, 2 * nh + h))],
        out_specs=pl.BlockSpec((rows, hp * HEAD_DIM), lambda h, i: (i, h)),
        name=name, compiler_params=_params("parallel", "arbitrary"),
    )(qkv, qkv, qkv)


SB_BWD_GROUP = 4


def _sb_bwd(qkv, out, dout, *, name):
    seq = qkv.shape[0]
    blk = min(ATT_BLOCK, seq)
    nkb = seq // blk
    qb = _pick(nkb, (4, 2, 1))
    rows = qb * blk
    scale = HEAD_DIM ** -0.5

    def body(q_ref, k_ref, v_ref, do_ref, o_ref, dq_ref, dk_ref, dv_ref):
        g = pl.program_id(1)
        base = g * qb

        @pl.when(g == 0)
        def _():
            dk_ref[...] = jnp.zeros_like(dk_ref)
            dv_ref[...] = jnp.zeros_like(dv_ref)

        qs = [q_ref[t * blk:(t + 1) * blk, :] for t in range(qb)]
        dos = [do_ref[t * blk:(t + 1) * blk, :].astype(BF16) for t in range(qb)]
        totals = [jnp.sum(dos[t].astype(F32) * o_ref[t * blk:(t + 1) * blk, :], axis=-1, keepdims=True)
                  for t in range(qb)]
        row = lax.broadcasted_iota(jnp.int32, (blk, blk), 0)
        col = lax.broadcasted_iota(jnp.int32, (blk, blk), 1)
        after = (row > col).astype(BF16)
        after2 = jnp.concatenate([after, after], axis=0)
        from_s = (row >= col).astype(BF16)
        from_s2 = jnp.concatenate([from_s, from_s], axis=0)
        causal = col < row

        def step(j, carry, modes):
            runs, rights, dqs = list(carry[0]), list(carry[1]), list(carry[2])
            off = pl.multiple_of(j * blk, blk)
            kb = k_ref[pl.ds(off, blk), :]
            vb = v_ref[pl.ds(off, blk), :]
            dv_inc = dk_inc = None
            for first in range(0, qb, SB_BWD_GROUP):
                act = [t for t in range(first, min(first + SB_BWD_GROUP, qb)) if modes[t]]
                das = {t: _dot_t(dos[t], vb) for t in act}
                zs, ls, sns = {}, {}, {}
                for t in act:
                    zs[t], w, l = _sb_terms(_dot_t(qs[t], kb))
                    sns[t] = pl.reciprocal(w, approx=True)
                    ls[t] = jnp.where(causal, l, 0.0) if modes[t] == "m" else l
                cs = {t: _split_dot(ls[t], after2) for t in act}
                abs_, des = {}, {}
                for t in act:
                    a = jnp.exp2(zs[t] - ls[t] - cs[t] - runs[t])
                    if modes[t] == "m":
                        a = jnp.where(causal, a, 0.0)
                    abs_[t] = a.astype(BF16)
                    des[t] = abs_[t].astype(F32) * das[t]
                sufs = {t: _split_dot(des[t], from_s2) for t in act}
                for t in act:
                    left = totals[t] - (sufs[t] + rights[t])
                    dz = (des[t] + left) * sns[t] - left
                    if modes[t] == "m":
                        dz = jnp.where(causal, dz, 0.0)
                    dzb = dz.astype(BF16)
                    dqs[t] = dqs[t] + _dot(dzb, kb)
                    inc_v = _tdot(abs_[t], dos[t])
                    inc_k = _tdot(dzb, qs[t])
                    dv_inc = inc_v if dv_inc is None else dv_inc + inc_v
                    dk_inc = inc_k if dk_inc is None else dk_inc + inc_k
                    runs[t] = runs[t] + jnp.sum(ls[t], axis=-1, keepdims=True)
                    rights[t] = rights[t] + jnp.sum(des[t], axis=-1, keepdims=True)
            dv_ref[pl.ds(off, blk), :] += dv_inc
            dk_ref[pl.ds(off, blk), :] += dk_inc
            return tuple(runs), tuple(rights), tuple(dqs)

        zero = (jnp.zeros((blk, 1), F32),) * qb
        carry = (zero, zero, (jnp.zeros((blk, HEAD_DIM), F32),) * qb)
        for s in reversed(range(qb)):
            carry = step(base + s, carry, _chain_modes(s, qb))
        carry = lax.fori_loop(0, base, lambda jj, c: step(base - 1 - jj, c, ("f",) * qb), carry)
        for t in range(qb):
            dq_ref[t * blk:(t + 1) * blk, :] = carry[2][t] * scale

        @pl.when(g == pl.num_programs(1) - 1)
        def _():
            dk_ref[...] = dk_ref[...] * (1.0 / LOG2_E)

    out_sd = jax.ShapeDtypeStruct((seq, SB_W), F32)
    return pl.pallas_call(
        body, out_shape=[out_sd, out_sd, out_sd], grid=(N_SB_HEADS, nkb // qb),
        in_specs=[pl.BlockSpec((rows, HEAD_DIM), lambda h, i: (i, h)),
                  pl.BlockSpec((seq, HEAD_DIM), lambda h, i: (0, N_SB_HEADS + h)),
                  pl.BlockSpec((seq, HEAD_DIM), lambda h, i: (0, 2 * N_SB_HEADS + h)),
                  pl.BlockSpec((rows, HEAD_DIM), lambda h, i: (i, h)),
                  pl.BlockSpec((rows, HEAD_DIM), lambda h, i: (i, h))],
        out_specs=[pl.BlockSpec((rows, HEAD_DIM), lambda h, i: (i, h)),
                   pl.BlockSpec((seq, HEAD_DIM), lambda h, i: (0, h)),
                   pl.BlockSpec((seq, HEAD_DIM), lambda h, i: (0, h))],
        name=name, compiler_params=_params("parallel", "arbitrary"),
    )(qkv, qkv, qkv, dout, out)


MLA_SCALE = (HEAD_DIM + ROPE_DIM) ** -0.5
MLA_Q_SCALE = MLA_SCALE * LOG2_E


def _mla_fwd(q_cat, k_cat, v, *, name, hp=2):
    seq = q_cat.shape[0]
    blk = min(ATT_BLOCK, seq)
    nkb = seq // blk
    qb = _pick(nkb, (2, 1))
    rows = qb * blk
    chains = [(t, s) for t in range(hp) for s in range(qb)]

    def body(q_ref, k_ref, v_ref, o_ref, lse_ref):
        base = pl.program_id(1) * qb
        qs = {(t, s): q_ref[s * blk:(s + 1) * blk, t * CAT_W:(t + 1) * CAT_W] for t, s in chains}
        row = lax.broadcasted_iota(jnp.int32, (blk, blk), 0)
        col = lax.broadcasted_iota(jnp.int32, (blk, blk), 1)
        causal = col <= row

        def step(j, carry, modes):
            off = pl.multiple_of(j * blk, blk)
            act = [c for c in chains if modes[c[1]]]
            ss = {c: _dot_t(qs[c], k_ref[pl.ds(off, blk), c[0] * CAT_W:(c[0] + 1) * CAT_W]) for c in act}
            carry = dict(carry)
            for c in act:
                m, l, acc = carry[c]
                s = ss[c]
                if modes[c[1]] == "m":
                    s = jnp.where(causal, s, -jnp.inf)
                m_new = jnp.maximum(m, jnp.max(s, axis=-1, keepdims=True))
                p = jnp.exp2(s - m_new)
                alpha = jnp.exp2(m - m_new)
                l = alpha * l + jnp.sum(p, axis=-1, keepdims=True)
                acc = alpha * acc + _dot(p.astype(BF16), v_ref[pl.ds(off, blk), _hs(c[0])])
                carry[c] = (m_new, l, acc)
            return carry

        init = (jnp.full((blk, 1), -jnp.inf, F32), jnp.zeros((blk, 1), F32),
                jnp.zeros((blk, HEAD_DIM), F32))
        carry = {c: init for c in chains}
        carry = lax.fori_loop(0, base, lambda j, c: step(j, c, ("f",) * qb), carry)
        for s in range(qb):
            carry = step(base + s, carry, _chain_modes(s, qb))
        for t, s in chains:
            m, l, acc = carry[(t, s)]
            o_ref[s * blk:(s + 1) * blk, _hs(t)] = acc / l
            lse_ref[s * blk:(s + 1) * blk, _hs(t)] = jnp.broadcast_to(
                (m + jnp.log2(l)) * (1.0 / LOG2_E), (blk, HEAD_DIM))

    out = jax.ShapeDtypeStruct((seq, MLA_W), F32)
    return pl.pallas_call(
        body, out_shape=[out, out], grid=(N_MLA_HEADS // hp, nkb // qb),
        in_specs=[pl.BlockSpec((rows, hp * CAT_W), lambda h, i: (i, h)),
                  pl.BlockSpec((seq, hp * CAT_W), lambda h, i: (0, h)),
                  pl.BlockSpec((seq, hp * HEAD_DIM), lambda h, i: (0, h))],
        out_specs=[pl.BlockSpec((rows, hp * HEAD_DIM), lambda h, i: (i, h)),
                   pl.BlockSpec((rows, hp * HEAD_DIM), lambda h, i: (i, h))],
        name=name, compiler_params=_params("parallel", "arbitrary"),
    )(q_cat, k_cat, v)


def _mla_bwd(q_cat, k_cat, v, out, lse, dout, *, name):
    seq = q_cat.shape[0]
    blk = min(ATT_BLOCK, seq)
    nkb = seq // blk
    qb = _pick(nkb, (4, 2, 1))
    rows = qb * blk

    def body(q_ref, k_ref, v_ref, o_ref, lse_ref, do_ref, dq_ref, dk_ref, dv_ref):
        g = pl.program_id(1)
        base = g * qb

        @pl.when(g == 0)
        def _():
            dk_ref[...] = jnp.zeros_like(dk_ref)
            dv_ref[...] = jnp.zeros_like(dv_ref)

        qs, dobs, deltas, lses = [], [], [], []
        for t in range(qb):
            rs = slice(t * blk, (t + 1) * blk)
            do = do_ref[rs, :]
            qs.append(q_ref[rs, :])
            dobs.append(do.astype(BF16))
            deltas.append(jnp.sum(do * o_ref[rs, :], axis=-1, keepdims=True))
            lses.append(lse_ref[rs, :1] * LOG2_E)
        row = lax.broadcasted_iota(jnp.int32, (blk, blk), 0)
        col = lax.broadcasted_iota(jnp.int32, (blk, blk), 1)
        causal = col <= row

        def step(j, dqs, modes):
            off = pl.multiple_of(j * blk, blk)
            kb = k_ref[pl.ds(off, blk), :]
            vb = v_ref[pl.ds(off, blk), :]
            act = [t for t in range(qb) if modes[t]]
            ss = {t: _dot_t(qs[t], kb) for t in act}
            dps = {t: _dot_t(dobs[t], vb) for t in act}
            dqs = list(dqs)
            dv_inc = dk_inc = None
            for t in act:
                p = jnp.exp2(ss[t] - lses[t])
                if modes[t] == "m":
                    p = jnp.where(causal, p, 0.0)
                ds = (p * (dps[t] - deltas[t])).astype(BF16)
                inc_v = _tdot(p.astype(BF16), dobs[t])
                inc_k = _tdot(ds, qs[t])
                dv_inc = inc_v if dv_inc is None else dv_inc + inc_v
                dk_inc = inc_k if dk_inc is None else dk_inc + inc_k
                dqs[t] = dqs[t] + _dot(ds, kb)
            dv_ref[pl.ds(off, blk), :] += dv_inc
            dk_ref[pl.ds(off, blk), :] += dk_inc
            return tuple(dqs)

        dqs = (jnp.zeros((blk, CAT_W), F32),) * qb
        dqs = lax.fori_loop(0, base, lambda j, c: step(j, c, ("f",) * qb), dqs)
        for s in range(qb):
            modes = tuple(None if t < s else ("m" if t == s else "f") for t in range(qb))
            dqs = step(base + s, dqs, modes)
        for t in range(qb):
            dq_ref[t * blk:(t + 1) * blk, :] = dqs[t] * MLA_SCALE

        @pl.when(g == pl.num_programs(1) - 1)
        def _():
            dk_ref[...] = dk_ref[...] * (1.0 / LOG2_E)

    return pl.pallas_call(
        body,
        out_shape=[jax.ShapeDtypeStruct((seq, N_MLA_HEADS * CAT_W), F32),
                   jax.ShapeDtypeStruct((seq, N_MLA_HEADS * CAT_W), F32),
                   jax.ShapeDtypeStruct((seq, MLA_W), F32)],
        grid=(N_MLA_HEADS, nkb // qb),
        in_specs=[pl.BlockSpec((rows, CAT_W), lambda h, i: (i, h)),
                  pl.BlockSpec((seq, CAT_W), lambda h, i: (0, h)),
                  pl.BlockSpec((seq, HEAD_DIM), lambda h, i: (0, h)),
                  pl.BlockSpec((rows, HEAD_DIM), lambda h, i: (i, h)),
                  pl.BlockSpec((rows, HEAD_DIM), lambda h, i: (i, h)),
                  pl.BlockSpec((rows, HEAD_DIM), lambda h, i: (i, h))],
        out_specs=[pl.BlockSpec((rows, CAT_W), lambda h, i: (i, h)),
                   pl.BlockSpec((seq, CAT_W), lambda h, i: (0, h)),
                   pl.BlockSpec((seq, HEAD_DIM), lambda h, i: (0, h))],
        name=name, compiler_params=_params("parallel", "arbitrary"),
    )(q_cat, k_cat, v, out, lse, dout)


def _local_step(x, mem, positions, target, w, g):
    seq = x.shape[0]
    inv_freq = jnp.power(ROPE_THETA, -jnp.arange(0, ROPE_DIM, 2, dtype=F32) / ROPE_DIM)
    ang = positions.astype(F32)[:, None] * inv_freq
    cos, sin = jnp.cos(ang), jnp.sin(ang)
    lane_pad = jnp.zeros((seq, HEAD_DIM - ROPE_DIM), F32)
    cos_t = jnp.concatenate([cos, cos, lane_pad], axis=1)
    sin_t = jnp.concatenate([-sin, sin, lane_pad], axis=1)
    gain_pad = jnp.zeros((1, HEAD_DIM - ROPE_DIM), F32)
    g_k_rope = jnp.concatenate([g["g_k_rope"], gain_pad], axis=1)
    g_q_rope = jnp.concatenate([g["b_g_q_rope"], gain_pad], axis=1)

    def norm_to_bf16(src, gain, name):
        def body(ins, outs, _):
            outs[0][...] = _rms(ins[0][...], ins[1][...]).astype(BF16)
        return _rowwise(body, seq, [(src, True), (gain, False)], [(src.shape[1], BF16)], name=name)[0]

    h_a = norm_to_bf16(x, g["a_norm"], "a_norm_fwd")
    qkv = _mm(h_a, w["a_in_qkv"], out_dtype=BF16, scale_cols=(SB_W, SB_Q_SCALE), name="a_in_qkv")
    gr = _mm(h_a, w["a_in_gate"], name="a_in_gate")
    sb = _sb_fwd(qkv, name="sb_fwd")
    mem0 = _mem_side_fwd(mem, g["mem_norm"][0:1], w["mem_kv"][0], g["g_mem_k"][0:1], tag="a")
    mixed_a = _mix_fwd(sb, gr, 0, mem0[2], mem0[3], g["g_mem_q"][0:1], name="a_mix_fwd")
    x1 = _mm(mixed_a, w["a_out"], res=x, name="a_out")

    def norms2_body(ins, outs, _):
        xv = ins[0][...]
        outs[0][...] = _rms(xv, ins[1][...]).astype(BF16)
        outs[1][...] = _rms(xv, ins[2][...]).astype(BF16)

    h_kv, h_b = _rowwise(norms2_body, seq, [(x1, True), (g["kv_norm"], False), (g["b_norm"], False)],
                         [(D_MODEL, BF16), (D_MODEL, BF16)], name="kv_b_norm_fwd")
    ckr = _mm(h_kv, w["dkv"], name="dkv")

    def ckr_body(ins, outs, _):
        ckr_ref, gc_ref, gr_ref, c_ref, s_ref = ins
        outs[0][...] = _rms(ckr_ref[:, :KV_LORA], gc_ref[...]).astype(BF16)
        kr = _rms(ckr_ref[:, KV_LORA:], gr_ref[...], n=ROPE_DIM)
        outs[1][...] = _rope(kr, c_ref[...], s_ref[...]).astype(BF16)

    c_n, k_r = _rowwise(ckr_body, seq,
                        [(ckr, True), (g["g_ckv"], False), (g_k_rope, False), (cos_t, True), (sin_t, True)],
                        [(KV_LORA, BF16), (HEAD_DIM, BF16)], name="ckv_prep_fwd")
    kv = _mm(c_n, w["ukv"], name="ukv")

    def kcat_body(ins, outs, _):
        kv_ref, kr_ref, gk_ref = ins
        kc_ref, v_ref = outs
        for h in range(N_MLA_HEADS):
            kc_ref[:, h * CAT_W:h * CAT_W + HEAD_DIM] = _rms(
                kv_ref[:, h * CAT_W:h * CAT_W + HEAD_DIM], gk_ref[...]).astype(BF16)
            kc_ref[:, h * CAT_W + HEAD_DIM:(h + 1) * CAT_W] = kr_ref[...]
            v_ref[:, _hs(h)] = kv_ref[:, h * CAT_W + HEAD_DIM:(h + 1) * CAT_W].astype(BF16)

    k_cat, v_mla = _rowwise(kcat_body, seq, [(kv, True), (k_r, True), (g["g_k_nope"], False)],
                            [(N_MLA_HEADS * CAT_W, BF16), (MLA_W, BF16)], name="k_prep_fwd")

    p2 = _mm(h_b, w["b_in"], name="b_in")

    def qlat_body(ins, outs, _):
        outs[0][...] = _rms(ins[0][:, :Q_LORA], ins[1][...]).astype(BF16)

    (q_l,) = _rowwise(qlat_body, seq, [(p2, True), (g["b_g_q_lat"], False)], [(Q_LORA, BF16)],
                      name="q_lat_norm_fwd")
    q_up = _mm(q_l, w["uq"], name="uq")

    def qcat_body(ins, outs, _):
        q_ref, gn_ref, gr_ref, c_ref, s_ref = ins
        (o_ref,) = outs
        for h in range(N_MLA_HEADS):
            o_ref[:, h * CAT_W:h * CAT_W + HEAD_DIM] = (MLA_Q_SCALE * _rms(
                q_ref[:, h * CAT_W:h * CAT_W + HEAD_DIM], gn_ref[...])).astype(BF16)
            qr = _rms(q_ref[:, h * CAT_W + HEAD_DIM:(h + 1) * CAT_W], gr_ref[...], n=ROPE_DIM)
            o_ref[:, h * CAT_W + HEAD_DIM:(h + 1) * CAT_W] = (
                MLA_Q_SCALE * _rope(qr, c_ref[...], s_ref[...])).astype(BF16)

    (q_cat,) = _rowwise(qcat_body, seq,
                        [(q_up, True), (g["b_g_q_nope"], False), (g_q_rope, False), (cos_t, True), (sin_t, True)],
                        [(N_MLA_HEADS * CAT_W, BF16)], name="q_prep_fwd")
    att, lse = _mla_fwd(q_cat, k_cat, v_mla, name="mla_fwd")
    mem1 = _mem_side_fwd(mem, g["mem_norm"][1:2], w["mem_kv"][1], g["g_mem_k"][1:2], tag="b")
    mixed_b = _mix_fwd(att, p2, Q_LORA, mem1[2], mem1[3], g["g_mem_q"][1:2], name="b_mix_fwd")
    y = _mm(mixed_b, w["b_out"], res=x1, name="b_out")

    def loss_body(ins, outs, accs):
        diff = ins[0][...] - ins[1][...]
        outs[0][...] = diff / D_MODEL
        col = jnp.sum(diff * diff, axis=0, keepdims=True)
        part = col[:, :HEAD_DIM]
        for c in range(1, D_MODEL // HEAD_DIM):
            part = part + col[:, _hs(c)]
        accs[0][...] += part * (0.5 / D_MODEL)

    dy, loss_part = _rowwise(loss_body, seq, [(y, True), (target, True)], [(D_MODEL, F32)],
                             [((1, HEAD_DIM), F32)], name="loss")

    gw, gg = {}, {}
    dmixed_b = _mm(dy, w["b_out"], tb=True, name="b_out_dx")
    gw["b_out"] = _mm(mixed_b, dy, ta=True, out_dtype=BF16, name="b_out_dw")
    datt, dgate_b, dmk1, dmv1, gq1 = _mix_bwd(dmixed_b, att, p2, Q_LORA, mem1[2], mem1[3],
                                              g["g_mem_q"][1:2], name="b_mix_bwd")
    dq_cat, dk_cat, dv_mla = _mla_bwd(q_cat, k_cat, v_mla, att, lse, datt, name="mla_bwd")

    def qcat_bwd_body(ins, outs, accs):
        q_ref, dq_ref, gn_ref, gr_ref, c_ref, s_ref = ins
        (o_ref,) = outs
        dgn_ref, dgr_ref = accs
        for h in range(N_MLA_HEADS):
            dx, dg = _rms_bwd(q_ref[:, h * CAT_W:h * CAT_W + HEAD_DIM], gn_ref[...],
                              dq_ref[:, h * CAT_W:h * CAT_W + HEAD_DIM])
            o_ref[:, h * CAT_W:h * CAT_W + HEAD_DIM] = dx.astype(BF16)
            dgn_ref[...] += dg
            dn = _rope_bwd(dq_ref[:, h * CAT_W + HEAD_DIM:(h + 1) * CAT_W], c_ref[...], s_ref[...])
            dx, dg = _rms_bwd(q_ref[:, h * CAT_W + HEAD_DIM:(h + 1) * CAT_W], gr_ref[...], dn, n=ROPE_DIM)
            o_ref[:, h * CAT_W + HEAD_DIM:(h + 1) * CAT_W] = dx.astype(BF16)
            dgr_ref[...] += dg

    dq_up, gg["b_g_q_nope"], dgqr = _rowwise(
        qcat_bwd_body, seq,
        [(q_up, True), (dq_cat, True), (g["b_g_q_nope"], False), (g_q_rope, False), (cos_t, True), (sin_t, True)],
        [(N_MLA_HEADS * CAT_W, BF16)], [((1, HEAD_DIM), F32), ((1, HEAD_DIM), F32)], name="q_prep_bwd")
    gg["b_g_q_rope"] = dgqr
    dq_l = _mm(dq_up, w["uq"], tb=True, name="uq_dx")
    gw["uq"] = _mm(q_l, dq_up, ta=True, out_dtype=BF16, n_split=N_CHIPS, name="uq_dw")

    def qlat_bwd_body(ins, outs, accs):
        p2_ref, dql_ref, dgate_ref, gl_ref = ins
        dx, dg = _rms_bwd(p2_ref[:, :Q_LORA], gl_ref[...], dql_ref[...])
        outs[0][:, :Q_LORA] = dx.astype(BF16)
        outs[0][:, Q_LORA:] = dgate_ref[...]
        accs[0][...] += dg

    dp2, gg["b_g_q_lat"] = _rowwise(
        qlat_bwd_body, seq, [(p2, True), (dq_l, True), (dgate_b, True), (g["b_g_q_lat"], False)],
        [(Q_LORA + GATE_W, BF16)], [((1, Q_LORA), F32)], name="q_lat_norm_bwd")
    dh_b = _mm(dp2, w["b_in"], tb=True, name="b_in_dx")
    gw["b_in"] = _mm(h_b, dp2, ta=True, out_dtype=BF16, n_split=N_CHIPS, name="b_in_dw")

    def kcat_bwd_body(ins, outs, accs):
        kv_ref, dkc_ref, dv_ref, gk_ref = ins
        dkv_ref, dkr_ref = outs
        (dgk_ref,) = accs
        dkr = jnp.zeros(dkr_ref.shape, F32)
        for h in range(N_MLA_HEADS):
            dx, dg = _rms_bwd(kv_ref[:, h * CAT_W:h * CAT_W + HEAD_DIM], gk_ref[...],
                              dkc_ref[:, h * CAT_W:h * CAT_W + HEAD_DIM])
            dkv_ref[:, h * CAT_W:h * CAT_W + HEAD_DIM] = dx.astype(BF16)
            dgk_ref[...] += dg
            dkv_ref[:, h * CAT_W + HEAD_DIM:(h + 1) * CAT_W] = dv_ref[:, _hs(h)].astype(BF16)
            dkr = dkr + dkc_ref[:, h * CAT_W + HEAD_DIM:(h + 1) * CAT_W]
        dkr_ref[...] = dkr

    dkv, dk_r, gg["g_k_nope"] = _rowwise(
        kcat_bwd_body, seq, [(kv, True), (dk_cat, True), (dv_mla, True), (g["g_k_nope"], False)],
        [(N_MLA_HEADS * CAT_W, BF16), (HEAD_DIM, F32)], [((1, HEAD_DIM), F32)], name="k_prep_bwd")
    dc_n = _mm(dkv, w["ukv"], tb=True, name="ukv_dx")
    gw["ukv"] = _mm(c_n, dkv, ta=True, out_dtype=BF16, n_split=N_CHIPS, name="ukv_dw")

    def ckr_bwd_body(ins, outs, accs):
        ckr_ref, dcn_ref, dkr_ref, gc_ref, gr_ref, c_ref, s_ref = ins
        dx, dg = _rms_bwd(ckr_ref[:, :KV_LORA], gc_ref[...], dcn_ref[...])
        outs[0][:, :KV_LORA] = dx.astype(BF16)
        accs[0][...] += dg
        dn = _rope_bwd(dkr_ref[...], c_ref[...], s_ref[...])
        dx, dg = _rms_bwd(ckr_ref[:, KV_LORA:], gr_ref[...], dn, n=ROPE_DIM)
        outs[0][:, KV_LORA:] = dx.astype(BF16)
        accs[1][...] += dg

    dckr, gg["g_ckv"], gg["g_k_rope"] = _rowwise(
        ckr_bwd_body, seq,
        [(ckr, True), (dc_n, True), (dk_r, True), (g["g_ckv"], False), (g_k_rope, False),
         (cos_t, True), (sin_t, True)],
        [(KV_LORA + HEAD_DIM, BF16)], [((1, KV_LORA), F32), ((1, HEAD_DIM), F32)], name="ckv_prep_bwd")
    dh_kv = _mm(dckr, w["dkv"], tb=True, name="dkv_dx")
    gw["dkv"] = _mm(h_kv, dckr, ta=True, out_dtype=BF16, name="dkv_dw")

    def norms2_bwd_body(ins, outs, accs):
        x_ref, dy_ref, dhk_ref, dhb_ref, gk_ref, gb_ref = ins
        xv = x_ref[...]
        dxk, dgk = _rms_bwd(xv, gk_ref[...], dhk_ref[...])
        dxb, dgb = _rms_bwd(xv, gb_ref[...], dhb_ref[...])
        outs[0][...] = dy_ref[...] + dxk + dxb
        accs[0][...] += dgk
        accs[1][...] += dgb

    dx1, gg["kv_norm"], gg["b_norm"] = _rowwise(
        norms2_bwd_body, seq,
        [(x1, True), (dy, True), (dh_kv, True), (dh_b, True), (g["kv_norm"], False), (g["b_norm"], False)],
        [(D_MODEL, F32)], [((1, D_MODEL), F32), ((1, D_MODEL), F32)], name="kv_b_norm_bwd")

    dmixed_a = _mm(dx1, w["a_out"], tb=True, name="a_out_dx")
    gw["a_out"] = _mm(mixed_a, dx1, ta=True, out_dtype=BF16, name="a_out_dw")
    dsb, dgate_a, dmk0, dmv0, gq0 = _mix_bwd(dmixed_a, sb, gr, 0, mem0[2], mem0[3],
                                             g["g_mem_q"][0:1], name="a_mix_bwd")
    dq, dk, dv = _sb_bwd(qkv, sb, dsb, name="sb_bwd")
    dp_a = jnp.concatenate([dq.astype(BF16), dk.astype(BF16), dv.astype(BF16), dgate_a], axis=1)
    dh_a = _mm(dp_a, w["a_in"], tb=True, name="a_in_dx")
    gw["a_in"] = _mm(h_a, dp_a, ta=True, out_dtype=BF16, n_split=N_CHIPS, name="a_in_dw")

    def norm_a_bwd_body(ins, outs, accs):
        dx, dg = _rms_bwd(ins[0][...], ins[3][...], ins[2][...])
        outs[0][...] = ins[1][...] + dx
        accs[0][...] += dg

    grad_x, gg["a_norm"] = _rowwise(
        norm_a_bwd_body, seq, [(x, True), (dx1, True), (dh_a, True), (g["a_norm"], False)],
        [(D_MODEL, F32)], [((1, D_MODEL), F32)], name="a_norm_bwd")

    dw0, dgn0, dgk0 = _mem_side_bwd(mem, g["mem_norm"][0:1], w["mem_kv"][0], g["g_mem_k"][0:1],
                                    mem0[0], mem0[1], dmk0, dmv0, tag="a")
    dw1, dgn1, dgk1 = _mem_side_bwd(mem, g["mem_norm"][1:2], w["mem_kv"][1], g["g_mem_k"][1:2],
                                    mem1[0], mem1[1], dmk1, dmv1, tag="b")
    gw["mem_kv"] = (dw0, dw1)
    gg["mem_norm"] = jnp.concatenate([dgn0, dgn1], axis=0)
    gg["g_mem_q"] = jnp.concatenate([gq0, gq1], axis=0)
    gg["g_mem_k"] = jnp.concatenate([dgk0, dgk1], axis=0)
    return loss_part, grad_x, gw, gg


HBM_SPEC = pl.BlockSpec(memory_space=pl.ANY)


def _other_chips():
    x, y = lax.axis_index("x"), lax.axis_index("y")
    return [(1 - x, y), (x, 1 - y), (1 - x, 1 - y)]


def _allgather_chips(shards):
    n = len(shards)
    split = [s.shape[0] % 32 == 0 for s in shards]

    def body(*refs):
        ins, outs = refs[:n], refs[n:2 * n]
        send, recv, fsend, frecv = refs[2 * n:]
        x, y, c = lax.axis_index("x"), lax.axis_index("y"), lax.axis_index("c")
        me = 2 * x + y
        chips = _other_chips()

        def part(ref, wi):
            if not split[wi]:
                return ref
            half = shards[wi].shape[0] // 2
            return ref.at[pl.ds(pl.multiple_of(c * half, 16), half)]

        def ici(wi, k, src_chip, to):
            return pltpu.make_async_remote_copy(
                src_ref=part(ins[wi], wi), dst_ref=part(outs[wi].at[src_chip], wi),
                send_sem=send.at[wi, k], recv_sem=recv.at[wi, k], device_id=to, device_id_type=MESH)

        def d2d(wi, k, src_chip):
            rows = part(outs[wi].at[src_chip], wi)
            return pltpu.make_async_remote_copy(
                src_ref=rows, dst_ref=rows, send_sem=fsend.at[wi, k], recv_sem=frecv.at[wi, k],
                device_id=(x, y, 1 - c), device_id_type=MESH)

        for wi in range(n):
            for k, (tx, ty) in enumerate(chips):
                ici(wi, k, me, (tx, ty, c)).start()
        for wi in range(n):
            for k, (tx, ty) in enumerate(chips):
                landed = ici(wi, k, 2 * tx + ty, (tx, ty, c))
                landed.wait_recv()
                if split[wi]:
                    d2d(wi, k, 2 * tx + ty).start()
        for wi in range(n):
            for k, (tx, ty) in enumerate(chips):
                ici(wi, k, me, (tx, ty, c)).wait_send()
                if split[wi]:
                    fwd = d2d(wi, k, 2 * tx + ty)
                    fwd.wait_send()
                    fwd.wait_recv()

    return pl.pallas_call(
        body, out_shape=[jax.ShapeDtypeStruct((N_CHIPS,) + s.shape, s.dtype) for s in shards],
        in_specs=[HBM_SPEC] * n, out_specs=[HBM_SPEC] * n,
        scratch_shapes=[pltpu.SemaphoreType.DMA((n, 3)), pltpu.SemaphoreType.DMA((n, 3)),
                        pltpu.SemaphoreType.DMA((n, 3)), pltpu.SemaphoreType.DMA((n, 3))],
        name="allgather_weights",
    )(*shards)


def _scatter_to_chips(grads):
    n = len(grads)

    def body(*refs):
        ins, outs = refs[:n], refs[n:2 * n]
        send, recv = refs[2 * n:]
        c = lax.axis_index("c")
        copies = []
        for wi in range(n):
            for k, (tx, ty) in enumerate(_other_chips()):
                cp = pltpu.make_async_remote_copy(
                    src_ref=ins[wi].at[2 * tx + ty], dst_ref=outs[wi].at[k], send_sem=send.at[wi, k],
                    recv_sem=recv.at[wi, k], device_id=(tx, ty, c), device_id_type=MESH)
                cp.start()
                copies.append(cp)
        for cp in copies:
            cp.wait()

    return pl.pallas_call(
        body, out_shape=[jax.ShapeDtypeStruct((3,) + s.shape[1:], s.dtype) for s in grads],
        in_specs=[HBM_SPEC] * n, out_specs=[HBM_SPEC] * n,
        scratch_shapes=[pltpu.SemaphoreType.DMA((n, 3)), pltpu.SemaphoreType.DMA((n, 3))],
        name="scatter_grads",
    )(*grads)


def _halve_with_sibling(grads):
    n = len(grads)
    n_slots = grads[0].shape[0]

    def body(*refs):
        ins, got = refs[:n], refs[n:2 * n]
        send, recv = refs[2 * n:]
        c = lax.axis_index("c")
        sib = (lax.axis_index("x"), lax.axis_index("y"), 1 - c)
        copies = []
        for wi in range(n):
            half = grads[wi].shape[1] // 2
            for s in range(n_slots):
                theirs = ins[wi].at[s, pl.ds(pl.multiple_of((1 - c) * half, 16), half)]
                give = pltpu.make_async_remote_copy(
                    src_ref=theirs, dst_ref=got[wi].at[s], send_sem=send.at[wi, s], recv_sem=recv.at[wi, s],
                    device_id=sib, device_id_type=MESH)
                give.start()
                copies.append(give)
        for cp in copies:
            cp.wait()

    halves = [jax.ShapeDtypeStruct((s.shape[0], s.shape[1] // 2) + s.shape[2:], s.dtype) for s in grads]
    return pl.pallas_call(
        body, out_shape=halves, in_specs=[HBM_SPEC] * n, out_specs=[HBM_SPEC] * n,
        scratch_shapes=[pltpu.SemaphoreType.DMA((n, n_slots)), pltpu.SemaphoreType.DMA((n, n_slots))],
        name="halve_grads_with_sibling",
    )(*grads)


def _swap_with_sibling(parts):
    n = len(parts)

    def body(*refs):
        ins, outs = refs[:n], refs[n:2 * n]
        send, recv = refs[2 * n:]
        sib = (lax.axis_index("x"), lax.axis_index("y"), 1 - lax.axis_index("c"))
        copies = []
        for wi in range(n):
            cp = pltpu.make_async_remote_copy(
                src_ref=ins[wi], dst_ref=outs[wi], send_sem=send.at[wi], recv_sem=recv.at[wi],
                device_id=sib, device_id_type=MESH)
            cp.start()
            copies.append(cp)
        for cp in copies:
            cp.wait()

    return pl.pallas_call(
        body, out_shape=[jax.ShapeDtypeStruct(s.shape, s.dtype) for s in parts],
        in_specs=[HBM_SPEC] * n, out_specs=[HBM_SPEC] * n,
        scratch_shapes=[pltpu.SemaphoreType.DMA((n,)), pltpu.SemaphoreType.DMA((n,))],
        name="swap_grad_halves",
    )(*parts)


def _allreduce_small(vec, loss_row):
    rows = vec.shape[0]

    def body(v_ref, o_ref, buf, send, recv):
        x, y, c = lax.axis_index("x"), lax.axis_index("y"), lax.axis_index("c")
        me = 4 * x + 2 * y + c
        buf[me] = v_ref[...]
        copies = []
        for r in range(1, N_DEV):
            peer = (x ^ ((r >> 2) & 1), y ^ ((r >> 1) & 1), c ^ (r & 1))
            cp = pltpu.make_async_remote_copy(
                src_ref=v_ref, dst_ref=buf.at[me], send_sem=send.at[r - 1], recv_sem=recv.at[r - 1],
                device_id=peer, device_id_type=MESH)
            cp.start()
            copies.append(cp)
        for cp in copies:
            cp.wait()
        total = buf[0]
        for d in range(1, N_DEV):
            total = total + buf[d]
        o_ref[...] = total
        o_ref[loss_row:loss_row + 1, :] = jnp.broadcast_to(
            jnp.sum(total[loss_row:loss_row + 1, :], axis=-1, keepdims=True), (1, HEAD_DIM))

    return pl.pallas_call(
        body, out_shape=jax.ShapeDtypeStruct(vec.shape, F32),
        in_specs=[pl.BlockSpec(memory_space=pltpu.VMEM)], out_specs=pl.BlockSpec(memory_space=pltpu.VMEM),
        scratch_shapes=[pltpu.VMEM((N_DEV, rows, HEAD_DIM), F32),
                        pltpu.SemaphoreType.DMA((N_DEV - 1,)), pltpu.SemaphoreType.DMA((N_DEV - 1,))],
        name="allreduce_gains",
    )(vec)


def _pair_sum(grads, got, *, name):
    slots, rows, width = got.shape
    blk = _pick(rows, (256, 128, 64, 32, 16))
    nbh = rows // blk

    def body(lo_ref, hi_ref, got_ref, o_ref):
        mine = jnp.where(lax.axis_index("c") == 0, lo_ref[...], hi_ref[...])
        o_ref[...] = (mine.astype(F32) + got_ref[...].astype(F32)).astype(BF16)

    spec = pl.BlockSpec((None, blk, width), lambda s, i: (s, i, 0))
    return pl.pallas_call(
        body, out_shape=jax.ShapeDtypeStruct(got.shape, BF16), grid=(slots, nbh),
        in_specs=[spec, pl.BlockSpec((None, blk, width), lambda s, i: (s, nbh + i, 0)), spec],
        out_specs=spec, name=name, compiler_params=_params("parallel", "parallel"),
    )(grads, grads, got)


def _sum_slots(recv, chip_sum, *, name):
    _, rows, width = recv.shape
    blk = _pick(rows, (256, 128, 64, 32, 16, 8))

    def body(r_ref, p_ref, o_ref):
        me = 2 * lax.axis_index("x") + lax.axis_index("y")
        own = jnp.where(me < 2, jnp.where(me == 0, p_ref[0], p_ref[1]), jnp.where(me == 2, p_ref[2], p_ref[3]))
        o_ref[...] = ((own.astype(F32) + r_ref[0].astype(F32)) + r_ref[1].astype(F32)) + r_ref[2].astype(F32)

    return pl.pallas_call(
        body, out_shape=jax.ShapeDtypeStruct((rows, width), F32), grid=(rows // blk,),
        in_specs=[pl.BlockSpec((3, blk, width), lambda i: (0, i, 0)),
                  pl.BlockSpec((N_CHIPS, blk, width), lambda i: (0, i, 0))],
        out_specs=pl.BlockSpec((blk, width), lambda i: (i, 0)),
        name=name, compiler_params=_params("parallel"),
    )(recv, chip_sum)


def _adamw(wgt, grad, m, v, *, name, halves=None):
    rows, width = wgt.shape
    blk = _pick(rows // 2 if halves else rows, (256, 128, 64, 32, 16, 8))
    nbh = rows // 2 // blk

    def body(*refs):
        if halves:
            w_ref, mine_ref, theirs_ref, m_ref, v_ref, g_out, d_out, m_out, v_out = refs
            grad_v = jnp.where(pl.program_id(0) // nbh == lax.axis_index("c"), mine_ref[...], theirs_ref[...])
        else:
            w_ref, g_ref, m_ref, v_ref, g_out, d_out, m_out, v_out = refs
            grad_v = g_ref[...]
        m_new = ADAM_B1 * m_ref[...] + (1.0 - ADAM_B1) * grad_v
        v_new = ADAM_B2 * v_ref[...] + (1.0 - ADAM_B2) * (grad_v * grad_v)
        m_hat = m_new / (1.0 - ADAM_B1 ** ADAM_STEP)
        v_hat = v_new / (1.0 - ADAM_B2 ** ADAM_STEP)
        g_out[...] = grad_v
        d_out[...] = -ADAM_LR * (m_hat / (jnp.sqrt(v_hat) + ADAM_EPS) + ADAM_WD * w_ref[...])
        m_out[...] = m_new
        v_out[...] = v_new

    spec = pl.BlockSpec((blk, width), lambda i: (i, 0))
    half_spec = pl.BlockSpec((blk, width), lambda i: (i % nbh, 0))
    g_specs, g_args = ([half_spec, half_spec], list(halves)) if halves else ([spec], [grad])
    out = jax.ShapeDtypeStruct((rows, width), F32)
    return pl.pallas_call(
        body, out_shape=[out] * 4, grid=(rows // blk,), in_specs=[spec] + g_specs + [spec, spec],
        out_specs=[spec] * 4, name=name, compiler_params=_params("parallel"),
    )(wgt, *g_args, m, v)


_SMALL = (("a_norm", 2048), ("kv_norm", 2048), ("g_ckv", 512), ("g_k_nope", 128), ("g_k_rope", 64),
          ("b_norm", 2048), ("b_g_q_lat", 512), ("b_g_q_nope", 128), ("b_g_q_rope", 64),
          ("mem_norm", 4096), ("g_mem_q", 256), ("g_mem_k", 256))


def _lanes(n):
    return -(-n // HEAD_DIM) * HEAD_DIM


def _pack_rows(pieces, pad_rows_to=8):
    flat = jnp.concatenate(pieces, axis=1)
    rows = flat.shape[1] // HEAD_DIM
    pad = (-rows) % pad_rows_to
    if pad:
        flat = jnp.concatenate([flat, jnp.zeros((1, pad * HEAD_DIM), F32)], axis=1)
    return flat.reshape(rows + pad, HEAD_DIM)


def _pad_lanes(a):
    a = a.reshape(1, -1)
    pad = _lanes(a.shape[1]) - a.shape[1]
    if pad:
        a = jnp.concatenate([a, jnp.zeros((1, pad), F32)], axis=1)
    return a


def kernel(x, mem, positions, a_norm, a_w_in, a_w_out, kv_norm, w_dkv, g_ckv, w_ukv, g_k_nope, g_k_rope, b_norm, b_w_in, b_g_q_lat, b_w_uq, b_g_q_nope, b_g_q_rope, b_w_out, mem_norm, w_mem_kv, g_mem_q, g_mem_k, loss_target, m_a_norm, m_a_w_in, m_a_w_out, m_kv_norm, m_w_dkv, m_g_ckv, m_w_ukv, m_g_k_nope, m_g_k_rope, m_b_norm, m_b_w_in, m_b_g_q_lat, m_b_w_uq, m_b_g_q_nope, m_b_g_q_rope, m_b_w_out, m_mem_norm, m_w_mem_kv, m_g_mem_q, m_g_mem_k, v_a_norm, v_a_w_in, v_a_w_out, v_kv_norm, v_w_dkv, v_g_ckv, v_w_ukv, v_g_k_nope, v_g_k_rope, v_b_norm, v_b_w_in, v_b_g_q_lat, v_b_w_uq, v_b_g_q_nope, v_b_g_q_rope, v_b_w_out, v_mem_norm, v_w_mem_kv, v_g_mem_q, v_g_mem_k):
    chip = 2 * lax.axis_index("x") + lax.axis_index("y")
    rows_dkv = D_MODEL // N_CHIPS
    heads_per_chip = N_MLA_HEADS // N_CHIPS
    qk_w = HEAD_DIM + ROPE_DIM

    big = {"a_in": a_w_in[0], "a_out": a_w_out[0], "dkv": w_dkv, "ukv": w_ukv, "b_in": b_w_in[0],
           "uq": b_w_uq[0], "b_out": b_w_out[0], "mem_kv": w_mem_kv.reshape(2 * rows_dkv, 2 * MEM_W)}
    big_m = {"a_in": m_a_w_in[0], "a_out": m_a_w_out[0], "dkv": m_w_dkv, "ukv": m_w_ukv, "b_in": m_b_w_in[0],
             "uq": m_b_w_uq[0], "b_out": m_b_w_out[0], "mem_kv": m_w_mem_kv.reshape(2 * rows_dkv, 2 * MEM_W)}
    big_v = {"a_in": v_a_w_in[0], "a_out": v_a_w_out[0], "dkv": v_w_dkv, "ukv": v_w_ukv, "b_in": v_b_w_in[0],
             "uq": v_b_w_uq[0], "b_out": v_b_w_out[0], "mem_kv": v_w_mem_kv.reshape(2 * rows_dkv, 2 * MEM_W)}
    names = list(big)
    own_shards = [big[n].astype(BF16) for n in names] + [a_norm]
    gathered = _allgather_chips(own_shards)
    gathered = [lax.dynamic_update_slice(g, s[None], (chip,) + (0,) * s.ndim)
                for g, s in zip(gathered, own_shards)]
    st = dict(zip(names, gathered[:-1]))
    a_in_full = st["a_in"].transpose(1, 0, 2).reshape(D_MODEL, QKV_W + GATE_W)
    uq = st["uq"].reshape(N_CHIPS, Q_LORA, heads_per_chip, qk_w)
    uq = jnp.pad(uq, ((0, 0), (0, 0), (0, 0), (0, CAT_W - qk_w)))
    w = {
        "a_in": a_in_full,
        "a_in_qkv": a_in_full[:, :QKV_W],
        "a_in_gate": a_in_full[:, QKV_W:],
        "a_out": st["a_out"].reshape(D_MODEL, D_MODEL),
        "dkv": jnp.pad(st["dkv"].reshape(D_MODEL, KV_LORA + ROPE_DIM), ((0, 0), (0, HEAD_DIM - ROPE_DIM))),
        "ukv": st["ukv"].transpose(1, 0, 2).reshape(KV_LORA, N_MLA_HEADS * CAT_W),
        "b_in": st["b_in"].transpose(1, 0, 2).reshape(D_MODEL, Q_LORA + GATE_W),
        "uq": uq.transpose(1, 0, 2, 3).reshape(Q_LORA, N_MLA_HEADS * CAT_W),
        "b_out": st["b_out"].reshape(D_MODEL, D_MODEL),
        "mem_kv": st["mem_kv"].reshape(N_CHIPS, 2, rows_dkv, 2 * MEM_W).transpose(1, 0, 2, 3).reshape(
            2, D_MODEL, 2 * MEM_W),
    }
    gains = {
        "a_norm": gathered[-1].reshape(1, D_MODEL), "kv_norm": kv_norm.reshape(1, -1),
        "g_ckv": g_ckv.reshape(1, -1), "g_k_nope": g_k_nope.reshape(1, -1), "g_k_rope": g_k_rope.reshape(1, -1),
        "b_norm": b_norm, "b_g_q_lat": b_g_q_lat, "b_g_q_nope": b_g_q_nope, "b_g_q_rope": b_g_q_rope,
        "mem_norm": mem_norm, "g_mem_q": g_mem_q, "g_mem_k": g_mem_k,
    }

    loss_part, grad_x, gw, gg = _local_step(x[0], mem[0], positions[0], loss_target[0], w, gains)

    stacked = {
        "a_in": gw["a_in"],
        "a_out": gw["a_out"].reshape(N_CHIPS, rows_dkv, D_MODEL),
        "dkv": gw["dkv"][:, :KV_LORA + ROPE_DIM].reshape(N_CHIPS, rows_dkv, KV_LORA + ROPE_DIM),
        "ukv": gw["ukv"],
        "b_in": gw["b_in"],
        "uq": gw["uq"].reshape(N_CHIPS, Q_LORA, heads_per_chip, CAT_W)[..., :qk_w].reshape(
            N_CHIPS, Q_LORA, heads_per_chip * qk_w),
        "b_out": gw["b_out"].reshape(N_CHIPS, rows_dkv, D_MODEL),
        "mem_kv": jnp.stack([gw["mem_kv"][0].reshape(N_CHIPS, rows_dkv, 2 * MEM_W),
                             gw["mem_kv"][1].reshape(N_CHIPS, rows_dkv, 2 * MEM_W)], axis=1).reshape(
            N_CHIPS, 2 * rows_dkv, 2 * MEM_W),
    }
    got = _halve_with_sibling([stacked[n] for n in names])
    chip_sum = [_pair_sum(stacked[n], g, name=f"pair_sum_{n}") for n, g in zip(names, got)]
    received = _scatter_to_chips(chip_sum)
    half_total = [_sum_slots(r, p, name=f"sum_slots_{n}") for n, r, p in zip(names, received, chip_sum)]
    sibling_half = _swap_with_sibling(half_total)
    big_out = {}
    for n, mine, theirs in zip(names, half_total, sibling_half):
        big_out[n] = _adamw(big[n], None, big_m[n], big_v[n], halves=(mine, theirs), name=f"adamw_{n}")

    pieces = [_pad_lanes(gg[n]) if n not in ("g_k_rope", "b_g_q_rope") else gg[n] for n, _ in _SMALL]
    pieces.append(loss_part)
    loss_row = sum(_lanes(size) for _, size in _SMALL) // HEAD_DIM
    summed = _allreduce_small(_pack_rows(pieces), loss_row)
    flat = summed.reshape(1, -1)
    small_g, off = {}, 0
    for n, size in _SMALL:
        small_g[n] = flat[:, off:off + size]
        off += _lanes(size)
    loss = flat[0, off]
    small_g["a_norm"] = lax.dynamic_slice(small_g["a_norm"], (0, chip * rows_dkv), (1, rows_dkv))

    small_w = {"a_norm": a_norm, "kv_norm": kv_norm, "g_ckv": g_ckv, "g_k_nope": g_k_nope, "g_k_rope": g_k_rope,
               "b_norm": b_norm, "b_g_q_lat": b_g_q_lat, "b_g_q_nope": b_g_q_nope, "b_g_q_rope": b_g_q_rope,
               "mem_norm": mem_norm, "g_mem_q": g_mem_q, "g_mem_k": g_mem_k}
    small_m = {"a_norm": m_a_norm, "kv_norm": m_kv_norm, "g_ckv": m_g_ckv, "g_k_nope": m_g_k_nope,
               "g_k_rope": m_g_k_rope, "b_norm": m_b_norm, "b_g_q_lat": m_b_g_q_lat, "b_g_q_nope": m_b_g_q_nope,
               "b_g_q_rope": m_b_g_q_rope, "mem_norm": m_mem_norm, "g_mem_q": m_g_mem_q, "g_mem_k": m_g_mem_k}
    small_v = {"a_norm": v_a_norm, "kv_norm": v_kv_norm, "g_ckv": v_g_ckv, "g_k_nope": v_g_k_nope,
               "g_k_rope": v_g_k_rope, "b_norm": v_b_norm, "b_g_q_lat": v_b_g_q_lat, "b_g_q_nope": v_b_g_q_nope,
               "b_g_q_rope": v_b_g_q_rope, "mem_norm": v_mem_norm, "g_mem_q": v_g_mem_q, "g_mem_k": v_g_mem_k}
    snames = [n for n, _ in _SMALL]
    packs = [_pack_rows([_pad_lanes(src[n]) for n in snames])
             for src in (small_w, small_g, small_m, small_v)]
    small_res = _adamw(packs[0], packs[1], packs[2], packs[3], name="adamw_gains")
    small_out = {n: [] for n in snames}
    for res in small_res:
        flat_r = res.reshape(1, -1)
        off = 0
        for n in snames:
            size = small_w[n].size
            small_out[n].append(flat_r[:, off:off + size].reshape(small_w[n].shape))
            off += _lanes(size)

    big_names = {"a_w_in": ("a_in", a_w_in), "a_w_out": ("a_out", a_w_out), "w_dkv": ("dkv", w_dkv),
                 "w_ukv": ("ukv", w_ukv), "b_w_in": ("b_in", b_w_in), "b_w_uq": ("uq", b_w_uq),
                 "b_w_out": ("b_out", b_w_out), "w_mem_kv": ("mem_kv", w_mem_kv)}
    order = ["a_norm", "a_w_in", "a_w_out", "kv_norm", "w_dkv", "g_ckv", "w_ukv", "g_k_nope", "g_k_rope",
             "b_norm", "b_w_in", "b_g_q_lat", "b_w_uq", "b_g_q_nope", "b_g_q_rope", "b_w_out", "mem_norm",
             "w_mem_kv", "g_mem_q", "g_mem_k"]
    groups = [[], [], [], []]
    for n in order:
        if n in big_names:
            key, ref_arr = big_names[n]
            for t in range(4):
                groups[t].append(big_out[key][t].reshape(ref_arr.shape))
        else:
            for t in range(4):
                groups[t].append(small_out[n][t])
    return (loss, grad_x[None], *groups[0], *groups[1], *groups[2], *groups[3])
```

```python
import functools

import jax
import jax.numpy as jnp
from jax import lax
from jax.experimental import pallas as pl
from jax.experimental.pallas import tpu as pltpu

F32 = jnp.float32
BF16 = jnp.bfloat16
MESH = pl.DeviceIdType.MESH

D_MODEL = 2048
HEAD_DIM = 128
N_SB_HEADS = 12
N_MEM_HEADS = 4
N_MLA_HEADS = 12
MEM_LEN = 256
Q_LORA = 512
KV_LORA = 512
ROPE_DIM = 64
SB_W = N_SB_HEADS * HEAD_DIM
MEM_W = N_MEM_HEADS * HEAD_DIM
MLA_W = N_MLA_HEADS * HEAD_DIM
QKV_W = 3 * SB_W
GATE_W = SB_W + 2 * MEM_W
CAT_W = 2 * HEAD_DIM
ROPE_THETA = 10000.0
EPS = 1e-6
N_CHIPS = 4
N_DEV = 8

ADAM_LR = 0.001
ADAM_B1 = 0.9
ADAM_B2 = 0.999
ADAM_EPS = 1e-08
ADAM_WD = 0.01
ADAM_STEP = 10

VMEM_LIMIT_BYTES = 56 * 1024 * 1024
MM_OPERAND_VMEM_BYTES = 24 * 1024 * 1024
ROW_BLOCK = 256
ATT_BLOCK = 256


def _params(*sem):
    return pltpu.CompilerParams(dimension_semantics=sem, vmem_limit_bytes=VMEM_LIMIT_BYTES)


def _pick(n, cands):
    for c in cands:
        if n % c == 0:
            return c
    return n


def _mm(a, b, *, name, ta=False, tb=False, out_dtype=F32, res=None, n_split=1, scale_cols=None):
    if ta:
        k_dim, m_dim = a.shape
    else:
        m_dim, k_dim = a.shape
    if tb:
        n_dim, kb = b.shape
    else:
        kb, n_dim = b.shape
    assert kb == k_dim, (a.shape, b.shape)
    n_per = n_dim // n_split
    bm = m_dim if m_dim <= 1024 else _pick(m_dim, (1024, 512, 256))
    bn = n_per if n_per <= 1024 else _pick(n_per, (1024, 896, 768, 640, 512, 256, 128))
    per_k = (bm * a.dtype.itemsize + bn * b.dtype.itemsize) * 2
    bk = next((c for c in (k_dim, 2048, 1024, 512, 256, 128)
               if c <= k_dim and k_dim % c == 0 and c * per_k <= MM_OPERAND_VMEM_BYTES), 128)
    nk = k_dim // bk
    nb_per = n_per // bn
    grid = (m_dim // bm, n_dim // bn, nk)
    a_spec = (pl.BlockSpec((bk, bm), lambda i, j, k: (k, i)) if ta
              else pl.BlockSpec((bm, bk), lambda i, j, k: (i, k)))
    b_spec = (pl.BlockSpec((bn, bk), lambda i, j, k: (j, k)) if tb
              else pl.BlockSpec((bk, bn), lambda i, j, k: (k, j)))
    dims = (((0 if ta else 1,), (1 if tb else 0,)), ((), ()))
    in_specs = [a_spec, b_spec]
    args = [a, b]
    if res is not None:
        in_specs.append(pl.BlockSpec((bm, bn), lambda i, j, k: (i, j)))
        args.append(res)
    if n_split == 1:
        out_shape = jax.ShapeDtypeStruct((m_dim, n_dim), out_dtype)
        out_spec = pl.BlockSpec((bm, bn), lambda i, j, k: (i, j))
    else:
        out_shape = jax.ShapeDtypeStruct((n_split, m_dim, n_per), out_dtype)
        out_spec = pl.BlockSpec((None, bm, bn), lambda i, j, k: (j // nb_per, i, j % nb_per))

    def body(*refs):
        if res is None:
            a_ref, b_ref, o_ref, acc = refs
            r_ref = None
        else:
            a_ref, b_ref, r_ref, o_ref, acc = refs
        k = pl.program_id(2)
        col_block = pl.program_id(1)

        @pl.when(k == 0)
        def _():
            acc[...] = jnp.zeros_like(acc)

        acc[...] += lax.dot_general(a_ref[...].astype(BF16), b_ref[...].astype(BF16), dims,
                                    preferred_element_type=F32)

        @pl.when(k == nk - 1)
        def _():
            r = acc[...]
            if r_ref is not None:
                r = r + r_ref[...]
            if scale_cols is not None:
                assert scale_cols[0] % bn == 0
                r = r * jnp.where(col_block < scale_cols[0] // bn, scale_cols[1], 1.0)
            o_ref[...] = r.astype(out_dtype)

    return pl.pallas_call(
        body, out_shape=out_shape, grid=grid, in_specs=in_specs, out_specs=out_spec,
        scratch_shapes=[pltpu.VMEM((bm, bn), F32)], name=name,
        compiler_params=_params("parallel", "parallel", "arbitrary"),
    )(*args)


def _rowwise(body, n_rows, ins, outs, accs=(), *, name, block=ROW_BLOCK):
    blk = min(block, n_rows)
    assert n_rows % blk == 0
    in_specs = []
    for arr, is_row in ins:
        if is_row:
            assert arr.shape[0] == n_rows, (name, arr.shape, n_rows)
            in_specs.append(pl.BlockSpec((blk, arr.shape[1]), lambda i: (i, 0)))
        else:
            in_specs.append(pl.BlockSpec(arr.shape, lambda i, nd=arr.ndim: (0,) * nd))
    out_shape = [jax.ShapeDtypeStruct((n_rows, w), dt) for w, dt in outs]
    out_specs = [pl.BlockSpec((blk, w), lambda i: (i, 0)) for w, _ in outs]
    out_shape += [jax.ShapeDtypeStruct(s, dt) for s, dt in accs]
    out_specs += [pl.BlockSpec(s, lambda i, nd=len(s): (0,) * nd) for s, _ in accs]
    n_in, n_out, n_acc = len(ins), len(outs), len(accs)

    def kern(*refs):
        in_refs = refs[:n_in]
        out_refs = refs[n_in:n_in + n_out]
        acc_refs = refs[n_in + n_out:]
        if n_acc:
            @pl.when(pl.program_id(0) == 0)
            def _():
                for r in acc_refs:
                    r[...] = jnp.zeros_like(r)
        body(in_refs, out_refs, acc_refs)

    return pl.pallas_call(
        kern, out_shape=out_shape, grid=(n_rows // blk,), in_specs=in_specs, out_specs=out_specs,
        name=name, compiler_params=_params("arbitrary"),
    )(*[arr for arr, _ in ins])


def _rms(x, g, n=None):
    n = x.shape[-1] if n is None else n
    r = lax.rsqrt(jnp.sum(x * x, axis=-1, keepdims=True) / n + EPS)
    return x * r * g


def _rms_bwd(x, g, dy, n=None):
    n = x.shape[-1] if n is None else n
    r = lax.rsqrt(jnp.sum(x * x, axis=-1, keepdims=True) / n + EPS)
    gdy = dy * g
    dx = r * (gdy - x * ((r * r) * (jnp.sum(gdy * x, axis=-1, keepdims=True) / n)))
    dg = jnp.sum(dy * x * r, axis=0, keepdims=True)
    return dx, dg


def _swap_halves(x):
    lane = lax.broadcasted_iota(jnp.int32, x.shape, 1)
    return jnp.where(lane < ROPE_DIM // 2, pltpu.roll(x, 128 - ROPE_DIM // 2, 1),
                     pltpu.roll(x, ROPE_DIM // 2, 1))


def _rope(n, cos_t, sin_t):
    return n * cos_t + _swap_halves(n) * sin_t


def _rope_bwd(dy, cos_t, sin_t):
    return dy * cos_t - _swap_halves(dy) * sin_t


def _sigmoid(g):
    return 1.0 / (1.0 + jnp.exp(-g))


def _dot_t(a, b):
    return lax.dot_general(a, b, (((1,), (1,)), ((), ())), preferred_element_type=F32)


def _tdot(a, b):
    return lax.dot_general(a, b, (((0,), (0,)), ((), ())), preferred_element_type=F32)


def _dot(a, b):
    return jnp.dot(a, b, preferred_element_type=F32)


def _hs(h, w=HEAD_DIM, base=0):
    return slice(base + h * w, base + (h + 1) * w)


def _mem_head(qm, gq, mk_h, mv_h):
    qb = _rms(qm, gq).astype(BF16)
    s = _dot_t(qb, mk_h) * (HEAD_DIM ** -0.5)
    e = jnp.exp(s - jnp.max(s, axis=-1, keepdims=True))
    p = e / jnp.sum(e, axis=-1, keepdims=True)
    mo = _dot(p.astype(BF16), mv_h)
    return qb, p, mo


def _mix_fwd(att, gates, c0, mk, mv, gq, *, name):
    n_rows = att.shape[0]

    def body(ins, outs, _):
        att_ref, g_ref, mk_ref, mv_ref, gq_ref = ins
        (o_ref,) = outs
        g = g_ref[:, c0:c0 + SB_W]
        o_ref[:, :SB_W] = (att_ref[...] * (g * _sigmoid(g))).astype(BF16)
        for h in range(N_MEM_HEADS):
            qm = g_ref[:, _hs(h, base=c0 + SB_W)]
            gm = g_ref[:, _hs(h, base=c0 + SB_W + MEM_W)]
            _, _, mo = _mem_head(qm, gq_ref[...], mk_ref[:, _hs(h)], mv_ref[:, _hs(h)])
            o_ref[:, _hs(h, base=SB_W)] = (mo * (gm * _sigmoid(gm))).astype(BF16)

    (mixed,) = _rowwise(body, n_rows,
                        [(att, True), (gates, True), (mk, False), (mv, False), (gq, False)],
                        [(D_MODEL, BF16)], name=name)
    return mixed


def _mix_bwd(dmixed, att, gates, c0, mk, mv, gq, *, name):
    n_rows = att.shape[0]
    scale = HEAD_DIM ** -0.5

    def body(ins, outs, accs):
        dm_ref, att_ref, g_ref, mk_ref, mv_ref, gq_ref = ins
        datt_ref, dg_ref = outs
        dmk_ref, dmv_ref, dgq_ref = accs
        g = g_ref[:, c0:c0 + SB_W]
        sg = _sigmoid(g)
        dm = dm_ref[:, :SB_W]
        datt_ref[...] = dm * (g * sg)
        dg_ref[:, :SB_W] = (dm * att_ref[...] * (sg * (1.0 + g * (1.0 - sg)))).astype(BF16)
        for h in range(N_MEM_HEADS):
            qm = g_ref[:, _hs(h, base=c0 + SB_W)]
            gm = g_ref[:, _hs(h, base=c0 + SB_W + MEM_W)]
            mk_h = mk_ref[:, _hs(h)]
            mv_h = mv_ref[:, _hs(h)]
            qb, p, mo = _mem_head(qm, gq_ref[...], mk_h, mv_h)
            sgm = _sigmoid(gm)
            dmh = dm_ref[:, _hs(h, base=SB_W)]
            dmo = dmh * (gm * sgm)
            dg_ref[:, _hs(h, base=SB_W + MEM_W)] = (
                dmh * mo * (sgm * (1.0 + gm * (1.0 - sgm)))).astype(BF16)
            dmo_b = dmo.astype(BF16)
            pb = p.astype(BF16)
            dp = _dot_t(dmo_b, mv_h)
            dmv_ref[:, _hs(h)] += _tdot(pb, dmo_b)
            ds = (p * (dp - jnp.sum(dp * p, axis=-1, keepdims=True)) * scale).astype(BF16)
            dqn = _dot(ds, mk_h)
            dmk_ref[:, _hs(h)] += _tdot(ds, qb)
            dqm, dgq = _rms_bwd(qm, gq_ref[...], dqn)
            dg_ref[:, _hs(h, base=SB_W)] = dqm.astype(BF16)
            dgq_ref[...] += dgq

    return _rowwise(body, n_rows,
                    [(dmixed, True), (att, True), (gates, True), (mk, False), (mv, False), (gq, False)],
                    [(SB_W, F32), (GATE_W, BF16)],
                    [((MEM_LEN, MEM_W), F32), ((MEM_LEN, MEM_W), F32), ((1, HEAD_DIM), F32)],
                    name=name)


def _mem_side_fwd(mem, g_norm, w_kv, g_k, *, tag):
    def norm_body(ins, outs, _):
        outs[0][...] = _rms(ins[0][...], ins[1][...]).astype(BF16)

    (mn,) = _rowwise(norm_body, MEM_LEN, [(mem, True), (g_norm, False)], [(D_MODEL, BF16)],
                     name=f"mem_norm_{tag}")
    mkv = _mm(mn, w_kv, name=f"mem_kv_{tag}")

    def kv_body(ins, outs, _):
        mkv_ref, gk_ref = ins
        mk_ref, mv_ref = outs
        for h in range(N_MEM_HEADS):
            mk_ref[:, _hs(h)] = _rms(mkv_ref[:, _hs(h)], gk_ref[...]).astype(BF16)
        mv_ref[...] = mkv_ref[:, MEM_W:].astype(BF16)

    mk, mv = _rowwise(kv_body, MEM_LEN, [(mkv, True), (g_k, False)], [(MEM_W, BF16), (MEM_W, BF16)],
                      name=f"mem_kv_prep_{tag}")
    return mn, mkv, mk, mv


def _mem_side_bwd(mem, g_norm, w_kv, g_k, mn, mkv, dmk, dmv, *, tag):
    def kv_body(ins, outs, accs):
        mkv_ref, gk_ref, dmk_ref, dmv_ref = ins
        (d_ref,) = outs
        (dgk_ref,) = accs
        for h in range(N_MEM_HEADS):
            dx, dg = _rms_bwd(mkv_ref[:, _hs(h)], gk_ref[...], dmk_ref[:, _hs(h)])
            d_ref[:, _hs(h)] = dx.astype(BF16)
            dgk_ref[...] += dg
        d_ref[:, MEM_W:] = dmv_ref[...].astype(BF16)

    dmkv, dgk = _rowwise(kv_body, MEM_LEN, [(mkv, True), (g_k, False), (dmk, True), (dmv, True)],
                         [(2 * MEM_W, BF16)], [((1, HEAD_DIM), F32)], name=f"mem_kv_prep_bwd_{tag}")
    dmn = _mm(dmkv, w_kv, tb=True, name=f"mem_kv_dx_{tag}")
    dw = _mm(mn, dmkv, ta=True, out_dtype=BF16, name=f"mem_kv_dw_{tag}")

    def norm_body(ins, outs, accs):
        _, dg = _rms_bwd(ins[0][...], ins[1][...], ins[2][...])
        accs[0][...] += dg

    (dgn,) = _rowwise(norm_body, MEM_LEN, [(mem, True), (g_norm, False), (dmn, True)], [],
                      [((1, D_MODEL), F32)], name=f"mem_norm_bwd_{tag}")
    return dw, dgn, dgk


LOG2_E = 1.4426950408889634
SB_Q_SCALE = HEAD_DIM ** -0.5 * LOG2_E


Z2_CAP = 126.0


def _sb_terms(z2):
    zc = jnp.minimum(z2, Z2_CAP)
    w = 1.0 + jnp.exp2(zc)
    return zc, w, jnp.log2(w)


LOOP_UNROLL = 4


def _loop_blocks(base, qb, body, carry, *, reverse):
    def run(start, trips, unroll, c0):
        def trip(t, c):
            for u in range(unroll):
                p = start + t * unroll + u
                c = body(base - 1 - p if reverse else p, c)
            return c
        return lax.fori_loop(0, trips, trip, c0)

    if qb % LOOP_UNROLL == 0:
        return run(0, base // LOOP_UNROLL, LOOP_UNROLL, carry)
    small = 2 if qb % 2 == 0 else 1
    n_big = base // LOOP_UNROLL
    carry = run(0, n_big, LOOP_UNROLL, carry)
    return run(n_big * LOOP_UNROLL, (base - n_big * LOOP_UNROLL) // small, small, carry)


def _chain_modes(s, qb):
    return tuple(None if t < s else ("m" if t == s else "f") for t in range(qb))


def _split_dot(x, tri2):
    hi = x.astype(BF16)
    lo = (x - hi.astype(F32)).astype(BF16)
    return _dot(jnp.concatenate([hi, lo], axis=1), tri2)


def _sb_fwd(qkv, *, name, hp=2):
    seq = qkv.shape[0]
    blk = min(ATT_BLOCK, seq)
    nkb = seq // blk
    qb = _pick(nkb, (2, 1))
    rows = qb * blk
    chains =[(t, s) for t in range(hp) for s in range(qb)]

    def body(q_ref, k_ref, v_ref, o_ref):
        base = pl.program_id(1) * qb
        qs = {(t, s): q_ref[s * blk:(s + 1) * blk, _hs(t)] for t, s in chains}
        row = lax.broadcasted_iota(jnp.int32, (blk, blk), 0)
        col = lax.broadcasted_iota(jnp.int32, (blk, blk), 1)
        after = (row > col).astype(BF16)
        after2 = jnp.concatenate([after, after], axis=0)
        causal = col < row

        def step(j, carry, modes):
            off = pl.multiple_of(j * blk, blk)
            act = [c for c in chains if modes[c[1]]]
            zs, ls = {}, {}
            for c in act:
                zs[c], _, l = _sb_terms(_dot_t(qs[c], k_ref[pl.ds(off, blk), _hs(c[0])]))
                ls[c] = jnp.where(causal, l, 0.0) if modes[c[1]] == "m" else l
            cs = {c: _split_dot(ls[c], after2) for c in act}
            carry = dict(carry)
            for c in act:
                run, acc = carry[c]
                a = jnp.exp2(zs[c] - ls[c] - cs[c] - run)
                if modes[c[1]] == "m":
                    a = jnp.where(causal, a, 0.0)
                acc = acc + _dot(a.astype(BF16), v_ref[pl.ds(off, blk), _hs(c[0])])
                carry[c] = (run + jnp.sum(ls[c], axis=-1, keepdims=True), acc)
            return carry

        init = (jnp.zeros((blk, 1), F32), jnp.zeros((blk, HEAD_DIM), F32))
        carry = {c: init for c in chains}
        for s in reversed(range(qb)):
            carry = step(base + s, carry, _chain_modes(s, qb))
        carry = _loop_blocks(base, qb, lambda j, c: step(j, c, ("f",) * qb), carry, reverse=True)
        for t, s in chains:
            o_ref[s * blk:(s + 1) * blk, _hs(t)] = carry[(t, s)][1]

    nh = N_SB_HEADS // hp
    return pl.pallas_call(
        body, out_shape=jax.ShapeDtypeStruct((seq, SB_W), F32), grid=(nh, nkb // qb),
        in_specs=[pl.BlockSpec((rows, hp * HEAD_DIM), lambda h, i: (i, h)),
                  pl.BlockSpec((seq, hp * HEAD_DIM), lambda h, i: (0, nh + h)),
                  pl.BlockSpec((seq, hp * HEAD_DIM), lambda h, i: (0, 2 * nh + h))],
        out_specs=pl.BlockSpec((rows, hp * HEAD_DIM), lambda h, i: (i, h)),
        name=name, compiler_params=_params("parallel", "arbitrary"),
    )(qkv, qkv, qkv)


SB_BWD_GROUP = 4


def _sb_bwd(qkv, out, dout, *, name):
    seq = qkv.shape[0]
    blk = min(ATT_BLOCK, seq)
    nkb = seq // blk
    qb = _pick(nkb, (4, 2, 1))
    rows = qb * blk
    scale = HEAD_DIM ** -0.5

    def body(q_ref, k_ref, v_ref, do_ref, o_ref, dq_ref, dk_ref, dv_ref):
        g = pl.program_id(1)
        base = g * qb

        @pl.when(g == 0)
        def _():
            dk_ref[...] = jnp.zeros_like(dk_ref)
            dv_ref[...] = jnp.zeros_like(dv_ref)

        qs = [q_ref[t * blk:(t + 1) * blk, :] for t in range(qb)]
        dos = [do_ref[t * blk:(t + 1) * blk, :].astype(BF16) for t in range(qb)]
        totals = [jnp.sum(dos[t].astype(F32) * o_ref[t * blk:(t + 1) * blk, :], axis=-1, keepdims=True)
                  for t in range(qb)]
        row = lax.broadcasted_iota(jnp.int32, (blk, blk), 0)
        col = lax.broadcasted_iota(jnp.int32, (blk, blk), 1)
        after = (row > col).astype(BF16)
        after2 = jnp.concatenate([after, after], axis=0)
        from_s = (row >= col).astype(BF16)
        from_s2 = jnp.concatenate([from_s, from_s], axis=0)
        causal = col < row

        def step(j, carry, modes):
            runs, rights, dqs = list(carry[0]), list(carry[1]), list(carry[2])
            off = pl.multiple_of(j * blk, blk)
            kb = k_ref[pl.ds(off, blk), :]
            vb = v_ref[pl.ds(off, blk), :]
            dv_inc = dk_inc = None
            for first in range(0, qb, SB_BWD_GROUP):
                act = [t for t in range(first, min(first + SB_BWD_GROUP, qb)) if modes[t]]
                das = {t: _dot_t(dos[t], vb) for t in act}
                zs, ls, sns = {}, {}, {}
                for t in act:
                    zs[t], w, l = _sb_terms(_dot_t(qs[t], kb))
                    sns[t] = pl.reciprocal(w, approx=True)
                    ls[t] = jnp.where(causal, l, 0.0) if modes[t] == "m" else l
                cs = {t: _split_dot(ls[t], after2) for t in act}
                abs_, des = {}, {}
                for t in act:
                    a = jnp.exp2(zs[t] - ls[t] - cs[t] - runs[t])
                    if modes[t] == "m":
                        a = jnp.where(causal, a, 0.0)
                    abs_[t] = a.astype(BF16)
                    des[t] = abs_[t].astype(F32) * das[t]
                sufs = {t: _split_dot(des[t], from_s2) for t in act}
                for t in act:
                    left = totals[t] - (sufs[t] + rights[t])
                    dz = (des[t] + left) * sns[t] - left
                    if modes[t] == "m":
                        dz = jnp.where(causal, dz, 0.0)
                    dzb = dz.astype(BF16)
                    dqs[t] = dqs[t] + _dot(dzb, kb)
                    inc_v = _tdot(abs_[t], dos[t])
                    inc_k = _tdot(dzb, qs[t])
                    dv_inc = inc_v if dv_inc is None else dv_inc + inc_v
                    dk_inc = inc_k if dk_inc is None else dk_inc + inc_k
                    runs[t] = runs[t] + jnp.sum(ls[t], axis=-1, keepdims=True)
                    rights[t] = rights[t] + jnp.sum(des[t], axis=-1, keepdims=True)
            dv_ref[pl.ds(off, blk), :] += dv_inc
            dk_ref[pl.ds(off, blk), :] += dk_inc
            return tuple(runs), tuple(rights), tuple(dqs)

        zero = (jnp.zeros((blk, 1), F32),) * qb
        carry = (zero, zero, (jnp.zeros((blk, HEAD_DIM), F32),) * qb)
        for s in reversed(range(qb)):
            carry = step(base + s, carry, _chain_modes(s, qb))
        carry = _loop_blocks(base, qb, lambda j, c: step(j, c, ("f",) * qb), carry, reverse=True)
        for t in range(qb):
            dq_ref[t * blk:(t + 1) * blk, :] = carry[2][t] * scale

        @pl.when(g == pl.num_programs(1) - 1)
        def _():
            dk_ref[...] = dk_ref[...] * (1.0 / LOG2_E)

    out_sd = jax.ShapeDtypeStruct((seq, SB_W), F32)
    return pl.pallas_call(
        body, out_shape=[out_sd, out_sd, out_sd], grid=(N_SB_HEADS, nkb // qb),
        in_specs=[pl.BlockSpec((rows, HEAD_DIM), lambda h, i: (i, h)),
                  pl.BlockSpec((seq, HEAD_DIM), lambda h, i: (0, N_SB_HEADS + h)),
                  pl.BlockSpec((seq, HEAD_DIM), lambda h, i: (0, 2 * N_SB_HEADS + h)),
                  pl.BlockSpec((rows, HEAD_DIM), lambda h, i: (i, h)),
                  pl.BlockSpec((rows, HEAD_DIM), lambda h, i: (i, h))],
        out_specs=[pl.BlockSpec((rows, HEAD_DIM), lambda h, i: (i, h)),
                   pl.BlockSpec((seq, HEAD_DIM), lambda h, i: (0, h)),
                   pl.BlockSpec((seq, HEAD_DIM), lambda h, i: (0, h))],
        name=name, compiler_params=_params("parallel", "arbitrary"),
    )(qkv, qkv, qkv, dout, out)


MLA_SCALE = (HEAD_DIM + ROPE_DIM) ** -0.5
MLA_Q_SCALE = MLA_SCALE * LOG2_E


def _mla_fwd(q_cat, k_cat, v, *, name, hp=2):
    seq = q_cat.shape[0]
    blk = min(ATT_BLOCK, seq)
    nkb = seq // blk
    qb = _pick(nkb, (2, 1))
    rows = qb * blk
    chains = [(t, s) for t in range(hp) for s in range(qb)]

    def body(q_ref, k_ref, v_ref, o_ref, lse_ref):
        base = pl.program_id(1) * qb
        qs = {(t, s): q_ref[s * blk:(s + 1) * blk, t * CAT_W:(t + 1) * CAT_W] for t, s in chains}
        row = lax.broadcasted_iota(jnp.int32, (blk, blk), 0)
        col = lax.broadcasted_iota(jnp.int32, (blk, blk), 1)
        causal = col <= row

        def step(j, carry, modes):
            off = pl.multiple_of(j * blk, blk)
            act = [c for c in chains if modes[c[1]]]
            ss = {c: _dot_t(qs[c], k_ref[pl.ds(off, blk), c[0] * CAT_W:(c[0] + 1) * CAT_W]) for c in act}
            carry = dict(carry)
            for c in act:
                m, l, acc = carry[c]
                s = ss[c]
                if modes[c[1]] == "m":
                    s = jnp.where(causal, s, -jnp.inf)
                m_new = jnp.maximum(m, jnp.max(s, axis=-1, keepdims=True))
                p = jnp.exp2(s - m_new)
                alpha = jnp.exp2(m - m_new)
                l = alpha * l + jnp.sum(p, axis=-1, keepdims=True)
                acc = alpha * acc + _dot(p.astype(BF16), v_ref[pl.ds(off, blk), _hs(c[0])])
                carry[c] = (m_new, l, acc)
            return carry

        init = (jnp.full((blk, 1), -jnp.inf, F32), jnp.zeros((blk, 1), F32),
                jnp.zeros((blk, HEAD_DIM), F32))
        carry = {c: init for c in chains}
        carry = _loop_blocks(base, qb, lambda j, c: step(j, c, ("f",) * qb), carry, reverse=False)
        for s in range(qb):
            carry = step(base + s, carry, _chain_modes(s, qb))
        for t, s in chains:
            m, l, acc = carry[(t, s)]
            o_ref[s * blk:(s + 1) * blk, _hs(t)] = acc / l
            lse_ref[s * blk:(s + 1) * blk, _hs(t)] = jnp.broadcast_to(
                (m + jnp.log2(l)) * (1.0 / LOG2_E), (blk, HEAD_DIM))

    out = jax.ShapeDtypeStruct((seq, MLA_W), F32)
    return pl.pallas_call(
        body, out_shape=[out, out], grid=(N_MLA_HEADS // hp, nkb // qb),
        in_specs=[pl.BlockSpec((rows, hp * CAT_W), lambda h, i: (i, h)),
                  pl.BlockSpec((seq, hp * CAT_W), lambda h, i: (0, h)),
                  pl.BlockSpec((seq, hp * HEAD_DIM), lambda h, i: (0, h))],
        out_specs=[pl.BlockSpec((rows, hp * HEAD_DIM), lambda h, i: (i, h)),
                   pl.BlockSpec((rows, hp * HEAD_DIM), lambda h, i: (i, h))],
        name=name, compiler_params=_params("parallel", "arbitrary"),
    )(q_cat, k_cat, v)


def _mla_bwd(q_cat, k_cat, v, out, lse, dout, *, name):
    seq = q_cat.shape[0]
    blk = min(ATT_BLOCK, seq)
    nkb = seq // blk
    qb = _pick(nkb, (4, 2, 1))
    rows = qb * blk

    def body(q_ref, k_ref, v_ref, o_ref, lse_ref, do_ref, dq_ref, dk_ref, dv_ref):
        g = pl.program_id(1)
        base = g * qb

        @pl.when(g == 0)
        def _():
            dk_ref[...] = jnp.zeros_like(dk_ref)
            dv_ref[...] = jnp.zeros_like(dv_ref)

        qs, dobs, deltas, lses = [], [], [], []
        for t in range(qb):
            rs = slice(t * blk, (t + 1) * blk)
            do = do_ref[rs, :]
            qs.append(q_ref[rs, :])
            dobs.append(do.astype(BF16))
            deltas.append(jnp.sum(do * o_ref[rs, :], axis=-1, keepdims=True))
            lses.append(lse_ref[rs, :1] * LOG2_E)
        row = lax.broadcasted_iota(jnp.int32, (blk, blk), 0)
        col = lax.broadcasted_iota(jnp.int32, (blk, blk), 1)
        causal = col <= row

        def step(j, dqs, modes):
            off = pl.multiple_of(j * blk, blk)
            kb = k_ref[pl.ds(off, blk), :]
            vb = v_ref[pl.ds(off, blk), :]
            act = [t for t in range(qb) if modes[t]]
            ss = {t: _dot_t(qs[t], kb) for t in act}
            dps = {t: _dot_t(dobs[t], vb) for t in act}
            dqs = list(dqs)
            dv_inc = dk_inc = None
            for t in act:
                p = jnp.exp2(ss[t] - lses[t])
                if modes[t] == "m":
                    p = jnp.where(causal, p, 0.0)
                ds = (p * (dps[t] - deltas[t])).astype(BF16)
                inc_v = _tdot(p.astype(BF16), dobs[t])
                inc_k = _tdot(ds, qs[t])
                dv_inc = inc_v if dv_inc is None else dv_inc + inc_v
                dk_inc = inc_k if dk_inc is None else dk_inc + inc_k
                dqs[t] = dqs[t] + _dot(ds, kb)
            dv_ref[pl.ds(off, blk), :] += dv_inc
            dk_ref[pl.ds(off, blk), :] += dk_inc
            return tuple(dqs)

        dqs = (jnp.zeros((blk, CAT_W), F32),) * qb
        dqs = _loop_blocks(base, qb, lambda j, c: step(j, c, ("f",) * qb), dqs, reverse=False)
        for s in range(qb):
            modes = tuple(None if t < s else ("m" if t == s else "f") for t in range(qb))
            dqs = step(base + s, dqs, modes)
        for t in range(qb):
            dq_ref[t * blk:(t + 1) * blk, :] = dqs[t] * MLA_SCALE

        @pl.when(g == pl.num_programs(1) - 1)
        def _():
            dk_ref[...] = dk_ref[...] * (1.0 / LOG2_E)

    return pl.pallas_call(
        body,
        out_shape=[jax.ShapeDtypeStruct((seq, N_MLA_HEADS * CAT_W), F32),
                   jax.ShapeDtypeStruct((seq, N_MLA_HEADS * CAT_W), F32),
                   jax.ShapeDtypeStruct((seq, MLA_W), F32)],
        grid=(N_MLA_HEADS, nkb // qb),
        in_specs=[pl.BlockSpec((rows, CAT_W), lambda h, i: (i, h)),
                  pl.BlockSpec((seq, CAT_W), lambda h, i: (0, h)),
                  pl.BlockSpec((seq, HEAD_DIM), lambda h, i: (0, h)),
                  pl.BlockSpec((rows, HEAD_DIM), lambda h, i: (i, h)),
                  pl.BlockSpec((rows, HEAD_DIM), lambda h, i: (i, h)),
                  pl.BlockSpec((rows, HEAD_DIM), lambda h, i: (i, h))],
        out_specs=[pl.BlockSpec((rows, CAT_W), lambda h, i: (i, h)),
                   pl.BlockSpec((seq, CAT_W), lambda h, i: (0, h)),
                   pl.BlockSpec((seq, HEAD_DIM), lambda h, i: (0, h))],
        name=name, compiler_params=_params("parallel", "arbitrary"),
    )(q_cat, k_cat, v, out, lse, dout)


def _local_step(x, mem, positions, target, w, g):
    seq = x.shape[0]
    inv_freq = jnp.power(ROPE_THETA, -jnp.arange(0, ROPE_DIM, 2, dtype=F32) / ROPE_DIM)
    ang = positions.astype(F32)[:, None] * inv_freq
    cos, sin = jnp.cos(ang), jnp.sin(ang)
    lane_pad = jnp.zeros((seq, HEAD_DIM - ROPE_DIM), F32)
    cos_t = jnp.concatenate([cos, cos, lane_pad], axis=1)
    sin_t = jnp.concatenate([-sin, sin, lane_pad], axis=1)
    gain_pad = jnp.zeros((1, HEAD_DIM - ROPE_DIM), F32)
    g_k_rope = jnp.concatenate([g["g_k_rope"], gain_pad], axis=1)
    g_q_rope = jnp.concatenate([g["b_g_q_rope"], gain_pad], axis=1)

    def norm_to_bf16(src, gain, name):
        def body(ins, outs, _):
            outs[0][...] = _rms(ins[0][...], ins[1][...]).astype(BF16)
        return _rowwise(body, seq, [(src, True), (gain, False)], [(src.shape[1], BF16)], name=name)[0]

    h_a = norm_to_bf16(x, g["a_norm"], "a_norm_fwd")
    qkv = _mm(h_a, w["a_in_qkv"], out_dtype=BF16, scale_cols=(SB_W, SB_Q_SCALE), name="a_in_qkv")
    gr = _mm(h_a, w["a_in_gate"], name="a_in_gate")
    sb = _sb_fwd(qkv, name="sb_fwd")
    mem0 = _mem_side_fwd(mem, g["mem_norm"][0:1], w["mem_kv"][0], g["g_mem_k"][0:1], tag="a")
    mixed_a = _mix_fwd(sb, gr, 0, mem0[2], mem0[3], g["g_mem_q"][0:1], name="a_mix_fwd")
    x1 = _mm(mixed_a, w["a_out"], res=x, name="a_out")

    def norms2_body(ins, outs, _):
        xv = ins[0][...]
        outs[0][...] = _rms(xv, ins[1][...]).astype(BF16)
        outs[1][...] = _rms(xv, ins[2][...]).astype(BF16)

    h_kv, h_b = _rowwise(norms2_body, seq, [(x1, True), (g["kv_norm"], False), (g["b_norm"], False)],
                         [(D_MODEL, BF16), (D_MODEL, BF16)], name="kv_b_norm_fwd")
    ckr = _mm(h_kv, w["dkv"], name="dkv")

    def ckr_body(ins, outs, _):
        ckr_ref, gc_ref, gr_ref, c_ref, s_ref = ins
        outs[0][...] = _rms(ckr_ref[:, :KV_LORA], gc_ref[...]).astype(BF16)
        kr = _rms(ckr_ref[:, KV_LORA:], gr_ref[...], n=ROPE_DIM)
        outs[1][...] = _rope(kr, c_ref[...], s_ref[...]).astype(BF16)

    c_n, k_r = _rowwise(ckr_body, seq,
                        [(ckr, True), (g["g_ckv"], False), (g_k_rope, False), (cos_t, True), (sin_t, True)],
                        [(KV_LORA, BF16), (HEAD_DIM, BF16)], name="ckv_prep_fwd")
    kv = _mm(c_n, w["ukv"], name="ukv")

    def kcat_body(ins, outs, _):
        kv_ref, kr_ref, gk_ref = ins
        kc_ref, v_ref = outs
        for h in range(N_MLA_HEADS):
            kc_ref[:, h * CAT_W:h * CAT_W + HEAD_DIM] = _rms(
                kv_ref[:, h * CAT_W:h * CAT_W + HEAD_DIM], gk_ref[...]).astype(BF16)
            kc_ref[:, h * CAT_W + HEAD_DIM:(h + 1) * CAT_W] = kr_ref[...]
            v_ref[:, _hs(h)] = kv_ref[:, h * CAT_W + HEAD_DIM:(h + 1) * CAT_W].astype(BF16)

    k_cat, v_mla = _rowwise(kcat_body, seq, [(kv, True), (k_r, True), (g["g_k_nope"], False)],
                            [(N_MLA_HEADS * CAT_W, BF16), (MLA_W, BF16)], name="k_prep_fwd")

    p2 = _mm(h_b, w["b_in"], name="b_in")

    def qlat_body(ins, outs, _):
        outs[0][...] = _rms(ins[0][:, :Q_LORA], ins[1][...]).astype(BF16)

    (q_l,) = _rowwise(qlat_body, seq, [(p2, True), (g["b_g_q_lat"], False)], [(Q_LORA, BF16)],
                      name="q_lat_norm_fwd")
    q_up = _mm(q_l, w["uq"], name="uq")

    def qcat_body(ins, outs, _):
        q_ref, gn_ref, gr_ref, c_ref, s_ref = ins
        (o_ref,) = outs
        for h in range(N_MLA_HEADS):
            o_ref[:, h * CAT_W:h * CAT_W + HEAD_DIM] = (MLA_Q_SCALE * _rms(
                q_ref[:, h * CAT_W:h * CAT_W + HEAD_DIM], gn_ref[...])).astype(BF16)
            qr = _rms(q_ref[:, h * CAT_W + HEAD_DIM:(h + 1) * CAT_W], gr_ref[...], n=ROPE_DIM)
            o_ref[:, h * CAT_W + HEAD_DIM:(h + 1) * CAT_W] = (
                MLA_Q_SCALE * _rope(qr, c_ref[...], s_ref[...])).astype(BF16)

    (q_cat,) = _rowwise(qcat_body, seq,
                        [(q_up, True), (g["b_g_q_nope"], False), (g_q_rope, False), (cos_t, True), (sin_t, True)],
                        [(N_MLA_HEADS * CAT_W, BF16)], name="q_prep_fwd")
    att, lse = _mla_fwd(q_cat, k_cat, v_mla, name="mla_fwd")
    mem1 = _mem_side_fwd(mem, g["mem_norm"][1:2], w["mem_kv"][1], g["g_mem_k"][1:2], tag="b")
    mixed_b = _mix_fwd(att, p2, Q_LORA, mem1[2], mem1[3], g["g_mem_q"][1:2], name="b_mix_fwd")
    y = _mm(mixed_b, w["b_out"], res=x1, name="b_out")

    def loss_body(ins, outs, accs):
        diff = ins[0][...] - ins[1][...]
        outs[0][...] = diff / D_MODEL
        col = jnp.sum(diff * diff, axis=0, keepdims=True)
        part = col[:, :HEAD_DIM]
        for c in range(1, D_MODEL // HEAD_DIM):
            part = part + col[:, _hs(c)]
        accs[0][...] += part * (0.5 / D_MODEL)

    dy, loss_part = _rowwise(loss_body, seq, [(y, True), (target, True)], [(D_MODEL, F32)],
                             [((1, HEAD_DIM), F32)], name="loss")

    gw, gg = {}, {}
    dmixed_b = _mm(dy, w["b_out"], tb=True, name="b_out_dx")
    gw["b_out"] = _mm(mixed_b, dy, ta=True, out_dtype=BF16, name="b_out_dw")
    datt, dgate_b, dmk1, dmv1, gq1 = _mix_bwd(dmixed_b, att, p2, Q_LORA, mem1[2], mem1[3],
                                              g["g_mem_q"][1:2], name="b_mix_bwd")
    dq_cat, dk_cat, dv_mla = _mla_bwd(q_cat, k_cat, v_mla, att, lse, datt, name="mla_bwd")

    def qcat_bwd_body(ins, outs, accs):
        q_ref, dq_ref, gn_ref, gr_ref, c_ref, s_ref = ins
        (o_ref,) = outs
        dgn_ref, dgr_ref = accs
        for h in range(N_MLA_HEADS):
            dx, dg = _rms_bwd(q_ref[:, h * CAT_W:h * CAT_W + HEAD_DIM], gn_ref[...],
                              dq_ref[:, h * CAT_W:h * CAT_W + HEAD_DIM])
            o_ref[:, h * CAT_W:h * CAT_W + HEAD_DIM] = dx.astype(BF16)
            dgn_ref[...] += dg
            dn = _rope_bwd(dq_ref[:, h * CAT_W + HEAD_DIM:(h + 1) * CAT_W], c_ref[...], s_ref[...])
            dx, dg = _rms_bwd(q_ref[:, h * CAT_W + HEAD_DIM:(h + 1) * CAT_W], gr_ref[...], dn, n=ROPE_DIM)
            o_ref[:, h * CAT_W + HEAD_DIM:(h + 1) * CAT_W] = dx.astype(BF16)
            dgr_ref[...] += dg

    dq_up, gg["b_g_q_nope"], dgqr = _rowwise(
        qcat_bwd_body, seq,
        [(q_up, True), (dq_cat, True), (g["b_g_q_nope"], False), (g_q_rope, False), (cos_t, True), (sin_t, True)],
        [(N_MLA_HEADS * CAT_W, BF16)], [((1, HEAD_DIM), F32), ((1, HEAD_DIM), F32)], name="q_prep_bwd")
    gg["b_g_q_rope"] = dgqr
    dq_l = _mm(dq_up, w["uq"], tb=True, name="uq_dx")
    gw["uq"] = _mm(q_l, dq_up, ta=True, out_dtype=BF16, n_split=N_CHIPS, name="uq_dw")

    def qlat_bwd_body(ins, outs, accs):
        p2_ref, dql_ref, dgate_ref, gl_ref = ins
        dx, dg = _rms_bwd(p2_ref[:, :Q_LORA], gl_ref[...], dql_ref[...])
        outs[0][:, :Q_LORA] = dx.astype(BF16)
        outs[0][:, Q_LORA:] = dgate_ref[...]
        accs[0][...] += dg

    dp2, gg["b_g_q_lat"] = _rowwise(
        qlat_bwd_body, seq, [(p2, True), (dq_l, True), (dgate_b, True), (g["b_g_q_lat"], False)],
        [(Q_LORA + GATE_W, BF16)], [((1, Q_LORA), F32)], name="q_lat_norm_bwd")
    dh_b = _mm(dp2, w["b_in"], tb=True, name="b_in_dx")
    gw["b_in"] = _mm(h_b, dp2, ta=True, out_dtype=BF16, n_split=N_CHIPS, name="b_in_dw")

    def kcat_bwd_body(ins, outs, accs):
        kv_ref, dkc_ref, dv_ref, gk_ref = ins
        dkv_ref, dkr_ref = outs
        (dgk_ref,) = accs
        dkr = jnp.zeros(dkr_ref.shape, F32)
        for h in range(N_MLA_HEADS):
            dx, dg = _rms_bwd(kv_ref[:, h * CAT_W:h * CAT_W + HEAD_DIM], gk_ref[...],
                              dkc_ref[:, h * CAT_W:h * CAT_W + HEAD_DIM])
            dkv_ref[:, h * CAT_W:h * CAT_W + HEAD_DIM] = dx.astype(BF16)
            dgk_ref[...] += dg
            dkv_ref[:, h * CAT_W + HEAD_DIM:(h + 1) * CAT_W] = dv_ref[:, _hs(h)].astype(BF16)
            dkr = dkr + dkc_ref[:, h * CAT_W + HEAD_DIM:(h + 1) * CAT_W]
        dkr_ref[...] = dkr

    dkv, dk_r, gg["g_k_nope"] = _rowwise(
        kcat_bwd_body, seq, [(kv, True), (dk_cat, True), (dv_mla, True), (g["g_k_nope"], False)],
        [(N_MLA_HEADS * CAT_W, BF16), (HEAD_DIM, F32)], [((1, HEAD_DIM), F32)], name="k_prep_bwd")
    dc_n = _mm(dkv, w["ukv"], tb=True, name="ukv_dx")
    gw["ukv"] = _mm(c_n, dkv, ta=True, out_dtype=BF16, n_split=N_CHIPS, name="ukv_dw")

    def ckr_bwd_body(ins, outs, accs):
        ckr_ref, dcn_ref, dkr_ref, gc_ref, gr_ref, c_ref, s_ref = ins
        dx, dg = _rms_bwd(ckr_ref[:, :KV_LORA], gc_ref[...], dcn_ref[...])
        outs[0][:, :KV_LORA] = dx.astype(BF16)
        accs[0][...] += dg
        dn = _rope_bwd(dkr_ref[...], c_ref[...], s_ref[...])
        dx, dg = _rms_bwd(ckr_ref[:, KV_LORA:], gr_ref[...], dn, n=ROPE_DIM)
        outs[0][:, KV_LORA:] = dx.astype(BF16)
        accs[1][...] += dg

    dckr, gg["g_ckv"], gg["g_k_rope"] = _rowwise(
        ckr_bwd_body, seq,
        [(ckr, True), (dc_n, True), (dk_r, True), (g["g_ckv"], False), (g_k_rope, False),
         (cos_t, True), (sin_t, True)],
        [(KV_LORA + HEAD_DIM, BF16)], [((1, KV_LORA), F32), ((1, HEAD_DIM), F32)], name="ckv_prep_bwd")
    dh_kv = _mm(dckr, w["dkv"], tb=True, name="dkv_dx")
    gw["dkv"] = _mm(h_kv, dckr, ta=True, out_dtype=BF16, name="dkv_dw")

    def norms2_bwd_body(ins, outs, accs):
        x_ref, dy_ref, dhk_ref, dhb_ref, gk_ref, gb_ref = ins
        xv = x_ref[...]
        dxk, dgk = _rms_bwd(xv, gk_ref[...], dhk_ref[...])
        dxb, dgb = _rms_bwd(xv, gb_ref[...], dhb_ref[...])
        outs[0][...] = dy_ref[...] + dxk + dxb
        accs[0][...] += dgk
        accs[1][...] += dgb

    dx1, gg["kv_norm"], gg["b_norm"] = _rowwise(
        norms2_bwd_body, seq,
        [(x1, True), (dy, True), (dh_kv, True), (dh_b, True), (g["kv_norm"], False), (g["b_norm"], False)],
        [(D_MODEL, F32)], [((1, D_MODEL), F32), ((1, D_MODEL), F32)], name="kv_b_norm_bwd")

    dmixed_a = _mm(dx1, w["a_out"], tb=True, name="a_out_dx")
    gw["a_out"] = _mm(mixed_a, dx1, ta=True, out_dtype=BF16, name="a_out_dw")
    dsb, dgate_a, dmk0, dmv0, gq0 = _mix_bwd(dmixed_a, sb, gr, 0, mem0[2], mem0[3],
                                             g["g_mem_q"][0:1], name="a_mix_bwd")
    dq, dk, dv = _sb_bwd(qkv, sb, dsb, name="sb_bwd")
    dp_a = jnp.concatenate([dq.astype(BF16), dk.astype(BF16), dv.astype(BF16), dgate_a], axis=1)
    dh_a = _mm(dp_a, w["a_in"], tb=True, name="a_in_dx")
    gw["a_in"] = _mm(h_a, dp_a, ta=True, out_dtype=BF16, n_split=N_CHIPS, name="a_in_dw")

    def norm_a_bwd_body(ins, outs, accs):
        dx, dg = _rms_bwd(ins[0][...], ins[3][...], ins[2][...])
        outs[0][...] = ins[1][...] + dx
        accs[0][...] += dg

    grad_x, gg["a_norm"] = _rowwise(
        norm_a_bwd_body, seq, [(x, True), (dx1, True), (dh_a, True), (g["a_norm"], False)],
        [(D_MODEL, F32)], [((1, D_MODEL), F32)], name="a_norm_bwd")

    dw0, dgn0, dgk0 = _mem_side_bwd(mem, g["mem_norm"][0:1], w["mem_kv"][0], g["g_mem_k"][0:1],
                                    mem0[0], mem0[1], dmk0, dmv0, tag="a")
    dw1, dgn1, dgk1 = _mem_side_bwd(mem, g["mem_norm"][1:2], w["mem_kv"][1], g["g_mem_k"][1:2],
                                    mem1[0], mem1[1], dmk1, dmv1, tag="b")
    gw["mem_kv"] = (dw0, dw1)
    gg["mem_norm"] = jnp.concatenate([dgn0, dgn1], axis=0)
    gg["g_mem_q"] = jnp.concatenate([gq0, gq1], axis=0)
    gg["g_mem_k"] = jnp.concatenate([dgk0, dgk1], axis=0)
    return loss_part, grad_x, gw, gg


HBM_SPEC = pl.BlockSpec(memory_space=pl.ANY)


def _other_chips():
    x, y = lax.axis_index("x"), lax.axis_index("y")
    return [(1 - x, y), (x, 1 - y), (1 - x, 1 - y)]


def _allgather_chips(shards):
    n = len(shards)
    split = [s.shape[0] % 32 == 0 for s in shards]

    def body(*refs):
        ins, outs = refs[:n], refs[n:2 * n]
        send, recv, fsend, frecv = refs[2 * n:]
        x, y, c = lax.axis_index("x"), lax.axis_index("y"), lax.axis_index("c")
        me = 2 * x + y
        chips = _other_chips()

        def part(ref, wi):
            if not split[wi]:
                return ref
            half = shards[wi].shape[0] // 2
            return ref.at[pl.ds(pl.multiple_of(c * half, 16), half)]

        def ici(wi, k, src_chip, to):
            return pltpu.make_async_remote_copy(
                src_ref=part(ins[wi], wi), dst_ref=part(outs[wi].at[src_chip], wi),
                send_sem=send.at[wi, k], recv_sem=recv.at[wi, k], device_id=to, device_id_type=MESH)

        def d2d(wi, k, src_chip):
            rows = part(outs[wi].at[src_chip], wi)
            return pltpu.make_async_remote_copy(
                src_ref=rows, dst_ref=rows, send_sem=fsend.at[wi, k], recv_sem=frecv.at[wi, k],
                device_id=(x, y, 1 - c), device_id_type=MESH)

        for wi in range(n):
            for k, (tx, ty) in enumerate(chips):
                ici(wi, k, me, (tx, ty, c)).start()
        for wi in range(n):
            for k, (tx, ty) in enumerate(chips):
                landed = ici(wi, k, 2 * tx + ty, (tx, ty, c))
                landed.wait_recv()
                if split[wi]:
                    d2d(wi, k, 2 * tx + ty).start()
        for wi in range(n):
            for k, (tx, ty) in enumerate(chips):
                ici(wi, k, me, (tx, ty, c)).wait_send()
                if split[wi]:
                    fwd = d2d(wi, k, 2 * tx + ty)
                    fwd.wait_send()
                    fwd.wait_recv()

    return pl.pallas_call(
        body, out_shape=[jax.ShapeDtypeStruct((N_CHIPS,) + s.shape, s.dtype) for s in shards],
        in_specs=[HBM_SPEC] * n, out_specs=[HBM_SPEC] * n,
        scratch_shapes=[pltpu.SemaphoreType.DMA((n, 3)), pltpu.SemaphoreType.DMA((n, 3)),
                        pltpu.SemaphoreType.DMA((n, 3)), pltpu.SemaphoreType.DMA((n, 3))],
        name="allgather_weights",
    )(*shards)


def _scatter_to_chips(grads):
    n = len(grads)

    def body(*refs):
        ins, outs = refs[:n], refs[n:2 * n]
        send, recv = refs[2 * n:]
        c = lax.axis_index("c")
        copies = []
        for wi in range(n):
            for k, (tx, ty) in enumerate(_other_chips()):
                cp = pltpu.make_async_remote_copy(
                    src_ref=ins[wi].at[2 * tx + ty], dst_ref=outs[wi].at[k], send_sem=send.at[wi, k],
                    recv_sem=recv.at[wi, k], device_id=(tx, ty, c), device_id_type=MESH)
                cp.start()
                copies.append(cp)
        for cp in copies:
            cp.wait()

    return pl.pallas_call(
        body, out_shape=[jax.ShapeDtypeStruct((3,) + s.shape[1:], s.dtype) for s in grads],
        in_specs=[HBM_SPEC] * n, out_specs=[HBM_SPEC] * n,
        scratch_shapes=[pltpu.SemaphoreType.DMA((n, 3)), pltpu.SemaphoreType.DMA((n, 3))],
        name="scatter_grads",
    )(*grads)


def _halve_with_sibling(grads):
    n = len(grads)
    n_slots = grads[0].shape[0]

    def body(*refs):
        ins, got = refs[:n], refs[n:2 * n]
        send, recv = refs[2 * n:]
        c = lax.axis_index("c")
        sib = (lax.axis_index("x"), lax.axis_index("y"), 1 - c)
        copies = []
        for wi in range(n):
            half = grads[wi].shape[1] // 2
            for s in range(n_slots):
                theirs = ins[wi].at[s, pl.ds(pl.multiple_of((1 - c) * half, 16), half)]
                give = pltpu.make_async_remote_copy(
                    src_ref=theirs, dst_ref=got[wi].at[s], send_sem=send.at[wi, s], recv_sem=recv.at[wi, s],
                    device_id=sib, device_id_type=MESH)
                give.start()
                copies.append(give)
        for cp in copies:
            cp.wait()

    halves = [jax.ShapeDtypeStruct((s.shape[0], s.shape[1] // 2) + s.shape[2:], s.dtype) for s in grads]
    return pl.pallas_call(
        body, out_shape=halves, in_specs=[HBM_SPEC] * n, out_specs=[HBM_SPEC] * n,
        scratch_shapes=[pltpu.SemaphoreType.DMA((n, n_slots)), pltpu.SemaphoreType.DMA((n, n_slots))],
        name="halve_grads_with_sibling",
    )(*grads)


def _swap_with_sibling(parts):
    n = len(parts)

    def body(*refs):
        ins, outs = refs[:n], refs[n:2 * n]
        send, recv = refs[2 * n:]
        sib = (lax.axis_index("x"), lax.axis_index("y"), 1 - lax.axis_index("c"))
        copies = []
        for wi in range(n):
            cp = pltpu.make_async_remote_copy(
                src_ref=ins[wi], dst_ref=outs[wi], send_sem=send.at[wi], recv_sem=recv.at[wi],
                device_id=sib, device_id_type=MESH)
            cp.start()
            copies.append(cp)
        for cp in copies:
            cp.wait()

    return pl.pallas_call(
        body, out_shape=[jax.ShapeDtypeStruct(s.shape, s.dtype) for s in parts],
        in_specs=[HBM_SPEC] * n, out_specs=[HBM_SPEC] * n,
        scratch_shapes=[pltpu.SemaphoreType.DMA((n,)), pltpu.SemaphoreType.DMA((n,))],
        name="swap_grad_halves",
    )(*parts)


def _allreduce_small(vec, loss_row):
    rows = vec.shape[0]

    def body(v_ref, o_ref, buf, send, recv):
        x, y, c = lax.axis_index("x"), lax.axis_index("y"), lax.axis_index("c")
        me = 4 * x + 2 * y + c
        buf[me] = v_ref[...]
        copies = []
        for r in range(1, N_DEV):
            peer = (x ^ ((r >> 2) & 1), y ^ ((r >> 1) & 1), c ^ (r & 1))
            cp = pltpu.make_async_remote_copy(
                src_ref=v_ref, dst_ref=buf.at[me], send_sem=send.at[r - 1], recv_sem=recv.at[r - 1],
                device_id=peer, device_id_type=MESH)
            cp.start()
            copies.append(cp)
        for cp in copies:
            cp.wait()
        total = buf[0]
        for d in range(1, N_DEV):
            total = total + buf[d]
        o_ref[...] = total
        o_ref[loss_row:loss_row + 1, :] = jnp.broadcast_to(
            jnp.sum(total[loss_row:loss_row + 1, :], axis=-1, keepdims=True), (1, HEAD_DIM))

    return pl.pallas_call(
        body, out_shape=jax.ShapeDtypeStruct(vec.shape, F32),
        in_specs=[pl.BlockSpec(memory_space=pltpu.VMEM)], out_specs=pl.BlockSpec(memory_space=pltpu.VMEM),
        scratch_shapes=[pltpu.VMEM((N_DEV, rows, HEAD_DIM), F32),
                        pltpu.SemaphoreType.DMA((N_DEV - 1,)), pltpu.SemaphoreType.DMA((N_DEV - 1,))],
        name="allreduce_gains",
    )(vec)


def _pair_sum(grads, got, *, name):
    slots, rows, width = got.shape
    blk = _pick(rows, (256, 128, 64, 32, 16))
    nbh = rows // blk

    def body(lo_ref, hi_ref, got_ref, o_ref):
        mine = jnp.where(lax.axis_index("c") == 0, lo_ref[...], hi_ref[...])
        o_ref[...] = (mine.astype(F32) + got_ref[...].astype(F32)).astype(BF16)

    spec = pl.BlockSpec((None, blk, width), lambda s, i: (s, i, 0))
    return pl.pallas_call(
        body, out_shape=jax.ShapeDtypeStruct(got.shape, BF16), grid=(slots, nbh),
        in_specs=[spec, pl.BlockSpec((None, blk, width), lambda s, i: (s, nbh + i, 0)), spec],
        out_specs=spec, name=name, compiler_params=_params("parallel", "parallel"),
    )(grads, grads, got)


def _sum_slots(recv, chip_sum, *, name):
    _, rows, width = recv.shape
    blk = _pick(rows, (256, 128, 64, 32, 16, 8))

    def body(r_ref, p_ref, o_ref):
        me = 2 * lax.axis_index("x") + lax.axis_index("y")
        own = jnp.where(me < 2, jnp.where(me == 0, p_ref[0], p_ref[1]), jnp.where(me == 2, p_ref[2], p_ref[3]))
        o_ref[...] = ((own.astype(F32) + r_ref[0].astype(F32)) + r_ref[1].astype(F32)) + r_ref[2].astype(F32)

    return pl.pallas_call(
        body, out_shape=jax.ShapeDtypeStruct((rows, width), F32), grid=(rows // blk,),
        in_specs=[pl.BlockSpec((3, blk, width), lambda i: (0, i, 0)),
                  pl.BlockSpec((N_CHIPS, blk, width), lambda i: (0, i, 0))],
        out_specs=pl.BlockSpec((blk, width), lambda i: (i, 0)),
        name=name, compiler_params=_params("parallel"),
    )(recv, chip_sum)


def _adamw(wgt, grad, m, v, *, name, halves=None):
    rows, width = wgt.shape
    blk = _pick(rows // 2 if halves else rows, (256, 128, 64, 32, 16, 8))
    nbh = rows // 2 // blk

    def body(*refs):
        if halves:
            w_ref, mine_ref, theirs_ref, m_ref, v_ref, g_out, d_out, m_out, v_out = refs
            grad_v = jnp.where(pl.program_id(0) // nbh == lax.axis_index("c"), mine_ref[...], theirs_ref[...])
        else:
            w_ref, g_ref, m_ref, v_ref, g_out, d_out, m_out, v_out = refs
            grad_v = g_ref[...]
        m_new = ADAM_B1 * m_ref[...] + (1.0 - ADAM_B1) * grad_v
        v_new = ADAM_B2 * v_ref[...] + (1.0 - ADAM_B2) * (grad_v * grad_v)
        m_hat = m_new / (1.0 - ADAM_B1 ** ADAM_STEP)
        v_hat = v_new / (1.0 - ADAM_B2 ** ADAM_STEP)
        g_out[...] = grad_v
        d_out[...] = -ADAM_LR * (m_hat / (jnp.sqrt(v_hat) + ADAM_EPS) + ADAM_WD * w_ref[...])
        m_out[...] = m_new
        v_out[...] = v_new

    spec = pl.BlockSpec((blk, width), lambda i: (i, 0))
    half_spec = pl.BlockSpec((blk, width), lambda i: (i % nbh, 0))
    g_specs, g_args = ([half_spec, half_spec], list(halves)) if halves else ([spec], [grad])
    out = jax.ShapeDtypeStruct((rows, width), F32)
    return pl.pallas_call(
        body, out_shape=[out] * 4, grid=(rows // blk,), in_specs=[spec] + g_specs + [spec, spec],
        out_specs=[spec] * 4, name=name, compiler_params=_params("parallel"),
    )(wgt, *g_args, m, v)


_SMALL = (("a_norm", 2048), ("kv_norm", 2048), ("g_ckv", 512), ("g_k_nope", 128), ("g_k_rope", 64),
          ("b_norm", 2048), ("b_g_q_lat", 512), ("b_g_q_nope", 128), ("b_g_q_rope", 64),
          ("mem_norm", 4096), ("g_mem_q", 256), ("g_mem_k", 256))


def _lanes(n):
    return -(-n // HEAD_DIM) * HEAD_DIM


def _pack_rows(pieces, pad_rows_to=8):
    flat = jnp.concatenate(pieces, axis=1)
    rows = flat.shape[1] // HEAD_DIM
    pad = (-rows) % pad_rows_to
    if pad:
        flat = jnp.concatenate([flat, jnp.zeros((1, pad * HEAD_DIM), F32)], axis=1)
    return flat.reshape(rows + pad, HEAD_DIM)


def _pad_lanes(a):
    a = a.reshape(1, -1)
    pad = _lanes(a.shape[1]) - a.shape[1]
    if pad:
        a = jnp.concatenate([a, jnp.zeros((1, pad), F32)], axis=1)
    return a


def kernel(x, mem, positions, a_norm, a_w_in, a_w_out, kv_norm, w_dkv, g_ckv, w_ukv, g_k_nope, g_k_rope, b_norm, b_w_in, b_g_q_lat, b_w_uq, b_g_q_nope, b_g_q_rope, b_w_out, mem_norm, w_mem_kv, g_mem_q, g_mem_k, loss_target, m_a_norm, m_a_w_in, m_a_w_out, m_kv_norm, m_w_dkv, m_g_ckv, m_w_ukv, m_g_k_nope, m_g_k_rope, m_b_norm, m_b_w_in, m_b_g_q_lat, m_b_w_uq, m_b_g_q_nope, m_b_g_q_rope, m_b_w_out, m_mem_norm, m_w_mem_kv, m_g_mem_q, m_g_mem_k, v_a_norm, v_a_w_in, v_a_w_out, v_kv_norm, v_w_dkv, v_g_ckv, v_w_ukv, v_g_k_nope, v_g_k_rope, v_b_norm, v_b_w_in, v_b_g_q_lat, v_b_w_uq, v_b_g_q_nope, v_b_g_q_rope, v_b_w_out, v_mem_norm, v_w_mem_kv, v_g_mem_q, v_g_mem_k):
    chip = 2 * lax.axis_index("x") + lax.axis_index("y")
    rows_dkv = D_MODEL // N_CHIPS
    heads_per_chip = N_MLA_HEADS // N_CHIPS
    qk_w = HEAD_DIM + ROPE_DIM

    big = {"a_in": a_w_in[0], "a_out": a_w_out[0], "dkv": w_dkv, "ukv": w_ukv, "b_in": b_w_in[0],
           "uq": b_w_uq[0], "b_out": b_w_out[0], "mem_kv": w_mem_kv.reshape(2 * rows_dkv, 2 * MEM_W)}
    big_m = {"a_in": m_a_w_in[0], "a_out": m_a_w_out[0], "dkv": m_w_dkv, "ukv": m_w_ukv, "b_in": m_b_w_in[0],
             "uq": m_b_w_uq[0], "b_out": m_b_w_out[0], "mem_kv": m_w_mem_kv.reshape(2 * rows_dkv, 2 * MEM_W)}
    big_v = {"a_in": v_a_w_in[0], "a_out": v_a_w_out[0], "dkv": v_w_dkv, "ukv": v_w_ukv, "b_in": v_b_w_in[0],
             "uq": v_b_w_uq[0], "b_out": v_b_w_out[0], "mem_kv": v_w_mem_kv.reshape(2 * rows_dkv, 2 * MEM_W)}
    names = list(big)
    own_shards = [big[n].astype(BF16) for n in names] + [a_norm]
    gathered = _allgather_chips(own_shards)
    gathered = [lax.dynamic_update_slice(g, s[None], (chip,) + (0,) * s.ndim)
                for g, s in zip(gathered, own_shards)]
    st = dict(zip(names, gathered[:-1]))
    a_in_full = st["a_in"].transpose(1, 0, 2).reshape(D_MODEL, QKV_W + GATE_W)
    uq = st["uq"].reshape(N_CHIPS, Q_LORA, heads_per_chip, qk_w)
    uq = jnp.pad(uq, ((0, 0), (0, 0), (0, 0), (0, CAT_W - qk_w)))
    w = {
        "a_in": a_in_full,
        "a_in_qkv": a_in_full[:, :QKV_W],
        "a_in_gate": a_in_full[:, QKV_W:],
        "a_out": st["a_out"].reshape(D_MODEL, D_MODEL),
        "dkv": jnp.pad(st["dkv"].reshape(D_MODEL, KV_LORA + ROPE_DIM), ((0, 0), (0, HEAD_DIM - ROPE_DIM))),
        "ukv": st["ukv"].transpose(1, 0, 2).reshape(KV_LORA, N_MLA_HEADS * CAT_W),
        "b_in": st["b_in"].transpose(1, 0, 2).reshape(D_MODEL, Q_LORA + GATE_W),
        "uq": uq.transpose(1, 0, 2, 3).reshape(Q_LORA, N_MLA_HEADS * CAT_W),
        "b_out": st["b_out"].reshape(D_MODEL, D_MODEL),
        "mem_kv": st["mem_kv"].reshape(N_CHIPS, 2, rows_dkv, 2 * MEM_W).transpose(1, 0, 2, 3).reshape(
            2, D_MODEL, 2 * MEM_W),
    }
    gains = {
        "a_norm": gathered[-1].reshape(1, D_MODEL), "kv_norm": kv_norm.reshape(1, -1),
        "g_ckv": g_ckv.reshape(1, -1), "g_k_nope": g_k_nope.reshape(1, -1), "g_k_rope": g_k_rope.reshape(1, -1),
        "b_norm": b_norm, "b_g_q_lat": b_g_q_lat, "b_g_q_nope": b_g_q_nope, "b_g_q_rope": b_g_q_rope,
        "mem_norm": mem_norm, "g_mem_q": g_mem_q, "g_mem_k": g_mem_k,
    }

    loss_part, grad_x, gw, gg = _local_step(x[0], mem[0], positions[0], loss_target[0], w, gains)

    stacked = {
        "a_in": gw["a_in"],
        "a_out": gw["a_out"].reshape(N_CHIPS, rows_dkv, D_MODEL),
        "dkv": gw["dkv"][:, :KV_LORA + ROPE_DIM].reshape(N_CHIPS, rows_dkv, KV_LORA + ROPE_DIM),
        "ukv": gw["ukv"],
        "b_in": gw["b_in"],
        "uq": gw["uq"].reshape(N_CHIPS, Q_LORA, heads_per_chip, CAT_W)[..., :qk_w].reshape(
            N_CHIPS, Q_LORA, heads_per_chip * qk_w),
        "b_out": gw["b_out"].reshape(N_CHIPS, rows_dkv, D_MODEL),
        "mem_kv": jnp.stack([gw["mem_kv"][0].reshape(N_CHIPS, rows_dkv, 2 * MEM_W),
                             gw["mem_kv"][1].reshape(N_CHIPS, rows_dkv, 2 * MEM_W)], axis=1).reshape(
            N_CHIPS, 2 * rows_dkv, 2 * MEM_W),
    }
    got = _halve_with_sibling([stacked[n] for n in names])
    chip_sum = [_pair_sum(stacked[n], g, name=f"pair_sum_{n}") for n, g in zip(names, got)]
    received = _scatter_to_chips(chip_sum)
    half_total = [_sum_slots(r, p, name=f"sum_slots_{n}") for n, r, p in zip(names, received, chip_sum)]
    sibling_half = _swap_with_sibling(half_total)
    big_out = {}
    for n, mine, theirs in zip(names, half_total, sibling_half):
        big_out[n] = _adamw(big[n], None, big_m[n], big_v[n], halves=(mine, theirs), name=f"adamw_{n}")

    pieces = [_pad_lanes(gg[n]) if n not in ("g_k_rope", "b_g_q_rope") else gg[n] for n, _ in _SMALL]
    pieces.append(loss_part)
    loss_row = sum(_lanes(size) for _, size in _SMALL) // HEAD_DIM
    summed = _allreduce_small(_pack_rows(pieces), loss_row)
    flat = summed.reshape(1, -1)
    small_g, off = {}, 0
    for n, size in _SMALL:
        small_g[n] = flat[:, off:off + size]
        off += _lanes(size)
    loss = flat[0, off]
    small_g["a_norm"] = lax.dynamic_slice(small_g["a_norm"], (0, chip * rows_dkv), (1, rows_dkv))

    small_w = {"a_norm": a_norm, "kv_norm": kv_norm, "g_ckv": g_ckv, "g_k_nope": g_k_nope, "g_k_rope": g_k_rope,
               "b_norm": b_norm, "b_g_q_lat": b_g_q_lat, "b_g_q_nope": b_g_q_nope, "b_g_q_rope": b_g_q_rope,
               "mem_norm": mem_norm, "g_mem_q": g_mem_q, "g_mem_k": g_mem_k}
    small_m = {"a_norm": m_a_norm, "kv_norm": m_kv_norm, "g_ckv": m_g_ckv, "g_k_nope": m_g_k_nope,
               "g_k_rope": m_g_k_rope, "b_norm": m_b_norm, "b_g_q_lat": m_b_g_q_lat, "b_g_q_nope": m_b_g_q_nope,
               "b_g_q_rope": m_b_g_q_rope, "mem_norm": m_mem_norm, "g_mem_q": m_g_mem_q, "g_mem_k": m_g_mem_k}
    small_v = {"a_norm": v_a_norm, "kv_norm": v_kv_norm, "g_ckv": v_g_ckv, "g_k_nope": v_g_k_nope,
               "g_k_rope": v_g_k_rope, "b_norm": v_b_norm, "b_g_q_lat": v_b_g_q_lat, "b_g_q_nope": v_b_g_q_nope,
               "b_g_q_rope": v_b_g_q_rope, "mem_norm": v_mem_norm, "g_mem_q": v_g_mem_q, "g_mem_k": v_g_mem_k}
    snames = [n for n, _ in _SMALL]
    packs = [_pack_rows([_pad_lanes(src[n]) for n in snames])
             for src in (small_w, small_g, small_m, small_v)]
    small_res = _adamw(packs[0], packs[1], packs[2], packs[3], name="adamw_gains")
    small_out = {n: [] for n in snames}
    for res in small_res:
        flat_r = res.reshape(1, -1)
        off = 0
        for n in snames:
            size = small_w[n].size
            small_out[n].append(flat_r[:, off:off + size].reshape(small_w[n].shape))
            off += _lanes(size)

    big_names = {"a_w_in": ("a_in", a_w_in), "a_w_out": ("a_out", a_w_out), "w_dkv": ("dkv", w_dkv),
                 "w_ukv": ("ukv", w_ukv), "b_w_in": ("b_in", b_w_in), "b_w_uq": ("uq", b_w_uq),
                 "b_w_out": ("b_out", b_w_out), "w_mem_kv": ("mem_kv", w_mem_kv)}
    order = ["a_norm", "a_w_in", "a_w_out", "kv_norm", "w_dkv", "g_ckv", "w_ukv", "g_k_nope", "g_k_rope",
             "b_norm", "b_w_in", "b_g_q_lat", "b_w_uq", "b_g_q_nope", "b_g_q_rope", "b_w_out", "mem_norm",
             "w_mem_kv", "g_mem_q", "g_mem_k"]
    groups = [[], [], [], []]
    for n in order:
        if n in big_names:
            key, ref_arr = big_names[n]
            for t in range(4):
                groups[t].append(big_out[key][t].reshape(ref_arr.shape))
        else:
            for t in range(4):
                groups[t].append(small_out[n][t])
    return (loss, grad_x[None], *groups[0], *groups[1], *groups[2], *groups[3])
```

```python
import functools

import jax
import jax.numpy as jnp
from jax import lax
from jax.experimental import pallas as pl
from jax.experimental.pallas import tpu as pltpu

F32 = jnp.float32
BF16 = jnp.bfloat16
MESH = pl.DeviceIdType.MESH

D_MODEL = 2048
HEAD_DIM = 128
N_SB_HEADS = 12
N_MEM_HEADS = 4
N_MLA_HEADS = 12
MEM_LEN = 256
Q_LORA = 512
KV_LORA = 512
ROPE_DIM = 64
SB_W = N_SB_HEADS * HEAD_DIM
MEM_W = N_MEM_HEADS * HEAD_DIM
MLA_W = N_MLA_HEADS * HEAD_DIM
QKV_W = 3 * SB_W
GATE_W = SB_W + 2 * MEM_W
CAT_W = 2 * HEAD_DIM
ROPE_THETA = 10000.0
EPS = 1e-6
N_CHIPS = 4
N_DEV = 8

ADAM_LR = 0.001
ADAM_B1 = 0.9
ADAM_B2 = 0.999
ADAM_EPS = 1e-08
ADAM_WD = 0.01
ADAM_STEP = 10

VMEM_LIMIT_BYTES = 56 * 1024 * 1024
MM_OPERAND_VMEM_BYTES = 24 * 1024 * 1024
ROW_BLOCK = 256
ATT_BLOCK = 256


def _params(*sem):
    return pltpu.CompilerParams(dimension_semantics=sem, vmem_limit_bytes=VMEM_LIMIT_BYTES)


def _pick(n, cands):
    for c in cands:
        if n % c == 0:
            return c
    return n


def _mm(a, b, *, name, ta=False, tb=False, out_dtype=F32, res=None, n_split=1, scale_cols=None):
    if ta:
        k_dim, m_dim = a.shape
    else:
        m_dim, k_dim = a.shape
    if tb:
        n_dim, kb = b.shape
    else:
        kb, n_dim = b.shape
    assert kb == k_dim, (a.shape, b.shape)
    n_per = n_dim // n_split
    bm = m_dim if m_dim <= 1024 else _pick(m_dim, (1024, 512, 256))
    bn = n_per if n_per <= 1024 else _pick(n_per, (1024, 896, 768, 640, 512, 256, 128))
    per_k = (bm * a.dtype.itemsize + bn * b.dtype.itemsize) * 2
    bk = next((c for c in (k_dim, 2048, 1024, 512, 256, 128)
               if c <= k_dim and k_dim % c == 0 and c * per_k <= MM_OPERAND_VMEM_BYTES), 128)
    nk = k_dim // bk
    nb_per = n_per // bn
    grid = (m_dim // bm, n_dim // bn, nk)
    a_spec = (pl.BlockSpec((bk, bm), lambda i, j, k: (k, i)) if ta
              else pl.BlockSpec((bm, bk), lambda i, j, k: (i, k)))
    b_spec = (pl.BlockSpec((bn, bk), lambda i, j, k: (j, k)) if tb
              else pl.BlockSpec((bk, bn), lambda i, j, k: (k, j)))
    dims = (((0 if ta else 1,), (1 if tb else 0,)), ((), ()))
    in_specs = [a_spec, b_spec]
    args = [a, b]
    if res is not None:
        in_specs.append(pl.BlockSpec((bm, bn), lambda i, j, k: (i, j)))
        args.append(res)
    if n_split == 1:
        out_shape = jax.ShapeDtypeStruct((m_dim, n_dim), out_dtype)
        out_spec = pl.BlockSpec((bm, bn), lambda i, j, k: (i, j))
    else:
        out_shape = jax.ShapeDtypeStruct((n_split, m_dim, n_per), out_dtype)
        out_spec = pl.BlockSpec((None, bm, bn), lambda i, j, k: (j // nb_per, i, j % nb_per))

    def body(*refs):
        if res is None:
            a_ref, b_ref, o_ref, acc = refs
            r_ref = None
        else:
            a_ref, b_ref, r_ref, o_ref, acc = refs
        k = pl.program_id(2)
        col_block = pl.program_id(1)

        @pl.when(k == 0)
        def _():
            acc[...] = jnp.zeros_like(acc)

        acc[...] += lax.dot_general(a_ref[...].astype(BF16), b_ref[...].astype(BF16), dims,
                                    preferred_element_type=F32)

        @pl.when(k == nk - 1)
        def _():
            r = acc[...]
            if r_ref is not None:
                r = r + r_ref[...]
            if scale_cols is not None:
                assert scale_cols[0] % bn == 0
                r = r * jnp.where(col_block < scale_cols[0] // bn, scale_cols[1], 1.0)
            o_ref[...] = r.astype(out_dtype)

    return pl.pallas_call(
        body, out_shape=out_shape, grid=grid, in_specs=in_specs, out_specs=out_spec,
        scratch_shapes=[pltpu.VMEM((bm, bn), F32)], name=name,
        compiler_params=_params("parallel", "parallel", "arbitrary"),
    )(*args)


def _rowwise(body, n_rows, ins, outs, accs=(), *, name, block=ROW_BLOCK):
    blk = min(block, n_rows)
    assert n_rows % blk == 0
    in_specs = []
    for arr, is_row in ins:
        if is_row:
            assert arr.shape[0] == n_rows, (name, arr.shape, n_rows)
            in_specs.append(pl.BlockSpec((blk, arr.shape[1]), lambda i: (i, 0)))
        else:
            in_specs.append(pl.BlockSpec(arr.shape, lambda i, nd=arr.ndim: (0,) * nd))
    out_shape = [jax.ShapeDtypeStruct((n_rows, w), dt) for w, dt in outs]
    out_specs = [pl.BlockSpec((blk, w), lambda i: (i, 0)) for w, _ in outs]
    out_shape += [jax.ShapeDtypeStruct(s, dt) for s, dt in accs]
    out_specs += [pl.BlockSpec(s, lambda i, nd=len(s): (0,) * nd) for s, _ in accs]
    n_in, n_out, n_acc = len(ins), len(outs), len(accs)

    def kern(*refs):
        in_refs = refs[:n_in]
        out_refs = refs[n_in:n_in + n_out]
        acc_refs = refs[n_in + n_out:]
        if n_acc:
            @pl.when(pl.program_id(0) == 0)
            def _():
                for r in acc_refs:
                    r[...] = jnp.zeros_like(r)
        body(in_refs, out_refs, acc_refs)

    return pl.pallas_call(
        kern, out_shape=out_shape, grid=(n_rows // blk,), in_specs=in_specs, out_specs=out_specs,
        name=name, compiler_params=_params("arbitrary"),
    )(*[arr for arr, _ in ins])


def _rms(x, g, n=None):
    n = x.shape[-1] if n is None else n
    r = lax.rsqrt(jnp.sum(x * x, axis=-1, keepdims=True) / n + EPS)
    return x * r * g


def _rms_bwd(x, g, dy, n=None):
    n = x.shape[-1] if n is None else n
    r = lax.rsqrt(jnp.sum(x * x, axis=-1, keepdims=True) / n + EPS)
    gdy = dy * g
    dx = r * (gdy - x * ((r * r) * (jnp.sum(gdy * x, axis=-1, keepdims=True) / n)))
    dg = jnp.sum(dy * x * r, axis=0, keepdims=True)
    return dx, dg


def _swap_halves(x):
    lane = lax.broadcasted_iota(jnp.int32, x.shape, 1)
    return jnp.where(lane < ROPE_DIM // 2, pltpu.roll(x, 128 - ROPE_DIM // 2, 1),
                     pltpu.roll(x, ROPE_DIM // 2, 1))


def _rope(n, cos_t, sin_t):
    return n * cos_t + _swap_halves(n) * sin_t


def _rope_bwd(dy, cos_t, sin_t):
    return dy * cos_t - _swap_halves(dy) * sin_t


def _sigmoid(g):
    return 1.0 / (1.0 + jnp.exp(-g))


def _dot_t(a, b):
    return lax.dot_general(a, b, (((1,), (1,)), ((), ())), preferred_element_type=F32)


def _tdot(a, b):
    return lax.dot_general(a, b, (((0,), (0,)), ((), ())), preferred_element_type=F32)


def _dot(a, b):
    return jnp.dot(a, b, preferred_element_type=F32)


def _hs(h, w=HEAD_DIM, base=0):
    return slice(base + h * w, base + (h + 1) * w)


def _mem_head(qm, gq, mk_h, mv_h):
    qb = _rms(qm, gq).astype(BF16)
    s = _dot_t(qb, mk_h) * (HEAD_DIM ** -0.5)
    e = jnp.exp(s - jnp.max(s, axis=-1, keepdims=True))
    p = e / jnp.sum(e, axis=-1, keepdims=True)
    mo = _dot(p.astype(BF16), mv_h)
    return qb, p, mo


def _mix_fwd(att, gates, c0, mk, mv, gq, *, name):
    n_rows = att.shape[0]

    def body(ins, outs, _):
        att_ref, g_ref, mk_ref, mv_ref, gq_ref = ins
        (o_ref,) = outs
        g = g_ref[:, c0:c0 + SB_W]
        o_ref[:, :SB_W] = (att_ref[...] * (g * _sigmoid(g))).astype(BF16)
        for h in range(N_MEM_HEADS):
            qm = g_ref[:, _hs(h, base=c0 + SB_W)]
            gm = g_ref[:, _hs(h, base=c0 + SB_W + MEM_W)]
            _, _, mo = _mem_head(qm, gq_ref[...], mk_ref[:, _hs(h)], mv_ref[:, _hs(h)])
            o_ref[:, _hs(h, base=SB_W)] = (mo * (gm * _sigmoid(gm))).astype(BF16)

    (mixed,) = _rowwise(body, n_rows,
                        [(att, True), (gates, True), (mk, False), (mv, False), (gq, False)],
                        [(D_MODEL, BF16)], name=name)
    return mixed


def _mix_bwd(dmixed, att, gates, c0, mk, mv, gq, *, name):
    n_rows = att.shape[0]
    scale = HEAD_DIM ** -0.5

    def body(ins, outs, accs):
        dm_ref, att_ref, g_ref, mk_ref, mv_ref, gq_ref = ins
        datt_ref, dg_ref = outs
        dmk_ref, dmv_ref, dgq_ref = accs
        g = g_ref[:, c0:c0 + SB_W]
        sg = _sigmoid(g)
        dm = dm_ref[:, :SB_W]
        datt_ref[...] = dm * (g * sg)
        dg_ref[:, :SB_W] = (dm * att_ref[...] * (sg * (1.0 + g * (1.0 - sg)))).astype(BF16)
        for h in range(N_MEM_HEADS):
            qm = g_ref[:, _hs(h, base=c0 + SB_W)]
            gm = g_ref[:, _hs(h, base=c0 + SB_W + MEM_W)]
            mk_h = mk_ref[:, _hs(h)]
            mv_h = mv_ref[:, _hs(h)]
            qb, p, mo = _mem_head(qm, gq_ref[...], mk_h, mv_h)
            sgm = _sigmoid(gm)
            dmh = dm_ref[:, _hs(h, base=SB_W)]
            dmo = dmh * (gm * sgm)
            dg_ref[:, _hs(h, base=SB_W + MEM_W)] = (
                dmh * mo * (sgm * (1.0 + gm * (1.0 - sgm)))).astype(BF16)
            dmo_b = dmo.astype(BF16)
            pb = p.astype(BF16)
            dp = _dot_t(dmo_b, mv_h)
            dmv_ref[:, _hs(h)] += _tdot(pb, dmo_b)
            ds = (p * (dp - jnp.sum(dp * p, axis=-1, keepdims=True)) * scale).astype(BF16)
            dqn = _dot(ds, mk_h)
            dmk_ref[:, _hs(h)] += _tdot(ds, qb)
            dqm, dgq = _rms_bwd(qm, gq_ref[...], dqn)
            dg_ref[:, _hs(h, base=SB_W)] = dqm.astype(BF16)
            dgq_ref[...] += dgq

    return _rowwise(body, n_rows,
                    [(dmixed, True), (att, True), (gates, True), (mk, False), (mv, False), (gq, False)],
                    [(SB_W, F32), (GATE_W, BF16)],
                    [((MEM_LEN, MEM_W), F32), ((MEM_LEN, MEM_W), F32), ((1, HEAD_DIM), F32)],
                    name=name)


def _mem_side_fwd(mem, g_norm, w_kv, g_k, *, tag):
    def norm_body(ins, outs, _):
        outs[0][...] = _rms(ins[0][...], ins[1][...]).astype(BF16)

    (mn,) = _rowwise(norm_body, MEM_LEN, [(mem, True), (g_norm, False)], [(D_MODEL, BF16)],
                     name=f"mem_norm_{tag}")
    mkv = _mm(mn, w_kv, name=f"mem_kv_{tag}")

    def kv_body(ins, outs, _):
        mkv_ref, gk_ref = ins
        mk_ref, mv_ref = outs
        for h in range(N_MEM_HEADS):
            mk_ref[:, _hs(h)] = _rms(mkv_ref[:, _hs(h)], gk_ref[...]).astype(BF16)
        mv_ref[...] = mkv_ref[:, MEM_W:].astype(BF16)

    mk, mv = _rowwise(kv_body, MEM_LEN, [(mkv, True), (g_k, False)], [(MEM_W, BF16), (MEM_W, BF16)],
                      name=f"mem_kv_prep_{tag}")
    return mn, mkv, mk, mv


def _mem_side_bwd(mem, g_norm, w_kv, g_k, mn, mkv, dmk, dmv, *, tag):
    def kv_body(ins, outs, accs):
        mkv_ref, gk_ref, dmk_ref, dmv_ref = ins
        (d_ref,) = outs
        (dgk_ref,) = accs
        for h in range(N_MEM_HEADS):
            dx, dg = _rms_bwd(mkv_ref[:, _hs(h)], gk_ref[...], dmk_ref[:, _hs(h)])
            d_ref[:, _hs(h)] = dx.astype(BF16)
            dgk_ref[...] += dg
        d_ref[:, MEM_W:] = dmv_ref[...].astype(BF16)

    dmkv, dgk = _rowwise(kv_body, MEM_LEN, [(mkv, True), (g_k, False), (dmk, True), (dmv, True)],
                         [(2 * MEM_W, BF16)], [((1, HEAD_DIM), F32)], name=f"mem_kv_prep_bwd_{tag}")
    dmn = _mm(dmkv, w_kv, tb=True, name=f"mem_kv_dx_{tag}")
    dw = _mm(mn, dmkv, ta=True, out_dtype=BF16, name=f"mem_kv_dw_{tag}")

    def norm_body(ins, outs, accs):
        _, dg = _rms_bwd(ins[0][...], ins[1][...], ins[2][...])
        accs[0][...] += dg

    (dgn,) = _rowwise(norm_body, MEM_LEN, [(mem, True), (g_norm, False), (dmn, True)], [],
                      [((1, D_MODEL), F32)], name=f"mem_norm_bwd_{tag}")
    return dw, dgn, dgk


LOG2_E = 1.4426950408889634
SB_Q_SCALE = HEAD_DIM ** -0.5 * LOG2_E


Z2_CAP = 126.0


def _sb_terms(z2):
    zc = jnp.minimum(z2, Z2_CAP)
    w = 1.0 + jnp.exp2(zc)
    return zc, w, jnp.log2(w)


LOOP_UNROLL = 8


def _loop_blocks(base, qb, body, carry, *, reverse):
    def run(start, trips, unroll, c0):
        def trip(t, c):
            for u in range(unroll):
                p = start + t * unroll + u
                c = body(base - 1 - p if reverse else p, c)
            return c
        return lax.fori_loop(0, trips, trip, c0)

    if qb % LOOP_UNROLL == 0:
        return run(0, base // LOOP_UNROLL, LOOP_UNROLL, carry)
    small = qb
    n_big = base // LOOP_UNROLL
    carry = run(0, n_big, LOOP_UNROLL, carry)
    return run(n_big * LOOP_UNROLL, (base - n_big * LOOP_UNROLL) // small, small, carry)


def _chain_modes(s, qb):
    return tuple(None if t < s else ("m" if t == s else "f") for t in range(qb))


def _split_dot(x, tri2):
    hi = x.astype(BF16)
    lo = (x - hi.astype(F32)).astype(BF16)
    return _dot(jnp.concatenate([hi, lo], axis=1), tri2)


def _sb_fwd(qkv, *, name, hp=1):
    seq = qkv.shape[0]
    blk = min(ATT_BLOCK, seq)
    nkb = seq // blk
    qb = _pick(nkb, (4, 2, 1))
    rows = qb * blk
    chains = [(t, s) for t in range(hp) for s in range(qb)]

    def body(q_ref, k_ref, v_ref, o_ref):
        base = pl.program_id(1) * qb
        qs = {(t, s): q_ref[s * blk:(s + 1) * blk, _hs(t)] for t, s in chains}
        row = lax.broadcasted_iota(jnp.int32, (blk, blk), 0)
        col = lax.broadcasted_iota(jnp.int32, (blk, blk), 1)
        after = (row > col).astype(BF16)
        after2 = jnp.concatenate([after, after], axis=0)
        causal = col < row

        def step(j, carry, modes):
            off = pl.multiple_of(j * blk, blk)
            act = [c for c in chains if modes[c[1]]]
            zs, ls = {}, {}
            for c in act:
                zs[c], _, l = _sb_terms(_dot_t(qs[c], k_ref[pl.ds(off, blk), _hs(c[0])]))
                ls[c] = jnp.where(causal, l, 0.0) if modes[c[1]] == "m" else l
            cs = {c: _split_dot(ls[c], after2) for c in act}
            carry = dict(carry)
            for c in act:
                run, acc = carry[c]
                a = jnp.exp2(zs[c] - ls[c] - cs[c] - run)
                if modes[c[1]] == "m":
                    a = jnp.where(causal, a, 0.0)
                acc = acc + _dot(a.astype(BF16), v_ref[pl.ds(off, blk), _hs(c[0])])
                carry[c] = (run + jnp.sum(ls[c], axis=-1, keepdims=True), acc)
            return carry

        init = (jnp.zeros((blk, 1), F32), jnp.zeros((blk, HEAD_DIM), F32))
        carry = {c: init for c in chains}
        for s in reversed(range(qb)):
            carry = step(base + s, carry, _chain_modes(s, qb))
        carry = _loop_blocks(base, qb, lambda j, c: step(j, c, ("f",) * qb), carry, reverse=True)
        for t, s in chains:
            o_ref[s * blk:(s + 1) * blk, _hs(t)] = carry[(t, s)][1]

    nh = N_SB_HEADS // hp
    return pl.pallas_call(
        body, out_shape=jax.ShapeDtypeStruct((seq, SB_W), F32), grid=(nh, nkb // qb),
        in_specs=[pl.BlockSpec((rows, hp * HEAD_DIM), lambda h, i: (i, h)),
                  pl.BlockSpec((seq, hp * HEAD_DIM), lambda h, i: (0, nh + h)),
                  pl.BlockSpec((seq, hp * HEAD_DIM), lambda h, i: (0, 2 * nh + h))],
        out_specs=pl.BlockSpec((rows, hp * HEAD_DIM), lambda h, i: (i, h)),
        name=name, compiler_params=_params("parallel", "arbitrary"),
    )(qkv, qkv, qkv)


SB_BWD_GROUP = 4


def _sb_bwd(qkv, out, dout, *, name):
    seq = qkv.shape[0]
    blk = min(ATT_BLOCK, seq)
    nkb = seq // blk
    qb = _pick(nkb, (4, 2, 1))
    rows = qb * blk
    scale = HEAD_DIM ** -0.5

    def body(q_ref, k_ref, v_ref, do_ref, o_ref, dq_ref, dk_out, dv_out, dk_ref, dv_ref):
        g = pl.program_id(1)
        base = g * qb

        @pl.when(g == 0)
        def _():
            dk_ref[...] = jnp.zeros_like(dk_ref)
            dv_ref[...] = jnp.zeros_like(dv_ref)

        qs = [q_ref[t * blk:(t + 1) * blk, :] for t in range(qb)]
        dos = [do_ref[t * blk:(t + 1) * blk, :].astype(BF16) for t in range(qb)]
        totals = [jnp.sum(dos[t].astype(F32) * o_ref[t * blk:(t + 1) * blk, :], axis=-1, keepdims=True)
                  for t in range(qb)]
        row = lax.broadcasted_iota(jnp.int32, (blk, blk), 0)
        col = lax.broadcasted_iota(jnp.int32, (blk, blk), 1)
        after = (row > col).astype(BF16)
        after2 = jnp.concatenate([after, after], axis=0)
        from_s = (row >= col).astype(BF16)
        from_s2 = jnp.concatenate([from_s, from_s], axis=0)
        causal = col < row

        def step(j, carry, modes):
            runs, rights, dqs = list(carry[0]), list(carry[1]), list(carry[2])
            off = pl.multiple_of(j * blk, blk)
            kb = k_ref[pl.ds(off, blk), :]
            vb = v_ref[pl.ds(off, blk), :]
            dv_inc = dk_inc = None
            for first in range(0, qb, SB_BWD_GROUP):
                act = [t for t in range(first, min(first + SB_BWD_GROUP, qb)) if modes[t]]
                das = {t: _dot_t(dos[t], vb) for t in act}
                zs, ls, sns = {}, {}, {}
                for t in act:
                    zs[t], w, l = _sb_terms(_dot_t(qs[t], kb))
                    sns[t] = pl.reciprocal(w, approx=True)
                    ls[t] = jnp.where(causal, l, 0.0) if modes[t] == "m" else l
                cs = {t: _split_dot(ls[t], after2) for t in act}
                abs_, des = {}, {}
                for t in act:
                    a = jnp.exp2(zs[t] - ls[t] - cs[t] - runs[t])
                    if modes[t] == "m":
                        a = jnp.where(causal, a, 0.0)
                    abs_[t] = a.astype(BF16)
                    des[t] = abs_[t].astype(F32) * das[t]
                sufs = {t: _split_dot(des[t], from_s2) for t in act}
                for t in act:
                    left = totals[t] - (sufs[t] + rights[t])
                    dz = (des[t] + left) * sns[t] - left
                    if modes[t] == "m":
                        dz = jnp.where(causal, dz, 0.0)
                    dzb = dz.astype(BF16)
                    dqs[t] = dqs[t] + _dot(dzb, kb)
                    inc_v = _tdot(abs_[t], dos[t])
                    inc_k = _tdot(dzb, qs[t])
                    dv_inc = inc_v if dv_inc is None else dv_inc + inc_v
                    dk_inc = inc_k if dk_inc is None else dk_inc + inc_k
                    runs[t] = runs[t] + jnp.sum(ls[t], axis=-1, keepdims=True)
                    rights[t] = rights[t] + jnp.sum(des[t], axis=-1, keepdims=True)
            dv_ref[pl.ds(off, blk), :] += dv_inc
            dk_ref[pl.ds(off, blk), :] += dk_inc
            return tuple(runs), tuple(rights), tuple(dqs)

        zero = (jnp.zeros((blk, 1), F32),) * qb
        carry = (zero, zero, (jnp.zeros((blk, HEAD_DIM), F32),) * qb)
        for s in reversed(range(qb)):
            carry = step(base + s, carry, _chain_modes(s, qb))
        carry = _loop_blocks(base, qb, lambda j, c: step(j, c, ("f",) * qb), carry, reverse=True)
        for t in range(qb):
            dq_ref[t * blk:(t + 1) * blk, :] = (carry[2][t] * scale).astype(BF16)

        @pl.when(g == pl.num_programs(1) - 1)
        def _():
            dk_out[...] = (dk_ref[...] * (1.0 / LOG2_E)).astype(BF16)
            dv_out[...] = dv_ref[...].astype(BF16)

    out_sd = jax.ShapeDtypeStruct((seq, SB_W), BF16)
    return pl.pallas_call(
        body, out_shape=[out_sd, out_sd, out_sd], grid=(N_SB_HEADS, nkb // qb),
        in_specs=[pl.BlockSpec((rows, HEAD_DIM), lambda h, i: (i, h)),
                  pl.BlockSpec((seq, HEAD_DIM), lambda h, i: (0, N_SB_HEADS + h)),
                  pl.BlockSpec((seq, HEAD_DIM), lambda h, i: (0, 2 * N_SB_HEADS + h)),
                  pl.BlockSpec((rows, HEAD_DIM), lambda h, i: (i, h)),
                  pl.BlockSpec((rows, HEAD_DIM), lambda h, i: (i, h))],
        out_specs=[pl.BlockSpec((rows, HEAD_DIM), lambda h, i: (i, h)),
                   pl.BlockSpec((seq, HEAD_DIM), lambda h, i: (0, h)),
                   pl.BlockSpec((seq, HEAD_DIM), lambda h, i: (0, h))],
        scratch_shapes=[pltpu.VMEM((seq, HEAD_DIM), F32), pltpu.VMEM((seq, HEAD_DIM), F32)],
        name=name, compiler_params=_params("parallel", "arbitrary"),
    )(qkv, qkv, qkv, dout, out)


MLA_SCALE = (HEAD_DIM + ROPE_DIM) ** -0.5
MLA_Q_SCALE = MLA_SCALE * LOG2_E


def _mla_fwd(q_cat, k_cat, v, *, name, hp=1):
    seq = q_cat.shape[0]
    blk = min(ATT_BLOCK, seq)
    nkb = seq // blk
    qb = _pick(nkb, (4, 2, 1))
    rows = qb * blk
    chains = [(t, s) for t in range(hp) for s in range(qb)]

    def body(q_ref, k_ref, v_ref, o_ref, lse_ref):
        base = pl.program_id(1) * qb
        qs = {(t, s): q_ref[s * blk:(s + 1) * blk, t * CAT_W:(t + 1) * CAT_W] for t, s in chains}
        row = lax.broadcasted_iota(jnp.int32, (blk, blk), 0)
        col = lax.broadcasted_iota(jnp.int32, (blk, blk), 1)
        causal = col <= row

        def step(j, carry, modes):
            off = pl.multiple_of(j * blk, blk)
            act = [c for c in chains if modes[c[1]]]
            ss = {c: _dot_t(qs[c], k_ref[pl.ds(off, blk), c[0] * CAT_W:(c[0] + 1) * CAT_W]) for c in act}
            carry = dict(carry)
            for c in act:
                m, l, acc = carry[c]
                s = ss[c]
                if modes[c[1]] == "m":
                    s = jnp.where(causal, s, -jnp.inf)
                m_new = jnp.maximum(m, jnp.max(s, axis=-1, keepdims=True))
                p = jnp.exp2(s - m_new)
                alpha = jnp.exp2(m - m_new)
                l = alpha * l + jnp.sum(p, axis=-1, keepdims=True)
                acc = alpha * acc + _dot(p.astype(BF16), v_ref[pl.ds(off, blk), _hs(c[0])])
                carry[c] = (m_new, l, acc)
            return carry

        init = (jnp.full((blk, 1), -jnp.inf, F32), jnp.zeros((blk, 1), F32),
                jnp.zeros((blk, HEAD_DIM), F32))
        carry = {c: init for c in chains}
        carry = _loop_blocks(base, qb, lambda j, c: step(j, c, ("f",) * qb), carry, reverse=False)
        for s in range(qb):
            carry = step(base + s, carry, _chain_modes(s, qb))
        for t, s in chains:
            m, l, acc = carry[(t, s)]
            o_ref[s * blk:(s + 1) * blk, _hs(t)] = acc / l
            lse_ref[s * blk:(s + 1) * blk, _hs(t)] = jnp.broadcast_to(
                (m + jnp.log2(l)) * (1.0 / LOG2_E), (blk, HEAD_DIM))

    out = jax.ShapeDtypeStruct((seq, MLA_W), F32)
    return pl.pallas_call(
        body, out_shape=[out, out], grid=(N_MLA_HEADS // hp, nkb // qb),
        in_specs=[pl.BlockSpec((rows, hp * CAT_W), lambda h, i: (i, h)),
                  pl.BlockSpec((seq, hp * CAT_W), lambda h, i: (0, h)),
                  pl.BlockSpec((seq, hp * HEAD_DIM), lambda h, i: (0, h))],
        out_specs=[pl.BlockSpec((rows, hp * HEAD_DIM), lambda h, i: (i, h)),
                   pl.BlockSpec((rows, hp * HEAD_DIM), lambda h, i: (i, h))],
        name=name, compiler_params=_params("parallel", "arbitrary"),
    )(q_cat, k_cat, v)


def _mla_bwd(q_cat, k_cat, v, out, lse, dout, *, name):
    seq = q_cat.shape[0]
    blk = min(ATT_BLOCK, seq)
    nkb = seq // blk
    qb = _pick(nkb, (4, 2, 1))
    rows = qb * blk

    def body(q_ref, k_ref, v_ref, o_ref, lse_ref, do_ref, dq_ref, dk_ref, dv_ref):
        g = pl.program_id(1)
        base = g * qb

        @pl.when(g == 0)
        def _():
            dk_ref[...] = jnp.zeros_like(dk_ref)
            dv_ref[...] = jnp.zeros_like(dv_ref)

        qs, dobs, deltas, lses = [], [], [], []
        for t in range(qb):
            rs = slice(t * blk, (t + 1) * blk)
            do = do_ref[rs, :]
            qs.append(q_ref[rs, :])
            dobs.append(do.astype(BF16))
            deltas.append(jnp.sum(do * o_ref[rs, :], axis=-1, keepdims=True))
            lses.append(lse_ref[rs, :1] * LOG2_E)
        row = lax.broadcasted_iota(jnp.int32, (blk, blk), 0)
        col = lax.broadcasted_iota(jnp.int32, (blk, blk), 1)
        causal = col <= row

        def step(j, dqs, modes):
            off = pl.multiple_of(j * blk, blk)
            kb = k_ref[pl.ds(off, blk), :]
            vb = v_ref[pl.ds(off, blk), :]
            act = [t for t in range(qb) if modes[t]]
            ss = {t: _dot_t(qs[t], kb) for t in act}
            dps = {t: _dot_t(dobs[t], vb) for t in act}
            dqs = list(dqs)
            dv_inc = dk_inc = None
            for t in act:
                p = jnp.exp2(ss[t] - lses[t])
                if modes[t] == "m":
                    p = jnp.where(causal, p, 0.0)
                ds = (p * (dps[t] - deltas[t])).astype(BF16)
                inc_v = _tdot(p.astype(BF16), dobs[t])
                inc_k = _tdot(ds, qs[t])
                dv_inc = inc_v if dv_inc is None else dv_inc + inc_v
                dk_inc = inc_k if dk_inc is None else dk_inc + inc_k
                dqs[t] = dqs[t] + _dot(ds, kb)
            dv_ref[pl.ds(off, blk), :] += dv_inc
            dk_ref[pl.ds(off, blk), :] += dk_inc
            return tuple(dqs)

        dqs = (jnp.zeros((blk, CAT_W), F32),) * qb
        dqs = _loop_blocks(base, qb, lambda j, c: step(j, c, ("f",) * qb), dqs, reverse=False)
        for s in range(qb):
            modes = tuple(None if t < s else ("m" if t == s else "f") for t in range(qb))
            dqs = step(base + s, dqs, modes)
        for t in range(qb):
            dq_ref[t * blk:(t + 1) * blk, :] = dqs[t] * MLA_SCALE

        @pl.when(g == pl.num_programs(1) - 1)
        def _():
            dk_ref[...] = dk_ref[...] * (1.0 / LOG2_E)

    return pl.pallas_call(
        body,
        out_shape=[jax.ShapeDtypeStruct((seq, N_MLA_HEADS * CAT_W), F32),
                   jax.ShapeDtypeStruct((seq, N_MLA_HEADS * CAT_W), F32),
                   jax.ShapeDtypeStruct((seq, MLA_W), F32)],
        grid=(N_MLA_HEADS, nkb // qb),
        in_specs=[pl.BlockSpec((rows, CAT_W), lambda h, i: (i, h)),
                  pl.BlockSpec((seq, CAT_W), lambda h, i: (0, h)),
                  pl.BlockSpec((seq, HEAD_DIM), lambda h, i: (0, h)),
                  pl.BlockSpec((rows, HEAD_DIM), lambda h, i: (i, h)),
                  pl.BlockSpec((rows, HEAD_DIM), lambda h, i: (i, h)),
                  pl.BlockSpec((rows, HEAD_DIM), lambda h, i: (i, h))],
        out_specs=[pl.BlockSpec((rows, CAT_W), lambda h, i: (i, h)),
                   pl.BlockSpec((seq, CAT_W), lambda h, i: (0, h)),
                   pl.BlockSpec((seq, HEAD_DIM), lambda h, i: (0, h))],
        name=name, compiler_params=_params("parallel", "arbitrary"),
    )(q_cat, k_cat, v, out, lse, dout)


def _local_step(x, mem, positions, target, w, g):
    seq = x.shape[0]
    inv_freq = jnp.power(ROPE_THETA, -jnp.arange(0, ROPE_DIM, 2, dtype=F32) / ROPE_DIM)
    ang = positions.astype(F32)[:, None] * inv_freq
    cos, sin = jnp.cos(ang), jnp.sin(ang)
    lane_pad = jnp.zeros((seq, HEAD_DIM - ROPE_DIM), F32)
    cos_t = jnp.concatenate([cos, cos, lane_pad], axis=1)
    sin_t = jnp.concatenate([-sin, sin, lane_pad], axis=1)
    gain_pad = jnp.zeros((1, HEAD_DIM - ROPE_DIM), F32)
    g_k_rope = jnp.concatenate([g["g_k_rope"], gain_pad], axis=1)
    g_q_rope = jnp.concatenate([g["b_g_q_rope"], gain_pad], axis=1)

    def norm_to_bf16(src, gain, name):
        def body(ins, outs, _):
            outs[0][...] = _rms(ins[0][...], ins[1][...]).astype(BF16)
        return _rowwise(body, seq, [(src, True), (gain, False)], [(src.shape[1], BF16)], name=name)[0]

    h_a = norm_to_bf16(x, g["a_norm"], "a_norm_fwd")
    qkv = _mm(h_a, w["a_in_qkv"], out_dtype=BF16, scale_cols=(SB_W, SB_Q_SCALE), name="a_in_qkv")
    gr = _mm(h_a, w["a_in_gate"], name="a_in_gate")
    sb = _sb_fwd(qkv, name="sb_fwd")
    mem0 = _mem_side_fwd(mem, g["mem_norm"][0:1], w["mem_kv"][0], g["g_mem_k"][0:1], tag="a")
    mixed_a = _mix_fwd(sb, gr, 0, mem0[2], mem0[3], g["g_mem_q"][0:1], name="a_mix_fwd")
    x1 = _mm(mixed_a, w["a_out"], res=x, name="a_out")

    def norms2_body(ins, outs, _):
        xv = ins[0][...]
        outs[0][...] = _rms(xv, ins[1][...]).astype(BF16)
        outs[1][...] = _rms(xv, ins[2][...]).astype(BF16)

    h_kv, h_b = _rowwise(norms2_body, seq, [(x1, True), (g["kv_norm"], False), (g["b_norm"], False)],
                         [(D_MODEL, BF16), (D_MODEL, BF16)], name="kv_b_norm_fwd")
    ckr = _mm(h_kv, w["dkv"], name="dkv")

    def ckr_body(ins, outs, _):
        ckr_ref, gc_ref, gr_ref, c_ref, s_ref = ins
        outs[0][...] = _rms(ckr_ref[:, :KV_LORA], gc_ref[...]).astype(BF16)
        kr = _rms(ckr_ref[:, KV_LORA:], gr_ref[...], n=ROPE_DIM)
        outs[1][...] = _rope(kr, c_ref[...], s_ref[...]).astype(BF16)

    c_n, k_r = _rowwise(ckr_body, seq,
                        [(ckr, True), (g["g_ckv"], False), (g_k_rope, False), (cos_t, True), (sin_t, True)],
                        [(KV_LORA, BF16), (HEAD_DIM, BF16)], name="ckv_prep_fwd")
    kv = _mm(c_n, w["ukv"], name="ukv")

    def kcat_body(ins, outs, _):
        kv_ref, kr_ref, gk_ref = ins
        kc_ref, v_ref = outs
        for h in range(N_MLA_HEADS):
            kc_ref[:, h * CAT_W:h * CAT_W + HEAD_DIM] = _rms(
                kv_ref[:, h * CAT_W:h * CAT_W + HEAD_DIM], gk_ref[...]).astype(BF16)
            kc_ref[:, h * CAT_W + HEAD_DIM:(h + 1) * CAT_W] = kr_ref[...]
            v_ref[:, _hs(h)] = kv_ref[:, h * CAT_W + HEAD_DIM:(h + 1) * CAT_W].astype(BF16)

    k_cat, v_mla = _rowwise(kcat_body, seq, [(kv, True), (k_r, True), (g["g_k_nope"], False)],
                            [(N_MLA_HEADS * CAT_W, BF16), (MLA_W, BF16)], name="k_prep_fwd")

    p2 = _mm(h_b, w["b_in"], name="b_in")

    def qlat_body(ins, outs, _):
        outs[0][...] = _rms(ins[0][:, :Q_LORA], ins[1][...]).astype(BF16)

    (q_l,) = _rowwise(qlat_body, seq, [(p2, True), (g["b_g_q_lat"], False)], [(Q_LORA, BF16)],
                      name="q_lat_norm_fwd")
    q_up = _mm(q_l, w["uq"], name="uq")

    def qcat_body(ins, outs, _):
        q_ref, gn_ref, gr_ref, c_ref, s_ref = ins
        (o_ref,) = outs
        for h in range(N_MLA_HEADS):
            o_ref[:, h * CAT_W:h * CAT_W + HEAD_DIM] = (MLA_Q_SCALE * _rms(
                q_ref[:, h * CAT_W:h * CAT_W + HEAD_DIM], gn_ref[...])).astype(BF16)
            qr = _rms(q_ref[:, h * CAT_W + HEAD_DIM:(h + 1) * CAT_W], gr_ref[...], n=ROPE_DIM)
            o_ref[:, h * CAT_W + HEAD_DIM:(h + 1) * CAT_W] = (
                MLA_Q_SCALE * _rope(qr, c_ref[...], s_ref[...])).astype(BF16)

    (q_cat,) = _rowwise(qcat_body, seq,
                        [(q_up, True), (g["b_g_q_nope"], False), (g_q_rope, False), (cos_t, True), (sin_t, True)],
                        [(N_MLA_HEADS * CAT_W, BF16)], name="q_prep_fwd")
    att, lse = _mla_fwd(q_cat, k_cat, v_mla, name="mla_fwd")
    mem1 = _mem_side_fwd(mem, g["mem_norm"][1:2], w["mem_kv"][1], g["g_mem_k"][1:2], tag="b")
    mixed_b = _mix_fwd(att, p2, Q_LORA, mem1[2], mem1[3], g["g_mem_q"][1:2], name="b_mix_fwd")
    y = _mm(mixed_b, w["b_out"], res=x1, name="b_out")

    def loss_body(ins, outs, accs):
        diff = ins[0][...] - ins[1][...]
        outs[0][...] = diff / D_MODEL
        col = jnp.sum(diff * diff, axis=0, keepdims=True)
        part = col[:, :HEAD_DIM]
        for c in range(1, D_MODEL // HEAD_DIM):
            part = part + col[:, _hs(c)]
        accs[0][...] += part * (0.5 / D_MODEL)

    dy, loss_part = _rowwise(loss_body, seq, [(y, True), (target, True)], [(D_MODEL, F32)],
                             [((1, HEAD_DIM), F32)], name="loss")

    gw, gg = {}, {}
    dmixed_b = _mm(dy, w["b_out"], tb=True, name="b_out_dx")
    gw["b_out"] = _mm(mixed_b, dy, ta=True, out_dtype=BF16, name="b_out_dw")
    datt, dgate_b, dmk1, dmv1, gq1 = _mix_bwd(dmixed_b, att, p2, Q_LORA, mem1[2], mem1[3],
                                              g["g_mem_q"][1:2], name="b_mix_bwd")
    dq_cat, dk_cat, dv_mla = _mla_bwd(q_cat, k_cat, v_mla, att, lse, datt, name="mla_bwd")

    def qcat_bwd_body(ins, outs, accs):
        q_ref, dq_ref, gn_ref, gr_ref, c_ref, s_ref = ins
        (o_ref,) = outs
        dgn_ref, dgr_ref = accs
        for h in range(N_MLA_HEADS):
            dx, dg = _rms_bwd(q_ref[:, h * CAT_W:h * CAT_W + HEAD_DIM], gn_ref[...],
                              dq_ref[:, h * CAT_W:h * CAT_W + HEAD_DIM])
            o_ref[:, h * CAT_W:h * CAT_W + HEAD_DIM] = dx.astype(BF16)
            dgn_ref[...] += dg
            dn = _rope_bwd(dq_ref[:, h * CAT_W + HEAD_DIM:(h + 1) * CAT_W], c_ref[...], s_ref[...])
            dx, dg = _rms_bwd(q_ref[:, h * CAT_W + HEAD_DIM:(h + 1) * CAT_W], gr_ref[...], dn, n=ROPE_DIM)
            o_ref[:, h * CAT_W + HEAD_DIM:(h + 1) * CAT_W] = dx.astype(BF16)
            dgr_ref[...] += dg

    dq_up, gg["b_g_q_nope"], dgqr = _rowwise(
        qcat_bwd_body, seq,
        [(q_up, True), (dq_cat, True), (g["b_g_q_nope"], False), (g_q_rope, False), (cos_t, True), (sin_t, True)],
        [(N_MLA_HEADS * CAT_W, BF16)], [((1, HEAD_DIM), F32), ((1, HEAD_DIM), F32)], name="q_prep_bwd")
    gg["b_g_q_rope"] = dgqr
    dq_l = _mm(dq_up, w["uq"], tb=True, name="uq_dx")
    gw["uq"] = _mm(q_l, dq_up, ta=True, out_dtype=BF16, n_split=N_CHIPS, name="uq_dw")

    def qlat_bwd_body(ins, outs, accs):
        p2_ref, dql_ref, dgate_ref, gl_ref = ins
        dx, dg = _rms_bwd(p2_ref[:, :Q_LORA], gl_ref[...], dql_ref[...])
        outs[0][:, :Q_LORA] = dx.astype(BF16)
        outs[0][:, Q_LORA:] = dgate_ref[...]
        accs[0][...] += dg

    dp2, gg["b_g_q_lat"] = _rowwise(
        qlat_bwd_body, seq, [(p2, True), (dq_l, True), (dgate_b, True), (g["b_g_q_lat"], False)],
        [(Q_LORA + GATE_W, BF16)], [((1, Q_LORA), F32)], name="q_lat_norm_bwd")
    dh_b = _mm(dp2, w["b_in"], tb=True, name="b_in_dx")
    gw["b_in"] = _mm(h_b, dp2, ta=True, out_dtype=BF16, n_split=N_CHIPS, name="b_in_dw")

    def kcat_bwd_body(ins, outs, accs):
        kv_ref, dkc_ref, dv_ref, gk_ref = ins
        dkv_ref, dkr_ref = outs
        (dgk_ref,) = accs
        dkr = jnp.zeros(dkr_ref.shape, F32)
        for h in range(N_MLA_HEADS):
            dx, dg = _rms_bwd(kv_ref[:, h * CAT_W:h * CAT_W + HEAD_DIM], gk_ref[...],
                              dkc_ref[:, h * CAT_W:h * CAT_W + HEAD_DIM])
            dkv_ref[:, h * CAT_W:h * CAT_W + HEAD_DIM] = dx.astype(BF16)
            dgk_ref[...] += dg
            dkv_ref[:, h * CAT_W + HEAD_DIM:(h + 1) * CAT_W] = dv_ref[:, _hs(h)].astype(BF16)
            dkr = dkr + dkc_ref[:, h * CAT_W + HEAD_DIM:(h + 1) * CAT_W]
        dkr_ref[...] = dkr

    dkv, dk_r, gg["g_k_nope"] = _rowwise(
        kcat_bwd_body, seq, [(kv, True), (dk_cat, True), (dv_mla, True), (g["g_k_nope"], False)],
        [(N_MLA_HEADS * CAT_W, BF16), (HEAD_DIM, F32)], [((1, HEAD_DIM), F32)], name="k_prep_bwd")
    dc_n = _mm(dkv, w["ukv"], tb=True, name="ukv_dx")
    gw["ukv"] = _mm(c_n, dkv, ta=True, out_dtype=BF16, n_split=N_CHIPS, name="ukv_dw")

    def ckr_bwd_body(ins, outs, accs):
        ckr_ref, dcn_ref, dkr_ref, gc_ref, gr_ref, c_ref, s_ref = ins
        dx, dg = _rms_bwd(ckr_ref[:, :KV_LORA], gc_ref[...], dcn_ref[...])
        outs[0][:, :KV_LORA] = dx.astype(BF16)
        accs[0][...] += dg
        dn = _rope_bwd(dkr_ref[...], c_ref[...], s_ref[...])
        dx, dg = _rms_bwd(ckr_ref[:, KV_LORA:], gr_ref[...], dn, n=ROPE_DIM)
        outs[0][:, KV_LORA:] = dx.astype(BF16)
        accs[1][...] += dg

    dckr, gg["g_ckv"], gg["g_k_rope"] = _rowwise(
        ckr_bwd_body, seq,
        [(ckr, True), (dc_n, True), (dk_r, True), (g["g_ckv"], False), (g_k_rope, False),
         (cos_t, True), (sin_t, True)],
        [(KV_LORA + HEAD_DIM, BF16)], [((1, KV_LORA), F32), ((1, HEAD_DIM), F32)], name="ckv_prep_bwd")
    dh_kv = _mm(dckr, w["dkv"], tb=True, name="dkv_dx")
    gw["dkv"] = _mm(h_kv, dckr, ta=True, out_dtype=BF16, name="dkv_dw")

    def norms2_bwd_body(ins, outs, accs):
        x_ref, dy_ref, dhk_ref, dhb_ref, gk_ref, gb_ref = ins
        xv = x_ref[...]
        dxk, dgk = _rms_bwd(xv, gk_ref[...], dhk_ref[...])
        dxb, dgb = _rms_bwd(xv, gb_ref[...], dhb_ref[...])
        outs[0][...] = dy_ref[...] + dxk + dxb
        accs[0][...] += dgk
        accs[1][...] += dgb

    dx1, gg["kv_norm"], gg["b_norm"] = _rowwise(
        norms2_bwd_body, seq,
        [(x1, True), (dy, True), (dh_kv, True), (dh_b, True), (g["kv_norm"], False), (g["b_norm"], False)],
        [(D_MODEL, F32)], [((1, D_MODEL), F32), ((1, D_MODEL), F32)], name="kv_b_norm_bwd")

    dmixed_a = _mm(dx1, w["a_out"], tb=True, name="a_out_dx")
    gw["a_out"] = _mm(mixed_a, dx1, ta=True, out_dtype=BF16, name="a_out_dw")
    dsb, dgate_a, dmk0, dmv0, gq0 = _mix_bwd(dmixed_a, sb, gr, 0, mem0[2], mem0[3],
                                             g["g_mem_q"][0:1], name="a_mix_bwd")
    dq, dk, dv = _sb_bwd(qkv, sb, dsb, name="sb_bwd")
    dp_a = jnp.concatenate([dq, dk, dv, dgate_a], axis=1)
    dh_a = _mm(dp_a, w["a_in"], tb=True, name="a_in_dx")
    gw["a_in"] = _mm(h_a, dp_a, ta=True, out_dtype=BF16, n_split=N_CHIPS, name="a_in_dw")

    def norm_a_bwd_body(ins, outs, accs):
        dx, dg = _rms_bwd(ins[0][...], ins[3][...], ins[2][...])
        outs[0][...] = ins[1][...] + dx
        accs[0][...] += dg

    grad_x, gg["a_norm"] = _rowwise(
        norm_a_bwd_body, seq, [(x, True), (dx1, True), (dh_a, True), (g["a_norm"], False)],
        [(D_MODEL, F32)], [((1, D_MODEL), F32)], name="a_norm_bwd")

    dw0, dgn0, dgk0 = _mem_side_bwd(mem, g["mem_norm"][0:1], w["mem_kv"][0], g["g_mem_k"][0:1],
                                    mem0[0], mem0[1], dmk0, dmv0, tag="a")
    dw1, dgn1, dgk1 = _mem_side_bwd(mem, g["mem_norm"][1:2], w["mem_kv"][1], g["g_mem_k"][1:2],
                                    mem1[0], mem1[1], dmk1, dmv1, tag="b")
    gw["mem_kv"] = (dw0, dw1)
    gg["mem_norm"] = jnp.concatenate([dgn0, dgn1], axis=0)
    gg["g_mem_q"] = jnp.concatenate([gq0, gq1], axis=0)
    gg["g_mem_k"] = jnp.concatenate([dgk0, dgk1], axis=0)
    return loss_part, grad_x, gw, gg


HBM_SPEC = pl.BlockSpec(memory_space=pl.ANY)


def _other_chips():
    x, y = lax.axis_index("x"), lax.axis_index("y")
    return [(1 - x, y), (x, 1 - y), (1 - x, 1 - y)]


def _allgather_chips(shards):
    n = len(shards)
    split = [s.shape[0] % 32 == 0 for s in shards]

    def body(*refs):
        ins, outs = refs[:n], refs[n:2 * n]
        send, recv, fsend, frecv = refs[2 * n:]
        x, y, c = lax.axis_index("x"), lax.axis_index("y"), lax.axis_index("c")
        me = 2 * x + y
        chips = _other_chips()

        def part(ref, wi):
            if not split[wi]:
                return ref
            half = shards[wi].shape[0] // 2
            return ref.at[pl.ds(pl.multiple_of(c * half, 16), half)]

        def ici(wi, k, src_chip, to):
            return pltpu.make_async_remote_copy(
                src_ref=part(ins[wi], wi), dst_ref=part(outs[wi].at[src_chip], wi),
                send_sem=send.at[wi, k], recv_sem=recv.at[wi, k], device_id=to, device_id_type=MESH)

        def d2d(wi, k, src_chip):
            rows = part(outs[wi].at[src_chip], wi)
            return pltpu.make_async_remote_copy(
                src_ref=rows, dst_ref=rows, send_sem=fsend.at[wi, k], recv_sem=frecv.at[wi, k],
                device_id=(x, y, 1 - c), device_id_type=MESH)

        for wi in range(n):
            for k, (tx, ty) in enumerate(chips):
                ici(wi, k, me, (tx, ty, c)).start()
        for wi in range(n):
            for k, (tx, ty) in enumerate(chips):
                landed = ici(wi, k, 2 * tx + ty, (tx, ty, c))
                landed.wait_recv()
                if split[wi]:
                    d2d(wi, k, 2 * tx + ty).start()
        for wi in range(n):
            for k, (tx, ty) in enumerate(chips):
                ici(wi, k, me, (tx, ty, c)).wait_send()
                if split[wi]:
                    fwd = d2d(wi, k, 2 * tx + ty)
                    fwd.wait_send()
                    fwd.wait_recv()

    return pl.pallas_call(
        body, out_shape=[jax.ShapeDtypeStruct((N_CHIPS,) + s.shape, s.dtype) for s in shards],
        in_specs=[HBM_SPEC] * n, out_specs=[HBM_SPEC] * n,
        scratch_shapes=[pltpu.SemaphoreType.DMA((n, 3)), pltpu.SemaphoreType.DMA((n, 3)),
                        pltpu.SemaphoreType.DMA((n, 3)), pltpu.SemaphoreType.DMA((n, 3))],
        name="allgather_weights",
    )(*shards)


def _scatter_to_chips(grads):
    n = len(grads)

    def body(*refs):
        ins, outs = refs[:n], refs[n:2 * n]
        send, recv = refs[2 * n:]
        c = lax.axis_index("c")
        copies = []
        for wi in range(n):
            for k, (tx, ty) in enumerate(_other_chips()):
                cp = pltpu.make_async_remote_copy(
                    src_ref=ins[wi].at[2 * tx + ty], dst_ref=outs[wi].at[k], send_sem=send.at[wi, k],
                    recv_sem=recv.at[wi, k], device_id=(tx, ty, c), device_id_type=MESH)
                cp.start()
                copies.append(cp)
        for cp in copies:
            cp.wait()

    return pl.pallas_call(
        body, out_shape=[jax.ShapeDtypeStruct((3,) + s.shape[1:], s.dtype) for s in grads],
        in_specs=[HBM_SPEC] * n, out_specs=[HBM_SPEC] * n,
        scratch_shapes=[pltpu.SemaphoreType.DMA((n, 3)), pltpu.SemaphoreType.DMA((n, 3))],
        name="scatter_grads",
    )(*grads)


def _halve_with_sibling(grads):
    n = len(grads)
    n_slots = grads[0].shape[0]

    def body(*refs):
        ins, got = refs[:n], refs[n:2 * n]
        send, recv = refs[2 * n:]
        c = lax.axis_index("c")
        sib = (lax.axis_index("x"), lax.axis_index("y"), 1 - c)
        copies = []
        for wi in range(n):
            half = grads[wi].shape[1] // 2
            for s in range(n_slots):
                theirs = ins[wi].at[s, pl.ds(pl.multiple_of((1 - c) * half, 16), half)]
                give = pltpu.make_async_remote_copy(
                    src_ref=theirs, dst_ref=got[wi].at[s], send_sem=send.at[wi, s], recv_sem=recv.at[wi, s],
                    device_id=sib, device_id_type=MESH)
                give.start()
                copies.append(give)
        for cp in copies:
            cp.wait()

    halves = [jax.ShapeDtypeStruct((s.shape[0], s.shape[1] // 2) + s.shape[2:], s.dtype) for s in grads]
    return pl.pallas_call(
        body, out_shape=halves, in_specs=[HBM_SPEC] * n, out_specs=[HBM_SPEC] * n,
        scratch_shapes=[pltpu.SemaphoreType.DMA((n, n_slots)), pltpu.SemaphoreType.DMA((n, n_slots))],
        name="halve_grads_with_sibling",
    )(*grads)


def _swap_with_sibling(parts):
    n = len(parts)

    def body(*refs):
        ins, outs = refs[:n], refs[n:2 * n]
        send, recv = refs[2 * n:]
        sib = (lax.axis_index("x"), lax.axis_index("y"), 1 - lax.axis_index("c"))
        copies = []
        for wi in range(n):
            cp = pltpu.make_async_remote_copy(
                src_ref=ins[wi], dst_ref=outs[wi], send_sem=send.at[wi], recv_sem=recv.at[wi],
                device_id=sib, device_id_type=MESH)
            cp.start()
            copies.append(cp)
        for cp in copies:
            cp.wait()

    return pl.pallas_call(
        body, out_shape=[jax.ShapeDtypeStruct(s.shape, s.dtype) for s in parts],
        in_specs=[HBM_SPEC] * n, out_specs=[HBM_SPEC] * n,
        scratch_shapes=[pltpu.SemaphoreType.DMA((n,)), pltpu.SemaphoreType.DMA((n,))],
        name="swap_grad_halves",
    )(*parts)


def _allreduce_small(vec, loss_row):
    rows = vec.shape[0]

    def body(v_ref, o_ref, buf, send, recv):
        x, y, c = lax.axis_index("x"), lax.axis_index("y"), lax.axis_index("c")
        me = 4 * x + 2 * y + c
        buf[me] = v_ref[...]
        copies = []
        for r in range(1, N_DEV):
            peer = (x ^ ((r >> 2) & 1), y ^ ((r >> 1) & 1), c ^ (r & 1))
            cp = pltpu.make_async_remote_copy(
                src_ref=v_ref, dst_ref=buf.at[me], send_sem=send.at[r - 1], recv_sem=recv.at[r - 1],
                device_id=peer, device_id_type=MESH)
            cp.start()
            copies.append(cp)
        for cp in copies:
            cp.wait()
        total = buf[0]
        for d in range(1, N_DEV):
            total = total + buf[d]
        o_ref[...] = total
        o_ref[loss_row:loss_row + 1, :] = jnp.broadcast_to(
            jnp.sum(total[loss_row:loss_row + 1, :], axis=-1, keepdims=True), (1, HEAD_DIM))

    return pl.pallas_call(
        body, out_shape=jax.ShapeDtypeStruct(vec.shape, F32),
        in_specs=[pl.BlockSpec(memory_space=pltpu.VMEM)], out_specs=pl.BlockSpec(memory_space=pltpu.VMEM),
        scratch_shapes=[pltpu.VMEM((N_DEV, rows, HEAD_DIM), F32),
                        pltpu.SemaphoreType.DMA((N_DEV - 1,)), pltpu.SemaphoreType.DMA((N_DEV - 1,))],
        name="allreduce_gains",
    )(vec)


def _pair_sum(grads, got, *, name):
    slots, rows, width = got.shape
    blk = _pick(rows, (256, 128, 64, 32, 16))
    nbh = rows // blk

    def body(lo_ref, hi_ref, got_ref, o_ref):
        mine = jnp.where(lax.axis_index("c") == 0, lo_ref[...], hi_ref[...])
        o_ref[...] = (mine.astype(F32) + got_ref[...].astype(F32)).astype(BF16)

    spec = pl.BlockSpec((None, blk, width), lambda s, i: (s, i, 0))
    return pl.pallas_call(
        body, out_shape=jax.ShapeDtypeStruct(got.shape, BF16), grid=(slots, nbh),
        in_specs=[spec, pl.BlockSpec((None, blk, width), lambda s, i: (s, nbh + i, 0)), spec],
        out_specs=spec, name=name, compiler_params=_params("parallel", "parallel"),
    )(grads, grads, got)


def _sum_slots(recv, chip_sum, *, name):
    _, rows, width = recv.shape
    blk = _pick(rows, (256, 128, 64, 32, 16, 8))

    def body(r_ref, p_ref, o_ref):
        me = 2 * lax.axis_index("x") + lax.axis_index("y")
        own = jnp.where(me < 2, jnp.where(me == 0, p_ref[0], p_ref[1]), jnp.where(me == 2, p_ref[2], p_ref[3]))
        o_ref[...] = ((own.astype(F32) + r_ref[0].astype(F32)) + r_ref[1].astype(F32)) + r_ref[2].astype(F32)

    return pl.pallas_call(
        body, out_shape=jax.ShapeDtypeStruct((rows, width), F32), grid=(rows // blk,),
        in_specs=[pl.BlockSpec((3, blk, width), lambda i: (0, i, 0)),
                  pl.BlockSpec((N_CHIPS, blk, width), lambda i: (0, i, 0))],
        out_specs=pl.BlockSpec((blk, width), lambda i: (i, 0)),
        name=name, compiler_params=_params("parallel"),
    )(recv, chip_sum)


def _adamw(wgt, grad, m, v, *, name, halves=None):
    rows, width = wgt.shape
    blk = _pick(rows // 2 if halves else rows, (256, 128, 64, 32, 16, 8))
    nbh = rows // 2 // blk

    def body(*refs):
        if halves:
            w_ref, mine_ref, theirs_ref, m_ref, v_ref, g_out, d_out, m_out, v_out = refs
            grad_v = jnp.where(pl.program_id(0) // nbh == lax.axis_index("c"), mine_ref[...], theirs_ref[...])
        else:
            w_ref, g_ref, m_ref, v_ref, g_out, d_out, m_out, v_out = refs
            grad_v = g_ref[...]
        m_new = ADAM_B1 * m_ref[...] + (1.0 - ADAM_B1) * grad_v
        v_new = ADAM_B2 * v_ref[...] + (1.0 - ADAM_B2) * (grad_v * grad_v)
        m_hat = m_new / (1.0 - ADAM_B1 ** ADAM_STEP)
        v_hat = v_new / (1.0 - ADAM_B2 ** ADAM_STEP)
        g_out[...] = grad_v
        d_out[...] = -ADAM_LR * (m_hat / (jnp.sqrt(v_hat) + ADAM_EPS) + ADAM_WD * w_ref[...])
        m_out[...] = m_new
        v_out[...] = v_new

    spec = pl.BlockSpec((blk, width), lambda i: (i, 0))
    half_spec = pl.BlockSpec((blk, width), lambda i: (i % nbh, 0))
    g_specs, g_args = ([half_spec, half_spec], list(halves)) if halves else ([spec], [grad])
    out = jax.ShapeDtypeStruct((rows, width), F32)
    return pl.pallas_call(
        body, out_shape=[out] * 4, grid=(rows // blk,), in_specs=[spec] + g_specs + [spec, spec],
        out_specs=[spec] * 4, name=name, compiler_params=_params("parallel"),
    )(wgt, *g_args, m, v)


_SMALL = (("a_norm", 2048), ("kv_norm", 2048), ("g_ckv", 512), ("g_k_nope", 128), ("g_k_rope", 64),
          ("b_norm", 2048), ("b_g_q_lat", 512), ("b_g_q_nope", 128), ("b_g_q_rope", 64),
          ("mem_norm", 4096), ("g_mem_q", 256), ("g_mem_k", 256))


def _lanes(n):
    return -(-n // HEAD_DIM) * HEAD_DIM


def _pack_rows(pieces, pad_rows_to=8):
    flat = jnp.concatenate(pieces, axis=1)
    rows = flat.shape[1] // HEAD_DIM
    pad = (-rows) % pad_rows_to
    if pad:
        flat = jnp.concatenate([flat, jnp.zeros((1, pad * HEAD_DIM), F32)], axis=1)
    return flat.reshape(rows + pad, HEAD_DIM)


def _pad_lanes(a):
    a = a.reshape(1, -1)
    pad = _lanes(a.shape[1]) - a.shape[1]
    if pad:
        a = jnp.concatenate([a, jnp.zeros((1, pad), F32)], axis=1)
    return a


def kernel(x, mem, positions, a_norm, a_w_in, a_w_out, kv_norm, w_dkv, g_ckv, w_ukv, g_k_nope, g_k_rope, b_norm, b_w_in, b_g_q_lat, b_w_uq, b_g_q_nope, b_g_q_rope, b_w_out, mem_norm, w_mem_kv, g_mem_q, g_mem_k, loss_target, m_a_norm, m_a_w_in, m_a_w_out, m_kv_norm, m_w_dkv, m_g_ckv, m_w_ukv, m_g_k_nope, m_g_k_rope, m_b_norm, m_b_w_in, m_b_g_q_lat, m_b_w_uq, m_b_g_q_nope, m_b_g_q_rope, m_b_w_out, m_mem_norm, m_w_mem_kv, m_g_mem_q, m_g_mem_k, v_a_norm, v_a_w_in, v_a_w_out, v_kv_norm, v_w_dkv, v_g_ckv, v_w_ukv, v_g_k_nope, v_g_k_rope, v_b_norm, v_b_w_in, v_b_g_q_lat, v_b_w_uq, v_b_g_q_nope, v_b_g_q_rope, v_b_w_out, v_mem_norm, v_w_mem_kv, v_g_mem_q, v_g_mem_k):
    chip = 2 * lax.axis_index("x") + lax.axis_index("y")
    rows_dkv = D_MODEL // N_CHIPS
    heads_per_chip = N_MLA_HEADS // N_CHIPS
    qk_w = HEAD_DIM + ROPE_DIM

    big = {"a_in": a_w_in[0], "a_out": a_w_out[0], "dkv": w_dkv, "ukv": w_ukv, "b_in": b_w_in[0],
           "uq": b_w_uq[0], "b_out": b_w_out[0], "mem_kv": w_mem_kv.reshape(2 * rows_dkv, 2 * MEM_W)}
    big_m = {"a_in": m_a_w_in[0], "a_out": m_a_w_out[0], "dkv": m_w_dkv, "ukv": m_w_ukv, "b_in": m_b_w_in[0],
             "uq": m_b_w_uq[0], "b_out": m_b_w_out[0], "mem_kv": m_w_mem_kv.reshape(2 * rows_dkv, 2 * MEM_W)}
    big_v = {"a_in": v_a_w_in[0], "a_out": v_a_w_out[0], "dkv": v_w_dkv, "ukv": v_w_ukv, "b_in": v_b_w_in[0],
             "uq": v_b_w_uq[0], "b_out": v_b_w_out[0], "mem_kv": v_w_mem_kv.reshape(2 * rows_dkv, 2 * MEM_W)}
    names = list(big)
    own_shards = [big[n].astype(BF16) for n in names] + [a_norm]
    gathered = _allgather_chips(own_shards)
    gathered = [lax.dynamic_update_slice(g, s[None], (chip,) + (0,) * s.ndim)
                for g, s in zip(gathered, own_shards)]
    st = dict(zip(names, gathered[:-1]))
    a_in_full = st["a_in"].transpose(1, 0, 2).reshape(D_MODEL, QKV_W + GATE_W)
    uq = st["uq"].reshape(N_CHIPS, Q_LORA, heads_per_chip, qk_w)
    uq = jnp.pad(uq, ((0, 0), (0, 0), (0, 0), (0, CAT_W - qk_w)))
    w = {
        "a_in": a_in_full,
        "a_in_qkv": a_in_full[:, :QKV_W],
        "a_in_gate": a_in_full[:, QKV_W:],
        "a_out": st["a_out"].reshape(D_MODEL, D_MODEL),
        "dkv": jnp.pad(st["dkv"].reshape(D_MODEL, KV_LORA + ROPE_DIM), ((0, 0), (0, HEAD_DIM - ROPE_DIM))),
        "ukv": st["ukv"].transpose(1, 0, 2).reshape(KV_LORA, N_MLA_HEADS * CAT_W),
        "b_in": st["b_in"].transpose(1, 0, 2).reshape(D_MODEL, Q_LORA + GATE_W),
        "uq": uq.transpose(1, 0, 2, 3).reshape(Q_LORA, N_MLA_HEADS * CAT_W),
        "b_out": st["b_out"].reshape(D_MODEL, D_MODEL),
        "mem_kv": st["mem_kv"].reshape(N_CHIPS, 2, rows_dkv, 2 * MEM_W).transpose(1, 0, 2, 3).reshape(
            2, D_MODEL, 2 * MEM_W),
    }
    gains = {
        "a_norm": gathered[-1].reshape(1, D_MODEL), "kv_norm": kv_norm.reshape(1, -1),
        "g_ckv": g_ckv.reshape(1, -1), "g_k_nope": g_k_nope.reshape(1, -1), "g_k_rope": g_k_rope.reshape(1, -1),
        "b_norm": b_norm, "b_g_q_lat": b_g_q_lat, "b_g_q_nope": b_g_q_nope, "b_g_q_rope": b_g_q_rope,
        "mem_norm": mem_norm, "g_mem_q": g_mem_q, "g_mem_k": g_mem_k,
    }

    loss_part, grad_x, gw, gg = _local_step(x[0], mem[0], positions[0], loss_target[0], w, gains)

    stacked = {
        "a_in": gw["a_in"],
        "a_out": gw["a_out"].reshape(N_CHIPS, rows_dkv, D_MODEL),
        "dkv": gw["dkv"][:, :KV_LORA + ROPE_DIM].reshape(N_CHIPS, rows_dkv, KV_LORA + ROPE_DIM),
        "ukv": gw["ukv"],
        "b_in": gw["b_in"],
        "uq": gw["uq"].reshape(N_CHIPS, Q_LORA, heads_per_chip, CAT_W)[..., :qk_w].reshape(
            N_CHIPS, Q_LORA, heads_per_chip * qk_w),
        "b_out": gw["b_out"].reshape(N_CHIPS, rows_dkv, D_MODEL),
        "mem_kv": jnp.stack([gw["mem_kv"][0].reshape(N_CHIPS, rows_dkv, 2 * MEM_W),
                             gw["mem_kv"][1].reshape(N_CHIPS, rows_dkv, 2 * MEM_W)], axis=1).reshape(
            N_CHIPS, 2 * rows_dkv, 2 * MEM_W),
    }
    got = _halve_with_sibling([stacked[n] for n in names])
    chip_sum = [_pair_sum(stacked[n], g, name=f"pair_sum_{n}") for n, g in zip(names, got)]
    received = _scatter_to_chips(chip_sum)
    half_total = [_sum_slots(r, p, name=f"sum_slots_{n}") for n, r, p in zip(names, received, chip_sum)]
    sibling_half = _swap_with_sibling(half_total)
    big_out = {}
    for n, mine, theirs in zip(names, half_total, sibling_half):
        big_out[n] = _adamw(big[n], None, big_m[n], big_v[n], halves=(mine, theirs), name=f"adamw_{n}")

    pieces = [_pad_lanes(gg[n]) if n not in ("g_k_rope", "b_g_q_rope") else gg[n] for n, _ in _SMALL]
    pieces.append(loss_part)
    loss_row = sum(_lanes(size) for _, size in _SMALL) // HEAD_DIM
    summed = _allreduce_small(_pack_rows(pieces), loss_row)
    flat = summed.reshape(1, -1)
    small_g, off = {}, 0
    for n, size in _SMALL:
        small_g[n] = flat[:, off:off + size]
        off += _lanes(size)
    loss = flat[0, off]
    small_g["a_norm"] = lax.dynamic_slice(small_g["a_norm"], (0, chip * rows_dkv), (1, rows_dkv))

    small_w = {"a_norm": a_norm, "kv_norm": kv_norm, "g_ckv": g_ckv, "g_k_nope": g_k_nope, "g_k_rope": g_k_rope,
               "b_norm": b_norm, "b_g_q_lat": b_g_q_lat, "b_g_q_nope": b_g_q_nope, "b_g_q_rope": b_g_q_rope,
               "mem_norm": mem_norm, "g_mem_q": g_mem_q, "g_mem_k": g_mem_k}
    small_m = {"a_norm": m_a_norm, "kv_norm": m_kv_norm, "g_ckv": m_g_ckv, "g_k_nope": m_g_k_nope,
               "g_k_rope": m_g_k_rope, "b_norm": m_b_norm, "b_g_q_lat": m_b_g_q_lat, "b_g_q_nope": m_b_g_q_nope,
               "b_g_q_rope": m_b_g_q_rope, "mem_norm": m_mem_norm, "g_mem_q": m_g_mem_q, "g_mem_k": m_g_mem_k}
    small_v = {"a_norm": v_a_norm, "kv_norm": v_kv_norm, "g_ckv": v_g_ckv, "g_k_nope": v_g_k_nope,
               "g_k_rope": v_g_k_rope, "b_norm": v_b_norm, "b_g_q_lat": v_b_g_q_lat, "b_g_q_nope": v_b_g_q_nope,
               "b_g_q_rope": v_b_g_q_rope, "mem_norm": v_mem_norm, "g_mem_q": v_g_mem_q, "g_mem_k": v_g_mem_k}
    snames = [n for n, _ in _SMALL]
    packs = [_pack_rows([_pad_lanes(src[n]) for n in snames])
             for src in (small_w, small_g, small_m, small_v)]
    small_res = _adamw(packs[0], packs[1], packs[2], packs[3], name="adamw_gains")
    small_out = {n: [] for n in snames}
    for res in small_res:
        flat_r = res.reshape(1, -1)
        off = 0
        for n in snames:
            size = small_w[n].size
            small_out[n].append(flat_r[:, off:off + size].reshape(small_w[n].shape))
            off += _lanes(size)

    big_names = {"a_w_in": ("a_in", a_w_in), "a_w_out": ("a_out", a_w_out), "w_dkv": ("dkv", w_dkv),
                 "w_ukv": ("ukv", w_ukv), "b_w_in": ("b_in", b_w_in), "b_w_uq": ("uq", b_w_uq),
                 "b_w_out": ("b_out", b_w_out), "w_mem_kv": ("mem_kv", w_mem_kv)}
    order = ["a_norm", "a_w_in", "a_w_out", "kv_norm", "w_dkv", "g_ckv", "w_ukv", "g_k_nope", "g_k_rope",
             "b_norm", "b_w_in", "b_g_q_lat", "b_w_uq", "b_g_q_nope", "b_g_q_rope", "b_w_out", "mem_norm",
             "w_mem_kv", "g_mem_q", "g_mem_k"]
    groups = [[], [], [], []]
    for n in order:
        if n in big_names:
            key, ref_arr = big_names[n]
            for t in range(4):
                groups[t].append(big_out[key][t].reshape(ref_arr.shape))
        else:
            for t in range(4):
                groups[t].append(small_out[n][t])
    return (loss, grad_x[None], *groups[0], *groups[1], *groups[2], *groups[3])
```

```python
import jax
import jax.numpy as jnp
from jax import lax
from jax.experimental import pallas as pl
from jax.experimental.pallas import tpu as pltpu

F32 = jnp.float32
BF16 = jnp.bfloat16
MESH = pl.DeviceIdType.MESH

D_MODEL = 2048
HEAD_DIM = 128
N_SB_HEADS = 12
N_MEM_HEADS = 4
N_MLA_HEADS = 12
MEM_LEN = 256
Q_LORA = 512
KV_LORA = 512
ROPE_DIM = 64
SB_W = N_SB_HEADS * HEAD_DIM
MEM_W = N_MEM_HEADS * HEAD_DIM
MLA_W = N_MLA_HEADS * HEAD_DIM
QKV_W = 3 * SB_W
GATE_W = SB_W + 2 * MEM_W
CAT_W = 2 * HEAD_DIM
ROPE_THETA = 10000.0
EPS = 1e-6
N_CHIPS = 4
N_DEV = 8

ADAM_LR = 0.001
ADAM_B1 = 0.9
ADAM_B2 = 0.999
ADAM_EPS = 1e-08
ADAM_WD = 0.01
ADAM_STEP = 10

VMEM_LIMIT_BYTES = 56 * 1024 * 1024
MM_OPERAND_VMEM_BYTES = 24 * 1024 * 1024
ROW_BLOCK = 256
ATT_BLOCK = 256


def _params(*sem):
    return pltpu.CompilerParams(dimension_semantics=sem, vmem_limit_bytes=VMEM_LIMIT_BYTES)


def _pick(n, cands):
    for c in cands:
        if n % c == 0:
            return c
    return n


def _mm(a, b, *, name, ta=False, tb=False, out_dtype=F32, res=None, n_split=1, scale_cols=None):
    if ta:
        k_dim, m_dim = a.shape
    else:
        m_dim, k_dim = a.shape
    if tb:
        n_dim, kb = b.shape
    else:
        kb, n_dim = b.shape
    assert kb == k_dim, (a.shape, b.shape)
    n_per = n_dim // n_split
    bm = m_dim if m_dim <= 1024 else _pick(m_dim, (1024, 512, 256))
    bn = n_per if n_per <= 1024 else _pick(n_per, (1024, 896, 768, 640, 512, 256, 128))
    per_k = (bm * a.dtype.itemsize + bn * b.dtype.itemsize) * 2
    bk = next((k_dim // d for d in range(1, k_dim // 128 + 1)
               if k_dim % d == 0 and (k_dim // d) % 128 == 0 and (k_dim // d) * per_k <= MM_OPERAND_VMEM_BYTES),
              k_dim)
    nk = k_dim // bk
    nb_per = n_per // bn
    grid = (m_dim // bm, n_dim // bn, nk)
    a_spec = (pl.BlockSpec((bk, bm), lambda i, j, k: (k, i)) if ta
              else pl.BlockSpec((bm, bk), lambda i, j, k: (i, k)))
    b_spec = (pl.BlockSpec((bn, bk), lambda i, j, k: (j, k)) if tb
              else pl.BlockSpec((bk, bn), lambda i, j, k: (k, j)))
    dims = (((0 if ta else 1,), (1 if tb else 0,)), ((), ()))
    in_specs = [a_spec, b_spec]
    args = [a, b]
    if res is not None:
        in_specs.append(pl.BlockSpec((bm, bn), lambda i, j, k: (i, j)))
        args.append(res)
    if n_split == 1:
        out_shape = jax.ShapeDtypeStruct((m_dim, n_dim), out_dtype)
        out_spec = pl.BlockSpec((bm, bn), lambda i, j, k: (i, j))
    else:
        out_shape = jax.ShapeDtypeStruct((n_split, m_dim, n_per), out_dtype)
        out_spec = pl.BlockSpec((None, bm, bn), lambda i, j, k: (j // nb_per, i, j % nb_per))

    def body(*refs):
        if res is None:
            a_ref, b_ref, o_ref, acc = refs
            r_ref = None
        else:
            a_ref, b_ref, r_ref, o_ref, acc = refs
        k = pl.program_id(2)
        col_block = pl.program_id(1)

        @pl.when(k == 0)
        def _():
            acc[...] = jnp.zeros_like(acc)

        acc[...] += lax.dot_general(a_ref[...].astype(BF16), b_ref[...].astype(BF16), dims,
                                    preferred_element_type=F32)

        @pl.when(k == nk - 1)
        def _():
            r = acc[...]
            if r_ref is not None:
                r = r + r_ref[...]
            if scale_cols is not None:
                assert scale_cols[0] % bn == 0
                r = r * jnp.where(col_block < scale_cols[0] // bn, scale_cols[1], 1.0)
            o_ref[...] = r.astype(out_dtype)

    return pl.pallas_call(
        body, out_shape=out_shape, grid=grid, in_specs=in_specs, out_specs=out_spec,
        scratch_shapes=[pltpu.VMEM((bm, bn), F32)], name=name,
        compiler_params=_params("parallel", "parallel", "arbitrary"),
    )(*args)


def _rowwise(body, n_rows, ins, outs, accs=(), *, name, block=ROW_BLOCK):
    blk = min(block, n_rows)
    assert n_rows % blk == 0
    in_specs = []
    for arr, is_row in ins:
        if is_row:
            assert arr.shape[0] == n_rows, (name, arr.shape, n_rows)
            width = arr.shape[1] if is_row is True else is_row
            in_specs.append(pl.BlockSpec((blk, width), lambda i: (i, 0)))
        else:
            in_specs.append(pl.BlockSpec(arr.shape, lambda i, nd=arr.ndim: (0,) * nd))
    out_shape = [jax.ShapeDtypeStruct((n_rows, w), dt) for w, dt in outs]
    out_specs = [pl.BlockSpec((blk, w), lambda i: (i, 0)) for w, _ in outs]
    out_shape += [jax.ShapeDtypeStruct(s, dt) for s, dt in accs]
    out_specs += [pl.BlockSpec(s, lambda i, nd=len(s): (0,) * nd) for s, _ in accs]
    n_in, n_out, n_acc = len(ins), len(outs), len(accs)

    def kern(*refs):
        in_refs = refs[:n_in]
        out_refs = refs[n_in:n_in + n_out]
        acc_refs = refs[n_in + n_out:]
        if n_acc:
            @pl.when(pl.program_id(0) == 0)
            def _():
                for r in acc_refs:
                    r[...] = jnp.zeros_like(r)
        body(in_refs, out_refs, acc_refs)

    return pl.pallas_call(
        kern, out_shape=out_shape, grid=(n_rows // blk,), in_specs=in_specs, out_specs=out_specs,
        name=name, compiler_params=_params("arbitrary"),
    )(*[arr for arr, _ in ins])


def _rms(x, g, n=None):
    n = x.shape[-1] if n is None else n
    r = lax.rsqrt(jnp.sum(x * x, axis=-1, keepdims=True) / n + EPS)
    return x * r * g


def _rms_bwd(x, g, dy, n=None):
    n = x.shape[-1] if n is None else n
    r = lax.rsqrt(jnp.sum(x * x, axis=-1, keepdims=True) / n + EPS)
    gdy = dy * g
    dx = r * (gdy - x * ((r * r) * (jnp.sum(gdy * x, axis=-1, keepdims=True) / n)))
    dg = jnp.sum(dy * x * r, axis=0, keepdims=True)
    return dx, dg


def _swap_halves(x):
    lane = lax.broadcasted_iota(jnp.int32, x.shape, 1)
    return jnp.where(lane < ROPE_DIM // 2, pltpu.roll(x, 128 - ROPE_DIM // 2, 1),
                     pltpu.roll(x, ROPE_DIM // 2, 1))


def _rope(n, cos_t, sin_t):
    return n * cos_t + _swap_halves(n) * sin_t


def _rope_bwd(dy, cos_t, sin_t):
    return dy * cos_t - _swap_halves(dy) * sin_t


def _sigmoid(g):
    return 1.0 / (1.0 + jnp.exp(-g))


def _dot_t(a, b):
    return lax.dot_general(a, b, (((1,), (1,)), ((), ())), preferred_element_type=F32)


def _tdot(a, b):
    return lax.dot_general(a, b, (((0,), (0,)), ((), ())), preferred_element_type=F32)


def _dot(a, b):
    return jnp.dot(a, b, preferred_element_type=F32)


def _hs(h, w=HEAD_DIM, base=0):
    return slice(base + h * w, base + (h + 1) * w)


def _mem_head(qm, gq, mk_h, mv_h):
    qb = _rms(qm, gq).astype(BF16)
    s = _dot_t(qb, mk_h) * (HEAD_DIM ** -0.5)
    e = jnp.exp(s - jnp.max(s, axis=-1, keepdims=True))
    p = e / jnp.sum(e, axis=-1, keepdims=True)
    mo = _dot(p.astype(BF16), mv_h)
    return qb, p, mo


def _mix_fwd(att, gates, c0, mk, mv, gq, *, name):
    n_rows = att.shape[0]

    def body(ins, outs, _):
        att_ref, g_ref, mk_ref, mv_ref, gq_ref = ins
        (o_ref,) = outs
        g = g_ref[:, c0:c0 + SB_W]
        o_ref[:, :SB_W] = (att_ref[...] * (g * _sigmoid(g))).astype(BF16)
        for h in range(N_MEM_HEADS):
            qm = g_ref[:, _hs(h, base=c0 + SB_W)]
            gm = g_ref[:, _hs(h, base=c0 + SB_W + MEM_W)]
            _, _, mo = _mem_head(qm, gq_ref[...], mk_ref[:, _hs(h)], mv_ref[:, _hs(h)])
            o_ref[:, _hs(h, base=SB_W)] = (mo * (gm * _sigmoid(gm))).astype(BF16)

    (mixed,) = _rowwise(body, n_rows,
                        [(att, True), (gates, True), (mk, False), (mv, False), (gq, False)],
                        [(D_MODEL, BF16)], name=name)
    return mixed


def _mix_bwd(dmixed, att, gates, c0, mk, mv, gq, *, name):
    n_rows = att.shape[0]
    scale = HEAD_DIM ** -0.5

    def body(ins, outs, accs):
        dm_ref, att_ref, g_ref, mk_ref, mv_ref, gq_ref = ins
        datt_ref, dg_ref = outs
        dmk_ref, dmv_ref, dgq_ref = accs
        g = g_ref[:, c0:c0 + SB_W]
        sg = _sigmoid(g)
        dm = dm_ref[:, :SB_W]
        datt_ref[...] = dm * (g * sg)
        dg_ref[:, :SB_W] = (dm * att_ref[...] * (sg * (1.0 + g * (1.0 - sg)))).astype(BF16)
        for h in range(N_MEM_HEADS):
            qm = g_ref[:, _hs(h, base=c0 + SB_W)]
            gm = g_ref[:, _hs(h, base=c0 + SB_W + MEM_W)]
            mk_h = mk_ref[:, _hs(h)]
            mv_h = mv_ref[:, _hs(h)]
            qb, p, mo = _mem_head(qm, gq_ref[...], mk_h, mv_h)
            sgm = _sigmoid(gm)
            dmh = dm_ref[:, _hs(h, base=SB_W)]
            dmo = dmh * (gm * sgm)
            dg_ref[:, _hs(h, base=SB_W + MEM_W)] = (
                dmh * mo * (sgm * (1.0 + gm * (1.0 - sgm)))).astype(BF16)
            dmo_b = dmo.astype(BF16)
            pb = p.astype(BF16)
            dp = _dot_t(dmo_b, mv_h)
            dmv_ref[:, _hs(h)] += _tdot(pb, dmo_b)
            ds = (p * (dp - jnp.sum(dp * p, axis=-1, keepdims=True)) * scale).astype(BF16)
            dqn = _dot(ds, mk_h)
            dmk_ref[:, _hs(h)] += _tdot(ds, qb)
            dqm, dgq = _rms_bwd(qm, gq_ref[...], dqn)
            dg_ref[:, _hs(h, base=SB_W)] = dqm.astype(BF16)
            dgq_ref[...] += dgq

    return _rowwise(body, n_rows,
                    [(dmixed, True), (att, True), (gates, True), (mk, False), (mv, False), (gq, False)],
                    [(SB_W, F32), (GATE_W, BF16)],
                    [((MEM_LEN, MEM_W), F32), ((MEM_LEN, MEM_W), F32), ((1, HEAD_DIM), F32)],
                    name=name)


def _mem_side_fwd(mem, g_norm, w_kv, g_k, *, tag):
    def norm_body(ins, outs, _):
        outs[0][...] = _rms(ins[0][...], ins[1][...]).astype(BF16)

    (mn,) = _rowwise(norm_body, MEM_LEN, [(mem, True), (g_norm, False)], [(D_MODEL, BF16)],
                     name=f"mem_norm_{tag}")
    mkv = _mm(mn, w_kv, name=f"mem_kv_{tag}")

    def kv_body(ins, outs, _):
        mkv_ref, gk_ref = ins
        mk_ref, mv_ref = outs
        for h in range(N_MEM_HEADS):
            mk_ref[:, _hs(h)] = _rms(mkv_ref[:, _hs(h)], gk_ref[...]).astype(BF16)
        mv_ref[...] = mkv_ref[:, MEM_W:].astype(BF16)

    mk, mv = _rowwise(kv_body, MEM_LEN, [(mkv, True), (g_k, False)], [(MEM_W, BF16), (MEM_W, BF16)],
                      name=f"mem_kv_prep_{tag}")
    return mn, mkv, mk, mv


def _mem_side_bwd(mem, g_norm, w_kv, g_k, mn, mkv, dmk, dmv, *, tag):
    def kv_body(ins, outs, accs):
        mkv_ref, gk_ref, dmk_ref, dmv_ref = ins
        (d_ref,) = outs
        (dgk_ref,) = accs
        for h in range(N_MEM_HEADS):
            dx, dg = _rms_bwd(mkv_ref[:, _hs(h)], gk_ref[...], dmk_ref[:, _hs(h)])
            d_ref[:, _hs(h)] = dx.astype(BF16)
            dgk_ref[...] += dg
        d_ref[:, MEM_W:] = dmv_ref[...].astype(BF16)

    dmkv, dgk = _rowwise(kv_body, MEM_LEN, [(mkv, True), (g_k, False), (dmk, True), (dmv, True)],
                         [(2 * MEM_W, BF16)], [((1, HEAD_DIM), F32)], name=f"mem_kv_prep_bwd_{tag}")
    dmn = _mm(dmkv, w_kv, tb=True, name=f"mem_kv_dx_{tag}")
    dw = _mm(mn, dmkv, ta=True, out_dtype=BF16, name=f"mem_kv_dw_{tag}")

    def norm_body(ins, outs, accs):
        _, dg = _rms_bwd(ins[0][...], ins[1][...], ins[2][...])
        accs[0][...] += dg

    (dgn,) = _rowwise(norm_body, MEM_LEN, [(mem, True), (g_norm, False), (dmn, True)], [],
                      [((1, D_MODEL), F32)], name=f"mem_norm_bwd_{tag}")
    return dw, dgn, dgk


LOG2_E = 1.4426950408889634
SB_Q_SCALE = HEAD_DIM ** -0.5 * LOG2_E


Z2_CAP = 126.0


def _sb_terms(z2):
    zc = jnp.minimum(z2, Z2_CAP)
    w = 1.0 + jnp.exp2(zc)
    return zc, w, jnp.log2(w)


LOOP_UNROLL = 8


def _loop_blocks(base, qb, body, carry, *, reverse):
    def run(start, trips, unroll, c0):
        def trip(t, c):
            for u in range(unroll):
                p = start + t * unroll + u
                c = body(base - 1 - p if reverse else p, c)
            return c
        return lax.fori_loop(0, trips, trip, c0)

    if qb % LOOP_UNROLL == 0:
        return run(0, base // LOOP_UNROLL, LOOP_UNROLL, carry)
    small = qb
    n_big = base // LOOP_UNROLL
    carry = run(0, n_big, LOOP_UNROLL, carry)
    return run(n_big * LOOP_UNROLL, (base - n_big * LOOP_UNROLL) // small, small, carry)


def _chain_modes(s, qb):
    return tuple(None if t < s else ("m" if t == s else "f") for t in range(qb))


def _split_dot(x, tri2):
    hi = x.astype(BF16)
    lo = (x - hi.astype(F32)).astype(BF16)
    return _dot(jnp.concatenate([hi, lo], axis=1), tri2)


def _sb_fwd(qkv, *, name, hp=1):
    seq = qkv.shape[0]
    blk = min(ATT_BLOCK, seq)
    nkb = seq // blk
    qb = _pick(nkb, (4, 2, 1))
    rows = qb * blk
    chains = [(t, s) for t in range(hp) for s in range(qb)]

    def body(q_ref, k_ref, v_ref, o_ref):
        base = pl.program_id(1) * qb
        qs = {(t, s): q_ref[s * blk:(s + 1) * blk, _hs(t)] for t, s in chains}
        row = lax.broadcasted_iota(jnp.int32, (blk, blk), 0)
        col = lax.broadcasted_iota(jnp.int32, (blk, blk), 1)
        after = (row > col).astype(BF16)
        after2 = jnp.concatenate([after, after], axis=0)
        causal = col < row

        def step(j, carry, modes):
            off = pl.multiple_of(j * blk, blk)
            act = [c for c in chains if modes[c[1]]]
            zs, ls = {}, {}
            for c in act:
                zs[c], _, l = _sb_terms(_dot_t(qs[c], k_ref[pl.ds(off, blk), _hs(c[0])]))
                ls[c] = jnp.where(causal, l, 0.0) if modes[c[1]] == "m" else l
            cs = {c: _split_dot(ls[c], after2) for c in act}
            carry = dict(carry)
            for c in act:
                run, acc = carry[c]
                a = jnp.exp2(zs[c] - ls[c] - cs[c] - run)
                if modes[c[1]] == "m":
                    a = jnp.where(causal, a, 0.0)
                acc = acc + _dot(a.astype(BF16), v_ref[pl.ds(off, blk), _hs(c[0])])
                carry[c] = (run + jnp.sum(ls[c], axis=-1, keepdims=True), acc)
            return carry

        init = (jnp.zeros((blk, 1), F32), jnp.zeros((blk, HEAD_DIM), F32))
        carry = {c: init for c in chains}
        for s in reversed(range(qb)):
            carry = step(base + s, carry, _chain_modes(s, qb))
        carry = _loop_blocks(base, qb, lambda j, c: step(j, c, ("f",) * qb), carry, reverse=True)
        for t, s in chains:
            o_ref[s * blk:(s + 1) * blk, _hs(t)] = carry[(t, s)][1]

    nh = N_SB_HEADS // hp
    return pl.pallas_call(
        body, out_shape=jax.ShapeDtypeStruct((seq, SB_W), F32), grid=(nh, nkb // qb),
        in_specs=[pl.BlockSpec((rows, hp * HEAD_DIM), lambda h, i: (i, h)),
                  pl.BlockSpec((seq, hp * HEAD_DIM), lambda h, i: (0, nh + h)),
                  pl.BlockSpec((seq, hp * HEAD_DIM), lambda h, i: (0, 2 * nh + h))],
        out_specs=pl.BlockSpec((rows, hp * HEAD_DIM), lambda h, i: (i, h)),
        name=name, compiler_params=_params("parallel", "arbitrary"),
    )(qkv, qkv, qkv)


SB_BWD_GROUP = 4


def _sb_bwd(qkv, out, dout, *, name):
    seq = qkv.shape[0]
    blk = min(ATT_BLOCK, seq)
    nkb = seq // blk
    qb = _pick(nkb, (4, 2, 1))
    rows = qb * blk
    scale = HEAD_DIM ** -0.5

    def body(q_ref, k_ref, v_ref, do_ref, o_ref, dq_ref, dk_out, dv_out, dk_ref, dv_ref):
        g = pl.program_id(1)
        base = g * qb

        @pl.when(g == 0)
        def _():
            dk_ref[...] = jnp.zeros_like(dk_ref)
            dv_ref[...] = jnp.zeros_like(dv_ref)

        qs = [q_ref[t * blk:(t + 1) * blk, :] for t in range(qb)]
        dos = [do_ref[t * blk:(t + 1) * blk, :].astype(BF16) for t in range(qb)]
        totals = [jnp.sum(dos[t].astype(F32) * o_ref[t * blk:(t + 1) * blk, :], axis=-1, keepdims=True)
                  for t in range(qb)]
        row = lax.broadcasted_iota(jnp.int32, (blk, blk), 0)
        col = lax.broadcasted_iota(jnp.int32, (blk, blk), 1)
        after = (row > col).astype(BF16)
        after2 = jnp.concatenate([after, after], axis=0)
        from_s = (row >= col).astype(BF16)
        from_s2 = jnp.concatenate([from_s, from_s], axis=0)
        causal = col < row

        def step(j, carry, modes):
            runs, rights, dqs = list(carry[0]), list(carry[1]), list(carry[2])
            off = pl.multiple_of(j * blk, blk)
            kb = k_ref[pl.ds(off, blk), :]
            vb = v_ref[pl.ds(off, blk), :]
            dv_inc = dk_inc = None
            for first in range(0, qb, SB_BWD_GROUP):
                act = [t for t in range(first, min(first + SB_BWD_GROUP, qb)) if modes[t]]
                zs, ls, sns = {}, {}, {}
                for t in act:
                    zs[t], w, l = _sb_terms(_dot_t(qs[t], kb))
                    sns[t] = pl.reciprocal(w, approx=True)
                    ls[t] = jnp.where(causal, l, 0.0) if modes[t] == "m" else l
                cs = {t: _split_dot(ls[t], after2) for t in act}
                das = {t: _dot_t(dos[t], vb) for t in act}
                abs_, des = {}, {}
                for t in act:
                    a = jnp.exp2(zs[t] - ls[t] - cs[t] - runs[t])
                    if modes[t] == "m":
                        a = jnp.where(causal, a, 0.0)
                    abs_[t] = a.astype(BF16)
                    des[t] = abs_[t].astype(F32) * das[t]
                sufs = {t: _split_dot(des[t], from_s2) for t in act}
                for t in act:
                    left = totals[t] - (sufs[t] + rights[t])
                    dz = (des[t] + left) * sns[t] - left
                    if modes[t] == "m":
                        dz = jnp.where(causal, dz, 0.0)
                    dzb = dz.astype(BF16)
                    dqs[t] = dqs[t] + _dot(dzb, kb)
                    inc_v = _tdot(abs_[t], dos[t])
                    inc_k = _tdot(dzb, qs[t])
                    dv_inc = inc_v if dv_inc is None else dv_inc + inc_v
                    dk_inc = inc_k if dk_inc is None else dk_inc + inc_k
                    runs[t] = runs[t] + jnp.sum(ls[t], axis=-1, keepdims=True)
                    rights[t] = rights[t] + jnp.sum(des[t], axis=-1, keepdims=True)
            dv_ref[pl.ds(off, blk), :] += dv_inc
            dk_ref[pl.ds(off, blk), :] += dk_inc
            return tuple(runs), tuple(rights), tuple(dqs)

        zero = (jnp.zeros((blk, 1), F32),) * qb
        carry = (zero, zero, (jnp.zeros((blk, HEAD_DIM), F32),) * qb)
        for s in reversed(range(qb)):
            carry = step(base + s, carry, _chain_modes(s, qb))
        carry = _loop_blocks(base, qb, lambda j, c: step(j, c, ("f",) * qb), carry, reverse=True)
        for t in range(qb):
            dq_ref[t * blk:(t + 1) * blk, :] = (carry[2][t] * scale).astype(BF16)

        @pl.when(g == pl.num_programs(1) - 1)
        def _():
            dk_out[...] = (dk_ref[...] * (1.0 / LOG2_E)).astype(BF16)
            dv_out[...] = dv_ref[...].astype(BF16)

    out_sd = jax.ShapeDtypeStruct((seq, SB_W), BF16)
    return pl.pallas_call(
        body, out_shape=[out_sd, out_sd, out_sd], grid=(N_SB_HEADS, nkb // qb),
        in_specs=[pl.BlockSpec((rows, HEAD_DIM), lambda h, i: (i, h)),
                  pl.BlockSpec((seq, HEAD_DIM), lambda h, i: (0, N_SB_HEADS + h)),
                  pl.BlockSpec((seq, HEAD_DIM), lambda h, i: (0, 2 * N_SB_HEADS + h)),
                  pl.BlockSpec((rows, HEAD_DIM), lambda h, i: (i, h)),
                  pl.BlockSpec((rows, HEAD_DIM), lambda h, i: (i, h))],
        out_specs=[pl.BlockSpec((rows, HEAD_DIM), lambda h, i: (i, h)),
                   pl.BlockSpec((seq, HEAD_DIM), lambda h, i: (0, h)),
                   pl.BlockSpec((seq, HEAD_DIM), lambda h, i: (0, h))],
        scratch_shapes=[pltpu.VMEM((seq, HEAD_DIM), F32), pltpu.VMEM((seq, HEAD_DIM), F32)],
        name=name, compiler_params=_params("parallel", "arbitrary"),
    )(qkv, qkv, qkv, dout, out)


MLA_SCALE = (HEAD_DIM + ROPE_DIM) ** -0.5
MLA_Q_SCALE = MLA_SCALE * LOG2_E


def _mla_fwd(q_cat, k_cat, v, *, name, hp=1):
    seq = q_cat.shape[0]
    blk = min(ATT_BLOCK, seq)
    nkb = seq // blk
    qb = _pick(nkb, (4, 2, 1))
    rows = qb * blk
    chains = [(t, s) for t in range(hp) for s in range(qb)]

    def body(q_ref, k_ref, v_ref, o_ref, lse_ref):
        base = pl.program_id(1) * qb
        qs = {(t, s): q_ref[s * blk:(s + 1) * blk, t * CAT_W:(t + 1) * CAT_W] for t, s in chains}
        row = lax.broadcasted_iota(jnp.int32, (blk, blk), 0)
        col = lax.broadcasted_iota(jnp.int32, (blk, blk), 1)
        causal = col <= row

        def step(j, carry, modes):
            off = pl.multiple_of(j * blk, blk)
            act = [c for c in chains if modes[c[1]]]
            ss = {c: _dot_t(qs[c], k_ref[pl.ds(off, blk), c[0] * CAT_W:(c[0] + 1) * CAT_W]) for c in act}
            carry = dict(carry)
            for c in act:
                m, l, acc = carry[c]
                s = ss[c]
                if modes[c[1]] == "m":
                    s = jnp.where(causal, s, -jnp.inf)
                m_new = jnp.maximum(m, jnp.max(s, axis=-1, keepdims=True))
                p = jnp.exp2(s - m_new)
                alpha = jnp.exp2(m - m_new)
                l = alpha * l + jnp.sum(p, axis=-1, keepdims=True)
                acc = alpha * acc + _dot(p.astype(BF16), v_ref[pl.ds(off, blk), _hs(c[0])])
                carry[c] = (m_new, l, acc)
            return carry

        init = (jnp.full((blk, 1), -jnp.inf, F32), jnp.zeros((blk, 1), F32),
                jnp.zeros((blk, HEAD_DIM), F32))
        carry = {c: init for c in chains}
        carry = _loop_blocks(base, qb, lambda j, c: step(j, c, ("f",) * qb), carry, reverse=False)
        for s in range(qb):
            carry = step(base + s, carry, _chain_modes(s, qb))
        for t, s in chains:
            m, l, acc = carry[(t, s)]
            o_ref[s * blk:(s + 1) * blk, _hs(t)] = acc / l
            lse_ref[s * blk:(s + 1) * blk, _hs(t)] = jnp.broadcast_to(
                (m + jnp.log2(l)) * (1.0 / LOG2_E), (blk, HEAD_DIM))

    out = jax.ShapeDtypeStruct((seq, MLA_W), F32)
    return pl.pallas_call(
        body, out_shape=[out, out], grid=(N_MLA_HEADS // hp, nkb // qb),
        in_specs=[pl.BlockSpec((rows, hp * CAT_W), lambda h, i: (i, h)),
                  pl.BlockSpec((seq, hp * CAT_W), lambda h, i: (0, h)),
                  pl.BlockSpec((seq, hp * HEAD_DIM), lambda h, i: (0, h))],
        out_specs=[pl.BlockSpec((rows, hp * HEAD_DIM), lambda h, i: (i, h)),
                   pl.BlockSpec((rows, hp * HEAD_DIM), lambda h, i: (i, h))],
        name=name, compiler_params=_params("parallel", "arbitrary"),
    )(q_cat, k_cat, v)


def _mla_bwd(q_cat, k_cat, v, out, lse, dout, *, name):
    seq = q_cat.shape[0]
    blk = min(ATT_BLOCK, seq)
    nkb = seq // blk
    qb = _pick(nkb, (4, 2, 1))
    rows = qb * blk

    def body(q_ref, k_ref, v_ref, o_ref, lse_ref, do_ref, dq_ref, dk_ref, dv_ref):
        g = pl.program_id(1)
        base = g * qb

        @pl.when(g == 0)
        def _():
            dk_ref[...] = jnp.zeros_like(dk_ref)
            dv_ref[...] = jnp.zeros_like(dv_ref)

        qs, dobs, deltas, lses = [], [], [], []
        for t in range(qb):
            rs = slice(t * blk, (t + 1) * blk)
            do = do_ref[rs, :]
            qs.append(q_ref[rs, :])
            dobs.append(do.astype(BF16))
            deltas.append(jnp.sum(do * o_ref[rs, :], axis=-1, keepdims=True))
            lses.append(lse_ref[rs, :1] * LOG2_E)
        row = lax.broadcasted_iota(jnp.int32, (blk, blk), 0)
        col = lax.broadcasted_iota(jnp.int32, (blk, blk), 1)
        causal = col <= row

        def step(j, dqs, modes):
            off = pl.multiple_of(j * blk, blk)
            kb = k_ref[pl.ds(off, blk), :]
            vb = v_ref[pl.ds(off, blk), :]
            act = [t for t in range(qb) if modes[t]]
            ss = {t: _dot_t(qs[t], kb) for t in act}
            dps = {t: _dot_t(dobs[t], vb) for t in act}
            dqs = list(dqs)
            dv_inc = dk_inc = None
            for t in act:
                p = jnp.exp2(ss[t] - lses[t])
                if modes[t] == "m":
                    p = jnp.where(causal, p, 0.0)
                ds = (p * (dps[t] - deltas[t])).astype(BF16)
                inc_v = _tdot(p.astype(BF16), dobs[t])
                inc_k = _tdot(ds, qs[t])
                dv_inc = inc_v if dv_inc is None else dv_inc + inc_v
                dk_inc = inc_k if dk_inc is None else dk_inc + inc_k
                dqs[t] = dqs[t] + _dot(ds, kb)
            dv_ref[pl.ds(off, blk), :] += dv_inc
            dk_ref[pl.ds(off, blk), :] += dk_inc
            return tuple(dqs)

        dqs = (jnp.zeros((blk, CAT_W), F32),) * qb
        dqs = _loop_blocks(base, qb, lambda j, c: step(j, c, ("f",) * qb), dqs, reverse=False)
        for s in range(qb):
            modes = tuple(None if t < s else ("m" if t == s else "f") for t in range(qb))
            dqs = step(base + s, dqs, modes)
        for t in range(qb):
            dq_ref[t * blk:(t + 1) * blk, :] = dqs[t] * MLA_SCALE

        @pl.when(g == pl.num_programs(1) - 1)
        def _():
            dk_ref[...] = dk_ref[...] * (1.0 / LOG2_E)

    return pl.pallas_call(
        body,
        out_shape=[jax.ShapeDtypeStruct((seq, N_MLA_HEADS * CAT_W), F32),
                   jax.ShapeDtypeStruct((seq, N_MLA_HEADS * CAT_W), F32),
                   jax.ShapeDtypeStruct((seq, MLA_W), F32)],
        grid=(N_MLA_HEADS, nkb // qb),
        in_specs=[pl.BlockSpec((rows, CAT_W), lambda h, i: (i, h)),
                  pl.BlockSpec((seq, CAT_W), lambda h, i: (0, h)),
                  pl.BlockSpec((seq, HEAD_DIM), lambda h, i: (0, h)),
                  pl.BlockSpec((rows, HEAD_DIM), lambda h, i: (i, h)),
                  pl.BlockSpec((rows, HEAD_DIM), lambda h, i: (i, h)),
                  pl.BlockSpec((rows, HEAD_DIM), lambda h, i: (i, h))],
        out_specs=[pl.BlockSpec((rows, CAT_W), lambda h, i: (i, h)),
                   pl.BlockSpec((seq, CAT_W), lambda h, i: (0, h)),
                   pl.BlockSpec((seq, HEAD_DIM), lambda h, i: (0, h))],
        name=name, compiler_params=_params("parallel", "arbitrary"),
    )(q_cat, k_cat, v, out, lse, dout)


def _local_step(x, mem, positions, target, w, g):
    seq = x.shape[0]
    inv_freq = jnp.power(ROPE_THETA, -jnp.arange(0, ROPE_DIM, 2, dtype=F32) / ROPE_DIM)
    ang = positions.astype(F32)[:, None] * inv_freq
    cos, sin = jnp.cos(ang), jnp.sin(ang)
    lane_pad = jnp.zeros((seq, HEAD_DIM - ROPE_DIM), F32)
    cos_t = jnp.concatenate([cos, cos, lane_pad], axis=1)
    sin_t = jnp.concatenate([-sin, sin, lane_pad], axis=1)
    gain_pad = jnp.zeros((1, HEAD_DIM - ROPE_DIM), F32)
    g_k_rope = jnp.concatenate([g["g_k_rope"], gain_pad], axis=1)
    g_q_rope = jnp.concatenate([g["b_g_q_rope"], gain_pad], axis=1)

    def norm_to_bf16(src, gain, name):
        def body(ins, outs, _):
            outs[0][...] = _rms(ins[0][...], ins[1][...]).astype(BF16)
        return _rowwise(body, seq, [(src, True), (gain, False)], [(src.shape[1], BF16)], name=name)[0]

    h_a = norm_to_bf16(x, g["a_norm"], "a_norm_fwd")
    qkv = _mm(h_a, w["a_in_qkv"], out_dtype=BF16, scale_cols=(SB_W, SB_Q_SCALE), name="a_in_qkv")
    gr = _mm(h_a, w["a_in_gate"], name="a_in_gate")
    sb = _sb_fwd(qkv, name="sb_fwd")
    mem0 = _mem_side_fwd(mem, g["mem_norm"][0:1], w["mem_kv"][0], g["g_mem_k"][0:1], tag="a")
    mixed_a = _mix_fwd(sb, gr, 0, mem0[2], mem0[3], g["g_mem_q"][0:1], name="a_mix_fwd")
    x1 = _mm(mixed_a, w["a_out"], res=x, name="a_out")

    def norms2_body(ins, outs, _):
        xv = ins[0][...]
        outs[0][...] = _rms(xv, ins[1][...]).astype(BF16)
        outs[1][...] = _rms(xv, ins[2][...]).astype(BF16)

    h_kv, h_b = _rowwise(norms2_body, seq, [(x1, True), (g["kv_norm"], False), (g["b_norm"], False)],
                         [(D_MODEL, BF16), (D_MODEL, BF16)], name="kv_b_norm_fwd")
    ckr = _mm(h_kv, w["dkv"], name="dkv")

    def ckr_body(ins, outs, _):
        ckr_ref, gc_ref, gr_ref, c_ref, s_ref = ins
        outs[0][...] = _rms(ckr_ref[:, :KV_LORA], gc_ref[...]).astype(BF16)
        kr = _rms(ckr_ref[:, KV_LORA:], gr_ref[...], n=ROPE_DIM)
        outs[1][...] = _rope(kr, c_ref[...], s_ref[...]).astype(BF16)

    c_n, k_r = _rowwise(ckr_body, seq,
                        [(ckr, True), (g["g_ckv"], False), (g_k_rope, False), (cos_t, True), (sin_t, True)],
                        [(KV_LORA, BF16), (HEAD_DIM, BF16)], name="ckv_prep_fwd")
    kv = _mm(c_n, w["ukv"], name="ukv")

    def kcat_body(ins, outs, _):
        kv_ref, kr_ref, gk_ref = ins
        kc_ref, v_ref = outs
        for h in range(N_MLA_HEADS):
            kc_ref[:, h * CAT_W:h * CAT_W + HEAD_DIM] = _rms(
                kv_ref[:, h * CAT_W:h * CAT_W + HEAD_DIM], gk_ref[...]).astype(BF16)
            kc_ref[:, h * CAT_W + HEAD_DIM:(h + 1) * CAT_W] = kr_ref[...]
            v_ref[:, _hs(h)] = kv_ref[:, h * CAT_W + HEAD_DIM:(h + 1) * CAT_W].astype(BF16)

    k_cat, v_mla = _rowwise(kcat_body, seq, [(kv, True), (k_r, True), (g["g_k_nope"], False)],
                            [(N_MLA_HEADS * CAT_W, BF16), (MLA_W, BF16)], name="k_prep_fwd")

    p2 = _mm(h_b, w["b_in"], name="b_in")

    def qlat_body(ins, outs, _):
        outs[0][...] = _rms(ins[0][:, :Q_LORA], ins[1][...]).astype(BF16)

    (q_l,) = _rowwise(qlat_body, seq, [(p2, Q_LORA), (g["b_g_q_lat"], False)], [(Q_LORA, BF16)],
                      name="q_lat_norm_fwd")
    q_up = _mm(q_l, w["uq"], name="uq")

    def qcat_body(ins, outs, _):
        q_ref, gn_ref, gr_ref, c_ref, s_ref = ins
        (o_ref,) = outs
        for h in range(N_MLA_HEADS):
            o_ref[:, h * CAT_W:h * CAT_W + HEAD_DIM] = (MLA_Q_SCALE * _rms(
                q_ref[:, h * CAT_W:h * CAT_W + HEAD_DIM], gn_ref[...])).astype(BF16)
            qr = _rms(q_ref[:, h * CAT_W + HEAD_DIM:(h + 1) * CAT_W], gr_ref[...], n=ROPE_DIM)
            o_ref[:, h * CAT_W + HEAD_DIM:(h + 1) * CAT_W] = (
                MLA_Q_SCALE * _rope(qr, c_ref[...], s_ref[...])).astype(BF16)

    (q_cat,) = _rowwise(qcat_body, seq,
                        [(q_up, True), (g["b_g_q_nope"], False), (g_q_rope, False), (cos_t, True), (sin_t, True)],
                        [(N_MLA_HEADS * CAT_W, BF16)], name="q_prep_fwd")
    att, lse = _mla_fwd(q_cat, k_cat, v_mla, name="mla_fwd")
    mem1 = _mem_side_fwd(mem, g["mem_norm"][1:2], w["mem_kv"][1], g["g_mem_k"][1:2], tag="b")
    mixed_b = _mix_fwd(att, p2, Q_LORA, mem1[2], mem1[3], g["g_mem_q"][1:2], name="b_mix_fwd")
    y = _mm(mixed_b, w["b_out"], res=x1, name="b_out")

    def loss_body(ins, outs, accs):
        diff = ins[0][...] - ins[1][...]
        outs[0][...] = diff / D_MODEL
        col = jnp.sum(diff * diff, axis=0, keepdims=True)
        part = col[:, :HEAD_DIM]
        for c in range(1, D_MODEL // HEAD_DIM):
            part = part + col[:, _hs(c)]
        accs[0][...] += part * (0.5 / D_MODEL)

    dy, loss_part = _rowwise(loss_body, seq, [(y, True), (target, True)], [(D_MODEL, F32)],
                             [((1, HEAD_DIM), F32)], name="loss")

    gw, gg = {}, {}
    dmixed_b = _mm(dy, w["b_out"], tb=True, name="b_out_dx")
    gw["b_out"] = _mm(mixed_b, dy, ta=True, out_dtype=BF16, name="b_out_dw")
    datt, dgate_b, dmk1, dmv1, gq1 = _mix_bwd(dmixed_b, att, p2, Q_LORA, mem1[2], mem1[3],
                                              g["g_mem_q"][1:2], name="b_mix_bwd")
    dq_cat, dk_cat, dv_mla = _mla_bwd(q_cat, k_cat, v_mla, att, lse, datt, name="mla_bwd")

    def qcat_bwd_body(ins, outs, accs):
        q_ref, dq_ref, gn_ref, gr_ref, c_ref, s_ref = ins
        (o_ref,) = outs
        dgn_ref, dgr_ref = accs
        for h in range(N_MLA_HEADS):
            dx, dg = _rms_bwd(q_ref[:, h * CAT_W:h * CAT_W + HEAD_DIM], gn_ref[...],
                              dq_ref[:, h * CAT_W:h * CAT_W + HEAD_DIM])
            o_ref[:, h * CAT_W:h * CAT_W + HEAD_DIM] = dx.astype(BF16)
            dgn_ref[...] += dg
            dn = _rope_bwd(dq_ref[:, h * CAT_W + HEAD_DIM:(h + 1) * CAT_W], c_ref[...], s_ref[...])
            dx, dg = _rms_bwd(q_ref[:, h * CAT_W + HEAD_DIM:(h + 1) * CAT_W], gr_ref[...], dn, n=ROPE_DIM)
            o_ref[:, h * CAT_W + HEAD_DIM:(h + 1) * CAT_W] = dx.astype(BF16)
            dgr_ref[...] += dg

    dq_up, gg["b_g_q_nope"], dgqr = _rowwise(
        qcat_bwd_body, seq,
        [(q_up, True), (dq_cat, True), (g["b_g_q_nope"], False), (g_q_rope, False), (cos_t, True), (sin_t, True)],
        [(N_MLA_HEADS * CAT_W, BF16)], [((1, HEAD_DIM), F32), ((1, HEAD_DIM), F32)], name="q_prep_bwd")
    gg["b_g_q_rope"] = dgqr
    dq_l = _mm(dq_up, w["uq"], tb=True, name="uq_dx")
    gw["uq"] = _mm(q_l, dq_up, ta=True, out_dtype=BF16, n_split=N_CHIPS, name="uq_dw")

    def qlat_bwd_body(ins, outs, accs):
        p2_ref, dql_ref, dgate_ref, gl_ref = ins
        dx, dg = _rms_bwd(p2_ref[:, :Q_LORA], gl_ref[...], dql_ref[...])
        outs[0][:, :Q_LORA] = dx.astype(BF16)
        outs[0][:, Q_LORA:] = dgate_ref[...]
        accs[0][...] += dg

    dp2, gg["b_g_q_lat"] = _rowwise(
        qlat_bwd_body, seq, [(p2, Q_LORA), (dq_l, True), (dgate_b, True), (g["b_g_q_lat"], False)],
        [(Q_LORA + GATE_W, BF16)], [((1, Q_LORA), F32)], name="q_lat_norm_bwd")
    dh_b = _mm(dp2, w["b_in"], tb=True, name="b_in_dx")
    gw["b_in"] = _mm(h_b, dp2, ta=True, out_dtype=BF16, n_split=N_CHIPS, name="b_in_dw")

    def kcat_bwd_body(ins, outs, accs):
        kv_ref, dkc_ref, dv_ref, gk_ref = ins
        dkv_ref, dkr_ref = outs
        (dgk_ref,) = accs
        dkr = jnp.zeros(dkr_ref.shape, F32)
        for h in range(N_MLA_HEADS):
            dx, dg = _rms_bwd(kv_ref[:, h * CAT_W:h * CAT_W + HEAD_DIM], gk_ref[...],
                              dkc_ref[:, h * CAT_W:h * CAT_W + HEAD_DIM])
            dkv_ref[:, h * CAT_W:h * CAT_W + HEAD_DIM] = dx.astype(BF16)
            dgk_ref[...] += dg
            dkv_ref[:, h * CAT_W + HEAD_DIM:(h + 1) * CAT_W] = dv_ref[:, _hs(h)].astype(BF16)
            dkr = dkr + dkc_ref[:, h * CAT_W + HEAD_DIM:(h + 1) * CAT_W]
        dkr_ref[...] = dkr

    dkv, dk_r, gg["g_k_nope"] = _rowwise(
        kcat_bwd_body, seq, [(kv, True), (dk_cat, True), (dv_mla, True), (g["g_k_nope"], False)],
        [(N_MLA_HEADS * CAT_W, BF16), (HEAD_DIM, F32)], [((1, HEAD_DIM), F32)], name="k_prep_bwd")
    dc_n = _mm(dkv, w["ukv"], tb=True, name="ukv_dx")
    gw["ukv"] = _mm(c_n, dkv, ta=True, out_dtype=BF16, n_split=N_CHIPS, name="ukv_dw")

    def ckr_bwd_body(ins, outs, accs):
        ckr_ref, dcn_ref, dkr_ref, gc_ref, gr_ref, c_ref, s_ref = ins
        dx, dg = _rms_bwd(ckr_ref[:, :KV_LORA], gc_ref[...], dcn_ref[...])
        outs[0][:, :KV_LORA] = dx.astype(BF16)
        accs[0][...] += dg
        dn = _rope_bwd(dkr_ref[...], c_ref[...], s_ref[...])
        dx, dg = _rms_bwd(ckr_ref[:, KV_LORA:], gr_ref[...], dn, n=ROPE_DIM)
        outs[0][:, KV_LORA:] = dx.astype(BF16)
        accs[1][...] += dg

    dckr, gg["g_ckv"], gg["g_k_rope"] = _rowwise(
        ckr_bwd_body, seq,
        [(ckr, True), (dc_n, True), (dk_r, True), (g["g_ckv"], False), (g_k_rope, False),
         (cos_t, True), (sin_t, True)],
        [(KV_LORA + HEAD_DIM, BF16)], [((1, KV_LORA), F32), ((1, HEAD_DIM), F32)], name="ckv_prep_bwd")
    dh_kv = _mm(dckr, w["dkv"], tb=True, name="dkv_dx")
    gw["dkv"] = _mm(h_kv, dckr, ta=True, out_dtype=BF16, name="dkv_dw")

    def norms2_bwd_body(ins, outs, accs):
        x_ref, dy_ref, dhk_ref, dhb_ref, gk_ref, gb_ref = ins
        xv = x_ref[...]
        dxk, dgk = _rms_bwd(xv, gk_ref[...], dhk_ref[...])
        dxb, dgb = _rms_bwd(xv, gb_ref[...], dhb_ref[...])
        outs[0][...] = dy_ref[...] + dxk + dxb
        accs[0][...] += dgk
        accs[1][...] += dgb

    dx1, gg["kv_norm"], gg["b_norm"] = _rowwise(
        norms2_bwd_body, seq,
        [(x1, True), (dy, True), (dh_kv, True), (dh_b, True), (g["kv_norm"], False), (g["b_norm"], False)],
        [(D_MODEL, F32)], [((1, D_MODEL), F32), ((1, D_MODEL), F32)], name="kv_b_norm_bwd")

    dmixed_a = _mm(dx1, w["a_out"], tb=True, name="a_out_dx")
    gw["a_out"] = _mm(mixed_a, dx1, ta=True, out_dtype=BF16, name="a_out_dw")
    dsb, dgate_a, dmk0, dmv0, gq0 = _mix_bwd(dmixed_a, sb, gr, 0, mem0[2], mem0[3],
                                             g["g_mem_q"][0:1], name="a_mix_bwd")
    dq, dk, dv = _sb_bwd(qkv, sb, dsb, name="sb_bwd")
    dp_a = jnp.concatenate([dq, dk, dv, dgate_a], axis=1)
    dh_a = _mm(dp_a, w["a_in"], tb=True, name="a_in_dx")
    gw["a_in"] = _mm(h_a, dp_a, ta=True, out_dtype=BF16, n_split=N_CHIPS, name="a_in_dw")

    def norm_a_bwd_body(ins, outs, accs):
        dx, dg = _rms_bwd(ins[0][...], ins[3][...], ins[2][...])
        outs[0][...] = ins[1][...] + dx
        accs[0][...] += dg

    grad_x, gg["a_norm"] = _rowwise(
        norm_a_bwd_body, seq, [(x, True), (dx1, True), (dh_a, True), (g["a_norm"], False)],
        [(D_MODEL, F32)], [((1, D_MODEL), F32)], name="a_norm_bwd")

    dw0, dgn0, dgk0 = _mem_side_bwd(mem, g["mem_norm"][0:1], w["mem_kv"][0], g["g_mem_k"][0:1],
                                    mem0[0], mem0[1], dmk0, dmv0, tag="a")
    dw1, dgn1, dgk1 = _mem_side_bwd(mem, g["mem_norm"][1:2], w["mem_kv"][1], g["g_mem_k"][1:2],
                                    mem1[0], mem1[1], dmk1, dmv1, tag="b")
    gw["mem_kv"] = (dw0, dw1)
    gg["mem_norm"] = jnp.concatenate([dgn0, dgn1], axis=0)
    gg["g_mem_q"] = jnp.concatenate([gq0, gq1], axis=0)
    gg["g_mem_k"] = jnp.concatenate([dgk0, dgk1], axis=0)
    return loss_part, grad_x, gw, gg


HBM_SPEC = pl.BlockSpec(memory_space=pl.ANY)


def _other_chips():
    x, y = lax.axis_index("x"), lax.axis_index("y")
    return [(1 - x, y), (x, 1 - y), (1 - x, 1 - y)]


def _allgather_chips(shards):
    n = len(shards)
    split = [s.shape[0] % 32 == 0 for s in shards]

    def body(*refs):
        ins, outs = refs[:n], refs[n:2 * n]
        send, recv, fsend, frecv = refs[2 * n:]
        x, y, c = lax.axis_index("x"), lax.axis_index("y"), lax.axis_index("c")
        me = 2 * x + y
        chips = _other_chips()

        def part(ref, wi):
            if not split[wi]:
                return ref
            half = shards[wi].shape[0] // 2
            return ref.at[pl.ds(pl.multiple_of(c * half, 16), half)]

        def ici(wi, k, src_chip, to):
            return pltpu.make_async_remote_copy(
                src_ref=part(ins[wi], wi), dst_ref=part(outs[wi].at[src_chip], wi),
                send_sem=send.at[wi, k], recv_sem=recv.at[wi, k], device_id=to, device_id_type=MESH)

        def d2d(wi, k, src_chip):
            rows = part(outs[wi].at[src_chip], wi)
            return pltpu.make_async_remote_copy(
                src_ref=rows, dst_ref=rows, send_sem=fsend.at[wi, k], recv_sem=frecv.at[wi, k],
                device_id=(x, y, 1 - c), device_id_type=MESH)

        for wi in range(n):
            for k, (tx, ty) in enumerate(chips):
                ici(wi, k, me, (tx, ty, c)).start()
        for wi in range(n):
            for k, (tx, ty) in enumerate(chips):
                landed = ici(wi, k, 2 * tx + ty, (tx, ty, c))
                landed.wait_recv()
                if split[wi]:
                    d2d(wi, k, 2 * tx + ty).start()
        for wi in range(n):
            for k, (tx, ty) in enumerate(chips):
                ici(wi, k, me, (tx, ty, c)).wait_send()
                if split[wi]:
                    fwd = d2d(wi, k, 2 * tx + ty)
                    fwd.wait_send()
                    fwd.wait_recv()

    return pl.pallas_call(
        body, out_shape=[jax.ShapeDtypeStruct((N_CHIPS,) + s.shape, s.dtype) for s in shards],
        in_specs=[HBM_SPEC] * n, out_specs=[HBM_SPEC] * n,
        scratch_shapes=[pltpu.SemaphoreType.DMA((n, 3)), pltpu.SemaphoreType.DMA((n, 3)),
                        pltpu.SemaphoreType.DMA((n, 3)), pltpu.SemaphoreType.DMA((n, 3))],
        name="allgather_weights",
    )(*shards)


def _scatter_to_chips(grads):
    n = len(grads)

    def body(*refs):
        ins, outs = refs[:n], refs[n:2 * n]
        send, recv = refs[2 * n:]
        c = lax.axis_index("c")
        copies = []
        for wi in range(n):
            for k, (tx, ty) in enumerate(_other_chips()):
                cp = pltpu.make_async_remote_copy(
                    src_ref=ins[wi].at[2 * tx + ty], dst_ref=outs[wi].at[k], send_sem=send.at[wi, k],
                    recv_sem=recv.at[wi, k], device_id=(tx, ty, c), device_id_type=MESH)
                cp.start()
                copies.append(cp)
        for cp in copies:
            cp.wait()

    return pl.pallas_call(
        body, out_shape=[jax.ShapeDtypeStruct((3,) + s.shape[1:], s.dtype) for s in grads],
        in_specs=[HBM_SPEC] * n, out_specs=[HBM_SPEC] * n,
        scratch_shapes=[pltpu.SemaphoreType.DMA((n, 3)), pltpu.SemaphoreType.DMA((n, 3))],
        name="scatter_grads",
    )(*grads)


def _halve_with_sibling(grads):
    n = len(grads)
    n_slots = grads[0].shape[0]

    def body(*refs):
        ins, got = refs[:n], refs[n:2 * n]
        send, recv = refs[2 * n:]
        c = lax.axis_index("c")
        sib = (lax.axis_index("x"), lax.axis_index("y"), 1 - c)
        copies = []
        for wi in range(n):
            half = grads[wi].shape[1] // 2
            for s in range(n_slots):
                theirs = ins[wi].at[s, pl.ds(pl.multiple_of((1 - c) * half, 16), half)]
                give = pltpu.make_async_remote_copy(
                    src_ref=theirs, dst_ref=got[wi].at[s], send_sem=send.at[wi, s], recv_sem=recv.at[wi, s],
                    device_id=sib, device_id_type=MESH)
                give.start()
                copies.append(give)
        for cp in copies:
            cp.wait()

    halves = [jax.ShapeDtypeStruct((s.shape[0], s.shape[1] // 2) + s.shape[2:], s.dtype) for s in grads]
    return pl.pallas_call(
        body, out_shape=halves, in_specs=[HBM_SPEC] * n, out_specs=[HBM_SPEC] * n,
        scratch_shapes=[pltpu.SemaphoreType.DMA((n, n_slots)), pltpu.SemaphoreType.DMA((n, n_slots))],
        name="halve_grads_with_sibling",
    )(*grads)


def _swap_with_sibling(parts):
    n = len(parts)

    def body(*refs):
        ins, outs = refs[:n], refs[n:2 * n]
        send, recv = refs[2 * n:]
        sib = (lax.axis_index("x"), lax.axis_index("y"), 1 - lax.axis_index("c"))
        copies = []
        for wi in range(n):
            cp = pltpu.make_async_remote_copy(
                src_ref=ins[wi], dst_ref=outs[wi], send_sem=send.at[wi], recv_sem=recv.at[wi],
                device_id=sib, device_id_type=MESH)
            cp.start()
            copies.append(cp)
        for cp in copies:
            cp.wait()

    return pl.pallas_call(
        body, out_shape=[jax.ShapeDtypeStruct(s.shape, s.dtype) for s in parts],
        in_specs=[HBM_SPEC] * n, out_specs=[HBM_SPEC] * n,
        scratch_shapes=[pltpu.SemaphoreType.DMA((n,)), pltpu.SemaphoreType.DMA((n,))],
        name="swap_grad_halves",
    )(*parts)


def _allreduce_small(vec, loss_row):
    rows = vec.shape[0]

    def body(v_ref, o_ref, buf, send, recv):
        x, y, c = lax.axis_index("x"), lax.axis_index("y"), lax.axis_index("c")
        me = 4 * x + 2 * y + c
        buf[me] = v_ref[...]
        copies = []
        for r in range(1, N_DEV):
            peer = (x ^ ((r >> 2) & 1), y ^ ((r >> 1) & 1), c ^ (r & 1))
            cp = pltpu.make_async_remote_copy(
                src_ref=v_ref, dst_ref=buf.at[me], send_sem=send.at[r - 1], recv_sem=recv.at[r - 1],
                device_id=peer, device_id_type=MESH)
            cp.start()
            copies.append(cp)
        for cp in copies:
            cp.wait()
        total = buf[0]
        for d in range(1, N_DEV):
            total = total + buf[d]
        o_ref[...] = total
        o_ref[loss_row:loss_row + 1, :] = jnp.broadcast_to(
            jnp.sum(total[loss_row:loss_row + 1, :], axis=-1, keepdims=True), (1, HEAD_DIM))

    return pl.pallas_call(
        body, out_shape=jax.ShapeDtypeStruct(vec.shape, F32),
        in_specs=[pl.BlockSpec(memory_space=pltpu.VMEM)], out_specs=pl.BlockSpec(memory_space=pltpu.VMEM),
        scratch_shapes=[pltpu.VMEM((N_DEV, rows, HEAD_DIM), F32),
                        pltpu.SemaphoreType.DMA((N_DEV - 1,)), pltpu.SemaphoreType.DMA((N_DEV - 1,))],
        name="allreduce_gains",
    )(vec)


def _pair_sum(grads, got, *, name):
    slots, rows, width = got.shape
    blk = _pick(rows, (256, 128, 64, 32, 16))
    nbh = rows // blk

    def body(lo_ref, hi_ref, got_ref, o_ref):
        mine = jnp.where(lax.axis_index("c") == 0, lo_ref[...], hi_ref[...])
        o_ref[...] = (mine.astype(F32) + got_ref[...].astype(F32)).astype(BF16)

    spec = pl.BlockSpec((None, blk, width), lambda s, i: (s, i, 0))
    return pl.pallas_call(
        body, out_shape=jax.ShapeDtypeStruct(got.shape, BF16), grid=(slots, nbh),
        in_specs=[spec, pl.BlockSpec((None, blk, width), lambda s, i: (s, nbh + i, 0)), spec],
        out_specs=spec, name=name, compiler_params=_params("parallel", "parallel"),
    )(grads, grads, got)


def _sum_slots(recv, chip_sum, *, name):
    _, rows, width = recv.shape
    blk = _pick(rows, (256, 128, 64, 32, 16, 8))

    def body(r_ref, p_ref, o_ref):
        me = 2 * lax.axis_index("x") + lax.axis_index("y")
        own = jnp.where(me < 2, jnp.where(me == 0, p_ref[0], p_ref[1]), jnp.where(me == 2, p_ref[2], p_ref[3]))
        o_ref[...] = ((own.astype(F32) + r_ref[0].astype(F32)) + r_ref[1].astype(F32)) + r_ref[2].astype(F32)

    return pl.pallas_call(
        body, out_shape=jax.ShapeDtypeStruct((rows, width), F32), grid=(rows // blk,),
        in_specs=[pl.BlockSpec((3, blk, width), lambda i: (0, i, 0)),
                  pl.BlockSpec((N_CHIPS, blk, width), lambda i: (0, i, 0))],
        out_specs=pl.BlockSpec((blk, width), lambda i: (i, 0)),
        name=name, compiler_params=_params("parallel"),
    )(recv, chip_sum)


def _adamw(wgt, grad, m, v, *, name, halves=None):
    rows, width = wgt.shape
    blk = _pick(rows // 2 if halves else rows, (256, 128, 64, 32, 16, 8))
    nbh = rows // 2 // blk

    def body(*refs):
        if halves:
            w_ref, mine_ref, theirs_ref, m_ref, v_ref, g_out, d_out, m_out, v_out = refs
            grad_v = jnp.where(pl.program_id(0) // nbh == lax.axis_index("c"), mine_ref[...], theirs_ref[...])
        else:
            w_ref, g_ref, m_ref, v_ref, g_out, d_out, m_out, v_out = refs
            grad_v = g_ref[...]
        m_new = ADAM_B1 * m_ref[...] + (1.0 - ADAM_B1) * grad_v
        v_new = ADAM_B2 * v_ref[...] + (1.0 - ADAM_B2) * (grad_v * grad_v)
        m_hat = m_new / (1.0 - ADAM_B1 ** ADAM_STEP)
        v_hat = v_new / (1.0 - ADAM_B2 ** ADAM_STEP)
        g_out[...] = grad_v
        d_out[...] = -ADAM_LR * (m_hat / (jnp.sqrt(v_hat) + ADAM_EPS) + ADAM_WD * w_ref[...])
        m_out[...] = m_new
        v_out[...] = v_new

    spec = pl.BlockSpec((blk, width), lambda i: (i, 0))
    half_spec = pl.BlockSpec((blk, width), lambda i: (i % nbh, 0))
    g_specs, g_args = ([half_spec, half_spec], list(halves)) if halves else ([spec], [grad])
    out = jax.ShapeDtypeStruct((rows, width), F32)
    return pl.pallas_call(
        body, out_shape=[out] * 4, grid=(rows // blk,), in_specs=[spec] + g_specs + [spec, spec],
        out_specs=[spec] * 4, name=name, compiler_params=_params("parallel"),
    )(wgt, *g_args, m, v)


_SMALL = (("a_norm", 2048), ("kv_norm", 2048), ("g_ckv", 512), ("g_k_nope", 128), ("g_k_rope", 64),
          ("b_norm", 2048), ("b_g_q_lat", 512), ("b_g_q_nope", 128), ("b_g_q_rope", 64),
          ("mem_norm", 4096), ("g_mem_q", 256), ("g_mem_k", 256))


def _lanes(n):
    return -(-n // HEAD_DIM) * HEAD_DIM


def _pack_rows(pieces, pad_rows_to=8):
    flat = jnp.concatenate(pieces, axis=1)
    rows = flat.shape[1] // HEAD_DIM
    pad = (-rows) % pad_rows_to
    if pad:
        flat = jnp.concatenate([flat, jnp.zeros((1, pad * HEAD_DIM), F32)], axis=1)
    return flat.reshape(rows + pad, HEAD_DIM)


def _pad_lanes(a):
    a = a.reshape(1, -1)
    pad = _lanes(a.shape[1]) - a.shape[1]
    if pad:
        a = jnp.concatenate([a, jnp.zeros((1, pad), F32)], axis=1)
    return a


def kernel(x, mem, positions, a_norm, a_w_in, a_w_out, kv_norm, w_dkv, g_ckv, w_ukv, g_k_nope, g_k_rope, b_norm, b_w_in, b_g_q_lat, b_w_uq, b_g_q_nope, b_g_q_rope, b_w_out, mem_norm, w_mem_kv, g_mem_q, g_mem_k, loss_target, m_a_norm, m_a_w_in, m_a_w_out, m_kv_norm, m_w_dkv, m_g_ckv, m_w_ukv, m_g_k_nope, m_g_k_rope, m_b_norm, m_b_w_in, m_b_g_q_lat, m_b_w_uq, m_b_g_q_nope, m_b_g_q_rope, m_b_w_out, m_mem_norm, m_w_mem_kv, m_g_mem_q, m_g_mem_k, v_a_norm, v_a_w_in, v_a_w_out, v_kv_norm, v_w_dkv, v_g_ckv, v_w_ukv, v_g_k_nope, v_g_k_rope, v_b_norm, v_b_w_in, v_b_g_q_lat, v_b_w_uq, v_b_g_q_nope, v_b_g_q_rope, v_b_w_out, v_mem_norm, v_w_mem_kv, v_g_mem_q, v_g_mem_k):
    chip = 2 * lax.axis_index("x") + lax.axis_index("y")
    rows_dkv = D_MODEL // N_CHIPS
    heads_per_chip = N_MLA_HEADS // N_CHIPS
    qk_w = HEAD_DIM + ROPE_DIM

    big = {"a_in": a_w_in[0], "a_out": a_w_out[0], "dkv": w_dkv, "ukv": w_ukv, "b_in": b_w_in[0],
           "uq": b_w_uq[0], "b_out": b_w_out[0], "mem_kv": w_mem_kv.reshape(2 * rows_dkv, 2 * MEM_W)}
    big_m = {"a_in": m_a_w_in[0], "a_out": m_a_w_out[0], "dkv": m_w_dkv, "ukv": m_w_ukv, "b_in": m_b_w_in[0],
             "uq": m_b_w_uq[0], "b_out": m_b_w_out[0], "mem_kv": m_w_mem_kv.reshape(2 * rows_dkv, 2 * MEM_W)}
    big_v = {"a_in": v_a_w_in[0], "a_out": v_a_w_out[0], "dkv": v_w_dkv, "ukv": v_w_ukv, "b_in": v_b_w_in[0],
             "uq": v_b_w_uq[0], "b_out": v_b_w_out[0], "mem_kv": v_w_mem_kv.reshape(2 * rows_dkv, 2 * MEM_W)}
    names = list(big)
    own_shards = [big[n].astype(BF16) for n in names] + [a_norm]
    gathered = _allgather_chips(own_shards)
    gathered = [lax.dynamic_update_slice(g, s[None], (chip,) + (0,) * s.ndim)
                for g, s in zip(gathered, own_shards)]
    st = dict(zip(names, gathered[:-1]))
    a_in_full = st["a_in"].transpose(1, 0, 2).reshape(D_MODEL, QKV_W + GATE_W)
    uq = st["uq"].reshape(N_CHIPS, Q_LORA, heads_per_chip, qk_w)
    uq = jnp.pad(uq, ((0, 0), (0, 0), (0, 0), (0, CAT_W - qk_w)))
    w = {
        "a_in": a_in_full,
        "a_in_qkv": a_in_full[:, :QKV_W],
        "a_in_gate": a_in_full[:, QKV_W:],
        "a_out": st["a_out"].reshape(D_MODEL, D_MODEL),
        "dkv": jnp.pad(st["dkv"].reshape(D_MODEL, KV_LORA + ROPE_DIM), ((0, 0), (0, HEAD_DIM - ROPE_DIM))),
        "ukv": st["ukv"].transpose(1, 0, 2).reshape(KV_LORA, N_MLA_HEADS * CAT_W),
        "b_in": st["b_in"].transpose(1, 0, 2).reshape(D_MODEL, Q_LORA + GATE_W),
        "uq": uq.transpose(1, 0, 2, 3).reshape(Q_LORA, N_MLA_HEADS * CAT_W),
        "b_out": st["b_out"].reshape(D_MODEL, D_MODEL),
        "mem_kv": st["mem_kv"].reshape(N_CHIPS, 2, rows_dkv, 2 * MEM_W).transpose(1, 0, 2, 3).reshape(
            2, D_MODEL, 2 * MEM_W),
    }
    gains = {
        "a_norm": gathered[-1].reshape(1, D_MODEL), "kv_norm": kv_norm.reshape(1, -1),
        "g_ckv": g_ckv.reshape(1, -1), "g_k_nope": g_k_nope.reshape(1, -1), "g_k_rope": g_k_rope.reshape(1, -1),
        "b_norm": b_norm, "b_g_q_lat": b_g_q_lat, "b_g_q_nope": b_g_q_nope, "b_g_q_rope": b_g_q_rope,
        "mem_norm": mem_norm, "g_mem_q": g_mem_q, "g_mem_k": g_mem_k,
    }

    loss_part, grad_x, gw, gg = _local_step(x[0], mem[0], positions[0], loss_target[0], w, gains)

    stacked = {
        "a_in": gw["a_in"],
        "a_out": gw["a_out"].reshape(N_CHIPS, rows_dkv, D_MODEL),
        "dkv": gw["dkv"][:, :KV_LORA + ROPE_DIM].reshape(N_CHIPS, rows_dkv, KV_LORA + ROPE_DIM),
        "ukv": gw["ukv"],
        "b_in": gw["b_in"],
        "uq": gw["uq"].reshape(N_CHIPS, Q_LORA, heads_per_chip, CAT_W)[..., :qk_w].reshape(
            N_CHIPS, Q_LORA, heads_per_chip * qk_w),
        "b_out": gw["b_out"].reshape(N_CHIPS, rows_dkv, D_MODEL),
        "mem_kv": jnp.stack([gw["mem_kv"][0].reshape(N_CHIPS, rows_dkv, 2 * MEM_W),
                             gw["mem_kv"][1].reshape(N_CHIPS, rows_dkv, 2 * MEM_W)], axis=1).reshape(
            N_CHIPS, 2 * rows_dkv, 2 * MEM_W),
    }
    got = _halve_with_sibling([stacked[n] for n in names])
    chip_sum = [_pair_sum(stacked[n], g, name=f"pair_sum_{n}") for n, g in zip(names, got)]
    received = _scatter_to_chips(chip_sum)
    half_total = [_sum_slots(r, p, name=f"sum_slots_{n}") for n, r, p in zip(names, received, chip_sum)]
    sibling_half = _swap_with_sibling(half_total)
    big_out = {}
    for n, mine, theirs in zip(names, half_total, sibling_half):
        big_out[n] = _adamw(big[n], None, big_m[n], big_v[n], halves=(mine, theirs), name=f"adamw_{n}")

    pieces = [_pad_lanes(gg[n]) if n not in ("g_k_rope", "b_g_q_rope") else gg[n] for n, _ in _SMALL]
    pieces.append(loss_part)
    loss_row = sum(_lanes(size) for _, size in _SMALL) // HEAD_DIM
    summed = _allreduce_small(_pack_rows(pieces), loss_row)
    flat = summed.reshape(1, -1)
    small_g, off = {}, 0
    for n, size in _SMALL:
        small_g[n] = flat[:, off:off + size]
        off += _lanes(size)
    loss = flat[0, off]
    small_g["a_norm"] = lax.dynamic_slice(small_g["a_norm"], (0, chip * rows_dkv), (1, rows_dkv))

    small_w = {"a_norm": a_norm, "kv_norm": kv_norm, "g_ckv": g_ckv, "g_k_nope": g_k_nope, "g_k_rope": g_k_rope,
               "b_norm": b_norm, "b_g_q_lat": b_g_q_lat, "b_g_q_nope": b_g_q_nope, "b_g_q_rope": b_g_q_rope,
               "mem_norm": mem_norm, "g_mem_q": g_mem_q, "g_mem_k": g_mem_k}
    small_m = {"a_norm": m_a_norm, "kv_norm": m_kv_norm, "g_ckv": m_g_ckv, "g_k_nope": m_g_k_nope,
               "g_k_rope": m_g_k_rope, "b_norm": m_b_norm, "b_g_q_lat": m_b_g_q_lat, "b_g_q_nope": m_b_g_q_nope,
               "b_g_q_rope": m_b_g_q_rope, "mem_norm": m_mem_norm, "g_mem_q": m_g_mem_q, "g_mem_k": m_g_mem_k}
    small_v = {"a_norm": v_a_norm, "kv_norm": v_kv_norm, "g_ckv": v_g_ckv, "g_k_nope": v_g_k_nope,
               "g_k_rope": v_g_k_rope, "b_norm": v_b_norm, "b_g_q_lat": v_b_g_q_lat, "b_g_q_nope": v_b_g_q_nope,
               "b_g_q_rope": v_b_g_q_rope, "mem_norm": v_mem_norm, "g_mem_q": v_g_mem_q, "g_mem_k": v_g_mem_k}
    snames = [n for n, _ in _SMALL]
    packs = [_pack_rows([_pad_lanes(src[n]) for n in snames])
             for src in (small_w, small_g, small_m, small_v)]
    small_res = _adamw(packs[0], packs[1], packs[2], packs[3], name="adamw_gains")
    small_out = {n: [] for n in snames}
    for res in small_res:
        flat_r = res.reshape(1, -1)
        off = 0
        for n in snames:
            size = small_w[n].size
            small_out[n].append(flat_r[:, off:off + size].reshape(small_w[n].shape))
            off += _lanes(size)

    big_names = {"a_w_in": ("a_in", a_w_in), "a_w_out": ("a_out", a_w_out), "w_dkv": ("dkv", w_dkv),
                 "w_ukv": ("ukv", w_ukv), "b_w_in": ("b_in", b_w_in), "b_w_uq": ("uq", b_w_uq),
                 "b_w_out": ("b_out", b_w_out), "w_mem_kv": ("mem_kv", w_mem_kv)}
    order = ["a_norm", "a_w_in", "a_w_out", "kv_norm", "w_dkv", "g_ckv", "w_ukv", "g_k_nope", "g_k_rope",
             "b_norm", "b_w_in", "b_g_q_lat", "b_w_uq", "b_g_q_nope", "b_g_q_rope", "b_w_out", "mem_norm",
             "w_mem_kv", "g_mem_q", "g_mem_k"]
    groups = [[], [], [], []]
    for n in order:
        if n in big_names:
            key, ref_arr = big_names[n]
            for t in range(4):
                groups[t].append(big_out[key][t].reshape(ref_arr.shape))
        else:
            for t in range(4):
                groups[t].append(small_out[n][t])
    return (loss, grad_x[None], *groups[0], *groups[1], *groups[2], *groups[3])
```

```python
import jax
import jax.numpy as jnp
from jax import lax
from jax.experimental import pallas as pl
from jax.experimental.pallas import tpu as pltpu

F32 = jnp.float32
BF16 = jnp.bfloat16
MESH = pl.DeviceIdType.MESH

D_MODEL = 2048
HEAD_DIM = 128
N_SB_HEADS = 12
N_MEM_HEADS = 4
N_MLA_HEADS = 12
MEM_LEN = 256
Q_LORA = 512
KV_LORA = 512
ROPE_DIM = 64
SB_W = N_SB_HEADS * HEAD_DIM
MEM_W = N_MEM_HEADS * HEAD_DIM
MLA_W = N_MLA_HEADS * HEAD_DIM
QKV_W = 3 * SB_W
GATE_W = SB_W + 2 * MEM_W
CAT_W = 2 * HEAD_DIM
ROPE_THETA = 10000.0
EPS = 1e-6
N_CHIPS = 4
N_DEV = 8

ADAM_LR = 0.001
ADAM_B1 = 0.9
ADAM_B2 = 0.999
ADAM_EPS = 1e-08
ADAM_WD = 0.01
ADAM_STEP = 10

VMEM_LIMIT_BYTES = 56 * 1024 * 1024
MM_OPERAND_VMEM_BYTES = 24 * 1024 * 1024
ROW_BLOCK = 256
ATT_BLOCK = 256


def _params(*sem):
    return pltpu.CompilerParams(dimension_semantics=sem, vmem_limit_bytes=VMEM_LIMIT_BYTES)


def _pick(n, cands):
    for c in cands:
        if n % c == 0:
            return c
    return n


def _mm(a, b, *, name, ta=False, tb=False, out_dtype=F32, res=None, n_split=1, scale_cols=None):
    if ta:
        k_dim, m_dim = a.shape
    else:
        m_dim, k_dim = a.shape
    if tb:
        n_dim, kb = b.shape
    else:
        kb, n_dim = b.shape
    assert kb == k_dim, (a.shape, b.shape)
    n_per = n_dim // n_split
    bm = m_dim if m_dim <= 1024 else _pick(m_dim, (1024, 512, 256))
    bn = n_per if n_per <= 1024 else _pick(n_per, (1024, 896, 768, 640, 512, 256, 128))
    per_k = (bm * a.dtype.itemsize + bn * b.dtype.itemsize) * 2
    bk = next((k_dim // d for d in range(1, k_dim // 128 + 1)
               if k_dim % d == 0 and (k_dim // d) % 128 == 0 and (k_dim // d) * per_k <= MM_OPERAND_VMEM_BYTES),
              k_dim)
    nk = k_dim // bk
    nb_per = n_per // bn
    grid = (m_dim // bm, n_dim // bn, nk)
    a_spec = (pl.BlockSpec((bk, bm), lambda i, j, k: (k, i)) if ta
              else pl.BlockSpec((bm, bk), lambda i, j, k: (i, k)))
    b_spec = (pl.BlockSpec((bn, bk), lambda i, j, k: (j, k)) if tb
              else pl.BlockSpec((bk, bn), lambda i, j, k: (k, j)))
    dims = (((0 if ta else 1,), (1 if tb else 0,)), ((), ()))
    in_specs = [a_spec, b_spec]
    args = [a, b]
    if res is not None:
        in_specs.append(pl.BlockSpec((bm, bn), lambda i, j, k: (i, j)))
        args.append(res)
    if n_split == 1:
        out_shape = jax.ShapeDtypeStruct((m_dim, n_dim), out_dtype)
        out_spec = pl.BlockSpec((bm, bn), lambda i, j, k: (i, j))
    else:
        out_shape = jax.ShapeDtypeStruct((n_split, m_dim, n_per), out_dtype)
        out_spec = pl.BlockSpec((None, bm, bn), lambda i, j, k: (j // nb_per, i, j % nb_per))

    def body(*refs):
        if res is None:
            a_ref, b_ref, o_ref, acc = refs
            r_ref = None
        else:
            a_ref, b_ref, r_ref, o_ref, acc = refs
        k = pl.program_id(2)
        col_block = pl.program_id(1)

        @pl.when(k == 0)
        def _():
            acc[...] = jnp.zeros_like(acc)

        acc[...] += lax.dot_general(a_ref[...].astype(BF16), b_ref[...].astype(BF16), dims,
                                    preferred_element_type=F32)

        @pl.when(k == nk - 1)
        def _():
            r = acc[...]
            if r_ref is not None:
                r = r + r_ref[...]
            if scale_cols is not None:
                assert scale_cols[0] % bn == 0
                r = r * jnp.where(col_block < scale_cols[0] // bn, scale_cols[1], 1.0)
            o_ref[...] = r.astype(out_dtype)

    return pl.pallas_call(
        body, out_shape=out_shape, grid=grid, in_specs=in_specs, out_specs=out_spec,
        scratch_shapes=[pltpu.VMEM((bm, bn), F32)], name=name,
        compiler_params=_params("parallel", "parallel", "arbitrary"),
    )(*args)


def _rowwise(body, n_rows, ins, outs, accs=(), *, name, block=ROW_BLOCK):
    blk = min(block, n_rows)
    assert n_rows % blk == 0
    in_specs = []
    for arr, is_row in ins:
        if is_row:
            assert arr.shape[0] == n_rows, (name, arr.shape, n_rows)
            width = arr.shape[1] if is_row is True else is_row
            in_specs.append(pl.BlockSpec((blk, width), lambda i: (i, 0)))
        else:
            in_specs.append(pl.BlockSpec(arr.shape, lambda i, nd=arr.ndim: (0,) * nd))
    out_shape = [jax.ShapeDtypeStruct((n_rows, w), dt) for w, dt in outs]
    out_specs = [pl.BlockSpec((blk, w), lambda i: (i, 0)) for w, _ in outs]
    out_shape += [jax.ShapeDtypeStruct(s, dt) for s, dt in accs]
    out_specs += [pl.BlockSpec(s, lambda i, nd=len(s): (0,) * nd) for s, _ in accs]
    n_in, n_out, n_acc = len(ins), len(outs), len(accs)

    def kern(*refs):
        in_refs = refs[:n_in]
        out_refs = refs[n_in:n_in + n_out]
        acc_refs = refs[n_in + n_out:]
        if n_acc:
            @pl.when(pl.program_id(0) == 0)
            def _():
                for r in acc_refs:
                    r[...] = jnp.zeros_like(r)
        body(in_refs, out_refs, acc_refs)

    return pl.pallas_call(
        kern, out_shape=out_shape, grid=(n_rows // blk,), in_specs=in_specs, out_specs=out_specs,
        name=name, compiler_params=_params("arbitrary"),
    )(*[arr for arr, _ in ins])


def _rms(x, g, n=None):
    n = x.shape[-1] if n is None else n
    r = lax.rsqrt(jnp.sum(x * x, axis=-1, keepdims=True) / n + EPS)
    return x * r * g


def _rms_bwd(x, g, dy, n=None):
    n = x.shape[-1] if n is None else n
    r = lax.rsqrt(jnp.sum(x * x, axis=-1, keepdims=True) / n + EPS)
    gdy = dy * g
    dx = r * (gdy - x * ((r * r) * (jnp.sum(gdy * x, axis=-1, keepdims=True) / n)))
    dg = jnp.sum(dy * x * r, axis=0, keepdims=True)
    return dx, dg


def _swap_halves(x):
    lane = lax.broadcasted_iota(jnp.int32, x.shape, 1)
    return jnp.where(lane < ROPE_DIM // 2, pltpu.roll(x, 128 - ROPE_DIM // 2, 1),
                     pltpu.roll(x, ROPE_DIM // 2, 1))


def _rope(n, cos_t, sin_t):
    return n * cos_t + _swap_halves(n) * sin_t


def _rope_bwd(dy, cos_t, sin_t):
    return dy * cos_t - _swap_halves(dy) * sin_t


def _sigmoid(g):
    return 1.0 / (1.0 + jnp.exp(-g))


def _dot_t(a, b):
    return lax.dot_general(a, b, (((1,), (1,)), ((), ())), preferred_element_type=F32)


def _tdot(a, b):
    return lax.dot_general(a, b, (((0,), (0,)), ((), ())), preferred_element_type=F32)


def _dot(a, b):
    return jnp.dot(a, b, preferred_element_type=F32)


def _hs(h, w=HEAD_DIM, base=0):
    return slice(base + h * w, base + (h + 1) * w)


def _mem_head(qm, gq, mk_h, mv_h):
    qb = _rms(qm, gq).astype(BF16)
    s = _dot_t(qb, mk_h) * (HEAD_DIM ** -0.5)
    e = jnp.exp(s - jnp.max(s, axis=-1, keepdims=True))
    p = e / jnp.sum(e, axis=-1, keepdims=True)
    mo = _dot(p.astype(BF16), mv_h)
    return qb, p, mo


def _mix_fwd(att, gates, c0, mk, mv, gq, *, name):
    n_rows = att.shape[0]

    def body(ins, outs, _):
        att_ref, g_ref, mk_ref, mv_ref, gq_ref = ins
        (o_ref,) = outs
        g = g_ref[:, c0:c0 + SB_W]
        o_ref[:, :SB_W] = (att_ref[...] * (g * _sigmoid(g))).astype(BF16)
        for h in range(N_MEM_HEADS):
            qm = g_ref[:, _hs(h, base=c0 + SB_W)]
            gm = g_ref[:, _hs(h, base=c0 + SB_W + MEM_W)]
            _, _, mo = _mem_head(qm, gq_ref[...], mk_ref[:, _hs(h)], mv_ref[:, _hs(h)])
            o_ref[:, _hs(h, base=SB_W)] = (mo * (gm * _sigmoid(gm))).astype(BF16)

    (mixed,) = _rowwise(body, n_rows,
                        [(att, True), (gates, True), (mk, False), (mv, False), (gq, False)],
                        [(D_MODEL, BF16)], name=name)
    return mixed


def _mix_bwd(dmixed, att, gates, c0, mk, mv, gq, *, name):
    n_rows = att.shape[0]
    scale = HEAD_DIM ** -0.5

    def body(ins, outs, accs):
        dm_ref, att_ref, g_ref, mk_ref, mv_ref, gq_ref = ins
        datt_ref, dg_ref = outs
        dmk_ref, dmv_ref, dgq_ref = accs
        g = g_ref[:, c0:c0 + SB_W]
        sg = _sigmoid(g)
        dm = dm_ref[:, :SB_W]
        datt_ref[...] = dm * (g * sg)
        dg_ref[:, :SB_W] = (dm * att_ref[...] * (sg * (1.0 + g * (1.0 - sg)))).astype(BF16)
        for h in range(N_MEM_HEADS):
            qm = g_ref[:, _hs(h, base=c0 + SB_W)]
            gm = g_ref[:, _hs(h, base=c0 + SB_W + MEM_W)]
            mk_h = mk_ref[:, _hs(h)]
            mv_h = mv_ref[:, _hs(h)]
            qb, p, mo = _mem_head(qm, gq_ref[...], mk_h, mv_h)
            sgm = _sigmoid(gm)
            dmh = dm_ref[:, _hs(h, base=SB_W)]
            dmo = dmh * (gm * sgm)
            dg_ref[:, _hs(h, base=SB_W + MEM_W)] = (
                dmh * mo * (sgm * (1.0 + gm * (1.0 - sgm)))).astype(BF16)
            dmo_b = dmo.astype(BF16)
            pb = p.astype(BF16)
            dp = _dot_t(dmo_b, mv_h)
            dmv_ref[:, _hs(h)] += _tdot(pb, dmo_b)
            ds = (p * (dp - jnp.sum(dp * p, axis=-1, keepdims=True)) * scale).astype(BF16)
            dqn = _dot(ds, mk_h)
            dmk_ref[:, _hs(h)] += _tdot(ds, qb)
            dqm, dgq = _rms_bwd(qm, gq_ref[...], dqn)
            dg_ref[:, _hs(h, base=SB_W)] = dqm.astype(BF16)
            dgq_ref[...] += dgq

    return _rowwise(body, n_rows,
                    [(dmixed, True), (att, True), (gates, True), (mk, False), (mv, False), (gq, False)],
                    [(SB_W, F32), (GATE_W, BF16)],
                    [((MEM_LEN, MEM_W), F32), ((MEM_LEN, MEM_W), F32), ((1, HEAD_DIM), F32)],
                    name=name)


def _mem_side_fwd(mem, g_norm, w_kv, g_k, *, tag):
    def norm_body(ins, outs, _):
        outs[0][...] = _rms(ins[0][...], ins[1][...]).astype(BF16)

    (mn,) = _rowwise(norm_body, MEM_LEN, [(mem, True), (g_norm, False)], [(D_MODEL, BF16)],
                     name=f"mem_norm_{tag}")
    mkv = _mm(mn, w_kv, name=f"mem_kv_{tag}")

    def kv_body(ins, outs, _):
        mkv_ref, gk_ref = ins
        mk_ref, mv_ref = outs
        for h in range(N_MEM_HEADS):
            mk_ref[:, _hs(h)] = _rms(mkv_ref[:, _hs(h)], gk_ref[...]).astype(BF16)
        mv_ref[...] = mkv_ref[:, MEM_W:].astype(BF16)

    mk, mv = _rowwise(kv_body, MEM_LEN, [(mkv, True), (g_k, False)], [(MEM_W, BF16), (MEM_W, BF16)],
                      name=f"mem_kv_prep_{tag}")
    return mn, mkv, mk, mv


def _mem_side_bwd(mem, g_norm, w_kv, g_k, mn, mkv, dmk, dmv, *, tag):
    def kv_body(ins, outs, accs):
        mkv_ref, gk_ref, dmk_ref, dmv_ref = ins
        (d_ref,) = outs
        (dgk_ref,) = accs
        for h in range(N_MEM_HEADS):
            dx, dg = _rms_bwd(mkv_ref[:, _hs(h)], gk_ref[...], dmk_ref[:, _hs(h)])
            d_ref[:, _hs(h)] = dx.astype(BF16)
            dgk_ref[...] += dg
        d_ref[:, MEM_W:] = dmv_ref[...].astype(BF16)

    dmkv, dgk = _rowwise(kv_body, MEM_LEN, [(mkv, True), (g_k, False), (dmk, True), (dmv, True)],
                         [(2 * MEM_W, BF16)], [((1, HEAD_DIM), F32)], name=f"mem_kv_prep_bwd_{tag}")
    dmn = _mm(dmkv, w_kv, tb=True, name=f"mem_kv_dx_{tag}")
    dw = _mm(mn, dmkv, ta=True, out_dtype=BF16, name=f"mem_kv_dw_{tag}")

    def norm_body(ins, outs, accs):
        _, dg = _rms_bwd(ins[0][...], ins[1][...], ins[2][...])
        accs[0][...] += dg

    (dgn,) = _rowwise(norm_body, MEM_LEN, [(mem, True), (g_norm, False), (dmn, True)], [],
                      [((1, D_MODEL), F32)], name=f"mem_norm_bwd_{tag}")
    return dw, dgn, dgk


LOG2_E = 1.4426950408889634
SB_Q_SCALE = HEAD_DIM ** -0.5 * LOG2_E


Z2_CAP = 126.0


def _sb_terms(z2):
    zc = jnp.minimum(z2, Z2_CAP)
    w = 1.0 + jnp.exp2(zc)
    return zc, w, jnp.log2(w)


LOOP_UNROLL = 8


def _loop_blocks(base, qb, body, carry, *, reverse):
    def run(start, trips, unroll, c0):
        def trip(t, c):
            for u in range(unroll):
                p = start + t * unroll + u
                c = body(base - 1 - p if reverse else p, c)
            return c
        return lax.fori_loop(0, trips, trip, c0)

    if qb % LOOP_UNROLL == 0:
        return run(0, base // LOOP_UNROLL, LOOP_UNROLL, carry)
    small = qb
    n_big = base // LOOP_UNROLL
    carry = run(0, n_big, LOOP_UNROLL, carry)
    return run(n_big * LOOP_UNROLL, (base - n_big * LOOP_UNROLL) // small, small, carry)


def _chain_modes(s, qb):
    return tuple(None if t < s else ("m" if t == s else "f") for t in range(qb))


def _split_dot(x, tri2):
    hi = x.astype(BF16)
    lo = (x - hi.astype(F32)).astype(BF16)
    return _dot(jnp.concatenate([hi, lo], axis=1), tri2)


def _sb_fwd(qkv, *, name, hp=1):
    seq = qkv.shape[0]
    blk = min(ATT_BLOCK, seq)
    nkb = seq // blk
    qb = _pick(nkb, (4, 2, 1))
    rows = qb * blk
    chains = [(t, s) for t in range(hp) for s in range(qb)]

    def body(q_ref, k_ref, v_ref, o_ref):
        base = pl.program_id(1) * qb
        qs = {(t, s): q_ref[s * blk:(s + 1) * blk, _hs(t)] for t, s in chains}
        row = lax.broadcasted_iota(jnp.int32, (blk, blk), 0)
        col = lax.broadcasted_iota(jnp.int32, (blk, blk), 1)
        after = (row > col).astype(BF16)
        after2 = jnp.concatenate([after, after], axis=0)
        causal = col < row

        def step(j, carry, modes):
            off = pl.multiple_of(j * blk, blk)
            act = [c for c in chains if modes[c[1]]]
            zs, ls = {}, {}
            for c in act:
                zs[c], _, l = _sb_terms(_dot_t(qs[c], k_ref[pl.ds(off, blk), _hs(c[0])]))
                ls[c] = jnp.where(causal, l, 0.0) if modes[c[1]] == "m" else l
            cs = {c: _split_dot(ls[c], after2) for c in act}
            carry = dict(carry)
            for c in act:
                run, acc = carry[c]
                a = jnp.exp2(zs[c] - ls[c] - cs[c] - run)
                if modes[c[1]] == "m":
                    a = jnp.where(causal, a, 0.0)
                acc = acc + _dot(a.astype(BF16), v_ref[pl.ds(off, blk), _hs(c[0])])
                carry[c] = (run + (cs[c][:, :1] + ls[c][:, :1]), acc)
            return carry

        init = (jnp.zeros((blk, 1), F32), jnp.zeros((blk, HEAD_DIM), F32))
        carry = {c: init for c in chains}
        for s in reversed(range(qb)):
            carry = step(base + s, carry, _chain_modes(s, qb))
        carry = _loop_blocks(base, qb, lambda j, c: step(j, c, ("f",) * qb), carry, reverse=True)
        for t, s in chains:
            o_ref[s * blk:(s + 1) * blk, _hs(t)] = carry[(t, s)][1]

    nh = N_SB_HEADS // hp
    return pl.pallas_call(
        body, out_shape=jax.ShapeDtypeStruct((seq, SB_W), F32), grid=(nh, nkb // qb),
        in_specs=[pl.BlockSpec((rows, hp * HEAD_DIM), lambda h, i: (i, h)),
                  pl.BlockSpec((seq, hp * HEAD_DIM), lambda h, i: (0, nh + h)),
                  pl.BlockSpec((seq, hp * HEAD_DIM), lambda h, i: (0, 2 * nh + h))],
        out_specs=pl.BlockSpec((rows, hp * HEAD_DIM), lambda h, i: (i, h)),
        name=name, compiler_params=_params("parallel", "arbitrary"),
    )(qkv, qkv, qkv)


SB_BWD_GROUP = 4


def _sb_bwd(qkv, out, dout, *, name):
    seq = qkv.shape[0]
    blk = min(ATT_BLOCK, seq)
    nkb = seq // blk
    qb = _pick(nkb, (4, 2, 1))
    rows = qb * blk
    scale = HEAD_DIM ** -0.5

    def body(q_ref, k_ref, v_ref, do_ref, o_ref, dq_ref, dk_out, dv_out, dk_ref, dv_ref):
        g = pl.program_id(1)
        base = g * qb

        @pl.when(g == 0)
        def _():
            dk_ref[...] = jnp.zeros_like(dk_ref)
            dv_ref[...] = jnp.zeros_like(dv_ref)

        qs = [q_ref[t * blk:(t + 1) * blk, :] for t in range(qb)]
        dos = [do_ref[t * blk:(t + 1) * blk, :].astype(BF16) for t in range(qb)]
        totals = [jnp.sum(dos[t].astype(F32) * o_ref[t * blk:(t + 1) * blk, :], axis=-1, keepdims=True)
                  for t in range(qb)]
        row = lax.broadcasted_iota(jnp.int32, (blk, blk), 0)
        col = lax.broadcasted_iota(jnp.int32, (blk, blk), 1)
        after = (row > col).astype(BF16)
        after2 = jnp.concatenate([after, after], axis=0)
        from_s = (row >= col).astype(BF16)
        from_s2 = jnp.concatenate([from_s, from_s], axis=0)
        causal = col < row

        def step(j, carry, modes):
            runs, rights, dqs = list(carry[0]), list(carry[1]), list(carry[2])
            off = pl.multiple_of(j * blk, blk)
            kb = k_ref[pl.ds(off, blk), :]
            vb = v_ref[pl.ds(off, blk), :]
            dv_inc = dk_inc = None
            for first in range(0, qb, SB_BWD_GROUP):
                act = [t for t in range(first, min(first + SB_BWD_GROUP, qb)) if modes[t]]
                zs, ls, sns = {}, {}, {}
                for t in act:
                    zs[t], w, l = _sb_terms(_dot_t(qs[t], kb))
                    sns[t] = pl.reciprocal(w, approx=True)
                    ls[t] = jnp.where(causal, l, 0.0) if modes[t] == "m" else l
                cs = {t: _split_dot(ls[t], after2) for t in act}
                das = {t: _dot_t(dos[t], vb) for t in act}
                abs_, des = {}, {}
                for t in act:
                    a = jnp.exp2(zs[t] - ls[t] - cs[t] - runs[t])
                    if modes[t] == "m":
                        a = jnp.where(causal, a, 0.0)
                    abs_[t] = a.astype(BF16)
                    des[t] = abs_[t].astype(F32) * das[t]
                sufs = {t: _split_dot(des[t], from_s2) for t in act}
                for t in act:
                    left = totals[t] - (sufs[t] + rights[t])
                    dz = (des[t] + left) * sns[t] - left
                    if modes[t] == "m":
                        dz = jnp.where(causal, dz, 0.0)
                    dzb = dz.astype(BF16)
                    dqs[t] = dqs[t] + _dot(dzb, kb)
                    inc_v = _tdot(abs_[t], dos[t])
                    inc_k = _tdot(dzb, qs[t])
                    dv_inc = inc_v if dv_inc is None else dv_inc + inc_v
                    dk_inc = inc_k if dk_inc is None else dk_inc + inc_k
                    runs[t] = runs[t] + (cs[t][:, :1] + ls[t][:, :1])
                    rights[t] = rights[t] + sufs[t][:, :1]
            dv_ref[pl.ds(off, blk), :] += dv_inc
            dk_ref[pl.ds(off, blk), :] += dk_inc
            return tuple(runs), tuple(rights), tuple(dqs)

        zero = (jnp.zeros((blk, 1), F32),) * qb
        carry = (zero, zero, (jnp.zeros((blk, HEAD_DIM), F32),) * qb)
        for s in reversed(range(qb)):
            carry = step(base + s, carry, _chain_modes(s, qb))
        carry = _loop_blocks(base, qb, lambda j, c: step(j, c, ("f",) * qb), carry, reverse=True)
        for t in range(qb):
            dq_ref[t * blk:(t + 1) * blk, :] = (carry[2][t] * scale).astype(BF16)

        @pl.when(g == pl.num_programs(1) - 1)
        def _():
            dk_out[...] = (dk_ref[...] * (1.0 / LOG2_E)).astype(BF16)
            dv_out[...] = dv_ref[...].astype(BF16)

    out_sd = jax.ShapeDtypeStruct((seq, SB_W), BF16)
    return pl.pallas_call(
        body, out_shape=[out_sd, out_sd, out_sd], grid=(N_SB_HEADS, nkb // qb),
        in_specs=[pl.BlockSpec((rows, HEAD_DIM), lambda h, i: (i, h)),
                  pl.BlockSpec((seq, HEAD_DIM), lambda h, i: (0, N_SB_HEADS + h)),
                  pl.BlockSpec((seq, HEAD_DIM), lambda h, i: (0, 2 * N_SB_HEADS + h)),
                  pl.BlockSpec((rows, HEAD_DIM), lambda h, i: (i, h)),
                  pl.BlockSpec((rows, HEAD_DIM), lambda h, i: (i, h))],
        out_specs=[pl.BlockSpec((rows, HEAD_DIM), lambda h, i: (i, h)),
                   pl.BlockSpec((seq, HEAD_DIM), lambda h, i: (0, h)),
                   pl.BlockSpec((seq, HEAD_DIM), lambda h, i: (0, h))],
        scratch_shapes=[pltpu.VMEM((seq, HEAD_DIM), F32), pltpu.VMEM((seq, HEAD_DIM), F32)],
        name=name, compiler_params=_params("parallel", "arbitrary"),
    )(qkv, qkv, qkv, dout, out)


MLA_SCALE = (HEAD_DIM + ROPE_DIM) ** -0.5
MLA_Q_SCALE = MLA_SCALE * LOG2_E


def _mla_fwd(q_cat, k_cat, v, *, name, hp=1):
    seq = q_cat.shape[0]
    blk = min(ATT_BLOCK, seq)
    nkb = seq // blk
    qb = _pick(nkb, (4, 2, 1))
    rows = qb * blk
    chains = [(t, s) for t in range(hp) for s in range(qb)]

    def body(q_ref, k_ref, v_ref, o_ref, lse_ref):
        base = pl.program_id(1) * qb
        qs = {(t, s): q_ref[s * blk:(s + 1) * blk, t * CAT_W:(t + 1) * CAT_W] for t, s in chains}
        row = lax.broadcasted_iota(jnp.int32, (blk, blk), 0)
        col = lax.broadcasted_iota(jnp.int32, (blk, blk), 1)
        causal = col <= row

        def step(j, carry, modes):
            off = pl.multiple_of(j * blk, blk)
            act = [c for c in chains if modes[c[1]]]
            ss = {c: _dot_t(qs[c], k_ref[pl.ds(off, blk), c[0] * CAT_W:(c[0] + 1) * CAT_W]) for c in act}
            carry = dict(carry)
            for c in act:
                m, l, acc = carry[c]
                s = ss[c]
                if modes[c[1]] == "m":
                    s = jnp.where(causal, s, -jnp.inf)
                m_new = jnp.maximum(m, jnp.max(s, axis=-1, keepdims=True))
                p = jnp.exp2(s - m_new)
                alpha = jnp.exp2(m - m_new)
                l = alpha * l + jnp.sum(p, axis=-1, keepdims=True)
                acc = alpha * acc + _dot(p.astype(BF16), v_ref[pl.ds(off, blk), _hs(c[0])])
                carry[c] = (m_new, l, acc)
            return carry

        init = (jnp.full((blk, 1), -jnp.inf, F32), jnp.zeros((blk, 1), F32),
                jnp.zeros((blk, HEAD_DIM), F32))
        carry = {c: init for c in chains}
        carry = _loop_blocks(base, qb, lambda j, c: step(j, c, ("f",) * qb), carry, reverse=False)
        for s in range(qb):
            carry = step(base + s, carry, _chain_modes(s, qb))
        for t, s in chains:
            m, l, acc = carry[(t, s)]
            o_ref[s * blk:(s + 1) * blk, _hs(t)] = acc / l
            lse_ref[s * blk:(s + 1) * blk, _hs(t)] = jnp.broadcast_to(
                (m + jnp.log2(l)) * (1.0 / LOG2_E), (blk, HEAD_DIM))

    out = jax.ShapeDtypeStruct((seq, MLA_W), F32)
    return pl.pallas_call(
        body, out_shape=[out, out], grid=(N_MLA_HEADS // hp, nkb // qb),
        in_specs=[pl.BlockSpec((rows, hp * CAT_W), lambda h, i: (i, h)),
                  pl.BlockSpec((seq, hp * CAT_W), lambda h, i: (0, h)),
                  pl.BlockSpec((seq, hp * HEAD_DIM), lambda h, i: (0, h))],
        out_specs=[pl.BlockSpec((rows, hp * HEAD_DIM), lambda h, i: (i, h)),
                   pl.BlockSpec((rows, hp * HEAD_DIM), lambda h, i: (i, h))],
        name=name, compiler_params=_params("parallel", "arbitrary"),
    )(q_cat, k_cat, v)


def _mla_bwd(q_cat, k_cat, v, out, lse, dout, *, name):
    seq = q_cat.shape[0]
    blk = min(ATT_BLOCK, seq)
    nkb = seq // blk
    qb = _pick(nkb, (4, 2, 1))
    rows = qb * blk

    def body(q_ref, k_ref, v_ref, o_ref, lse_ref, do_ref, dq_ref, dk_ref, dv_ref):
        g = pl.program_id(1)
        base = g * qb

        @pl.when(g == 0)
        def _():
            dk_ref[...] = jnp.zeros_like(dk_ref)
            dv_ref[...] = jnp.zeros_like(dv_ref)

        qs, dobs, deltas, lses = [], [], [], []
        for t in range(qb):
            rs = slice(t * blk, (t + 1) * blk)
            do = do_ref[rs, :]
            qs.append(q_ref[rs, :])
            dobs.append(do.astype(BF16))
            deltas.append(jnp.sum(do * o_ref[rs, :], axis=-1, keepdims=True))
            lses.append(lse_ref[rs, :1] * LOG2_E)
        row = lax.broadcasted_iota(jnp.int32, (blk, blk), 0)
        col = lax.broadcasted_iota(jnp.int32, (blk, blk), 1)
        causal = col <= row

        def step(j, dqs, modes):
            off = pl.multiple_of(j * blk, blk)
            kb = k_ref[pl.ds(off, blk), :]
            vb = v_ref[pl.ds(off, blk), :]
            act = [t for t in range(qb) if modes[t]]
            ss = {t: _dot_t(qs[t], kb) for t in act}
            dps = {t: _dot_t(dobs[t], vb) for t in act}
            dqs = list(dqs)
            dv_inc = dk_inc = None
            for t in act:
                p = jnp.exp2(ss[t] - lses[t])
                if modes[t] == "m":
                    p = jnp.where(causal, p, 0.0)
                ds = (p * (dps[t] - deltas[t])).astype(BF16)
                inc_v = _tdot(p.astype(BF16), dobs[t])
                inc_k = _tdot(ds, qs[t])
                dv_inc = inc_v if dv_inc is None else dv_inc + inc_v
                dk_inc = inc_k if dk_inc is None else dk_inc + inc_k
                dqs[t] = dqs[t] + _dot(ds, kb)
            dv_ref[pl.ds(off, blk), :] += dv_inc
            dk_ref[pl.ds(off, blk), :] += dk_inc
            return tuple(dqs)

        dqs = (jnp.zeros((blk, CAT_W), F32),) * qb
        dqs = _loop_blocks(base, qb, lambda j, c: step(j, c, ("f",) * qb), dqs, reverse=False)
        for s in range(qb):
            modes = tuple(None if t < s else ("m" if t == s else "f") for t in range(qb))
            dqs = step(base + s, dqs, modes)
        for t in range(qb):
            dq_ref[t * blk:(t + 1) * blk, :] = dqs[t] * MLA_SCALE

        @pl.when(g == pl.num_programs(1) - 1)
        def _():
            dk_ref[...] = dk_ref[...] * (1.0 / LOG2_E)

    return pl.pallas_call(
        body,
        out_shape=[jax.ShapeDtypeStruct((seq, N_MLA_HEADS * CAT_W), F32),
                   jax.ShapeDtypeStruct((seq, N_MLA_HEADS * CAT_W), F32),
                   jax.ShapeDtypeStruct((seq, MLA_W), F32)],
        grid=(N_MLA_HEADS, nkb // qb),
        in_specs=[pl.BlockSpec((rows, CAT_W), lambda h, i: (i, h)),
                  pl.BlockSpec((seq, CAT_W), lambda h, i: (0, h)),
                  pl.BlockSpec((seq, HEAD_DIM), lambda h, i: (0, h)),
                  pl.BlockSpec((rows, HEAD_DIM), lambda h, i: (i, h)),
                  pl.BlockSpec((rows, HEAD_DIM), lambda h, i: (i, h)),
                  pl.BlockSpec((rows, HEAD_DIM), lambda h, i: (i, h))],
        out_specs=[pl.BlockSpec((rows, CAT_W), lambda h, i: (i, h)),
                   pl.BlockSpec((seq, CAT_W), lambda h, i: (0, h)),
                   pl.BlockSpec((seq, HEAD_DIM), lambda h, i: (0, h))],
        name=name, compiler_params=_params("parallel", "arbitrary"),
    )(q_cat, k_cat, v, out, lse, dout)


def _local_step(x, mem, positions, target, w, g):
    seq = x.shape[0]
    inv_freq = jnp.power(ROPE_THETA, -jnp.arange(0, ROPE_DIM, 2, dtype=F32) / ROPE_DIM)
    ang = positions.astype(F32)[:, None] * inv_freq
    cos, sin = jnp.cos(ang), jnp.sin(ang)
    lane_pad = jnp.zeros((seq, HEAD_DIM - ROPE_DIM), F32)
    cos_t = jnp.concatenate([cos, cos, lane_pad], axis=1)
    sin_t = jnp.concatenate([-sin, sin, lane_pad], axis=1)
    gain_pad = jnp.zeros((1, HEAD_DIM - ROPE_DIM), F32)
    g_k_rope = jnp.concatenate([g["g_k_rope"], gain_pad], axis=1)
    g_q_rope = jnp.concatenate([g["b_g_q_rope"], gain_pad], axis=1)

    def norm_to_bf16(src, gain, name):
        def body(ins, outs, _):
            outs[0][...] = _rms(ins[0][...], ins[1][...]).astype(BF16)
        return _rowwise(body, seq, [(src, True), (gain, False)], [(src.shape[1], BF16)], name=name)[0]

    h_a = norm_to_bf16(x, g["a_norm"], "a_norm_fwd")
    qkv = _mm(h_a, w["a_in_qkv"], out_dtype=BF16, scale_cols=(SB_W, SB_Q_SCALE), name="a_in_qkv")
    gr = _mm(h_a, w["a_in_gate"], name="a_in_gate")
    sb = _sb_fwd(qkv, name="sb_fwd")
    mem0 = _mem_side_fwd(mem, g["mem_norm"][0:1], w["mem_kv"][0], g["g_mem_k"][0:1], tag="a")
    mixed_a = _mix_fwd(sb, gr, 0, mem0[2], mem0[3], g["g_mem_q"][0:1], name="a_mix_fwd")
    x1 = _mm(mixed_a, w["a_out"], res=x, name="a_out")

    def norms2_body(ins, outs, _):
        xv = ins[0][...]
        outs[0][...] = _rms(xv, ins[1][...]).astype(BF16)
        outs[1][...] = _rms(xv, ins[2][...]).astype(BF16)

    h_kv, h_b = _rowwise(norms2_body, seq, [(x1, True), (g["kv_norm"], False), (g["b_norm"], False)],
                         [(D_MODEL, BF16), (D_MODEL, BF16)], name="kv_b_norm_fwd")
    ckr = _mm(h_kv, w["dkv"], name="dkv")

    def ckr_body(ins, outs, _):
        ckr_ref, gc_ref, gr_ref, c_ref, s_ref = ins
        outs[0][...] = _rms(ckr_ref[:, :KV_LORA], gc_ref[...]).astype(BF16)
        kr = _rms(ckr_ref[:, KV_LORA:], gr_ref[...], n=ROPE_DIM)
        outs[1][...] = _rope(kr, c_ref[...], s_ref[...]).astype(BF16)

    c_n, k_r = _rowwise(ckr_body, seq,
                        [(ckr, True), (g["g_ckv"], False), (g_k_rope, False), (cos_t, True), (sin_t, True)],
                        [(KV_LORA, BF16), (HEAD_DIM, BF16)], name="ckv_prep_fwd")
    kv = _mm(c_n, w["ukv"], name="ukv")

    def kcat_body(ins, outs, _):
        kv_ref, kr_ref, gk_ref = ins
        kc_ref, v_ref = outs
        for h in range(N_MLA_HEADS):
            kc_ref[:, h * CAT_W:h * CAT_W + HEAD_DIM] = _rms(
                kv_ref[:, h * CAT_W:h * CAT_W + HEAD_DIM], gk_ref[...]).astype(BF16)
            kc_ref[:, h * CAT_W + HEAD_DIM:(h + 1) * CAT_W] = kr_ref[...]
            v_ref[:, _hs(h)] = kv_ref[:, h * CAT_W + HEAD_DIM:(h + 1) * CAT_W].astype(BF16)

    k_cat, v_mla = _rowwise(kcat_body, seq, [(kv, True), (k_r, True), (g["g_k_nope"], False)],
                            [(N_MLA_HEADS * CAT_W, BF16), (MLA_W, BF16)], name="k_prep_fwd")

    p2 = _mm(h_b, w["b_in"], name="b_in")

    def qlat_body(ins, outs, _):
        outs[0][...] = _rms(ins[0][:, :Q_LORA], ins[1][...]).astype(BF16)

    (q_l,) = _rowwise(qlat_body, seq, [(p2, Q_LORA), (g["b_g_q_lat"], False)], [(Q_LORA, BF16)],
                      name="q_lat_norm_fwd")
    q_up = _mm(q_l, w["uq"], name="uq")

    def qcat_body(ins, outs, _):
        q_ref, gn_ref, gr_ref, c_ref, s_ref = ins
        (o_ref,) = outs
        for h in range(N_MLA_HEADS):
            o_ref[:, h * CAT_W:h * CAT_W + HEAD_DIM] = (MLA_Q_SCALE * _rms(
                q_ref[:, h * CAT_W:h * CAT_W + HEAD_DIM], gn_ref[...])).astype(BF16)
            qr = _rms(q_ref[:, h * CAT_W + HEAD_DIM:(h + 1) * CAT_W], gr_ref[...], n=ROPE_DIM)
            o_ref[:, h * CAT_W + HEAD_DIM:(h + 1) * CAT_W] = (
                MLA_Q_SCALE * _rope(qr, c_ref[...], s_ref[...])).astype(BF16)

    (q_cat,) = _rowwise(qcat_body, seq,
                        [(q_up, True), (g["b_g_q_nope"], False), (g_q_rope, False), (cos_t, True), (sin_t, True)],
                        [(N_MLA_HEADS * CAT_W, BF16)], name="q_prep_fwd")
    att, lse = _mla_fwd(q_cat, k_cat, v_mla, name="mla_fwd")
    mem1 = _mem_side_fwd(mem, g["mem_norm"][1:2], w["mem_kv"][1], g["g_mem_k"][1:2], tag="b")
    mixed_b = _mix_fwd(att, p2, Q_LORA, mem1[2], mem1[3], g["g_mem_q"][1:2], name="b_mix_fwd")
    y = _mm(mixed_b, w["b_out"], res=x1, name="b_out")

    def loss_body(ins, outs, accs):
        diff = ins[0][...] - ins[1][...]
        outs[0][...] = diff / D_MODEL
        col = jnp.sum(diff * diff, axis=0, keepdims=True)
        part = col[:, :HEAD_DIM]
        for c in range(1, D_MODEL // HEAD_DIM):
            part = part + col[:, _hs(c)]
        accs[0][...] += part * (0.5 / D_MODEL)

    dy, loss_part = _rowwise(loss_body, seq, [(y, True), (target, True)], [(D_MODEL, F32)],
                             [((1, HEAD_DIM), F32)], name="loss")

    gw, gg = {}, {}
    dmixed_b = _mm(dy, w["b_out"], tb=True, name="b_out_dx")
    gw["b_out"] = _mm(mixed_b, dy, ta=True, out_dtype=BF16, name="b_out_dw")
    datt, dgate_b, dmk1, dmv1, gq1 = _mix_bwd(dmixed_b, att, p2, Q_LORA, mem1[2], mem1[3],
                                              g["g_mem_q"][1:2], name="b_mix_bwd")
    dq_cat, dk_cat, dv_mla = _mla_bwd(q_cat, k_cat, v_mla, att, lse, datt, name="mla_bwd")

    def qcat_bwd_body(ins, outs, accs):
        q_ref, dq_ref, gn_ref, gr_ref, c_ref, s_ref = ins
        (o_ref,) = outs
        dgn_ref, dgr_ref = accs
        for h in range(N_MLA_HEADS):
            dx, dg = _rms_bwd(q_ref[:, h * CAT_W:h * CAT_W + HEAD_DIM], gn_ref[...],
                              dq_ref[:, h * CAT_W:h * CAT_W + HEAD_DIM])
            o_ref[:, h * CAT_W:h * CAT_W + HEAD_DIM] = dx.astype(BF16)
            dgn_ref[...] += dg
            dn = _rope_bwd(dq_ref[:, h * CAT_W + HEAD_DIM:(h + 1) * CAT_W], c_ref[...], s_ref[...])
            dx, dg = _rms_bwd(q_ref[:, h * CAT_W + HEAD_DIM:(h + 1) * CAT_W], gr_ref[...], dn, n=ROPE_DIM)
            o_ref[:, h * CAT_W + HEAD_DIM:(h + 1) * CAT_W] = dx.astype(BF16)
            dgr_ref[...] += dg

    dq_up, gg["b_g_q_nope"], dgqr = _rowwise(
        qcat_bwd_body, seq,
        [(q_up, True), (dq_cat, True), (g["b_g_q_nope"], False), (g_q_rope, False), (cos_t, True), (sin_t, True)],
        [(N_MLA_HEADS * CAT_W, BF16)], [((1, HEAD_DIM), F32), ((1, HEAD_DIM), F32)], name="q_prep_bwd")
    gg["b_g_q_rope"] = dgqr
    dq_l = _mm(dq_up, w["uq"], tb=True, name="uq_dx")
    gw["uq"] = _mm(q_l, dq_up, ta=True, out_dtype=BF16, n_split=N_CHIPS, name="uq_dw")

    def qlat_bwd_body(ins, outs, accs):
        p2_ref, dql_ref, dgate_ref, gl_ref = ins
        dx, dg = _rms_bwd(p2_ref[:, :Q_LORA], gl_ref[...], dql_ref[...])
        outs[0][:, :Q_LORA] = dx.astype(BF16)
        outs[0][:, Q_LORA:] = dgate_ref[...]
        accs[0][...] += dg

    dp2, gg["b_g_q_lat"] = _rowwise(
        qlat_bwd_body, seq, [(p2, Q_LORA), (dq_l, True), (dgate_b, True), (g["b_g_q_lat"], False)],
        [(Q_LORA + GATE_W, BF16)], [((1, Q_LORA), F32)], name="q_lat_norm_bwd")
    dh_b = _mm(dp2, w["b_in"], tb=True, name="b_in_dx")
    gw["b_in"] = _mm(h_b, dp2, ta=True, out_dtype=BF16, n_split=N_CHIPS, name="b_in_dw")

    def kcat_bwd_body(ins, outs, accs):
        kv_ref, dkc_ref, dv_ref, gk_ref = ins
        dkv_ref, dkr_ref = outs
        (dgk_ref,) = accs
        dkr = jnp.zeros(dkr_ref.shape, F32)
        for h in range(N_MLA_HEADS):
            dx, dg = _rms_bwd(kv_ref[:, h * CAT_W:h * CAT_W + HEAD_DIM], gk_ref[...],
                              dkc_ref[:, h * CAT_W:h * CAT_W + HEAD_DIM])
            dkv_ref[:, h * CAT_W:h * CAT_W + HEAD_DIM] = dx.astype(BF16)
            dgk_ref[...] += dg
            dkv_ref[:, h * CAT_W + HEAD_DIM:(h + 1) * CAT_W] = dv_ref[:, _hs(h)].astype(BF16)
            dkr = dkr + dkc_ref[:, h * CAT_W + HEAD_DIM:(h + 1) * CAT_W]
        dkr_ref[...] = dkr

    dkv, dk_r, gg["g_k_nope"] = _rowwise(
        kcat_bwd_body, seq, [(kv, True), (dk_cat, True), (dv_mla, True), (g["g_k_nope"], False)],
        [(N_MLA_HEADS * CAT_W, BF16), (HEAD_DIM, F32)], [((1, HEAD_DIM), F32)], name="k_prep_bwd")
    dc_n = _mm(dkv, w["ukv"], tb=True, name="ukv_dx")
    gw["ukv"] = _mm(c_n, dkv, ta=True, out_dtype=BF16, n_split=N_CHIPS, name="ukv_dw")

    def ckr_bwd_body(ins, outs, accs):
        ckr_ref, dcn_ref, dkr_ref, gc_ref, gr_ref, c_ref, s_ref = ins
        dx, dg = _rms_bwd(ckr_ref[:, :KV_LORA], gc_ref[...], dcn_ref[...])
        outs[0][:, :KV_LORA] = dx.astype(BF16)
        accs[0][...] += dg
        dn = _rope_bwd(dkr_ref[...], c_ref[...], s_ref[...])
        dx, dg = _rms_bwd(ckr_ref[:, KV_LORA:], gr_ref[...], dn, n=ROPE_DIM)
        outs[0][:, KV_LORA:] = dx.astype(BF16)
        accs[1][...] += dg

    dckr, gg["g_ckv"], gg["g_k_rope"] = _rowwise(
        ckr_bwd_body, seq,
        [(ckr, True), (dc_n, True), (dk_r, True), (g["g_ckv"], False), (g_k_rope, False),
         (cos_t, True), (sin_t, True)],
        [(KV_LORA + HEAD_DIM, BF16)], [((1, KV_LORA), F32), ((1, HEAD_DIM), F32)], name="ckv_prep_bwd")
    dh_kv = _mm(dckr, w["dkv"], tb=True, name="dkv_dx")
    gw["dkv"] = _mm(h_kv, dckr, ta=True, out_dtype=BF16, name="dkv_dw")

    def norms2_bwd_body(ins, outs, accs):
        x_ref, dy_ref, dhk_ref, dhb_ref, gk_ref, gb_ref = ins
        xv = x_ref[...]
        dxk, dgk = _rms_bwd(xv, gk_ref[...], dhk_ref[...])
        dxb, dgb = _rms_bwd(xv, gb_ref[...], dhb_ref[...])
        outs[0][...] = dy_ref[...] + dxk + dxb
        accs[0][...] += dgk
        accs[1][...] += dgb

    dx1, gg["kv_norm"], gg["b_norm"] = _rowwise(
        norms2_bwd_body, seq,
        [(x1, True), (dy, True), (dh_kv, True), (dh_b, True), (g["kv_norm"], False), (g["b_norm"], False)],
        [(D_MODEL, F32)], [((1, D_MODEL), F32), ((1, D_MODEL), F32)], name="kv_b_norm_bwd")

    dmixed_a = _mm(dx1, w["a_out"], tb=True, name="a_out_dx")
    gw["a_out"] = _mm(mixed_a, dx1, ta=True, out_dtype=BF16, name="a_out_dw")
    dsb, dgate_a, dmk0, dmv0, gq0 = _mix_bwd(dmixed_a, sb, gr, 0, mem0[2], mem0[3],
                                             g["g_mem_q"][0:1], name="a_mix_bwd")
    dq, dk, dv = _sb_bwd(qkv, sb, dsb, name="sb_bwd")
    dp_a = jnp.concatenate([dq, dk, dv, dgate_a], axis=1)
    dh_a = _mm(dp_a, w["a_in"], tb=True, name="a_in_dx")
    gw["a_in"] = _mm(h_a, dp_a, ta=True, out_dtype=BF16, n_split=N_CHIPS, name="a_in_dw")

    def norm_a_bwd_body(ins, outs, accs):
        dx, dg = _rms_bwd(ins[0][...], ins[3][...], ins[2][...])
        outs[0][...] = ins[1][...] + dx
        accs[0][...] += dg

    grad_x, gg["a_norm"] = _rowwise(
        norm_a_bwd_body, seq, [(x, True), (dx1, True), (dh_a, True), (g["a_norm"], False)],
        [(D_MODEL, F32)], [((1, D_MODEL), F32)], name="a_norm_bwd")

    dw0, dgn0, dgk0 = _mem_side_bwd(mem, g["mem_norm"][0:1], w["mem_kv"][0], g["g_mem_k"][0:1],
                                    mem0[0], mem0[1], dmk0, dmv0, tag="a")
    dw1, dgn1, dgk1 = _mem_side_bwd(mem, g["mem_norm"][1:2], w["mem_kv"][1], g["g_mem_k"][1:2],
                                    mem1[0], mem1[1], dmk1, dmv1, tag="b")
    gw["mem_kv"] = (dw0, dw1)
    gg["mem_norm"] = jnp.concatenate([dgn0, dgn1], axis=0)
    gg["g_mem_q"] = jnp.concatenate([gq0, gq1], axis=0)
    gg["g_mem_k"] = jnp.concatenate([dgk0, dgk1], axis=0)
    return loss_part, grad_x, gw, gg


HBM_SPEC = pl.BlockSpec(memory_space=pl.ANY)


def _other_chips():
    x, y = lax.axis_index("x"), lax.axis_index("y")
    return [(1 - x, y), (x, 1 - y), (1 - x, 1 - y)]


def _allgather_chips(shards):
    n = len(shards)
    split = [s.shape[0] % 32 == 0 for s in shards]

    def body(*refs):
        ins, outs = refs[:n], refs[n:2 * n]
        send, recv, fsend, frecv = refs[2 * n:]
        x, y, c = lax.axis_index("x"), lax.axis_index("y"), lax.axis_index("c")
        me = 2 * x + y
        chips = _other_chips()

        def part(ref, wi):
            if not split[wi]:
                return ref
            half = shards[wi].shape[0] // 2
            return ref.at[pl.ds(pl.multiple_of(c * half, 16), half)]

        def ici(wi, k, src_chip, to):
            return pltpu.make_async_remote_copy(
                src_ref=part(ins[wi], wi), dst_ref=part(outs[wi].at[src_chip], wi),
                send_sem=send.at[wi, k], recv_sem=recv.at[wi, k], device_id=to, device_id_type=MESH)

        def d2d(wi, k, src_chip):
            rows = part(outs[wi].at[src_chip], wi)
            return pltpu.make_async_remote_copy(
                src_ref=rows, dst_ref=rows, send_sem=fsend.at[wi, k], recv_sem=frecv.at[wi, k],
                device_id=(x, y, 1 - c), device_id_type=MESH)

        for wi in range(n):
            for k, (tx, ty) in enumerate(chips):
                ici(wi, k, me, (tx, ty, c)).start()
        for wi in range(n):
            for k, (tx, ty) in enumerate(chips):
                landed = ici(wi, k, 2 * tx + ty, (tx, ty, c))
                landed.wait_recv()
                if split[wi]:
                    d2d(wi, k, 2 * tx + ty).start()
        for wi in range(n):
            for k, (tx, ty) in enumerate(chips):
                ici(wi, k, me, (tx, ty, c)).wait_send()
                if split[wi]:
                    fwd = d2d(wi, k, 2 * tx + ty)
                    fwd.wait_send()
                    fwd.wait_recv()

    return pl.pallas_call(
        body, out_shape=[jax.ShapeDtypeStruct((N_CHIPS,) + s.shape, s.dtype) for s in shards],
        in_specs=[HBM_SPEC] * n, out_specs=[HBM_SPEC] * n,
        scratch_shapes=[pltpu.SemaphoreType.DMA((n, 3)), pltpu.SemaphoreType.DMA((n, 3)),
                        pltpu.SemaphoreType.DMA((n, 3)), pltpu.SemaphoreType.DMA((n, 3))],
        name="allgather_weights",
    )(*shards)


def _scatter_to_chips(grads):
    n = len(grads)

    def body(*refs):
        ins, outs = refs[:n], refs[n:2 * n]
        send, recv = refs[2 * n:]
        c = lax.axis_index("c")
        copies = []
        for wi in range(n):
            for k, (tx, ty) in enumerate(_other_chips()):
                cp = pltpu.make_async_remote_copy(
                    src_ref=ins[wi].at[2 * tx + ty], dst_ref=outs[wi].at[k], send_sem=send.at[wi, k],
                    recv_sem=recv.at[wi, k], device_id=(tx, ty, c), device_id_type=MESH)
                cp.start()
                copies.append(cp)
        for cp in copies:
            cp.wait()

    return pl.pallas_call(
        body, out_shape=[jax.ShapeDtypeStruct((3,) + s.shape[1:], s.dtype) for s in grads],
        in_specs=[HBM_SPEC] * n, out_specs=[HBM_SPEC] * n,
        scratch_shapes=[pltpu.SemaphoreType.DMA((n, 3)), pltpu.SemaphoreType.DMA((n, 3))],
        name="scatter_grads",
    )(*grads)


def _halve_with_sibling(grads):
    n = len(grads)
    n_slots = grads[0].shape[0]

    def body(*refs):
        ins, got = refs[:n], refs[n:2 * n]
        send, recv = refs[2 * n:]
        c = lax.axis_index("c")
        sib = (lax.axis_index("x"), lax.axis_index("y"), 1 - c)
        copies = []
        for wi in range(n):
            half = grads[wi].shape[1] // 2
            for s in range(n_slots):
                theirs = ins[wi].at[s, pl.ds(pl.multiple_of((1 - c) * half, 16), half)]
                give = pltpu.make_async_remote_copy(
                    src_ref=theirs, dst_ref=got[wi].at[s], send_sem=send.at[wi, s], recv_sem=recv.at[wi, s],
                    device_id=sib, device_id_type=MESH)
                give.start()
                copies.append(give)
        for cp in copies:
            cp.wait()

    halves = [jax.ShapeDtypeStruct((s.shape[0], s.shape[1] // 2) + s.shape[2:], s.dtype) for s in grads]
    return pl.pallas_call(
        body, out_shape=halves, in_specs=[HBM_SPEC] * n, out_specs=[HBM_SPEC] * n,
        scratch_shapes=[pltpu.SemaphoreType.DMA((n, n_slots)), pltpu.SemaphoreType.DMA((n, n_slots))],
        name="halve_grads_with_sibling",
    )(*grads)


def _swap_with_sibling(parts):
    n = len(parts)

    def body(*refs):
        ins, outs = refs[:n], refs[n:2 * n]
        send, recv = refs[2 * n:]
        sib = (lax.axis_index("x"), lax.axis_index("y"), 1 - lax.axis_index("c"))
        copies = []
        for wi in range(n):
            cp = pltpu.make_async_remote_copy(
                src_ref=ins[wi], dst_ref=outs[wi], send_sem=send.at[wi], recv_sem=recv.at[wi],
                device_id=sib, device_id_type=MESH)
            cp.start()
            copies.append(cp)
        for cp in copies:
            cp.wait()

    return pl.pallas_call(
        body, out_shape=[jax.ShapeDtypeStruct(s.shape, s.dtype) for s in parts],
        in_specs=[HBM_SPEC] * n, out_specs=[HBM_SPEC] * n,
        scratch_shapes=[pltpu.SemaphoreType.DMA((n,)), pltpu.SemaphoreType.DMA((n,))],
        name="swap_grad_halves",
    )(*parts)


def _allreduce_small(vec, loss_row):
    rows = vec.shape[0]

    def body(v_ref, o_ref, buf, send, recv):
        x, y, c = lax.axis_index("x"), lax.axis_index("y"), lax.axis_index("c")
        me = 4 * x + 2 * y + c
        buf[me] = v_ref[...]
        copies = []
        for r in range(1, N_DEV):
            peer = (x ^ ((r >> 2) & 1), y ^ ((r >> 1) & 1), c ^ (r & 1))
            cp = pltpu.make_async_remote_copy(
                src_ref=v_ref, dst_ref=buf.at[me], send_sem=send.at[r - 1], recv_sem=recv.at[r - 1],
                device_id=peer, device_id_type=MESH)
            cp.start()
            copies.append(cp)
        for cp in copies:
            cp.wait()
        total = buf[0]
        for d in range(1, N_DEV):
            total = total + buf[d]
        o_ref[...] = total
        o_ref[loss_row:loss_row + 1, :] = jnp.broadcast_to(
            jnp.sum(total[loss_row:loss_row + 1, :], axis=-1, keepdims=True), (1, HEAD_DIM))

    return pl.pallas_call(
        body, out_shape=jax.ShapeDtypeStruct(vec.shape, F32),
        in_specs=[pl.BlockSpec(memory_space=pltpu.VMEM)], out_specs=pl.BlockSpec(memory_space=pltpu.VMEM),
        scratch_shapes=[pltpu.VMEM((N_DEV, rows, HEAD_DIM), F32),
                        pltpu.SemaphoreType.DMA((N_DEV - 1,)), pltpu.SemaphoreType.DMA((N_DEV - 1,))],
        name="allreduce_gains",
    )(vec)


def _pair_sum(grads, got, *, name):
    slots, rows, width = got.shape
    blk = _pick(rows, (256, 128, 64, 32, 16))
    nbh = rows // blk

    def body(lo_ref, hi_ref, got_ref, o_ref):
        mine = jnp.where(lax.axis_index("c") == 0, lo_ref[...], hi_ref[...])
        o_ref[...] = (mine.astype(F32) + got_ref[...].astype(F32)).astype(BF16)

    spec = pl.BlockSpec((None, blk, width), lambda s, i: (s, i, 0))
    return pl.pallas_call(
        body, out_shape=jax.ShapeDtypeStruct(got.shape, BF16), grid=(slots, nbh),
        in_specs=[spec, pl.BlockSpec((None, blk, width), lambda s, i: (s, nbh + i, 0)), spec],
        out_specs=spec, name=name, compiler_params=_params("parallel", "parallel"),
    )(grads, grads, got)


def _sum_slots(recv, chip_sum, *, name):
    _, rows, width = recv.shape
    blk = _pick(rows, (256, 128, 64, 32, 16, 8))

    def body(r_ref, p_ref, o_ref):
        me = 2 * lax.axis_index("x") + lax.axis_index("y")
        own = jnp.where(me < 2, jnp.where(me == 0, p_ref[0], p_ref[1]), jnp.where(me == 2, p_ref[2], p_ref[3]))
        o_ref[...] = ((own.astype(F32) + r_ref[0].astype(F32)) + r_ref[1].astype(F32)) + r_ref[2].astype(F32)

    return pl.pallas_call(
        body, out_shape=jax.ShapeDtypeStruct((rows, width), F32), grid=(rows // blk,),
        in_specs=[pl.BlockSpec((3, blk, width), lambda i: (0, i, 0)),
                  pl.BlockSpec((N_CHIPS, blk, width), lambda i: (0, i, 0))],
        out_specs=pl.BlockSpec((blk, width), lambda i: (i, 0)),
        name=name, compiler_params=_params("parallel"),
    )(recv, chip_sum)


def _adamw(wgt, grad, m, v, *, name, halves=None):
    rows, width = wgt.shape
    blk = _pick(rows // 2 if halves else rows, (256, 128, 64, 32, 16, 8))
    nbh = rows // 2 // blk

    def body(*refs):
        if halves:
            w_ref, mine_ref, theirs_ref, m_ref, v_ref, g_out, d_out, m_out, v_out = refs
            grad_v = jnp.where(pl.program_id(0) // nbh == lax.axis_index("c"), mine_ref[...], theirs_ref[...])
        else:
            w_ref, g_ref, m_ref, v_ref, g_out, d_out, m_out, v_out = refs
            grad_v = g_ref[...]
        m_new = ADAM_B1 * m_ref[...] + (1.0 - ADAM_B1) * grad_v
        v_new = ADAM_B2 * v_ref[...] + (1.0 - ADAM_B2) * (grad_v * grad_v)
        m_hat = m_new / (1.0 - ADAM_B1 ** ADAM_STEP)
        v_hat = v_new / (1.0 - ADAM_B2 ** ADAM_STEP)
        g_out[...] = grad_v
        d_out[...] = -ADAM_LR * (m_hat / (jnp.sqrt(v_hat) + ADAM_EPS) + ADAM_WD * w_ref[...])
        m_out[...] = m_new
        v_out[...] = v_new

    spec = pl.BlockSpec((blk, width), lambda i: (i, 0))
    half_spec = pl.BlockSpec((blk, width), lambda i: (i % nbh, 0))
    g_specs, g_args = ([half_spec, half_spec], list(halves)) if halves else ([spec], [grad])
    out = jax.ShapeDtypeStruct((rows, width), F32)
    return pl.pallas_call(
        body, out_shape=[out] * 4, grid=(rows // blk,), in_specs=[spec] + g_specs + [spec, spec],
        out_specs=[spec] * 4, name=name, compiler_params=_params("parallel"),
    )(wgt, *g_args, m, v)


_SMALL = (("a_norm", 2048), ("kv_norm", 2048), ("g_ckv", 512), ("g_k_nope", 128), ("g_k_rope", 64),
          ("b_norm", 2048), ("b_g_q_lat", 512), ("b_g_q_nope", 128), ("b_g_q_rope", 64),
          ("mem_norm", 4096), ("g_mem_q", 256), ("g_mem_k", 256))


def _lanes(n):
    return -(-n // HEAD_DIM) * HEAD_DIM


def _pack_rows(pieces, pad_rows_to=8):
    flat = jnp.concatenate(pieces, axis=1)
    rows = flat.shape[1] // HEAD_DIM
    pad = (-rows) % pad_rows_to
    if pad:
        flat = jnp.concatenate([flat, jnp.zeros((1, pad * HEAD_DIM), F32)], axis=1)
    return flat.reshape(rows + pad, HEAD_DIM)


def _pad_lanes(a):
    a = a.reshape(1, -1)
    pad = _lanes(a.shape[1]) - a.shape[1]
    if pad:
        a = jnp.concatenate([a, jnp.zeros((1, pad), F32)], axis=1)
    return a


def kernel(x, mem, positions, a_norm, a_w_in, a_w_out, kv_norm, w_dkv, g_ckv, w_ukv, g_k_nope, g_k_rope, b_norm, b_w_in, b_g_q_lat, b_w_uq, b_g_q_nope, b_g_q_rope, b_w_out, mem_norm, w_mem_kv, g_mem_q, g_mem_k, loss_target, m_a_norm, m_a_w_in, m_a_w_out, m_kv_norm, m_w_dkv, m_g_ckv, m_w_ukv, m_g_k_nope, m_g_k_rope, m_b_norm, m_b_w_in, m_b_g_q_lat, m_b_w_uq, m_b_g_q_nope, m_b_g_q_rope, m_b_w_out, m_mem_norm, m_w_mem_kv, m_g_mem_q, m_g_mem_k, v_a_norm, v_a_w_in, v_a_w_out, v_kv_norm, v_w_dkv, v_g_ckv, v_w_ukv, v_g_k_nope, v_g_k_rope, v_b_norm, v_b_w_in, v_b_g_q_lat, v_b_w_uq, v_b_g_q_nope, v_b_g_q_rope, v_b_w_out, v_mem_norm, v_w_mem_kv, v_g_mem_q, v_g_mem_k):
    chip = 2 * lax.axis_index("x") + lax.axis_index("y")
    rows_dkv = D_MODEL // N_CHIPS
    heads_per_chip = N_MLA_HEADS // N_CHIPS
    qk_w = HEAD_DIM + ROPE_DIM

    big = {"a_in": a_w_in[0], "a_out": a_w_out[0], "dkv": w_dkv, "ukv": w_ukv, "b_in": b_w_in[0],
           "uq": b_w_uq[0], "b_out": b_w_out[0], "mem_kv": w_mem_kv.reshape(2 * rows_dkv, 2 * MEM_W)}
    big_m = {"a_in": m_a_w_in[0], "a_out": m_a_w_out[0], "dkv": m_w_dkv, "ukv": m_w_ukv, "b_in": m_b_w_in[0],
             "uq": m_b_w_uq[0], "b_out": m_b_w_out[0], "mem_kv": m_w_mem_kv.reshape(2 * rows_dkv, 2 * MEM_W)}
    big_v = {"a_in": v_a_w_in[0], "a_out": v_a_w_out[0], "dkv": v_w_dkv, "ukv": v_w_ukv, "b_in": v_b_w_in[0],
             "uq": v_b_w_uq[0], "b_out": v_b_w_out[0], "mem_kv": v_w_mem_kv.reshape(2 * rows_dkv, 2 * MEM_W)}
    names = list(big)
    own_shards = [big[n].astype(BF16) for n in names] + [a_norm]
    gathered = _allgather_chips(own_shards)
    gathered = [lax.dynamic_update_slice(g, s[None], (chip,) + (0,) * s.ndim)
                for g, s in zip(gathered, own_shards)]
    st = dict(zip(names, gathered[:-1]))
    a_in_full = st["a_in"].transpose(1, 0, 2).reshape(D_MODEL, QKV_W + GATE_W)
    uq = st["uq"].reshape(N_CHIPS, Q_LORA, heads_per_chip, qk_w)
    uq = jnp.pad(uq, ((0, 0), (0, 0), (0, 0), (0, CAT_W - qk_w)))
    w = {
        "a_in": a_in_full,
        "a_in_qkv": a_in_full[:, :QKV_W],
        "a_in_gate": a_in_full[:, QKV_W:],
        "a_out": st["a_out"].reshape(D_MODEL, D_MODEL),
        "dkv": jnp.pad(st["dkv"].reshape(D_MODEL, KV_LORA + ROPE_DIM), ((0, 0), (0, HEAD_DIM - ROPE_DIM))),
        "ukv": st["ukv"].transpose(1, 0, 2).reshape(KV_LORA, N_MLA_HEADS * CAT_W),
        "b_in": st["b_in"].transpose(1, 0, 2).reshape(D_MODEL, Q_LORA + GATE_W),
        "uq": uq.transpose(1, 0, 2, 3).reshape(Q_LORA, N_MLA_HEADS * CAT_W),
        "b_out": st["b_out"].reshape(D_MODEL, D_MODEL),
        "mem_kv": st["mem_kv"].reshape(N_CHIPS, 2, rows_dkv, 2 * MEM_W).transpose(1, 0, 2, 3).reshape(
            2, D_MODEL, 2 * MEM_W),
    }
    gains = {
        "a_norm": gathered[-1].reshape(1, D_MODEL), "kv_norm": kv_norm.reshape(1, -1),
        "g_ckv": g_ckv.reshape(1, -1), "g_k_nope": g_k_nope.reshape(1, -1), "g_k_rope": g_k_rope.reshape(1, -1),
        "b_norm": b_norm, "b_g_q_lat": b_g_q_lat, "b_g_q_nope": b_g_q_nope, "b_g_q_rope": b_g_q_rope,
        "mem_norm": mem_norm, "g_mem_q": g_mem_q, "g_mem_k": g_mem_k,
    }

    loss_part, grad_x, gw, gg = _local_step(x[0], mem[0], positions[0], loss_target[0], w, gains)

    stacked = {
        "a_in": gw["a_in"],
        "a_out": gw["a_out"].reshape(N_CHIPS, rows_dkv, D_MODEL),
        "dkv": gw["dkv"][:, :KV_LORA + ROPE_DIM].reshape(N_CHIPS, rows_dkv, KV_LORA + ROPE_DIM),
        "ukv": gw["ukv"],
        "b_in": gw["b_in"],
        "uq": gw["uq"].reshape(N_CHIPS, Q_LORA, heads_per_chip, CAT_W)[..., :qk_w].reshape(
            N_CHIPS, Q_LORA, heads_per_chip * qk_w),
        "b_out": gw["b_out"].reshape(N_CHIPS, rows_dkv, D_MODEL),
        "mem_kv": jnp.stack([gw["mem_kv"][0].reshape(N_CHIPS, rows_dkv, 2 * MEM_W),
                             gw["mem_kv"][1].reshape(N_CHIPS, rows_dkv, 2 * MEM_W)], axis=1).reshape(
            N_CHIPS, 2 * rows_dkv, 2 * MEM_W),
    }
    got = _halve_with_sibling([stacked[n] for n in names])
    chip_sum = [_pair_sum(stacked[n], g, name=f"pair_sum_{n}") for n, g in zip(names, got)]
    received = _scatter_to_chips(chip_sum)
    half_total = [_sum_slots(r, p, name=f"sum_slots_{n}") for n, r, p in zip(names, received, chip_sum)]
    sibling_half = _swap_with_sibling(half_total)
    big_out = {}
    for n, mine, theirs in zip(names, half_total, sibling_half):
        big_out[n] = _adamw(big[n], None, big_m[n], big_v[n], halves=(mine, theirs), name=f"adamw_{n}")

    pieces = [_pad_lanes(gg[n]) if n not in ("g_k_rope", "b_g_q_rope") else gg[n] for n, _ in _SMALL]
    pieces.append(loss_part)
    loss_row = sum(_lanes(size) for _, size in _SMALL) // HEAD_DIM
    summed = _allreduce_small(_pack_rows(pieces), loss_row)
    flat = summed.reshape(1, -1)
    small_g, off = {}, 0
    for n, size in _SMALL:
        small_g[n] = flat[:, off:off + size]
        off += _lanes(size)
    loss = flat[0, off]
    small_g["a_norm"] = lax.dynamic_slice(small_g["a_norm"], (0, chip * rows_dkv), (1, rows_dkv))

    small_w = {"a_norm": a_norm, "kv_norm": kv_norm, "g_ckv": g_ckv, "g_k_nope": g_k_nope, "g_k_rope": g_k_rope,
               "b_norm": b_norm, "b_g_q_lat": b_g_q_lat, "b_g_q_nope": b_g_q_nope, "b_g_q_rope": b_g_q_rope,
               "mem_norm": mem_norm, "g_mem_q": g_mem_q, "g_mem_k": g_mem_k}
    small_m = {"a_norm": m_a_norm, "kv_norm": m_kv_norm, "g_ckv": m_g_ckv, "g_k_nope": m_g_k_nope,
               "g_k_rope": m_g_k_rope, "b_norm": m_b_norm, "b_g_q_lat": m_b_g_q_lat, "b_g_q_nope": m_b_g_q_nope,
               "b_g_q_rope": m_b_g_q_rope, "mem_norm": m_mem_norm, "g_mem_q": m_g_mem_q, "g_mem_k": m_g_mem_k}
    small_v = {"a_norm": v_a_norm, "kv_norm": v_kv_norm, "g_ckv": v_g_ckv, "g_k_nope": v_g_k_nope,
               "g_k_rope": v_g_k_rope, "b_norm": v_b_norm, "b_g_q_lat": v_b_g_q_lat, "b_g_q_nope": v_b_g_q_nope,
               "b_g_q_rope": v_b_g_q_rope, "mem_norm": v_mem_norm, "g_mem_q": v_g_mem_q, "g_mem_k": v_g_mem_k}
    snames = [n for n, _ in _SMALL]
    packs = [_pack_rows([_pad_lanes(src[n]) for n in snames])
             for src in (small_w, small_g, small_m, small_v)]
    small_res = _adamw(packs[0], packs[1], packs[2], packs[3], name="adamw_gains")
    small_out = {n: [] for n in snames}
    for res in small_res:
        flat_r = res.reshape(1, -1)
        off = 0
        for n in snames:
            size = small_w[n].size
            small_out[n].append(flat_r[:, off:off + size].reshape(small_w[n].shape))
            off += _lanes(size)

    big_names = {"a_w_in": ("a_in", a_w_in), "a_w_out": ("a_out", a_w_out), "w_dkv": ("dkv", w_dkv),
                 "w_ukv": ("ukv", w_ukv), "b_w_in": ("b_in", b_w_in), "b_w_uq": ("uq", b_w_uq),
                 "b_w_out": ("b_out", b_w_out), "w_mem_kv": ("mem_kv", w_mem_kv)}
    order = ["a_norm", "a_w_in", "a_w_out", "kv_norm", "w_dkv", "g_ckv", "w_ukv", "g_k_nope", "g_k_rope",
             "b_norm", "b_w_in", "b_g_q_lat", "b_w_uq", "b_g_q_nope", "b_g_q_rope", "b_w_out", "mem_norm",
             "w_mem_kv", "g_mem_q", "g_mem_k"]
    groups = [[], [], [], []]
    for n in order:
        if n in big_names:
            key, ref_arr = big_names[n]
            for t in range(4):
                groups[t].append(big_out[key][t].reshape(ref_arr.shape))
        else:
            for t in range(4):
                groups[t].append(small_out[n][t])
    return (loss, grad_x[None], *groups[0], *groups[1], *groups[2], *groups[3])
```

```python
import jax
import jax.numpy as jnp
from jax import lax
from jax.experimental import pallas as pl
from jax.experimental.pallas import tpu as pltpu

F32 = jnp.float32
BF16 = jnp.bfloat16
MESH = pl.DeviceIdType.MESH

D_MODEL = 2048
HEAD_DIM = 128
N_SB_HEADS = 12
N_MEM_HEADS = 4
N_MLA_HEADS = 12
MEM_LEN = 256
Q_LORA = 512
KV_LORA = 512
ROPE_DIM = 64
SB_W = N_SB_HEADS * HEAD_DIM
MEM_W = N_MEM_HEADS * HEAD_DIM
MLA_W = N_MLA_HEADS * HEAD_DIM
QKV_W = 3 * SB_W
GATE_W = SB_W + 2 * MEM_W
CAT_W = 2 * HEAD_DIM
ROPE_THETA = 10000.0
EPS = 1e-6
N_CHIPS = 4
N_DEV = 8

ADAM_LR = 0.001
ADAM_B1 = 0.9
ADAM_B2 = 0.999
ADAM_EPS = 1e-08
ADAM_WD = 0.01
ADAM_STEP = 10

VMEM_LIMIT_BYTES = 56 * 1024 * 1024
MM_OPERAND_VMEM_BYTES = 24 * 1024 * 1024
ROW_BLOCK = 256
ATT_BLOCK = 256


def _params(*sem):
    return pltpu.CompilerParams(dimension_semantics=sem, vmem_limit_bytes=VMEM_LIMIT_BYTES)


def _pick(n, cands):
    for c in cands:
        if n % c == 0:
            return c
    return n


def _mm(a, b, *, name, ta=False, tb=False, out_dtype=F32, res=None, n_split=1, scale_cols=None):
    if ta:
        k_dim, m_dim = a.shape
    else:
        m_dim, k_dim = a.shape
    if tb:
        n_dim, kb = b.shape
    else:
        kb, n_dim = b.shape
    assert kb == k_dim, (a.shape, b.shape)
    n_per = n_dim // n_split
    bm = m_dim if m_dim <= 1024 else _pick(m_dim, (1024, 512, 256))
    bn = n_per if n_per <= 1024 else _pick(n_per, (1024, 896, 768, 640, 512, 256, 128))
    per_k = (bm * a.dtype.itemsize + bn * b.dtype.itemsize) * 2
    bk = next((k_dim // d for d in range(1, k_dim // 128 + 1)
               if k_dim % d == 0 and (k_dim // d) % 128 == 0 and (k_dim // d) * per_k <= MM_OPERAND_VMEM_BYTES),
              k_dim)
    nk = k_dim // bk
    nb_per = n_per // bn
    grid = (m_dim // bm, n_dim // bn, nk)
    a_spec = (pl.BlockSpec((bk, bm), lambda i, j, k: (k, i)) if ta
              else pl.BlockSpec((bm, bk), lambda i, j, k: (i, k)))
    b_spec = (pl.BlockSpec((bn, bk), lambda i, j, k: (j, k)) if tb
              else pl.BlockSpec((bk, bn), lambda i, j, k: (k, j)))
    dims = (((0 if ta else 1,), (1 if tb else 0,)), ((), ()))
    in_specs = [a_spec, b_spec]
    args = [a, b]
    if res is not None:
        in_specs.append(pl.BlockSpec((bm, bn), lambda i, j, k: (i, j)))
        args.append(res)
    if n_split == 1:
        out_shape = jax.ShapeDtypeStruct((m_dim, n_dim), out_dtype)
        out_spec = pl.BlockSpec((bm, bn), lambda i, j, k: (i, j))
    else:
        out_shape = jax.ShapeDtypeStruct((n_split, m_dim, n_per), out_dtype)
        out_spec = pl.BlockSpec((None, bm, bn), lambda i, j, k: (j // nb_per, i, j % nb_per))

    def body(*refs):
        if res is None:
            a_ref, b_ref, o_ref, acc = refs
            r_ref = None
        else:
            a_ref, b_ref, r_ref, o_ref, acc = refs
        k = pl.program_id(2)
        col_block = pl.program_id(1)

        @pl.when(k == 0)
        def _():
            acc[...] = jnp.zeros_like(acc)

        acc[...] += lax.dot_general(a_ref[...].astype(BF16), b_ref[...].astype(BF16), dims,
                                    preferred_element_type=F32)

        @pl.when(k == nk - 1)
        def _():
            r = acc[...]
            if r_ref is not None:
                r = r + r_ref[...]
            if scale_cols is not None:
                assert scale_cols[0] % bn == 0
                r = r * jnp.where(col_block < scale_cols[0] // bn, scale_cols[1], 1.0)
            o_ref[...] = r.astype(out_dtype)

    return pl.pallas_call(
        body, out_shape=out_shape, grid=grid, in_specs=in_specs, out_specs=out_spec,
        scratch_shapes=[pltpu.VMEM((bm, bn), F32)], name=name,
        compiler_params=_params("parallel", "parallel", "arbitrary"),
    )(*args)


def _rowwise(body, n_rows, ins, outs, accs=(), *, name, block=ROW_BLOCK):
    blk = min(block, n_rows)
    assert n_rows % blk == 0
    in_specs = []
    for arr, is_row in ins:
        if is_row:
            assert arr.shape[0] == n_rows, (name, arr.shape, n_rows)
            width = arr.shape[1] if is_row is True else is_row
            in_specs.append(pl.BlockSpec((blk, width), lambda i: (i, 0)))
        else:
            in_specs.append(pl.BlockSpec(arr.shape, lambda i, nd=arr.ndim: (0,) * nd))
    out_shape = [jax.ShapeDtypeStruct((n_rows, w), dt) for w, dt in outs]
    out_specs = [pl.BlockSpec((blk, w), lambda i: (i, 0)) for w, _ in outs]
    out_shape += [jax.ShapeDtypeStruct(s, dt) for s, dt in accs]
    out_specs += [pl.BlockSpec(s, lambda i, nd=len(s): (0,) * nd) for s, _ in accs]
    n_in, n_out, n_acc = len(ins), len(outs), len(accs)

    def kern(*refs):
        in_refs = refs[:n_in]
        out_refs = refs[n_in:n_in + n_out]
        acc_refs = refs[n_in + n_out:]
        if n_acc:
            @pl.when(pl.program_id(0) == 0)
            def _():
                for r in acc_refs:
                    r[...] = jnp.zeros_like(r)
        body(in_refs, out_refs, acc_refs)

    return pl.pallas_call(
        kern, out_shape=out_shape, grid=(n_rows // blk,), in_specs=in_specs, out_specs=out_specs,
        name=name, compiler_params=_params("arbitrary"),
    )(*[arr for arr, _ in ins])


def _rms(x, g, n=None):
    n = x.shape[-1] if n is None else n
    r = lax.rsqrt(jnp.sum(x * x, axis=-1, keepdims=True) / n + EPS)
    return x * r * g


def _rms_bwd(x, g, dy, n=None):
    n = x.shape[-1] if n is None else n
    r = lax.rsqrt(jnp.sum(x * x, axis=-1, keepdims=True) / n + EPS)
    gdy = dy * g
    dx = r * (gdy - x * ((r * r) * (jnp.sum(gdy * x, axis=-1, keepdims=True) / n)))
    dg = jnp.sum(dy * x * r, axis=0, keepdims=True)
    return dx, dg


def _swap_halves(x):
    lane = lax.broadcasted_iota(jnp.int32, x.shape, 1)
    return jnp.where(lane < ROPE_DIM // 2, pltpu.roll(x, 128 - ROPE_DIM // 2, 1),
                     pltpu.roll(x, ROPE_DIM // 2, 1))


def _rope(n, cos_t, sin_t):
    return n * cos_t + _swap_halves(n) * sin_t


def _rope_bwd(dy, cos_t, sin_t):
    return dy * cos_t - _swap_halves(dy) * sin_t


def _sigmoid(g):
    return 1.0 / (1.0 + jnp.exp(-g))


def _dot_t(a, b):
    return lax.dot_general(a, b, (((1,), (1,)), ((), ())), preferred_element_type=F32)


def _tdot(a, b):
    return lax.dot_general(a, b, (((0,), (0,)), ((), ())), preferred_element_type=F32)


def _dot(a, b):
    return jnp.dot(a, b, preferred_element_type=F32)


def _hs(h, w=HEAD_DIM, base=0):
    return slice(base + h * w, base + (h + 1) * w)


def _mem_head(qm, gq, mk_h, mv_h):
    qb = _rms(qm, gq).astype(BF16)
    s = _dot_t(qb, mk_h) * (HEAD_DIM ** -0.5)
    e = jnp.exp(s - jnp.max(s, axis=-1, keepdims=True))
    p = e / jnp.sum(e, axis=-1, keepdims=True)
    mo = _dot(p.astype(BF16), mv_h)
    return qb, p, mo


def _mix_fwd(att, gates, c0, mk, mv, gq, *, name):
    n_rows = att.shape[0]

    def body(ins, outs, _):
        att_ref, g_ref, mk_ref, mv_ref, gq_ref = ins
        (o_ref,) = outs
        g = g_ref[:, c0:c0 + SB_W]
        o_ref[:, :SB_W] = (att_ref[...] * (g * _sigmoid(g))).astype(BF16)
        for h in range(N_MEM_HEADS):
            qm = g_ref[:, _hs(h, base=c0 + SB_W)]
            gm = g_ref[:, _hs(h, base=c0 + SB_W + MEM_W)]
            _, _, mo = _mem_head(qm, gq_ref[...], mk_ref[:, _hs(h)], mv_ref[:, _hs(h)])
            o_ref[:, _hs(h, base=SB_W)] = (mo * (gm * _sigmoid(gm))).astype(BF16)

    (mixed,) = _rowwise(body, n_rows,
                        [(att, True), (gates, True), (mk, False), (mv, False), (gq, False)],
                        [(D_MODEL, BF16)], name=name)
    return mixed


def _mix_bwd(dmixed, att, gates, c0, mk, mv, gq, *, name):
    n_rows = att.shape[0]
    scale = HEAD_DIM ** -0.5

    def body(ins, outs, accs):
        dm_ref, att_ref, g_ref, mk_ref, mv_ref, gq_ref = ins
        datt_ref, dg_ref = outs
        dmk_ref, dmv_ref, dgq_ref = accs
        g = g_ref[:, c0:c0 + SB_W]
        sg = _sigmoid(g)
        dm = dm_ref[:, :SB_W]
        datt_ref[...] = dm * (g * sg)
        dg_ref[:, :SB_W] = (dm * att_ref[...] * (sg * (1.0 + g * (1.0 - sg)))).astype(BF16)
        for h in range(N_MEM_HEADS):
            qm = g_ref[:, _hs(h, base=c0 + SB_W)]
            gm = g_ref[:, _hs(h, base=c0 + SB_W + MEM_W)]
            mk_h = mk_ref[:, _hs(h)]
            mv_h = mv_ref[:, _hs(h)]
            qb, p, mo = _mem_head(qm, gq_ref[...], mk_h, mv_h)
            sgm = _sigmoid(gm)
            dmh = dm_ref[:, _hs(h, base=SB_W)]
            dmo = dmh * (gm * sgm)
            dg_ref[:, _hs(h, base=SB_W + MEM_W)] = (
                dmh * mo * (sgm * (1.0 + gm * (1.0 - sgm)))).astype(BF16)
            dmo_b = dmo.astype(BF16)
            pb = p.astype(BF16)
            dp = _dot_t(dmo_b, mv_h)
            dmv_ref[:, _hs(h)] += _tdot(pb, dmo_b)
            ds = (p * (dp - jnp.sum(dp * p, axis=-1, keepdims=True)) * scale).astype(BF16)
            dqn = _dot(ds, mk_h)
            dmk_ref[:, _hs(h)] += _tdot(ds, qb)
            dqm, dgq = _rms_bwd(qm, gq_ref[...], dqn)
            dg_ref[:, _hs(h, base=SB_W)] = dqm.astype(BF16)
            dgq_ref[...] += dgq

    return _rowwise(body, n_rows,
                    [(dmixed, True), (att, True), (gates, True), (mk, False), (mv, False), (gq, False)],
                    [(SB_W, F32), (GATE_W, BF16)],
                    [((MEM_LEN, MEM_W), F32), ((MEM_LEN, MEM_W), F32), ((1, HEAD_DIM), F32)],
                    name=name)


def _mem_side_fwd(mem, g_norm, w_kv, g_k, *, tag):
    def norm_body(ins, outs, _):
        outs[0][...] = _rms(ins[0][...], ins[1][...]).astype(BF16)

    (mn,) = _rowwise(norm_body, MEM_LEN, [(mem, True), (g_norm, False)], [(D_MODEL, BF16)],
                     name=f"mem_norm_{tag}")
    mkv = _mm(mn, w_kv, name=f"mem_kv_{tag}")

    def kv_body(ins, outs, _):
        mkv_ref, gk_ref = ins
        mk_ref, mv_ref = outs
        for h in range(N_MEM_HEADS):
            mk_ref[:, _hs(h)] = _rms(mkv_ref[:, _hs(h)], gk_ref[...]).astype(BF16)
        mv_ref[...] = mkv_ref[:, MEM_W:].astype(BF16)

    mk, mv = _rowwise(kv_body, MEM_LEN, [(mkv, True), (g_k, False)], [(MEM_W, BF16), (MEM_W, BF16)],
                      name=f"mem_kv_prep_{tag}")
    return mn, mkv, mk, mv


def _mem_side_bwd(mem, g_norm, w_kv, g_k, mn, mkv, dmk, dmv, *, tag):
    def kv_body(ins, outs, accs):
        mkv_ref, gk_ref, dmk_ref, dmv_ref = ins
        (d_ref,) = outs
        (dgk_ref,) = accs
        for h in range(N_MEM_HEADS):
            dx, dg = _rms_bwd(mkv_ref[:, _hs(h)], gk_ref[...], dmk_ref[:, _hs(h)])
            d_ref[:, _hs(h)] = dx.astype(BF16)
            dgk_ref[...] += dg
        d_ref[:, MEM_W:] = dmv_ref[...].astype(BF16)

    dmkv, dgk = _rowwise(kv_body, MEM_LEN, [(mkv, True), (g_k, False), (dmk, True), (dmv, True)],
                         [(2 * MEM_W, BF16)], [((1, HEAD_DIM), F32)], name=f"mem_kv_prep_bwd_{tag}")
    dmn = _mm(dmkv, w_kv, tb=True, name=f"mem_kv_dx_{tag}")
    dw = _mm(mn, dmkv, ta=True, out_dtype=BF16, name=f"mem_kv_dw_{tag}")

    def norm_body(ins, outs, accs):
        _, dg = _rms_bwd(ins[0][...], ins[1][...], ins[2][...])
        accs[0][...] += dg

    (dgn,) = _rowwise(norm_body, MEM_LEN, [(mem, True), (g_norm, False), (dmn, True)], [],
                      [((1, D_MODEL), F32)], name=f"mem_norm_bwd_{tag}")
    return dw, dgn, dgk


LOG2_E = 1.4426950408889634
SB_Q_SCALE = HEAD_DIM ** -0.5 * LOG2_E


Z2_CAP = 126.0


def _sb_terms(z2):
    zc = jnp.minimum(z2, Z2_CAP)
    w = 1.0 + jnp.exp2(zc)
    return zc, w, jnp.log2(w)


LOOP_UNROLL = 8


def _loop_blocks(base, qb, body, carry, *, reverse):
    def run(start, trips, unroll, c0):
        def trip(t, c):
            for u in range(unroll):
                p = start + t * unroll + u
                c = body(base - 1 - p if reverse else p, c)
            return c
        return lax.fori_loop(0, trips, trip, c0)

    if qb % LOOP_UNROLL == 0:
        return run(0, base // LOOP_UNROLL, LOOP_UNROLL, carry)
    small = qb
    n_big = base // LOOP_UNROLL
    carry = run(0, n_big, LOOP_UNROLL, carry)
    return run(n_big * LOOP_UNROLL, (base - n_big * LOOP_UNROLL) // small, small, carry)


def _chain_modes(s, qb):
    return tuple(None if t < s else ("m" if t == s else "f") for t in range(qb))


def _split_dot(x, tri2):
    hi = x.astype(BF16)
    lo = (x - hi.astype(F32)).astype(BF16)
    return _dot(jnp.concatenate([hi, lo], axis=1), tri2)


def _sb_fwd(qkv, *, name, hp=1):
    seq = qkv.shape[0]
    blk = min(ATT_BLOCK, seq)
    nkb = seq // blk
    qb = _pick(nkb, (4, 2, 1))
    rows = qb * blk
    chains = [(t, s) for t in range(hp) for s in range(qb)]

    def body(q_ref, k_ref, v_ref, o_ref):
        base = pl.program_id(1) * qb
        qs = {(t, s): q_ref[s * blk:(s + 1) * blk, _hs(t)] for t, s in chains}
        row = lax.broadcasted_iota(jnp.int32, (blk, blk), 0)
        col = lax.broadcasted_iota(jnp.int32, (blk, blk), 1)
        after = (row > col).astype(BF16)
        after2 = jnp.concatenate([after, after], axis=0)
        causal = col < row

        def step(j, carry, modes):
            off = pl.multiple_of(j * blk, blk)
            act = [c for c in chains if modes[c[1]]]
            zs, ls = {}, {}
            for c in act:
                zs[c], _, l = _sb_terms(_dot_t(qs[c], k_ref[pl.ds(off, blk), _hs(c[0])]))
                ls[c] = jnp.where(causal, l, 0.0) if modes[c[1]] == "m" else l
            cs = {c: _split_dot(ls[c], after2) for c in act}
            carry = dict(carry)
            for c in act:
                run, acc = carry[c]
                a = jnp.exp2(zs[c] - ls[c] - cs[c] - run)
                if modes[c[1]] == "m":
                    a = jnp.where(causal, a, 0.0)
                acc = acc + _dot(a.astype(BF16), v_ref[pl.ds(off, blk), _hs(c[0])])
                carry[c] = (run + (cs[c][:, :1] + ls[c][:, :1]), acc)
            return carry

        init = (jnp.zeros((blk, 1), F32), jnp.zeros((blk, HEAD_DIM), F32))
        carry = {c: init for c in chains}
        for s in reversed(range(qb)):
            carry = step(base + s, carry, _chain_modes(s, qb))
        carry = _loop_blocks(base, qb, lambda j, c: step(j, c, ("f",) * qb), carry, reverse=True)
        for t, s in chains:
            o_ref[s * blk:(s + 1) * blk, _hs(t)] = carry[(t, s)][1]

    nh = N_SB_HEADS // hp
    return pl.pallas_call(
        body, out_shape=jax.ShapeDtypeStruct((seq, SB_W), F32), grid=(nh, nkb // qb),
        in_specs=[pl.BlockSpec((rows, hp * HEAD_DIM), lambda h, i: (i, h)),
                  pl.BlockSpec((seq, hp * HEAD_DIM), lambda h, i: (0, nh + h)),
                  pl.BlockSpec((seq, hp * HEAD_DIM), lambda h, i: (0, 2 * nh + h))],
        out_specs=pl.BlockSpec((rows, hp * HEAD_DIM), lambda h, i: (i, h)),
        name=name, compiler_params=_params("parallel", "arbitrary"),
    )(qkv, qkv, qkv)


SB_BWD_GROUP = 4


def _sb_bwd(qkv, out, dout, *, name):
    seq = qkv.shape[0]
    blk = min(ATT_BLOCK, seq)
    nkb = seq // blk
    qb = _pick(nkb, (4, 2, 1))
    rows = qb * blk
    scale = HEAD_DIM ** -0.5

    def body(q_ref, k_ref, v_ref, do_ref, o_ref, dq_ref, dk_out, dv_out, dk_ref, dv_ref):
        g = pl.program_id(1)
        base = g * qb

        @pl.when(g == 0)
        def _():
            dk_ref[...] = jnp.zeros_like(dk_ref)
            dv_ref[...] = jnp.zeros_like(dv_ref)

        qs = [q_ref[t * blk:(t + 1) * blk, :] for t in range(qb)]
        dos = [do_ref[t * blk:(t + 1) * blk, :].astype(BF16) for t in range(qb)]
        totals = [jnp.sum(dos[t].astype(F32) * o_ref[t * blk:(t + 1) * blk, :], axis=-1, keepdims=True)
                  for t in range(qb)]
        row = lax.broadcasted_iota(jnp.int32, (blk, blk), 0)
        col = lax.broadcasted_iota(jnp.int32, (blk, blk), 1)
        after = (row > col).astype(BF16)
        after2 = jnp.concatenate([after, after], axis=0)
        from_s = (row >= col).astype(BF16)
        from_s2 = jnp.concatenate([from_s, from_s], axis=0)
        causal = col < row

        def step(j, carry, modes):
            runs, rights, dqs = list(carry[0]), list(carry[1]), list(carry[2])
            off = pl.multiple_of(j * blk, blk)
            kb = k_ref[pl.ds(off, blk), :]
            vb = v_ref[pl.ds(off, blk), :]
            dv_inc = dk_inc = None
            for first in range(0, qb, SB_BWD_GROUP):
                act = [t for t in range(first, min(first + SB_BWD_GROUP, qb)) if modes[t]]
                zs, ls, sns = {}, {}, {}
                for t in act:
                    zs[t], w, l = _sb_terms(_dot_t(qs[t], kb))
                    sns[t] = pl.reciprocal(w, approx=True)
                    ls[t] = jnp.where(causal, l, 0.0) if modes[t] == "m" else l
                cs = {t: _split_dot(ls[t], after2) for t in act}
                das = {t: _dot_t(dos[t], vb) for t in act}
                abs_, des = {}, {}
                for t in act:
                    a = jnp.exp2(zs[t] - ls[t] - cs[t] - runs[t])
                    if modes[t] == "m":
                        a = jnp.where(causal, a, 0.0)
                    abs_[t] = a.astype(BF16)
                    des[t] = abs_[t].astype(F32) * das[t]
                sufs = {t: _split_dot(des[t], from_s2) for t in act}
                for t in act:
                    left = totals[t] - (sufs[t] + rights[t])
                    dz = (des[t] + left) * sns[t] - left
                    if modes[t] == "m":
                        dz = jnp.where(causal, dz, 0.0)
                    dzb = dz.astype(BF16)
                    dqs[t] = dqs[t] + _dot(dzb, kb)
                    inc_v = _tdot(abs_[t], dos[t])
                    inc_k = _tdot(dzb, qs[t])
                    dv_inc = inc_v if dv_inc is None else dv_inc + inc_v
                    dk_inc = inc_k if dk_inc is None else dk_inc + inc_k
                    runs[t] = runs[t] + (cs[t][:, :1] + ls[t][:, :1])
                    rights[t] = rights[t] + sufs[t][:, :1]
            dv_ref[pl.ds(off, blk), :] += dv_inc
            dk_ref[pl.ds(off, blk), :] += dk_inc
            return tuple(runs), tuple(rights), tuple(dqs)

        zero = (jnp.zeros((blk, 1), F32),) * qb
        carry = (zero, zero, (jnp.zeros((blk, HEAD_DIM), F32),) * qb)
        for s in reversed(range(qb)):
            carry = step(base + s, carry, _chain_modes(s, qb))
        carry = _loop_blocks(base, qb, lambda j, c: step(j, c, ("f",) * qb), carry, reverse=True)
        for t in range(qb):
            dq_ref[t * blk:(t + 1) * blk, :] = (carry[2][t] * scale).astype(BF16)

        @pl.when(g == pl.num_programs(1) - 1)
        def _():
            dk_out[...] = (dk_ref[...] * (1.0 / LOG2_E)).astype(BF16)
            dv_out[...] = dv_ref[...].astype(BF16)

    out_sd = jax.ShapeDtypeStruct((seq, SB_W), BF16)
    return pl.pallas_call(
        body, out_shape=[out_sd, out_sd, out_sd], grid=(N_SB_HEADS, nkb // qb),
        in_specs=[pl.BlockSpec((rows, HEAD_DIM), lambda h, i: (i, h)),
                  pl.BlockSpec((seq, HEAD_DIM), lambda h, i: (0, N_SB_HEADS + h)),
                  pl.BlockSpec((seq, HEAD_DIM), lambda h, i: (0, 2 * N_SB_HEADS + h)),
                  pl.BlockSpec((rows, HEAD_DIM), lambda h, i: (i, h)),
                  pl.BlockSpec((rows, HEAD_DIM), lambda h, i: (i, h))],
        out_specs=[pl.BlockSpec((rows, HEAD_DIM), lambda h, i: (i, h)),
                   pl.BlockSpec((seq, HEAD_DIM), lambda h, i: (0, h)),
                   pl.BlockSpec((seq, HEAD_DIM), lambda h, i: (0, h))],
        scratch_shapes=[pltpu.VMEM((seq, HEAD_DIM), F32), pltpu.VMEM((seq, HEAD_DIM), F32)],
        name=name, compiler_params=_params("parallel", "arbitrary"),
    )(qkv, qkv, qkv, dout, out)


MLA_SCALE = (HEAD_DIM + ROPE_DIM) ** -0.5
MLA_Q_SCALE = MLA_SCALE * LOG2_E


def _mla_fwd(q_cat, k_cat, v, *, name, hp=1):
    seq = q_cat.shape[0]
    blk = min(ATT_BLOCK, seq)
    nkb = seq // blk
    qb = _pick(nkb, (4, 2, 1))
    rows = qb * blk
    chains = [(t, s) for t in range(hp) for s in range(qb)]

    def body(q_ref, k_ref, v_ref, o_ref, lse_ref):
        base = pl.program_id(1) * qb
        qs = {(t, s): q_ref[s * blk:(s + 1) * blk, t * CAT_W:(t + 1) * CAT_W] for t, s in chains}
        row = lax.broadcasted_iota(jnp.int32, (blk, blk), 0)
        col = lax.broadcasted_iota(jnp.int32, (blk, blk), 1)
        causal = col <= row
        ones = jnp.ones((blk, HEAD_DIM), BF16)

        def step(j, carry, modes):
            off = pl.multiple_of(j * blk, blk)
            act = [c for c in chains if modes[c[1]]]
            ss = {c: _dot_t(qs[c], k_ref[pl.ds(off, blk), c[0] * CAT_W:(c[0] + 1) * CAT_W]) for c in act}
            carry = dict(carry)
            for c in act:
                m, l, acc = carry[c]
                s = ss[c]
                if modes[c[1]] == "m":
                    s = jnp.where(causal, s, -jnp.inf)
                m_new = jnp.maximum(m, jnp.max(s, axis=-1, keepdims=True))
                pb = jnp.exp2(s - m_new).astype(BF16)
                alpha = jnp.exp2(m - m_new)
                both = _dot(pb, jnp.concatenate([v_ref[pl.ds(off, blk), _hs(c[0])], ones], axis=1))
                l = alpha * l + both[:, HEAD_DIM:HEAD_DIM + 1]
                acc = alpha * acc + both[:, :HEAD_DIM]
                carry[c] = (m_new, l, acc)
            return carry

        init = (jnp.full((blk, 1), -jnp.inf, F32), jnp.zeros((blk, 1), F32),
                jnp.zeros((blk, HEAD_DIM), F32))
        carry = {c: init for c in chains}
        carry = _loop_blocks(base, qb, lambda j, c: step(j, c, ("f",) * qb), carry, reverse=False)
        for s in range(qb):
            carry = step(base + s, carry, _chain_modes(s, qb))
        for t, s in chains:
            m, l, acc = carry[(t, s)]
            o_ref[s * blk:(s + 1) * blk, _hs(t)] = acc / l
            lse_ref[s * blk:(s + 1) * blk, _hs(t)] = jnp.broadcast_to(
                (m + jnp.log2(l)) * (1.0 / LOG2_E), (blk, HEAD_DIM))

    out = jax.ShapeDtypeStruct((seq, MLA_W), F32)
    return pl.pallas_call(
        body, out_shape=[out, out], grid=(N_MLA_HEADS // hp, nkb // qb),
        in_specs=[pl.BlockSpec((rows, hp * CAT_W), lambda h, i: (i, h)),
                  pl.BlockSpec((seq, hp * CAT_W), lambda h, i: (0, h)),
                  pl.BlockSpec((seq, hp * HEAD_DIM), lambda h, i: (0, h))],
        out_specs=[pl.BlockSpec((rows, hp * HEAD_DIM), lambda h, i: (i, h)),
                   pl.BlockSpec((rows, hp * HEAD_DIM), lambda h, i: (i, h))],
        name=name, compiler_params=_params("parallel", "arbitrary"),
    )(q_cat, k_cat, v)


def _mla_bwd(q_cat, k_cat, v, out, lse, dout, *, name):
    seq = q_cat.shape[0]
    blk = min(ATT_BLOCK, seq)
    nkb = seq // blk
    qb = _pick(nkb, (4, 2, 1))
    rows = qb * blk

    def body(q_ref, k_ref, v_ref, o_ref, lse_ref, do_ref, dq_ref, dk_ref, dv_ref):
        g = pl.program_id(1)
        base = g * qb

        @pl.when(g == 0)
        def _():
            dk_ref[...] = jnp.zeros_like(dk_ref)
            dv_ref[...] = jnp.zeros_like(dv_ref)

        qs, dobs, deltas, lses = [], [], [], []
        for t in range(qb):
            rs = slice(t * blk, (t + 1) * blk)
            do = do_ref[rs, :]
            qs.append(q_ref[rs, :])
            dobs.append(do.astype(BF16))
            deltas.append(jnp.sum(do * o_ref[rs, :], axis=-1, keepdims=True))
            lses.append(lse_ref[rs, :1] * LOG2_E)
        row = lax.broadcasted_iota(jnp.int32, (blk, blk), 0)
        col = lax.broadcasted_iota(jnp.int32, (blk, blk), 1)
        causal = col <= row

        def step(j, dqs, modes):
            off = pl.multiple_of(j * blk, blk)
            kb = k_ref[pl.ds(off, blk), :]
            vb = v_ref[pl.ds(off, blk), :]
            act = [t for t in range(qb) if modes[t]]
            ss = {t: _dot_t(qs[t], kb) for t in act}
            dps = {t: _dot_t(dobs[t], vb) for t in act}
            dqs = list(dqs)
            dv_inc = dk_inc = None
            for t in act:
                p = jnp.exp2(ss[t] - lses[t])
                if modes[t] == "m":
                    p = jnp.where(causal, p, 0.0)
                ds = (p * (dps[t] - deltas[t])).astype(BF16)
                inc_v = _tdot(p.astype(BF16), dobs[t])
                inc_k = _tdot(ds, qs[t])
                dv_inc = inc_v if dv_inc is None else dv_inc + inc_v
                dk_inc = inc_k if dk_inc is None else dk_inc + inc_k
                dqs[t] = dqs[t] + _dot(ds, kb)
            dv_ref[pl.ds(off, blk), :] += dv_inc
            dk_ref[pl.ds(off, blk), :] += dk_inc
            return tuple(dqs)

        dqs = (jnp.zeros((blk, CAT_W), F32),) * qb
        dqs = _loop_blocks(base, qb, lambda j, c: step(j, c, ("f",) * qb), dqs, reverse=False)
        for s in range(qb):
            modes = tuple(None if t < s else ("m" if t == s else "f") for t in range(qb))
            dqs = step(base + s, dqs, modes)
        for t in range(qb):
            dq_ref[t * blk:(t + 1) * blk, :] = dqs[t] * MLA_SCALE

        @pl.when(g == pl.num_programs(1) - 1)
        def _():
            dk_ref[...] = dk_ref[...] * (1.0 / LOG2_E)

    return pl.pallas_call(
        body,
        out_shape=[jax.ShapeDtypeStruct((seq, N_MLA_HEADS * CAT_W), F32),
                   jax.ShapeDtypeStruct((seq, N_MLA_HEADS * CAT_W), F32),
                   jax.ShapeDtypeStruct((seq, MLA_W), F32)],
        grid=(N_MLA_HEADS, nkb // qb),
        in_specs=[pl.BlockSpec((rows, CAT_W), lambda h, i: (i, h)),
                  pl.BlockSpec((seq, CAT_W), lambda h, i: (0, h)),
                  pl.BlockSpec((seq, HEAD_DIM), lambda h, i: (0, h)),
                  pl.BlockSpec((rows, HEAD_DIM), lambda h, i: (i, h)),
                  pl.BlockSpec((rows, HEAD_DIM), lambda h, i: (i, h)),
                  pl.BlockSpec((rows, HEAD_DIM), lambda h, i: (i, h))],
        out_specs=[pl.BlockSpec((rows, CAT_W), lambda h, i: (i, h)),
                   pl.BlockSpec((seq, CAT_W), lambda h, i: (0, h)),
                   pl.BlockSpec((seq, HEAD_DIM), lambda h, i: (0, h))],
        name=name, compiler_params=_params("parallel", "arbitrary"),
    )(q_cat, k_cat, v, out, lse, dout)


def _local_step(x, mem, positions, target, w, g):
    seq = x.shape[0]
    inv_freq = jnp.power(ROPE_THETA, -jnp.arange(0, ROPE_DIM, 2, dtype=F32) / ROPE_DIM)
    ang = positions.astype(F32)[:, None] * inv_freq
    cos, sin = jnp.cos(ang), jnp.sin(ang)
    lane_pad = jnp.zeros((seq, HEAD_DIM - ROPE_DIM), F32)
    cos_t = jnp.concatenate([cos, cos, lane_pad], axis=1)
    sin_t = jnp.concatenate([-sin, sin, lane_pad], axis=1)
    gain_pad = jnp.zeros((1, HEAD_DIM - ROPE_DIM), F32)
    g_k_rope = jnp.concatenate([g["g_k_rope"], gain_pad], axis=1)
    g_q_rope = jnp.concatenate([g["b_g_q_rope"], gain_pad], axis=1)

    def norm_to_bf16(src, gain, name):
        def body(ins, outs, _):
            outs[0][...] = _rms(ins[0][...], ins[1][...]).astype(BF16)
        return _rowwise(body, seq, [(src, True), (gain, False)], [(src.shape[1], BF16)], name=name)[0]

    h_a = norm_to_bf16(x, g["a_norm"], "a_norm_fwd")
    qkv = _mm(h_a, w["a_in_qkv"], out_dtype=BF16, scale_cols=(SB_W, SB_Q_SCALE), name="a_in_qkv")
    gr = _mm(h_a, w["a_in_gate"], name="a_in_gate")
    sb = _sb_fwd(qkv, name="sb_fwd")
    mem0 = _mem_side_fwd(mem, g["mem_norm"][0:1], w["mem_kv"][0], g["g_mem_k"][0:1], tag="a")
    mixed_a = _mix_fwd(sb, gr, 0, mem0[2], mem0[3], g["g_mem_q"][0:1], name="a_mix_fwd")
    x1 = _mm(mixed_a, w["a_out"], res=x, name="a_out")

    def norms2_body(ins, outs, _):
        xv = ins[0][...]
        outs[0][...] = _rms(xv, ins[1][...]).astype(BF16)
        outs[1][...] = _rms(xv, ins[2][...]).astype(BF16)

    h_kv, h_b = _rowwise(norms2_body, seq, [(x1, True), (g["kv_norm"], False), (g["b_norm"], False)],
                         [(D_MODEL, BF16), (D_MODEL, BF16)], name="kv_b_norm_fwd")
    ckr = _mm(h_kv, w["dkv"], name="dkv")

    def ckr_body(ins, outs, _):
        ckr_ref, gc_ref, gr_ref, c_ref, s_ref = ins
        outs[0][...] = _rms(ckr_ref[:, :KV_LORA], gc_ref[...]).astype(BF16)
        kr = _rms(ckr_ref[:, KV_LORA:], gr_ref[...], n=ROPE_DIM)
        outs[1][...] = _rope(kr, c_ref[...], s_ref[...]).astype(BF16)

    c_n, k_r = _rowwise(ckr_body, seq,
                        [(ckr, True), (g["g_ckv"], False), (g_k_rope, False), (cos_t, True), (sin_t, True)],
                        [(KV_LORA, BF16), (HEAD_DIM, BF16)], name="ckv_prep_fwd")
    kv = _mm(c_n, w["ukv"], name="ukv")

    def kcat_body(ins, outs, _):
        kv_ref, kr_ref, gk_ref = ins
        kc_ref, v_ref = outs
        for h in range(N_MLA_HEADS):
            kc_ref[:, h * CAT_W:h * CAT_W + HEAD_DIM] = _rms(
                kv_ref[:, h * CAT_W:h * CAT_W + HEAD_DIM], gk_ref[...]).astype(BF16)
            kc_ref[:, h * CAT_W + HEAD_DIM:(h + 1) * CAT_W] = kr_ref[...]
            v_ref[:, _hs(h)] = kv_ref[:, h * CAT_W + HEAD_DIM:(h + 1) * CAT_W].astype(BF16)

    k_cat, v_mla = _rowwise(kcat_body, seq, [(kv, True), (k_r, True), (g["g_k_nope"], False)],
                            [(N_MLA_HEADS * CAT_W, BF16), (MLA_W, BF16)], name="k_prep_fwd")

    p2 = _mm(h_b, w["b_in"], name="b_in")

    def qlat_body(ins, outs, _):
        outs[0][...] = _rms(ins[0][:, :Q_LORA], ins[1][...]).astype(BF16)

    (q_l,) = _rowwise(qlat_body, seq, [(p2, Q_LORA), (g["b_g_q_lat"], False)], [(Q_LORA, BF16)],
                      name="q_lat_norm_fwd")
    q_up = _mm(q_l, w["uq"], name="uq")

    def qcat_body(ins, outs, _):
        q_ref, gn_ref, gr_ref, c_ref, s_ref = ins
        (o_ref,) = outs
        for h in range(N_MLA_HEADS):
            o_ref[:, h * CAT_W:h * CAT_W + HEAD_DIM] = (MLA_Q_SCALE * _rms(
                q_ref[:, h * CAT_W:h * CAT_W + HEAD_DIM], gn_ref[...])).astype(BF16)
            qr = _rms(q_ref[:, h * CAT_W + HEAD_DIM:(h + 1) * CAT_W], gr_ref[...], n=ROPE_DIM)
            o_ref[:, h * CAT_W + HEAD_DIM:(h + 1) * CAT_W] = (
                MLA_Q_SCALE * _rope(qr, c_ref[...], s_ref[...])).astype(BF16)

    (q_cat,) = _rowwise(qcat_body, seq,
                        [(q_up, True), (g["b_g_q_nope"], False), (g_q_rope, False), (cos_t, True), (sin_t, True)],
                        [(N_MLA_HEADS * CAT_W, BF16)], name="q_prep_fwd")
    att, lse = _mla_fwd(q_cat, k_cat, v_mla, name="mla_fwd")
    mem1 = _mem_side_fwd(mem, g["mem_norm"][1:2], w["mem_kv"][1], g["g_mem_k"][1:2], tag="b")
    mixed_b = _mix_fwd(att, p2, Q_LORA, mem1[2], mem1[3], g["g_mem_q"][1:2], name="b_mix_fwd")
    y = _mm(mixed_b, w["b_out"], res=x1, name="b_out")

    def loss_body(ins, outs, accs):
        diff = ins[0][...] - ins[1][...]
        outs[0][...] = diff / D_MODEL
        col = jnp.sum(diff * diff, axis=0, keepdims=True)
        part = col[:, :HEAD_DIM]
        for c in range(1, D_MODEL // HEAD_DIM):
            part = part + col[:, _hs(c)]
        accs[0][...] += part * (0.5 / D_MODEL)

    dy, loss_part = _rowwise(loss_body, seq, [(y, True), (target, True)], [(D_MODEL, F32)],
                             [((1, HEAD_DIM), F32)], name="loss")

    gw, gg = {}, {}
    dmixed_b = _mm(dy, w["b_out"], tb=True, name="b_out_dx")
    gw["b_out"] = _mm(mixed_b, dy, ta=True, out_dtype=BF16, name="b_out_dw")
    datt, dgate_b, dmk1, dmv1, gq1 = _mix_bwd(dmixed_b, att, p2, Q_LORA, mem1[2], mem1[3],
                                              g["g_mem_q"][1:2], name="b_mix_bwd")
    dq_cat, dk_cat, dv_mla = _mla_bwd(q_cat, k_cat, v_mla, att, lse, datt, name="mla_bwd")

    def qcat_bwd_body(ins, outs, accs):
        q_ref, dq_ref, gn_ref, gr_ref, c_ref, s_ref = ins
        (o_ref,) = outs
        dgn_ref, dgr_ref = accs
        for h in range(N_MLA_HEADS):
            dx, dg = _rms_bwd(q_ref[:, h * CAT_W:h * CAT_W + HEAD_DIM], gn_ref[...],
                              dq_ref[:, h * CAT_W:h * CAT_W + HEAD_DIM])
            o_ref[:, h * CAT_W:h * CAT_W + HEAD_DIM] = dx.astype(BF16)
            dgn_ref[...] += dg
            dn = _rope_bwd(dq_ref[:, h * CAT_W + HEAD_DIM:(h + 1) * CAT_W], c_ref[...], s_ref[...])
            dx, dg = _rms_bwd(q_ref[:, h * CAT_W + HEAD_DIM:(h + 1) * CAT_W], gr_ref[...], dn, n=ROPE_DIM)
            o_ref[:, h * CAT_W + HEAD_DIM:(h + 1) * CAT_W] = dx.astype(BF16)
            dgr_ref[...] += dg

    dq_up, gg["b_g_q_nope"], dgqr = _rowwise(
        qcat_bwd_body, seq,
        [(q_up, True), (dq_cat, True), (g["b_g_q_nope"], False), (g_q_rope, False), (cos_t, True), (sin_t, True)],
        [(N_MLA_HEADS * CAT_W, BF16)], [((1, HEAD_DIM), F32), ((1, HEAD_DIM), F32)], name="q_prep_bwd")
    gg["b_g_q_rope"] = dgqr
    dq_l = _mm(dq_up, w["uq"], tb=True, name="uq_dx")
    gw["uq"] = _mm(q_l, dq_up, ta=True, out_dtype=BF16, n_split=N_CHIPS, name="uq_dw")

    def qlat_bwd_body(ins, outs, accs):
        p2_ref, dql_ref, dgate_ref, gl_ref = ins
        dx, dg = _rms_bwd(p2_ref[:, :Q_LORA], gl_ref[...], dql_ref[...])
        outs[0][:, :Q_LORA] = dx.astype(BF16)
        outs[0][:, Q_LORA:] = dgate_ref[...]
        accs[0][...] += dg

    dp2, gg["b_g_q_lat"] = _rowwise(
        qlat_bwd_body, seq, [(p2, Q_LORA), (dq_l, True), (dgate_b, True), (g["b_g_q_lat"], False)],
        [(Q_LORA + GATE_W, BF16)], [((1, Q_LORA), F32)], name="q_lat_norm_bwd")
    dh_b = _mm(dp2, w["b_in"], tb=True, name="b_in_dx")
    gw["b_in"] = _mm(h_b, dp2, ta=True, out_dtype=BF16, n_split=N_CHIPS, name="b_in_dw")

    def kcat_bwd_body(ins, outs, accs):
        kv_ref, dkc_ref, dv_ref, gk_ref = ins
        dkv_ref, dkr_ref = outs
        (dgk_ref,) = accs
        dkr = jnp.zeros(dkr_ref.shape, F32)
        for h in range(N_MLA_HEADS):
            dx, dg = _rms_bwd(kv_ref[:, h * CAT_W:h * CAT_W + HEAD_DIM], gk_ref[...],
                              dkc_ref[:, h * CAT_W:h * CAT_W + HEAD_DIM])
            dkv_ref[:, h * CAT_W:h * CAT_W + HEAD_DIM] = dx.astype(BF16)
            dgk_ref[...] += dg
            dkv_ref[:, h * CAT_W + HEAD_DIM:(h + 1) * CAT_W] = dv_ref[:, _hs(h)].astype(BF16)
            dkr = dkr + dkc_ref[:, h * CAT_W + HEAD_DIM:(h + 1) * CAT_W]
        dkr_ref[...] = dkr

    dkv, dk_r, gg["g_k_nope"] = _rowwise(
        kcat_bwd_body, seq, [(kv, True), (dk_cat, True), (dv_mla, True), (g["g_k_nope"], False)],
        [(N_MLA_HEADS * CAT_W, BF16), (HEAD_DIM, F32)], [((1, HEAD_DIM), F32)], name="k_prep_bwd")
    dc_n = _mm(dkv, w["ukv"], tb=True, name="ukv_dx")
    gw["ukv"] = _mm(c_n, dkv, ta=True, out_dtype=BF16, n_split=N_CHIPS, name="ukv_dw")

    def ckr_bwd_body(ins, outs, accs):
        ckr_ref, dcn_ref, dkr_ref, gc_ref, gr_ref, c_ref, s_ref = ins
        dx, dg = _rms_bwd(ckr_ref[:, :KV_LORA], gc_ref[...], dcn_ref[...])
        outs[0][:, :KV_LORA] = dx.astype(BF16)
        accs[0][...] += dg
        dn = _rope_bwd(dkr_ref[...], c_ref[...], s_ref[...])
        dx, dg = _rms_bwd(ckr_ref[:, KV_LORA:], gr_ref[...], dn, n=ROPE_DIM)
        outs[0][:, KV_LORA:] = dx.astype(BF16)
        accs[1][...] += dg

    dckr, gg["g_ckv"], gg["g_k_rope"] = _rowwise(
        ckr_bwd_body, seq,
        [(ckr, True), (dc_n, True), (dk_r, True), (g["g_ckv"], False), (g_k_rope, False),
         (cos_t, True), (sin_t, True)],
        [(KV_LORA + HEAD_DIM, BF16)], [((1, KV_LORA), F32), ((1, HEAD_DIM), F32)], name="ckv_prep_bwd")
    dh_kv = _mm(dckr, w["dkv"], tb=True, name="dkv_dx")
    gw["dkv"] = _mm(h_kv, dckr, ta=True, out_dtype=BF16, name="dkv_dw")

    def norms2_bwd_body(ins, outs, accs):
        x_ref, dy_ref, dhk_ref, dhb_ref, gk_ref, gb_ref = ins
        xv = x_ref[...]
        dxk, dgk = _rms_bwd(xv, gk_ref[...], dhk_ref[...])
        dxb, dgb = _rms_bwd(xv, gb_ref[...], dhb_ref[...])
        outs[0][...] = dy_ref[...] + dxk + dxb
        accs[0][...] += dgk
        accs[1][...] += dgb

    dx1, gg["kv_norm"], gg["b_norm"] = _rowwise(
        norms2_bwd_body, seq,
        [(x1, True), (dy, True), (dh_kv, True), (dh_b, True), (g["kv_norm"], False), (g["b_norm"], False)],
        [(D_MODEL, F32)], [((1, D_MODEL), F32), ((1, D_MODEL), F32)], name="kv_b_norm_bwd")

    dmixed_a = _mm(dx1, w["a_out"], tb=True, name="a_out_dx")
    gw["a_out"] = _mm(mixed_a, dx1, ta=True, out_dtype=BF16, name="a_out_dw")
    dsb, dgate_a, dmk0, dmv0, gq0 = _mix_bwd(dmixed_a, sb, gr, 0, mem0[2], mem0[3],
                                             g["g_mem_q"][0:1], name="a_mix_bwd")
    dq, dk, dv = _sb_bwd(qkv, sb, dsb, name="sb_bwd")
    dp_a = jnp.concatenate([dq, dk, dv, dgate_a], axis=1)
    dh_a = _mm(dp_a, w["a_in"], tb=True, name="a_in_dx")
    gw["a_in"] = _mm(h_a, dp_a, ta=True, out_dtype=BF16, n_split=N_CHIPS, name="a_in_dw")

    def norm_a_bwd_body(ins, outs, accs):
        dx, dg = _rms_bwd(ins[0][...], ins[3][...], ins[2][...])
        outs[0][...] = ins[1][...] + dx
        accs[0][...] += dg

    grad_x, gg["a_norm"] = _rowwise(
        norm_a_bwd_body, seq, [(x, True), (dx1, True), (dh_a, True), (g["a_norm"], False)],
        [(D_MODEL, F32)], [((1, D_MODEL), F32)], name="a_norm_bwd")

    dw0, dgn0, dgk0 = _mem_side_bwd(mem, g["mem_norm"][0:1], w["mem_kv"][0], g["g_mem_k"][0:1],
                                    mem0[0], mem0[1], dmk0, dmv0, tag="a")
    dw1, dgn1, dgk1 = _mem_side_bwd(mem, g["mem_norm"][1:2], w["mem_kv"][1], g["g_mem_k"][1:2],
                                    mem1[0], mem1[1], dmk1, dmv1, tag="b")
    gw["mem_kv"] = (dw0, dw1)
    gg["mem_norm"] = jnp.concatenate([dgn0, dgn1], axis=0)
    gg["g_mem_q"] = jnp.concatenate([gq0, gq1], axis=0)
    gg["g_mem_k"] = jnp.concatenate([dgk0, dgk1], axis=0)
    return loss_part, grad_x, gw, gg


HBM_SPEC = pl.BlockSpec(memory_space=pl.ANY)


def _other_chips():
    x, y = lax.axis_index("x"), lax.axis_index("y")
    return [(1 - x, y), (x, 1 - y), (1 - x, 1 - y)]


def _allgather_chips(shards):
    n = len(shards)
    split = [s.shape[0] % 32 == 0 for s in shards]

    def body(*refs):
        ins, outs = refs[:n], refs[n:2 * n]
        send, recv, fsend, frecv = refs[2 * n:]
        x, y, c = lax.axis_index("x"), lax.axis_index("y"), lax.axis_index("c")
        me = 2 * x + y
        chips = _other_chips()

        def part(ref, wi):
            if not split[wi]:
                return ref
            half = shards[wi].shape[0] // 2
            return ref.at[pl.ds(pl.multiple_of(c * half, 16), half)]

        def ici(wi, k, src_chip, to):
            return pltpu.make_async_remote_copy(
                src_ref=part(ins[wi], wi), dst_ref=part(outs[wi].at[src_chip], wi),
                send_sem=send.at[wi, k], recv_sem=recv.at[wi, k], device_id=to, device_id_type=MESH)

        def d2d(wi, k, src_chip):
            rows = part(outs[wi].at[src_chip], wi)
            return pltpu.make_async_remote_copy(
                src_ref=rows, dst_ref=rows, send_sem=fsend.at[wi, k], recv_sem=frecv.at[wi, k],
                device_id=(x, y, 1 - c), device_id_type=MESH)

        for wi in range(n):
            for k, (tx, ty) in enumerate(chips):
                ici(wi, k, me, (tx, ty, c)).start()
        for wi in range(n):
            for k, (tx, ty) in enumerate(chips):
                landed = ici(wi, k, 2 * tx + ty, (tx, ty, c))
                landed.wait_recv()
                if split[wi]:
                    d2d(wi, k, 2 * tx + ty).start()
        for wi in range(n):
            for k, (tx, ty) in enumerate(chips):
                ici(wi, k, me, (tx, ty, c)).wait_send()
                if split[wi]:
                    fwd = d2d(wi, k, 2 * tx + ty)
                    fwd.wait_send()
                    fwd.wait_recv()

    return pl.pallas_call(
        body, out_shape=[jax.ShapeDtypeStruct((N_CHIPS,) + s.shape, s.dtype) for s in shards],
        in_specs=[HBM_SPEC] * n, out_specs=[HBM_SPEC] * n,
        scratch_shapes=[pltpu.SemaphoreType.DMA((n, 3)), pltpu.SemaphoreType.DMA((n, 3)),
                        pltpu.SemaphoreType.DMA((n, 3)), pltpu.SemaphoreType.DMA((n, 3))],
        name="allgather_weights",
    )(*shards)


def _scatter_to_chips(grads):
    n = len(grads)

    def body(*refs):
        ins, outs = refs[:n], refs[n:2 * n]
        send, recv = refs[2 * n:]
        c = lax.axis_index("c")
        copies = []
        for wi in range(n):
            for k, (tx, ty) in enumerate(_other_chips()):
                cp = pltpu.make_async_remote_copy(
                    src_ref=ins[wi].at[2 * tx + ty], dst_ref=outs[wi].at[k], send_sem=send.at[wi, k],
                    recv_sem=recv.at[wi, k], device_id=(tx, ty, c), device_id_type=MESH)
                cp.start()
                copies.append(cp)
        for cp in copies:
            cp.wait()

    return pl.pallas_call(
        body, out_shape=[jax.ShapeDtypeStruct((3,) + s.shape[1:], s.dtype) for s in grads],
        in_specs=[HBM_SPEC] * n, out_specs=[HBM_SPEC] * n,
        scratch_shapes=[pltpu.SemaphoreType.DMA((n, 3)), pltpu.SemaphoreType.DMA((n, 3))],
        name="scatter_grads",
    )(*grads)


def _halve_with_sibling(grads):
    n = len(grads)
    n_slots = grads[0].shape[0]

    def body(*refs):
        ins, got = refs[:n], refs[n:2 * n]
        send, recv = refs[2 * n:]
        c = lax.axis_index("c")
        sib = (lax.axis_index("x"), lax.axis_index("y"), 1 - c)
        copies = []
        for wi in range(n):
            half = grads[wi].shape[1] // 2
            for s in range(n_slots):
                theirs = ins[wi].at[s, pl.ds(pl.multiple_of((1 - c) * half, 16), half)]
                give = pltpu.make_async_remote_copy(
                    src_ref=theirs, dst_ref=got[wi].at[s], send_sem=send.at[wi, s], recv_sem=recv.at[wi, s],
                    device_id=sib, device_id_type=MESH)
                give.start()
                copies.append(give)
        for cp in copies:
            cp.wait()

    halves = [jax.ShapeDtypeStruct((s.shape[0], s.shape[1] // 2) + s.shape[2:], s.dtype) for s in grads]
    return pl.pallas_call(
        body, out_shape=halves, in_specs=[HBM_SPEC] * n, out_specs=[HBM_SPEC] * n,
        scratch_shapes=[pltpu.SemaphoreType.DMA((n, n_slots)), pltpu.SemaphoreType.DMA((n, n_slots))],
        name="halve_grads_with_sibling",
    )(*grads)


def _swap_with_sibling(parts):
    n = len(parts)

    def body(*refs):
        ins, outs = refs[:n], refs[n:2 * n]
        send, recv = refs[2 * n:]
        sib = (lax.axis_index("x"), lax.axis_index("y"), 1 - lax.axis_index("c"))
        copies = []
        for wi in range(n):
            cp = pltpu.make_async_remote_copy(
                src_ref=ins[wi], dst_ref=outs[wi], send_sem=send.at[wi], recv_sem=recv.at[wi],
                device_id=sib, device_id_type=MESH)
            cp.start()
            copies.append(cp)
        for cp in copies:
            cp.wait()

    return pl.pallas_call(
        body, out_shape=[jax.ShapeDtypeStruct(s.shape, s.dtype) for s in parts],
        in_specs=[HBM_SPEC] * n, out_specs=[HBM_SPEC] * n,
        scratch_shapes=[pltpu.SemaphoreType.DMA((n,)), pltpu.SemaphoreType.DMA((n,))],
        name="swap_grad_halves",
    )(*parts)


def _allreduce_small(vec, loss_row):
    rows = vec.shape[0]

    def body(v_ref, o_ref, buf, send, recv):
        x, y, c = lax.axis_index("x"), lax.axis_index("y"), lax.axis_index("c")
        me = 4 * x + 2 * y + c
        buf[me] = v_ref[...]
        copies = []
        for r in range(1, N_DEV):
            peer = (x ^ ((r >> 2) & 1), y ^ ((r >> 1) & 1), c ^ (r & 1))
            cp = pltpu.make_async_remote_copy(
                src_ref=v_ref, dst_ref=buf.at[me], send_sem=send.at[r - 1], recv_sem=recv.at[r - 1],
                device_id=peer, device_id_type=MESH)
            cp.start()
            copies.append(cp)
        for cp in copies:
            cp.wait()
        total = buf[0]
        for d in range(1, N_DEV):
            total = total + buf[d]
        o_ref[...] = total
        o_ref[loss_row:loss_row + 1, :] = jnp.broadcast_to(
            jnp.sum(total[loss_row:loss_row + 1, :], axis=-1, keepdims=True), (1, HEAD_DIM))

    return pl.pallas_call(
        body, out_shape=jax.ShapeDtypeStruct(vec.shape, F32),
        in_specs=[pl.BlockSpec(memory_space=pltpu.VMEM)], out_specs=pl.BlockSpec(memory_space=pltpu.VMEM),
        scratch_shapes=[pltpu.VMEM((N_DEV, rows, HEAD_DIM), F32),
                        pltpu.SemaphoreType.DMA((N_DEV - 1,)), pltpu.SemaphoreType.DMA((N_DEV - 1,))],
        name="allreduce_gains",
    )(vec)


def _pair_sum(grads, got, *, name):
    slots, rows, width = got.shape
    blk = _pick(rows, (256, 128, 64, 32, 16))
    nbh = rows // blk

    def body(lo_ref, hi_ref, got_ref, o_ref):
        mine = jnp.where(lax.axis_index("c") == 0, lo_ref[...], hi_ref[...])
        o_ref[...] = (mine.astype(F32) + got_ref[...].astype(F32)).astype(BF16)

    spec = pl.BlockSpec((None, blk, width), lambda s, i: (s, i, 0))
    return pl.pallas_call(
        body, out_shape=jax.ShapeDtypeStruct(got.shape, BF16), grid=(slots, nbh),
        in_specs=[spec, pl.BlockSpec((None, blk, width), lambda s, i: (s, nbh + i, 0)), spec],
        out_specs=spec, name=name, compiler_params=_params("parallel", "parallel"),
    )(grads, grads, got)


def _sum_slots(recv, chip_sum, *, name):
    _, rows, width = recv.shape
    blk = _pick(rows, (256, 128, 64, 32, 16, 8))

    def body(r_ref, p_ref, o_ref):
        me = 2 * lax.axis_index("x") + lax.axis_index("y")
        own = jnp.where(me < 2, jnp.where(me == 0, p_ref[0], p_ref[1]), jnp.where(me == 2, p_ref[2], p_ref[3]))
        o_ref[...] = ((own.astype(F32) + r_ref[0].astype(F32)) + r_ref[1].astype(F32)) + r_ref[2].astype(F32)

    return pl.pallas_call(
        body, out_shape=jax.ShapeDtypeStruct((rows, width), F32), grid=(rows // blk,),
        in_specs=[pl.BlockSpec((3, blk, width), lambda i: (0, i, 0)),
                  pl.BlockSpec((N_CHIPS, blk, width), lambda i: (0, i, 0))],
        out_specs=pl.BlockSpec((blk, width), lambda i: (i, 0)),
        name=name, compiler_params=_params("parallel"),
    )(recv, chip_sum)


def _adamw(wgt, grad, m, v, *, name, halves=None):
    rows, width = wgt.shape
    blk = _pick(rows // 2 if halves else rows, (256, 128, 64, 32, 16, 8))
    nbh = rows // 2 // blk

    def body(*refs):
        if halves:
            w_ref, mine_ref, theirs_ref, m_ref, v_ref, g_out, d_out, m_out, v_out = refs
            grad_v = jnp.where(pl.program_id(0) // nbh == lax.axis_index("c"), mine_ref[...], theirs_ref[...])
        else:
            w_ref, g_ref, m_ref, v_ref, g_out, d_out, m_out, v_out = refs
            grad_v = g_ref[...]
        m_new = ADAM_B1 * m_ref[...] + (1.0 - ADAM_B1) * grad_v
        v_new = ADAM_B2 * v_ref[...] + (1.0 - ADAM_B2) * (grad_v * grad_v)
        m_hat = m_new / (1.0 - ADAM_B1 ** ADAM_STEP)
        v_hat = v_new / (1.0 - ADAM_B2 ** ADAM_STEP)
        g_out[...] = grad_v
        d_out[...] = -ADAM_LR * (m_hat / (jnp.sqrt(v_hat) + ADAM_EPS) + ADAM_WD * w_ref[...])
        m_out[...] = m_new
        v_out[...] = v_new

    spec = pl.BlockSpec((blk, width), lambda i: (i, 0))
    half_spec = pl.BlockSpec((blk, width), lambda i: (i % nbh, 0))
    g_specs, g_args = ([half_spec, half_spec], list(halves)) if halves else ([spec], [grad])
    out = jax.ShapeDtypeStruct((rows, width), F32)
    return pl.pallas_call(
        body, out_shape=[out] * 4, grid=(rows // blk,), in_specs=[spec] + g_specs + [spec, spec],
        out_specs=[spec] * 4, name=name, compiler_params=_params("parallel"),
    )(wgt, *g_args, m, v)


_SMALL = (("a_norm", 2048), ("kv_norm", 2048), ("g_ckv", 512), ("g_k_nope", 128), ("g_k_rope", 64),
          ("b_norm", 2048), ("b_g_q_lat", 512), ("b_g_q_nope", 128), ("b_g_q_rope", 64),
          ("mem_norm", 4096), ("g_mem_q", 256), ("g_mem_k", 256))


def _lanes(n):
    return -(-n // HEAD_DIM) * HEAD_DIM


def _pack_rows(pieces, pad_rows_to=8):
    flat = jnp.concatenate(pieces, axis=1)
    rows = flat.shape[1] // HEAD_DIM
    pad = (-rows) % pad_rows_to
    if pad:
        flat = jnp.concatenate([flat, jnp.zeros((1, pad * HEAD_DIM), F32)], axis=1)
    return flat.reshape(rows + pad, HEAD_DIM)


def _pad_lanes(a):
    a = a.reshape(1, -1)
    pad = _lanes(a.shape[1]) - a.shape[1]
    if pad:
        a = jnp.concatenate([a, jnp.zeros((1, pad), F32)], axis=1)
    return a


def kernel(x, mem, positions, a_norm, a_w_in, a_w_out, kv_norm, w_dkv, g_ckv, w_ukv, g_k_nope, g_k_rope, b_norm, b_w_in, b_g_q_lat, b_w_uq, b_g_q_nope, b_g_q_rope, b_w_out, mem_norm, w_mem_kv, g_mem_q, g_mem_k, loss_target, m_a_norm, m_a_w_in, m_a_w_out, m_kv_norm, m_w_dkv, m_g_ckv, m_w_ukv, m_g_k_nope, m_g_k_rope, m_b_norm, m_b_w_in, m_b_g_q_lat, m_b_w_uq, m_b_g_q_nope, m_b_g_q_rope, m_b_w_out, m_mem_norm, m_w_mem_kv, m_g_mem_q, m_g_mem_k, v_a_norm, v_a_w_in, v_a_w_out, v_kv_norm, v_w_dkv, v_g_ckv, v_w_ukv, v_g_k_nope, v_g_k_rope, v_b_norm, v_b_w_in, v_b_g_q_lat, v_b_w_uq, v_b_g_q_nope, v_b_g_q_rope, v_b_w_out, v_mem_norm, v_w_mem_kv, v_g_mem_q, v_g_mem_k):
    chip = 2 * lax.axis_index("x") + lax.axis_index("y")
    rows_dkv = D_MODEL // N_CHIPS
    heads_per_chip = N_MLA_HEADS // N_CHIPS
    qk_w = HEAD_DIM + ROPE_DIM

    big = {"a_in": a_w_in[0], "a_out": a_w_out[0], "dkv": w_dkv, "ukv": w_ukv, "b_in": b_w_in[0],
           "uq": b_w_uq[0], "b_out": b_w_out[0], "mem_kv": w_mem_kv.reshape(2 * rows_dkv, 2 * MEM_W)}
    big_m = {"a_in": m_a_w_in[0], "a_out": m_a_w_out[0], "dkv": m_w_dkv, "ukv": m_w_ukv, "b_in": m_b_w_in[0],
             "uq": m_b_w_uq[0], "b_out": m_b_w_out[0], "mem_kv": m_w_mem_kv.reshape(2 * rows_dkv, 2 * MEM_W)}
    big_v = {"a_in": v_a_w_in[0], "a_out": v_a_w_out[0], "dkv": v_w_dkv, "ukv": v_w_ukv, "b_in": v_b_w_in[0],
             "uq": v_b_w_uq[0], "b_out": v_b_w_out[0], "mem_kv": v_w_mem_kv.reshape(2 * rows_dkv, 2 * MEM_W)}
    names = list(big)
    own_shards = [big[n].astype(BF16) for n in names] + [a_norm]
    gathered = _allgather_chips(own_shards)
    gathered = [lax.dynamic_update_slice(g, s[None], (chip,) + (0,) * s.ndim)
                for g, s in zip(gathered, own_shards)]
    st = dict(zip(names, gathered[:-1]))
    a_in_full = st["a_in"].transpose(1, 0, 2).reshape(D_MODEL, QKV_W + GATE_W)
    uq = st["uq"].reshape(N_CHIPS, Q_LORA, heads_per_chip, qk_w)
    uq = jnp.pad(uq, ((0, 0), (0, 0), (0, 0), (0, CAT_W - qk_w)))
    w = {
        "a_in": a_in_full,
        "a_in_qkv": a_in_full[:, :QKV_W],
        "a_in_gate": a_in_full[:, QKV_W:],
        "a_out": st["a_out"].reshape(D_MODEL, D_MODEL),
        "dkv": jnp.pad(st["dkv"].reshape(D_MODEL, KV_LORA + ROPE_DIM), ((0, 0), (0, HEAD_DIM - ROPE_DIM))),
        "ukv": st["ukv"].transpose(1, 0, 2).reshape(KV_LORA, N_MLA_HEADS * CAT_W),
        "b_in": st["b_in"].transpose(1, 0, 2).reshape(D_MODEL, Q_LORA + GATE_W),
        "uq": uq.transpose(1, 0, 2, 3).reshape(Q_LORA, N_MLA_HEADS * CAT_W),
        "b_out": st["b_out"].reshape(D_MODEL, D_MODEL),
        "mem_kv": st["mem_kv"].reshape(N_CHIPS, 2, rows_dkv, 2 * MEM_W).transpose(1, 0, 2, 3).reshape(
            2, D_MODEL, 2 * MEM_W),
    }
    gains = {
        "a_norm": gathered[-1].reshape(1, D_MODEL), "kv_norm": kv_norm.reshape(1, -1),
        "g_ckv": g_ckv.reshape(1, -1), "g_k_nope": g_k_nope.reshape(1, -1), "g_k_rope": g_k_rope.reshape(1, -1),
        "b_norm": b_norm, "b_g_q_lat": b_g_q_lat, "b_g_q_nope": b_g_q_nope, "b_g_q_rope": b_g_q_rope,
        "mem_norm": mem_norm, "g_mem_q": g_mem_q, "g_mem_k": g_mem_k,
    }

    loss_part, grad_x, gw, gg = _local_step(x[0], mem[0], positions[0], loss_target[0], w, gains)

    stacked = {
        "a_in": gw["a_in"],
        "a_out": gw["a_out"].reshape(N_CHIPS, rows_dkv, D_MODEL),
        "dkv": gw["dkv"][:, :KV_LORA + ROPE_DIM].reshape(N_CHIPS, rows_dkv, KV_LORA + ROPE_DIM),
        "ukv": gw["ukv"],
        "b_in": gw["b_in"],
        "uq": gw["uq"].reshape(N_CHIPS, Q_LORA, heads_per_chip, CAT_W)[..., :qk_w].reshape(
            N_CHIPS, Q_LORA, heads_per_chip * qk_w),
        "b_out": gw["b_out"].reshape(N_CHIPS, rows_dkv, D_MODEL),
        "mem_kv": jnp.stack([gw["mem_kv"][0].reshape(N_CHIPS, rows_dkv, 2 * MEM_W),
                             gw["mem_kv"][1].reshape(N_CHIPS, rows_dkv, 2 * MEM_W)], axis=1).reshape(
            N_CHIPS, 2 * rows_dkv, 2 * MEM_W),
    }
    got = _halve_with_sibling([stacked[n] for n in names])
    chip_sum = [_pair_sum(stacked[n], g, name=f"pair_sum_{n}") for n, g in zip(names, got)]
    received = _scatter_to_chips(chip_sum)
    half_total = [_sum_slots(r, p, name=f"sum_slots_{n}") for n, r, p in zip(names, received, chip_sum)]
    sibling_half = _swap_with_sibling(half_total)
    big_out = {}
    for n, mine, theirs in zip(names, half_total, sibling_half):
        big_out[n] = _adamw(big[n], None, big_m[n], big_v[n], halves=(mine, theirs), name=f"adamw_{n}")

    pieces = [_pad_lanes(gg[n]) if n not in ("g_k_rope", "b_g_q_rope") else gg[n] for n, _ in _SMALL]
    pieces.append(loss_part)
    loss_row = sum(_lanes(size) for _, size in _SMALL) // HEAD_DIM
    summed = _allreduce_small(_pack_rows(pieces), loss_row)
    flat = summed.reshape(1, -1)
    small_g, off = {}, 0
    for n, size in _SMALL:
        small_g[n] = flat[:, off:off + size]
        off += _lanes(size)
    loss = flat[0, off]
    small_g["a_norm"] = lax.dynamic_slice(small_g["a_norm"], (0, chip * rows_dkv), (1, rows_dkv))

    small_w = {"a_norm": a_norm, "kv_norm": kv_norm, "g_ckv": g_ckv, "g_k_nope": g_k_nope, "g_k_rope": g_k_rope,
               "b_norm": b_norm, "b_g_q_lat": b_g_q_lat, "b_g_q_nope": b_g_q_nope, "b_g_q_rope": b_g_q_rope,
               "mem_norm": mem_norm, "g_mem_q": g_mem_q, "g_mem_k": g_mem_k}
    small_m = {"a_norm": m_a_norm, "kv_norm": m_kv_norm, "g_ckv": m_g_ckv, "g_k_nope": m_g_k_nope,
               "g_k_rope": m_g_k_rope, "b_norm": m_b_norm, "b_g_q_lat": m_b_g_q_lat, "b_g_q_nope": m_b_g_q_nope,
               "b_g_q_rope": m_b_g_q_rope, "mem_norm": m_mem_norm, "g_mem_q": m_g_mem_q, "g_mem_k": m_g_mem_k}
    small_v = {"a_norm": v_a_norm, "kv_norm": v_kv_norm, "g_ckv": v_g_ckv, "g_k_nope": v_g_k_nope,
               "g_k_rope": v_g_k_rope, "b_norm": v_b_norm, "b_g_q_lat": v_b_g_q_lat, "b_g_q_nope": v_b_g_q_nope,
               "b_g_q_rope": v_b_g_q_rope, "mem_norm": v_mem_norm, "g_mem_q": v_g_mem_q, "g_mem_k": v_g_mem_k}
    snames = [n for n, _ in _SMALL]
    packs = [_pack_rows([_pad_lanes(src[n]) for n in snames])
             for src in (small_w, small_g, small_m, small_v)]
    small_res = _adamw(packs[0], packs[1], packs[2], packs[3], name="adamw_gains")
    small_out = {n: [] for n in snames}
    for res in small_res:
        flat_r = res.reshape(1, -1)
        off = 0
        for n in snames:
            size = small_w[n].size
            small_out[n].append(flat_r[:, off:off + size].reshape(small_w[n].shape))
            off += _lanes(size)

    big_names = {"a_w_in": ("a_in", a_w_in), "a_w_out": ("a_out", a_w_out), "w_dkv": ("dkv", w_dkv),
                 "w_ukv": ("ukv", w_ukv), "b_w_in": ("b_in", b_w_in), "b_w_uq": ("uq", b_w_uq),
                 "b_w_out": ("b_out", b_w_out), "w_mem_kv": ("mem_kv", w_mem_kv)}
    order = ["a_norm", "a_w_in", "a_w_out", "kv_norm", "w_dkv", "g_ckv", "w_ukv", "g_k_nope", "g_k_rope",
             "b_norm", "b_w_in", "b_g_q_lat", "b_w_uq", "b_g_q_nope", "b_g_q_rope", "b_w_out", "mem_norm",
             "w_mem_kv", "g_mem_q", "g_mem_k"]
    groups = [[], [], [], []]
    for n in order:
        if n in big_names:
            key, ref_arr = big_names[n]
            for t in range(4):
                groups[t].append(big_out[key][t].reshape(ref_arr.shape))
        else:
            for t in range(4):
                groups[t].append(small_out[n][t])
    return (loss, grad_x[None], *groups[0], *groups[1], *groups[2], *groups[3])
```

```python
import jax
import jax.numpy as jnp
from jax import lax
from jax.experimental import pallas as pl
from jax.experimental.pallas import tpu as pltpu

F32 = jnp.float32
BF16 = jnp.bfloat16
MESH = pl.DeviceIdType.MESH

D_MODEL = 2048
HEAD_DIM = 128
N_SB_HEADS = 12
N_MEM_HEADS = 4
N_MLA_HEADS = 12
MEM_LEN = 256
Q_LORA = 512
KV_LORA = 512
ROPE_DIM = 64
SB_W = N_SB_HEADS * HEAD_DIM
MEM_W = N_MEM_HEADS * HEAD_DIM
MLA_W = N_MLA_HEADS * HEAD_DIM
QKV_W = 3 * SB_W
GATE_W = SB_W + 2 * MEM_W
CAT_W = 2 * HEAD_DIM
ROPE_THETA = 10000.0
EPS = 1e-6
N_CHIPS = 4
N_DEV = 8

ADAM_LR = 0.001
ADAM_B1 = 0.9
ADAM_B2 = 0.999
ADAM_EPS = 1e-08
ADAM_WD = 0.01
ADAM_STEP = 10

VMEM_LIMIT_BYTES = 56 * 1024 * 1024
MM_OPERAND_VMEM_BYTES = 24 * 1024 * 1024
ROW_BLOCK = 256
ATT_BLOCK = 256


def _params(*sem):
    return pltpu.CompilerParams(dimension_semantics=sem, vmem_limit_bytes=VMEM_LIMIT_BYTES)


def _pick(n, cands):
    for c in cands:
        if n % c == 0:
            return c
    return n


def _mm(a, b, *, name, ta=False, tb=False, out_dtype=F32, res=None, n_split=1, scale_cols=None):
    if ta:
        k_dim, m_dim = a.shape
    else:
        m_dim, k_dim = a.shape
    if tb:
        n_dim, kb = b.shape
    else:
        kb, n_dim = b.shape
    assert kb == k_dim, (a.shape, b.shape)
    n_per = n_dim // n_split
    bm = m_dim if m_dim <= 1024 else _pick(m_dim, (1024, 512, 256))
    bn = n_per if n_per <= 1024 else _pick(n_per, (1024, 896, 768, 640, 512, 256, 128))
    per_k = (bm * a.dtype.itemsize + bn * b.dtype.itemsize) * 2
    bk = next((k_dim // d for d in range(1, k_dim // 128 + 1)
               if k_dim % d == 0 and (k_dim // d) % 128 == 0 and (k_dim // d) * per_k <= MM_OPERAND_VMEM_BYTES),
              k_dim)
    nk = k_dim // bk
    nb_per = n_per // bn
    grid = (m_dim // bm, n_dim // bn, nk)
    a_spec = (pl.BlockSpec((bk, bm), lambda i, j, k: (k, i)) if ta
              else pl.BlockSpec((bm, bk), lambda i, j, k: (i, k)))
    b_spec = (pl.BlockSpec((bn, bk), lambda i, j, k: (j, k)) if tb
              else pl.BlockSpec((bk, bn), lambda i, j, k: (k, j)))
    dims = (((0 if ta else 1,), (1 if tb else 0,)), ((), ()))
    in_specs = [a_spec, b_spec]
    args = [a, b]
    if res is not None:
        in_specs.append(pl.BlockSpec((bm, bn), lambda i, j, k: (i, j)))
        args.append(res)
    if n_split == 1:
        out_shape = jax.ShapeDtypeStruct((m_dim, n_dim), out_dtype)
        out_spec = pl.BlockSpec((bm, bn), lambda i, j, k: (i, j))
    else:
        out_shape = jax.ShapeDtypeStruct((n_split, m_dim, n_per), out_dtype)
        out_spec = pl.BlockSpec((None, bm, bn), lambda i, j, k: (j // nb_per, i, j % nb_per))

    def body(*refs):
        if res is None:
            a_ref, b_ref, o_ref, acc = refs
            r_ref = None
        else:
            a_ref, b_ref, r_ref, o_ref, acc = refs
        k = pl.program_id(2)
        col_block = pl.program_id(1)

        @pl.when(k == 0)
        def _():
            acc[...] = jnp.zeros_like(acc)

        acc[...] += lax.dot_general(a_ref[...].astype(BF16), b_ref[...].astype(BF16), dims,
                                    preferred_element_type=F32)

        @pl.when(k == nk - 1)
        def _():
            r = acc[...]
            if r_ref is not None:
                r = r + r_ref[...]
            if scale_cols is not None:
                assert scale_cols[0] % bn == 0
                r = r * jnp.where(col_block < scale_cols[0] // bn, scale_cols[1], 1.0)
            o_ref[...] = r.astype(out_dtype)

    return pl.pallas_call(
        body, out_shape=out_shape, grid=grid, in_specs=in_specs, out_specs=out_spec,
        scratch_shapes=[pltpu.VMEM((bm, bn), F32)], name=name,
        compiler_params=_params("parallel", "parallel", "arbitrary"),
    )(*args)


def _out_proj_loss(mixed, w_out, resid, target, *, name):
    seq, k_dim = mixed.shape
    bm = min(512, seq)
    bn = 1024

    def body(a_ref, b_ref, r_ref, t_ref, dy_ref, loss_ref):
        @pl.when((pl.program_id(0) == 0) & (pl.program_id(1) == 0))
        def _():
            loss_ref[...] = jnp.zeros_like(loss_ref)

        diff = _dot(a_ref[...], b_ref[...]) + r_ref[...] - t_ref[...]
        dy_ref[...] = diff / D_MODEL
        col = jnp.sum(diff * diff, axis=0, keepdims=True)
        part = col[:, :HEAD_DIM]
        for c in range(1, bn // HEAD_DIM):
            part = part + col[:, _hs(c)]
        loss_ref[...] += part * (0.5 / D_MODEL)

    tile = pl.BlockSpec((bm, bn), lambda i, j: (i, j))
    return pl.pallas_call(
        body,
        out_shape=[jax.ShapeDtypeStruct((seq, D_MODEL), F32), jax.ShapeDtypeStruct((1, HEAD_DIM), F32)],
        grid=(seq // bm, D_MODEL // bn),
        in_specs=[pl.BlockSpec((bm, k_dim), lambda i, j: (i, 0)), pl.BlockSpec((k_dim, bn), lambda i, j: (0, j)),
                  tile, tile],
        out_specs=[tile, pl.BlockSpec((1, HEAD_DIM), lambda i, j: (0, 0))],
        name=name, compiler_params=_params("arbitrary", "arbitrary"),
    )(mixed, w_out, resid, target)


def _rowwise(body, n_rows, ins, outs, accs=(), *, name, block=ROW_BLOCK):
    blk = min(block, n_rows)
    assert n_rows % blk == 0
    in_specs = []
    for arr, is_row in ins:
        if is_row:
            assert arr.shape[0] == n_rows, (name, arr.shape, n_rows)
            width = arr.shape[1] if is_row is True else is_row
            in_specs.append(pl.BlockSpec((blk, width), lambda i: (i, 0)))
        else:
            in_specs.append(pl.BlockSpec(arr.shape, lambda i, nd=arr.ndim: (0,) * nd))
    out_shape = [jax.ShapeDtypeStruct((n_rows, w), dt) for w, dt in outs]
    out_specs = [pl.BlockSpec((blk, w), lambda i: (i, 0)) for w, _ in outs]
    out_shape += [jax.ShapeDtypeStruct(s, dt) for s, dt in accs]
    out_specs += [pl.BlockSpec(s, lambda i, nd=len(s): (0,) * nd) for s, _ in accs]
    n_in, n_out, n_acc = len(ins), len(outs), len(accs)

    def kern(*refs):
        in_refs = refs[:n_in]
        out_refs = refs[n_in:n_in + n_out]
        acc_refs = refs[n_in + n_out:]
        if n_acc:
            @pl.when(pl.program_id(0) == 0)
            def _():
                for r in acc_refs:
                    r[...] = jnp.zeros_like(r)
        body(in_refs, out_refs, acc_refs)

    return pl.pallas_call(
        kern, out_shape=out_shape, grid=(n_rows // blk,), in_specs=in_specs, out_specs=out_specs,
        name=name, compiler_params=_params("arbitrary"),
    )(*[arr for arr, _ in ins])


def _rms(x, g, n=None):
    n = x.shape[-1] if n is None else n
    r = lax.rsqrt(jnp.sum(x * x, axis=-1, keepdims=True) / n + EPS)
    return x * r * g


def _rms_bwd(x, g, dy, n=None):
    n = x.shape[-1] if n is None else n
    r = lax.rsqrt(jnp.sum(x * x, axis=-1, keepdims=True) / n + EPS)
    gdy = dy * g
    dx = r * (gdy - x * ((r * r) * (jnp.sum(gdy * x, axis=-1, keepdims=True) / n)))
    dg = jnp.sum(dy * x * r, axis=0, keepdims=True)
    return dx, dg


def _swap_halves(x):
    lane = lax.broadcasted_iota(jnp.int32, x.shape, 1)
    return jnp.where(lane < ROPE_DIM // 2, pltpu.roll(x, 128 - ROPE_DIM // 2, 1),
                     pltpu.roll(x, ROPE_DIM // 2, 1))


def _rope(n, cos_t, sin_t):
    return n * cos_t + _swap_halves(n) * sin_t


def _rope_bwd(dy, cos_t, sin_t):
    return dy * cos_t - _swap_halves(dy) * sin_t


def _sigmoid(g):
    return 1.0 / (1.0 + jnp.exp(-g))


def _dot_t(a, b):
    return lax.dot_general(a, b, (((1,), (1,)), ((), ())), preferred_element_type=F32)


def _tdot(a, b):
    return lax.dot_general(a, b, (((0,), (0,)), ((), ())), preferred_element_type=F32)


def _dot(a, b):
    return jnp.dot(a, b, preferred_element_type=F32)


def _hs(h, w=HEAD_DIM, base=0):
    return slice(base + h * w, base + (h + 1) * w)


def _mem_head(qm, gq, mk_h, mv_h):
    qb = _rms(qm, gq).astype(BF16)
    s = _dot_t(qb, mk_h) * (HEAD_DIM ** -0.5)
    e = jnp.exp(s - jnp.max(s, axis=-1, keepdims=True))
    p = e / jnp.sum(e, axis=-1, keepdims=True)
    mo = _dot(p.astype(BF16), mv_h)
    return qb, p, mo


def _mix_fwd(att, gates, c0, mk, mv, gq, *, name):
    n_rows = att.shape[0]

    def body(ins, outs, _):
        att_ref, g_ref, mk_ref, mv_ref, gq_ref = ins
        (o_ref,) = outs
        g = g_ref[:, c0:c0 + SB_W]
        o_ref[:, :SB_W] = (att_ref[...] * (g * _sigmoid(g))).astype(BF16)
        for h in range(N_MEM_HEADS):
            qm = g_ref[:, _hs(h, base=c0 + SB_W)]
            gm = g_ref[:, _hs(h, base=c0 + SB_W + MEM_W)]
            _, _, mo = _mem_head(qm, gq_ref[...], mk_ref[:, _hs(h)], mv_ref[:, _hs(h)])
            o_ref[:, _hs(h, base=SB_W)] = (mo * (gm * _sigmoid(gm))).astype(BF16)

    (mixed,) = _rowwise(body, n_rows,
                        [(att, True), (gates, True), (mk, False), (mv, False), (gq, False)],
                        [(D_MODEL, BF16)], name=name)
    return mixed


def _mix_bwd(dmixed, att, gates, c0, mk, mv, gq, *, name):
    n_rows = att.shape[0]
    scale = HEAD_DIM ** -0.5

    def body(ins, outs, accs):
        dm_ref, att_ref, g_ref, mk_ref, mv_ref, gq_ref = ins
        datt_ref, dg_ref = outs
        dmk_ref, dmv_ref, dgq_ref = accs
        g = g_ref[:, c0:c0 + SB_W]
        sg = _sigmoid(g)
        dm = dm_ref[:, :SB_W]
        datt_ref[...] = dm * (g * sg)
        dg_ref[:, :SB_W] = (dm * att_ref[...] * (sg * (1.0 + g * (1.0 - sg)))).astype(BF16)
        for h in range(N_MEM_HEADS):
            qm = g_ref[:, _hs(h, base=c0 + SB_W)]
            gm = g_ref[:, _hs(h, base=c0 + SB_W + MEM_W)]
            mk_h = mk_ref[:, _hs(h)]
            mv_h = mv_ref[:, _hs(h)]
            qb, p, mo = _mem_head(qm, gq_ref[...], mk_h, mv_h)
            sgm = _sigmoid(gm)
            dmh = dm_ref[:, _hs(h, base=SB_W)]
            dmo = dmh * (gm * sgm)
            dg_ref[:, _hs(h, base=SB_W + MEM_W)] = (
                dmh * mo * (sgm * (1.0 + gm * (1.0 - sgm)))).astype(BF16)
            dmo_b = dmo.astype(BF16)
            pb = p.astype(BF16)
            dp = _dot_t(dmo_b, mv_h)
            dmv_ref[:, _hs(h)] += _tdot(pb, dmo_b)
            ds = (p * (dp - jnp.sum(dp * p, axis=-1, keepdims=True)) * scale).astype(BF16)
            dqn = _dot(ds, mk_h)
            dmk_ref[:, _hs(h)] += _tdot(ds, qb)
            dqm, dgq = _rms_bwd(qm, gq_ref[...], dqn)
            dg_ref[:, _hs(h, base=SB_W)] = dqm.astype(BF16)
            dgq_ref[...] += dgq

    return _rowwise(body, n_rows,
                    [(dmixed, True), (att, True), (gates, True), (mk, False), (mv, False), (gq, False)],
                    [(SB_W, F32), (GATE_W, BF16)],
                    [((MEM_LEN, MEM_W), F32), ((MEM_LEN, MEM_W), F32), ((1, HEAD_DIM), F32)],
                    name=name)


def _mem_side_fwd(mem, g_norm, w_kv, g_k, *, tag):
    def norm_body(ins, outs, _):
        outs[0][...] = _rms(ins[0][...], ins[1][...]).astype(BF16)

    (mn,) = _rowwise(norm_body, MEM_LEN, [(mem, True), (g_norm, False)], [(D_MODEL, BF16)],
                     name=f"mem_norm_{tag}")
    mkv = _mm(mn, w_kv, name=f"mem_kv_{tag}")

    def kv_body(ins, outs, _):
        mkv_ref, gk_ref = ins
        mk_ref, mv_ref = outs
        for h in range(N_MEM_HEADS):
            mk_ref[:, _hs(h)] = _rms(mkv_ref[:, _hs(h)], gk_ref[...]).astype(BF16)
        mv_ref[...] = mkv_ref[:, MEM_W:].astype(BF16)

    mk, mv = _rowwise(kv_body, MEM_LEN, [(mkv, True), (g_k, False)], [(MEM_W, BF16), (MEM_W, BF16)],
                      name=f"mem_kv_prep_{tag}")
    return mn, mkv, mk, mv


def _mem_side_bwd(mem, g_norm, w_kv, g_k, mn, mkv, dmk, dmv, *, tag):
    def kv_body(ins, outs, accs):
        mkv_ref, gk_ref, dmk_ref, dmv_ref = ins
        (d_ref,) = outs
        (dgk_ref,) = accs
        for h in range(N_MEM_HEADS):
            dx, dg = _rms_bwd(mkv_ref[:, _hs(h)], gk_ref[...], dmk_ref[:, _hs(h)])
            d_ref[:, _hs(h)] = dx.astype(BF16)
            dgk_ref[...] += dg
        d_ref[:, MEM_W:] = dmv_ref[...].astype(BF16)

    dmkv, dgk = _rowwise(kv_body, MEM_LEN, [(mkv, True), (g_k, False), (dmk, True), (dmv, True)],
                         [(2 * MEM_W, BF16)], [((1, HEAD_DIM), F32)], name=f"mem_kv_prep_bwd_{tag}")
    dmn = _mm(dmkv, w_kv, tb=True, name=f"mem_kv_dx_{tag}")
    dw = _mm(mn, dmkv, ta=True, out_dtype=BF16, name=f"mem_kv_dw_{tag}")

    def norm_body(ins, outs, accs):
        _, dg = _rms_bwd(ins[0][...], ins[1][...], ins[2][...])
        accs[0][...] += dg

    (dgn,) = _rowwise(norm_body, MEM_LEN, [(mem, True), (g_norm, False), (dmn, True)], [],
                      [((1, D_MODEL), F32)], name=f"mem_norm_bwd_{tag}")
    return dw, dgn, dgk


LOG2_E = 1.4426950408889634
SB_Q_SCALE = HEAD_DIM ** -0.5 * LOG2_E


Z2_CAP = 126.0


def _sb_terms(z2):
    zc = jnp.minimum(z2, Z2_CAP)
    w = 1.0 + jnp.exp2(zc)
    return zc, w, jnp.log2(w)


LOOP_UNROLL = 8


def _loop_blocks(base, qb, body, carry, *, reverse):
    def run(start, trips, unroll, c0):
        def trip(t, c):
            for u in range(unroll):
                p = start + t * unroll + u
                c = body(base - 1 - p if reverse else p, c)
            return c
        return lax.fori_loop(0, trips, trip, c0)

    if qb % LOOP_UNROLL == 0:
        return run(0, base // LOOP_UNROLL, LOOP_UNROLL, carry)
    small = qb
    n_big = base // LOOP_UNROLL
    carry = run(0, n_big, LOOP_UNROLL, carry)
    return run(n_big * LOOP_UNROLL, (base - n_big * LOOP_UNROLL) // small, small, carry)


def _chain_modes(s, qb):
    return tuple(None if t < s else ("m" if t == s else "f") for t in range(qb))


def _split_dot(x, tri2):
    hi = x.astype(BF16)
    lo = (x - hi.astype(F32)).astype(BF16)
    return _dot(jnp.concatenate([hi, lo], axis=1), tri2)


def _sb_fwd(qkv, *, name, hp=1):
    seq = qkv.shape[0]
    blk = min(ATT_BLOCK, seq)
    nkb = seq // blk
    qb = _pick(nkb, (4, 2, 1))
    rows = qb * blk
    chains = [(t, s) for t in range(hp) for s in range(qb)]

    def body(q_ref, k_ref, v_ref, o_ref):
        base = pl.program_id(1) * qb
        qs = {(t, s): q_ref[s * blk:(s + 1) * blk, _hs(t)] for t, s in chains}
        row = lax.broadcasted_iota(jnp.int32, (blk, blk), 0)
        col = lax.broadcasted_iota(jnp.int32, (blk, blk), 1)
        after = (row > col).astype(BF16)
        after2 = jnp.concatenate([after, after], axis=0)
        causal = col < row

        def step(j, carry, modes):
            off = pl.multiple_of(j * blk, blk)
            act = [c for c in chains if modes[c[1]]]
            zs, ls = {}, {}
            for c in act:
                zs[c], _, l = _sb_terms(_dot_t(qs[c], k_ref[pl.ds(off, blk), _hs(c[0])]))
                ls[c] = jnp.where(causal, l, 0.0) if modes[c[1]] == "m" else l
            cs = {c: _split_dot(ls[c], after2) for c in act}
            carry = dict(carry)
            for c in act:
                run, acc = carry[c]
                a = jnp.exp2(zs[c] - ls[c] - cs[c] - run)
                if modes[c[1]] == "m":
                    a = jnp.where(causal, a, 0.0)
                acc = acc + _dot(a.astype(BF16), v_ref[pl.ds(off, blk), _hs(c[0])])
                carry[c] = (run + (cs[c][:, :1] + ls[c][:, :1]), acc)
            return carry

        init = (jnp.zeros((blk, 1), F32), jnp.zeros((blk, HEAD_DIM), F32))
        carry = {c: init for c in chains}
        for s in reversed(range(qb)):
            carry = step(base + s, carry, _chain_modes(s, qb))
        carry = _loop_blocks(base, qb, lambda j, c: step(j, c, ("f",) * qb), carry, reverse=True)
        for t, s in chains:
            o_ref[s * blk:(s + 1) * blk, _hs(t)] = carry[(t, s)][1]

    nh = N_SB_HEADS // hp
    return pl.pallas_call(
        body, out_shape=jax.ShapeDtypeStruct((seq, SB_W), F32), grid=(nh, nkb // qb),
        in_specs=[pl.BlockSpec((rows, hp * HEAD_DIM), lambda h, i: (i, h)),
                  pl.BlockSpec((seq, hp * HEAD_DIM), lambda h, i: (0, nh + h)),
                  pl.BlockSpec((seq, hp * HEAD_DIM), lambda h, i: (0, 2 * nh + h))],
        out_specs=pl.BlockSpec((rows, hp * HEAD_DIM), lambda h, i: (i, h)),
        name=name, compiler_params=_params("parallel", "arbitrary"),
    )(qkv, qkv, qkv)


SB_BWD_GROUP = 4


def _sb_bwd(qkv, out, dout, *, name):
    seq = qkv.shape[0]
    blk = min(ATT_BLOCK, seq)
    nkb = seq // blk
    qb = _pick(nkb, (4, 2, 1))
    rows = qb * blk
    scale = HEAD_DIM ** -0.5

    def body(q_ref, k_ref, v_ref, do_ref, o_ref, dq_ref, dk_out, dv_out, dk_ref, dv_ref):
        g = pl.program_id(1)
        base = g * qb

        @pl.when(g == 0)
        def _():
            dk_ref[...] = jnp.zeros_like(dk_ref)
            dv_ref[...] = jnp.zeros_like(dv_ref)

        qs = [q_ref[t * blk:(t + 1) * blk, :] for t in range(qb)]
        dos = [do_ref[t * blk:(t + 1) * blk, :].astype(BF16) for t in range(qb)]
        totals = [jnp.sum(dos[t].astype(F32) * o_ref[t * blk:(t + 1) * blk, :], axis=-1, keepdims=True)
                  for t in range(qb)]
        row = lax.broadcasted_iota(jnp.int32, (blk, blk), 0)
        col = lax.broadcasted_iota(jnp.int32, (blk, blk), 1)
        after = (row > col).astype(BF16)
        after2 = jnp.concatenate([after, after], axis=0)
        from_s = (row >= col).astype(BF16)
        from_s2 = jnp.concatenate([from_s, from_s], axis=0)
        causal = col < row

        def step(j, carry, modes):
            runs, rights, dqs = list(carry[0]), list(carry[1]), list(carry[2])
            off = pl.multiple_of(j * blk, blk)
            kb = k_ref[pl.ds(off, blk), :]
            vb = v_ref[pl.ds(off, blk), :]
            dv_inc = dk_inc = None
            for first in range(0, qb, SB_BWD_GROUP):
                act = [t for t in range(first, min(first + SB_BWD_GROUP, qb)) if modes[t]]
                zs, ls, sns = {}, {}, {}
                for t in act:
                    zs[t], w, l = _sb_terms(_dot_t(qs[t], kb))
                    sns[t] = pl.reciprocal(w, approx=True)
                    ls[t] = jnp.where(causal, l, 0.0) if modes[t] == "m" else l
                cs = {t: _split_dot(ls[t], after2) for t in act}
                das = {t: _dot_t(dos[t], vb) for t in act}
                abs_, des = {}, {}
                for t in act:
                    a = jnp.exp2(zs[t] - ls[t] - cs[t] - runs[t])
                    if modes[t] == "m":
                        a = jnp.where(causal, a, 0.0)
                    abs_[t] = a.astype(BF16)
                    des[t] = abs_[t].astype(F32) * das[t]
                sufs = {t: _split_dot(des[t], from_s2) for t in act}
                for t in act:
                    left = totals[t] - (sufs[t] + rights[t])
                    dz = (des[t] + left) * sns[t] - left
                    if modes[t] == "m":
                        dz = jnp.where(causal, dz, 0.0)
                    dzb = dz.astype(BF16)
                    dqs[t] = dqs[t] + _dot(dzb, kb)
                    inc_v = _tdot(abs_[t], dos[t])
                    inc_k = _tdot(dzb, qs[t])
                    dv_inc = inc_v if dv_inc is None else dv_inc + inc_v
                    dk_inc = inc_k if dk_inc is None else dk_inc + inc_k
                    runs[t] = runs[t] + (cs[t][:, :1] + ls[t][:, :1])
                    rights[t] = rights[t] + sufs[t][:, :1]
            dv_ref[pl.ds(off, blk), :] += dv_inc
            dk_ref[pl.ds(off, blk), :] += dk_inc
            return tuple(runs), tuple(rights), tuple(dqs)

        zero = (jnp.zeros((blk, 1), F32),) * qb
        carry = (zero, zero, (jnp.zeros((blk, HEAD_DIM), F32),) * qb)
        for s in reversed(range(qb)):
            carry = step(base + s, carry, _chain_modes(s, qb))
        carry = _loop_blocks(base, qb, lambda j, c: step(j, c, ("f",) * qb), carry, reverse=True)
        for t in range(qb):
            dq_ref[t * blk:(t + 1) * blk, :] = (carry[2][t] * scale).astype(BF16)

        @pl.when(g == pl.num_programs(1) - 1)
        def _():
            dk_out[...] = (dk_ref[...] * (1.0 / LOG2_E)).astype(BF16)
            dv_out[...] = dv_ref[...].astype(BF16)

    out_sd = jax.ShapeDtypeStruct((seq, SB_W), BF16)
    return pl.pallas_call(
        body, out_shape=[out_sd, out_sd, out_sd], grid=(N_SB_HEADS, nkb // qb),
        in_specs=[pl.BlockSpec((rows, HEAD_DIM), lambda h, i: (i, h)),
                  pl.BlockSpec((seq, HEAD_DIM), lambda h, i: (0, N_SB_HEADS + h)),
                  pl.BlockSpec((seq, HEAD_DIM), lambda h, i: (0, 2 * N_SB_HEADS + h)),
                  pl.BlockSpec((rows, HEAD_DIM), lambda h, i: (i, h)),
                  pl.BlockSpec((rows, HEAD_DIM), lambda h, i: (i, h))],
        out_specs=[pl.BlockSpec((rows, HEAD_DIM), lambda h, i: (i, h)),
                   pl.BlockSpec((seq, HEAD_DIM), lambda h, i: (0, h)),
                   pl.BlockSpec((seq, HEAD_DIM), lambda h, i: (0, h))],
        scratch_shapes=[pltpu.VMEM((seq, HEAD_DIM), F32), pltpu.VMEM((seq, HEAD_DIM), F32)],
        name=name, compiler_params=_params("parallel", "arbitrary"),
    )(qkv, qkv, qkv, dout, out)


MLA_SCALE = (HEAD_DIM + ROPE_DIM) ** -0.5
MLA_Q_SCALE = MLA_SCALE * LOG2_E


def _mla_fwd(q_cat, k_cat, v, *, name, hp=1):
    seq = q_cat.shape[0]
    blk = min(ATT_BLOCK, seq)
    nkb = seq // blk
    qb = _pick(nkb, (4, 2, 1))
    rows = qb * blk
    chains = [(t, s) for t in range(hp) for s in range(qb)]

    def body(q_ref, k_ref, v_ref, o_ref, lse_ref):
        base = pl.program_id(1) * qb
        qs = {(t, s): q_ref[s * blk:(s + 1) * blk, t * CAT_W:(t + 1) * CAT_W] for t, s in chains}
        row = lax.broadcasted_iota(jnp.int32, (blk, blk), 0)
        col = lax.broadcasted_iota(jnp.int32, (blk, blk), 1)
        causal = col <= row
        ones = jnp.ones((blk, HEAD_DIM), BF16)

        def step(j, carry, modes):
            off = pl.multiple_of(j * blk, blk)
            act = [c for c in chains if modes[c[1]]]
            ss = {c: _dot_t(qs[c], k_ref[pl.ds(off, blk), c[0] * CAT_W:(c[0] + 1) * CAT_W]) for c in act}
            carry = dict(carry)
            for c in act:
                m, l, acc = carry[c]
                s = ss[c]
                if modes[c[1]] == "m":
                    s = jnp.where(causal, s, -jnp.inf)
                m_new = jnp.maximum(m, jnp.max(s, axis=-1, keepdims=True))
                pb = jnp.exp2(s - m_new).astype(BF16)
                alpha = jnp.exp2(m - m_new)
                both = _dot(pb, jnp.concatenate([v_ref[pl.ds(off, blk), _hs(c[0])], ones], axis=1))
                l = alpha * l + both[:, HEAD_DIM:HEAD_DIM + 1]
                acc = alpha * acc + both[:, :HEAD_DIM]
                carry[c] = (m_new, l, acc)
            return carry

        init = (jnp.full((blk, 1), -jnp.inf, F32), jnp.zeros((blk, 1), F32),
                jnp.zeros((blk, HEAD_DIM), F32))
        carry = {c: init for c in chains}
        carry = _loop_blocks(base, qb, lambda j, c: step(j, c, ("f",) * qb), carry, reverse=False)
        for s in range(qb):
            carry = step(base + s, carry, _chain_modes(s, qb))
        for t, s in chains:
            m, l, acc = carry[(t, s)]
            o_ref[s * blk:(s + 1) * blk, _hs(t)] = acc / l
            lse_ref[s * blk:(s + 1) * blk, _hs(t)] = jnp.broadcast_to(
                (m + jnp.log2(l)) * (1.0 / LOG2_E), (blk, HEAD_DIM))

    out = jax.ShapeDtypeStruct((seq, MLA_W), F32)
    return pl.pallas_call(
        body, out_shape=[out, out], grid=(N_MLA_HEADS // hp, nkb // qb),
        in_specs=[pl.BlockSpec((rows, hp * CAT_W), lambda h, i: (i, h)),
                  pl.BlockSpec((seq, hp * CAT_W), lambda h, i: (0, h)),
                  pl.BlockSpec((seq, hp * HEAD_DIM), lambda h, i: (0, h))],
        out_specs=[pl.BlockSpec((rows, hp * HEAD_DIM), lambda h, i: (i, h)),
                   pl.BlockSpec((rows, hp * HEAD_DIM), lambda h, i: (i, h))],
        name=name, compiler_params=_params("parallel", "arbitrary"),
    )(q_cat, k_cat, v)


def _mla_bwd(q_cat, k_cat, v, out, lse, dout, *, name):
    seq = q_cat.shape[0]
    blk = min(ATT_BLOCK, seq)
    nkb = seq // blk
    qb = _pick(nkb, (4, 2, 1))
    rows = qb * blk

    def body(q_ref, k_ref, v_ref, o_ref, lse_ref, do_ref, dq_ref, dk_ref, dv_ref):
        g = pl.program_id(1)
        base = g * qb

        @pl.when(g == 0)
        def _():
            dk_ref[...] = jnp.zeros_like(dk_ref)
            dv_ref[...] = jnp.zeros_like(dv_ref)

        qs, dobs, deltas, lses = [], [], [], []
        for t in range(qb):
            rs = slice(t * blk, (t + 1) * blk)
            do = do_ref[rs, :]
            qs.append(q_ref[rs, :])
            dobs.append(do.astype(BF16))
            deltas.append(jnp.sum(do * o_ref[rs, :], axis=-1, keepdims=True))
            lses.append(lse_ref[rs, :1] * LOG2_E)
        row = lax.broadcasted_iota(jnp.int32, (blk, blk), 0)
        col = lax.broadcasted_iota(jnp.int32, (blk, blk), 1)
        causal = col <= row

        def step(j, dqs, modes):
            off = pl.multiple_of(j * blk, blk)
            kb = k_ref[pl.ds(off, blk), :]
            vb = v_ref[pl.ds(off, blk), :]
            act = [t for t in range(qb) if modes[t]]
            ss = {t: _dot_t(qs[t], kb) for t in act}
            dps = {t: _dot_t(dobs[t], vb) for t in act}
            dqs = list(dqs)
            dv_inc = dk_inc = None
            for t in act:
                p = jnp.exp2(ss[t] - lses[t])
                if modes[t] == "m":
                    p = jnp.where(causal, p, 0.0)
                ds = (p * (dps[t] - deltas[t])).astype(BF16)
                inc_v = _tdot(p.astype(BF16), dobs[t])
                inc_k = _tdot(ds, qs[t])
                dv_inc = inc_v if dv_inc is None else dv_inc + inc_v
                dk_inc = inc_k if dk_inc is None else dk_inc + inc_k
                dqs[t] = dqs[t] + _dot(ds, kb)
            dv_ref[pl.ds(off, blk), :] += dv_inc
            dk_ref[pl.ds(off, blk), :] += dk_inc
            return tuple(dqs)

        dqs = (jnp.zeros((blk, CAT_W), F32),) * qb
        dqs = _loop_blocks(base, qb, lambda j, c: step(j, c, ("f",) * qb), dqs, reverse=False)
        for s in range(qb):
            modes = tuple(None if t < s else ("m" if t == s else "f") for t in range(qb))
            dqs = step(base + s, dqs, modes)
        for t in range(qb):
            dq_ref[t * blk:(t + 1) * blk, :] = dqs[t] * MLA_SCALE

        @pl.when(g == pl.num_programs(1) - 1)
        def _():
            dk_ref[...] = dk_ref[...] * (1.0 / LOG2_E)

    return pl.pallas_call(
        body,
        out_shape=[jax.ShapeDtypeStruct((seq, N_MLA_HEADS * CAT_W), F32),
                   jax.ShapeDtypeStruct((seq, N_MLA_HEADS * CAT_W), F32),
                   jax.ShapeDtypeStruct((seq, MLA_W), F32)],
        grid=(N_MLA_HEADS, nkb // qb),
        in_specs=[pl.BlockSpec((rows, CAT_W), lambda h, i: (i, h)),
                  pl.BlockSpec((seq, CAT_W), lambda h, i: (0, h)),
                  pl.BlockSpec((seq, HEAD_DIM), lambda h, i: (0, h)),
                  pl.BlockSpec((rows, HEAD_DIM), lambda h, i: (i, h)),
                  pl.BlockSpec((rows, HEAD_DIM), lambda h, i: (i, h)),
                  pl.BlockSpec((rows, HEAD_DIM), lambda h, i: (i, h))],
        out_specs=[pl.BlockSpec((rows, CAT_W), lambda h, i: (i, h)),
                   pl.BlockSpec((seq, CAT_W), lambda h, i: (0, h)),
                   pl.BlockSpec((seq, HEAD_DIM), lambda h, i: (0, h))],
        name=name, compiler_params=_params("parallel", "arbitrary"),
    )(q_cat, k_cat, v, out, lse, dout)


def _local_step(x, mem, positions, target, w, g):
    seq = x.shape[0]
    inv_freq = jnp.power(ROPE_THETA, -jnp.arange(0, ROPE_DIM, 2, dtype=F32) / ROPE_DIM)
    ang = positions.astype(F32)[:, None] * inv_freq
    cos, sin = jnp.cos(ang), jnp.sin(ang)
    lane_pad = jnp.zeros((seq, HEAD_DIM - ROPE_DIM), F32)
    cos_t = jnp.concatenate([cos, cos, lane_pad], axis=1)
    sin_t = jnp.concatenate([-sin, sin, lane_pad], axis=1)
    gain_pad = jnp.zeros((1, HEAD_DIM - ROPE_DIM), F32)
    g_k_rope = jnp.concatenate([g["g_k_rope"], gain_pad], axis=1)
    g_q_rope = jnp.concatenate([g["b_g_q_rope"], gain_pad], axis=1)

    def norm_to_bf16(src, gain, name):
        def body(ins, outs, _):
            outs[0][...] = _rms(ins[0][...], ins[1][...]).astype(BF16)
        return _rowwise(body, seq, [(src, True), (gain, False)], [(src.shape[1], BF16)], name=name)[0]

    h_a = norm_to_bf16(x, g["a_norm"], "a_norm_fwd")
    qkv = _mm(h_a, w["a_in_qkv"], out_dtype=BF16, scale_cols=(SB_W, SB_Q_SCALE), name="a_in_qkv")
    gr = _mm(h_a, w["a_in_gate"], name="a_in_gate")
    sb = _sb_fwd(qkv, name="sb_fwd")
    mem0 = _mem_side_fwd(mem, g["mem_norm"][0:1], w["mem_kv"][0], g["g_mem_k"][0:1], tag="a")
    mixed_a = _mix_fwd(sb, gr, 0, mem0[2], mem0[3], g["g_mem_q"][0:1], name="a_mix_fwd")
    x1 = _mm(mixed_a, w["a_out"], res=x, name="a_out")

    def norms2_body(ins, outs, _):
        xv = ins[0][...]
        outs[0][...] = _rms(xv, ins[1][...]).astype(BF16)
        outs[1][...] = _rms(xv, ins[2][...]).astype(BF16)

    h_kv, h_b = _rowwise(norms2_body, seq, [(x1, True), (g["kv_norm"], False), (g["b_norm"], False)],
                         [(D_MODEL, BF16), (D_MODEL, BF16)], name="kv_b_norm_fwd")
    ckr = _mm(h_kv, w["dkv"], name="dkv")

    def ckr_body(ins, outs, _):
        ckr_ref, gc_ref, gr_ref, c_ref, s_ref = ins
        outs[0][...] = _rms(ckr_ref[:, :KV_LORA], gc_ref[...]).astype(BF16)
        kr = _rms(ckr_ref[:, KV_LORA:], gr_ref[...], n=ROPE_DIM)
        outs[1][...] = _rope(kr, c_ref[...], s_ref[...]).astype(BF16)

    c_n, k_r = _rowwise(ckr_body, seq,
                        [(ckr, True), (g["g_ckv"], False), (g_k_rope, False), (cos_t, True), (sin_t, True)],
                        [(KV_LORA, BF16), (HEAD_DIM, BF16)], name="ckv_prep_fwd")
    kv = _mm(c_n, w["ukv"], name="ukv")

    def kcat_body(ins, outs, _):
        kv_ref, kr_ref, gk_ref = ins
        kc_ref, v_ref = outs
        for h in range(N_MLA_HEADS):
            kc_ref[:, h * CAT_W:h * CAT_W + HEAD_DIM] = _rms(
                kv_ref[:, h * CAT_W:h * CAT_W + HEAD_DIM], gk_ref[...]).astype(BF16)
            kc_ref[:, h * CAT_W + HEAD_DIM:(h + 1) * CAT_W] = kr_ref[...]
            v_ref[:, _hs(h)] = kv_ref[:, h * CAT_W + HEAD_DIM:(h + 1) * CAT_W].astype(BF16)

    k_cat, v_mla = _rowwise(kcat_body, seq, [(kv, True), (k_r, True), (g["g_k_nope"], False)],
                            [(N_MLA_HEADS * CAT_W, BF16), (MLA_W, BF16)], name="k_prep_fwd")

    p2 = _mm(h_b, w["b_in"], name="b_in")

    def qlat_body(ins, outs, _):
        outs[0][...] = _rms(ins[0][:, :Q_LORA], ins[1][...]).astype(BF16)

    (q_l,) = _rowwise(qlat_body, seq, [(p2, Q_LORA), (g["b_g_q_lat"], False)], [(Q_LORA, BF16)],
                      name="q_lat_norm_fwd")
    q_up = _mm(q_l, w["uq"], name="uq")

    def qcat_body(ins, outs, _):
        q_ref, gn_ref, gr_ref, c_ref, s_ref = ins
        (o_ref,) = outs
        for h in range(N_MLA_HEADS):
            o_ref[:, h * CAT_W:h * CAT_W + HEAD_DIM] = (MLA_Q_SCALE * _rms(
                q_ref[:, h * CAT_W:h * CAT_W + HEAD_DIM], gn_ref[...])).astype(BF16)
            qr = _rms(q_ref[:, h * CAT_W + HEAD_DIM:(h + 1) * CAT_W], gr_ref[...], n=ROPE_DIM)
            o_ref[:, h * CAT_W + HEAD_DIM:(h + 1) * CAT_W] = (
                MLA_Q_SCALE * _rope(qr, c_ref[...], s_ref[...])).astype(BF16)

    (q_cat,) = _rowwise(qcat_body, seq,
                        [(q_up, True), (g["b_g_q_nope"], False), (g_q_rope, False), (cos_t, True), (sin_t, True)],
                        [(N_MLA_HEADS * CAT_W, BF16)], name="q_prep_fwd")
    att, lse = _mla_fwd(q_cat, k_cat, v_mla, name="mla_fwd")
    mem1 = _mem_side_fwd(mem, g["mem_norm"][1:2], w["mem_kv"][1], g["g_mem_k"][1:2], tag="b")
    mixed_b = _mix_fwd(att, p2, Q_LORA, mem1[2], mem1[3], g["g_mem_q"][1:2], name="b_mix_fwd")
    dy, loss_part = _out_proj_loss(mixed_b, w["b_out"], x1, target, name="b_out_loss")

    gw, gg = {}, {}
    dmixed_b = _mm(dy, w["b_out"], tb=True, name="b_out_dx")
    gw["b_out"] = _mm(mixed_b, dy, ta=True, out_dtype=BF16, name="b_out_dw")
    datt, dgate_b, dmk1, dmv1, gq1 = _mix_bwd(dmixed_b, att, p2, Q_LORA, mem1[2], mem1[3],
                                              g["g_mem_q"][1:2], name="b_mix_bwd")
    dq_cat, dk_cat, dv_mla = _mla_bwd(q_cat, k_cat, v_mla, att, lse, datt, name="mla_bwd")

    def qcat_bwd_body(ins, outs, accs):
        q_ref, dq_ref, gn_ref, gr_ref, c_ref, s_ref = ins
        (o_ref,) = outs
        dgn_ref, dgr_ref = accs
        for h in range(N_MLA_HEADS):
            dx, dg = _rms_bwd(q_ref[:, h * CAT_W:h * CAT_W + HEAD_DIM], gn_ref[...],
                              dq_ref[:, h * CAT_W:h * CAT_W + HEAD_DIM])
            o_ref[:, h * CAT_W:h * CAT_W + HEAD_DIM] = dx.astype(BF16)
            dgn_ref[...] += dg
            dn = _rope_bwd(dq_ref[:, h * CAT_W + HEAD_DIM:(h + 1) * CAT_W], c_ref[...], s_ref[...])
            dx, dg = _rms_bwd(q_ref[:, h * CAT_W + HEAD_DIM:(h + 1) * CAT_W], gr_ref[...], dn, n=ROPE_DIM)
            o_ref[:, h * CAT_W + HEAD_DIM:(h + 1) * CAT_W] = dx.astype(BF16)
            dgr_ref[...] += dg

    dq_up, gg["b_g_q_nope"], dgqr = _rowwise(
        qcat_bwd_body, seq,
        [(q_up, True), (dq_cat, True), (g["b_g_q_nope"], False), (g_q_rope, False), (cos_t, True), (sin_t, True)],
        [(N_MLA_HEADS * CAT_W, BF16)], [((1, HEAD_DIM), F32), ((1, HEAD_DIM), F32)], name="q_prep_bwd")
    gg["b_g_q_rope"] = dgqr
    dq_l = _mm(dq_up, w["uq"], tb=True, name="uq_dx")
    gw["uq"] = _mm(q_l, dq_up, ta=True, out_dtype=BF16, n_split=N_CHIPS, name="uq_dw")

    def qlat_bwd_body(ins, outs, accs):
        p2_ref, dql_ref, dgate_ref, gl_ref = ins
        dx, dg = _rms_bwd(p2_ref[:, :Q_LORA], gl_ref[...], dql_ref[...])
        outs[0][:, :Q_LORA] = dx.astype(BF16)
        outs[0][:, Q_LORA:] = dgate_ref[...]
        accs[0][...] += dg

    dp2, gg["b_g_q_lat"] = _rowwise(
        qlat_bwd_body, seq, [(p2, Q_LORA), (dq_l, True), (dgate_b, True), (g["b_g_q_lat"], False)],
        [(Q_LORA + GATE_W, BF16)], [((1, Q_LORA), F32)], name="q_lat_norm_bwd")
    dh_b = _mm(dp2, w["b_in"], tb=True, name="b_in_dx")
    gw["b_in"] = _mm(h_b, dp2, ta=True, out_dtype=BF16, n_split=N_CHIPS, name="b_in_dw")

    def kcat_bwd_body(ins, outs, accs):
        kv_ref, dkc_ref, dv_ref, gk_ref = ins
        dkv_ref, dkr_ref = outs
        (dgk_ref,) = accs
        dkr = jnp.zeros(dkr_ref.shape, F32)
        for h in range(N_MLA_HEADS):
            dx, dg = _rms_bwd(kv_ref[:, h * CAT_W:h * CAT_W + HEAD_DIM], gk_ref[...],
                              dkc_ref[:, h * CAT_W:h * CAT_W + HEAD_DIM])
            dkv_ref[:, h * CAT_W:h * CAT_W + HEAD_DIM] = dx.astype(BF16)
            dgk_ref[...] += dg
            dkv_ref[:, h * CAT_W + HEAD_DIM:(h + 1) * CAT_W] = dv_ref[:, _hs(h)].astype(BF16)
            dkr = dkr + dkc_ref[:, h * CAT_W + HEAD_DIM:(h + 1) * CAT_W]
        dkr_ref[...] = dkr

    dkv, dk_r, gg["g_k_nope"] = _rowwise(
        kcat_bwd_body, seq, [(kv, True), (dk_cat, True), (dv_mla, True), (g["g_k_nope"], False)],
        [(N_MLA_HEADS * CAT_W, BF16), (HEAD_DIM, F32)], [((1, HEAD_DIM), F32)], name="k_prep_bwd")
    dc_n = _mm(dkv, w["ukv"], tb=True, name="ukv_dx")
    gw["ukv"] = _mm(c_n, dkv, ta=True, out_dtype=BF16, n_split=N_CHIPS, name="ukv_dw")

    def ckr_bwd_body(ins, outs, accs):
        ckr_ref, dcn_ref, dkr_ref, gc_ref, gr_ref, c_ref, s_ref = ins
        dx, dg = _rms_bwd(ckr_ref[:, :KV_LORA], gc_ref[...], dcn_ref[...])
        outs[0][:, :KV_LORA] = dx.astype(BF16)
        accs[0][...] += dg
        dn = _rope_bwd(dkr_ref[...], c_ref[...], s_ref[...])
        dx, dg = _rms_bwd(ckr_ref[:, KV_LORA:], gr_ref[...], dn, n=ROPE_DIM)
        outs[0][:, KV_LORA:] = dx.astype(BF16)
        accs[1][...] += dg

    dckr, gg["g_ckv"], gg["g_k_rope"] = _rowwise(
        ckr_bwd_body, seq,
        [(ckr, True), (dc_n, True), (dk_r, True), (g["g_ckv"], False), (g_k_rope, False),
         (cos_t, True), (sin_t, True)],
        [(KV_LORA + HEAD_DIM, BF16)], [((1, KV_LORA), F32), ((1, HEAD_DIM), F32)], name="ckv_prep_bwd")
    dh_kv = _mm(dckr, w["dkv"], tb=True, name="dkv_dx")
    gw["dkv"] = _mm(h_kv, dckr, ta=True, out_dtype=BF16, name="dkv_dw")

    def norms2_bwd_body(ins, outs, accs):
        x_ref, dy_ref, dhk_ref, dhb_ref, gk_ref, gb_ref = ins
        xv = x_ref[...]
        dxk, dgk = _rms_bwd(xv, gk_ref[...], dhk_ref[...])
        dxb, dgb = _rms_bwd(xv, gb_ref[...], dhb_ref[...])
        outs[0][...] = dy_ref[...] + dxk + dxb
        accs[0][...] += dgk
        accs[1][...] += dgb

    dx1, gg["kv_norm"], gg["b_norm"] = _rowwise(
        norms2_bwd_body, seq,
        [(x1, True), (dy, True), (dh_kv, True), (dh_b, True), (g["kv_norm"], False), (g["b_norm"], False)],
        [(D_MODEL, F32)], [((1, D_MODEL), F32), ((1, D_MODEL), F32)], name="kv_b_norm_bwd")

    dmixed_a = _mm(dx1, w["a_out"], tb=True, name="a_out_dx")
    gw["a_out"] = _mm(mixed_a, dx1, ta=True, out_dtype=BF16, name="a_out_dw")
    dsb, dgate_a, dmk0, dmv0, gq0 = _mix_bwd(dmixed_a, sb, gr, 0, mem0[2], mem0[3],
                                             g["g_mem_q"][0:1], name="a_mix_bwd")
    dq, dk, dv = _sb_bwd(qkv, sb, dsb, name="sb_bwd")
    dp_a = jnp.concatenate([dq, dk, dv, dgate_a], axis=1)
    dh_a = _mm(dp_a, w["a_in"], tb=True, name="a_in_dx")
    gw["a_in"] = _mm(h_a, dp_a, ta=True, out_dtype=BF16, n_split=N_CHIPS, name="a_in_dw")

    def norm_a_bwd_body(ins, outs, accs):
        dx, dg = _rms_bwd(ins[0][...], ins[3][...], ins[2][...])
        outs[0][...] = ins[1][...] + dx
        accs[0][...] += dg

    grad_x, gg["a_norm"] = _rowwise(
        norm_a_bwd_body, seq, [(x, True), (dx1, True), (dh_a, True), (g["a_norm"], False)],
        [(D_MODEL, F32)], [((1, D_MODEL), F32)], name="a_norm_bwd")

    dw0, dgn0, dgk0 = _mem_side_bwd(mem, g["mem_norm"][0:1], w["mem_kv"][0], g["g_mem_k"][0:1],
                                    mem0[0], mem0[1], dmk0, dmv0, tag="a")
    dw1, dgn1, dgk1 = _mem_side_bwd(mem, g["mem_norm"][1:2], w["mem_kv"][1], g["g_mem_k"][1:2],
                                    mem1[0], mem1[1], dmk1, dmv1, tag="b")
    gw["mem_kv"] = (dw0, dw1)
    gg["mem_norm"] = jnp.concatenate([dgn0, dgn1], axis=0)
    gg["g_mem_q"] = jnp.concatenate([gq0, gq1], axis=0)
    gg["g_mem_k"] = jnp.concatenate([dgk0, dgk1], axis=0)
    return loss_part, grad_x, gw, gg


HBM_SPEC = pl.BlockSpec(memory_space=pl.ANY)


def _other_chips():
    x, y = lax.axis_index("x"), lax.axis_index("y")
    return [(1 - x, y), (x, 1 - y), (1 - x, 1 - y)]


def _allgather_chips(shards):
    n = len(shards)
    split = [s.shape[0] % 32 == 0 for s in shards]

    def body(*refs):
        ins, outs = refs[:n], refs[n:2 * n]
        send, recv, fsend, frecv = refs[2 * n:]
        x, y, c = lax.axis_index("x"), lax.axis_index("y"), lax.axis_index("c")
        me = 2 * x + y
        chips = _other_chips()

        def part(ref, wi):
            if not split[wi]:
                return ref
            half = shards[wi].shape[0] // 2
            return ref.at[pl.ds(pl.multiple_of(c * half, 16), half)]

        def ici(wi, k, src_chip, to):
            return pltpu.make_async_remote_copy(
                src_ref=part(ins[wi], wi), dst_ref=part(outs[wi].at[src_chip], wi),
                send_sem=send.at[wi, k], recv_sem=recv.at[wi, k], device_id=to, device_id_type=MESH)

        def d2d(wi, k, src_chip):
            rows = part(outs[wi].at[src_chip], wi)
            return pltpu.make_async_remote_copy(
                src_ref=rows, dst_ref=rows, send_sem=fsend.at[wi, k], recv_sem=frecv.at[wi, k],
                device_id=(x, y, 1 - c), device_id_type=MESH)

        for wi in range(n):
            for k, (tx, ty) in enumerate(chips):
                ici(wi, k, me, (tx, ty, c)).start()
        for wi in range(n):
            for k, (tx, ty) in enumerate(chips):
                landed = ici(wi, k, 2 * tx + ty, (tx, ty, c))
                landed.wait_recv()
                if split[wi]:
                    d2d(wi, k, 2 * tx + ty).start()
        for wi in range(n):
            for k, (tx, ty) in enumerate(chips):
                ici(wi, k, me, (tx, ty, c)).wait_send()
                if split[wi]:
                    fwd = d2d(wi, k, 2 * tx + ty)
                    fwd.wait_send()
                    fwd.wait_recv()

    return pl.pallas_call(
        body, out_shape=[jax.ShapeDtypeStruct((N_CHIPS,) + s.shape, s.dtype) for s in shards],
        in_specs=[HBM_SPEC] * n, out_specs=[HBM_SPEC] * n,
        scratch_shapes=[pltpu.SemaphoreType.DMA((n, 3)), pltpu.SemaphoreType.DMA((n, 3)),
                        pltpu.SemaphoreType.DMA((n, 3)), pltpu.SemaphoreType.DMA((n, 3))],
        name="allgather_weights",
    )(*shards)


def _scatter_to_chips(grads):
    n = len(grads)

    def body(*refs):
        ins, outs = refs[:n], refs[n:2 * n]
        send, recv = refs[2 * n:]
        c = lax.axis_index("c")
        copies = []
        for wi in range(n):
            for k, (tx, ty) in enumerate(_other_chips()):
                cp = pltpu.make_async_remote_copy(
                    src_ref=ins[wi].at[2 * tx + ty], dst_ref=outs[wi].at[k], send_sem=send.at[wi, k],
                    recv_sem=recv.at[wi, k], device_id=(tx, ty, c), device_id_type=MESH)
                cp.start()
                copies.append(cp)
        for cp in copies:
            cp.wait()

    return pl.pallas_call(
        body, out_shape=[jax.ShapeDtypeStruct((3,) + s.shape[1:], s.dtype) for s in grads],
        in_specs=[HBM_SPEC] * n, out_specs=[HBM_SPEC] * n,
        scratch_shapes=[pltpu.SemaphoreType.DMA((n, 3)), pltpu.SemaphoreType.DMA((n, 3))],
        name="scatter_grads",
    )(*grads)


def _halve_with_sibling(grads):
    n = len(grads)
    n_slots = grads[0].shape[0]

    def body(*refs):
        ins, got = refs[:n], refs[n:2 * n]
        send, recv = refs[2 * n:]
        c = lax.axis_index("c")
        sib = (lax.axis_index("x"), lax.axis_index("y"), 1 - c)
        copies = []
        for wi in range(n):
            half = grads[wi].shape[1] // 2
            for s in range(n_slots):
                theirs = ins[wi].at[s, pl.ds(pl.multiple_of((1 - c) * half, 16), half)]
                give = pltpu.make_async_remote_copy(
                    src_ref=theirs, dst_ref=got[wi].at[s], send_sem=send.at[wi, s], recv_sem=recv.at[wi, s],
                    device_id=sib, device_id_type=MESH)
                give.start()
                copies.append(give)
        for cp in copies:
            cp.wait()

    halves = [jax.ShapeDtypeStruct((s.shape[0], s.shape[1] // 2) + s.shape[2:], s.dtype) for s in grads]
    return pl.pallas_call(
        body, out_shape=halves, in_specs=[HBM_SPEC] * n, out_specs=[HBM_SPEC] * n,
        scratch_shapes=[pltpu.SemaphoreType.DMA((n, n_slots)), pltpu.SemaphoreType.DMA((n, n_slots))],
        name="halve_grads_with_sibling",
    )(*grads)


def _swap_with_sibling(parts):
    n = len(parts)

    def body(*refs):
        ins, outs = refs[:n], refs[n:2 * n]
        send, recv = refs[2 * n:]
        sib = (lax.axis_index("x"), lax.axis_index("y"), 1 - lax.axis_index("c"))
        copies = []
        for wi in range(n):
            cp = pltpu.make_async_remote_copy(
                src_ref=ins[wi], dst_ref=outs[wi], send_sem=send.at[wi], recv_sem=recv.at[wi],
                device_id=sib, device_id_type=MESH)
            cp.start()
            copies.append(cp)
        for cp in copies:
            cp.wait()

    return pl.pallas_call(
        body, out_shape=[jax.ShapeDtypeStruct(s.shape, s.dtype) for s in parts],
        in_specs=[HBM_SPEC] * n, out_specs=[HBM_SPEC] * n,
        scratch_shapes=[pltpu.SemaphoreType.DMA((n,)), pltpu.SemaphoreType.DMA((n,))],
        name="swap_grad_halves",
    )(*parts)


def _allreduce_small(vec, loss_row):
    rows = vec.shape[0]

    def body(v_ref, o_ref, buf, send, recv):
        x, y, c = lax.axis_index("x"), lax.axis_index("y"), lax.axis_index("c")
        me = 4 * x + 2 * y + c
        buf[me] = v_ref[...]
        copies = []
        for r in range(1, N_DEV):
            peer = (x ^ ((r >> 2) & 1), y ^ ((r >> 1) & 1), c ^ (r & 1))
            cp = pltpu.make_async_remote_copy(
                src_ref=v_ref, dst_ref=buf.at[me], send_sem=send.at[r - 1], recv_sem=recv.at[r - 1],
                device_id=peer, device_id_type=MESH)
            cp.start()
            copies.append(cp)
        for cp in copies:
            cp.wait()
        total = buf[0]
        for d in range(1, N_DEV):
            total = total + buf[d]
        o_ref[...] = total
        o_ref[loss_row:loss_row + 1, :] = jnp.broadcast_to(
            jnp.sum(total[loss_row:loss_row + 1, :], axis=-1, keepdims=True), (1, HEAD_DIM))

    return pl.pallas_call(
        body, out_shape=jax.ShapeDtypeStruct(vec.shape, F32),
        in_specs=[pl.BlockSpec(memory_space=pltpu.VMEM)], out_specs=pl.BlockSpec(memory_space=pltpu.VMEM),
        scratch_shapes=[pltpu.VMEM((N_DEV, rows, HEAD_DIM), F32),
                        pltpu.SemaphoreType.DMA((N_DEV - 1,)), pltpu.SemaphoreType.DMA((N_DEV - 1,))],
        name="allreduce_gains",
    )(vec)


def _pair_sum(grads, got, *, name):
    slots, rows, width = got.shape
    blk = _pick(rows, (256, 128, 64, 32, 16))
    nbh = rows // blk

    def body(lo_ref, hi_ref, got_ref, o_ref):
        mine = jnp.where(lax.axis_index("c") == 0, lo_ref[...], hi_ref[...])
        o_ref[...] = (mine.astype(F32) + got_ref[...].astype(F32)).astype(BF16)

    spec = pl.BlockSpec((None, blk, width), lambda s, i: (s, i, 0))
    return pl.pallas_call(
        body, out_shape=jax.ShapeDtypeStruct(got.shape, BF16), grid=(slots, nbh),
        in_specs=[spec, pl.BlockSpec((None, blk, width), lambda s, i: (s, nbh + i, 0)), spec],
        out_specs=spec, name=name, compiler_params=_params("parallel", "parallel"),
    )(grads, grads, got)


def _sum_slots(recv, chip_sum, *, name):
    _, rows, width = recv.shape
    blk = _pick(rows, (256, 128, 64, 32, 16, 8))

    def body(r_ref, p_ref, o_ref):
        me = 2 * lax.axis_index("x") + lax.axis_index("y")
        own = jnp.where(me < 2, jnp.where(me == 0, p_ref[0], p_ref[1]), jnp.where(me == 2, p_ref[2], p_ref[3]))
        o_ref[...] = ((own.astype(F32) + r_ref[0].astype(F32)) + r_ref[1].astype(F32)) + r_ref[2].astype(F32)

    return pl.pallas_call(
        body, out_shape=jax.ShapeDtypeStruct((rows, width), F32), grid=(rows // blk,),
        in_specs=[pl.BlockSpec((3, blk, width), lambda i: (0, i, 0)),
                  pl.BlockSpec((N_CHIPS, blk, width), lambda i: (0, i, 0))],
        out_specs=pl.BlockSpec((blk, width), lambda i: (i, 0)),
        name=name, compiler_params=_params("parallel"),
    )(recv, chip_sum)


def _adamw(wgt, grad, m, v, *, name, halves=None):
    rows, width = wgt.shape
    blk = _pick(rows // 2 if halves else rows, (256, 128, 64, 32, 16, 8))
    nbh = rows // 2 // blk

    def body(*refs):
        if halves:
            w_ref, mine_ref, theirs_ref, m_ref, v_ref, g_out, d_out, m_out, v_out = refs
            grad_v = jnp.where(pl.program_id(0) // nbh == lax.axis_index("c"), mine_ref[...], theirs_ref[...])
        else:
            w_ref, g_ref, m_ref, v_ref, g_out, d_out, m_out, v_out = refs
            grad_v = g_ref[...]
        m_new = ADAM_B1 * m_ref[...] + (1.0 - ADAM_B1) * grad_v
        v_new = ADAM_B2 * v_ref[...] + (1.0 - ADAM_B2) * (grad_v * grad_v)
        m_hat = m_new / (1.0 - ADAM_B1 ** ADAM_STEP)
        v_hat = v_new / (1.0 - ADAM_B2 ** ADAM_STEP)
        g_out[...] = grad_v
        d_out[...] = -ADAM_LR * (m_hat / (jnp.sqrt(v_hat) + ADAM_EPS) + ADAM_WD * w_ref[...])
        m_out[...] = m_new
        v_out[...] = v_new

    spec = pl.BlockSpec((blk, width), lambda i: (i, 0))
    half_spec = pl.BlockSpec((blk, width), lambda i: (i % nbh, 0))
    g_specs, g_args = ([half_spec, half_spec], list(halves)) if halves else ([spec], [grad])
    out = jax.ShapeDtypeStruct((rows, width), F32)
    return pl.pallas_call(
        body, out_shape=[out] * 4, grid=(rows // blk,), in_specs=[spec] + g_specs + [spec, spec],
        out_specs=[spec] * 4, name=name, compiler_params=_params("parallel"),
    )(wgt, *g_args, m, v)


_SMALL = (("a_norm", 2048), ("kv_norm", 2048), ("g_ckv", 512), ("g_k_nope", 128), ("g_k_rope", 64),
          ("b_norm", 2048), ("b_g_q_lat", 512), ("b_g_q_nope", 128), ("b_g_q_rope", 64),
          ("mem_norm", 4096), ("g_mem_q", 256), ("g_mem_k", 256))


def _lanes(n):
    return -(-n // HEAD_DIM) * HEAD_DIM


def _pack_rows(pieces, pad_rows_to=8):
    flat = jnp.concatenate(pieces, axis=1)
    rows = flat.shape[1] // HEAD_DIM
    pad = (-rows) % pad_rows_to
    if pad:
        flat = jnp.concatenate([flat, jnp.zeros((1, pad * HEAD_DIM), F32)], axis=1)
    return flat.reshape(rows + pad, HEAD_DIM)


def _pad_lanes(a):
    a = a.reshape(1, -1)
    pad = _lanes(a.shape[1]) - a.shape[1]
    if pad:
        a = jnp.concatenate([a, jnp.zeros((1, pad), F32)], axis=1)
    return a


def kernel(x, mem, positions, a_norm, a_w_in, a_w_out, kv_norm, w_dkv, g_ckv, w_ukv, g_k_nope, g_k_rope, b_norm, b_w_in, b_g_q_lat, b_w_uq, b_g_q_nope, b_g_q_rope, b_w_out, mem_norm, w_mem_kv, g_mem_q, g_mem_k, loss_target, m_a_norm, m_a_w_in, m_a_w_out, m_kv_norm, m_w_dkv, m_g_ckv, m_w_ukv, m_g_k_nope, m_g_k_rope, m_b_norm, m_b_w_in, m_b_g_q_lat, m_b_w_uq, m_b_g_q_nope, m_b_g_q_rope, m_b_w_out, m_mem_norm, m_w_mem_kv, m_g_mem_q, m_g_mem_k, v_a_norm, v_a_w_in, v_a_w_out, v_kv_norm, v_w_dkv, v_g_ckv, v_w_ukv, v_g_k_nope, v_g_k_rope, v_b_norm, v_b_w_in, v_b_g_q_lat, v_b_w_uq, v_b_g_q_nope, v_b_g_q_rope, v_b_w_out, v_mem_norm, v_w_mem_kv, v_g_mem_q, v_g_mem_k):
    chip = 2 * lax.axis_index("x") + lax.axis_index("y")
    rows_dkv = D_MODEL // N_CHIPS
    heads_per_chip = N_MLA_HEADS // N_CHIPS
    qk_w = HEAD_DIM + ROPE_DIM

    big = {"a_in": a_w_in[0], "a_out": a_w_out[0], "dkv": w_dkv, "ukv": w_ukv, "b_in": b_w_in[0],
           "uq": b_w_uq[0], "b_out": b_w_out[0], "mem_kv": w_mem_kv.reshape(2 * rows_dkv, 2 * MEM_W)}
    big_m = {"a_in": m_a_w_in[0], "a_out": m_a_w_out[0], "dkv": m_w_dkv, "ukv": m_w_ukv, "b_in": m_b_w_in[0],
             "uq": m_b_w_uq[0], "b_out": m_b_w_out[0], "mem_kv": m_w_mem_kv.reshape(2 * rows_dkv, 2 * MEM_W)}
    big_v = {"a_in": v_a_w_in[0], "a_out": v_a_w_out[0], "dkv": v_w_dkv, "ukv": v_w_ukv, "b_in": v_b_w_in[0],
             "uq": v_b_w_uq[0], "b_out": v_b_w_out[0], "mem_kv": v_w_mem_kv.reshape(2 * rows_dkv, 2 * MEM_W)}
    names = list(big)
    own_shards = [big[n].astype(BF16) for n in names] + [a_norm]
    gathered = _allgather_chips(own_shards)
    gathered = [lax.dynamic_update_slice(g, s[None], (chip,) + (0,) * s.ndim)
                for g, s in zip(gathered, own_shards)]
    st = dict(zip(names, gathered[:-1]))
    a_in_full = st["a_in"].transpose(1, 0, 2).reshape(D_MODEL, QKV_W + GATE_W)
    uq = st["uq"].reshape(N_CHIPS, Q_LORA, heads_per_chip, qk_w)
    uq = jnp.pad(uq, ((0, 0), (0, 0), (0, 0), (0, CAT_W - qk_w)))
    w = {
        "a_in": a_in_full,
        "a_in_qkv": a_in_full[:, :QKV_W],
        "a_in_gate": a_in_full[:, QKV_W:],
        "a_out": st["a_out"].reshape(D_MODEL, D_MODEL),
        "dkv": jnp.pad(st["dkv"].reshape(D_MODEL, KV_LORA + ROPE_DIM), ((0, 0), (0, HEAD_DIM - ROPE_DIM))),
        "ukv": st["ukv"].transpose(1, 0, 2).reshape(KV_LORA, N_MLA_HEADS * CAT_W),
        "b_in": st["b_in"].transpose(1, 0, 2).reshape(D_MODEL, Q_LORA + GATE_W),
        "uq": uq.transpose(1, 0, 2, 3).reshape(Q_LORA, N_MLA_HEADS * CAT_W),
        "b_out": st["b_out"].reshape(D_MODEL, D_MODEL),
        "mem_kv": st["mem_kv"].reshape(N_CHIPS, 2, rows_dkv, 2 * MEM_W).transpose(1, 0, 2, 3).reshape(
            2, D_MODEL, 2 * MEM_W),
    }
    gains = {
        "a_norm": gathered[-1].reshape(1, D_MODEL), "kv_norm": kv_norm.reshape(1, -1),
        "g_ckv": g_ckv.reshape(1, -1), "g_k_nope": g_k_nope.reshape(1, -1), "g_k_rope": g_k_rope.reshape(1, -1),
        "b_norm": b_norm, "b_g_q_lat": b_g_q_lat, "b_g_q_nope": b_g_q_nope, "b_g_q_rope": b_g_q_rope,
        "mem_norm": mem_norm, "g_mem_q": g_mem_q, "g_mem_k": g_mem_k,
    }

    loss_part, grad_x, gw, gg = _local_step(x[0], mem[0], positions[0], loss_target[0], w, gains)

    stacked = {
        "a_in": gw["a_in"],
        "a_out": gw["a_out"].reshape(N_CHIPS, rows_dkv, D_MODEL),
        "dkv": gw["dkv"][:, :KV_LORA + ROPE_DIM].reshape(N_CHIPS, rows_dkv, KV_LORA + ROPE_DIM),
        "ukv": gw["ukv"],
        "b_in": gw["b_in"],
        "uq": gw["uq"].reshape(N_CHIPS, Q_LORA, heads_per_chip, CAT_W)[..., :qk_w].reshape(
            N_CHIPS, Q_LORA, heads_per_chip * qk_w),
        "b_out": gw["b_out"].reshape(N_CHIPS, rows_dkv, D_MODEL),
        "mem_kv": jnp.stack([gw["mem_kv"][0].reshape(N_CHIPS, rows_dkv, 2 * MEM_W),
                             gw["mem_kv"][1].reshape(N_CHIPS, rows_dkv, 2 * MEM_W)], axis=1).reshape(
            N_CHIPS, 2 * rows_dkv, 2 * MEM_W),
    }
    got = _halve_with_sibling([stacked[n] for n in names])
    chip_sum = [_pair_sum(stacked[n], g, name=f"pair_sum_{n}") for n, g in zip(names, got)]
    received = _scatter_to_chips(chip_sum)
    half_total = [_sum_slots(r, p, name=f"sum_slots_{n}") for n, r, p in zip(names, received, chip_sum)]
    sibling_half = _swap_with_sibling(half_total)
    big_out = {}
    for n, mine, theirs in zip(names, half_total, sibling_half):
        big_out[n] = _adamw(big[n], None, big_m[n], big_v[n], halves=(mine, theirs), name=f"adamw_{n}")

    pieces = [_pad_lanes(gg[n]) if n not in ("g_k_rope", "b_g_q_rope") else gg[n] for n, _ in _SMALL]
    pieces.append(loss_part)
    loss_row = sum(_lanes(size) for _, size in _SMALL) // HEAD_DIM
    summed = _allreduce_small(_pack_rows(pieces), loss_row)
    flat = summed.reshape(1, -1)
    small_g, off = {}, 0
    for n, size in _SMALL:
        small_g[n] = flat[:, off:off + size]
        off += _lanes(size)
    loss = flat[0, off]
    small_g["a_norm"] = lax.dynamic_slice(small_g["a_norm"], (0, chip * rows_dkv), (1, rows_dkv))

    small_w = {"a_norm": a_norm, "kv_norm": kv_norm, "g_ckv": g_ckv, "g_k_nope": g_k_nope, "g_k_rope": g_k_rope,
               "b_norm": b_norm, "b_g_q_lat": b_g_q_lat, "b_g_q_nope": b_g_q_nope, "b_g_q_rope": b_g_q_rope,
               "mem_norm": mem_norm, "g_mem_q": g_mem_q, "g_mem_k": g_mem_k}
    small_m = {"a_norm": m_a_norm, "kv_norm": m_kv_norm, "g_ckv": m_g_ckv, "g_k_nope": m_g_k_nope,
               "g_k_rope": m_g_k_rope, "b_norm": m_b_norm, "b_g_q_lat": m_b_g_q_lat, "b_g_q_nope": m_b_g_q_nope,
               "b_g_q_rope": m_b_g_q_rope, "mem_norm": m_mem_norm, "g_mem_q": m_g_mem_q, "g_mem_k": m_g_mem_k}
    small_v = {"a_norm": v_a_norm, "kv_norm": v_kv_norm, "g_ckv": v_g_ckv, "g_k_nope": v_g_k_nope,
               "g_k_rope": v_g_k_rope, "b_norm": v_b_norm, "b_g_q_lat": v_b_g_q_lat, "b_g_q_nope": v_b_g_q_nope,
               "b_g_q_rope": v_b_g_q_rope, "mem_norm": v_mem_norm, "g_mem_q": v_g_mem_q, "g_mem_k": v_g_mem_k}
    snames = [n for n, _ in _SMALL]
    packs = [_pack_rows([_pad_lanes(src[n]) for n in snames])
             for src in (small_w, small_g, small_m, small_v)]
    small_res = _adamw(packs[0], packs[1], packs[2], packs[3], name="adamw_gains")
    small_out = {n: [] for n in snames}
    for res in small_res:
        flat_r = res.reshape(1, -1)
        off = 0
        for n in snames:
            size = small_w[n].size
            small_out[n].append(flat_r[:, off:off + size].reshape(small_w[n].shape))
            off += _lanes(size)

    big_names = {"a_w_in": ("a_in", a_w_in), "a_w_out": ("a_out", a_w_out), "w_dkv": ("dkv", w_dkv),
                 "w_ukv": ("ukv", w_ukv), "b_w_in": ("b_in", b_w_in), "b_w_uq": ("uq", b_w_uq),
                 "b_w_out": ("b_out", b_w_out), "w_mem_kv": ("mem_kv", w_mem_kv)}
    order = ["a_norm", "a_w_in", "a_w_out", "kv_norm", "w_dkv", "g_ckv", "w_ukv", "g_k_nope", "g_k_rope",
             "b_norm", "b_w_in", "b_g_q_lat", "b_w_uq", "b_g_q_nope", "b_g_q_rope", "b_w_out", "mem_norm",
             "w_mem_kv", "g_mem_q", "g_mem_k"]
    groups = [[], [], [], []]
    for n in order:
        if n in big_names:
            key, ref_arr = big_names[n]
            for t in range(4):
                groups[t].append(big_out[key][t].reshape(ref_arr.shape))
        else:
            for t in range(4):
                groups[t].append(small_out[n][t])
    return (loss, grad_x[None], *groups[0], *groups[1], *groups[2], *groups[3])
```

```python
import jax
import jax.numpy as jnp
from jax import lax
from jax.experimental import pallas as pl
from jax.experimental.pallas import tpu as pltpu

F32 = jnp.float32
BF16 = jnp.bfloat16
MESH = pl.DeviceIdType.MESH

D_MODEL = 2048
HEAD_DIM = 128
N_SB_HEADS = 12
N_MEM_HEADS = 4
N_MLA_HEADS = 12
MEM_LEN = 256
Q_LORA = 512
KV_LORA = 512
ROPE_DIM = 64
SB_W = N_SB_HEADS * HEAD_DIM
MEM_W = N_MEM_HEADS * HEAD_DIM
MLA_W = N_MLA_HEADS * HEAD_DIM
QKV_W = 3 * SB_W
GATE_W = SB_W + 2 * MEM_W
CAT_W = 2 * HEAD_DIM
ROPE_THETA = 10000.0
EPS = 1e-6
N_CHIPS = 4
N_DEV = 8

ADAM_LR = 0.001
ADAM_B1 = 0.9
ADAM_B2 = 0.999
ADAM_EPS = 1e-08
ADAM_WD = 0.01
ADAM_STEP = 10

VMEM_LIMIT_BYTES = 56 * 1024 * 1024
MM_OPERAND_VMEM_BYTES = 24 * 1024 * 1024
ROW_BLOCK = 256
ATT_BLOCK = 256


def _params(*sem):
    return pltpu.CompilerParams(dimension_semantics=sem, vmem_limit_bytes=VMEM_LIMIT_BYTES)


def _pick(n, cands):
    for c in cands:
        if n % c == 0:
            return c
    return n


def _mm(a, b, *, name, ta=False, tb=False, out_dtype=F32, res=None, n_split=1, scale_cols=None):
    if ta:
        k_dim, m_dim = a.shape
    else:
        m_dim, k_dim = a.shape
    if tb:
        n_dim, kb = b.shape
    else:
        kb, n_dim = b.shape
    assert kb == k_dim, (a.shape, b.shape)
    n_per = n_dim // n_split
    bm = m_dim if m_dim <= 1024 else _pick(m_dim, (1024, 512, 256))
    bn = n_per if n_per <= 1024 else _pick(n_per, (1024, 896, 768, 640, 512, 256, 128))
    per_k = (bm * a.dtype.itemsize + bn * b.dtype.itemsize) * 2
    bk = next((k_dim // d for d in range(1, k_dim // 128 + 1)
               if k_dim % d == 0 and (k_dim // d) % 128 == 0 and (k_dim // d) * per_k <= MM_OPERAND_VMEM_BYTES),
              k_dim)
    nk = k_dim // bk
    nb_per = n_per // bn
    grid = (m_dim // bm, n_dim // bn, nk)
    a_spec = (pl.BlockSpec((bk, bm), lambda i, j, k: (k, i)) if ta
              else pl.BlockSpec((bm, bk), lambda i, j, k: (i, k)))
    b_spec = (pl.BlockSpec((bn, bk), lambda i, j, k: (j, k)) if tb
              else pl.BlockSpec((bk, bn), lambda i, j, k: (k, j)))
    dims = (((0 if ta else 1,), (1 if tb else 0,)), ((), ()))
    in_specs = [a_spec, b_spec]
    args = [a, b]
    if res is not None:
        in_specs.append(pl.BlockSpec((bm, bn), lambda i, j, k: (i, j)))
        args.append(res)
    if n_split == 1:
        out_shape = jax.ShapeDtypeStruct((m_dim, n_dim), out_dtype)
        out_spec = pl.BlockSpec((bm, bn), lambda i, j, k: (i, j))
    else:
        out_shape = jax.ShapeDtypeStruct((n_split, m_dim, n_per), out_dtype)
        out_spec = pl.BlockSpec((None, bm, bn), lambda i, j, k: (j // nb_per, i, j % nb_per))

    def body(*refs):
        if res is None:
            a_ref, b_ref, o_ref, acc = refs
            r_ref = None
        else:
            a_ref, b_ref, r_ref, o_ref, acc = refs
        k = pl.program_id(2)
        col_block = pl.program_id(1)

        @pl.when(k == 0)
        def _():
            acc[...] = jnp.zeros_like(acc)

        acc[...] += lax.dot_general(a_ref[...].astype(BF16), b_ref[...].astype(BF16), dims,
                                    preferred_element_type=F32)

        @pl.when(k == nk - 1)
        def _():
            r = acc[...]
            if r_ref is not None:
                r = r + r_ref[...]
            if scale_cols is not None:
                assert scale_cols[0] % bn == 0
                r = r * jnp.where(col_block < scale_cols[0] // bn, scale_cols[1], 1.0)
            o_ref[...] = r.astype(out_dtype)

    return pl.pallas_call(
        body, out_shape=out_shape, grid=grid, in_specs=in_specs, out_specs=out_spec,
        scratch_shapes=[pltpu.VMEM((bm, bn), F32)], name=name,
        compiler_params=_params("parallel", "parallel", "arbitrary"),
    )(*args)


def _out_proj_loss(mixed, w_out, resid, target, *, name):
    seq, k_dim = mixed.shape
    bm = min(1024, seq)
    bn = 1024

    def body(a_ref, b_ref, r_ref, t_ref, dy_ref, loss_ref):
        @pl.when((pl.program_id(0) == 0) & (pl.program_id(1) == 0))
        def _():
            loss_ref[...] = jnp.zeros_like(loss_ref)

        diff = _dot(a_ref[...], b_ref[...]) + r_ref[...] - t_ref[...]
        dy_ref[...] = diff / D_MODEL
        col = jnp.sum(diff * diff, axis=0, keepdims=True)
        part = col[:, :HEAD_DIM]
        for c in range(1, bn // HEAD_DIM):
            part = part + col[:, _hs(c)]
        loss_ref[...] += part * (0.5 / D_MODEL)

    tile = pl.BlockSpec((bm, bn), lambda i, j: (i, j))
    return pl.pallas_call(
        body,
        out_shape=[jax.ShapeDtypeStruct((seq, D_MODEL), F32), jax.ShapeDtypeStruct((1, HEAD_DIM), F32)],
        grid=(seq // bm, D_MODEL // bn),
        in_specs=[pl.BlockSpec((bm, k_dim), lambda i, j: (i, 0)), pl.BlockSpec((k_dim, bn), lambda i, j: (0, j)),
                  tile, tile],
        out_specs=[tile, pl.BlockSpec((1, HEAD_DIM), lambda i, j: (0, 0))],
        name=name, compiler_params=_params("arbitrary", "arbitrary"),
    )(mixed, w_out, resid, target)


def _rowwise(body, n_rows, ins, outs, accs=(), *, name, block=ROW_BLOCK):
    blk = min(block, n_rows)
    assert n_rows % blk == 0
    in_specs = []
    for arr, is_row in ins:
        if is_row:
            assert arr.shape[0] == n_rows, (name, arr.shape, n_rows)
            width = arr.shape[1] if is_row is True else is_row
            in_specs.append(pl.BlockSpec((blk, width), lambda i: (i, 0)))
        else:
            in_specs.append(pl.BlockSpec(arr.shape, lambda i, nd=arr.ndim: (0,) * nd))
    out_shape = [jax.ShapeDtypeStruct((n_rows, w), dt) for w, dt in outs]
    out_specs = [pl.BlockSpec((blk, w), lambda i: (i, 0)) for w, _ in outs]
    out_shape += [jax.ShapeDtypeStruct(s, dt) for s, dt in accs]
    out_specs += [pl.BlockSpec(s, lambda i, nd=len(s): (0,) * nd) for s, _ in accs]
    n_in, n_out, n_acc = len(ins), len(outs), len(accs)

    def kern(*refs):
        in_refs = refs[:n_in]
        out_refs = refs[n_in:n_in + n_out]
        acc_refs = refs[n_in + n_out:]
        if n_acc:
            @pl.when(pl.program_id(0) == 0)
            def _():
                for r in acc_refs:
                    r[...] = jnp.zeros_like(r)
        body(in_refs, out_refs, acc_refs)

    return pl.pallas_call(
        kern, out_shape=out_shape, grid=(n_rows // blk,), in_specs=in_specs, out_specs=out_specs,
        name=name, compiler_params=_params("arbitrary"),
    )(*[arr for arr, _ in ins])


def _rms(x, g, n=None):
    n = x.shape[-1] if n is None else n
    r = lax.rsqrt(jnp.sum(x * x, axis=-1, keepdims=True) / n + EPS)
    return x * r * g


def _rms_bwd(x, g, dy, n=None):
    n = x.shape[-1] if n is None else n
    r = lax.rsqrt(jnp.sum(x * x, axis=-1, keepdims=True) / n + EPS)
    gdy = dy * g
    dx = r * (gdy - x * ((r * r) * (jnp.sum(gdy * x, axis=-1, keepdims=True) / n)))
    dg = jnp.sum(dy * x * r, axis=0, keepdims=True)
    return dx, dg


def _swap_halves(x):
    lane = lax.broadcasted_iota(jnp.int32, x.shape, 1)
    return jnp.where(lane < ROPE_DIM // 2, pltpu.roll(x, 128 - ROPE_DIM // 2, 1),
                     pltpu.roll(x, ROPE_DIM // 2, 1))


def _rope(n, cos_t, sin_t):
    return n * cos_t + _swap_halves(n) * sin_t


def _rope_bwd(dy, cos_t, sin_t):
    return dy * cos_t - _swap_halves(dy) * sin_t


def _sigmoid(g):
    return 1.0 / (1.0 + jnp.exp(-g))


def _dot_t(a, b):
    return lax.dot_general(a, b, (((1,), (1,)), ((), ())), preferred_element_type=F32)


def _tdot(a, b):
    return lax.dot_general(a, b, (((0,), (0,)), ((), ())), preferred_element_type=F32)


def _dot(a, b):
    return jnp.dot(a, b, preferred_element_type=F32)


def _hs(h, w=HEAD_DIM, base=0):
    return slice(base + h * w, base + (h + 1) * w)


def _mem_head(qm, gq, mk_h, mv_h):
    qb = _rms(qm, gq).astype(BF16)
    s = _dot_t(qb, mk_h) * (HEAD_DIM ** -0.5)
    e = jnp.exp(s - jnp.max(s, axis=-1, keepdims=True))
    p = e / jnp.sum(e, axis=-1, keepdims=True)
    mo = _dot(p.astype(BF16), mv_h)
    return qb, p, mo


def _mix_fwd(att, gates, c0, mk, mv, gq, *, name):
    n_rows = att.shape[0]

    def body(ins, outs, _):
        att_ref, g_ref, mk_ref, mv_ref, gq_ref = ins
        (o_ref,) = outs
        g = g_ref[:, c0:c0 + SB_W]
        o_ref[:, :SB_W] = (att_ref[...] * (g * _sigmoid(g))).astype(BF16)
        for h in range(N_MEM_HEADS):
            qm = g_ref[:, _hs(h, base=c0 + SB_W)]
            gm = g_ref[:, _hs(h, base=c0 + SB_W + MEM_W)]
            _, _, mo = _mem_head(qm, gq_ref[...], mk_ref[:, _hs(h)], mv_ref[:, _hs(h)])
            o_ref[:, _hs(h, base=SB_W)] = (mo * (gm * _sigmoid(gm))).astype(BF16)

    (mixed,) = _rowwise(body, n_rows,
                        [(att, True), (gates, True), (mk, False), (mv, False), (gq, False)],
                        [(D_MODEL, BF16)], name=name)
    return mixed


def _mix_bwd(dmixed, att, gates, c0, mk, mv, gq, *, name):
    n_rows = att.shape[0]
    scale = HEAD_DIM ** -0.5

    def body(ins, outs, accs):
        dm_ref, att_ref, g_ref, mk_ref, mv_ref, gq_ref = ins
        datt_ref, dg_ref = outs
        dmk_ref, dmv_ref, dgq_ref = accs
        g = g_ref[:, c0:c0 + SB_W]
        sg = _sigmoid(g)
        dm = dm_ref[:, :SB_W]
        datt_ref[...] = dm * (g * sg)
        dg_ref[:, :SB_W] = (dm * att_ref[...] * (sg * (1.0 + g * (1.0 - sg)))).astype(BF16)
        for h in range(N_MEM_HEADS):
            qm = g_ref[:, _hs(h, base=c0 + SB_W)]
            gm = g_ref[:, _hs(h, base=c0 + SB_W + MEM_W)]
            mk_h = mk_ref[:, _hs(h)]
            mv_h = mv_ref[:, _hs(h)]
            qb, p, mo = _mem_head(qm, gq_ref[...], mk_h, mv_h)
            sgm = _sigmoid(gm)
            dmh = dm_ref[:, _hs(h, base=SB_W)]
            dmo = dmh * (gm * sgm)
            dg_ref[:, _hs(h, base=SB_W + MEM_W)] = (
                dmh * mo * (sgm * (1.0 + gm * (1.0 - sgm)))).astype(BF16)
            dmo_b = dmo.astype(BF16)
            pb = p.astype(BF16)
            dp = _dot_t(dmo_b, mv_h)
            dmv_ref[:, _hs(h)] += _tdot(pb, dmo_b)
            ds = (p * (dp - jnp.sum(dp * p, axis=-1, keepdims=True)) * scale).astype(BF16)
            dqn = _dot(ds, mk_h)
            dmk_ref[:, _hs(h)] += _tdot(ds, qb)
            dqm, dgq = _rms_bwd(qm, gq_ref[...], dqn)
            dg_ref[:, _hs(h, base=SB_W)] = dqm.astype(BF16)
            dgq_ref[...] += dgq

    return _rowwise(body, n_rows,
                    [(dmixed, True), (att, True), (gates, True), (mk, False), (mv, False), (gq, False)],
                    [(SB_W, F32), (GATE_W, BF16)],
                    [((MEM_LEN, MEM_W), F32), ((MEM_LEN, MEM_W), F32), ((1, HEAD_DIM), F32)],
                    name=name)


def _mem_side_fwd(mem, g_norm, w_kv, g_k, *, tag):
    def norm_body(ins, outs, _):
        outs[0][...] = _rms(ins[0][...], ins[1][...]).astype(BF16)

    (mn,) = _rowwise(norm_body, MEM_LEN, [(mem, True), (g_norm, False)], [(D_MODEL, BF16)],
                     name=f"mem_norm_{tag}")
    mkv = _mm(mn, w_kv, name=f"mem_kv_{tag}")

    def kv_body(ins, outs, _):
        mkv_ref, gk_ref = ins
        mk_ref, mv_ref = outs
        for h in range(N_MEM_HEADS):
            mk_ref[:, _hs(h)] = _rms(mkv_ref[:, _hs(h)], gk_ref[...]).astype(BF16)
        mv_ref[...] = mkv_ref[:, MEM_W:].astype(BF16)

    mk, mv = _rowwise(kv_body, MEM_LEN, [(mkv, True), (g_k, False)], [(MEM_W, BF16), (MEM_W, BF16)],
                      name=f"mem_kv_prep_{tag}")
    return mn, mkv, mk, mv


def _mem_side_bwd(mem, g_norm, w_kv, g_k, mn, mkv, dmk, dmv, *, tag):
    def kv_body(ins, outs, accs):
        mkv_ref, gk_ref, dmk_ref, dmv_ref = ins
        (d_ref,) = outs
        (dgk_ref,) = accs
        for h in range(N_MEM_HEADS):
            dx, dg = _rms_bwd(mkv_ref[:, _hs(h)], gk_ref[...], dmk_ref[:, _hs(h)])
            d_ref[:, _hs(h)] = dx.astype(BF16)
            dgk_ref[...] += dg
        d_ref[:, MEM_W:] = dmv_ref[...].astype(BF16)

    dmkv, dgk = _rowwise(kv_body, MEM_LEN, [(mkv, True), (g_k, False), (dmk, True), (dmv, True)],
                         [(2 * MEM_W, BF16)], [((1, HEAD_DIM), F32)], name=f"mem_kv_prep_bwd_{tag}")
    dmn = _mm(dmkv, w_kv, tb=True, name=f"mem_kv_dx_{tag}")
    dw = _mm(mn, dmkv, ta=True, out_dtype=BF16, name=f"mem_kv_dw_{tag}")

    def norm_body(ins, outs, accs):
        _, dg = _rms_bwd(ins[0][...], ins[1][...], ins[2][...])
        accs[0][...] += dg

    (dgn,) = _rowwise(norm_body, MEM_LEN, [(mem, True), (g_norm, False), (dmn, True)], [],
                      [((1, D_MODEL), F32)], name=f"mem_norm_bwd_{tag}")
    return dw, dgn, dgk


LOG2_E = 1.4426950408889634
SB_Q_SCALE = HEAD_DIM ** -0.5 * LOG2_E


Z2_CAP = 126.0


def _sb_terms(z2):
    zc = jnp.minimum(z2, Z2_CAP)
    w = 1.0 + jnp.exp2(zc)
    return zc, w, jnp.log2(w)


LOOP_UNROLL = 8


def _loop_blocks(base, qb, body, carry, *, reverse):
    def run(start, trips, unroll, c0):
        def trip(t, c):
            for u in range(unroll):
                p = start + t * unroll + u
                c = body(base - 1 - p if reverse else p, c)
            return c
        return lax.fori_loop(0, trips, trip, c0)

    if qb % LOOP_UNROLL == 0:
        return run(0, base // LOOP_UNROLL, LOOP_UNROLL, carry)
    small = qb
    n_big = base // LOOP_UNROLL
    carry = run(0, n_big, LOOP_UNROLL, carry)
    return run(n_big * LOOP_UNROLL, (base - n_big * LOOP_UNROLL) // small, small, carry)


def _chain_modes(s, qb):
    return tuple(None if t < s else ("m" if t == s else "f") for t in range(qb))


def _split_dot(x, tri2):
    hi = x.astype(BF16)
    lo = (x - hi.astype(F32)).astype(BF16)
    return _dot(jnp.concatenate([hi, lo], axis=1), tri2)


def _sb_fwd(qkv, *, name, hp=1):
    seq = qkv.shape[0]
    blk = min(ATT_BLOCK, seq)
    nkb = seq // blk
    qb = _pick(nkb, (4, 2, 1))
    rows = qb * blk
    chains = [(t, s) for t in range(hp) for s in range(qb)]

    def body(q_ref, k_ref, v_ref, o_ref):
        base = pl.program_id(1) * qb
        qs = {(t, s): q_ref[s * blk:(s + 1) * blk, _hs(t)] for t, s in chains}
        row = lax.broadcasted_iota(jnp.int32, (blk, blk), 0)
        col = lax.broadcasted_iota(jnp.int32, (blk, blk), 1)
        after = (row > col).astype(BF16)
        after2 = jnp.concatenate([after, after], axis=0)
        causal = col < row

        def step(j, carry, modes):
            off = pl.multiple_of(j * blk, blk)
            act = [c for c in chains if modes[c[1]]]
            zs, ls = {}, {}
            for c in act:
                zs[c], _, l = _sb_terms(_dot_t(qs[c], k_ref[pl.ds(off, blk), _hs(c[0])]))
                ls[c] = jnp.where(causal, l, 0.0) if modes[c[1]] == "m" else l
            cs = {c: _split_dot(ls[c], after2) for c in act}
            carry = dict(carry)
            for c in act:
                run, acc = carry[c]
                a = jnp.exp2(zs[c] - ls[c] - cs[c] - run)
                if modes[c[1]] == "m":
                    a = jnp.where(causal, a, 0.0)
                acc = acc + _dot(a.astype(BF16), v_ref[pl.ds(off, blk), _hs(c[0])])
                carry[c] = (run + (cs[c][:, :1] + ls[c][:, :1]), acc)
            return carry

        init = (jnp.zeros((blk, 1), F32), jnp.zeros((blk, HEAD_DIM), F32))
        carry = {c: init for c in chains}
        for s in reversed(range(qb)):
            carry = step(base + s, carry, _chain_modes(s, qb))
        carry = _loop_blocks(base, qb, lambda j, c: step(j, c, ("f",) * qb), carry, reverse=True)
        for t, s in chains:
            o_ref[s * blk:(s + 1) * blk, _hs(t)] = carry[(t, s)][1]

    nh = N_SB_HEADS // hp
    return pl.pallas_call(
        body, out_shape=jax.ShapeDtypeStruct((seq, SB_W), F32), grid=(nh, nkb // qb),
        in_specs=[pl.BlockSpec((rows, hp * HEAD_DIM), lambda h, i: (i, h)),
                  pl.BlockSpec((seq, hp * HEAD_DIM), lambda h, i: (0, nh + h)),
                  pl.BlockSpec((seq, hp * HEAD_DIM), lambda h, i: (0, 2 * nh + h))],
        out_specs=pl.BlockSpec((rows, hp * HEAD_DIM), lambda h, i: (i, h)),
        name=name, compiler_params=_params("parallel", "arbitrary"),
    )(qkv, qkv, qkv)


SB_BWD_GROUP = 4


def _sb_bwd(qkv, out, dout, *, name):
    seq = qkv.shape[0]
    blk = min(ATT_BLOCK, seq)
    nkb = seq // blk
    qb = _pick(nkb, (4, 2, 1))
    rows = qb * blk
    scale = HEAD_DIM ** -0.5

    def body(q_ref, k_ref, v_ref, do_ref, o_ref, dq_ref, dk_out, dv_out, dk_ref, dv_ref):
        g = pl.program_id(1)
        base = g * qb

        @pl.when(g == 0)
        def _():
            dk_ref[...] = jnp.zeros_like(dk_ref)
            dv_ref[...] = jnp.zeros_like(dv_ref)

        qs = [q_ref[t * blk:(t + 1) * blk, :] for t in range(qb)]
        dos = [do_ref[t * blk:(t + 1) * blk, :].astype(BF16) for t in range(qb)]
        totals = [jnp.sum(dos[t].astype(F32) * o_ref[t * blk:(t + 1) * blk, :], axis=-1, keepdims=True)
                  for t in range(qb)]
        row = lax.broadcasted_iota(jnp.int32, (blk, blk), 0)
        col = lax.broadcasted_iota(jnp.int32, (blk, blk), 1)
        after = (row > col).astype(BF16)
        after2 = jnp.concatenate([after, after], axis=0)
        from_s = (row >= col).astype(BF16)
        from_s2 = jnp.concatenate([from_s, from_s], axis=0)
        causal = col < row

        def step(j, carry, modes):
            runs, rights, dqs = list(carry[0]), list(carry[1]), list(carry[2])
            off = pl.multiple_of(j * blk, blk)
            kb = k_ref[pl.ds(off, blk), :]
            vb = v_ref[pl.ds(off, blk), :]
            dv_inc = dk_inc = None
            for first in range(0, qb, SB_BWD_GROUP):
                act = [t for t in range(first, min(first + SB_BWD_GROUP, qb)) if modes[t]]
                zs, ls, sns = {}, {}, {}
                for t in act:
                    zs[t], w, l = _sb_terms(_dot_t(qs[t], kb))
                    sns[t] = pl.reciprocal(w, approx=True)
                    ls[t] = jnp.where(causal, l, 0.0) if modes[t] == "m" else l
                cs = {t: _split_dot(ls[t], after2) for t in act}
                das = {t: _dot_t(dos[t], vb) for t in act}
                abs_, des = {}, {}
                for t in act:
                    a = jnp.exp2(zs[t] - ls[t] - cs[t] - runs[t])
                    if modes[t] == "m":
                        a = jnp.where(causal, a, 0.0)
                    abs_[t] = a.astype(BF16)
                    des[t] = abs_[t].astype(F32) * das[t]
                sufs = {t: _split_dot(des[t], from_s2) for t in act}
                for t in act:
                    left = totals[t] - (sufs[t] + rights[t])
                    dz = (des[t] + left) * sns[t] - left
                    if modes[t] == "m":
                        dz = jnp.where(causal, dz, 0.0)
                    dzb = dz.astype(BF16)
                    dqs[t] = dqs[t] + _dot(dzb, kb)
                    inc_v = _tdot(abs_[t], dos[t])
                    inc_k = _tdot(dzb, qs[t])
                    dv_inc = inc_v if dv_inc is None else dv_inc + inc_v
                    dk_inc = inc_k if dk_inc is None else dk_inc + inc_k
                    runs[t] = runs[t] + (cs[t][:, :1] + ls[t][:, :1])
                    rights[t] = rights[t] + sufs[t][:, :1]
            dv_ref[pl.ds(off, blk), :] += dv_inc
            dk_ref[pl.ds(off, blk), :] += dk_inc
            return tuple(runs), tuple(rights), tuple(dqs)

        zero = (jnp.zeros((blk, 1), F32),) * qb
        carry = (zero, zero, (jnp.zeros((blk, HEAD_DIM), F32),) * qb)
        for s in reversed(range(qb)):
            carry = step(base + s, carry, _chain_modes(s, qb))
        carry = _loop_blocks(base, qb, lambda j, c: step(j, c, ("f",) * qb), carry, reverse=True)
        for t in range(qb):
            dq_ref[t * blk:(t + 1) * blk, :] = (carry[2][t] * scale).astype(BF16)

        @pl.when(g == pl.num_programs(1) - 1)
        def _():
            dk_out[...] = (dk_ref[...] * (1.0 / LOG2_E)).astype(BF16)
            dv_out[...] = dv_ref[...].astype(BF16)

    out_sd = jax.ShapeDtypeStruct((seq, SB_W), BF16)
    return pl.pallas_call(
        body, out_shape=[out_sd, out_sd, out_sd], grid=(N_SB_HEADS, nkb // qb),
        in_specs=[pl.BlockSpec((rows, HEAD_DIM), lambda h, i: (i, h)),
                  pl.BlockSpec((seq, HEAD_DIM), lambda h, i: (0, N_SB_HEADS + h)),
                  pl.BlockSpec((seq, HEAD_DIM), lambda h, i: (0, 2 * N_SB_HEADS + h)),
                  pl.BlockSpec((rows, HEAD_DIM), lambda h, i: (i, h)),
                  pl.BlockSpec((rows, HEAD_DIM), lambda h, i: (i, h))],
        out_specs=[pl.BlockSpec((rows, HEAD_DIM), lambda h, i: (i, h)),
                   pl.BlockSpec((seq, HEAD_DIM), lambda h, i: (0, h)),
                   pl.BlockSpec((seq, HEAD_DIM), lambda h, i: (0, h))],
        scratch_shapes=[pltpu.VMEM((seq, HEAD_DIM), F32), pltpu.VMEM((seq, HEAD_DIM), F32)],
        name=name, compiler_params=_params("parallel", "arbitrary"),
    )(qkv, qkv, qkv, dout, out)


MLA_SCALE = (HEAD_DIM + ROPE_DIM) ** -0.5
MLA_Q_SCALE = MLA_SCALE * LOG2_E


def _mla_fwd(q_cat, k_cat, v, *, name, hp=1):
    seq = q_cat.shape[0]
    blk = min(ATT_BLOCK, seq)
    nkb = seq // blk
    qb = _pick(nkb, (4, 2, 1))
    rows = qb * blk
    chains = [(t, s) for t in range(hp) for s in range(qb)]

    def body(q_ref, k_ref, v_ref, o_ref, lse_ref):
        base = pl.program_id(1) * qb
        qs = {(t, s): q_ref[s * blk:(s + 1) * blk, t * CAT_W:(t + 1) * CAT_W] for t, s in chains}
        row = lax.broadcasted_iota(jnp.int32, (blk, blk), 0)
        col = lax.broadcasted_iota(jnp.int32, (blk, blk), 1)
        causal = col <= row
        ones = jnp.ones((blk, HEAD_DIM), BF16)

        def step(j, carry, modes):
            off = pl.multiple_of(j * blk, blk)
            act = [c for c in chains if modes[c[1]]]
            ss = {c: _dot_t(qs[c], k_ref[pl.ds(off, blk), c[0] * CAT_W:(c[0] + 1) * CAT_W]) for c in act}
            carry = dict(carry)
            for c in act:
                m, l, acc = carry[c]
                s = ss[c]
                if modes[c[1]] == "m":
                    s = jnp.where(causal, s, -jnp.inf)
                m_new = jnp.maximum(m, jnp.max(s, axis=-1, keepdims=True))
                pb = jnp.exp2(s - m_new).astype(BF16)
                alpha = jnp.exp2(m - m_new)
                both = _dot(pb, jnp.concatenate([v_ref[pl.ds(off, blk), _hs(c[0])], ones], axis=1))
                l = alpha * l + both[:, HEAD_DIM:HEAD_DIM + 1]
                acc = alpha * acc + both[:, :HEAD_DIM]
                carry[c] = (m_new, l, acc)
            return carry

        init = (jnp.full((blk, 1), -jnp.inf, F32), jnp.zeros((blk, 1), F32),
                jnp.zeros((blk, HEAD_DIM), F32))
        carry = {c: init for c in chains}
        carry = _loop_blocks(base, qb, lambda j, c: step(j, c, ("f",) * qb), carry, reverse=False)
        for s in range(qb):
            carry = step(base + s, carry, _chain_modes(s, qb))
        for t, s in chains:
            m, l, acc = carry[(t, s)]
            o_ref[s * blk:(s + 1) * blk, _hs(t)] = acc / l
            lse_ref[s * blk:(s + 1) * blk, _hs(t)] = jnp.broadcast_to(
                (m + jnp.log2(l)) * (1.0 / LOG2_E), (blk, HEAD_DIM))

    out = jax.ShapeDtypeStruct((seq, MLA_W), F32)
    return pl.pallas_call(
        body, out_shape=[out, out], grid=(N_MLA_HEADS // hp, nkb // qb),
        in_specs=[pl.BlockSpec((rows, hp * CAT_W), lambda h, i: (i, h)),
                  pl.BlockSpec((seq, hp * CAT_W), lambda h, i: (0, h)),
                  pl.BlockSpec((seq, hp * HEAD_DIM), lambda h, i: (0, h))],
        out_specs=[pl.BlockSpec((rows, hp * HEAD_DIM), lambda h, i: (i, h)),
                   pl.BlockSpec((rows, hp * HEAD_DIM), lambda h, i: (i, h))],
        name=name, compiler_params=_params("parallel", "arbitrary"),
    )(q_cat, k_cat, v)


def _mla_bwd(q_cat, k_cat, v, out, lse, dout, *, name):
    seq = q_cat.shape[0]
    blk = min(ATT_BLOCK, seq)
    nkb = seq // blk
    qb = _pick(nkb, (4, 2, 1))
    rows = qb * blk

    def body(q_ref, k_ref, v_ref, o_ref, lse_ref, do_ref, dq_ref, dk_ref, dv_ref):
        g = pl.program_id(1)
        base = g * qb

        @pl.when(g == 0)
        def _():
            dk_ref[...] = jnp.zeros_like(dk_ref)
            dv_ref[...] = jnp.zeros_like(dv_ref)

        qs, dobs, deltas, lses = [], [], [], []
        for t in range(qb):
            rs = slice(t * blk, (t + 1) * blk)
            do = do_ref[rs, :]
            qs.append(q_ref[rs, :])
            dobs.append(do.astype(BF16))
            deltas.append(jnp.sum(do * o_ref[rs, :], axis=-1, keepdims=True))
            lses.append(lse_ref[rs, :1] * LOG2_E)
        row = lax.broadcasted_iota(jnp.int32, (blk, blk), 0)
        col = lax.broadcasted_iota(jnp.int32, (blk, blk), 1)
        causal = col <= row

        def step(j, dqs, modes):
            off = pl.multiple_of(j * blk, blk)
            kb = k_ref[pl.ds(off, blk), :]
            vb = v_ref[pl.ds(off, blk), :]
            act = [t for t in range(qb) if modes[t]]
            ss = {t: _dot_t(qs[t], kb) for t in act}
            dps = {t: _dot_t(dobs[t], vb) for t in act}
            dqs = list(dqs)
            dv_inc = dk_inc = None
            for t in act:
                p = jnp.exp2(ss[t] - lses[t])
                if modes[t] == "m":
                    p = jnp.where(causal, p, 0.0)
                ds = (p * (dps[t] - deltas[t])).astype(BF16)
                inc_v = _tdot(p.astype(BF16), dobs[t])
                inc_k = _tdot(ds, qs[t])
                dv_inc = inc_v if dv_inc is None else dv_inc + inc_v
                dk_inc = inc_k if dk_inc is None else dk_inc + inc_k
                dqs[t] = dqs[t] + _dot(ds, kb)
            dv_ref[pl.ds(off, blk), :] += dv_inc
            dk_ref[pl.ds(off, blk), :] += dk_inc
            return tuple(dqs)

        dqs = (jnp.zeros((blk, CAT_W), F32),) * qb
        dqs = _loop_blocks(base, qb, lambda j, c: step(j, c, ("f",) * qb), dqs, reverse=False)
        for s in range(qb):
            modes = tuple(None if t < s else ("m" if t == s else "f") for t in range(qb))
            dqs = step(base + s, dqs, modes)
        for t in range(qb):
            dq_ref[t * blk:(t + 1) * blk, :] = dqs[t] * MLA_SCALE

        @pl.when(g == pl.num_programs(1) - 1)
        def _():
            dk_ref[...] = dk_ref[...] * (1.0 / LOG2_E)

    return pl.pallas_call(
        body,
        out_shape=[jax.ShapeDtypeStruct((seq, N_MLA_HEADS * CAT_W), F32),
                   jax.ShapeDtypeStruct((seq, N_MLA_HEADS * CAT_W), F32),
                   jax.ShapeDtypeStruct((seq, MLA_W), F32)],
        grid=(N_MLA_HEADS, nkb // qb),
        in_specs=[pl.BlockSpec((rows, CAT_W), lambda h, i: (i, h)),
                  pl.BlockSpec((seq, CAT_W), lambda h, i: (0, h)),
                  pl.BlockSpec((seq, HEAD_DIM), lambda h, i: (0, h)),
                  pl.BlockSpec((rows, HEAD_DIM), lambda h, i: (i, h)),
                  pl.BlockSpec((rows, HEAD_DIM), lambda h, i: (i, h)),
                  pl.BlockSpec((rows, HEAD_DIM), lambda h, i: (i, h))],
        out_specs=[pl.BlockSpec((rows, CAT_W), lambda h, i: (i, h)),
                   pl.BlockSpec((seq, CAT_W), lambda h, i: (0, h)),
                   pl.BlockSpec((seq, HEAD_DIM), lambda h, i: (0, h))],
        name=name, compiler_params=_params("parallel", "arbitrary"),
    )(q_cat, k_cat, v, out, lse, dout)


def _local_step(x, mem, positions, target, w, g):
    seq = x.shape[0]
    inv_freq = jnp.power(ROPE_THETA, -jnp.arange(0, ROPE_DIM, 2, dtype=F32) / ROPE_DIM)
    ang = positions.astype(F32)[:, None] * inv_freq
    cos, sin = jnp.cos(ang), jnp.sin(ang)
    lane_pad = jnp.zeros((seq, HEAD_DIM - ROPE_DIM), F32)
    cos_t = jnp.concatenate([cos, cos, lane_pad], axis=1)
    sin_t = jnp.concatenate([-sin, sin, lane_pad], axis=1)
    gain_pad = jnp.zeros((1, HEAD_DIM - ROPE_DIM), F32)
    g_k_rope = jnp.concatenate([g["g_k_rope"], gain_pad], axis=1)
    g_q_rope = jnp.concatenate([g["b_g_q_rope"], gain_pad], axis=1)

    def norm_to_bf16(src, gain, name):
        def body(ins, outs, _):
            outs[0][...] = _rms(ins[0][...], ins[1][...]).astype(BF16)
        return _rowwise(body, seq, [(src, True), (gain, False)], [(src.shape[1], BF16)], name=name)[0]

    h_a = norm_to_bf16(x, g["a_norm"], "a_norm_fwd")
    qkv = _mm(h_a, w["a_in_qkv"], out_dtype=BF16, scale_cols=(SB_W, SB_Q_SCALE), name="a_in_qkv")
    gr = _mm(h_a, w["a_in_gate"], name="a_in_gate")
    sb = _sb_fwd(qkv, name="sb_fwd")
    mem0 = _mem_side_fwd(mem, g["mem_norm"][0:1], w["mem_kv"][0], g["g_mem_k"][0:1], tag="a")
    mixed_a = _mix_fwd(sb, gr, 0, mem0[2], mem0[3], g["g_mem_q"][0:1], name="a_mix_fwd")
    x1 = _mm(mixed_a, w["a_out"], res=x, name="a_out")

    def norms2_body(ins, outs, _):
        xv = ins[0][...]
        outs[0][...] = _rms(xv, ins[1][...]).astype(BF16)
        outs[1][...] = _rms(xv, ins[2][...]).astype(BF16)

    h_kv, h_b = _rowwise(norms2_body, seq, [(x1, True), (g["kv_norm"], False), (g["b_norm"], False)],
                         [(D_MODEL, BF16), (D_MODEL, BF16)], name="kv_b_norm_fwd")
    ckr = _mm(h_kv, w["dkv"], name="dkv")

    def ckr_body(ins, outs, _):
        ckr_ref, gc_ref, gr_ref, c_ref, s_ref = ins
        outs[0][...] = _rms(ckr_ref[:, :KV_LORA], gc_ref[...]).astype(BF16)
        kr = _rms(ckr_ref[:, KV_LORA:], gr_ref[...], n=ROPE_DIM)
        outs[1][...] = _rope(kr, c_ref[...], s_ref[...]).astype(BF16)

    c_n, k_r = _rowwise(ckr_body, seq,
                        [(ckr, True), (g["g_ckv"], False), (g_k_rope, False), (cos_t, True), (sin_t, True)],
                        [(KV_LORA, BF16), (HEAD_DIM, BF16)], name="ckv_prep_fwd")
    kv = _mm(c_n, w["ukv"], name="ukv")

    def kcat_body(ins, outs, _):
        kv_ref, kr_ref, gk_ref = ins
        kc_ref, v_ref = outs
        for h in range(N_MLA_HEADS):
            kc_ref[:, h * CAT_W:h * CAT_W + HEAD_DIM] = _rms(
                kv_ref[:, h * CAT_W:h * CAT_W + HEAD_DIM], gk_ref[...]).astype(BF16)
            kc_ref[:, h * CAT_W + HEAD_DIM:(h + 1) * CAT_W] = kr_ref[...]
            v_ref[:, _hs(h)] = kv_ref[:, h * CAT_W + HEAD_DIM:(h + 1) * CAT_W].astype(BF16)

    k_cat, v_mla = _rowwise(kcat_body, seq, [(kv, True), (k_r, True), (g["g_k_nope"], False)],
                            [(N_MLA_HEADS * CAT_W, BF16), (MLA_W, BF16)], name="k_prep_fwd")

    p2 = _mm(h_b, w["b_in"], name="b_in")

    def qlat_body(ins, outs, _):
        outs[0][...] = _rms(ins[0][:, :Q_LORA], ins[1][...]).astype(BF16)

    (q_l,) = _rowwise(qlat_body, seq, [(p2, Q_LORA), (g["b_g_q_lat"], False)], [(Q_LORA, BF16)],
                      name="q_lat_norm_fwd")
    q_up = _mm(q_l, w["uq"], name="uq")

    def qcat_body(ins, outs, _):
        q_ref, gn_ref, gr_ref, c_ref, s_ref = ins
        (o_ref,) = outs
        for h in range(N_MLA_HEADS):
            o_ref[:, h * CAT_W:h * CAT_W + HEAD_DIM] = (MLA_Q_SCALE * _rms(
                q_ref[:, h * CAT_W:h * CAT_W + HEAD_DIM], gn_ref[...])).astype(BF16)
            qr = _rms(q_ref[:, h * CAT_W + HEAD_DIM:(h + 1) * CAT_W], gr_ref[...], n=ROPE_DIM)
            o_ref[:, h * CAT_W + HEAD_DIM:(h + 1) * CAT_W] = (
                MLA_Q_SCALE * _rope(qr, c_ref[...], s_ref[...])).astype(BF16)

    (q_cat,) = _rowwise(qcat_body, seq,
                        [(q_up, True), (g["b_g_q_nope"], False), (g_q_rope, False), (cos_t, True), (sin_t, True)],
                        [(N_MLA_HEADS * CAT_W, BF16)], name="q_prep_fwd")
    att, lse = _mla_fwd(q_cat, k_cat, v_mla, name="mla_fwd")
    mem1 = _mem_side_fwd(mem, g["mem_norm"][1:2], w["mem_kv"][1], g["g_mem_k"][1:2], tag="b")
    mixed_b = _mix_fwd(att, p2, Q_LORA, mem1[2], mem1[3], g["g_mem_q"][1:2], name="b_mix_fwd")
    dy, loss_part = _out_proj_loss(mixed_b, w["b_out"], x1, target, name="b_out_loss")

    gw, gg = {}, {}
    dmixed_b = _mm(dy, w["b_out"], tb=True, name="b_out_dx")
    gw["b_out"] = _mm(mixed_b, dy, ta=True, out_dtype=BF16, name="b_out_dw")
    datt, dgate_b, dmk1, dmv1, gq1 = _mix_bwd(dmixed_b, att, p2, Q_LORA, mem1[2], mem1[3],
                                              g["g_mem_q"][1:2], name="b_mix_bwd")
    dq_cat, dk_cat, dv_mla = _mla_bwd(q_cat, k_cat, v_mla, att, lse, datt, name="mla_bwd")

    def qcat_bwd_body(ins, outs, accs):
        q_ref, dq_ref, gn_ref, gr_ref, c_ref, s_ref = ins
        (o_ref,) = outs
        dgn_ref, dgr_ref = accs
        for h in range(N_MLA_HEADS):
            dx, dg = _rms_bwd(q_ref[:, h * CAT_W:h * CAT_W + HEAD_DIM], gn_ref[...],
                              dq_ref[:, h * CAT_W:h * CAT_W + HEAD_DIM])
            o_ref[:, h * CAT_W:h * CAT_W + HEAD_DIM] = dx.astype(BF16)
            dgn_ref[...] += dg
            dn = _rope_bwd(dq_ref[:, h * CAT_W + HEAD_DIM:(h + 1) * CAT_W], c_ref[...], s_ref[...])
            dx, dg = _rms_bwd(q_ref[:, h * CAT_W + HEAD_DIM:(h + 1) * CAT_W], gr_ref[...], dn, n=ROPE_DIM)
            o_ref[:, h * CAT_W + HEAD_DIM:(h + 1) * CAT_W] = dx.astype(BF16)
            dgr_ref[...] += dg

    dq_up, gg["b_g_q_nope"], dgqr = _rowwise(
        qcat_bwd_body, seq,
        [(q_up, True), (dq_cat, True), (g["b_g_q_nope"], False), (g_q_rope, False), (cos_t, True), (sin_t, True)],
        [(N_MLA_HEADS * CAT_W, BF16)], [((1, HEAD_DIM), F32), ((1, HEAD_DIM), F32)], name="q_prep_bwd")
    gg["b_g_q_rope"] = dgqr
    dq_l = _mm(dq_up, w["uq"], tb=True, name="uq_dx")
    gw["uq"] = _mm(q_l, dq_up, ta=True, out_dtype=BF16, n_split=N_CHIPS, name="uq_dw")

    def qlat_bwd_body(ins, outs, accs):
        p2_ref, dql_ref, dgate_ref, gl_ref = ins
        dx, dg = _rms_bwd(p2_ref[:, :Q_LORA], gl_ref[...], dql_ref[...])
        outs[0][:, :Q_LORA] = dx.astype(BF16)
        outs[0][:, Q_LORA:] = dgate_ref[...]
        accs[0][...] += dg

    dp2, gg["b_g_q_lat"] = _rowwise(
        qlat_bwd_body, seq, [(p2, Q_LORA), (dq_l, True), (dgate_b, True), (g["b_g_q_lat"], False)],
        [(Q_LORA + GATE_W, BF16)], [((1, Q_LORA), F32)], name="q_lat_norm_bwd")
    dh_b = _mm(dp2, w["b_in"], tb=True, name="b_in_dx")
    gw["b_in"] = _mm(h_b, dp2, ta=True, out_dtype=BF16, n_split=N_CHIPS, name="b_in_dw")

    def kcat_bwd_body(ins, outs, accs):
        kv_ref, dkc_ref, dv_ref, gk_ref = ins
        dkv_ref, dkr_ref = outs
        (dgk_ref,) = accs
        dkr = jnp.zeros(dkr_ref.shape, F32)
        for h in range(N_MLA_HEADS):
            dx, dg = _rms_bwd(kv_ref[:, h * CAT_W:h * CAT_W + HEAD_DIM], gk_ref[...],
                              dkc_ref[:, h * CAT_W:h * CAT_W + HEAD_DIM])
            dkv_ref[:, h * CAT_W:h * CAT_W + HEAD_DIM] = dx.astype(BF16)
            dgk_ref[...] += dg
            dkv_ref[:, h * CAT_W + HEAD_DIM:(h + 1) * CAT_W] = dv_ref[:, _hs(h)].astype(BF16)
            dkr = dkr + dkc_ref[:, h * CAT_W + HEAD_DIM:(h + 1) * CAT_W]
        dkr_ref[...] = dkr

    dkv, dk_r, gg["g_k_nope"] = _rowwise(
        kcat_bwd_body, seq, [(kv, True), (dk_cat, True), (dv_mla, True), (g["g_k_nope"], False)],
        [(N_MLA_HEADS * CAT_W, BF16), (HEAD_DIM, F32)], [((1, HEAD_DIM), F32)], name="k_prep_bwd")
    dc_n = _mm(dkv, w["ukv"], tb=True, name="ukv_dx")
    gw["ukv"] = _mm(c_n, dkv, ta=True, out_dtype=BF16, n_split=N_CHIPS, name="ukv_dw")

    def ckr_bwd_body(ins, outs, accs):
        ckr_ref, dcn_ref, dkr_ref, gc_ref, gr_ref, c_ref, s_ref = ins
        dx, dg = _rms_bwd(ckr_ref[:, :KV_LORA], gc_ref[...], dcn_ref[...])
        outs[0][:, :KV_LORA] = dx.astype(BF16)
        accs[0][...] += dg
        dn = _rope_bwd(dkr_ref[...], c_ref[...], s_ref[...])
        dx, dg = _rms_bwd(ckr_ref[:, KV_LORA:], gr_ref[...], dn, n=ROPE_DIM)
        outs[0][:, KV_LORA:] = dx.astype(BF16)
        accs[1][...] += dg

    dckr, gg["g_ckv"], gg["g_k_rope"] = _rowwise(
        ckr_bwd_body, seq,
        [(ckr, True), (dc_n, True), (dk_r, True), (g["g_ckv"], False), (g_k_rope, False),
         (cos_t, True), (sin_t, True)],
        [(KV_LORA + HEAD_DIM, BF16)], [((1, KV_LORA), F32), ((1, HEAD_DIM), F32)], name="ckv_prep_bwd")
    dh_kv = _mm(dckr, w["dkv"], tb=True, name="dkv_dx")
    gw["dkv"] = _mm(h_kv, dckr, ta=True, out_dtype=BF16, name="dkv_dw")

    def norms2_bwd_body(ins, outs, accs):
        x_ref, dy_ref, dhk_ref, dhb_ref, gk_ref, gb_ref = ins
        xv = x_ref[...]
        dxk, dgk = _rms_bwd(xv, gk_ref[...], dhk_ref[...])
        dxb, dgb = _rms_bwd(xv, gb_ref[...], dhb_ref[...])
        outs[0][...] = dy_ref[...] + dxk + dxb
        accs[0][...] += dgk
        accs[1][...] += dgb

    dx1, gg["kv_norm"], gg["b_norm"] = _rowwise(
        norms2_bwd_body, seq,
        [(x1, True), (dy, True), (dh_kv, True), (dh_b, True), (g["kv_norm"], False), (g["b_norm"], False)],
        [(D_MODEL, F32)], [((1, D_MODEL), F32), ((1, D_MODEL), F32)], name="kv_b_norm_bwd")

    dmixed_a = _mm(dx1, w["a_out"], tb=True, name="a_out_dx")
    gw["a_out"] = _mm(mixed_a, dx1, ta=True, out_dtype=BF16, name="a_out_dw")
    dsb, dgate_a, dmk0, dmv0, gq0 = _mix_bwd(dmixed_a, sb, gr, 0, mem0[2], mem0[3],
                                             g["g_mem_q"][0:1], name="a_mix_bwd")
    dq, dk, dv = _sb_bwd(qkv, sb, dsb, name="sb_bwd")
    dp_a = jnp.concatenate([dq, dk, dv, dgate_a], axis=1)
    dh_a = _mm(dp_a, w["a_in"], tb=True, name="a_in_dx")
    gw["a_in"] = _mm(h_a, dp_a, ta=True, out_dtype=BF16, n_split=N_CHIPS, name="a_in_dw")

    def norm_a_bwd_body(ins, outs, accs):
        dx, dg = _rms_bwd(ins[0][...], ins[3][...], ins[2][...])
        outs[0][...] = ins[1][...] + dx
        accs[0][...] += dg

    grad_x, gg["a_norm"] = _rowwise(
        norm_a_bwd_body, seq, [(x, True), (dx1, True), (dh_a, True), (g["a_norm"], False)],
        [(D_MODEL, F32)], [((1, D_MODEL), F32)], name="a_norm_bwd")

    dw0, dgn0, dgk0 = _mem_side_bwd(mem, g["mem_norm"][0:1], w["mem_kv"][0], g["g_mem_k"][0:1],
                                    mem0[0], mem0[1], dmk0, dmv0, tag="a")
    dw1, dgn1, dgk1 = _mem_side_bwd(mem, g["mem_norm"][1:2], w["mem_kv"][1], g["g_mem_k"][1:2],
                                    mem1[0], mem1[1], dmk1, dmv1, tag="b")
    gw["mem_kv"] = (dw0, dw1)
    gg["mem_norm"] = jnp.concatenate([dgn0, dgn1], axis=0)
    gg["g_mem_q"] = jnp.concatenate([gq0, gq1], axis=0)
    gg["g_mem_k"] = jnp.concatenate([dgk0, dgk1], axis=0)
    return loss_part, grad_x, gw, gg


HBM_SPEC = pl.BlockSpec(memory_space=pl.ANY)


def _other_chips():
    x, y = lax.axis_index("x"), lax.axis_index("y")
    return [(1 - x, y), (x, 1 - y), (1 - x, 1 - y)]


def _allgather_chips(shards):
    n = len(shards)
    split = [s.shape[0] % 32 == 0 for s in shards]

    def body(*refs):
        ins, outs = refs[:n], refs[n:2 * n]
        send, recv, fsend, frecv = refs[2 * n:]
        x, y, c = lax.axis_index("x"), lax.axis_index("y"), lax.axis_index("c")
        me = 2 * x + y
        chips = _other_chips()

        def part(ref, wi):
            if not split[wi]:
                return ref
            half = shards[wi].shape[0] // 2
            return ref.at[pl.ds(pl.multiple_of(c * half, 16), half)]

        def ici(wi, k, src_chip, to):
            return pltpu.make_async_remote_copy(
                src_ref=part(ins[wi], wi), dst_ref=part(outs[wi].at[src_chip], wi),
                send_sem=send.at[wi, k], recv_sem=recv.at[wi, k], device_id=to, device_id_type=MESH)

        def d2d(wi, k, src_chip):
            rows = part(outs[wi].at[src_chip], wi)
            return pltpu.make_async_remote_copy(
                src_ref=rows, dst_ref=rows, send_sem=fsend.at[wi, k], recv_sem=frecv.at[wi, k],
                device_id=(x, y, 1 - c), device_id_type=MESH)

        for wi in range(n):
            for k, (tx, ty) in enumerate(chips):
                ici(wi, k, me, (tx, ty, c)).start()
        for wi in range(n):
            for k, (tx, ty) in enumerate(chips):
                landed = ici(wi, k, 2 * tx + ty, (tx, ty, c))
                landed.wait_recv()
                if split[wi]:
                    d2d(wi, k, 2 * tx + ty).start()
        for wi in range(n):
            for k, (tx, ty) in enumerate(chips):
                ici(wi, k, me, (tx, ty, c)).wait_send()
                if split[wi]:
                    fwd = d2d(wi, k, 2 * tx + ty)
                    fwd.wait_send()
                    fwd.wait_recv()

    return pl.pallas_call(
        body, out_shape=[jax.ShapeDtypeStruct((N_CHIPS,) + s.shape, s.dtype) for s in shards],
        in_specs=[HBM_SPEC] * n, out_specs=[HBM_SPEC] * n,
        scratch_shapes=[pltpu.SemaphoreType.DMA((n, 3)), pltpu.SemaphoreType.DMA((n, 3)),
                        pltpu.SemaphoreType.DMA((n, 3)), pltpu.SemaphoreType.DMA((n, 3))],
        name="allgather_weights",
    )(*shards)


def _scatter_to_chips(grads):
    n = len(grads)

    def body(*refs):
        ins, outs = refs[:n], refs[n:2 * n]
        send, recv = refs[2 * n:]
        c = lax.axis_index("c")
        copies = []
        for wi in range(n):
            for k, (tx, ty) in enumerate(_other_chips()):
                cp = pltpu.make_async_remote_copy(
                    src_ref=ins[wi].at[2 * tx + ty], dst_ref=outs[wi].at[k], send_sem=send.at[wi, k],
                    recv_sem=recv.at[wi, k], device_id=(tx, ty, c), device_id_type=MESH)
                cp.start()
                copies.append(cp)
        for cp in copies:
            cp.wait()

    return pl.pallas_call(
        body, out_shape=[jax.ShapeDtypeStruct((3,) + s.shape[1:], s.dtype) for s in grads],
        in_specs=[HBM_SPEC] * n, out_specs=[HBM_SPEC] * n,
        scratch_shapes=[pltpu.SemaphoreType.DMA((n, 3)), pltpu.SemaphoreType.DMA((n, 3))],
        name="scatter_grads",
    )(*grads)


def _halve_with_sibling(grads):
    n = len(grads)
    n_slots = grads[0].shape[0]

    def body(*refs):
        ins, got = refs[:n], refs[n:2 * n]
        send, recv = refs[2 * n:]
        c = lax.axis_index("c")
        sib = (lax.axis_index("x"), lax.axis_index("y"), 1 - c)
        copies = []
        for wi in range(n):
            half = grads[wi].shape[1] // 2
            for s in range(n_slots):
                theirs = ins[wi].at[s, pl.ds(pl.multiple_of((1 - c) * half, 16), half)]
                give = pltpu.make_async_remote_copy(
                    src_ref=theirs, dst_ref=got[wi].at[s], send_sem=send.at[wi, s], recv_sem=recv.at[wi, s],
                    device_id=sib, device_id_type=MESH)
                give.start()
                copies.append(give)
        for cp in copies:
            cp.wait()

    halves = [jax.ShapeDtypeStruct((s.shape[0], s.shape[1] // 2) + s.shape[2:], s.dtype) for s in grads]
    return pl.pallas_call(
        body, out_shape=halves, in_specs=[HBM_SPEC] * n, out_specs=[HBM_SPEC] * n,
        scratch_shapes=[pltpu.SemaphoreType.DMA((n, n_slots)), pltpu.SemaphoreType.DMA((n, n_slots))],
        name="halve_grads_with_sibling",
    )(*grads)


def _swap_with_sibling(parts):
    n = len(parts)

    def body(*refs):
        ins, outs = refs[:n], refs[n:2 * n]
        send, recv = refs[2 * n:]
        sib = (lax.axis_index("x"), lax.axis_index("y"), 1 - lax.axis_index("c"))
        copies = []
        for wi in range(n):
            cp = pltpu.make_async_remote_copy(
                src_ref=ins[wi], dst_ref=outs[wi], send_sem=send.at[wi], recv_sem=recv.at[wi],
                device_id=sib, device_id_type=MESH)
            cp.start()
            copies.append(cp)
        for cp in copies:
            cp.wait()

    return pl.pallas_call(
        body, out_shape=[jax.ShapeDtypeStruct(s.shape, s.dtype) for s in parts],
        in_specs=[HBM_SPEC] * n, out_specs=[HBM_SPEC] * n,
        scratch_shapes=[pltpu.SemaphoreType.DMA((n,)), pltpu.SemaphoreType.DMA((n,))],
        name="swap_grad_halves",
    )(*parts)


def _allreduce_small(vec, loss_row):
    rows = vec.shape[0]

    def body(v_ref, o_ref, buf, send, recv):
        x, y, c = lax.axis_index("x"), lax.axis_index("y"), lax.axis_index("c")
        me = 4 * x + 2 * y + c
        buf[me] = v_ref[...]
        copies = []
        for r in range(1, N_DEV):
            peer = (x ^ ((r >> 2) & 1), y ^ ((r >> 1) & 1), c ^ (r & 1))
            cp = pltpu.make_async_remote_copy(
                src_ref=v_ref, dst_ref=buf.at[me], send_sem=send.at[r - 1], recv_sem=recv.at[r - 1],
                device_id=peer, device_id_type=MESH)
            cp.start()
            copies.append(cp)
        for cp in copies:
            cp.wait()
        total = buf[0]
        for d in range(1, N_DEV):
            total = total + buf[d]
        o_ref[...] = total
        o_ref[loss_row:loss_row + 1, :] = jnp.broadcast_to(
            jnp.sum(total[loss_row:loss_row + 1, :], axis=-1, keepdims=True), (1, HEAD_DIM))

    return pl.pallas_call(
        body, out_shape=jax.ShapeDtypeStruct(vec.shape, F32),
        in_specs=[pl.BlockSpec(memory_space=pltpu.VMEM)], out_specs=pl.BlockSpec(memory_space=pltpu.VMEM),
        scratch_shapes=[pltpu.VMEM((N_DEV, rows, HEAD_DIM), F32),
                        pltpu.SemaphoreType.DMA((N_DEV - 1,)), pltpu.SemaphoreType.DMA((N_DEV - 1,))],
        name="allreduce_gains",
    )(vec)


def _pair_sum(grads, got, *, name):
    slots, rows, width = got.shape
    blk = _pick(rows, (256, 128, 64, 32, 16))
    nbh = rows // blk

    def body(lo_ref, hi_ref, got_ref, o_ref):
        mine = jnp.where(lax.axis_index("c") == 0, lo_ref[...], hi_ref[...])
        o_ref[...] = (mine.astype(F32) + got_ref[...].astype(F32)).astype(BF16)

    spec = pl.BlockSpec((None, blk, width), lambda s, i: (s, i, 0))
    return pl.pallas_call(
        body, out_shape=jax.ShapeDtypeStruct(got.shape, BF16), grid=(slots, nbh),
        in_specs=[spec, pl.BlockSpec((None, blk, width), lambda s, i: (s, nbh + i, 0)), spec],
        out_specs=spec, name=name, compiler_params=_params("parallel", "parallel"),
    )(grads, grads, got)


def _sum_slots(recv, chip_sum, *, name):
    _, rows, width = recv.shape
    blk = _pick(rows, (256, 128, 64, 32, 16, 8))

    def body(r_ref, p_ref, o_ref):
        me = 2 * lax.axis_index("x") + lax.axis_index("y")
        own = jnp.where(me < 2, jnp.where(me == 0, p_ref[0], p_ref[1]), jnp.where(me == 2, p_ref[2], p_ref[3]))
        o_ref[...] = ((own.astype(F32) + r_ref[0].astype(F32)) + r_ref[1].astype(F32)) + r_ref[2].astype(F32)

    return pl.pallas_call(
        body, out_shape=jax.ShapeDtypeStruct((rows, width), F32), grid=(rows // blk,),
        in_specs=[pl.BlockSpec((3, blk, width), lambda i: (0, i, 0)),
                  pl.BlockSpec((N_CHIPS, blk, width), lambda i: (0, i, 0))],
        out_specs=pl.BlockSpec((blk, width), lambda i: (i, 0)),
        name=name, compiler_params=_params("parallel"),
    )(recv, chip_sum)


def _adamw(wgt, grad, m, v, *, name, halves=None):
    rows, width = wgt.shape
    blk = _pick(rows // 2 if halves else rows, (256, 128, 64, 32, 16, 8))
    nbh = rows // 2 // blk

    def body(*refs):
        if halves:
            w_ref, mine_ref, theirs_ref, m_ref, v_ref, g_out, d_out, m_out, v_out = refs
            grad_v = jnp.where(pl.program_id(0) // nbh == lax.axis_index("c"), mine_ref[...], theirs_ref[...])
        else:
            w_ref, g_ref, m_ref, v_ref, g_out, d_out, m_out, v_out = refs
            grad_v = g_ref[...]
        m_new = ADAM_B1 * m_ref[...] + (1.0 - ADAM_B1) * grad_v
        v_new = ADAM_B2 * v_ref[...] + (1.0 - ADAM_B2) * (grad_v * grad_v)
        m_hat = m_new / (1.0 - ADAM_B1 ** ADAM_STEP)
        v_hat = v_new / (1.0 - ADAM_B2 ** ADAM_STEP)
        g_out[...] = grad_v
        d_out[...] = -ADAM_LR * (m_hat / (jnp.sqrt(v_hat) + ADAM_EPS) + ADAM_WD * w_ref[...])
        m_out[...] = m_new
        v_out[...] = v_new

    spec = pl.BlockSpec((blk, width), lambda i: (i, 0))
    half_spec = pl.BlockSpec((blk, width), lambda i: (i % nbh, 0))
    g_specs, g_args = ([half_spec, half_spec], list(halves)) if halves else ([spec], [grad])
    out = jax.ShapeDtypeStruct((rows, width), F32)
    return pl.pallas_call(
        body, out_shape=[out] * 4, grid=(rows // blk,), in_specs=[spec] + g_specs + [spec, spec],
        out_specs=[spec] * 4, name=name, compiler_params=_params("parallel"),
    )(wgt, *g_args, m, v)


_SMALL = (("a_norm", 2048), ("kv_norm", 2048), ("g_ckv", 512), ("g_k_nope", 128), ("g_k_rope", 64),
          ("b_norm", 2048), ("b_g_q_lat", 512), ("b_g_q_nope", 128), ("b_g_q_rope", 64),
          ("mem_norm", 4096), ("g_mem_q", 256), ("g_mem_k", 256))


def _lanes(n):
    return -(-n // HEAD_DIM) * HEAD_DIM


def _pack_rows(pieces, pad_rows_to=8):
    flat = jnp.concatenate(pieces, axis=1)
    rows = flat.shape[1] // HEAD_DIM
    pad = (-rows) % pad_rows_to
    if pad:
        flat = jnp.concatenate([flat, jnp.zeros((1, pad * HEAD_DIM), F32)], axis=1)
    return flat.reshape(rows + pad, HEAD_DIM)


def _pad_lanes(a):
    a = a.reshape(1, -1)
    pad = _lanes(a.shape[1]) - a.shape[1]
    if pad:
        a = jnp.concatenate([a, jnp.zeros((1, pad), F32)], axis=1)
    return a


def kernel(x, mem, positions, a_norm, a_w_in, a_w_out, kv_norm, w_dkv, g_ckv, w_ukv, g_k_nope, g_k_rope, b_norm, b_w_in, b_g_q_lat, b_w_uq, b_g_q_nope, b_g_q_rope, b_w_out, mem_norm, w_mem_kv, g_mem_q, g_mem_k, loss_target, m_a_norm, m_a_w_in, m_a_w_out, m_kv_norm, m_w_dkv, m_g_ckv, m_w_ukv, m_g_k_nope, m_g_k_rope, m_b_norm, m_b_w_in, m_b_g_q_lat, m_b_w_uq, m_b_g_q_nope, m_b_g_q_rope, m_b_w_out, m_mem_norm, m_w_mem_kv, m_g_mem_q, m_g_mem_k, v_a_norm, v_a_w_in, v_a_w_out, v_kv_norm, v_w_dkv, v_g_ckv, v_w_ukv, v_g_k_nope, v_g_k_rope, v_b_norm, v_b_w_in, v_b_g_q_lat, v_b_w_uq, v_b_g_q_nope, v_b_g_q_rope, v_b_w_out, v_mem_norm, v_w_mem_kv, v_g_mem_q, v_g_mem_k):
    chip = 2 * lax.axis_index("x") + lax.axis_index("y")
    rows_dkv = D_MODEL // N_CHIPS
    heads_per_chip = N_MLA_HEADS // N_CHIPS
    qk_w = HEAD_DIM + ROPE_DIM

    big = {"a_in": a_w_in[0], "a_out": a_w_out[0], "dkv": w_dkv, "ukv": w_ukv, "b_in": b_w_in[0],
           "uq": b_w_uq[0], "b_out": b_w_out[0], "mem_kv": w_mem_kv.reshape(2 * rows_dkv, 2 * MEM_W)}
    big_m = {"a_in": m_a_w_in[0], "a_out": m_a_w_out[0], "dkv": m_w_dkv, "ukv": m_w_ukv, "b_in": m_b_w_in[0],
             "uq": m_b_w_uq[0], "b_out": m_b_w_out[0], "mem_kv": m_w_mem_kv.reshape(2 * rows_dkv, 2 * MEM_W)}
    big_v = {"a_in": v_a_w_in[0], "a_out": v_a_w_out[0], "dkv": v_w_dkv, "ukv": v_w_ukv, "b_in": v_b_w_in[0],
             "uq": v_b_w_uq[0], "b_out": v_b_w_out[0], "mem_kv": v_w_mem_kv.reshape(2 * rows_dkv, 2 * MEM_W)}
    names = list(big)
    own_shards = [big[n].astype(BF16) for n in names] + [a_norm]
    gathered = _allgather_chips(own_shards)
    gathered = [lax.dynamic_update_slice(g, s[None], (chip,) + (0,) * s.ndim)
                for g, s in zip(gathered, own_shards)]
    st = dict(zip(names, gathered[:-1]))
    a_in_full = st["a_in"].transpose(1, 0, 2).reshape(D_MODEL, QKV_W + GATE_W)
    uq = st["uq"].reshape(N_CHIPS, Q_LORA, heads_per_chip, qk_w)
    uq = jnp.pad(uq, ((0, 0), (0, 0), (0, 0), (0, CAT_W - qk_w)))
    w = {
        "a_in": a_in_full,
        "a_in_qkv": a_in_full[:, :QKV_W],
        "a_in_gate": a_in_full[:, QKV_W:],
        "a_out": st["a_out"].reshape(D_MODEL, D_MODEL),
        "dkv": jnp.pad(st["dkv"].reshape(D_MODEL, KV_LORA + ROPE_DIM), ((0, 0), (0, HEAD_DIM - ROPE_DIM))),
        "ukv": st["ukv"].transpose(1, 0, 2).reshape(KV_LORA, N_MLA_HEADS * CAT_W),
        "b_in": st["b_in"].transpose(1, 0, 2).reshape(D_MODEL, Q_LORA + GATE_W),
        "uq": uq.transpose(1, 0, 2, 3).reshape(Q_LORA, N_MLA_HEADS * CAT_W),
        "b_out": st["b_out"].reshape(D_MODEL, D_MODEL),
        "mem_kv": st["mem_kv"].reshape(N_CHIPS, 2, rows_dkv, 2 * MEM_W).transpose(1, 0, 2, 3).reshape(
            2, D_MODEL, 2 * MEM_W),
    }
    gains = {
        "a_norm": gathered[-1].reshape(1, D_MODEL), "kv_norm": kv_norm.reshape(1, -1),
        "g_ckv": g_ckv.reshape(1, -1), "g_k_nope": g_k_nope.reshape(1, -1), "g_k_rope": g_k_rope.reshape(1, -1),
        "b_norm": b_norm, "b_g_q_lat": b_g_q_lat, "b_g_q_nope": b_g_q_nope, "b_g_q_rope": b_g_q_rope,
        "mem_norm": mem_norm, "g_mem_q": g_mem_q, "g_mem_k": g_mem_k,
    }

    loss_part, grad_x, gw, gg = _local_step(x[0], mem[0], positions[0], loss_target[0], w, gains)

    stacked = {
        "a_in": gw["a_in"],
        "a_out": gw["a_out"].reshape(N_CHIPS, rows_dkv, D_MODEL),
        "dkv": gw["dkv"][:, :KV_LORA + ROPE_DIM].reshape(N_CHIPS, rows_dkv, KV_LORA + ROPE_DIM),
        "ukv": gw["ukv"],
        "b_in": gw["b_in"],
        "uq": gw["uq"].reshape(N_CHIPS, Q_LORA, heads_per_chip, CAT_W)[..., :qk_w].reshape(
            N_CHIPS, Q_LORA, heads_per_chip * qk_w),
        "b_out": gw["b_out"].reshape(N_CHIPS, rows_dkv, D_MODEL),
        "mem_kv": jnp.stack([gw["mem_kv"][0].reshape(N_CHIPS, rows_dkv, 2 * MEM_W),
                             gw["mem_kv"][1].reshape(N_CHIPS, rows_dkv, 2 * MEM_W)], axis=1).reshape(
            N_CHIPS, 2 * rows_dkv, 2 * MEM_W),
    }
    got = _halve_with_sibling([stacked[n] for n in names])
    chip_sum = [_pair_sum(stacked[n], g, name=f"pair_sum_{n}") for n, g in zip(names, got)]
    received = _scatter_to_chips(chip_sum)
    half_total = [_sum_slots(r, p, name=f"sum_slots_{n}") for n, r, p in zip(names, received, chip_sum)]
    sibling_half = _swap_with_sibling(half_total)
    big_out = {}
    for n, mine, theirs in zip(names, half_total, sibling_half):
        big_out[n] = _adamw(big[n], None, big_m[n], big_v[n], halves=(mine, theirs), name=f"adamw_{n}")

    pieces = [_pad_lanes(gg[n]) if n not in ("g_k_rope", "b_g_q_rope") else gg[n] for n, _ in _SMALL]
    pieces.append(loss_part)
    loss_row = sum(_lanes(size) for _, size in _SMALL) // HEAD_DIM
    summed = _allreduce_small(_pack_rows(pieces), loss_row)
    flat = summed.reshape(1, -1)
    small_g, off = {}, 0
    for n, size in _SMALL:
        small_g[n] = flat[:, off:off + size]
        off += _lanes(size)
    loss = flat[0, off]
    small_g["a_norm"] = lax.dynamic_slice(small_g["a_norm"], (0, chip * rows_dkv), (1, rows_dkv))

    small_w = {"a_norm": a_norm, "kv_norm": kv_norm, "g_ckv": g_ckv, "g_k_nope": g_k_nope, "g_k_rope": g_k_rope,
               "b_norm": b_norm, "b_g_q_lat": b_g_q_lat, "b_g_q_nope": b_g_q_nope, "b_g_q_rope": b_g_q_rope,
               "mem_norm": mem_norm, "g_mem_q": g_mem_q, "g_mem_k": g_mem_k}
    small_m = {"a_norm": m_a_norm, "kv_norm": m_kv_norm, "g_ckv": m_g_ckv, "g_k_nope": m_g_k_nope,
               "g_k_rope": m_g_k_rope, "b_norm": m_b_norm, "b_g_q_lat": m_b_g_q_lat, "b_g_q_nope": m_b_g_q_nope,
               "b_g_q_rope": m_b_g_q_rope, "mem_norm": m_mem_norm, "g_mem_q": m_g_mem_q, "g_mem_k": m_g_mem_k}
    small_v = {"a_norm": v_a_norm, "kv_norm": v_kv_norm, "g_ckv": v_g_ckv, "g_k_nope": v_g_k_nope,
               "g_k_rope": v_g_k_rope, "b_norm": v_b_norm, "b_g_q_lat": v_b_g_q_lat, "b_g_q_nope": v_b_g_q_nope,
               "b_g_q_rope": v_b_g_q_rope, "mem_norm": v_mem_norm, "g_mem_q": v_g_mem_q, "g_mem_k": v_g_mem_k}
    snames = [n for n, _ in _SMALL]
    packs = [_pack_rows([_pad_lanes(src[n]) for n in snames])
             for src in (small_w, small_g, small_m, small_v)]
    small_res = _adamw(packs[0], packs[1], packs[2], packs[3], name="adamw_gains")
    small_out = {n: [] for n in snames}
    for res in small_res:
        flat_r = res.reshape(1, -1)
        off = 0
        for n in snames:
            size = small_w[n].size
            small_out[n].append(flat_r[:, off:off + size].reshape(small_w[n].shape))
            off += _lanes(size)

    big_names = {"a_w_in": ("a_in", a_w_in), "a_w_out": ("a_out", a_w_out), "w_dkv": ("dkv", w_dkv),
                 "w_ukv": ("ukv", w_ukv), "b_w_in": ("b_in", b_w_in), "b_w_uq": ("uq", b_w_uq),
                 "b_w_out": ("b_out", b_w_out), "w_mem_kv": ("mem_kv", w_mem_kv)}
    order = ["a_norm", "a_w_in", "a_w_out", "kv_norm", "w_dkv", "g_ckv", "w_ukv", "g_k_nope", "g_k_rope",
             "b_norm", "b_w_in", "b_g_q_lat", "b_w_uq", "b_g_q_nope", "b_g_q_rope", "b_w_out", "mem_norm",
             "w_mem_kv", "g_mem_q", "g_mem_k"]
    groups = [[], [], [], []]
    for n in order:
        if n in big_names:
            key, ref_arr = big_names[n]
            for t in range(4):
                groups[t].append(big_out[key][t].reshape(ref_arr.shape))
        else:
            for t in range(4):
                groups[t].append(small_out[n][t])
    return (loss, grad_x[None], *groups[0], *groups[1], *groups[2], *groups[3])
```

```python
import jax
import jax.numpy as jnp
from jax import lax
from jax.experimental import pallas as pl
from jax.experimental.pallas import tpu as pltpu

F32 = jnp.float32
BF16 = jnp.bfloat16
MESH = pl.DeviceIdType.MESH

D_MODEL = 2048
HEAD_DIM = 128
N_SB_HEADS = 12
N_MEM_HEADS = 4
N_MLA_HEADS = 12
MEM_LEN = 256
Q_LORA = 512
KV_LORA = 512
ROPE_DIM = 64
SB_W = N_SB_HEADS * HEAD_DIM
MEM_W = N_MEM_HEADS * HEAD_DIM
MLA_W = N_MLA_HEADS * HEAD_DIM
QKV_W = 3 * SB_W
GATE_W = SB_W + 2 * MEM_W
CAT_W = 2 * HEAD_DIM
ROPE_THETA = 10000.0
EPS = 1e-6
N_CHIPS = 4
N_DEV = 8

ADAM_LR = 0.001
ADAM_B1 = 0.9
ADAM_B2 = 0.999
ADAM_EPS = 1e-08
ADAM_WD = 0.01
ADAM_STEP = 10

VMEM_LIMIT_BYTES = 56 * 1024 * 1024
MM_OPERAND_VMEM_BYTES = 24 * 1024 * 1024
ROW_BLOCK = 256
NORM_CHUNK = 32
ATT_BLOCK = 256


def _params(*sem):
    return pltpu.CompilerParams(dimension_semantics=sem, vmem_limit_bytes=VMEM_LIMIT_BYTES)


def _pick(n, cands):
    for c in cands:
        if n % c == 0:
            return c
    return n


def _mm(a, b, *, name, ta=False, tb=False, out_dtype=F32, res=None, n_split=1, scale_cols=None):
    if ta:
        k_dim, m_dim = a.shape
    else:
        m_dim, k_dim = a.shape
    if tb:
        n_dim, kb = b.shape
    else:
        kb, n_dim = b.shape
    assert kb == k_dim, (a.shape, b.shape)
    n_per = n_dim // n_split
    bm = m_dim if m_dim <= 1024 else _pick(m_dim, (1024, 512, 256))
    bn = n_per if n_per <= 1024 else _pick(n_per, (1024, 896, 768, 640, 512, 256, 128))
    per_k = (bm * a.dtype.itemsize + bn * b.dtype.itemsize) * 2
    bk = next((k_dim // d for d in range(1, k_dim // 128 + 1)
               if k_dim % d == 0 and (k_dim // d) % 128 == 0 and (k_dim // d) * per_k <= MM_OPERAND_VMEM_BYTES),
              k_dim)
    nk = k_dim // bk
    nb_per = n_per // bn
    grid = (m_dim // bm, n_dim // bn, nk)
    a_spec = (pl.BlockSpec((bk, bm), lambda i, j, k: (k, i)) if ta
              else pl.BlockSpec((bm, bk), lambda i, j, k: (i, k)))
    b_spec = (pl.BlockSpec((bn, bk), lambda i, j, k: (j, k)) if tb
              else pl.BlockSpec((bk, bn), lambda i, j, k: (k, j)))
    dims = (((0 if ta else 1,), (1 if tb else 0,)), ((), ()))
    in_specs = [a_spec, b_spec]
    args = [a, b]
    if res is not None:
        in_specs.append(pl.BlockSpec((bm, bn), lambda i, j, k: (i, j)))
        args.append(res)
    if n_split == 1:
        out_shape = jax.ShapeDtypeStruct((m_dim, n_dim), out_dtype)
        out_spec = pl.BlockSpec((bm, bn), lambda i, j, k: (i, j))
    else:
        out_shape = jax.ShapeDtypeStruct((n_split, m_dim, n_per), out_dtype)
        out_spec = pl.BlockSpec((None, bm, bn), lambda i, j, k: (j // nb_per, i, j % nb_per))

    def body(*refs):
        if res is None:
            a_ref, b_ref, o_ref, acc = refs
            r_ref = None
        else:
            a_ref, b_ref, r_ref, o_ref, acc = refs
        k = pl.program_id(2)
        col_block = pl.program_id(1)

        @pl.when(k == 0)
        def _():
            acc[...] = jnp.zeros_like(acc)

        acc[...] += lax.dot_general(a_ref[...].astype(BF16), b_ref[...].astype(BF16), dims,
                                    preferred_element_type=F32)

        @pl.when(k == nk - 1)
        def _():
            r = acc[...]
            if r_ref is not None:
                r = r + r_ref[...]
            if scale_cols is not None:
                assert scale_cols[0] % bn == 0
                r = r * jnp.where(col_block < scale_cols[0] // bn, scale_cols[1], 1.0)
            o_ref[...] = r.astype(out_dtype)

    return pl.pallas_call(
        body, out_shape=out_shape, grid=grid, in_specs=in_specs, out_specs=out_spec,
        scratch_shapes=[pltpu.VMEM((bm, bn), F32)], name=name,
        compiler_params=_params("parallel", "parallel", "arbitrary"),
    )(*args)


def _out_proj_loss(mixed, w_out, resid, target, *, name):
    seq, k_dim = mixed.shape
    bm = min(1024, seq)
    bn = 1024

    def body(a_ref, b_ref, r_ref, t_ref, dy_ref, loss_ref):
        @pl.when((pl.program_id(0) == 0) & (pl.program_id(1) == 0))
        def _():
            loss_ref[...] = jnp.zeros_like(loss_ref)

        diff = _dot(a_ref[...], b_ref[...]) + r_ref[...] - t_ref[...]
        dy_ref[...] = diff / D_MODEL
        col = jnp.sum(diff * diff, axis=0, keepdims=True)
        part = col[:, :HEAD_DIM]
        for c in range(1, bn // HEAD_DIM):
            part = part + col[:, _hs(c)]
        loss_ref[...] += part * (0.5 / D_MODEL)

    tile = pl.BlockSpec((bm, bn), lambda i, j: (i, j))
    return pl.pallas_call(
        body,
        out_shape=[jax.ShapeDtypeStruct((seq, D_MODEL), F32), jax.ShapeDtypeStruct((1, HEAD_DIM), F32)],
        grid=(seq // bm, D_MODEL // bn),
        in_specs=[pl.BlockSpec((bm, k_dim), lambda i, j: (i, 0)), pl.BlockSpec((k_dim, bn), lambda i, j: (0, j)),
                  tile, tile],
        out_specs=[tile, pl.BlockSpec((1, HEAD_DIM), lambda i, j: (0, 0))],
        name=name, compiler_params=_params("arbitrary", "arbitrary"),
    )(mixed, w_out, resid, target)


def _rowwise(body, n_rows, ins, outs, accs=(), *, name, block=ROW_BLOCK, chunk=None):
    blk = min(block, n_rows)
    assert n_rows % blk == 0
    in_specs = []
    for arr, is_row in ins:
        if is_row:
            assert arr.shape[0] == n_rows, (name, arr.shape, n_rows)
            width = arr.shape[1] if is_row is True else is_row
            in_specs.append(pl.BlockSpec((blk, width), lambda i: (i, 0)))
        else:
            in_specs.append(pl.BlockSpec(arr.shape, lambda i, nd=arr.ndim: (0,) * nd))
    out_shape = [jax.ShapeDtypeStruct((n_rows, w), dt) for w, dt in outs]
    out_specs = [pl.BlockSpec((blk, w), lambda i: (i, 0)) for w, _ in outs]
    out_shape += [jax.ShapeDtypeStruct(s, dt) for s, dt in accs]
    out_specs += [pl.BlockSpec(s, lambda i, nd=len(s): (0,) * nd) for s, _ in accs]
    n_in, n_out, n_acc = len(ins), len(outs), len(accs)

    def kern(*refs):
        in_refs = refs[:n_in]
        out_refs = refs[n_in:n_in + n_out]
        acc_refs = refs[n_in + n_out:]
        if n_acc:
            @pl.when(pl.program_id(0) == 0)
            def _():
                for r in acc_refs:
                    r[...] = jnp.zeros_like(r)
        if not chunk or chunk >= blk:
            body(in_refs, out_refs, acc_refs)
            return
        for r in range(blk // chunk):
            rows = pl.ds(r * chunk, chunk)
            body([ref.at[rows] if is_row else ref for ref, (_, is_row) in zip(in_refs, ins)],
                 [ref.at[rows] for ref in out_refs], acc_refs)

    return pl.pallas_call(
        kern, out_shape=out_shape, grid=(n_rows // blk,), in_specs=in_specs, out_specs=out_specs,
        name=name, compiler_params=_params("arbitrary"),
    )(*[arr for arr, _ in ins])


def _rms(x, g, n=None):
    n = x.shape[-1] if n is None else n
    r = lax.rsqrt(jnp.sum(x * x, axis=-1, keepdims=True) / n + EPS)
    return x * r * g


def _rms_bwd(x, g, dy, n=None):
    n = x.shape[-1] if n is None else n
    r = lax.rsqrt(jnp.sum(x * x, axis=-1, keepdims=True) / n + EPS)
    gdy = dy * g
    dx = r * (gdy - x * ((r * r) * (jnp.sum(gdy * x, axis=-1, keepdims=True) / n)))
    dg = jnp.sum(dy * x * r, axis=0, keepdims=True)
    return dx, dg


def _swap_halves(x):
    lane = lax.broadcasted_iota(jnp.int32, x.shape, 1)
    return jnp.where(lane < ROPE_DIM // 2, pltpu.roll(x, 128 - ROPE_DIM // 2, 1),
                     pltpu.roll(x, ROPE_DIM // 2, 1))


def _rope(n, cos_t, sin_t):
    return n * cos_t + _swap_halves(n) * sin_t


def _rope_bwd(dy, cos_t, sin_t):
    return dy * cos_t - _swap_halves(dy) * sin_t


def _sigmoid(g):
    return 1.0 / (1.0 + jnp.exp(-g))


def _dot_t(a, b):
    return lax.dot_general(a, b, (((1,), (1,)), ((), ())), preferred_element_type=F32)


def _tdot(a, b):
    return lax.dot_general(a, b, (((0,), (0,)), ((), ())), preferred_element_type=F32)


def _dot(a, b):
    return jnp.dot(a, b, preferred_element_type=F32)


def _hs(h, w=HEAD_DIM, base=0):
    return slice(base + h * w, base + (h + 1) * w)


def _mem_head(qm, gq, mk_h, mv_h):
    qb = _rms(qm, gq).astype(BF16)
    s = _dot_t(qb, mk_h) * (HEAD_DIM ** -0.5)
    e = jnp.exp(s - jnp.max(s, axis=-1, keepdims=True))
    p = e / jnp.sum(e, axis=-1, keepdims=True)
    mo = _dot(p.astype(BF16), mv_h)
    return qb, p, mo


def _mix_fwd(att, gates, c0, mk, mv, gq, *, name):
    n_rows = att.shape[0]

    def body(ins, outs, _):
        att_ref, g_ref, mk_ref, mv_ref, gq_ref = ins
        (o_ref,) = outs
        g = g_ref[:, c0:c0 + SB_W]
        o_ref[:, :SB_W] = (att_ref[...] * (g * _sigmoid(g))).astype(BF16)
        for h in range(N_MEM_HEADS):
            qm = g_ref[:, _hs(h, base=c0 + SB_W)]
            gm = g_ref[:, _hs(h, base=c0 + SB_W + MEM_W)]
            _, _, mo = _mem_head(qm, gq_ref[...], mk_ref[:, _hs(h)], mv_ref[:, _hs(h)])
            o_ref[:, _hs(h, base=SB_W)] = (mo * (gm * _sigmoid(gm))).astype(BF16)

    (mixed,) = _rowwise(body, n_rows,
                        [(att, True), (gates, True), (mk, False), (mv, False), (gq, False)],
                        [(D_MODEL, BF16)], name=name)
    return mixed


def _mix_bwd(dmixed, att, gates, c0, mk, mv, gq, *, name):
    n_rows = att.shape[0]
    scale = HEAD_DIM ** -0.5

    def body(ins, outs, accs):
        dm_ref, att_ref, g_ref, mk_ref, mv_ref, gq_ref = ins
        datt_ref, dg_ref = outs
        dmk_ref, dmv_ref, dgq_ref = accs
        g = g_ref[:, c0:c0 + SB_W]
        sg = _sigmoid(g)
        dm = dm_ref[:, :SB_W]
        datt_ref[...] = dm * (g * sg)
        dg_ref[:, :SB_W] = (dm * att_ref[...] * (sg * (1.0 + g * (1.0 - sg)))).astype(BF16)
        for h in range(N_MEM_HEADS):
            qm = g_ref[:, _hs(h, base=c0 + SB_W)]
            gm = g_ref[:, _hs(h, base=c0 + SB_W + MEM_W)]
            mk_h = mk_ref[:, _hs(h)]
            mv_h = mv_ref[:, _hs(h)]
            qb, p, mo = _mem_head(qm, gq_ref[...], mk_h, mv_h)
            sgm = _sigmoid(gm)
            dmh = dm_ref[:, _hs(h, base=SB_W)]
            dmo = dmh * (gm * sgm)
            dg_ref[:, _hs(h, base=SB_W + MEM_W)] = (
                dmh * mo * (sgm * (1.0 + gm * (1.0 - sgm)))).astype(BF16)
            dmo_b = dmo.astype(BF16)
            pb = p.astype(BF16)
            dp = _dot_t(dmo_b, mv_h)
            dmv_ref[:, _hs(h)] += _tdot(pb, dmo_b)
            ds = (p * (dp - jnp.sum(dp * p, axis=-1, keepdims=True)) * scale).astype(BF16)
            dqn = _dot(ds, mk_h)
            dmk_ref[:, _hs(h)] += _tdot(ds, qb)
            dqm, dgq = _rms_bwd(qm, gq_ref[...], dqn)
            dg_ref[:, _hs(h, base=SB_W)] = dqm.astype(BF16)
            dgq_ref[...] += dgq

    return _rowwise(body, n_rows,
                    [(dmixed, True), (att, True), (gates, True), (mk, False), (mv, False), (gq, False)],
                    [(SB_W, F32), (GATE_W, BF16)],
                    [((MEM_LEN, MEM_W), F32), ((MEM_LEN, MEM_W), F32), ((1, HEAD_DIM), F32)],
                    name=name)


def _mem_side_fwd(mem, g_norm, w_kv, g_k, *, tag):
    def norm_body(ins, outs, _):
        outs[0][...] = _rms(ins[0][...], ins[1][...]).astype(BF16)

    (mn,) = _rowwise(norm_body, MEM_LEN, [(mem, True), (g_norm, False)], [(D_MODEL, BF16)],
                     name=f"mem_norm_{tag}")
    mkv = _mm(mn, w_kv, name=f"mem_kv_{tag}")

    def kv_body(ins, outs, _):
        mkv_ref, gk_ref = ins
        mk_ref, mv_ref = outs
        for h in range(N_MEM_HEADS):
            mk_ref[:, _hs(h)] = _rms(mkv_ref[:, _hs(h)], gk_ref[...]).astype(BF16)
        mv_ref[...] = mkv_ref[:, MEM_W:].astype(BF16)

    mk, mv = _rowwise(kv_body, MEM_LEN, [(mkv, True), (g_k, False)], [(MEM_W, BF16), (MEM_W, BF16)],
                      name=f"mem_kv_prep_{tag}")
    return mn, mkv, mk, mv


def _mem_side_bwd(mem, g_norm, w_kv, g_k, mn, mkv, dmk, dmv, *, tag):
    def kv_body(ins, outs, accs):
        mkv_ref, gk_ref, dmk_ref, dmv_ref = ins
        (d_ref,) = outs
        (dgk_ref,) = accs
        for h in range(N_MEM_HEADS):
            dx, dg = _rms_bwd(mkv_ref[:, _hs(h)], gk_ref[...], dmk_ref[:, _hs(h)])
            d_ref[:, _hs(h)] = dx.astype(BF16)
            dgk_ref[...] += dg
        d_ref[:, MEM_W:] = dmv_ref[...].astype(BF16)

    dmkv, dgk = _rowwise(kv_body, MEM_LEN, [(mkv, True), (g_k, False), (dmk, True), (dmv, True)],
                         [(2 * MEM_W, BF16)], [((1, HEAD_DIM), F32)], name=f"mem_kv_prep_bwd_{tag}")
    dmn = _mm(dmkv, w_kv, tb=True, name=f"mem_kv_dx_{tag}")
    dw = _mm(mn, dmkv, ta=True, out_dtype=BF16, name=f"mem_kv_dw_{tag}")

    def norm_body(ins, outs, accs):
        _, dg = _rms_bwd(ins[0][...], ins[1][...], ins[2][...])
        accs[0][...] += dg

    (dgn,) = _rowwise(norm_body, MEM_LEN, [(mem, True), (g_norm, False), (dmn, True)], [],
                      [((1, D_MODEL), F32)], name=f"mem_norm_bwd_{tag}")
    return dw, dgn, dgk


LOG2_E = 1.4426950408889634
SB_Q_SCALE = HEAD_DIM ** -0.5 * LOG2_E


Z2_CAP = 126.0


def _sb_terms(z2):
    zc = jnp.minimum(z2, Z2_CAP)
    w = 1.0 + jnp.exp2(zc)
    return zc, w, jnp.log2(w)


LOOP_UNROLL = 8


def _loop_blocks(base, qb, body, carry, *, reverse):
    def run(start, trips, unroll, c0):
        def trip(t, c):
            for u in range(unroll):
                p = start + t * unroll + u
                c = body(base - 1 - p if reverse else p, c)
            return c
        return lax.fori_loop(0, trips, trip, c0)

    if qb % LOOP_UNROLL == 0:
        return run(0, base // LOOP_UNROLL, LOOP_UNROLL, carry)
    small = qb
    n_big = base // LOOP_UNROLL
    carry = run(0, n_big, LOOP_UNROLL, carry)
    return run(n_big * LOOP_UNROLL, (base - n_big * LOOP_UNROLL) // small, small, carry)


def _chain_modes(s, qb):
    return tuple(None if t < s else ("m" if t == s else "f") for t in range(qb))


def _split_dot(x, tri2):
    hi = x.astype(BF16)
    lo = (x - hi.astype(F32)).astype(BF16)
    return _dot(jnp.concatenate([hi, lo], axis=1), tri2)


def _sb_fwd(qkv, *, name, hp=1):
    seq = qkv.shape[0]
    blk = min(ATT_BLOCK, seq)
    nkb = seq // blk
    qb = _pick(nkb, (4, 2, 1))
    rows = qb * blk
    chains = [(t, s) for t in range(hp) for s in range(qb)]

    def body(q_ref, k_ref, v_ref, o_ref):
        base = pl.program_id(1) * qb
        qs = {(t, s): q_ref[s * blk:(s + 1) * blk, _hs(t)] for t, s in chains}
        row = lax.broadcasted_iota(jnp.int32, (blk, blk), 0)
        col = lax.broadcasted_iota(jnp.int32, (blk, blk), 1)
        after = (row > col).astype(BF16)
        after2 = jnp.concatenate([after, after], axis=0)
        causal = col < row

        def step(j, carry, modes):
            off = pl.multiple_of(j * blk, blk)
            act = [c for c in chains if modes[c[1]]]
            zs, ls = {}, {}
            for c in act:
                zs[c], _, l = _sb_terms(_dot_t(qs[c], k_ref[pl.ds(off, blk), _hs(c[0])]))
                ls[c] = jnp.where(causal, l, 0.0) if modes[c[1]] == "m" else l
            cs = {c: _split_dot(ls[c], after2) for c in act}
            carry = dict(carry)
            for c in act:
                run, acc = carry[c]
                a = jnp.exp2(zs[c] - ls[c] - cs[c] - run)
                if modes[c[1]] == "m":
                    a = jnp.where(causal, a, 0.0)
                acc = acc + _dot(a.astype(BF16), v_ref[pl.ds(off, blk), _hs(c[0])])
                carry[c] = (run + (cs[c][:, :1] + ls[c][:, :1]), acc)
            return carry

        init = (jnp.zeros((blk, 1), F32), jnp.zeros((blk, HEAD_DIM), F32))
        carry = {c: init for c in chains}
        for s in reversed(range(qb)):
            carry = step(base + s, carry, _chain_modes(s, qb))
        carry = _loop_blocks(base, qb, lambda j, c: step(j, c, ("f",) * qb), carry, reverse=True)
        for t, s in chains:
            o_ref[s * blk:(s + 1) * blk, _hs(t)] = carry[(t, s)][1]

    nh = N_SB_HEADS // hp
    return pl.pallas_call(
        body, out_shape=jax.ShapeDtypeStruct((seq, SB_W), F32), grid=(nh, nkb // qb),
        in_specs=[pl.BlockSpec((rows, hp * HEAD_DIM), lambda h, i: (i, h)),
                  pl.BlockSpec((seq, hp * HEAD_DIM), lambda h, i: (0, nh + h)),
                  pl.BlockSpec((seq, hp * HEAD_DIM), lambda h, i: (0, 2 * nh + h))],
        out_specs=pl.BlockSpec((rows, hp * HEAD_DIM), lambda h, i: (i, h)),
        name=name, compiler_params=_params("parallel", "arbitrary"),
    )(qkv, qkv, qkv)


SB_BWD_GROUP = 4


def _sb_bwd(qkv, out, dout, *, name):
    seq = qkv.shape[0]
    blk = min(ATT_BLOCK, seq)
    nkb = seq // blk
    qb = _pick(nkb, (4, 2, 1))
    rows = qb * blk
    scale = HEAD_DIM ** -0.5

    def body(q_ref, k_ref, v_ref, do_ref, o_ref, dq_ref, dk_out, dv_out, dk_ref, dv_ref):
        g = pl.program_id(1)
        base = g * qb

        @pl.when(g == 0)
        def _():
            dk_ref[...] = jnp.zeros_like(dk_ref)
            dv_ref[...] = jnp.zeros_like(dv_ref)

        qs = [q_ref[t * blk:(t + 1) * blk, :] for t in range(qb)]
        dos = [do_ref[t * blk:(t + 1) * blk, :].astype(BF16) for t in range(qb)]
        totals = [jnp.sum(dos[t].astype(F32) * o_ref[t * blk:(t + 1) * blk, :], axis=-1, keepdims=True)
                  for t in range(qb)]
        row = lax.broadcasted_iota(jnp.int32, (blk, blk), 0)
        col = lax.broadcasted_iota(jnp.int32, (blk, blk), 1)
        after = (row > col).astype(BF16)
        after2 = jnp.concatenate([after, after], axis=0)
        from_s = (row >= col).astype(BF16)
        from_s2 = jnp.concatenate([from_s, from_s], axis=0)
        causal = col < row

        def step(j, carry, modes):
            runs, rights, dqs = list(carry[0]), list(carry[1]), list(carry[2])
            off = pl.multiple_of(j * blk, blk)
            kb = k_ref[pl.ds(off, blk), :]
            vb = v_ref[pl.ds(off, blk), :]
            dv_inc = dk_inc = None
            for first in range(0, qb, SB_BWD_GROUP):
                act = [t for t in range(first, min(first + SB_BWD_GROUP, qb)) if modes[t]]
                zs, ls, sns = {}, {}, {}
                for t in act:
                    zs[t], w, l = _sb_terms(_dot_t(qs[t], kb))
                    sns[t] = pl.reciprocal(w, approx=True)
                    ls[t] = jnp.where(causal, l, 0.0) if modes[t] == "m" else l
                cs = {t: _split_dot(ls[t], after2) for t in act}
                das = {t: _dot_t(dos[t], vb) for t in act}
                abs_, des = {}, {}
                for t in act:
                    a = jnp.exp2(zs[t] - ls[t] - cs[t] - runs[t])
                    if modes[t] == "m":
                        a = jnp.where(causal, a, 0.0)
                    abs_[t] = a.astype(BF16)
                    des[t] = abs_[t].astype(F32) * das[t]
                sufs = {t: _split_dot(des[t], from_s2) for t in act}
                for t in act:
                    left = totals[t] - (sufs[t] + rights[t])
                    dz = (des[t] + left) * sns[t] - left
                    if modes[t] == "m":
                        dz = jnp.where(causal, dz, 0.0)
                    dzb = dz.astype(BF16)
                    dqs[t] = dqs[t] + _dot(dzb, kb)
                    inc_v = _tdot(abs_[t], dos[t])
                    inc_k = _tdot(dzb, qs[t])
                    dv_inc = inc_v if dv_inc is None else dv_inc + inc_v
                    dk_inc = inc_k if dk_inc is None else dk_inc + inc_k
                    runs[t] = runs[t] + (cs[t][:, :1] + ls[t][:, :1])
                    rights[t] = rights[t] + sufs[t][:, :1]
            dv_ref[pl.ds(off, blk), :] += dv_inc
            dk_ref[pl.ds(off, blk), :] += dk_inc
            return tuple(runs), tuple(rights), tuple(dqs)

        zero = (jnp.zeros((blk, 1), F32),) * qb
        carry = (zero, zero, (jnp.zeros((blk, HEAD_DIM), F32),) * qb)
        for s in reversed(range(qb)):
            carry = step(base + s, carry, _chain_modes(s, qb))
        carry = _loop_blocks(base, qb, lambda j, c: step(j, c, ("f",) * qb), carry, reverse=True)
        for t in range(qb):
            dq_ref[t * blk:(t + 1) * blk, :] = (carry[2][t] * scale).astype(BF16)

        @pl.when(g == pl.num_programs(1) - 1)
        def _():
            dk_out[...] = (dk_ref[...] * (1.0 / LOG2_E)).astype(BF16)
            dv_out[...] = dv_ref[...].astype(BF16)

    out_sd = jax.ShapeDtypeStruct((seq, SB_W), BF16)
    return pl.pallas_call(
        body, out_shape=[out_sd, out_sd, out_sd], grid=(N_SB_HEADS, nkb // qb),
        in_specs=[pl.BlockSpec((rows, HEAD_DIM), lambda h, i: (i, h)),
                  pl.BlockSpec((seq, HEAD_DIM), lambda h, i: (0, N_SB_HEADS + h)),
                  pl.BlockSpec((seq, HEAD_DIM), lambda h, i: (0, 2 * N_SB_HEADS + h)),
                  pl.BlockSpec((rows, HEAD_DIM), lambda h, i: (i, h)),
                  pl.BlockSpec((rows, HEAD_DIM), lambda h, i: (i, h))],
        out_specs=[pl.BlockSpec((rows, HEAD_DIM), lambda h, i: (i, h)),
                   pl.BlockSpec((seq, HEAD_DIM), lambda h, i: (0, h)),
                   pl.BlockSpec((seq, HEAD_DIM), lambda h, i: (0, h))],
        scratch_shapes=[pltpu.VMEM((seq, HEAD_DIM), F32), pltpu.VMEM((seq, HEAD_DIM), F32)],
        name=name, compiler_params=_params("parallel", "arbitrary"),
    )(qkv, qkv, qkv, dout, out)


MLA_SCALE = (HEAD_DIM + ROPE_DIM) ** -0.5
MLA_Q_SCALE = MLA_SCALE * LOG2_E


def _mla_fwd(q_cat, k_cat, v, *, name, hp=1):
    seq = q_cat.shape[0]
    blk = min(ATT_BLOCK, seq)
    nkb = seq // blk
    qb = _pick(nkb, (4, 2, 1))
    rows = qb * blk
    chains = [(t, s) for t in range(hp) for s in range(qb)]

    def body(q_ref, k_ref, v_ref, o_ref, lse_ref):
        base = pl.program_id(1) * qb
        qs = {(t, s): q_ref[s * blk:(s + 1) * blk, t * CAT_W:(t + 1) * CAT_W] for t, s in chains}
        row = lax.broadcasted_iota(jnp.int32, (blk, blk), 0)
        col = lax.broadcasted_iota(jnp.int32, (blk, blk), 1)
        causal = col <= row
        ones = jnp.ones((blk, HEAD_DIM), BF16)

        def step(j, carry, modes):
            off = pl.multiple_of(j * blk, blk)
            act = [c for c in chains if modes[c[1]]]
            ss = {c: _dot_t(qs[c], k_ref[pl.ds(off, blk), c[0] * CAT_W:(c[0] + 1) * CAT_W]) for c in act}
            carry = dict(carry)
            for c in act:
                m, l, acc = carry[c]
                s = ss[c]
                if modes[c[1]] == "m":
                    s = jnp.where(causal, s, -jnp.inf)
                m_new = jnp.maximum(m, jnp.max(s, axis=-1, keepdims=True))
                pb = jnp.exp2(s - m_new).astype(BF16)
                alpha = jnp.exp2(m - m_new)
                both = _dot(pb, jnp.concatenate([v_ref[pl.ds(off, blk), _hs(c[0])], ones], axis=1))
                l = alpha * l + both[:, HEAD_DIM:HEAD_DIM + 1]
                acc = alpha * acc + both[:, :HEAD_DIM]
                carry[c] = (m_new, l, acc)
            return carry

        init = (jnp.full((blk, 1), -jnp.inf, F32), jnp.zeros((blk, 1), F32),
                jnp.zeros((blk, HEAD_DIM), F32))
        carry = {c: init for c in chains}
        carry = _loop_blocks(base, qb, lambda j, c: step(j, c, ("f",) * qb), carry, reverse=False)
        for s in range(qb):
            carry = step(base + s, carry, _chain_modes(s, qb))
        for t, s in chains:
            m, l, acc = carry[(t, s)]
            o_ref[s * blk:(s + 1) * blk, _hs(t)] = acc / l
            lse_ref[s * blk:(s + 1) * blk, _hs(t)] = jnp.broadcast_to(
                (m + jnp.log2(l)) * (1.0 / LOG2_E), (blk, HEAD_DIM))

    out = jax.ShapeDtypeStruct((seq, MLA_W), F32)
    return pl.pallas_call(
        body, out_shape=[out, out], grid=(N_MLA_HEADS // hp, nkb // qb),
        in_specs=[pl.BlockSpec((rows, hp * CAT_W), lambda h, i: (i, h)),
                  pl.BlockSpec((seq, hp * CAT_W), lambda h, i: (0, h)),
                  pl.BlockSpec((seq, hp * HEAD_DIM), lambda h, i: (0, h))],
        out_specs=[pl.BlockSpec((rows, hp * HEAD_DIM), lambda h, i: (i, h)),
                   pl.BlockSpec((rows, hp * HEAD_DIM), lambda h, i: (i, h))],
        name=name, compiler_params=_params("parallel", "arbitrary"),
    )(q_cat, k_cat, v)


def _mla_bwd(q_cat, k_cat, v, out, lse, dout, *, name):
    seq = q_cat.shape[0]
    blk = min(ATT_BLOCK, seq)
    nkb = seq // blk
    qb = _pick(nkb, (4, 2, 1))
    rows = qb * blk

    def body(q_ref, k_ref, v_ref, o_ref, lse_ref, do_ref, dq_ref, dk_ref, dv_ref):
        g = pl.program_id(1)
        base = g * qb

        @pl.when(g == 0)
        def _():
            dk_ref[...] = jnp.zeros_like(dk_ref)
            dv_ref[...] = jnp.zeros_like(dv_ref)

        qs, dobs, deltas, lses = [], [], [], []
        for t in range(qb):
            rs = slice(t * blk, (t + 1) * blk)
            do = do_ref[rs, :]
            qs.append(q_ref[rs, :])
            dobs.append(do.astype(BF16))
            deltas.append(jnp.sum(do * o_ref[rs, :], axis=-1, keepdims=True))
            lses.append(lse_ref[rs, :1] * LOG2_E)
        row = lax.broadcasted_iota(jnp.int32, (blk, blk), 0)
        col = lax.broadcasted_iota(jnp.int32, (blk, blk), 1)
        causal = col <= row

        def step(j, dqs, modes):
            off = pl.multiple_of(j * blk, blk)
            kb = k_ref[pl.ds(off, blk), :]
            vb = v_ref[pl.ds(off, blk), :]
            act = [t for t in range(qb) if modes[t]]
            ss = {t: _dot_t(qs[t], kb) for t in act}
            dps = {t: _dot_t(dobs[t], vb) for t in act}
            dqs = list(dqs)
            dv_inc = dk_inc = None
            for t in act:
                p = jnp.exp2(ss[t] - lses[t])
                if modes[t] == "m":
                    p = jnp.where(causal, p, 0.0)
                ds = (p * (dps[t] - deltas[t])).astype(BF16)
                inc_v = _tdot(p.astype(BF16), dobs[t])
                inc_k = _tdot(ds, qs[t])
                dv_inc = inc_v if dv_inc is None else dv_inc + inc_v
                dk_inc = inc_k if dk_inc is None else dk_inc + inc_k
                dqs[t] = dqs[t] + _dot(ds, kb)
            dv_ref[pl.ds(off, blk), :] += dv_inc
            dk_ref[pl.ds(off, blk), :] += dk_inc
            return tuple(dqs)

        dqs = (jnp.zeros((blk, CAT_W), F32),) * qb
        dqs = _loop_blocks(base, qb, lambda j, c: step(j, c, ("f",) * qb), dqs, reverse=False)
        for s in range(qb):
            modes = tuple(None if t < s else ("m" if t == s else "f") for t in range(qb))
            dqs = step(base + s, dqs, modes)
        for t in range(qb):
            dq_ref[t * blk:(t + 1) * blk, :] = dqs[t] * MLA_SCALE

        @pl.when(g == pl.num_programs(1) - 1)
        def _():
            dk_ref[...] = dk_ref[...] * (1.0 / LOG2_E)

    return pl.pallas_call(
        body,
        out_shape=[jax.ShapeDtypeStruct((seq, N_MLA_HEADS * CAT_W), F32),
                   jax.ShapeDtypeStruct((seq, N_MLA_HEADS * CAT_W), F32),
                   jax.ShapeDtypeStruct((seq, MLA_W), F32)],
        grid=(N_MLA_HEADS, nkb // qb),
        in_specs=[pl.BlockSpec((rows, CAT_W), lambda h, i: (i, h)),
                  pl.BlockSpec((seq, CAT_W), lambda h, i: (0, h)),
                  pl.BlockSpec((seq, HEAD_DIM), lambda h, i: (0, h)),
                  pl.BlockSpec((rows, HEAD_DIM), lambda h, i: (i, h)),
                  pl.BlockSpec((rows, HEAD_DIM), lambda h, i: (i, h)),
                  pl.BlockSpec((rows, HEAD_DIM), lambda h, i: (i, h))],
        out_specs=[pl.BlockSpec((rows, CAT_W), lambda h, i: (i, h)),
                   pl.BlockSpec((seq, CAT_W), lambda h, i: (0, h)),
                   pl.BlockSpec((seq, HEAD_DIM), lambda h, i: (0, h))],
        name=name, compiler_params=_params("parallel", "arbitrary"),
    )(q_cat, k_cat, v, out, lse, dout)


def _local_step(x, mem, positions, target, w, g):
    seq = x.shape[0]
    inv_freq = jnp.power(ROPE_THETA, -jnp.arange(0, ROPE_DIM, 2, dtype=F32) / ROPE_DIM)
    ang = positions.astype(F32)[:, None] * inv_freq
    cos, sin = jnp.cos(ang), jnp.sin(ang)
    lane_pad = jnp.zeros((seq, HEAD_DIM - ROPE_DIM), F32)
    cos_t = jnp.concatenate([cos, cos, lane_pad], axis=1)
    sin_t = jnp.concatenate([-sin, sin, lane_pad], axis=1)
    gain_pad = jnp.zeros((1, HEAD_DIM - ROPE_DIM), F32)
    g_k_rope = jnp.concatenate([g["g_k_rope"], gain_pad], axis=1)
    g_q_rope = jnp.concatenate([g["b_g_q_rope"], gain_pad], axis=1)

    def norm_to_bf16(src, gain, name):
        def body(ins, outs, _):
            outs[0][...] = _rms(ins[0][...], ins[1][...]).astype(BF16)
        return _rowwise(body, seq, [(src, True), (gain, False)], [(src.shape[1], BF16)], name=name,
                        chunk=NORM_CHUNK)[0]

    h_a = norm_to_bf16(x, g["a_norm"], "a_norm_fwd")
    qkv = _mm(h_a, w["a_in_qkv"], out_dtype=BF16, scale_cols=(SB_W, SB_Q_SCALE), name="a_in_qkv")
    gr = _mm(h_a, w["a_in_gate"], name="a_in_gate")
    sb = _sb_fwd(qkv, name="sb_fwd")
    mem0 = _mem_side_fwd(mem, g["mem_norm"][0:1], w["mem_kv"][0], g["g_mem_k"][0:1], tag="a")
    mixed_a = _mix_fwd(sb, gr, 0, mem0[2], mem0[3], g["g_mem_q"][0:1], name="a_mix_fwd")
    x1 = _mm(mixed_a, w["a_out"], res=x, name="a_out")

    def norms2_body(ins, outs, _):
        xv = ins[0][...]
        outs[0][...] = _rms(xv, ins[1][...]).astype(BF16)
        outs[1][...] = _rms(xv, ins[2][...]).astype(BF16)

    h_kv, h_b = _rowwise(norms2_body, seq, [(x1, True), (g["kv_norm"], False), (g["b_norm"], False)],
                         [(D_MODEL, BF16), (D_MODEL, BF16)], name="kv_b_norm_fwd", chunk=NORM_CHUNK)
    ckr = _mm(h_kv, w["dkv"], name="dkv")

    def ckr_body(ins, outs, _):
        ckr_ref, gc_ref, gr_ref, c_ref, s_ref = ins
        outs[0][...] = _rms(ckr_ref[:, :KV_LORA], gc_ref[...]).astype(BF16)
        kr = _rms(ckr_ref[:, KV_LORA:], gr_ref[...], n=ROPE_DIM)
        outs[1][...] = _rope(kr, c_ref[...], s_ref[...]).astype(BF16)

    c_n, k_r = _rowwise(ckr_body, seq,
                        [(ckr, True), (g["g_ckv"], False), (g_k_rope, False), (cos_t, True), (sin_t, True)],
                        [(KV_LORA, BF16), (HEAD_DIM, BF16)], name="ckv_prep_fwd")
    kv = _mm(c_n, w["ukv"], name="ukv")

    def kcat_body(ins, outs, _):
        kv_ref, kr_ref, gk_ref = ins
        kc_ref, v_ref = outs
        for h in range(N_MLA_HEADS):
            kc_ref[:, h * CAT_W:h * CAT_W + HEAD_DIM] = _rms(
                kv_ref[:, h * CAT_W:h * CAT_W + HEAD_DIM], gk_ref[...]).astype(BF16)
            kc_ref[:, h * CAT_W + HEAD_DIM:(h + 1) * CAT_W] = kr_ref[...]
            v_ref[:, _hs(h)] = kv_ref[:, h * CAT_W + HEAD_DIM:(h + 1) * CAT_W].astype(BF16)

    k_cat, v_mla = _rowwise(kcat_body, seq, [(kv, True), (k_r, True), (g["g_k_nope"], False)],
                            [(N_MLA_HEADS * CAT_W, BF16), (MLA_W, BF16)], name="k_prep_fwd")

    p2 = _mm(h_b, w["b_in"], name="b_in")

    def qlat_body(ins, outs, _):
        outs[0][...] = _rms(ins[0][:, :Q_LORA], ins[1][...]).astype(BF16)

    (q_l,) = _rowwise(qlat_body, seq, [(p2, Q_LORA), (g["b_g_q_lat"], False)], [(Q_LORA, BF16)],
                      name="q_lat_norm_fwd")
    q_up = _mm(q_l, w["uq"], name="uq")

    def qcat_body(ins, outs, _):
        q_ref, gn_ref, gr_ref, c_ref, s_ref = ins
        (o_ref,) = outs
        for h in range(N_MLA_HEADS):
            o_ref[:, h * CAT_W:h * CAT_W + HEAD_DIM] = (MLA_Q_SCALE * _rms(
                q_ref[:, h * CAT_W:h * CAT_W + HEAD_DIM], gn_ref[...])).astype(BF16)
            qr = _rms(q_ref[:, h * CAT_W + HEAD_DIM:(h + 1) * CAT_W], gr_ref[...], n=ROPE_DIM)
            o_ref[:, h * CAT_W + HEAD_DIM:(h + 1) * CAT_W] = (
                MLA_Q_SCALE * _rope(qr, c_ref[...], s_ref[...])).astype(BF16)

    (q_cat,) = _rowwise(qcat_body, seq,
                        [(q_up, True), (g["b_g_q_nope"], False), (g_q_rope, False), (cos_t, True), (sin_t, True)],
                        [(N_MLA_HEADS * CAT_W, BF16)], name="q_prep_fwd")
    att, lse = _mla_fwd(q_cat, k_cat, v_mla, name="mla_fwd")
    mem1 = _mem_side_fwd(mem, g["mem_norm"][1:2], w["mem_kv"][1], g["g_mem_k"][1:2], tag="b")
    mixed_b = _mix_fwd(att, p2, Q_LORA, mem1[2], mem1[3], g["g_mem_q"][1:2], name="b_mix_fwd")
    dy, loss_part = _out_proj_loss(mixed_b, w["b_out"], x1, target, name="b_out_loss")

    gw, gg = {}, {}
    dmixed_b = _mm(dy, w["b_out"], tb=True, name="b_out_dx")
    gw["b_out"] = _mm(mixed_b, dy, ta=True, out_dtype=BF16, name="b_out_dw")
    datt, dgate_b, dmk1, dmv1, gq1 = _mix_bwd(dmixed_b, att, p2, Q_LORA, mem1[2], mem1[3],
                                              g["g_mem_q"][1:2], name="b_mix_bwd")
    dq_cat, dk_cat, dv_mla = _mla_bwd(q_cat, k_cat, v_mla, att, lse, datt, name="mla_bwd")

    def qcat_bwd_body(ins, outs, accs):
        q_ref, dq_ref, gn_ref, gr_ref, c_ref, s_ref = ins
        (o_ref,) = outs
        dgn_ref, dgr_ref = accs
        for h in range(N_MLA_HEADS):
            dx, dg = _rms_bwd(q_ref[:, h * CAT_W:h * CAT_W + HEAD_DIM], gn_ref[...],
                              dq_ref[:, h * CAT_W:h * CAT_W + HEAD_DIM])
            o_ref[:, h * CAT_W:h * CAT_W + HEAD_DIM] = dx.astype(BF16)
            dgn_ref[...] += dg
            dn = _rope_bwd(dq_ref[:, h * CAT_W + HEAD_DIM:(h + 1) * CAT_W], c_ref[...], s_ref[...])
            dx, dg = _rms_bwd(q_ref[:, h * CAT_W + HEAD_DIM:(h + 1) * CAT_W], gr_ref[...], dn, n=ROPE_DIM)
            o_ref[:, h * CAT_W + HEAD_DIM:(h + 1) * CAT_W] = dx.astype(BF16)
            dgr_ref[...] += dg

    dq_up, gg["b_g_q_nope"], dgqr = _rowwise(
        qcat_bwd_body, seq,
        [(q_up, True), (dq_cat, True), (g["b_g_q_nope"], False), (g_q_rope, False), (cos_t, True), (sin_t, True)],
        [(N_MLA_HEADS * CAT_W, BF16)], [((1, HEAD_DIM), F32), ((1, HEAD_DIM), F32)], name="q_prep_bwd")
    gg["b_g_q_rope"] = dgqr
    dq_l = _mm(dq_up, w["uq"], tb=True, name="uq_dx")
    gw["uq"] = _mm(q_l, dq_up, ta=True, out_dtype=BF16, n_split=N_CHIPS, name="uq_dw")

    def qlat_bwd_body(ins, outs, accs):
        p2_ref, dql_ref, dgate_ref, gl_ref = ins
        dx, dg = _rms_bwd(p2_ref[:, :Q_LORA], gl_ref[...], dql_ref[...])
        outs[0][:, :Q_LORA] = dx.astype(BF16)
        outs[0][:, Q_LORA:] = dgate_ref[...]
        accs[0][...] += dg

    dp2, gg["b_g_q_lat"] = _rowwise(
        qlat_bwd_body, seq, [(p2, Q_LORA), (dq_l, True), (dgate_b, True), (g["b_g_q_lat"], False)],
        [(Q_LORA + GATE_W, BF16)], [((1, Q_LORA), F32)], name="q_lat_norm_bwd")
    dh_b = _mm(dp2, w["b_in"], tb=True, name="b_in_dx")
    gw["b_in"] = _mm(h_b, dp2, ta=True, out_dtype=BF16, n_split=N_CHIPS, name="b_in_dw")

    def kcat_bwd_body(ins, outs, accs):
        kv_ref, dkc_ref, dv_ref, gk_ref = ins
        dkv_ref, dkr_ref = outs
        (dgk_ref,) = accs
        dkr = jnp.zeros(dkr_ref.shape, F32)
        for h in range(N_MLA_HEADS):
            dx, dg = _rms_bwd(kv_ref[:, h * CAT_W:h * CAT_W + HEAD_DIM], gk_ref[...],
                              dkc_ref[:, h * CAT_W:h * CAT_W + HEAD_DIM])
            dkv_ref[:, h * CAT_W:h * CAT_W + HEAD_DIM] = dx.astype(BF16)
            dgk_ref[...] += dg
            dkv_ref[:, h * CAT_W + HEAD_DIM:(h + 1) * CAT_W] = dv_ref[:, _hs(h)].astype(BF16)
            dkr = dkr + dkc_ref[:, h * CAT_W + HEAD_DIM:(h + 1) * CAT_W]
        dkr_ref[...] = dkr

    dkv, dk_r, gg["g_k_nope"] = _rowwise(
        kcat_bwd_body, seq, [(kv, True), (dk_cat, True), (dv_mla, True), (g["g_k_nope"], False)],
        [(N_MLA_HEADS * CAT_W, BF16), (HEAD_DIM, F32)], [((1, HEAD_DIM), F32)], name="k_prep_bwd")
    dc_n = _mm(dkv, w["ukv"], tb=True, name="ukv_dx")
    gw["ukv"] = _mm(c_n, dkv, ta=True, out_dtype=BF16, n_split=N_CHIPS, name="ukv_dw")

    def ckr_bwd_body(ins, outs, accs):
        ckr_ref, dcn_ref, dkr_ref, gc_ref, gr_ref, c_ref, s_ref = ins
        dx, dg = _rms_bwd(ckr_ref[:, :KV_LORA], gc_ref[...], dcn_ref[...])
        outs[0][:, :KV_LORA] = dx.astype(BF16)
        accs[0][...] += dg
        dn = _rope_bwd(dkr_ref[...], c_ref[...], s_ref[...])
        dx, dg = _rms_bwd(ckr_ref[:, KV_LORA:], gr_ref[...], dn, n=ROPE_DIM)
        outs[0][:, KV_LORA:] = dx.astype(BF16)
        accs[1][...] += dg

    dckr, gg["g_ckv"], gg["g_k_rope"] = _rowwise(
        ckr_bwd_body, seq,
        [(ckr, True), (dc_n, True), (dk_r, True), (g["g_ckv"], False), (g_k_rope, False),
         (cos_t, True), (sin_t, True)],
        [(KV_LORA + HEAD_DIM, BF16)], [((1, KV_LORA), F32), ((1, HEAD_DIM), F32)], name="ckv_prep_bwd")
    dh_kv = _mm(dckr, w["dkv"], tb=True, name="dkv_dx")
    gw["dkv"] = _mm(h_kv, dckr, ta=True, out_dtype=BF16, name="dkv_dw")

    def norms2_bwd_body(ins, outs, accs):
        x_ref, dy_ref, dhk_ref, dhb_ref, gk_ref, gb_ref = ins
        xv = x_ref[...]
        dxk, dgk = _rms_bwd(xv, gk_ref[...], dhk_ref[...])
        dxb, dgb = _rms_bwd(xv, gb_ref[...], dhb_ref[...])
        outs[0][...] = dy_ref[...] + dxk + dxb
        accs[0][...] += dgk
        accs[1][...] += dgb

    dx1, gg["kv_norm"], gg["b_norm"] = _rowwise(
        norms2_bwd_body, seq,
        [(x1, True), (dy, True), (dh_kv, True), (dh_b, True), (g["kv_norm"], False), (g["b_norm"], False)],
        [(D_MODEL, F32)], [((1, D_MODEL), F32), ((1, D_MODEL), F32)], name="kv_b_norm_bwd", chunk=NORM_CHUNK)

    dmixed_a = _mm(dx1, w["a_out"], tb=True, name="a_out_dx")
    gw["a_out"] = _mm(mixed_a, dx1, ta=True, out_dtype=BF16, name="a_out_dw")
    dsb, dgate_a, dmk0, dmv0, gq0 = _mix_bwd(dmixed_a, sb, gr, 0, mem0[2], mem0[3],
                                             g["g_mem_q"][0:1], name="a_mix_bwd")
    dq, dk, dv = _sb_bwd(qkv, sb, dsb, name="sb_bwd")
    dp_a = jnp.concatenate([dq, dk, dv, dgate_a], axis=1)
    dh_a = _mm(dp_a, w["a_in"], tb=True, name="a_in_dx")
    gw["a_in"] = _mm(h_a, dp_a, ta=True, out_dtype=BF16, n_split=N_CHIPS, name="a_in_dw")

    def norm_a_bwd_body(ins, outs, accs):
        dx, dg = _rms_bwd(ins[0][...], ins[3][...], ins[2][...])
        outs[0][...] = ins[1][...] + dx
        accs[0][...] += dg

    grad_x, gg["a_norm"] = _rowwise(
        norm_a_bwd_body, seq, [(x, True), (dx1, True), (dh_a, True), (g["a_norm"], False)],
        [(D_MODEL, F32)], [((1, D_MODEL), F32)], name="a_norm_bwd", chunk=NORM_CHUNK)

    dw0, dgn0, dgk0 = _mem_side_bwd(mem, g["mem_norm"][0:1], w["mem_kv"][0], g["g_mem_k"][0:1],
                                    mem0[0], mem0[1], dmk0, dmv0, tag="a")
    dw1, dgn1, dgk1 = _mem_side_bwd(mem, g["mem_norm"][1:2], w["mem_kv"][1], g["g_mem_k"][1:2],
                                    mem1[0], mem1[1], dmk1, dmv1, tag="b")
    gw["mem_kv"] = (dw0, dw1)
    gg["mem_norm"] = jnp.concatenate([dgn0, dgn1], axis=0)
    gg["g_mem_q"] = jnp.concatenate([gq0, gq1], axis=0)
    gg["g_mem_k"] = jnp.concatenate([dgk0, dgk1], axis=0)
    return loss_part, grad_x, gw, gg


HBM_SPEC = pl.BlockSpec(memory_space=pl.ANY)


def _other_chips():
    x, y = lax.axis_index("x"), lax.axis_index("y")
    return [(1 - x, y), (x, 1 - y), (1 - x, 1 - y)]


def _allgather_chips(shards):
    n = len(shards)
    split = [s.shape[0] % 32 == 0 for s in shards]

    def body(*refs):
        ins, outs = refs[:n], refs[n:2 * n]
        send, recv, fsend, frecv = refs[2 * n:]
        x, y, c = lax.axis_index("x"), lax.axis_index("y"), lax.axis_index("c")
        me = 2 * x + y
        chips = _other_chips()

        def part(ref, wi):
            if not split[wi]:
                return ref
            half = shards[wi].shape[0] // 2
            return ref.at[pl.ds(pl.multiple_of(c * half, 16), half)]

        def ici(wi, k, src_chip, to):
            return pltpu.make_async_remote_copy(
                src_ref=part(ins[wi], wi), dst_ref=part(outs[wi].at[src_chip], wi),
                send_sem=send.at[wi, k], recv_sem=recv.at[wi, k], device_id=to, device_id_type=MESH)

        def d2d(wi, k, src_chip):
            rows = part(outs[wi].at[src_chip], wi)
            return pltpu.make_async_remote_copy(
                src_ref=rows, dst_ref=rows, send_sem=fsend.at[wi, k], recv_sem=frecv.at[wi, k],
                device_id=(x, y, 1 - c), device_id_type=MESH)

        for wi in range(n):
            for k, (tx, ty) in enumerate(chips):
                ici(wi, k, me, (tx, ty, c)).start()
        for wi in range(n):
            for k, (tx, ty) in enumerate(chips):
                landed = ici(wi, k, 2 * tx + ty, (tx, ty, c))
                landed.wait_recv()
                if split[wi]:
                    d2d(wi, k, 2 * tx + ty).start()
        for wi in range(n):
            for k, (tx, ty) in enumerate(chips):
                ici(wi, k, me, (tx, ty, c)).wait_send()
                if split[wi]:
                    fwd = d2d(wi, k, 2 * tx + ty)
                    fwd.wait_send()
                    fwd.wait_recv()

    return pl.pallas_call(
        body, out_shape=[jax.ShapeDtypeStruct((N_CHIPS,) + s.shape, s.dtype) for s in shards],
        in_specs=[HBM_SPEC] * n, out_specs=[HBM_SPEC] * n,
        scratch_shapes=[pltpu.SemaphoreType.DMA((n, 3)), pltpu.SemaphoreType.DMA((n, 3)),
                        pltpu.SemaphoreType.DMA((n, 3)), pltpu.SemaphoreType.DMA((n, 3))],
        name="allgather_weights",
    )(*shards)


def _scatter_to_chips(grads):
    n = len(grads)

    def body(*refs):
        ins, outs = refs[:n], refs[n:2 * n]
        send, recv = refs[2 * n:]
        c = lax.axis_index("c")
        copies = []
        for wi in range(n):
            for k, (tx, ty) in enumerate(_other_chips()):
                cp = pltpu.make_async_remote_copy(
                    src_ref=ins[wi].at[2 * tx + ty], dst_ref=outs[wi].at[k], send_sem=send.at[wi, k],
                    recv_sem=recv.at[wi, k], device_id=(tx, ty, c), device_id_type=MESH)
                cp.start()
                copies.append(cp)
        for cp in copies:
            cp.wait()

    return pl.pallas_call(
        body, out_shape=[jax.ShapeDtypeStruct((3,) + s.shape[1:], s.dtype) for s in grads],
        in_specs=[HBM_SPEC] * n, out_specs=[HBM_SPEC] * n,
        scratch_shapes=[pltpu.SemaphoreType.DMA((n, 3)), pltpu.SemaphoreType.DMA((n, 3))],
        name="scatter_grads",
    )(*grads)


def _halve_with_sibling(grads):
    n = len(grads)
    n_slots = grads[0].shape[0]

    def body(*refs):
        ins, got = refs[:n], refs[n:2 * n]
        send, recv = refs[2 * n:]
        c = lax.axis_index("c")
        sib = (lax.axis_index("x"), lax.axis_index("y"), 1 - c)
        copies = []
        for wi in range(n):
            half = grads[wi].shape[1] // 2
            for s in range(n_slots):
                theirs = ins[wi].at[s, pl.ds(pl.multiple_of((1 - c) * half, 16), half)]
                give = pltpu.make_async_remote_copy(
                    src_ref=theirs, dst_ref=got[wi].at[s], send_sem=send.at[wi, s], recv_sem=recv.at[wi, s],
                    device_id=sib, device_id_type=MESH)
                give.start()
                copies.append(give)
        for cp in copies:
            cp.wait()

    halves = [jax.ShapeDtypeStruct((s.shape[0], s.shape[1] // 2) + s.shape[2:], s.dtype) for s in grads]
    return pl.pallas_call(
        body, out_shape=halves, in_specs=[HBM_SPEC] * n, out_specs=[HBM_SPEC] * n,
        scratch_shapes=[pltpu.SemaphoreType.DMA((n, n_slots)), pltpu.SemaphoreType.DMA((n, n_slots))],
        name="halve_grads_with_sibling",
    )(*grads)


def _swap_with_sibling(parts):
    n = len(parts)

    def body(*refs):
        ins, outs = refs[:n], refs[n:2 * n]
        send, recv = refs[2 * n:]
        sib = (lax.axis_index("x"), lax.axis_index("y"), 1 - lax.axis_index("c"))
        copies = []
        for wi in range(n):
            cp = pltpu.make_async_remote_copy(
                src_ref=ins[wi], dst_ref=outs[wi], send_sem=send.at[wi], recv_sem=recv.at[wi],
                device_id=sib, device_id_type=MESH)
            cp.start()
            copies.append(cp)
        for cp in copies:
            cp.wait()

    return pl.pallas_call(
        body, out_shape=[jax.ShapeDtypeStruct(s.shape, s.dtype) for s in parts],
        in_specs=[HBM_SPEC] * n, out_specs=[HBM_SPEC] * n,
        scratch_shapes=[pltpu.SemaphoreType.DMA((n,)), pltpu.SemaphoreType.DMA((n,))],
        name="swap_grad_halves",
    )(*parts)


def _allreduce_small(vec, loss_row):
    rows = vec.shape[0]

    def body(v_ref, o_ref, buf, send, recv):
        x, y, c = lax.axis_index("x"), lax.axis_index("y"), lax.axis_index("c")
        me = 4 * x + 2 * y + c
        buf[me] = v_ref[...]
        copies = []
        for r in range(1, N_DEV):
            peer = (x ^ ((r >> 2) & 1), y ^ ((r >> 1) & 1), c ^ (r & 1))
            cp = pltpu.make_async_remote_copy(
                src_ref=v_ref, dst_ref=buf.at[me], send_sem=send.at[r - 1], recv_sem=recv.at[r - 1],
                device_id=peer, device_id_type=MESH)
            cp.start()
            copies.append(cp)
        for cp in copies:
            cp.wait()
        total = buf[0]
        for d in range(1, N_DEV):
            total = total + buf[d]
        o_ref[...] = total
        o_ref[loss_row:loss_row + 1, :] = jnp.broadcast_to(
            jnp.sum(total[loss_row:loss_row + 1, :], axis=-1, keepdims=True), (1, HEAD_DIM))

    return pl.pallas_call(
        body, out_shape=jax.ShapeDtypeStruct(vec.shape, F32),
        in_specs=[pl.BlockSpec(memory_space=pltpu.VMEM)], out_specs=pl.BlockSpec(memory_space=pltpu.VMEM),
        scratch_shapes=[pltpu.VMEM((N_DEV, rows, HEAD_DIM), F32),
                        pltpu.SemaphoreType.DMA((N_DEV - 1,)), pltpu.SemaphoreType.DMA((N_DEV - 1,))],
        name="allreduce_gains",
    )(vec)


def _pair_sum(grads, got, *, name):
    slots, rows, width = got.shape
    blk = _pick(rows, (256, 128, 64, 32, 16))
    nbh = rows // blk

    def body(lo_ref, hi_ref, got_ref, o_ref):
        mine = jnp.where(lax.axis_index("c") == 0, lo_ref[...], hi_ref[...])
        o_ref[...] = (mine.astype(F32) + got_ref[...].astype(F32)).astype(BF16)

    spec = pl.BlockSpec((None, blk, width), lambda s, i: (s, i, 0))
    return pl.pallas_call(
        body, out_shape=jax.ShapeDtypeStruct(got.shape, BF16), grid=(slots, nbh),
        in_specs=[spec, pl.BlockSpec((None, blk, width), lambda s, i: (s, nbh + i, 0)), spec],
        out_specs=spec, name=name, compiler_params=_params("parallel", "parallel"),
    )(grads, grads, got)


def _sum_slots(recv, chip_sum, *, name):
    _, rows, width = recv.shape
    blk = _pick(rows, (256, 128, 64, 32, 16, 8))

    def body(r_ref, p_ref, o_ref):
        me = 2 * lax.axis_index("x") + lax.axis_index("y")
        own = jnp.where(me < 2, jnp.where(me == 0, p_ref[0], p_ref[1]), jnp.where(me == 2, p_ref[2], p_ref[3]))
        o_ref[...] = ((own.astype(F32) + r_ref[0].astype(F32)) + r_ref[1].astype(F32)) + r_ref[2].astype(F32)

    return pl.pallas_call(
        body, out_shape=jax.ShapeDtypeStruct((rows, width), F32), grid=(rows // blk,),
        in_specs=[pl.BlockSpec((3, blk, width), lambda i: (0, i, 0)),
                  pl.BlockSpec((N_CHIPS, blk, width), lambda i: (0, i, 0))],
        out_specs=pl.BlockSpec((blk, width), lambda i: (i, 0)),
        name=name, compiler_params=_params("parallel"),
    )(recv, chip_sum)


def _adamw(wgt, grad, m, v, *, name, halves=None):
    rows, width = wgt.shape
    blk = _pick(rows // 2 if halves else rows, (256, 128, 64, 32, 16, 8))
    nbh = rows // 2 // blk

    def body(*refs):
        if halves:
            w_ref, mine_ref, theirs_ref, m_ref, v_ref, g_out, d_out, m_out, v_out = refs
            grad_v = jnp.where(pl.program_id(0) // nbh == lax.axis_index("c"), mine_ref[...], theirs_ref[...])
        else:
            w_ref, g_ref, m_ref, v_ref, g_out, d_out, m_out, v_out = refs
            grad_v = g_ref[...]
        m_new = ADAM_B1 * m_ref[...] + (1.0 - ADAM_B1) * grad_v
        v_new = ADAM_B2 * v_ref[...] + (1.0 - ADAM_B2) * (grad_v * grad_v)
        m_hat = m_new / (1.0 - ADAM_B1 ** ADAM_STEP)
        v_hat = v_new / (1.0 - ADAM_B2 ** ADAM_STEP)
        g_out[...] = grad_v
        d_out[...] = -ADAM_LR * (m_hat / (jnp.sqrt(v_hat) + ADAM_EPS) + ADAM_WD * w_ref[...])
        m_out[...] = m_new
        v_out[...] = v_new

    spec = pl.BlockSpec((blk, width), lambda i: (i, 0))
    half_spec = pl.BlockSpec((blk, width), lambda i: (i % nbh, 0))
    g_specs, g_args = ([half_spec, half_spec], list(halves)) if halves else ([spec], [grad])
    out = jax.ShapeDtypeStruct((rows, width), F32)
    return pl.pallas_call(
        body, out_shape=[out] * 4, grid=(rows // blk,), in_specs=[spec] + g_specs + [spec, spec],
        out_specs=[spec] * 4, name=name, compiler_params=_params("parallel"),
    )(wgt, *g_args, m, v)


_SMALL = (("a_norm", 2048), ("kv_norm", 2048), ("g_ckv", 512), ("g_k_nope", 128), ("g_k_rope", 64),
          ("b_norm", 2048), ("b_g_q_lat", 512), ("b_g_q_nope", 128), ("b_g_q_rope", 64),
          ("mem_norm", 4096), ("g_mem_q", 256), ("g_mem_k", 256))


def _lanes(n):
    return -(-n // HEAD_DIM) * HEAD_DIM


def _pack_rows(pieces, pad_rows_to=8):
    flat = jnp.concatenate(pieces, axis=1)
    rows = flat.shape[1] // HEAD_DIM
    pad = (-rows) % pad_rows_to
    if pad:
        flat = jnp.concatenate([flat, jnp.zeros((1, pad * HEAD_DIM), F32)], axis=1)
    return flat.reshape(rows + pad, HEAD_DIM)


def _pad_lanes(a):
    a = a.reshape(1, -1)
    pad = _lanes(a.shape[1]) - a.shape[1]
    if pad:
        a = jnp.concatenate([a, jnp.zeros((1, pad), F32)], axis=1)
    return a


def kernel(x, mem, positions, a_norm, a_w_in, a_w_out, kv_norm, w_dkv, g_ckv, w_ukv, g_k_nope, g_k_rope, b_norm, b_w_in, b_g_q_lat, b_w_uq, b_g_q_nope, b_g_q_rope, b_w_out, mem_norm, w_mem_kv, g_mem_q, g_mem_k, loss_target, m_a_norm, m_a_w_in, m_a_w_out, m_kv_norm, m_w_dkv, m_g_ckv, m_w_ukv, m_g_k_nope, m_g_k_rope, m_b_norm, m_b_w_in, m_b_g_q_lat, m_b_w_uq, m_b_g_q_nope, m_b_g_q_rope, m_b_w_out, m_mem_norm, m_w_mem_kv, m_g_mem_q, m_g_mem_k, v_a_norm, v_a_w_in, v_a_w_out, v_kv_norm, v_w_dkv, v_g_ckv, v_w_ukv, v_g_k_nope, v_g_k_rope, v_b_norm, v_b_w_in, v_b_g_q_lat, v_b_w_uq, v_b_g_q_nope, v_b_g_q_rope, v_b_w_out, v_mem_norm, v_w_mem_kv, v_g_mem_q, v_g_mem_k):
    chip = 2 * lax.axis_index("x") + lax.axis_index("y")
    rows_dkv = D_MODEL // N_CHIPS
    heads_per_chip = N_MLA_HEADS // N_CHIPS
    qk_w = HEAD_DIM + ROPE_DIM

    big = {"a_in": a_w_in[0], "a_out": a_w_out[0], "dkv": w_dkv, "ukv": w_ukv, "b_in": b_w_in[0],
           "uq": b_w_uq[0], "b_out": b_w_out[0], "mem_kv": w_mem_kv.reshape(2 * rows_dkv, 2 * MEM_W)}
    big_m = {"a_in": m_a_w_in[0], "a_out": m_a_w_out[0], "dkv": m_w_dkv, "ukv": m_w_ukv, "b_in": m_b_w_in[0],
             "uq": m_b_w_uq[0], "b_out": m_b_w_out[0], "mem_kv": m_w_mem_kv.reshape(2 * rows_dkv, 2 * MEM_W)}
    big_v = {"a_in": v_a_w_in[0], "a_out": v_a_w_out[0], "dkv": v_w_dkv, "ukv": v_w_ukv, "b_in": v_b_w_in[0],
             "uq": v_b_w_uq[0], "b_out": v_b_w_out[0], "mem_kv": v_w_mem_kv.reshape(2 * rows_dkv, 2 * MEM_W)}
    names = list(big)
    own_shards = [big[n].astype(BF16) for n in names] + [a_norm]
    gathered = _allgather_chips(own_shards)
    gathered = [lax.dynamic_update_slice(g, s[None], (chip,) + (0,) * s.ndim)
                for g, s in zip(gathered, own_shards)]
    st = dict(zip(names, gathered[:-1]))
    a_in_full = st["a_in"].transpose(1, 0, 2).reshape(D_MODEL, QKV_W + GATE_W)
    uq = st["uq"].reshape(N_CHIPS, Q_LORA, heads_per_chip, qk_w)
    uq = jnp.pad(uq, ((0, 0), (0, 0), (0, 0), (0, CAT_W - qk_w)))
    w = {
        "a_in": a_in_full,
        "a_in_qkv": a_in_full[:, :QKV_W],
        "a_in_gate": a_in_full[:, QKV_W:],
        "a_out": st["a_out"].reshape(D_MODEL, D_MODEL),
        "dkv": jnp.pad(st["dkv"].reshape(D_MODEL, KV_LORA + ROPE_DIM), ((0, 0), (0, HEAD_DIM - ROPE_DIM))),
        "ukv": st["ukv"].transpose(1, 0, 2).reshape(KV_LORA, N_MLA_HEADS * CAT_W),
        "b_in": st["b_in"].transpose(1, 0, 2).reshape(D_MODEL, Q_LORA + GATE_W),
        "uq": uq.transpose(1, 0, 2, 3).reshape(Q_LORA, N_MLA_HEADS * CAT_W),
        "b_out": st["b_out"].reshape(D_MODEL, D_MODEL),
        "mem_kv": st["mem_kv"].reshape(N_CHIPS, 2, rows_dkv, 2 * MEM_W).transpose(1, 0, 2, 3).reshape(
            2, D_MODEL, 2 * MEM_W),
    }
    gains = {
        "a_norm": gathered[-1].reshape(1, D_MODEL), "kv_norm": kv_norm.reshape(1, -1),
        "g_ckv": g_ckv.reshape(1, -1), "g_k_nope": g_k_nope.reshape(1, -1), "g_k_rope": g_k_rope.reshape(1, -1),
        "b_norm": b_norm, "b_g_q_lat": b_g_q_lat, "b_g_q_nope": b_g_q_nope, "b_g_q_rope": b_g_q_rope,
        "mem_norm": mem_norm, "g_mem_q": g_mem_q, "g_mem_k": g_mem_k,
    }

    loss_part, grad_x, gw, gg = _local_step(x[0], mem[0], positions[0], loss_target[0], w, gains)

    stacked = {
        "a_in": gw["a_in"],
        "a_out": gw["a_out"].reshape(N_CHIPS, rows_dkv, D_MODEL),
        "dkv": gw["dkv"][:, :KV_LORA + ROPE_DIM].reshape(N_CHIPS, rows_dkv, KV_LORA + ROPE_DIM),
        "ukv": gw["ukv"],
        "b_in": gw["b_in"],
        "uq": gw["uq"].reshape(N_CHIPS, Q_LORA, heads_per_chip, CAT_W)[..., :qk_w].reshape(
            N_CHIPS, Q_LORA, heads_per_chip * qk_w),
        "b_out": gw["b_out"].reshape(N_CHIPS, rows_dkv, D_MODEL),
        "mem_kv": jnp.stack([gw["mem_kv"][0].reshape(N_CHIPS, rows_dkv, 2 * MEM_W),
                             gw["mem_kv"][1].reshape(N_CHIPS, rows_dkv, 2 * MEM_W)], axis=1).reshape(
            N_CHIPS, 2 * rows_dkv, 2 * MEM_W),
    }
    got = _halve_with_sibling([stacked[n] for n in names])
    chip_sum = [_pair_sum(stacked[n], g, name=f"pair_sum_{n}") for n, g in zip(names, got)]
    received = _scatter_to_chips(chip_sum)
    half_total = [_sum_slots(r, p, name=f"sum_slots_{n}") for n, r, p in zip(names, received, chip_sum)]
    sibling_half = _swap_with_sibling(half_total)
    big_out = {}
    for n, mine, theirs in zip(names, half_total, sibling_half):
        big_out[n] = _adamw(big[n], None, big_m[n], big_v[n], halves=(mine, theirs), name=f"adamw_{n}")

    pieces = [_pad_lanes(gg[n]) if n not in ("g_k_rope", "b_g_q_rope") else gg[n] for n, _ in _SMALL]
    pieces.append(loss_part)
    loss_row = sum(_lanes(size) for _, size in _SMALL) // HEAD_DIM
    summed = _allreduce_small(_pack_rows(pieces), loss_row)
    flat = summed.reshape(1, -1)
    small_g, off = {}, 0
    for n, size in _SMALL:
        small_g[n] = flat[:, off:off + size]
        off += _lanes(size)
    loss = flat[0, off]
    small_g["a_norm"] = lax.dynamic_slice(small_g["a_norm"], (0, chip * rows_dkv), (1, rows_dkv))

    small_w = {"a_norm": a_norm, "kv_norm": kv_norm, "g_ckv": g_ckv, "g_k_nope": g_k_nope, "g_k_rope": g_k_rope,
               "b_norm": b_norm, "b_g_q_lat": b_g_q_lat, "b_g_q_nope": b_g_q_nope, "b_g_q_rope": b_g_q_rope,
               "mem_norm": mem_norm, "g_mem_q": g_mem_q, "g_mem_k": g_mem_k}
    small_m = {"a_norm": m_a_norm, "kv_norm": m_kv_norm, "g_ckv": m_g_ckv, "g_k_nope": m_g_k_nope,
               "g_k_rope": m_g_k_rope, "b_norm": m_b_norm, "b_g_q_lat": m_b_g_q_lat, "b_g_q_nope": m_b_g_q_nope,
               "b_g_q_rope": m_b_g_q_rope, "mem_norm": m_mem_norm, "g_mem_q": m_g_mem_q, "g_mem_k": m_g_mem_k}
    small_v = {"a_norm": v_a_norm, "kv_norm": v_kv_norm, "g_ckv": v_g_ckv, "g_k_nope": v_g_k_nope,
               "g_k_rope": v_g_k_rope, "b_norm": v_b_norm, "b_g_q_lat": v_b_g_q_lat, "b_g_q_nope": v_b_g_q_nope,
               "b_g_q_rope": v_b_g_q_rope, "mem_norm": v_mem_norm, "g_mem_q": v_g_mem_q, "g_mem_k": v_g_mem_k}
    snames = [n for n, _ in _SMALL]
    packs = [_pack_rows([_pad_lanes(src[n]) for n in snames])
             for src in (small_w, small_g, small_m, small_v)]
    small_res = _adamw(packs[0], packs[1], packs[2], packs[3], name="adamw_gains")
    small_out = {n: [] for n in snames}
    for res in small_res:
        flat_r = res.reshape(1, -1)
        off = 0
        for n in snames:
            size = small_w[n].size
            small_out[n].append(flat_r[:, off:off + size].reshape(small_w[n].shape))
            off += _lanes(size)

    big_names = {"a_w_in": ("a_in", a_w_in), "a_w_out": ("a_out", a_w_out), "w_dkv": ("dkv", w_dkv),
                 "w_ukv": ("ukv", w_ukv), "b_w_in": ("b_in", b_w_in), "b_w_uq": ("uq", b_w_uq),
                 "b_w_out": ("b_out", b_w_out), "w_mem_kv": ("mem_kv", w_mem_kv)}
    order = ["a_norm", "a_w_in", "a_w_out", "kv_norm", "w_dkv", "g_ckv", "w_ukv", "g_k_nope", "g_k_rope",
             "b_norm", "b_w_in", "b_g_q_lat", "b_w_uq", "b_g_q_nope", "b_g_q_rope", "b_w_out", "mem_norm",
             "w_mem_kv", "g_mem_q", "g_mem_k"]
    groups = [[], [], [], []]
    for n in order:
        if n in big_names:
            key, ref_arr = big_names[n]
            for t in range(4):
                groups[t].append(big_out[key][t].reshape(ref_arr.shape))
        else:
            for t in range(4):
                groups[t].append(small_out[n][t])
    return (loss, grad_x[None], *groups[0], *groups[1], *groups[2], *groups[3])
```
